```python
import jax, jax.numpy as jnp
from jax import lax
import numpy as np

D_MODEL = 1024
BATCH = 8
SEQ = 8192
DEPTH = 4

N_A_LAYERS = DEPTH // 2
N_B_LAYERS = DEPTH - N_A_LAYERS
D_FF = 2816
CONV_WIDTH = 3
HEAD_DIM = 64
HEADS_PER_GROUP = 8
DILATED_GROUPS = ((128, 1), (512, 4), (2048, 16))
N_GROUPS = len(DILATED_GROUPS)
N_Q_HEADS = N_GROUPS * HEADS_PER_GROUP
Q_WIDTH = N_Q_HEADS * HEAD_DIM
OUT_WIDTH = HEADS_PER_GROUP * HEAD_DIM
ROPE_DIM = HEAD_DIM // 4
ROPE_THETA = 500000.0
NORM_EPS = 1e-5
FFN_RES_WEIGHT = 0.5
N_MOD = 9

kernel_name = "hybrid_shortconv_dilated_yoco_trunk"


def rms_norm(x, g):
    xf = x.astype(jnp.float32)
    y = xf * lax.rsqrt(jnp.mean(xf * xf, axis=-1, keepdims=True) + NORM_EPS)
    return (y * g.astype(jnp.float32)).astype(x.dtype)


def modulate(h, shift, scale):
    return h * (1 + scale[:, None, :]) + shift[:, None, :]


def swiglu(h, w_in, w_out):
    a, b = jnp.split(h @ w_in, 2, axis=-1)
    return (jax.nn.silu(a) * b) @ w_out


def short_conv_mixer(h, w_in, conv_w, w_out):
    b_gate, c_gate, u = jnp.split(h @ w_in, 3, axis=-1)
    v = c_gate * u
    conv = lax.conv_general_dilated(
        v, conv_w[:, None, :], window_strides=(1,), padding=[(CONV_WIDTH - 1, 0)],
        dimension_numbers=('NWC', 'WIO', 'NWC'), feature_group_count=D_MODEL)
    return (b_gate * conv) @ w_out


def rope_tables(positions):
    inv = ROPE_THETA ** (-jnp.arange(0, ROPE_DIM, 2, dtype=jnp.float32) / ROPE_DIM)
    ang = positions.astype(jnp.float32)[..., None] * inv
    return jnp.cos(ang)[:, :, None, :], jnp.sin(ang)[:, :, None, :]


def apply_partial_rope(t, cos, sin):
    tf = t.astype(jnp.float32)
    r1 = tf[..., :ROPE_DIM // 2]
    r2 = tf[..., ROPE_DIM // 2:ROPE_DIM]
    out = jnp.concatenate([r1 * cos - r2 * sin, r2 * cos + r1 * sin, tf[..., ROPE_DIM:]], axis=-1)
    return out.astype(t.dtype)


def dilated_window_attention(q, k, v, window, dilation):
    bsz, seq, nh, hd = q.shape
    n = window // dilation
    span = n * dilation
    seq_p = -(-seq // span) * span
    pad = seq_p - seq
    m_len = seq_p // dilation
    nb = m_len // n

    def to_blocks(t):
        t = jnp.pad(t, ((0, 0), (0, pad), (0, 0), (0, 0)))
        t = t.reshape(bsz, m_len, dilation, nh, hd).transpose(0, 2, 3, 1, 4)
        return t.reshape(bsz, dilation, nh, nb, n, hd)

    def with_prev(t):
        prev = jnp.pad(t[:, :, :, :-1], ((0, 0), (0, 0), (0, 0), (1, 0), (0, 0), (0, 0)))
        return jnp.concatenate([prev, t], axis=-2)

    qb = to_blocks(q)
    kk = with_prev(to_blocks(k))
    vv = with_prev(to_blocks(v))
    s = jnp.einsum('brhiqe,brhike->brhiqk', qb, kk,
                   preferred_element_type=jnp.float32) * (hd ** -0.5)
    blk = jnp.arange(nb)[:, None, None]
    qi = jnp.arange(n)[None, :, None]
    kj = jnp.arange(2 * n)[None, None, :]
    dist = n + qi - kj
    valid = (dist >= 0) & (dist <= n) & ((blk - 1) * n + kj >= 0)
    s = jnp.where(valid, s, -jnp.inf)
    m = jnp.max(s, axis=-1, keepdims=True)
    p = jnp.exp(s - m)
    den = jnp.sum(p, axis=-1, keepdims=True)
    o = jnp.einsum('brhiqk,brhike->brhiqe', (p / den).astype(v.dtype), vv)
    lse = (m + jnp.log(den))[..., 0]
    o = o.reshape(bsz, dilation, nh, m_len, hd).transpose(0, 3, 1, 2, 4).reshape(bsz, seq_p, nh, hd)
    lse = lse.reshape(bsz, dilation, nh, m_len).transpose(0, 3, 1, 2).reshape(bsz, seq_p, nh)
    return o[:, :seq], lse[:, :seq]


def dilated_attention_mixer(h, w_q, w_o, k_sh, v_sh, cos, sin):
    bsz, seq, _ = h.shape
    q = apply_partial_rope((h @ w_q).reshape(bsz, seq, N_Q_HEADS, HEAD_DIM), cos, sin)
    outs, lses = [], []
    for g, (win, dil) in enumerate(DILATED_GROUPS):
        sl = slice(g * HEADS_PER_GROUP, (g + 1) * HEADS_PER_GROUP)
        o, l = dilated_window_attention(q[:, :, sl], k_sh[:, :, sl], v_sh[:, :, sl], win, dil)
        outs.append(o)
        lses.append(l)
    o = jnp.stack(outs, axis=0).astype(jnp.float32)
    w = jax.nn.softmax(jnp.stack(lses, axis=0), axis=0)
    mixed = jnp.sum(w[..., None] * o, axis=0).astype(h.dtype)
    return mixed.reshape(bsz, seq, OUT_WIDTH) @ w_o


def shared_kv(x, g, shift, scale, w_kv, cos, sin):
    bsz, seq, _ = x.shape
    h = modulate(rms_norm(x, g), shift, scale)
    k, v = jnp.split(h @ w_kv, 2, axis=-1)
    k = apply_partial_rope(k.reshape(bsz, seq, N_Q_HEADS, HEAD_DIM), cos, sin)
    v = v.reshape(bsz, seq, N_Q_HEADS, HEAD_DIM)
    return k, v


def _fwd_setup_inputs(seed: int = 0) -> dict:
    key = jax.random.key(seed)
    ks = jax.random.split(key, 24)
    f32 = jnp.float32
    D, F = D_MODEL, D_FF

    def nrm(k, shape, fan_in, mult=1.0):
        return jax.random.normal(k, shape, f32) * (mult * fan_in ** -0.5)

    x = jax.random.normal(ks[0], (BATCH, SEQ, D), f32)
    c = jax.random.normal(ks[1], (BATCH, D), f32)
    offset = jax.random.randint(ks[2], (BATCH, 1), 0, 1024, dtype=jnp.int32)
    positions = offset + jnp.arange(SEQ, dtype=jnp.int32)[None, :]
    return {
        "x": x,
        "c": c,
        "positions": positions,
        "norm_g": 1.0 + 0.02 * jax.random.normal(ks[3], (DEPTH, 3, D), f32),
        "ada_w": nrm(ks[4], (DEPTH, D, N_MOD * D), D, 0.1),
        "ada_b": 0.01 * jax.random.normal(ks[5], (DEPTH, N_MOD * D), f32),
        "ffn1_w_in": nrm(ks[6], (DEPTH, D, 2 * F), D),
        "ffn1_w_out": nrm(ks[7], (DEPTH, F, D), F),
        "ffn2_w_in": nrm(ks[8], (DEPTH, D, 2 * F), D),
        "ffn2_w_out": nrm(ks[9], (DEPTH, F, D), F),
        "conv_w_in": nrm(ks[10], (N_A_LAYERS, D, 3 * D), D),
        "conv_w": nrm(ks[11], (N_A_LAYERS, CONV_WIDTH, D), CONV_WIDTH),
        "conv_w_out": nrm(ks[12], (N_A_LAYERS, D, D), D),
        "kv_norm_g": 1.0 + 0.02 * jax.random.normal(ks[13], (D,), f32),
        "kv_ada_w": nrm(ks[14], (D, 2 * D), D, 0.1),
        "kv_ada_b": 0.01 * jax.random.normal(ks[15], (2 * D,), f32),
        "w_kv": nrm(ks[16], (D, 2 * Q_WIDTH), D),
        "attn_w_q": nrm(ks[17], (N_B_LAYERS, D, Q_WIDTH), D),
        "attn_w_o": nrm(ks[18], (N_B_LAYERS, OUT_WIDTH, D), OUT_WIDTH),
        "final_norm_g": 1.0 + 0.02 * jax.random.normal(ks[19], (D,), f32),
    }


def _fwd_reference(x, c, positions, norm_g, ada_w, ada_b, ffn1_w_in, ffn1_w_out, ffn2_w_in, ffn2_w_out,
              conv_w_in, conv_w, conv_w_out, kv_norm_g, kv_ada_w, kv_ada_b, w_kv,
              attn_w_q, attn_w_o, final_norm_g):
    cond = jax.nn.silu(c)
    cos, sin = rope_tables(positions)
    k_sh = v_sh = None
    for layer in range(DEPTH):
        if layer == N_A_LAYERS:
            kv_shift, kv_scale = jnp.split(cond @ kv_ada_w + kv_ada_b, 2, axis=-1)
            k_sh, v_sh = shared_kv(x, kv_norm_g, kv_shift, kv_scale, w_kv, cos, sin)
        mods = cond @ ada_w[layer] + ada_b[layer]
        sh1, sc1, g1, sh2, sc2, g2, sh3, sc3, g3 = jnp.split(mods, N_MOD, axis=-1)
        h = modulate(rms_norm(x, norm_g[layer, 0]), sh1, sc1)
        x = x + FFN_RES_WEIGHT * (1 + g1)[:, None, :] * swiglu(h, ffn1_w_in[layer], ffn1_w_out[layer])
        h = modulate(rms_norm(x, norm_g[layer, 1]), sh2, sc2)
        if layer < N_A_LAYERS:
            mix = short_conv_mixer(h, conv_w_in[layer], conv_w[layer], conv_w_out[layer])
        else:
            j = layer - N_A_LAYERS
            mix = dilated_attention_mixer(h, attn_w_q[j], attn_w_o[j], k_sh, v_sh, cos, sin)
        x = x + (1 + g2)[:, None, :] * mix
        h = modulate(rms_norm(x, norm_g[layer, 2]), sh3, sc3)
        x = x + FFN_RES_WEIGHT * (1 + g3)[:, None, :] * swiglu(h, ffn2_w_in[layer], ffn2_w_out[layer])
    return rms_norm(x, final_norm_g)


import jax as _jax
import jax.numpy as _jnp

TWIN_FORMAT = 'train_step'
FWD_PARAMS = ['x', 'c', 'positions', 'norm_g', 'ada_w', 'ada_b', 'ffn1_w_in', 'ffn1_w_out', 'ffn2_w_in', 'ffn2_w_out', 'conv_w_in', 'conv_w', 'conv_w_out', 'kv_norm_g', 'kv_ada_w', 'kv_ada_b', 'w_kv', 'attn_w_q', 'attn_w_o', 'final_norm_g']
TWIN_WEIGHTS = ['norm_g', 'ada_w', 'ada_b', 'ffn1_w_in', 'ffn1_w_out', 'ffn2_w_in', 'ffn2_w_out', 'conv_w_in', 'conv_w', 'conv_w_out', 'kv_norm_g', 'kv_ada_w', 'kv_ada_b', 'w_kv', 'attn_w_q', 'attn_w_o', 'final_norm_g']
TWIN_DIFF_INPUT = 'x'
TWIN_INPUTS = ['x', 'c', 'positions', 'norm_g', 'ada_w', 'ada_b', 'ffn1_w_in', 'ffn1_w_out', 'ffn2_w_in', 'ffn2_w_out', 'conv_w_in', 'conv_w', 'conv_w_out', 'kv_norm_g', 'kv_ada_w', 'kv_ada_b', 'w_kv', 'attn_w_q', 'attn_w_o', 'final_norm_g', 'loss_target', 'm_norm_g', 'm_ada_w', 'm_ada_b', 'm_ffn1_w_in', 'm_ffn1_w_out', 'm_ffn2_w_in', 'm_ffn2_w_out', 'm_conv_w_in', 'm_conv_w', 'm_conv_w_out', 'm_kv_norm_g', 'm_kv_ada_w', 'm_kv_ada_b', 'm_w_kv', 'm_attn_w_q', 'm_attn_w_o', 'm_final_norm_g', 'v_norm_g', 'v_ada_w', 'v_ada_b', 'v_ffn1_w_in', 'v_ffn1_w_out', 'v_ffn2_w_in', 'v_ffn2_w_out', 'v_conv_w_in', 'v_conv_w', 'v_conv_w_out', 'v_kv_norm_g', 'v_kv_ada_w', 'v_kv_ada_b', 'v_w_kv', 'v_attn_w_q', 'v_attn_w_o', 'v_final_norm_g']
TWIN_OUTPUTS = ['loss', 'grad_x', 'grad_norm_g', 'grad_ada_w', 'grad_ada_b', 'grad_ffn1_w_in', 'grad_ffn1_w_out', 'grad_ffn2_w_in', 'grad_ffn2_w_out', 'grad_conv_w_in', 'grad_conv_w', 'grad_conv_w_out', 'grad_kv_norm_g', 'grad_kv_ada_w', 'grad_kv_ada_b', 'grad_w_kv', 'grad_attn_w_q', 'grad_attn_w_o', 'grad_final_norm_g', 'delta_norm_g', 'delta_ada_w', 'delta_ada_b', 'delta_ffn1_w_in', 'delta_ffn1_w_out', 'delta_ffn2_w_in', 'delta_ffn2_w_out', 'delta_conv_w_in', 'delta_conv_w', 'delta_conv_w_out', 'delta_kv_norm_g', 'delta_kv_ada_w', 'delta_kv_ada_b', 'delta_w_kv', 'delta_attn_w_q', 'delta_attn_w_o', 'delta_final_norm_g', 'new_m_norm_g', 'new_m_ada_w', 'new_m_ada_b', 'new_m_ffn1_w_in', 'new_m_ffn1_w_out', 'new_m_ffn2_w_in', 'new_m_ffn2_w_out', 'new_m_conv_w_in', 'new_m_conv_w', 'new_m_conv_w_out', 'new_m_kv_norm_g', 'new_m_kv_ada_w', 'new_m_kv_ada_b', 'new_m_w_kv', 'new_m_attn_w_q', 'new_m_attn_w_o', 'new_m_final_norm_g', 'new_v_norm_g', 'new_v_ada_w', 'new_v_ada_b', 'new_v_ffn1_w_in', 'new_v_ffn1_w_out', 'new_v_ffn2_w_in', 'new_v_ffn2_w_out', 'new_v_conv_w_in', 'new_v_conv_w', 'new_v_conv_w_out', 'new_v_kv_norm_g', 'new_v_kv_ada_w', 'new_v_kv_ada_b', 'new_v_w_kv', 'new_v_attn_w_q', 'new_v_attn_w_o', 'new_v_final_norm_g']
TWIN_LEAF_KINDS = {'loss': 'loss', 'grad_x': 'grad_x', 'grad_norm_g': 'grad_w', 'grad_ada_w': 'grad_w', 'grad_ada_b': 'grad_w', 'grad_ffn1_w_in': 'grad_w', 'grad_ffn1_w_out': 'grad_w', 'grad_ffn2_w_in': 'grad_w', 'grad_ffn2_w_out': 'grad_w', 'grad_conv_w_in': 'grad_w', 'grad_conv_w': 'grad_w', 'grad_conv_w_out': 'grad_w', 'grad_kv_norm_g': 'grad_w', 'grad_kv_ada_w': 'grad_w', 'grad_kv_ada_b': 'grad_w', 'grad_w_kv': 'grad_w', 'grad_attn_w_q': 'grad_w', 'grad_attn_w_o': 'grad_w', 'grad_final_norm_g': 'grad_w', 'delta_norm_g': 'delta_w', 'delta_ada_w': 'delta_w', 'delta_ada_b': 'delta_w', 'delta_ffn1_w_in': 'delta_w', 'delta_ffn1_w_out': 'delta_w', 'delta_ffn2_w_in': 'delta_w', 'delta_ffn2_w_out': 'delta_w', 'delta_conv_w_in': 'delta_w', 'delta_conv_w': 'delta_w', 'delta_conv_w_out': 'delta_w', 'delta_kv_norm_g': 'delta_w', 'delta_kv_ada_w': 'delta_w', 'delta_kv_ada_b': 'delta_w', 'delta_w_kv': 'delta_w', 'delta_attn_w_q': 'delta_w', 'delta_attn_w_o': 'delta_w', 'delta_final_norm_g': 'delta_w', 'new_m_norm_g': 'new_m', 'new_m_ada_w': 'new_m', 'new_m_ada_b': 'new_m', 'new_m_ffn1_w_in': 'new_m', 'new_m_ffn1_w_out': 'new_m', 'new_m_ffn2_w_in': 'new_m', 'new_m_ffn2_w_out': 'new_m', 'new_m_conv_w_in': 'new_m', 'new_m_conv_w': 'new_m', 'new_m_conv_w_out': 'new_m', 'new_m_kv_norm_g': 'new_m', 'new_m_kv_ada_w': 'new_m', 'new_m_kv_ada_b': 'new_m', 'new_m_w_kv': 'new_m', 'new_m_attn_w_q': 'new_m', 'new_m_attn_w_o': 'new_m', 'new_m_final_norm_g': 'new_m', 'new_v_norm_g': 'new_v', 'new_v_ada_w': 'new_v', 'new_v_ada_b': 'new_v', 'new_v_ffn1_w_in': 'new_v', 'new_v_ffn1_w_out': 'new_v', 'new_v_ffn2_w_in': 'new_v', 'new_v_ffn2_w_out': 'new_v', 'new_v_conv_w_in': 'new_v', 'new_v_conv_w': 'new_v', 'new_v_conv_w_out': 'new_v', 'new_v_kv_norm_g': 'new_v', 'new_v_kv_ada_w': 'new_v', 'new_v_kv_ada_b': 'new_v', 'new_v_w_kv': 'new_v', 'new_v_attn_w_q': 'new_v', 'new_v_attn_w_o': 'new_v', 'new_v_final_norm_g': 'new_v'}


def _forward(args):
    return _fwd_reference(*[args[k] for k in FWD_PARAMS])


def _output_shape():
    def fwd():
        inp = _fwd_setup_inputs(0)
        return _fwd_reference(*[inp[k] for k in FWD_PARAMS])
    out = _jax.eval_shape(fwd)
    return out.shape, out.dtype

N_MICROBATCH = 1
ADAM_LR = 0.001
ADAM_B1 = 0.9
ADAM_B2 = 0.999
ADAM_EPS = 1e-08
ADAM_WD = 0.01
ADAM_STEP = 10
PER_EXAMPLE_BATCH_AXIS = {'x': 0, 'c': 0, 'positions': 0, 'loss_target': 0}
SHARED_INPUTS = []
_WEIGHT_DTYPES = {'norm_g': _jnp.float32, 'ada_w': _jnp.float32, 'ada_b': _jnp.float32, 'ffn1_w_in': _jnp.float32, 'ffn1_w_out': _jnp.float32, 'ffn2_w_in': _jnp.float32, 'ffn2_w_out': _jnp.float32, 'conv_w_in': _jnp.float32, 'conv_w': _jnp.float32, 'conv_w_out': _jnp.float32, 'kv_norm_g': _jnp.float32, 'kv_ada_w': _jnp.float32, 'kv_ada_b': _jnp.float32, 'w_kv': _jnp.float32, 'attn_w_q': _jnp.float32, 'attn_w_o': _jnp.float32, 'final_norm_g': _jnp.float32}
MOMENT_SCALE = {'norm_g': 1.680958e-01, 'ada_w': 9.101076e-02, 'ada_b': 1.705617e-01, 'ffn1_w_in': 5.236302e-02, 'ffn1_w_out': 8.532466e-02, 'ffn2_w_in': 3.110642e-02, 'ffn2_w_out': 5.071612e-02, 'conv_w_in': 2.061549e-01, 'conv_w': 2.114478e-01, 'conv_w_out': 2.055920e-01, 'kv_norm_g': 4.947325e-02, 'kv_ada_w': 6.668051e-02, 'kv_ada_b': 1.089595e-01, 'w_kv': 2.844164e-02, 'attn_w_q': 1.855838e-02, 'attn_w_o': 2.636580e-02, 'final_norm_g': 6.401589e+01}


def _to_microbatches(a, axis):
    t = _jnp.moveaxis(a, axis, 0)
    t = t.reshape((N_MICROBATCH, t.shape[0] // N_MICROBATCH) + t.shape[1:])
    return _jnp.moveaxis(t, 1, axis + 1)


def setup_inputs(seed: int = 0) -> dict:
    inp = _fwd_setup_inputs(seed)
    key = _jax.random.fold_in(_jax.random.key(seed), 7919)
    shape, _ = _output_shape()
    out = dict(inp)
    out["loss_target"] = _jax.random.normal(_jax.random.fold_in(key, 0), shape, _jnp.float32)
    for i, name in enumerate(TWIN_WEIGHTS):
        w = inp[name].astype(_jnp.float32)
        if MOMENT_SCALE is None:
            s = _jnp.sqrt(_jnp.mean(_jnp.square(w)) + 1e-30)
        else:
            s = MOMENT_SCALE[name]
        km, kv = _jax.random.split(_jax.random.fold_in(key, i + 1))
        out[name] = w
        out["m_" + name] = s * _jax.random.normal(km, w.shape, _jnp.float32)
        out["v_" + name] = (s * s) * _jax.random.uniform(kv, w.shape, _jnp.float32, 0.5, 1.5)
    if N_MICROBATCH > 1:
        for name, axis in PER_EXAMPLE_BATCH_AXIS.items():
            out[name] = _to_microbatches(out[name], axis)
    return {'x': out['x'], 'c': out['c'], 'positions': out['positions'], 'norm_g': out['norm_g'], 'ada_w': out['ada_w'], 'ada_b': out['ada_b'], 'ffn1_w_in': out['ffn1_w_in'], 'ffn1_w_out': out['ffn1_w_out'], 'ffn2_w_in': out['ffn2_w_in'], 'ffn2_w_out': out['ffn2_w_out'], 'conv_w_in': out['conv_w_in'], 'conv_w': out['conv_w'], 'conv_w_out': out['conv_w_out'], 'kv_norm_g': out['kv_norm_g'], 'kv_ada_w': out['kv_ada_w'], 'kv_ada_b': out['kv_ada_b'], 'w_kv': out['w_kv'], 'attn_w_q': out['attn_w_q'], 'attn_w_o': out['attn_w_o'], 'final_norm_g': out['final_norm_g'], 'loss_target': out['loss_target'], 'm_norm_g': out['m_norm_g'], 'm_ada_w': out['m_ada_w'], 'm_ada_b': out['m_ada_b'], 'm_ffn1_w_in': out['m_ffn1_w_in'], 'm_ffn1_w_out': out['m_ffn1_w_out'], 'm_ffn2_w_in': out['m_ffn2_w_in'], 'm_ffn2_w_out': out['m_ffn2_w_out'], 'm_conv_w_in': out['m_conv_w_in'], 'm_conv_w': out['m_conv_w'], 'm_conv_w_out': out['m_conv_w_out'], 'm_kv_norm_g': out['m_kv_norm_g'], 'm_kv_ada_w': out['m_kv_ada_w'], 'm_kv_ada_b': out['m_kv_ada_b'], 'm_w_kv': out['m_w_kv'], 'm_attn_w_q': out['m_attn_w_q'], 'm_attn_w_o': out['m_attn_w_o'], 'm_final_norm_g': out['m_final_norm_g'], 'v_norm_g': out['v_norm_g'], 'v_ada_w': out['v_ada_w'], 'v_ada_b': out['v_ada_b'], 'v_ffn1_w_in': out['v_ffn1_w_in'], 'v_ffn1_w_out': out['v_ffn1_w_out'], 'v_ffn2_w_in': out['v_ffn2_w_in'], 'v_ffn2_w_out': out['v_ffn2_w_out'], 'v_conv_w_in': out['v_conv_w_in'], 'v_conv_w': out['v_conv_w'], 'v_conv_w_out': out['v_conv_w_out'], 'v_kv_norm_g': out['v_kv_norm_g'], 'v_kv_ada_w': out['v_kv_ada_w'], 'v_kv_ada_b': out['v_kv_ada_b'], 'v_w_kv': out['v_w_kv'], 'v_attn_w_q': out['v_attn_w_q'], 'v_attn_w_o': out['v_attn_w_o'], 'v_final_norm_g': out['v_final_norm_g']}


def _loss(weights, diff, rest, loss_target):
    with _jax.named_scope("forward"):
        args = {**rest, TWIN_DIFF_INPUT: diff, **{k: w.astype(_WEIGHT_DTYPES[k]) for k, w in weights.items()}}
        y = _forward(args)
    with _jax.named_scope("loss_head"):
        err = _jnp.square(y.astype(_jnp.float32) - loss_target)
        return 0.5 * _jnp.sum(_jnp.mean(err, axis=-1)) if err.ndim else 0.5 * err


def _adamw(w, g, m, v):
    m = ADAM_B1 * m + (1.0 - ADAM_B1) * g
    v = ADAM_B2 * v + (1.0 - ADAM_B2) * _jnp.square(g)
    m_hat = m / (1.0 - ADAM_B1 ** ADAM_STEP)
    v_hat = v / (1.0 - ADAM_B2 ** ADAM_STEP)
    delta = -ADAM_LR * (m_hat / (_jnp.sqrt(v_hat) + ADAM_EPS) + ADAM_WD * w)
    return delta, m, v


def reference(x, c, positions, norm_g, ada_w, ada_b, ffn1_w_in, ffn1_w_out, ffn2_w_in, ffn2_w_out, conv_w_in, conv_w, conv_w_out, kv_norm_g, kv_ada_w, kv_ada_b, w_kv, attn_w_q, attn_w_o, final_norm_g, loss_target, m_norm_g, m_ada_w, m_ada_b, m_ffn1_w_in, m_ffn1_w_out, m_ffn2_w_in, m_ffn2_w_out, m_conv_w_in, m_conv_w, m_conv_w_out, m_kv_norm_g, m_kv_ada_w, m_kv_ada_b, m_w_kv, m_attn_w_q, m_attn_w_o, m_final_norm_g, v_norm_g, v_ada_w, v_ada_b, v_ffn1_w_in, v_ffn1_w_out, v_ffn2_w_in, v_ffn2_w_out, v_conv_w_in, v_conv_w, v_conv_w_out, v_kv_norm_g, v_kv_ada_w, v_kv_ada_b, v_w_kv, v_attn_w_q, v_attn_w_o, v_final_norm_g):
    given = dict(x=x, c=c, positions=positions, norm_g=norm_g, ada_w=ada_w, ada_b=ada_b, ffn1_w_in=ffn1_w_in, ffn1_w_out=ffn1_w_out, ffn2_w_in=ffn2_w_in, ffn2_w_out=ffn2_w_out, conv_w_in=conv_w_in, conv_w=conv_w, conv_w_out=conv_w_out, kv_norm_g=kv_norm_g, kv_ada_w=kv_ada_w, kv_ada_b=kv_ada_b, w_kv=w_kv, attn_w_q=attn_w_q, attn_w_o=attn_w_o, final_norm_g=final_norm_g, loss_target=loss_target, m_norm_g=m_norm_g, m_ada_w=m_ada_w, m_ada_b=m_ada_b, m_ffn1_w_in=m_ffn1_w_in, m_ffn1_w_out=m_ffn1_w_out, m_ffn2_w_in=m_ffn2_w_in, m_ffn2_w_out=m_ffn2_w_out, m_conv_w_in=m_conv_w_in, m_conv_w=m_conv_w, m_conv_w_out=m_conv_w_out, m_kv_norm_g=m_kv_norm_g, m_kv_ada_w=m_kv_ada_w, m_kv_ada_b=m_kv_ada_b, m_w_kv=m_w_kv, m_attn_w_q=m_attn_w_q, m_attn_w_o=m_attn_w_o, m_final_norm_g=m_final_norm_g, v_norm_g=v_norm_g, v_ada_w=v_ada_w, v_ada_b=v_ada_b, v_ffn1_w_in=v_ffn1_w_in, v_ffn1_w_out=v_ffn1_w_out, v_ffn2_w_in=v_ffn2_w_in, v_ffn2_w_out=v_ffn2_w_out, v_conv_w_in=v_conv_w_in, v_conv_w=v_conv_w, v_conv_w_out=v_conv_w_out, v_kv_norm_g=v_kv_norm_g, v_kv_ada_w=v_kv_ada_w, v_kv_ada_b=v_kv_ada_b, v_w_kv=v_w_kv, v_attn_w_q=v_attn_w_q, v_attn_w_o=v_attn_w_o, v_final_norm_g=v_final_norm_g)
    weights = {n: given[n] for n in TWIN_WEIGHTS}
    shared = {n: given[n] for n in SHARED_INPUTS}
    per_example = {n: given[n] for n in ['x', 'c', 'positions']}
    grad_fn = _jax.value_and_grad(_loss, argnums=(0, 1))

    def one_microbatch(ex, loss_target):
        ex = dict(ex)
        diff = ex.pop(TWIN_DIFF_INPUT)
        return grad_fn(weights, diff, {**shared, **ex}, loss_target)

    if N_MICROBATCH == 1:
        loss, (grad_w, grad_x) = one_microbatch(per_example, given["loss_target"])
    else:
        def body(carry, xs):
            loss_sum, grad_sum = carry
            l_k, (gw_k, gx_k) = one_microbatch(xs[0], xs[1])
            with _jax.named_scope("update"):
                return (loss_sum + l_k, _jax.tree.map(_jnp.add, grad_sum, gw_k)), gx_k

        init = (_jnp.zeros((), _jnp.float32), _jax.tree.map(_jnp.zeros_like, weights))
        (loss, grad_w), grad_x = _jax.lax.scan(body, init, (per_example, given["loss_target"]))
    with _jax.named_scope("update"):
        delta_w, new_m, new_v = {}, {}, {}
        for n in TWIN_WEIGHTS:
            delta_w[n], new_m[n], new_v[n] = _adamw(weights[n], grad_w[n], given["m_" + n], given["v_" + n])
    return (loss, grad_x, *[grad_w[n] for n in TWIN_WEIGHTS], *[delta_w[n] for n in TWIN_WEIGHTS],
            *[new_m[n] for n in TWIN_WEIGHTS], *[new_v[n] for n in TWIN_WEIGHTS])
```

```python
import functools

import jax
import jax.numpy as jnp
from jax import lax
from jax.experimental import pallas as pl
from jax.experimental.pallas import tpu as pltpu

F32, BF16 = jnp.float32, jnp.bfloat16

N_DEV = 8
MESH_AXES = ("x", "y", "c")
DEPTH = 4
N_A_LAYERS = 2
HEAD_DIM = 64
HEADS_PER_GROUP = 8
GROUP_WIDTH = HEAD_DIM * HEADS_PER_GROUP
DILATED_GROUPS = ((128, 1), (512, 4), (2048, 16))
ROPE_DIM = HEAD_DIM // 4
ROPE_THETA = 500000.0
NORM_EPS = 1e-5
FFN_RES_WEIGHT = 0.5
N_MOD = 9
ADAM_LR, ADAM_B1, ADAM_B2, ADAM_EPS, ADAM_WD, ADAM_STEP = 0.001, 0.9, 0.999, 1e-08, 0.01, 10

LANES = 128
TOKEN_TILE = 512
VMEM_LIMIT = 56 * 1024 * 1024
MESH = pl.DeviceIdType.MESH


def _cp(*sem):
    return pltpu.CompilerParams(dimension_semantics=sem, vmem_limit_bytes=VMEM_LIMIT)


def _pick(n, cap, mult=LANES):
    if n <= cap:
        return n
    best = None
    for t in range(mult, cap + 1, mult):
        if n % t == 0:
            best = t
    assert best is not None, (n, cap)
    return best


def _tok(tm, w):
    return pl.BlockSpec((tm, w), lambda i: (i, 0))


def _res(shape):
    nd = len(shape)
    return pl.BlockSpec(shape, lambda *_: (0,) * nd, pipeline_mode=pl.Buffered(1))


def _sds(shape, dt):
    return jax.ShapeDtypeStruct(shape, dt)


def _sigmoid(a):
    return 1.0 / (1.0 + jnp.exp(-a))


def _modnorm(x, g, sh, sc):
    r = lax.rsqrt(jnp.mean(x * x, axis=-1, keepdims=True) + NORM_EPS)
    return (x * r * g) * (1.0 + sc) + sh


def _dot(a, b):
    return jnp.dot(a, b, preferred_element_type=F32)


def _dot_nt(a, b):
    return lax.dot_general(a, b, (((1,), (1,)), ((), ())), preferred_element_type=F32)


def _dot_tn(a, b):
    return lax.dot_general(a, b, (((0,), (0,)), ((), ())), preferred_element_type=F32)


def _rows8(rows, d):
    pad = 8 - len(rows)
    return jnp.concatenate(list(rows) + [jnp.zeros((pad, d), F32)], axis=0)


def _acc_rows(ref, tile, first):
    @pl.when(first)
    def _():
        ref[...] = tile

    @pl.when(jnp.logical_not(first))
    def _():
        ref[...] += tile


def ffn_up(x, vec, w_in):
    T, D = x.shape
    F = w_in.shape[1] // 2
    tm, tf = min(TOKEN_TILE, T), _pick(F, 1408)
    nj = F // tf

    def body(x_ref, vec_ref, wa_ref, wb_ref, h_ref, a_ref, b_ref, u_ref, hs):
        @pl.when(pl.program_id(1) == 0)
        def _():
            hb = _modnorm(x_ref[...], vec_ref[0:1], vec_ref[1:2], vec_ref[2:3]).astype(BF16)
            hs[...] = hb
            h_ref[...] = hb

        h = hs[...]
        a = _dot(h, wa_ref[...])
        b = _dot(h, wb_ref[...])
        a_ref[...] = a.astype(BF16)
        b_ref[...] = b.astype(BF16)
        u_ref[...] = (a * _sigmoid(a) * b).astype(BF16)

    row = pl.BlockSpec((tm, D), lambda i, j: (i, 0))
    til = pl.BlockSpec((tm, tf), lambda i, j: (i, j))
    return pl.pallas_call(
        body, grid=(T // tm, nj),
        in_specs=[row, pl.BlockSpec((8, D), lambda i, j: (0, 0)),
                  pl.BlockSpec((D, tf), lambda i, j: (0, j)), pl.BlockSpec((D, tf), lambda i, j: (0, nj + j))],
        out_specs=[row, til, til, til],
        out_shape=[_sds((T, D), BF16), _sds((T, F), BF16), _sds((T, F), BF16), _sds((T, F), BF16)],
        scratch_shapes=[pltpu.VMEM((tm, D), BF16)],
        compiler_params=_cp("arbitrary", "arbitrary"), name="ffn_up")(x, vec, w_in, w_in)


def proj_out(u, x, vec, w_out, res_weight, name):
    T, D = x.shape
    K = u.shape[1]
    tm = min(TOKEN_TILE, T)

    def body(u_ref, x_ref, vec_ref, w_ref, xn_ref, y_ref):
        y = _dot(u_ref[...], w_ref[...])
        y_ref[...] = y.astype(BF16)
        xn_ref[...] = x_ref[...] + (res_weight * (1.0 + vec_ref[3:4])) * y

    return pl.pallas_call(
        body, grid=(T // tm,),
        in_specs=[_tok(tm, K), _tok(tm, D), _res((8, D)), _res((K, D))],
        out_specs=[_tok(tm, D), _tok(tm, D)],
        out_shape=[_sds((T, D), F32), _sds((T, D), BF16)],
        compiler_params=_cp("arbitrary"), name=name)(u, x, vec, w_out)


def ffn_down_bwd(dxo, y, vec, w_out, a, b):
    T, D = dxo.shape
    F = a.shape[1]
    tm, tf = min(TOKEN_TILE, T), _pick(F, 1408)

    def body(dxo_ref, y_ref, vec_ref, w_ref, a_ref, b_ref, dy_ref, da_ref, db_ref, part_ref, dys):
        i, j = pl.program_id(0), pl.program_id(1)

        @pl.when(j == 0)
        def _():
            dxo_t = dxo_ref[...]
            dyb = (dxo_t * (FFN_RES_WEIGHT * (1.0 + vec_ref[3:4]))).astype(BF16)
            dys[...] = dyb
            dy_ref[...] = dyb
            dgate = FFN_RES_WEIGHT * jnp.sum(dxo_t * y_ref[...].astype(F32), axis=0, keepdims=True)
            _acc_rows(part_ref, _rows8([dgate], D), i == 0)

        du = _dot_nt(dys[...], w_ref[...])
        av = a_ref[...].astype(F32)
        bv = b_ref[...].astype(F32)
        sg = _sigmoid(av)
        da_ref[...] = (du * bv * (sg * (1.0 + av * (1.0 - sg)))).astype(BF16)
        db_ref[...] = (du * (av * sg)).astype(BF16)

    row = pl.BlockSpec((tm, D), lambda i, j: (i, 0))
    til = pl.BlockSpec((tm, tf), lambda i, j: (i, j))
    return pl.pallas_call(
        body, grid=(T // tm, F // tf),
        in_specs=[row, row, pl.BlockSpec((8, D), lambda i, j: (0, 0)), pl.BlockSpec((tf, D), lambda i, j: (j, 0)), til, til],
        out_specs=[row, til, til, pl.BlockSpec((8, D), lambda i, j: (0, 0))],
        out_shape=[_sds((T, D), BF16), _sds((T, F), BF16), _sds((T, F), BF16), _sds((8, D), F32)],
        scratch_shapes=[pltpu.VMEM((tm, D), BF16)],
        compiler_params=_cp("arbitrary", "arbitrary"), name="ffn_down_bwd")(dxo, y, vec, w_out, a, b)


def ffn_up_bwd(da, db, w_in, x, dxo, vec):
    T, D = x.shape
    F = da.shape[1]
    tm, tf = min(TOKEN_TILE, T), _pick(F, 1408)
    nj = F // tf

    def body(da_ref, db_ref, wa_ref, wb_ref, x_ref, dxo_ref, vec_ref, dx_ref, part_ref, acc):
        i, j = pl.program_id(0), pl.program_id(1)
        t = _dot_nt(da_ref[...], wa_ref[...]) + _dot_nt(db_ref[...], wb_ref[...])

        @pl.when(j == 0)
        def _():
            acc[...] = t

        @pl.when(j > 0)
        def _():
            acc[...] += t

        @pl.when(j == nj - 1)
        def _():
            _, vjp = jax.vjp(_modnorm, x_ref[...], vec_ref[0:1], vec_ref[1:2], vec_ref[2:3])
            dx, dg, dsh, dsc = vjp(acc[...])
            dx_ref[...] = dxo_ref[...] + dx
            _acc_rows(part_ref, _rows8([dg, dsh, dsc], D), i == 0)

    row = pl.BlockSpec((tm, D), lambda i, j: (i, 0))
    til = pl.BlockSpec((tm, tf), lambda i, j: (i, j))
    return pl.pallas_call(
        body, grid=(T // tm, nj),
        in_specs=[til, til, pl.BlockSpec((D, tf), lambda i, j: (0, j)), pl.BlockSpec((D, tf), lambda i, j: (0, nj + j)),
                  row, row, pl.BlockSpec((8, D), lambda i, j: (0, 0))],
        out_specs=[row, pl.BlockSpec((8, D), lambda i, j: (0, 0))],
        out_shape=[_sds((T, D), F32), _sds((8, D), F32)],
        scratch_shapes=[pltpu.VMEM((tm, D), F32)],
        compiler_params=_cp("arbitrary", "arbitrary"), name="ffn_up_bwd")(da, db, w_in, w_in, x, dxo, vec)


def mm_tn(a, b, name):
    T, M = a.shape
    N = b.shape[1]
    tk = min(TOKEN_TILE, T)
    tmm, tn = _pick(M, 1408), _pick(N, 1536)

    def body(a_ref, b_ref, o_ref):
        t = _dot_tn(a_ref[...], b_ref[...])

        @pl.when(pl.program_id(2) == 0)
        def _():
            o_ref[...] = t

        @pl.when(pl.program_id(2) > 0)
        def _():
            o_ref[...] += t

    return pl.pallas_call(
        body, grid=(M // tmm, N // tn, T // tk),
        in_specs=[pl.BlockSpec((tk, tmm), lambda i, j, k: (k, i)), pl.BlockSpec((tk, tn), lambda i, j, k: (k, j))],
        out_specs=pl.BlockSpec((tmm, tn), lambda i, j, k: (i, j)),
        out_shape=_sds((M, N), F32),
        compiler_params=_cp("arbitrary", "arbitrary", "arbitrary"), name=name)(a, b)


def conv_fwd(x, vec, cw, w_in, w_out):
    T, D = x.shape
    tm = min(TOKEN_TILE, T)

    def body(x_ref, vec_ref, cw_ref, wi_ref, wo_ref, xn_ref, h_ref, bcu_ref, cv_ref, z_ref, y_ref, vbuf):
        @pl.when(pl.program_id(0) == 0)
        def _():
            vbuf[0:8, :] = jnp.zeros((8, D), F32)

        x_t = x_ref[...]
        hb = _modnorm(x_t, vec_ref[0:1], vec_ref[1:2], vec_ref[2:3]).astype(BF16)
        h_ref[...] = hb
        bcu = _dot(hb, wi_ref[...])
        bcu_ref[...] = bcu.astype(BF16)
        bg, v = bcu[:, 0:D], bcu[:, D:2 * D] * bcu[:, 2 * D:3 * D]
        vbuf[8:8 + tm, :] = v
        conv = cw_ref[0:1] * vbuf[6:6 + tm, :] + cw_ref[1:2] * vbuf[7:7 + tm, :] + cw_ref[2:3] * v
        cv_ref[...] = conv.astype(BF16)
        zb = (bg * conv).astype(BF16)
        z_ref[...] = zb
        y = _dot(zb, wo_ref[...])
        y_ref[...] = y.astype(BF16)
        xn_ref[...] = x_t + (1.0 + vec_ref[3:4]) * y
        vbuf[0:8, :] = vbuf[tm:tm + 8, :]

    return pl.pallas_call(
        body, grid=(T // tm,),
        in_specs=[_tok(tm, D), _res((8, D)), _res((8, D)), _res((D, 3 * D)), _res((D, D))],
        out_specs=[_tok(tm, D), _tok(tm, D), _tok(tm, 3 * D), _tok(tm, D), _tok(tm, D), _tok(tm, D)],
        out_shape=[_sds((T, D), F32), _sds((T, D), BF16), _sds((T, 3 * D), BF16), _sds((T, D), BF16),
                   _sds((T, D), BF16), _sds((T, D), BF16)],
        scratch_shapes=[pltpu.VMEM((tm + 8, D), F32)],
        compiler_params=_cp("arbitrary"), name="conv_fwd")(x, vec, cw, w_in, w_out)


def conv_bwd(dxo, x, y, bcu, cv, vec, cw, w_in, w_out):
    T, D = x.shape
    tm = min(TOKEN_TILE, T)
    nt = T // tm

    def body(dxo_ref, x_ref, y_ref, bcu_ref, cv_ref, vec_ref, cw_ref, wi_ref, wo_ref,
             dx_ref, dy_ref, dbcu_ref, part_ref, dcw_ref, dcbuf):
        first = pl.program_id(0) == 0

        @pl.when(first)
        def _():
            dcbuf[tm:tm + 8, :] = jnp.zeros((8, D), F32)

        dxo_t = dxo_ref[...]
        dyb = (dxo_t * (1.0 + vec_ref[3:4])).astype(BF16)
        dy_ref[...] = dyb
        dgate = jnp.sum(dxo_t * y_ref[...].astype(F32), axis=0, keepdims=True)
        dz = _dot_nt(dyb, wo_ref[...])
        bcu_t = bcu_ref[...].astype(F32)
        bg, cg, ug = bcu_t[:, 0:D], bcu_t[:, D:2 * D], bcu_t[:, 2 * D:3 * D]
        dconv = dz * bg
        dbg = dz * cv_ref[...].astype(F32)
        dcbuf[0:tm, :] = dconv
        d1, d2 = dcbuf[1:tm + 1, :], dcbuf[2:tm + 2, :]
        dv = cw_ref[2:3] * dconv + cw_ref[1:2] * d1 + cw_ref[0:1] * d2
        v = cg * ug
        dcw = _rows8([jnp.sum(d2 * v, axis=0, keepdims=True), jnp.sum(d1 * v, axis=0, keepdims=True),
                      jnp.sum(dconv * v, axis=0, keepdims=True)], D)
        dbcu = jnp.concatenate([dbg, dv * ug, dv * cg], axis=1).astype(BF16)
        dbcu_ref[...] = dbcu
        dh = _dot_nt(dbcu, wi_ref[...])
        _, vjp = jax.vjp(_modnorm, x_ref[...], vec_ref[0:1], vec_ref[1:2], vec_ref[2:3])
        dx, dg, dsh, dsc = vjp(dh)
        dx_ref[...] = dxo_t + dx
        _acc_rows(part_ref, _rows8([dg, dsh, dsc, dgate], D), first)
        _acc_rows(dcw_ref, dcw, first)
        dcbuf[tm:tm + 8, :] = dcbuf[0:8, :]

    def rev(w):
        return pl.BlockSpec((tm, w), lambda i: (nt - 1 - i, 0))

    return pl.pallas_call(
        body, grid=(nt,),
        in_specs=[rev(D), rev(D), rev(D), rev(3 * D), rev(D), _res((8, D)), _res((8, D)), _res((D, 3 * D)), _res((D, D))],
        out_specs=[rev(D), rev(D), rev(3 * D), pl.BlockSpec((8, D), lambda i: (0, 0)), pl.BlockSpec((8, D), lambda i: (0, 0))],
        out_shape=[_sds((T, D), F32), _sds((T, D), BF16), _sds((T, 3 * D), BF16), _sds((8, D), F32), _sds((8, D), F32)],
        scratch_shapes=[pltpu.VMEM((tm + 8, D), F32)],
        compiler_params=_cp("arbitrary"), name="conv_bwd")(dxo, x, y, bcu, cv, vec, cw, w_in, w_out)


def rope_tables(pos, lane_rows):
    T = pos.shape[0]
    tm = min(TOKEN_TILE, T)

    def body(p_ref, lr_ref, c_ref, sp_ref, sm_ref):
        ang = p_ref[...].astype(F32) * lr_ref[0:1]
        cs, sn = jnp.cos(ang), jnp.sin(ang)
        c_ref[...] = jnp.where(lr_ref[1:2] > 0.5, cs, 1.0)
        sp_ref[...] = jnp.where(lr_ref[2:3] > 0.5, sn, 0.0)
        sm_ref[...] = jnp.where(lr_ref[3:4] > 0.5, -sn, 0.0)

    return pl.pallas_call(
        body, grid=(T // tm,),
        in_specs=[_tok(tm, 1), _res((8, LANES))],
        out_specs=[_tok(tm, LANES)] * 3,
        out_shape=[_sds((T, LANES), F32)] * 3,
        compiler_params=_cp("arbitrary"), name="rope_tables")(pos, lane_rows)


def _rope(t, c, sp, sm):
    w = t.shape[1]
    reps = w // LANES
    cf, spf, smf = jnp.tile(c, (1, reps)), jnp.tile(sp, (1, reps)), jnp.tile(sm, (1, reps))
    half = ROPE_DIM // 2
    return t * cf + pltpu.roll(t, half, axis=1) * spf + pltpu.roll(t, w - half, axis=1) * smf


def _rope_t(d, c, sp, sm):
    w = d.shape[1]
    reps = w // LANES
    cf, spf, smf = jnp.tile(c, (1, reps)), jnp.tile(sp, (1, reps)), jnp.tile(sm, (1, reps))
    half = ROPE_DIM // 2
    return d * cf + pltpu.roll(d * spf, w - half, axis=1) + pltpu.roll(d * smf, half, axis=1)


def proj_rope_fwd(x, vec, w, tabs, n_rope, name):
    T, D = x.shape
    N = w.shape[1]
    tm = min(TOKEN_TILE, T)

    def body(x_ref, vec_ref, w_ref, c_ref, sp_ref, sm_ref, h_ref, p_ref):
        hb = _modnorm(x_ref[...], vec_ref[0:1], vec_ref[1:2], vec_ref[2:3]).astype(BF16)
        h_ref[...] = hb
        p = _dot(hb, w_ref[...])
        pr = _rope(p[:, 0:n_rope], c_ref[...], sp_ref[...], sm_ref[...])
        if n_rope < N:
            pr = jnp.concatenate([pr, p[:, n_rope:N]], axis=1)
        p_ref[...] = pr.astype(BF16)

    return pl.pallas_call(
        body, grid=(T // tm,),
        in_specs=[_tok(tm, D), _res((8, D)), _res((D, N))] + [_tok(tm, LANES)] * 3,
        out_specs=[_tok(tm, D), _tok(tm, N)],
        out_shape=[_sds((T, D), BF16), _sds((T, N), BF16)],
        compiler_params=_cp("arbitrary"), name=name)(x, vec, w, *tabs)


def proj_rope_bwd(dparts, x, dxo, vec, w, tabs, n_rope, name):
    T, D = x.shape
    N = w.shape[1]
    tm = min(TOKEN_TILE, T)
    npart = len(dparts)

    def body(*refs):
        d_refs = refs[:npart]
        x_ref, dxo_ref, vec_ref, w_ref, c_ref, sp_ref, sm_ref, dx_ref, dp_ref, part_ref = refs[npart:]
        d = jnp.concatenate([r[...].astype(F32) for r in d_refs], axis=1)
        dr = _rope_t(d[:, 0:n_rope], c_ref[...], sp_ref[...], sm_ref[...])
        if n_rope < N:
            dr = jnp.concatenate([dr, d[:, n_rope:N]], axis=1)
        dpb = dr.astype(BF16)
        dp_ref[...] = dpb
        dh = _dot_nt(dpb, w_ref[...])
        _, vjp = jax.vjp(_modnorm, x_ref[...], vec_ref[0:1], vec_ref[1:2], vec_ref[2:3])
        dx, dg, dsh, dsc = vjp(dh)
        dx_ref[...] = dxo_ref[...] + dx
        _acc_rows(part_ref, _rows8([dg, dsh, dsc], D), pl.program_id(0) == 0)

    return pl.pallas_call(
        body, grid=(T // tm,),
        in_specs=[_tok(tm, p.shape[1]) for p in dparts] + [_tok(tm, D), _tok(tm, D), _res((8, D)), _res((D, N))]
        + [_tok(tm, LANES)] * 3,
        out_specs=[_tok(tm, D), _tok(tm, N), pl.BlockSpec((8, D), lambda i: (0, 0))],
        out_shape=[_sds((T, D), F32), _sds((T, N), BF16), _sds((8, D), F32)],
        compiler_params=_cp("arbitrary"), name=name)(*dparts, x, dxo, vec, w, *tabs)


def _valid_mask(n, i):
    qi = lax.broadcasted_iota(jnp.int32, (n, 2 * n), 0)
    kj = lax.broadcasted_iota(jnp.int32, (n, 2 * n), 1)
    dist = n + qi - kj
    return (dist >= 0) & (dist <= n) & ((kj >= n) | (i > 0))


def attn_core_fwd(q, k, v, g, n, d):
    T, QW = q.shape
    GW = GROUP_WIDTH
    ng = QW // GW
    M = T // d
    scale = HEAD_DIM ** -0.5

    def body(q_ref, kp_ref, kc_ref, vp_ref, vc_ref, o_ref, l_ref):
        valid = _valid_mask(n, pl.program_id(1))
        qv = q_ref[...]
        kk = jnp.concatenate([kp_ref[...], kc_ref[...]], axis=0)
        vv = jnp.concatenate([vp_ref[...], vc_ref[...]], axis=0)
        for h in range(HEADS_PER_GROUP):
            hs = slice(HEAD_DIM * h, HEAD_DIM * (h + 1))
            s = jnp.where(valid, _dot_nt(qv[:, hs], kk[:, hs]) * scale, -1e30)
            m = jnp.max(s, axis=1, keepdims=True)
            p = jnp.exp(s - m)
            den = jnp.sum(p, axis=1, keepdims=True)
            o_ref[:, hs] = _dot((p / den).astype(BF16), vv[:, hs])
            l_ref[:, hs] = jnp.broadcast_to(m + jnp.log(den), (n, HEAD_DIM))

    cur = pl.BlockSpec((n, GW), lambda r, i: (i, r * ng + g))
    prv = pl.BlockSpec((n, GW), lambda r, i: (jnp.maximum(i - 1, 0), r * ng + g))
    out = pl.BlockSpec((n, GW), lambda r, i: (i, r))
    qv, kv, vv = q.reshape(M, d * QW), k.reshape(M, d * QW), v.reshape(M, d * QW)
    o, l = pl.pallas_call(
        body, grid=(d, M // n),
        in_specs=[cur, prv, cur, prv, cur], out_specs=[out, out],
        out_shape=[_sds((M, d * GW), F32), _sds((M, d * GW), F32)],
        compiler_params=_cp("arbitrary", "arbitrary"), name=f"attn_fwd_g{g}")(qv, kv, kv, vv, vv)
    return o.reshape(T, GW), l.reshape(T, GW)


def attn_core_bwd(q, k, v, do, rr, lse, g, n, d):
    T, QW = q.shape
    GW = GROUP_WIDTH
    ng = QW // GW
    M = T // d
    scale = HEAD_DIM ** -0.5

    def body(q_ref, kp_ref, kc_ref, vp_ref, vc_ref, do_ref, r_ref, l_ref, dq_ref, dkc_ref, dkp_ref, dvc_ref, dvp_ref):
        valid = _valid_mask(n, pl.program_id(1))
        qv, dov = q_ref[...], do_ref[...]
        kk = jnp.concatenate([kp_ref[...], kc_ref[...]], axis=0)
        vv = jnp.concatenate([vp_ref[...], vc_ref[...]], axis=0)
        for h in range(HEADS_PER_GROUP):
            hs = slice(HEAD_DIM * h, HEAD_DIM * (h + 1))
            s = jnp.where(valid, _dot_nt(qv[:, hs], kk[:, hs]) * scale, -1e30)
            p = jnp.exp(s - l_ref[:, HEAD_DIM * h:HEAD_DIM * h + 1])
            dp = _dot_nt(dov[:, hs], vv[:, hs])
            delta = jnp.sum(r_ref[:, hs], axis=1, keepdims=True)
            ds = (p * (dp - delta) * scale).astype(BF16)
            dq_ref[:, hs] = _dot(ds, kk[:, hs]).astype(BF16)
            dk = _dot_tn(ds, qv[:, hs]).astype(BF16)
            dv = _dot_tn(p.astype(BF16), dov[:, hs]).astype(BF16)
            dkp_ref[:, hs], dkc_ref[:, hs] = dk[0:n], dk[n:2 * n]
            dvp_ref[:, hs], dvc_ref[:, hs] = dv[0:n], dv[n:2 * n]

    cur = pl.BlockSpec((n, GW), lambda r, i: (i, r * ng + g))
    prv = pl.BlockSpec((n, GW), lambda r, i: (jnp.maximum(i - 1, 0), r * ng + g))
    blk = pl.BlockSpec((n, GW), lambda r, i: (i, r))
    qv, kv, vv = q.reshape(M, d * QW), k.reshape(M, d * QW), v.reshape(M, d * QW)
    outs = pl.pallas_call(
        body, grid=(d, M // n),
        in_specs=[cur, prv, cur, prv, cur, blk, blk, blk], out_specs=[blk] * 5,
        out_shape=[_sds((M, d * GW), BF16)] * 5,
        compiler_params=_cp("arbitrary", "arbitrary"), name=f"attn_bwd_g{g}")(
            qv, kv, kv, vv, vv, do.reshape(M, d * GW), rr.reshape(M, d * GW), lse.reshape(M, d * GW))
    return [o.reshape(T, GW) for o in outs]


def dkv_combine(cur_prev, n, d, name):
    T, GW = cur_prev[0][0].shape
    M = T // d
    nb = M // n
    flat = [a.reshape(M, d * GW) for pair in cur_prev for a in pair]

    def body(*refs):
        o_ref = refs[-1]
        last = pl.program_id(1) == nb - 1
        acc = jnp.zeros((n, GW), F32)
        for t in range(0, len(refs) - 1, 2):
            acc = acc + refs[t][...].astype(F32) + jnp.where(last, 0.0, refs[t + 1][...].astype(F32))
        o_ref[...] = acc.astype(BF16)

    cur = pl.BlockSpec((n, GW), lambda r, i: (i, r))
    nxt = pl.BlockSpec((n, GW), lambda r, i: (jnp.minimum(i + 1, nb - 1), r))
    out = pl.pallas_call(
        body, grid=(d, nb), in_specs=[cur, nxt] * len(cur_prev), out_specs=cur,
        out_shape=_sds((M, d * GW), BF16),
        compiler_params=_cp("arbitrary", "arbitrary"), name=name)(*flat)
    return out.reshape(T, GW)


def _group_weights(ls):
    mx = functools.reduce(jnp.maximum, ls)
    es = [jnp.exp(l - mx) for l in ls]
    tot = functools.reduce(lambda a, b: a + b, es)
    return [e / tot for e in es]


def attn_mix_out(os_, ls, x, vec, w_o):
    T, D = x.shape
    GW = GROUP_WIDTH
    tm = min(TOKEN_TILE, T)
    ng = len(os_)

    def body(*refs):
        o_refs, l_refs = refs[:ng], refs[ng:2 * ng]
        x_ref, vec_ref, w_ref, xn_ref, mix_ref, y_ref = refs[2 * ng:]
        ws = _group_weights([r[...] for r in l_refs])
        mixed = functools.reduce(lambda a, b: a + b, [w * r[...] for w, r in zip(ws, o_refs)])
        mb = mixed.astype(BF16)
        mix_ref[...] = mb
        y = _dot(mb, w_ref[...])
        y_ref[...] = y.astype(BF16)
        xn_ref[...] = x_ref[...] + (1.0 + vec_ref[3:4]) * y

    return pl.pallas_call(
        body, grid=(T // tm,),
        in_specs=[_tok(tm, GW)] * (2 * ng) + [_tok(tm, D), _res((8, D)), _res((GW, D))],
        out_specs=[_tok(tm, D), _tok(tm, GW), _tok(tm, D)],
        out_shape=[_sds((T, D), F32), _sds((T, GW), BF16), _sds((T, D), BF16)],
        compiler_params=_cp("arbitrary"), name="attn_mix_out")(*os_, *ls, x, vec, w_o)


def attn_mix_bwd(dxo, y, vec, w_o, os_, ls):
    T, D = dxo.shape
    GW = GROUP_WIDTH
    tm = min(TOKEN_TILE, T)
    ng = len(os_)

    def body(*refs):
        dxo_ref, y_ref, vec_ref, w_ref = refs[:4]
        o_refs, l_refs = refs[4:4 + ng], refs[4 + ng:4 + 2 * ng]
        dy_ref = refs[4 + 2 * ng]
        do_refs = refs[5 + 2 * ng:5 + 3 * ng]
        r_refs = refs[5 + 3 * ng:5 + 4 * ng]
        part_ref = refs[5 + 4 * ng]
        dxo_t = dxo_ref[...]
        dyb = (dxo_t * (1.0 + vec_ref[3:4])).astype(BF16)
        dy_ref[...] = dyb
        dgate = jnp.sum(dxo_t * y_ref[...].astype(F32), axis=0, keepdims=True)
        _acc_rows(part_ref, _rows8([dgate], D), pl.program_id(0) == 0)
        dmix = _dot_nt(dyb, w_ref[...])
        ws = _group_weights([r[...] for r in l_refs])
        ov = [r[...] for r in o_refs]
        mixed = functools.reduce(lambda a, b: a + b, [w * o for w, o in zip(ws, ov)])
        for gi in range(ng):
            do = ws[gi] * dmix
            do_refs[gi][...] = do.astype(BF16)
            r_refs[gi][...] = do * mixed

    return pl.pallas_call(
        body, grid=(T // tm,),
        in_specs=[_tok(tm, D), _tok(tm, D), _res((8, D)), _res((GW, D))] + [_tok(tm, GW)] * (2 * ng),
        out_specs=[_tok(tm, D)] + [_tok(tm, GW)] * (2 * ng) + [pl.BlockSpec((8, D), lambda i: (0, 0))],
        out_shape=[_sds((T, D), BF16)] + [_sds((T, GW), BF16)] * ng + [_sds((T, GW), F32)] * ng + [_sds((8, D), F32)],
        compiler_params=_cp("arbitrary"), name="attn_mix_bwd")(dxo, y, vec, w_o, *os_, *ls)


def final_loss(x, gvec, target):
    T, D = x.shape
    tm = min(TOKEN_TILE, T)

    def norm(xv, g):
        return xv * lax.rsqrt(jnp.mean(xv * xv, axis=-1, keepdims=True) + NORM_EPS) * g

    def body(x_ref, g_ref, t_ref, dx_ref, part_ref, loss_ref):
        first = pl.program_id(0) == 0
        yv, vjp = jax.vjp(norm, x_ref[...], g_ref[0:1])
        err = yv - t_ref[...]
        dx, dg = vjp(err * (1.0 / D))
        dx_ref[...] = dx
        _acc_rows(part_ref, _rows8([dg], D), first)
        tile_loss = 0.5 * jnp.sum(jnp.sum(err * err, axis=1, keepdims=True) * (1.0 / D), axis=0, keepdims=True)
        _acc_rows(loss_ref, jnp.broadcast_to(tile_loss, (8, LANES)), first)

    return pl.pallas_call(
        body, grid=(T // tm,),
        in_specs=[_tok(tm, D), _res((8, D)), _tok(tm, D)],
        out_specs=[_tok(tm, D), pl.BlockSpec((8, D), lambda i: (0, 0)), pl.BlockSpec((8, LANES), lambda i: (0, 0))],
        out_shape=[_sds((T, D), F32), _sds((8, D), F32), _sds((8, LANES), F32)],
        compiler_params=_cp("arbitrary"), name="final_loss")(x, gvec, target)


def mods_project(c_all, w, b):
    B, D = c_all.shape
    L, _, N = w.shape

    def body(c_ref, w_ref, b_ref, o_ref):
        cv = c_ref[...]
        cond = cv * _sigmoid(cv)
        o_ref[0] = jnp.dot(cond, w_ref[0], preferred_element_type=F32, precision=lax.Precision.HIGHEST) + b_ref[0]

    return pl.pallas_call(
        body, grid=(L,),
        in_specs=[pl.BlockSpec((B, D), lambda l: (0, 0)), pl.BlockSpec((1, D, N), lambda l: (l, 0, 0)),
                  pl.BlockSpec((1, 1, N), lambda l: (l, 0, 0))],
        out_specs=pl.BlockSpec((1, B, N), lambda l: (l, 0, 0)),
        out_shape=_sds((L, B, N), F32),
        compiler_params=_cp("arbitrary"), name="mods_project")(c_all, w, b)


def mods_weight_grad(c_all, dm):
    B, D = c_all.shape
    L, _, N = dm.shape

    def body(c_ref, d_ref, o_ref):
        cv = c_ref[...]
        cond = cv * _sigmoid(cv)
        o_ref[0] = lax.dot_general(cond, d_ref[0], (((0,), (0,)), ((), ())), preferred_element_type=F32,
                                   precision=lax.Precision.HIGHEST)

    return pl.pallas_call(
        body, grid=(L,),
        in_specs=[pl.BlockSpec((B, D), lambda l: (0, 0)), pl.BlockSpec((1, B, N), lambda l: (l, 0, 0))],
        out_specs=pl.BlockSpec((1, D, N), lambda l: (l, 0, 0)),
        out_shape=_sds((L, D, N), F32),
        compiler_params=_cp("arbitrary"), name="mods_weight_grad")(c_all, dm)


def _adam_math(g, w, m, v):
    m2 = ADAM_B1 * m + (1.0 - ADAM_B1) * g
    v2 = ADAM_B2 * v + (1.0 - ADAM_B2) * (g * g)
    m_hat = m2 / (1.0 - ADAM_B1 ** ADAM_STEP)
    v_hat = v2 / (1.0 - ADAM_B2 ** ADAM_STEP)
    delta = -ADAM_LR * (m_hat / (jnp.sqrt(v_hat) + ADAM_EPS) + ADAM_WD * w)
    return delta, m2, v2


def adam_update(g, w, m, v, parts, name):
    R, C = w.shape
    tr = _pick(R, 256, 8)

    def body(g_ref, w_ref, m_ref, v_ref, go_ref, d_ref, mo_ref, vo_ref):
        if parts:
            gv = g_ref[0].astype(F32)
            for s in range(1, N_DEV):
                gv = gv + g_ref[s].astype(F32)
        else:
            gv = g_ref[...]
        go_ref[...] = gv
        d_ref[...], mo_ref[...], vo_ref[...] = _adam_math(gv, w_ref[...], m_ref[...], v_ref[...])

    gspec = pl.BlockSpec((N_DEV, tr, C), lambda i: (0, i, 0)) if parts else _tok(tr, C)
    return pl.pallas_call(
        body, grid=(R // tr,),
        in_specs=[gspec, _tok(tr, C), _tok(tr, C), _tok(tr, C)],
        out_specs=[_tok(tr, C)] * 4, out_shape=[_sds((R, C), F32)] * 4,
        compiler_params=_cp("arbitrary"), name=name)(g, w, m, v)


def _my_id():
    return 4 * lax.axis_index("x") + 2 * lax.axis_index("y") + lax.axis_index("c")


def _peer(s):
    x, y, c = lax.axis_index("x"), lax.axis_index("y"), lax.axis_index("c")
    px = (1 - x) if s & 4 else x
    py = (1 - y) if s & 2 else y
    pc = (1 - c) if s & 1 else c
    return (px, py, pc), 4 * px + 2 * py + pc


def all_gather(xs, space, name):
    na = len(xs)

    def body(*refs):
        x_refs, o_refs = refs[:na], refs[na:2 * na]
        send_sems, recv_sems, local_sems = refs[2 * na:]
        me = _my_id()
        locals_, sends = [], []
        for a in range(na):
            cp = pltpu.make_async_copy(x_refs[a], o_refs[a].at[me], local_sems.at[a])
            cp.start()
            locals_.append(cp)
        for s in range(1, N_DEV):
            peer, _ = _peer(s)
            for a in range(na):
                cp = pltpu.make_async_remote_copy(
                    src_ref=x_refs[a], dst_ref=o_refs[a].at[me], send_sem=send_sems.at[a, s - 1],
                    recv_sem=recv_sems.at[a, s - 1], device_id=peer, device_id_type=MESH)
                cp.start()
                sends.append(cp)
        for s in range(1, N_DEV):
            peer, pid = _peer(s)
            for a in range(na):
                pltpu.make_async_remote_copy(
                    src_ref=x_refs[a], dst_ref=o_refs[a].at[pid], send_sem=send_sems.at[a, s - 1],
                    recv_sem=recv_sems.at[a, s - 1], device_id=peer, device_id_type=MESH).wait_recv()
        for cp in sends:
            cp.wait_send()
        for cp in locals_:
            cp.wait()

    spec = pl.BlockSpec(memory_space=space)
    return pl.pallas_call(
        body, in_specs=[spec] * na, out_specs=[spec] * na,
        out_shape=[_sds((N_DEV,) + x.shape, x.dtype) for x in xs],
        scratch_shapes=[pltpu.SemaphoreType.DMA((na, N_DEV - 1)), pltpu.SemaphoreType.DMA((na, N_DEV - 1)),
                        pltpu.SemaphoreType.DMA((na,))],
        compiler_params=pltpu.CompilerParams(vmem_limit_bytes=VMEM_LIMIT), name=name)(*xs)


def exchange_slots(xs, name):
    na = len(xs)

    def body(*refs):
        x_refs, o_refs = refs[:na], refs[na:2 * na]
        send_sems, recv_sems, local_sems = refs[2 * na:]
        me = _my_id()
        locals_, sends = [], []
        for a in range(na):
            cp = pltpu.make_async_copy(x_refs[a].at[me], o_refs[a].at[me], local_sems.at[a])
            cp.start()
            locals_.append(cp)
        for s in range(1, N_DEV):
            peer, pid = _peer(s)
            for a in range(na):
                cp = pltpu.make_async_remote_copy(
                    src_ref=x_refs[a].at[pid], dst_ref=o_refs[a].at[me], send_sem=send_sems.at[a, s - 1],
                    recv_sem=recv_sems.at[a, s - 1], device_id=peer, device_id_type=MESH)
                cp.start()
                sends.append(cp)
        for s in range(1, N_DEV):
            peer, pid = _peer(s)
            for a in range(na):
                pltpu.make_async_remote_copy(
                    src_ref=x_refs[a].at[pid], dst_ref=o_refs[a].at[pid], send_sem=send_sems.at[a, s - 1],
                    recv_sem=recv_sems.at[a, s - 1], device_id=peer, device_id_type=MESH).wait_recv()
        for cp in sends:
            cp.wait_send()
        for cp in locals_:
            cp.wait()

    spec = pl.BlockSpec(memory_space=pl.ANY)
    return pl.pallas_call(
        body, in_specs=[spec] * na, out_specs=[spec] * na,
        out_shape=[_sds(x.shape, x.dtype) for x in xs],
        scratch_shapes=[pltpu.SemaphoreType.DMA((na, N_DEV - 1)), pltpu.SemaphoreType.DMA((na, N_DEV - 1)),
                        pltpu.SemaphoreType.DMA((na,))],
        compiler_params=pltpu.CompilerParams(vmem_limit_bytes=VMEM_LIMIT), name=name)(*xs)


def _cols_to_natural(g, lead):
    ns = g.shape[2]
    return g.transpose(1, 0, 2).reshape(tuple(lead) + (N_DEV * ns,))


def _cols_to_slots(w):
    n = w.shape[-1]
    rows = w.size // n
    return w.reshape(rows, N_DEV, n // N_DEV).transpose(1, 0, 2)


def _rows_to_natural(g, lead, rs):
    n = g.shape[2]
    nl = len(lead)
    g = g.reshape((N_DEV,) + tuple(lead) + (rs, n))
    perm = tuple(range(1, nl + 1)) + (0, nl + 1, nl + 2)
    return g.transpose(perm).reshape(tuple(lead) + (N_DEV * rs, n))


def _rows_to_slots(w, nl):
    lead, (r, n) = w.shape[:nl], w.shape[nl:]
    g = w.reshape(tuple(lead) + (N_DEV, r // N_DEV, n))
    perm = (nl,) + tuple(range(nl)) + (nl + 1, nl + 2)
    return g.transpose(perm).reshape(N_DEV, -1, n)


def _vec8(rows, d):
    rows = [r.reshape(1, d).astype(F32) for r in rows]
    return jnp.concatenate(rows + [jnp.zeros((8 - len(rows), d), F32)], axis=0)


def _ffn_forward(x, vec, w_in, w_out):
    h, a, b, u = ffn_up(x, vec, w_in)
    xn, y = proj_out(u, x, vec, w_out, FFN_RES_WEIGHT, "ffn_down")
    return xn, (x, h, a, b, u, y)


def _ffn_backward(dxo, saved, vec, w_in, w_out):
    x, h, a, b, u, y = saved
    dy, da, db, part_gate = ffn_down_bwd(dxo, y, vec, w_out, a, b)
    dx, part_norm = ffn_up_bwd(da, db, w_in, x, dxo, vec)
    g_out = mm_tn(u, dy, "ffn_dw_out")
    g_in = jnp.concatenate([mm_tn(h, da, "ffn_dw_a"), mm_tn(h, db, "ffn_dw_b")], axis=1)
    rows = jnp.concatenate([part_norm[0:3], part_gate[0:1]], axis=0)
    return dx, g_in, g_out, rows


def device_step(x, positions, target, mods, kvmods, W):
    T, D = x.shape
    groups = DILATED_GROUPS
    lane = jnp.arange(LANES) % HEAD_DIM
    inv = ROPE_THETA ** (-jnp.arange(0, ROPE_DIM, 2, dtype=F32) / ROPE_DIM)
    lane_rows = _vec8([jnp.where(lane < ROPE_DIM, inv[lane % (ROPE_DIM // 2)], 0.0), lane < ROPE_DIM,
                       (lane >= ROPE_DIM // 2) & (lane < ROPE_DIM), lane < ROPE_DIM // 2], LANES)
    tabs = rope_tables(positions.reshape(T, 1), lane_rows)

    def vec_of(layer, sub):
        return _vec8([W["norm_g"][layer, sub], mods[layer, 3 * sub], mods[layer, 3 * sub + 1], mods[layer, 3 * sub + 2]], D)

    saved = []
    kv_saved = None
    k_sh = v_sh = None
    qw = W["attn_w_q"].shape[2] if DEPTH > N_A_LAYERS else 0
    for layer in range(DEPTH):
        if layer == N_A_LAYERS:
            kv_vec = _vec8([W["kv_norm_g"], kvmods[0], kvmods[1]], D)
            h_kv, kvp = proj_rope_fwd(x, kv_vec, W["w_kv"], tabs, qw, "kv_fwd")
            k_sh, v_sh = kvp[:, :qw], kvp[:, qw:]
            kv_saved = (x, h_kv, kv_vec)
        rec = {}
        v1 = vec_of(layer, 0)
        x, rec["ffn1"] = _ffn_forward(x, v1, W["ffn1_w_in"][layer], W["ffn1_w_out"][layer])
        v2 = vec_of(layer, 1)
        if layer < N_A_LAYERS:
            cw = _vec8(list(W["conv_w"][layer]), D)
            x_in = x
            x, h, bcu, cv, z, y = conv_fwd(x, v2, cw, W["conv_w_in"][layer], W["conv_w_out"][layer])
            rec["mix"] = (x_in, h, bcu, cv, z, y, cw)
        else:
            j = layer - N_A_LAYERS
            x_in = x
            h, q = proj_rope_fwd(x, v2, W["attn_w_q"][j], tabs, qw, "q_fwd")
            os_, ls = [], []
            for g, (win, dil) in enumerate(groups):
                o, l = attn_core_fwd(q, k_sh, v_sh, g, win // dil, dil)
                os_.append(o)
                ls.append(l)
            x, mixed, y = attn_mix_out(os_, ls, x, v2, W["attn_w_o"][j])
            rec["mix"] = (x_in, h, q, os_, ls, mixed, y)
        v3 = vec_of(layer, 2)
        x, rec["ffn2"] = _ffn_forward(x, v3, W["ffn2_w_in"][layer], W["ffn2_w_out"][layer])
        rec["vecs"] = (v1, v2, v3)
        saved.append(rec)

    dx, part_final, loss_tile = final_loss(x, _vec8([W["final_norm_g"]], D), target)
    loss = loss_tile[0, 0]

    G = {k: [None] * DEPTH for k in ("ffn1_w_in", "ffn1_w_out", "ffn2_w_in", "ffn2_w_out")}
    G["conv_w_in"], G["conv_w_out"], G["conv_w"] = [None] * N_A_LAYERS, [None] * N_A_LAYERS, [None] * N_A_LAYERS
    G["attn_w_q"], G["attn_w_o"] = [None] * (DEPTH - N_A_LAYERS), [None] * (DEPTH - N_A_LAYERS)
    mod_rows = [[None] * 3 for _ in range(DEPTH)]
    dkv_pairs = [{"k": [], "v": []} for _ in groups]
    for layer in reversed(range(DEPTH)):
        rec = saved[layer]
        v1, v2, v3 = rec["vecs"]
        dx, G["ffn2_w_in"][layer], G["ffn2_w_out"][layer], mod_rows[layer][2] = _ffn_backward(
            dx, rec["ffn2"], v3, W["ffn2_w_in"][layer], W["ffn2_w_out"][layer])
        if layer < N_A_LAYERS:
            x_in, h, bcu, cv, z, y, cw = rec["mix"]
            dx, dy, dbcu, part, dcw = conv_bwd(dx, x_in, y, bcu, cv, v2, cw, W["conv_w_in"][layer], W["conv_w_out"][layer])
            G["conv_w_out"][layer] = mm_tn(z, dy, "conv_dw_out")
            G["conv_w_in"][layer] = mm_tn(h, dbcu, "conv_dw_in")
            G["conv_w"][layer] = dcw[0:3]
            mod_rows[layer][1] = part[0:4]
        else:
            j = layer - N_A_LAYERS
            x_in, h, q, os_, ls, mixed, y = rec["mix"]
            outs = attn_mix_bwd(dx, y, v2, W["attn_w_o"][j], os_, ls)
            ng = len(groups)
            dy, dos, rrs, part_gate = outs[0], outs[1:1 + ng], outs[1 + ng:1 + 2 * ng], outs[1 + 2 * ng]
            G["attn_w_o"][j] = mm_tn(mixed, dy, "attn_dw_o")
            dqs = []
            for g, (win, dil) in enumerate(groups):
                dq, dkc, dkp, dvc, dvp = attn_core_bwd(q, k_sh, v_sh, dos[g], rrs[g], ls[g], g, win // dil, dil)
                dqs.append(dq)
                dkv_pairs[g]["k"].append((dkc, dkp))
                dkv_pairs[g]["v"].append((dvc, dvp))
            dx, dqr, part_norm = proj_rope_bwd(dqs, x_in, dx, v2, W["attn_w_q"][j], tabs, qw, "q_bwd")
            G["attn_w_q"][j] = mm_tn(h, dqr, "attn_dw_q")
            mod_rows[layer][1] = jnp.concatenate([part_norm[0:3], part_gate[0:1]], axis=0)
        dx, G["ffn1_w_in"][layer], G["ffn1_w_out"][layer], mod_rows[layer][0] = _ffn_backward(
            dx, rec["ffn1"], v1, W["ffn1_w_in"][layer], W["ffn1_w_out"][layer])
        if layer == N_A_LAYERS:
            x_kv, h_kv, kv_vec = kv_saved
            dparts = [dkv_combine(dkv_pairs[g]["k"], win // dil, dil, f"dk_combine_g{g}") for g, (win, dil) in enumerate(groups)]
            dparts += [dkv_combine(dkv_pairs[g]["v"], win // dil, dil, f"dv_combine_g{g}") for g, (win, dil) in enumerate(groups)]
            dx, dkvp, part_kv = proj_rope_bwd(dparts, x_kv, dx, kv_vec, W["w_kv"], tabs, qw, "kv_bwd")
            G["w_kv"] = mm_tn(h_kv, dkvp, "kv_dw")
            G["kv_rows"] = part_kv[0:3]

    grads = {k: jnp.stack(v) for k, v in G.items() if isinstance(v, list)}
    grads["w_kv"], grads["kv_rows"] = G["w_kv"], G["kv_rows"]
    grads["final_norm_g"] = part_final[0]
    rows = jnp.stack([jnp.stack(r) for r in mod_rows])
    grads["norm_g"] = rows[:, :, 0]
    grads["mods"] = rows[:, :, 1:4].reshape(DEPTH, N_MOD, D)
    return loss, dx, grads


_COL_SHARDED = ("ffn1_w_in", "ffn2_w_in", "conv_w_in", "w_kv", "attn_w_q", "attn_w_o")
_ROW_SHARDED = ("ffn1_w_out", "ffn2_w_out", "conv_w_out")
_BIG = _COL_SHARDED + _ROW_SHARDED


def _flat2(a):
    return a.reshape(-1, a.shape[-1])


def _pad_rows(a, mult):
    r = a.shape[0]
    pad = (-r) % mult
    return a if pad == 0 else jnp.concatenate([a, jnp.zeros((pad,) + a.shape[1:], a.dtype)], axis=0)


def kernel(x, c, positions, norm_g, ada_w, ada_b, ffn1_w_in, ffn1_w_out, ffn2_w_in, ffn2_w_out, conv_w_in, conv_w, conv_w_out, kv_norm_g, kv_ada_w, kv_ada_b, w_kv, attn_w_q, attn_w_o, final_norm_g, loss_target, m_norm_g, m_ada_w, m_ada_b, m_ffn1_w_in, m_ffn1_w_out, m_ffn2_w_in, m_ffn2_w_out, m_conv_w_in, m_conv_w, m_conv_w_out, m_kv_norm_g, m_kv_ada_w, m_kv_ada_b, m_w_kv, m_attn_w_q, m_attn_w_o, m_final_norm_g, v_norm_g, v_ada_w, v_ada_b, v_ffn1_w_in, v_ffn1_w_out, v_ffn2_w_in, v_ffn2_w_out, v_conv_w_in, v_conv_w, v_conv_w_out, v_kv_norm_g, v_kv_ada_w, v_kv_ada_b, v_w_kv, v_attn_w_q, v_attn_w_o, v_final_norm_g):
    names = ("norm_g", "ada_w", "ada_b", "ffn1_w_in", "ffn1_w_out", "ffn2_w_in", "ffn2_w_out", "conv_w_in", "conv_w",
             "conv_w_out", "kv_norm_g", "kv_ada_w", "kv_ada_b", "w_kv", "attn_w_q", "attn_w_o", "final_norm_g")
    wts = dict(zip(names, (norm_g, ada_w, ada_b, ffn1_w_in, ffn1_w_out, ffn2_w_in, ffn2_w_out, conv_w_in, conv_w, conv_w_out,
                           kv_norm_g, kv_ada_w, kv_ada_b, w_kv, attn_w_q, attn_w_o, final_norm_g)))
    mom = dict(zip(names, (m_norm_g, m_ada_w, m_ada_b, m_ffn1_w_in, m_ffn1_w_out, m_ffn2_w_in, m_ffn2_w_out, m_conv_w_in,
                           m_conv_w, m_conv_w_out, m_kv_norm_g, m_kv_ada_w, m_kv_ada_b, m_w_kv, m_attn_w_q, m_attn_w_o,
                           m_final_norm_g)))
    var = dict(zip(names, (v_norm_g, v_ada_w, v_ada_b, v_ffn1_w_in, v_ffn1_w_out, v_ffn2_w_in, v_ffn2_w_out, v_conv_w_in,
                           v_conv_w, v_conv_w_out, v_kv_norm_g, v_kv_ada_w, v_kv_ada_b, v_w_kv, v_attn_w_q, v_attn_w_o,
                           v_final_norm_g)))
    T, D = x.shape[1], x.shape[2]
    me = _my_id()
    nmod = ada_w.shape[2]
    nkv = kv_ada_w.shape[1]

    gathered = all_gather([_flat2(wts[n]).astype(BF16) for n in _BIG], pl.ANY, "gather_weights")
    W = {}
    for n, g in zip(_BIG, gathered):
        shp = wts[n].shape
        if n in _COL_SHARDED:
            W[n] = _cols_to_natural(g, shp[:-1])
        else:
            W[n] = _rows_to_natural(g, shp[:-2], shp[-2])

    ds = norm_g.shape[2]
    small = jnp.concatenate([c.reshape(-1), norm_g.reshape(-1), conv_w.reshape(-1)]).astype(F32)
    n_small = small.shape[0]
    small = _pad_rows(small.reshape(-1, 1), 8 * LANES).reshape(-1, LANES)
    (small_all,) = all_gather([small], pltpu.VMEM, "gather_small")
    small_all = small_all.reshape(N_DEV, -1)[:, :n_small]
    c_all = small_all[:, :D]
    W["norm_g"] = small_all[:, D:D + DEPTH * 3 * ds].reshape(N_DEV, DEPTH, 3, ds).transpose(1, 2, 0, 3).reshape(DEPTH, 3, D)
    W["conv_w"] = small_all[:, D + DEPTH * 3 * ds:].reshape(N_DEV, N_A_LAYERS, 3, ds).transpose(1, 2, 0, 3).reshape(N_A_LAYERS, 3, D)
    W["kv_norm_g"], W["final_norm_g"] = kv_norm_g, final_norm_g

    ada_b_mine = lax.dynamic_slice_in_dim(ada_b, me * nmod, nmod, axis=1).reshape(DEPTH, 1, nmod)
    kv_b_mine = lax.dynamic_slice_in_dim(kv_ada_b, me * nkv, nkv, axis=0).reshape(1, 1, nkv)
    mods_cols = mods_project(c_all, ada_w, ada_b_mine)
    kv_cols = mods_project(c_all, kv_ada_w.reshape(1, D, nkv), kv_b_mine)
    mcat = jnp.concatenate([mods_cols.transpose(1, 0, 2).reshape(N_DEV, DEPTH * nmod), kv_cols[0]], axis=1)
    wm = mcat.shape[1]
    mcat = _pad_rows(mcat.T, LANES).T if wm % LANES else mcat
    (mods_all,) = exchange_slots([mcat.reshape(N_DEV, 1, -1)], "exchange_mods")
    mods_all = mods_all.reshape(N_DEV, -1)
    mods = mods_all[:, :DEPTH * nmod].reshape(N_DEV, DEPTH, nmod).transpose(1, 0, 2).reshape(DEPTH, N_MOD, D)
    kvmods = mods_all[:, DEPTH * nmod:DEPTH * nmod + nkv].reshape(2, D)

    loss_local, dx, grads = device_step(x[0], positions[0], loss_target[0], mods, kvmods, W)
    loss = lax.psum(loss_local, MESH_AXES)

    dmods = grads["mods"].reshape(-1)
    dkvm = grads["kv_rows"][1:3].reshape(-1)
    vecs = jnp.concatenate([dmods, dkvm, grads["kv_rows"][0], grads["final_norm_g"], grads["norm_g"].reshape(-1),
                            grads["conv_w"].reshape(-1)])
    n_vec = vecs.shape[0]
    vecs = _pad_rows(vecs.reshape(-1, 1), 8 * LANES).reshape(-1, LANES)
    (vec_all,) = all_gather([vecs], pltpu.VMEM, "gather_vector_grads")
    vec_all = vec_all.reshape(N_DEV, -1)[:, :n_vec]
    nm_, nk_ = DEPTH * N_MOD * D, 2 * D
    dmods_all = vec_all[:, :nm_].reshape(N_DEV, DEPTH, N_MOD * D)
    dkvm_all = vec_all[:, nm_:nm_ + nk_]
    rest = vec_all[:, nm_ + nk_:]
    parts_kv_norm, parts_final = rest[:, :D].reshape(N_DEV, 1, D), rest[:, D:2 * D].reshape(N_DEV, 1, D)
    parts_norm = lax.dynamic_slice_in_dim(rest[:, 2 * D:2 * D + DEPTH * 3 * D].reshape(N_DEV, DEPTH * 3, D), me * ds, ds, axis=2)
    parts_conv = lax.dynamic_slice_in_dim(rest[:, 2 * D + DEPTH * 3 * D:].reshape(N_DEV, N_A_LAYERS * 3, D), me * ds, ds, axis=2)
    dm_mine = lax.dynamic_slice_in_dim(dmods_all, me * nmod, nmod, axis=2).transpose(1, 0, 2)
    dkv_mine = lax.dynamic_slice_in_dim(dkvm_all, me * nkv, nkv, axis=1).reshape(1, N_DEV, nkv)
    g_ada_w = mods_weight_grad(c_all, dm_mine)
    g_kv_ada_w = mods_weight_grad(c_all, dkv_mine)[0]

    slots = []
    for n in _BIG:
        nl = wts[n].ndim - 2
        slots.append((_cols_to_slots(grads[n]) if n in _COL_SHARDED else _rows_to_slots(grads[n], nl)).astype(BF16))
    received = exchange_slots(slots, "exchange_weight_grads")

    out_g, out_d, out_m, out_v = {}, {}, {}, {}

    def update(n, g, w, parts=False):
        shp = w.shape
        w2 = w.reshape(1, -1) if w.ndim == 1 else _flat2(w)
        g2 = g if parts else g.reshape(w2.shape)
        res = adam_update(g2, w2, mom[n].reshape(w2.shape), var[n].reshape(w2.shape), parts, "adam_" + n)
        out_g[n], out_d[n], out_m[n], out_v[n] = (r.reshape(shp) for r in res)

    for n, r in zip(_BIG, received):
        update(n, r, wts[n], True)
    update("ada_w", g_ada_w, ada_w)
    update("kv_ada_w", g_kv_ada_w, kv_ada_w)
    update("ada_b", dmods_all, ada_b, True)
    update("kv_ada_b", dkvm_all.reshape(N_DEV, 1, nk_), kv_ada_b, True)
    update("kv_norm_g", parts_kv_norm, kv_norm_g, True)
    update("final_norm_g", parts_final, final_norm_g, True)
    update("norm_g", parts_norm, norm_g, True)
    update("conv_w", parts_conv, conv_w, True)

    return (loss, dx.reshape(x.shape), *[out_g[n] for n in names], *[out_d[n] for n in names],
            *[out_m[n] for n in names], *[out_v[n] for n in names])
```

```python
import functools

import jax
import jax.numpy as jnp
from jax import lax
from jax.experimental import pallas as pl
from jax.experimental.pallas import tpu as pltpu

F32, BF16 = jnp.float32, jnp.bfloat16

N_DEV = 8
MESH_AXES = ("x", "y", "c")
DEPTH = 4
N_A_LAYERS = 2
HEAD_DIM = 64
HEADS_PER_GROUP = 8
GROUP_WIDTH = HEAD_DIM * HEADS_PER_GROUP
DILATED_GROUPS = ((128, 1), (512, 4), (2048, 16))
ROPE_DIM = HEAD_DIM // 4
ROPE_THETA = 500000.0
NORM_EPS = 1e-5
FFN_RES_WEIGHT = 0.5
N_MOD = 9
ADAM_LR, ADAM_B1, ADAM_B2, ADAM_EPS, ADAM_WD, ADAM_STEP = 0.001, 0.9, 0.999, 1e-08, 0.01, 10

LANES = 128
TOKEN_TILE = 512
VMEM_LIMIT = 56 * 1024 * 1024
MESH = pl.DeviceIdType.MESH


def _cp(*sem):
    return pltpu.CompilerParams(dimension_semantics=sem, vmem_limit_bytes=VMEM_LIMIT)


def _pick(n, cap, mult=LANES):
    if n <= cap:
        return n
    best = None
    for t in range(mult, cap + 1, mult):
        if n % t == 0:
            best = t
    assert best is not None, (n, cap)
    return best


def _tok(tm, w):
    return pl.BlockSpec((tm, w), lambda i: (i, 0))


def _res(shape):
    nd = len(shape)
    return pl.BlockSpec(shape, lambda *_: (0,) * nd, pipeline_mode=pl.Buffered(1))


def _sds(shape, dt):
    return jax.ShapeDtypeStruct(shape, dt)


def _sigmoid(a):
    return 1.0 / (1.0 + jnp.exp(-a))


def _modnorm(x, g, sh, sc):
    r = lax.rsqrt(jnp.mean(x * x, axis=-1, keepdims=True) + NORM_EPS)
    return (x * r * g) * (1.0 + sc) + sh


def _dot(a, b):
    return jnp.dot(a, b, preferred_element_type=F32)


def _dot_nt(a, b):
    return lax.dot_general(a, b, (((1,), (1,)), ((), ())), preferred_element_type=F32)


def _dot_tn(a, b):
    return lax.dot_general(a, b, (((0,), (0,)), ((), ())), preferred_element_type=F32)


def _rows8(rows, d):
    pad = 8 - len(rows)
    return jnp.concatenate(list(rows) + [jnp.zeros((pad, d), F32)], axis=0)


def _acc_rows(ref, tile, first):
    @pl.when(first)
    def _():
        ref[...] = tile

    @pl.when(jnp.logical_not(first))
    def _():
        ref[...] += tile


def ffn_up(x, vec, w_in):
    T, D = x.shape
    F = w_in.shape[1] // 2
    tm, tf = min(TOKEN_TILE, T), _pick(F, 1408)
    nj = F // tf

    def body(x_ref, vec_ref, wa_ref, wb_ref, h_ref, a_ref, b_ref, u_ref, hs):
        @pl.when(pl.program_id(1) == 0)
        def _():
            hb = _modnorm(x_ref[...], vec_ref[0:1], vec_ref[1:2], vec_ref[2:3]).astype(BF16)
            hs[...] = hb
            h_ref[...] = hb

        h = hs[...]
        a = _dot(h, wa_ref[...])
        b = _dot(h, wb_ref[...])
        a_ref[...] = a.astype(BF16)
        b_ref[...] = b.astype(BF16)
        u_ref[...] = (a * _sigmoid(a) * b).astype(BF16)

    row = pl.BlockSpec((tm, D), lambda i, j: (i, 0))
    til = pl.BlockSpec((tm, tf), lambda i, j: (i, j))
    return pl.pallas_call(
        body, grid=(T // tm, nj),
        in_specs=[row, pl.BlockSpec((8, D), lambda i, j: (0, 0)),
                  pl.BlockSpec((D, tf), lambda i, j: (0, j)), pl.BlockSpec((D, tf), lambda i, j: (0, nj + j))],
        out_specs=[row, til, til, til],
        out_shape=[_sds((T, D), BF16), _sds((T, F), BF16), _sds((T, F), BF16), _sds((T, F), BF16)],
        scratch_shapes=[pltpu.VMEM((tm, D), BF16)],
        compiler_params=_cp("arbitrary", "arbitrary"), name="ffn_up")(x, vec, w_in, w_in)


def proj_out(u, x, vec, w_out, res_weight, name):
    T, D = x.shape
    K = u.shape[1]
    tm = min(TOKEN_TILE, T)

    def body(u_ref, x_ref, vec_ref, w_ref, xn_ref, y_ref):
        y = _dot(u_ref[...], w_ref[...])
        y_ref[...] = y.astype(BF16)
        xn_ref[...] = x_ref[...] + (res_weight * (1.0 + vec_ref[3:4])) * y

    return pl.pallas_call(
        body, grid=(T // tm,),
        in_specs=[_tok(tm, K), _tok(tm, D), _res((8, D)), _res((K, D))],
        out_specs=[_tok(tm, D), _tok(tm, D)],
        out_shape=[_sds((T, D), F32), _sds((T, D), BF16)],
        compiler_params=_cp("arbitrary"), name=name)(u, x, vec, w_out)


def ffn_down_bwd(dxo, y, vec, w_out, a, b):
    T, D = dxo.shape
    F = a.shape[1]
    tm, tf = min(TOKEN_TILE, T), _pick(F, 1408)

    def body(dxo_ref, y_ref, vec_ref, w_ref, a_ref, b_ref, dy_ref, da_ref, db_ref, part_ref, dys):
        i, j = pl.program_id(0), pl.program_id(1)

        @pl.when(j == 0)
        def _():
            dxo_t = dxo_ref[...]
            dyb = (dxo_t * (FFN_RES_WEIGHT * (1.0 + vec_ref[3:4]))).astype(BF16)
            dys[...] = dyb
            dy_ref[...] = dyb
            dgate = FFN_RES_WEIGHT * jnp.sum(dxo_t * y_ref[...].astype(F32), axis=0, keepdims=True)
            _acc_rows(part_ref, _rows8([dgate], D), i == 0)

        du = _dot_nt(dys[...], w_ref[...])
        av = a_ref[...].astype(F32)
        bv = b_ref[...].astype(F32)
        sg = _sigmoid(av)
        da_ref[...] = (du * bv * (sg * (1.0 + av * (1.0 - sg)))).astype(BF16)
        db_ref[...] = (du * (av * sg)).astype(BF16)

    row = pl.BlockSpec((tm, D), lambda i, j: (i, 0))
    til = pl.BlockSpec((tm, tf), lambda i, j: (i, j))
    return pl.pallas_call(
        body, grid=(T // tm, F // tf),
        in_specs=[row, row, pl.BlockSpec((8, D), lambda i, j: (0, 0)), pl.BlockSpec((tf, D), lambda i, j: (j, 0)), til, til],
        out_specs=[row, til, til, pl.BlockSpec((8, D), lambda i, j: (0, 0))],
        out_shape=[_sds((T, D), BF16), _sds((T, F), BF16), _sds((T, F), BF16), _sds((8, D), F32)],
        scratch_shapes=[pltpu.VMEM((tm, D), BF16)],
        compiler_params=_cp("arbitrary", "arbitrary"), name="ffn_down_bwd")(dxo, y, vec, w_out, a, b)


def ffn_up_bwd(da, db, w_in, x, dxo, vec):
    T, D = x.shape
    F = da.shape[1]
    tm, tf = min(TOKEN_TILE, T), _pick(F, 1408)
    nj = F // tf

    def body(da_ref, db_ref, wa_ref, wb_ref, x_ref, dxo_ref, vec_ref, dx_ref, part_ref, acc):
        i, j = pl.program_id(0), pl.program_id(1)
        t = _dot_nt(da_ref[...], wa_ref[...]) + _dot_nt(db_ref[...], wb_ref[...])

        @pl.when(j == 0)
        def _():
            acc[...] = t

        @pl.when(j > 0)
        def _():
            acc[...] += t

        @pl.when(j == nj - 1)
        def _():
            _, vjp = jax.vjp(_modnorm, x_ref[...], vec_ref[0:1], vec_ref[1:2], vec_ref[2:3])
            dx, dg, dsh, dsc = vjp(acc[...])
            dx_ref[...] = dxo_ref[...] + dx
            _acc_rows(part_ref, _rows8([dg, dsh, dsc], D), i == 0)

    row = pl.BlockSpec((tm, D), lambda i, j: (i, 0))
    til = pl.BlockSpec((tm, tf), lambda i, j: (i, j))
    return pl.pallas_call(
        body, grid=(T // tm, nj),
        in_specs=[til, til, pl.BlockSpec((D, tf), lambda i, j: (0, j)), pl.BlockSpec((D, tf), lambda i, j: (0, nj + j)),
                  row, row, pl.BlockSpec((8, D), lambda i, j: (0, 0))],
        out_specs=[row, pl.BlockSpec((8, D), lambda i, j: (0, 0))],
        out_shape=[_sds((T, D), F32), _sds((8, D), F32)],
        scratch_shapes=[pltpu.VMEM((tm, D), F32)],
        compiler_params=_cp("arbitrary", "arbitrary"), name="ffn_up_bwd")(da, db, w_in, w_in, x, dxo, vec)


def mm_tn(a, b, name):
    T, M = a.shape
    N = b.shape[1]
    tk = min(TOKEN_TILE, T)
    tmm, tn = _pick(M, 1408), _pick(N, 1536)

    def body(a_ref, b_ref, o_ref):
        t = _dot_tn(a_ref[...], b_ref[...])

        @pl.when(pl.program_id(2) == 0)
        def _():
            o_ref[...] = t

        @pl.when(pl.program_id(2) > 0)
        def _():
            o_ref[...] += t

    return pl.pallas_call(
        body, grid=(M // tmm, N // tn, T // tk),
        in_specs=[pl.BlockSpec((tk, tmm), lambda i, j, k: (k, i)), pl.BlockSpec((tk, tn), lambda i, j, k: (k, j))],
        out_specs=pl.BlockSpec((tmm, tn), lambda i, j, k: (i, j)),
        out_shape=_sds((M, N), F32),
        compiler_params=_cp("arbitrary", "arbitrary", "arbitrary"), name=name)(a, b)


def conv_fwd(x, vec, cw, w_in, w_out):
    T, D = x.shape
    tm = min(TOKEN_TILE, T)

    def body(x_ref, vec_ref, cw_ref, wi_ref, wo_ref, xn_ref, h_ref, bcu_ref, cv_ref, z_ref, y_ref, vbuf):
        @pl.when(pl.program_id(0) == 0)
        def _():
            vbuf[0:8, :] = jnp.zeros((8, D), F32)

        x_t = x_ref[...]
        hb = _modnorm(x_t, vec_ref[0:1], vec_ref[1:2], vec_ref[2:3]).astype(BF16)
        h_ref[...] = hb
        bcu = _dot(hb, wi_ref[...])
        bcu_ref[...] = bcu.astype(BF16)
        bg, v = bcu[:, 0:D], bcu[:, D:2 * D] * bcu[:, 2 * D:3 * D]
        vbuf[8:8 + tm, :] = v
        conv = cw_ref[0:1] * vbuf[6:6 + tm, :] + cw_ref[1:2] * vbuf[7:7 + tm, :] + cw_ref[2:3] * v
        cv_ref[...] = conv.astype(BF16)
        zb = (bg * conv).astype(BF16)
        z_ref[...] = zb
        y = _dot(zb, wo_ref[...])
        y_ref[...] = y.astype(BF16)
        xn_ref[...] = x_t + (1.0 + vec_ref[3:4]) * y
        vbuf[0:8, :] = vbuf[tm:tm + 8, :]

    return pl.pallas_call(
        body, grid=(T // tm,),
        in_specs=[_tok(tm, D), _res((8, D)), _res((8, D)), _res((D, 3 * D)), _res((D, D))],
        out_specs=[_tok(tm, D), _tok(tm, D), _tok(tm, 3 * D), _tok(tm, D), _tok(tm, D), _tok(tm, D)],
        out_shape=[_sds((T, D), F32), _sds((T, D), BF16), _sds((T, 3 * D), BF16), _sds((T, D), BF16),
                   _sds((T, D), BF16), _sds((T, D), BF16)],
        scratch_shapes=[pltpu.VMEM((tm + 8, D), F32)],
        compiler_params=_cp("arbitrary"), name="conv_fwd")(x, vec, cw, w_in, w_out)


def conv_bwd(dxo, x, y, bcu, cv, vec, cw, w_in, w_out):
    T, D = x.shape
    tm = min(TOKEN_TILE, T)
    nt = T // tm

    def body(dxo_ref, x_ref, y_ref, bcu_ref, cv_ref, vec_ref, cw_ref, wi_ref, wo_ref,
             dx_ref, dy_ref, dbcu_ref, part_ref, dcw_ref, dcbuf):
        first = pl.program_id(0) == 0

        @pl.when(first)
        def _():
            dcbuf[tm:tm + 8, :] = jnp.zeros((8, D), F32)

        dxo_t = dxo_ref[...]
        dyb = (dxo_t * (1.0 + vec_ref[3:4])).astype(BF16)
        dy_ref[...] = dyb
        dgate = jnp.sum(dxo_t * y_ref[...].astype(F32), axis=0, keepdims=True)
        dz = _dot_nt(dyb, wo_ref[...])
        bcu_t = bcu_ref[...].astype(F32)
        bg, cg, ug = bcu_t[:, 0:D], bcu_t[:, D:2 * D], bcu_t[:, 2 * D:3 * D]
        dconv = dz * bg
        dbg = dz * cv_ref[...].astype(F32)
        dcbuf[0:tm, :] = dconv
        d1, d2 = dcbuf[1:tm + 1, :], dcbuf[2:tm + 2, :]
        dv = cw_ref[2:3] * dconv + cw_ref[1:2] * d1 + cw_ref[0:1] * d2
        v = cg * ug
        dcw = _rows8([jnp.sum(d2 * v, axis=0, keepdims=True), jnp.sum(d1 * v, axis=0, keepdims=True),
                      jnp.sum(dconv * v, axis=0, keepdims=True)], D)
        dbcu = jnp.concatenate([dbg, dv * ug, dv * cg], axis=1).astype(BF16)
        dbcu_ref[...] = dbcu
        dh = _dot_nt(dbcu, wi_ref[...])
        _, vjp = jax.vjp(_modnorm, x_ref[...], vec_ref[0:1], vec_ref[1:2], vec_ref[2:3])
        dx, dg, dsh, dsc = vjp(dh)
        dx_ref[...] = dxo_t + dx
        _acc_rows(part_ref, _rows8([dg, dsh, dsc, dgate], D), first)
        _acc_rows(dcw_ref, dcw, first)
        dcbuf[tm:tm + 8, :] = dcbuf[0:8, :]

    def rev(w):
        return pl.BlockSpec((tm, w), lambda i: (nt - 1 - i, 0))

    return pl.pallas_call(
        body, grid=(nt,),
        in_specs=[rev(D), rev(D), rev(D), rev(3 * D), rev(D), _res((8, D)), _res((8, D)), _res((D, 3 * D)), _res((D, D))],
        out_specs=[rev(D), rev(D), rev(3 * D), pl.BlockSpec((8, D), lambda i: (0, 0)), pl.BlockSpec((8, D), lambda i: (0, 0))],
        out_shape=[_sds((T, D), F32), _sds((T, D), BF16), _sds((T, 3 * D), BF16), _sds((8, D), F32), _sds((8, D), F32)],
        scratch_shapes=[pltpu.VMEM((tm + 8, D), F32)],
        compiler_params=_cp("arbitrary"), name="conv_bwd")(dxo, x, y, bcu, cv, vec, cw, w_in, w_out)


def rope_tables(pos, lane_rows):
    T = pos.shape[0]
    tm = min(TOKEN_TILE, T)

    def body(p_ref, lr_ref, c_ref, sp_ref, sm_ref):
        ang = p_ref[...].astype(F32) * lr_ref[0:1]
        cs, sn = jnp.cos(ang), jnp.sin(ang)
        c_ref[...] = jnp.where(lr_ref[1:2] > 0.5, cs, 1.0)
        sp_ref[...] = jnp.where(lr_ref[2:3] > 0.5, sn, 0.0)
        sm_ref[...] = jnp.where(lr_ref[3:4] > 0.5, -sn, 0.0)

    return pl.pallas_call(
        body, grid=(T // tm,),
        in_specs=[_tok(tm, 1), _res((8, LANES))],
        out_specs=[_tok(tm, LANES)] * 3,
        out_shape=[_sds((T, LANES), F32)] * 3,
        compiler_params=_cp("arbitrary"), name="rope_tables")(pos, lane_rows)


def _rope(t, c, sp, sm):
    w = t.shape[1]
    reps = w // LANES
    cf, spf, smf = jnp.tile(c, (1, reps)), jnp.tile(sp, (1, reps)), jnp.tile(sm, (1, reps))
    half = ROPE_DIM // 2
    return t * cf + pltpu.roll(t, half, axis=1) * spf + pltpu.roll(t, w - half, axis=1) * smf


def _rope_t(d, c, sp, sm):
    w = d.shape[1]
    reps = w // LANES
    cf, spf, smf = jnp.tile(c, (1, reps)), jnp.tile(sp, (1, reps)), jnp.tile(sm, (1, reps))
    half = ROPE_DIM // 2
    return d * cf + pltpu.roll(d * spf, w - half, axis=1) + pltpu.roll(d * smf, half, axis=1)


def proj_rope_fwd(x, vec, w, tabs, n_rope, name):
    T, D = x.shape
    N = w.shape[1]
    tm = min(TOKEN_TILE, T)

    def body(x_ref, vec_ref, w_ref, c_ref, sp_ref, sm_ref, h_ref, p_ref):
        hb = _modnorm(x_ref[...], vec_ref[0:1], vec_ref[1:2], vec_ref[2:3]).astype(BF16)
        h_ref[...] = hb
        p = _dot(hb, w_ref[...])
        pr = _rope(p[:, 0:n_rope], c_ref[...], sp_ref[...], sm_ref[...])
        if n_rope < N:
            pr = jnp.concatenate([pr, p[:, n_rope:N]], axis=1)
        p_ref[...] = pr.astype(BF16)

    return pl.pallas_call(
        body, grid=(T // tm,),
        in_specs=[_tok(tm, D), _res((8, D)), _res((D, N))] + [_tok(tm, LANES)] * 3,
        out_specs=[_tok(tm, D), _tok(tm, N)],
        out_shape=[_sds((T, D), BF16), _sds((T, N), BF16)],
        compiler_params=_cp("arbitrary"), name=name)(x, vec, w, *tabs)


def proj_rope_bwd(dparts, x, dxo, vec, w, tabs, n_rope, name):
    T, D = x.shape
    N = w.shape[1]
    tm = min(TOKEN_TILE, T)
    npart = len(dparts)

    def body(*refs):
        d_refs = refs[:npart]
        x_ref, dxo_ref, vec_ref, w_ref, c_ref, sp_ref, sm_ref, dx_ref, dp_ref, part_ref = refs[npart:]
        d = jnp.concatenate([r[...].astype(F32) for r in d_refs], axis=1)
        dr = _rope_t(d[:, 0:n_rope], c_ref[...], sp_ref[...], sm_ref[...])
        if n_rope < N:
            dr = jnp.concatenate([dr, d[:, n_rope:N]], axis=1)
        dpb = dr.astype(BF16)
        dp_ref[...] = dpb
        dh = _dot_nt(dpb, w_ref[...])
        _, vjp = jax.vjp(_modnorm, x_ref[...], vec_ref[0:1], vec_ref[1:2], vec_ref[2:3])
        dx, dg, dsh, dsc = vjp(dh)
        dx_ref[...] = dxo_ref[...] + dx
        _acc_rows(part_ref, _rows8([dg, dsh, dsc], D), pl.program_id(0) == 0)

    return pl.pallas_call(
        body, grid=(T // tm,),
        in_specs=[_tok(tm, p.shape[1]) for p in dparts] + [_tok(tm, D), _tok(tm, D), _res((8, D)), _res((D, N))]
        + [_tok(tm, LANES)] * 3,
        out_specs=[_tok(tm, D), _tok(tm, N), pl.BlockSpec((8, D), lambda i: (0, 0))],
        out_shape=[_sds((T, D), F32), _sds((T, N), BF16), _sds((8, D), F32)],
        compiler_params=_cp("arbitrary"), name=name)(*dparts, x, dxo, vec, w, *tabs)


def _valid_mask(n, i):
    qi = lax.broadcasted_iota(jnp.int32, (n, 2 * n), 0)
    kj = lax.broadcasted_iota(jnp.int32, (n, 2 * n), 1)
    dist = n + qi - kj
    return (dist >= 0) & (dist <= n) & ((kj >= n) | (i > 0))


def attn_core_fwd(q, k, v, g, n, d):
    T, QW = q.shape
    GW = GROUP_WIDTH
    ng = QW // GW
    M = T // d
    scale = HEAD_DIM ** -0.5

    def body(q_ref, kp_ref, kc_ref, vp_ref, vc_ref, o_ref, l_ref):
        valid = _valid_mask(n, pl.program_id(1))
        qv = q_ref[...]
        kk = jnp.concatenate([kp_ref[...], kc_ref[...]], axis=0)
        vv = jnp.concatenate([vp_ref[...], vc_ref[...]], axis=0)
        for h in range(HEADS_PER_GROUP):
            hs = slice(HEAD_DIM * h, HEAD_DIM * (h + 1))
            s = jnp.where(valid, _dot_nt(qv[:, hs], kk[:, hs]) * scale, -1e30)
            m = jnp.max(s, axis=1, keepdims=True)
            p = jnp.exp(s - m)
            den = jnp.sum(p, axis=1, keepdims=True)
            o_ref[:, hs] = _dot((p / den).astype(BF16), vv[:, hs])
            l_ref[:, hs] = jnp.broadcast_to(m + jnp.log(den), (n, HEAD_DIM))

    cur = pl.BlockSpec((n, GW), lambda r, i: (i, r * ng + g))
    prv = pl.BlockSpec((n, GW), lambda r, i: (jnp.maximum(i - 1, 0), r * ng + g))
    out = pl.BlockSpec((n, GW), lambda r, i: (i, r))
    qv, kv, vv = q.reshape(M, d * QW), k.reshape(M, d * QW), v.reshape(M, d * QW)
    o, l = pl.pallas_call(
        body, grid=(d, M // n),
        in_specs=[cur, prv, cur, prv, cur], out_specs=[out, out],
        out_shape=[_sds((M, d * GW), F32), _sds((M, d * GW), F32)],
        compiler_params=_cp("arbitrary", "arbitrary"), name=f"attn_fwd_g{g}")(qv, kv, kv, vv, vv)
    return o.reshape(T, GW), l.reshape(T, GW)


def attn_core_bwd(q, k, v, do, rr, lse, g, n, d):
    T, QW = q.shape
    GW = GROUP_WIDTH
    ng = QW // GW
    M = T // d
    scale = HEAD_DIM ** -0.5

    def body(q_ref, kp_ref, kc_ref, vp_ref, vc_ref, do_ref, r_ref, l_ref, dq_ref, dkc_ref, dkp_ref, dvc_ref, dvp_ref):
        valid = _valid_mask(n, pl.program_id(1))
        qv, dov = q_ref[...], do_ref[...]
        kk = jnp.concatenate([kp_ref[...], kc_ref[...]], axis=0)
        vv = jnp.concatenate([vp_ref[...], vc_ref[...]], axis=0)
        for h in range(HEADS_PER_GROUP):
            hs = slice(HEAD_DIM * h, HEAD_DIM * (h + 1))
            s = jnp.where(valid, _dot_nt(qv[:, hs], kk[:, hs]) * scale, -1e30)
            p = jnp.exp(s - l_ref[:, HEAD_DIM * h:HEAD_DIM * h + 1])
            dp = _dot_nt(dov[:, hs], vv[:, hs])
            delta = jnp.sum(r_ref[:, hs], axis=1, keepdims=True)
            ds = (p * (dp - delta) * scale).astype(BF16)
            dq_ref[:, hs] = _dot(ds, kk[:, hs]).astype(BF16)
            dk = _dot_tn(ds, qv[:, hs]).astype(BF16)
            dv = _dot_tn(p.astype(BF16), dov[:, hs]).astype(BF16)
            dkp_ref[:, hs], dkc_ref[:, hs] = dk[0:n], dk[n:2 * n]
            dvp_ref[:, hs], dvc_ref[:, hs] = dv[0:n], dv[n:2 * n]

    cur = pl.BlockSpec((n, GW), lambda r, i: (i, r * ng + g))
    prv = pl.BlockSpec((n, GW), lambda r, i: (jnp.maximum(i - 1, 0), r * ng + g))
    blk = pl.BlockSpec((n, GW), lambda r, i: (i, r))
    qv, kv, vv = q.reshape(M, d * QW), k.reshape(M, d * QW), v.reshape(M, d * QW)
    outs = pl.pallas_call(
        body, grid=(d, M // n),
        in_specs=[cur, prv, cur, prv, cur, blk, blk, blk], out_specs=[blk] * 5,
        out_shape=[_sds((M, d * GW), BF16)] * 5,
        compiler_params=_cp("arbitrary", "arbitrary"), name=f"attn_bwd_g{g}")(
            qv, kv, kv, vv, vv, do.reshape(M, d * GW), rr.reshape(M, d * GW), lse.reshape(M, d * GW))
    return [o.reshape(T, GW) for o in outs]


def dkv_combine(cur_prev, n, d, name):
    T, GW = cur_prev[0][0].shape
    M = T // d
    nb = M // n
    flat = [a.reshape(M, d * GW) for pair in cur_prev for a in pair]

    def body(*refs):
        o_ref = refs[-1]
        last = pl.program_id(1) == nb - 1
        acc = jnp.zeros((n, GW), F32)
        for t in range(0, len(refs) - 1, 2):
            acc = acc + refs[t][...].astype(F32) + jnp.where(last, 0.0, refs[t + 1][...].astype(F32))
        o_ref[...] = acc.astype(BF16)

    cur = pl.BlockSpec((n, GW), lambda r, i: (i, r))
    nxt = pl.BlockSpec((n, GW), lambda r, i: (jnp.minimum(i + 1, nb - 1), r))
    out = pl.pallas_call(
        body, grid=(d, nb), in_specs=[cur, nxt] * len(cur_prev), out_specs=cur,
        out_shape=_sds((M, d * GW), BF16),
        compiler_params=_cp("arbitrary", "arbitrary"), name=name)(*flat)
    return out.reshape(T, GW)


def _group_weights(ls):
    mx = functools.reduce(jnp.maximum, ls)
    es = [jnp.exp(l - mx) for l in ls]
    tot = functools.reduce(lambda a, b: a + b, es)
    return [e / tot for e in es]


def attn_mix_out(os_, ls, x, vec, w_o):
    T, D = x.shape
    GW = GROUP_WIDTH
    tm = min(TOKEN_TILE, T)
    ng = len(os_)

    def body(*refs):
        o_refs, l_refs = refs[:ng], refs[ng:2 * ng]
        x_ref, vec_ref, w_ref, xn_ref, mix_ref, y_ref = refs[2 * ng:]
        ws = _group_weights([r[...] for r in l_refs])
        mixed = functools.reduce(lambda a, b: a + b, [w * r[...] for w, r in zip(ws, o_refs)])
        mb = mixed.astype(BF16)
        mix_ref[...] = mb
        y = _dot(mb, w_ref[...])
        y_ref[...] = y.astype(BF16)
        xn_ref[...] = x_ref[...] + (1.0 + vec_ref[3:4]) * y

    return pl.pallas_call(
        body, grid=(T // tm,),
        in_specs=[_tok(tm, GW)] * (2 * ng) + [_tok(tm, D), _res((8, D)), _res((GW, D))],
        out_specs=[_tok(tm, D), _tok(tm, GW), _tok(tm, D)],
        out_shape=[_sds((T, D), F32), _sds((T, GW), BF16), _sds((T, D), BF16)],
        compiler_params=_cp("arbitrary"), name="attn_mix_out")(*os_, *ls, x, vec, w_o)


def attn_mix_bwd(dxo, y, vec, w_o, os_, ls):
    T, D = dxo.shape
    GW = GROUP_WIDTH
    tm = min(TOKEN_TILE, T)
    ng = len(os_)

    def body(*refs):
        dxo_ref, y_ref, vec_ref, w_ref = refs[:4]
        o_refs, l_refs = refs[4:4 + ng], refs[4 + ng:4 + 2 * ng]
        dy_ref = refs[4 + 2 * ng]
        do_refs = refs[5 + 2 * ng:5 + 3 * ng]
        r_refs = refs[5 + 3 * ng:5 + 4 * ng]
        part_ref = refs[5 + 4 * ng]
        dxo_t = dxo_ref[...]
        dyb = (dxo_t * (1.0 + vec_ref[3:4])).astype(BF16)
        dy_ref[...] = dyb
        dgate = jnp.sum(dxo_t * y_ref[...].astype(F32), axis=0, keepdims=True)
        _acc_rows(part_ref, _rows8([dgate], D), pl.program_id(0) == 0)
        dmix = _dot_nt(dyb, w_ref[...])
        ws = _group_weights([r[...] for r in l_refs])
        ov = [r[...] for r in o_refs]
        mixed = functools.reduce(lambda a, b: a + b, [w * o for w, o in zip(ws, ov)])
        for gi in range(ng):
            do = ws[gi] * dmix
            do_refs[gi][...] = do.astype(BF16)
            r_refs[gi][...] = do * mixed

    return pl.pallas_call(
        body, grid=(T // tm,),
        in_specs=[_tok(tm, D), _tok(tm, D), _res((8, D)), _res((GW, D))] + [_tok(tm, GW)] * (2 * ng),
        out_specs=[_tok(tm, D)] + [_tok(tm, GW)] * (2 * ng) + [pl.BlockSpec((8, D), lambda i: (0, 0))],
        out_shape=[_sds((T, D), BF16)] + [_sds((T, GW), BF16)] * ng + [_sds((T, GW), F32)] * ng + [_sds((8, D), F32)],
        compiler_params=_cp("arbitrary"), name="attn_mix_bwd")(dxo, y, vec, w_o, *os_, *ls)


def final_loss(x, gvec, target):
    T, D = x.shape
    tm = min(TOKEN_TILE, T)

    def norm(xv, g):
        return xv * lax.rsqrt(jnp.mean(xv * xv, axis=-1, keepdims=True) + NORM_EPS) * g

    def body(x_ref, g_ref, t_ref, dx_ref, part_ref, loss_ref):
        first = pl.program_id(0) == 0
        yv, vjp = jax.vjp(norm, x_ref[...], g_ref[0:1])
        err = yv - t_ref[...]
        dx, dg = vjp(err * (1.0 / D))
        dx_ref[...] = dx
        _acc_rows(part_ref, _rows8([dg], D), first)
        tile_loss = 0.5 * jnp.sum(jnp.sum(err * err, axis=1, keepdims=True) * (1.0 / D), axis=0, keepdims=True)
        _acc_rows(loss_ref, jnp.broadcast_to(tile_loss, (8, LANES)), first)

    return pl.pallas_call(
        body, grid=(T // tm,),
        in_specs=[_tok(tm, D), _res((8, D)), _tok(tm, D)],
        out_specs=[_tok(tm, D), pl.BlockSpec((8, D), lambda i: (0, 0)), pl.BlockSpec((8, LANES), lambda i: (0, 0))],
        out_shape=[_sds((T, D), F32), _sds((8, D), F32), _sds((8, LANES), F32)],
        compiler_params=_cp("arbitrary"), name="final_loss")(x, gvec, target)


def mods_project(c_all, w, b):
    B, D = c_all.shape
    L, _, N = w.shape

    def body(c_ref, w_ref, b_ref, o_ref):
        cv = c_ref[...]
        cond = cv * _sigmoid(cv)
        o_ref[0] = jnp.dot(cond, w_ref[0], preferred_element_type=F32, precision=lax.Precision.HIGHEST) + b_ref[0]

    return pl.pallas_call(
        body, grid=(L,),
        in_specs=[pl.BlockSpec((B, D), lambda l: (0, 0)), pl.BlockSpec((1, D, N), lambda l: (l, 0, 0)),
                  pl.BlockSpec((1, 1, N), lambda l: (l, 0, 0))],
        out_specs=pl.BlockSpec((1, B, N), lambda l: (l, 0, 0)),
        out_shape=_sds((L, B, N), F32),
        compiler_params=_cp("arbitrary"), name="mods_project")(c_all, w, b)


def mods_weight_grad(c_all, dm):
    B, D = c_all.shape
    L, _, N = dm.shape

    def body(c_ref, d_ref, o_ref):
        cv = c_ref[...]
        cond = cv * _sigmoid(cv)
        o_ref[0] = lax.dot_general(cond, d_ref[0], (((0,), (0,)), ((), ())), preferred_element_type=F32,
                                   precision=lax.Precision.HIGHEST)

    return pl.pallas_call(
        body, grid=(L,),
        in_specs=[pl.BlockSpec((B, D), lambda l: (0, 0)), pl.BlockSpec((1, B, N), lambda l: (l, 0, 0))],
        out_specs=pl.BlockSpec((1, D, N), lambda l: (l, 0, 0)),
        out_shape=_sds((L, D, N), F32),
        compiler_params=_cp("arbitrary"), name="mods_weight_grad")(c_all, dm)


def _adam_math(g, w, m, v):
    m2 = ADAM_B1 * m + (1.0 - ADAM_B1) * g
    v2 = ADAM_B2 * v + (1.0 - ADAM_B2) * (g * g)
    m_hat = m2 / (1.0 - ADAM_B1 ** ADAM_STEP)
    v_hat = v2 / (1.0 - ADAM_B2 ** ADAM_STEP)
    delta = -ADAM_LR * (m_hat / (jnp.sqrt(v_hat) + ADAM_EPS) + ADAM_WD * w)
    return delta, m2, v2


def adam_update(g, w, m, v, parts, name):
    R, C = w.shape
    tr = _pick(R, 256, 8)

    def body(g_ref, w_ref, m_ref, v_ref, go_ref, d_ref, mo_ref, vo_ref):
        if parts:
            gv = g_ref[0].astype(F32)
            for s in range(1, N_DEV):
                gv = gv + g_ref[s].astype(F32)
        else:
            gv = g_ref[...]
        go_ref[...] = gv
        d_ref[...], mo_ref[...], vo_ref[...] = _adam_math(gv, w_ref[...], m_ref[...], v_ref[...])

    gspec = pl.BlockSpec((N_DEV, tr, C), lambda i: (0, i, 0)) if parts else _tok(tr, C)
    return pl.pallas_call(
        body, grid=(R // tr,),
        in_specs=[gspec, _tok(tr, C), _tok(tr, C), _tok(tr, C)],
        out_specs=[_tok(tr, C)] * 4, out_shape=[_sds((R, C), F32)] * 4,
        compiler_params=_cp("arbitrary"), name=name)(g, w, m, v)


def adam_update_layers(parts, w, m, v, name):
    L, R, C = w.shape
    tr = _pick(R, 256, 8)
    nb = R // tr

    def body(*refs):
        p_refs = refs[:L]
        w_ref, m_ref, v_ref, go_ref, d_ref, mo_ref, vo_ref = refs[L:]
        for a in range(L):
            @pl.when(pl.program_id(0) == a)
            def _(a=a):
                gv = p_refs[a][0].astype(F32)
                for s in range(1, N_DEV):
                    gv = gv + p_refs[a][s].astype(F32)
                go_ref[...] = gv
                d_ref[...], mo_ref[...], vo_ref[...] = _adam_math(gv, w_ref[...], m_ref[...], v_ref[...])

    def part_spec(a):
        return pl.BlockSpec((N_DEV, tr, C), lambda l, i: (0, jnp.where(l == a, i, jnp.where(l < a, 0, nb - 1)), 0))

    lay = pl.BlockSpec((None, tr, C), lambda l, i: (l, i, 0))
    return pl.pallas_call(
        body, grid=(L, nb),
        in_specs=[part_spec(a) for a in range(L)] + [lay, lay, lay],
        out_specs=[lay] * 4, out_shape=[_sds((L, R, C), F32)] * 4,
        compiler_params=_cp("arbitrary", "arbitrary"), name=name)(*parts, w, m, v)


def _my_id():
    return 4 * lax.axis_index("x") + 2 * lax.axis_index("y") + lax.axis_index("c")


def _peer(s):
    x, y, c = lax.axis_index("x"), lax.axis_index("y"), lax.axis_index("c")
    px = (1 - x) if s & 4 else x
    py = (1 - y) if s & 2 else y
    pc = (1 - c) if s & 1 else c
    return (px, py, pc), 4 * px + 2 * py + pc


def all_gather(xs, space, name):
    na = len(xs)

    def body(*refs):
        x_refs, o_refs = refs[:na], refs[na:2 * na]
        send_sems, recv_sems, local_sems = refs[2 * na:]
        me = _my_id()
        locals_, sends = [], []
        for a in range(na):
            cp = pltpu.make_async_copy(x_refs[a], o_refs[a].at[me], local_sems.at[a])
            cp.start()
            locals_.append(cp)
        for s in range(1, N_DEV):
            peer, _ = _peer(s)
            for a in range(na):
                cp = pltpu.make_async_remote_copy(
                    src_ref=x_refs[a], dst_ref=o_refs[a].at[me], send_sem=send_sems.at[a, s - 1],
                    recv_sem=recv_sems.at[a, s - 1], device_id=peer, device_id_type=MESH)
                cp.start()
                sends.append(cp)
        for s in range(1, N_DEV):
            peer, pid = _peer(s)
            for a in range(na):
                pltpu.make_async_remote_copy(
                    src_ref=x_refs[a], dst_ref=o_refs[a].at[pid], send_sem=send_sems.at[a, s - 1],
                    recv_sem=recv_sems.at[a, s - 1], device_id=peer, device_id_type=MESH).wait_recv()
        for cp in sends:
            cp.wait_send()
        for cp in locals_:
            cp.wait()

    spec = pl.BlockSpec(memory_space=space)
    return pl.pallas_call(
        body, in_specs=[spec] * na, out_specs=[spec] * na,
        out_shape=[_sds((N_DEV,) + x.shape, x.dtype) for x in xs],
        scratch_shapes=[pltpu.SemaphoreType.DMA((na, N_DEV - 1)), pltpu.SemaphoreType.DMA((na, N_DEV - 1)),
                        pltpu.SemaphoreType.DMA((na,))],
        compiler_params=pltpu.CompilerParams(vmem_limit_bytes=VMEM_LIMIT), name=name)(*xs)


def exchange_slots(xs, name):
    na = len(xs)

    def body(*refs):
        x_refs, o_refs = refs[:na], refs[na:2 * na]
        send_sems, recv_sems, local_sems = refs[2 * na:]
        me = _my_id()
        locals_, sends = [], []
        for a in range(na):
            cp = pltpu.make_async_copy(x_refs[a].at[me], o_refs[a].at[me], local_sems.at[a])
            cp.start()
            locals_.append(cp)
        for s in range(1, N_DEV):
            peer, pid = _peer(s)
            for a in range(na):
                cp = pltpu.make_async_remote_copy(
                    src_ref=x_refs[a].at[pid], dst_ref=o_refs[a].at[me], send_sem=send_sems.at[a, s - 1],
                    recv_sem=recv_sems.at[a, s - 1], device_id=peer, device_id_type=MESH)
                cp.start()
                sends.append(cp)
        for s in range(1, N_DEV):
            peer, pid = _peer(s)
            for a in range(na):
                pltpu.make_async_remote_copy(
                    src_ref=x_refs[a].at[pid], dst_ref=o_refs[a].at[pid], send_sem=send_sems.at[a, s - 1],
                    recv_sem=recv_sems.at[a, s - 1], device_id=peer, device_id_type=MESH).wait_recv()
        for cp in sends:
            cp.wait_send()
        for cp in locals_:
            cp.wait()

    spec = pl.BlockSpec(memory_space=pl.ANY)
    return pl.pallas_call(
        body, in_specs=[spec] * na, out_specs=[spec] * na,
        out_shape=[_sds(x.shape, x.dtype) for x in xs],
        scratch_shapes=[pltpu.SemaphoreType.DMA((na, N_DEV - 1)), pltpu.SemaphoreType.DMA((na, N_DEV - 1)),
                        pltpu.SemaphoreType.DMA((na,))],
        compiler_params=pltpu.CompilerParams(vmem_limit_bytes=VMEM_LIMIT), name=name)(*xs)


_HBM = pl.BlockSpec(memory_space=pltpu.HBM)
_SEM = pl.BlockSpec(memory_space=pltpu.SEMAPHORE)
_EFFECT = pltpu.SideEffectType.DATAFLOW_SIDE_EFFECTING


def _split_copy(x_ref, land_ref, s, send_sem, recv_sem, scatter):
    peer, pid = _peer(s)
    src = x_ref.at[pid] if scatter else x_ref
    return pltpu.make_async_remote_copy(src_ref=src, dst_ref=land_ref.at[_my_id()], send_sem=send_sem, recv_sem=recv_sem,
                                        device_id=peer, device_id_type=MESH)


def comm_start(xs, scatter, name):
    na = len(xs)
    me = _my_id()
    lands = []
    for x in xs:
        shape = x.shape if scatter else (N_DEV,) + x.shape
        own = lax.dynamic_slice_in_dim(x, me, 1, 0) if scatter else x[None]
        lands.append(lax.dynamic_update_slice(lax.empty(shape, x.dtype), own, (me,) + (0,) * (len(shape) - 1)))

    def body(*refs):
        x_refs, land_refs = refs[:na], refs[na:2 * na]
        send_sem, recv_sem = refs[2 * na], refs[2 * na + 1]
        token = refs[-1]
        for s in range(1, N_DEV):
            for a in range(na):
                _split_copy(x_refs[a], land_refs[a], s, send_sem, recv_sem, scatter).start()
        token[...] = jnp.zeros_like(token)

    outs = pl.pallas_call(
        body, name=name,
        out_shape=(pltpu.SemaphoreType.DMA(()), pltpu.SemaphoreType.DMA(()))
        + tuple(pltpu.HBM(x.shape, x.dtype) for x in xs) + tuple(pltpu.HBM(l.shape, l.dtype) for l in lands)
        + (_sds((8, LANES), F32),),
        in_specs=(_HBM,) * (2 * na),
        out_specs=(_SEM, _SEM) + (_HBM,) * (2 * na) + (pl.BlockSpec(memory_space=pltpu.VMEM),),
        input_output_aliases={a: 2 + a for a in range(2 * na)},
        compiler_params=pltpu.CompilerParams(has_side_effects=_EFFECT),
    )(*[pltpu.with_memory_space_constraint(x, pltpu.HBM) for x in xs],
      *[pltpu.with_memory_space_constraint(l, pltpu.HBM) for l in lands])
    return dict(sems=outs[0:2], xs=outs[2:2 + na], lands=outs[2 + na:2 + 2 * na], token=outs[-1], scatter=scatter)


def comm_wait(started, after, name):
    xs, lands = started["xs"], started["lands"]
    scatter = started["scatter"]
    na = len(xs)

    def body(*refs):
        x_refs, land_refs = refs[:na], refs[na:2 * na]
        send_sem, recv_sem = refs[2 * na], refs[2 * na + 1]
        for s in range(1, N_DEV):
            for a in range(na):
                cp = _split_copy(x_refs[a], land_refs[a], s, send_sem, recv_sem, scatter)
                cp.wait_send()
                cp.wait_recv()

    outs = pl.pallas_call(
        body, name=name,
        out_shape=tuple(pltpu.HBM(x.shape, x.dtype) for x in xs) + tuple(pltpu.HBM(l.shape, l.dtype) for l in lands),
        in_specs=(_HBM,) * (2 * na) + (_SEM, _SEM, pl.BlockSpec(memory_space=pl.ANY)),
        out_specs=(_HBM,) * (2 * na),
        input_output_aliases={a: a for a in range(2 * na)},
        compiler_params=pltpu.CompilerParams(has_side_effects=_EFFECT),
    )(*xs, *lands, *started["sems"], after)
    return list(outs[na:])


def _cols_to_natural(g):
    return jnp.concatenate([g[k] for k in range(N_DEV)], axis=1)


def _cols_to_slots(w):
    ns = w.shape[1] // N_DEV
    return jnp.stack([w[:, k * ns:(k + 1) * ns] for k in range(N_DEV)])


def _vec8(rows, d):
    rows = [r.reshape(1, d).astype(F32) for r in rows]
    return jnp.concatenate(rows + [jnp.zeros((8 - len(rows), d), F32)], axis=0)


def _ffn_forward(x, vec, w_in, w_out):
    h, a, b, u = ffn_up(x, vec, w_in)
    xn, y = proj_out(u, x, vec, w_out, FFN_RES_WEIGHT, "ffn_down")
    return xn, (x, h, a, b, u, y)


def _ffn_backward(dxo, saved, vec, w_in, w_out):
    x, h, a, b, u, y = saved
    dy, da, db, part_gate = ffn_down_bwd(dxo, y, vec, w_out, a, b)
    dx, part_norm = ffn_up_bwd(da, db, w_in, x, dxo, vec)
    g_out = mm_tn(u, dy, "ffn_dw_out")
    g_in = jnp.concatenate([mm_tn(h, da, "ffn_dw_a"), mm_tn(h, db, "ffn_dw_b")], axis=1)
    rows = jnp.concatenate([part_norm[0:3], part_gate[0:1]], axis=0)
    return dx, g_in, g_out, rows


def _layer_weight_names(layer):
    names = ["ffn1_w_in", "ffn1_w_out", "ffn2_w_in", "ffn2_w_out"]
    names += ["conv_w_in", "conv_w_out"] if layer < N_A_LAYERS else ["attn_w_q", "attn_w_o"]
    if layer == N_A_LAYERS:
        names.append("w_kv")
    return names


class LayerComm:
    def __init__(self, wts):
        self.wts = wts
        self.received = {}

    def _shard(self, name, layer):
        w = self.wts[name]
        if name == "w_kv":
            return w
        return w[layer - N_A_LAYERS] if name.startswith("attn") else w[layer]

    def start_gather(self, layer):
        xs = [self._shard(n, layer).astype(BF16) for n in _layer_weight_names(layer)]
        return comm_start(xs, False, f"gather_start_l{layer}")

    def finish_gather(self, layer, started, after):
        lands = comm_wait(started, after, f"gather_wait_l{layer}")
        W = {}
        for n, g in zip(_layer_weight_names(layer), lands):
            W[n] = _cols_to_natural(g) if n in _COL_SHARDED else g.reshape(-1, g.shape[2])
        return W

    def start_exchange(self, layer, G):
        xs = []
        for n in _layer_weight_names(layer):
            g = G[n]
            xs.append((_cols_to_slots(g) if n in _COL_SHARDED else g.reshape(N_DEV, -1, g.shape[1])).astype(BF16))
        return comm_start(xs, True, f"exchange_start_l{layer}")

    def finish_exchange(self, layer, started, after):
        lands = comm_wait(started, after, f"exchange_wait_l{layer}")
        for n, r in zip(_layer_weight_names(layer), lands):
            self.received[(n, layer)] = r

    def parts(self, name):
        if name == "w_kv":
            return [self.received[(name, N_A_LAYERS)]]
        layers = range(N_A_LAYERS) if name.startswith("conv") else (
            range(N_A_LAYERS, DEPTH) if name.startswith("attn") else range(DEPTH))
        return [self.received[(name, l)] for l in layers]


def device_step(x, positions, target, mods, kvmods, small, comm):
    T, D = x.shape
    groups = DILATED_GROUPS
    lane = jnp.arange(LANES) % HEAD_DIM
    inv = ROPE_THETA ** (-jnp.arange(0, ROPE_DIM, 2, dtype=F32) / ROPE_DIM)
    lane_rows = _vec8([jnp.where(lane < ROPE_DIM, inv[lane % (ROPE_DIM // 2)], 0.0), lane < ROPE_DIM,
                       (lane >= ROPE_DIM // 2) & (lane < ROPE_DIM), lane < ROPE_DIM // 2], LANES)
    tabs = rope_tables(positions.reshape(T, 1), lane_rows)

    def after_token(v, token):
        return v if token is None else v + token[0, 0]

    def vec_of(layer, sub, token=None):
        return after_token(_vec8([small["norm_g"][layer, sub], mods[layer, 3 * sub], mods[layer, 3 * sub + 1],
                                  mods[layer, 3 * sub + 2]], D), token)

    saved = []
    kv_saved = None
    k_sh = v_sh = None
    qw = GROUP_WIDTH * len(groups)
    started = comm.start_gather(0)
    weights = [None] * DEPTH
    weights[0] = comm.finish_gather(0, started, started["token"])
    for layer in range(DEPTH):
        W = weights[layer]
        token = None
        if layer + 1 < DEPTH:
            started = comm.start_gather(layer + 1)
            token = started["token"]
        if layer == N_A_LAYERS:
            kv_vec = after_token(_vec8([small["kv_norm_g"], kvmods[0], kvmods[1]], D), token)
            h_kv, kvp = proj_rope_fwd(x, kv_vec, W["w_kv"], tabs, qw, "kv_fwd")
            k_sh, v_sh = kvp[:, :qw], kvp[:, qw:]
            kv_saved = (x, h_kv, kv_vec)
        rec = {}
        v1 = vec_of(layer, 0, token)
        x, rec["ffn1"] = _ffn_forward(x, v1, W["ffn1_w_in"], W["ffn1_w_out"])
        v2 = vec_of(layer, 1)
        if layer < N_A_LAYERS:
            cw = _vec8(list(small["conv_w"][layer]), D)
            x_in = x
            x, h, bcu, cv, z, y = conv_fwd(x, v2, cw, W["conv_w_in"], W["conv_w_out"])
            rec["mix"] = (x_in, h, bcu, cv, z, y, cw)
        else:
            x_in = x
            h, q = proj_rope_fwd(x, v2, W["attn_w_q"], tabs, qw, "q_fwd")
            os_, ls = [], []
            for g, (win, dil) in enumerate(groups):
                o, l = attn_core_fwd(q, k_sh, v_sh, g, win // dil, dil)
                os_.append(o)
                ls.append(l)
            x, mixed, y = attn_mix_out(os_, ls, x, v2, W["attn_w_o"])
            rec["mix"] = (x_in, h, q, os_, ls, mixed, y)
        v3 = vec_of(layer, 2)
        x, rec["ffn2"] = _ffn_forward(x, v3, W["ffn2_w_in"], W["ffn2_w_out"])
        rec["vecs"] = (v1, v2, v3)
        saved.append(rec)
        if layer + 1 < DEPTH:
            weights[layer + 1] = comm.finish_gather(layer + 1, started, x)

    dx, part_final, loss_tile = final_loss(x, _vec8([small["final_norm_g"]], D), target)
    loss = loss_tile[0, 0]

    conv_rows = [None] * N_A_LAYERS
    kv_rows = None
    mod_rows = [[None] * 3 for _ in range(DEPTH)]
    dkv_pairs = [{"k": [], "v": []} for _ in groups]
    pending = None
    for layer in reversed(range(DEPTH)):
        rec = saved[layer]
        W = weights[layer]
        G = {}
        v1, v2, v3 = rec["vecs"]
        if pending is not None:
            v3 = after_token(v3, pending[1]["token"])
        dx, G["ffn2_w_in"], G["ffn2_w_out"], mod_rows[layer][2] = _ffn_backward(dx, rec["ffn2"], v3, W["ffn2_w_in"], W["ffn2_w_out"])
        if layer < N_A_LAYERS:
            x_in, h, bcu, cv, z, y, cw = rec["mix"]
            dx, dy, dbcu, part, dcw = conv_bwd(dx, x_in, y, bcu, cv, v2, cw, W["conv_w_in"], W["conv_w_out"])
            G["conv_w_out"] = mm_tn(z, dy, "conv_dw_out")
            G["conv_w_in"] = mm_tn(h, dbcu, "conv_dw_in")
            conv_rows[layer] = dcw[0:3]
            mod_rows[layer][1] = part[0:4]
        else:
            x_in, h, q, os_, ls, mixed, y = rec["mix"]
            outs = attn_mix_bwd(dx, y, v2, W["attn_w_o"], os_, ls)
            ng = len(groups)
            dy, dos, rrs, part_gate = outs[0], outs[1:1 + ng], outs[1 + ng:1 + 2 * ng], outs[1 + 2 * ng]
            G["attn_w_o"] = mm_tn(mixed, dy, "attn_dw_o")
            dqs = []
            for g, (win, dil) in enumerate(groups):
                dq, dkc, dkp, dvc, dvp = attn_core_bwd(q, k_sh, v_sh, dos[g], rrs[g], ls[g], g, win // dil, dil)
                dqs.append(dq)
                dkv_pairs[g]["k"].append((dkc, dkp))
                dkv_pairs[g]["v"].append((dvc, dvp))
            dx, dqr, part_norm = proj_rope_bwd(dqs, x_in, dx, v2, W["attn_w_q"], tabs, qw, "q_bwd")
            G["attn_w_q"] = mm_tn(h, dqr, "attn_dw_q")
            mod_rows[layer][1] = jnp.concatenate([part_norm[0:3], part_gate[0:1]], axis=0)
        dx, G["ffn1_w_in"], G["ffn1_w_out"], mod_rows[layer][0] = _ffn_backward(dx, rec["ffn1"], v1, W["ffn1_w_in"], W["ffn1_w_out"])
        if layer == N_A_LAYERS:
            x_kv, h_kv, kv_vec = kv_saved
            dparts = [dkv_combine(dkv_pairs[g]["k"], win // dil, dil, f"dk_combine_g{g}") for g, (win, dil) in enumerate(groups)]
            dparts += [dkv_combine(dkv_pairs[g]["v"], win // dil, dil, f"dv_combine_g{g}") for g, (win, dil) in enumerate(groups)]
            dx, dkvp, part_kv = proj_rope_bwd(dparts, x_kv, dx, kv_vec, W["w_kv"], tabs, qw, "kv_bwd")
            G["w_kv"] = mm_tn(h_kv, dkvp, "kv_dw")
            kv_rows = part_kv[0:3]
        if pending is not None:
            comm.finish_exchange(pending[0], pending[1], dx)
        pending = (layer, comm.start_exchange(layer, G))
    comm.finish_exchange(pending[0], pending[1], dx)

    grads = {"conv_w": jnp.stack(conv_rows), "kv_rows": kv_rows}
    grads["final_norm_g"] = part_final[0]
    rows = jnp.stack([jnp.stack(r) for r in mod_rows])
    grads["norm_g"] = rows[:, :, 0]
    grads["mods"] = rows[:, :, 1:4].reshape(DEPTH, N_MOD, D)
    return loss, dx, grads


_COL_SHARDED = ("ffn1_w_in", "ffn2_w_in", "conv_w_in", "w_kv", "attn_w_q", "attn_w_o")
_ROW_SHARDED = ("ffn1_w_out", "ffn2_w_out", "conv_w_out")
_BIG = _COL_SHARDED + _ROW_SHARDED


def _flat2(a):
    return a.reshape(-1, a.shape[-1])


def _pad_rows(a, mult):
    r = a.shape[0]
    pad = (-r) % mult
    return a if pad == 0 else jnp.concatenate([a, jnp.zeros((pad,) + a.shape[1:], a.dtype)], axis=0)


def kernel(x, c, positions, norm_g, ada_w, ada_b, ffn1_w_in, ffn1_w_out, ffn2_w_in, ffn2_w_out, conv_w_in, conv_w, conv_w_out, kv_norm_g, kv_ada_w, kv_ada_b, w_kv, attn_w_q, attn_w_o, final_norm_g, loss_target, m_norm_g, m_ada_w, m_ada_b, m_ffn1_w_in, m_ffn1_w_out, m_ffn2_w_in, m_ffn2_w_out, m_conv_w_in, m_conv_w, m_conv_w_out, m_kv_norm_g, m_kv_ada_w, m_kv_ada_b, m_w_kv, m_attn_w_q, m_attn_w_o, m_final_norm_g, v_norm_g, v_ada_w, v_ada_b, v_ffn1_w_in, v_ffn1_w_out, v_ffn2_w_in, v_ffn2_w_out, v_conv_w_in, v_conv_w, v_conv_w_out, v_kv_norm_g, v_kv_ada_w, v_kv_ada_b, v_w_kv, v_attn_w_q, v_attn_w_o, v_final_norm_g):
    names = ("norm_g", "ada_w", "ada_b", "ffn1_w_in", "ffn1_w_out", "ffn2_w_in", "ffn2_w_out", "conv_w_in", "conv_w",
             "conv_w_out", "kv_norm_g", "kv_ada_w", "kv_ada_b", "w_kv", "attn_w_q", "attn_w_o", "final_norm_g")
    wts = dict(zip(names, (norm_g, ada_w, ada_b, ffn1_w_in, ffn1_w_out, ffn2_w_in, ffn2_w_out, conv_w_in, conv_w, conv_w_out,
                           kv_norm_g, kv_ada_w, kv_ada_b, w_kv, attn_w_q, attn_w_o, final_norm_g)))
    mom = dict(zip(names, (m_norm_g, m_ada_w, m_ada_b, m_ffn1_w_in, m_ffn1_w_out, m_ffn2_w_in, m_ffn2_w_out, m_conv_w_in,
                           m_conv_w, m_conv_w_out, m_kv_norm_g, m_kv_ada_w, m_kv_ada_b, m_w_kv, m_attn_w_q, m_attn_w_o,
                           m_final_norm_g)))
    var = dict(zip(names, (v_norm_g, v_ada_w, v_ada_b, v_ffn1_w_in, v_ffn1_w_out, v_ffn2_w_in, v_ffn2_w_out, v_conv_w_in,
                           v_conv_w, v_conv_w_out, v_kv_norm_g, v_kv_ada_w, v_kv_ada_b, v_w_kv, v_attn_w_q, v_attn_w_o,
                           v_final_norm_g)))
    T, D = x.shape[1], x.shape[2]
    me = _my_id()
    nmod = ada_w.shape[2]
    nkv = kv_ada_w.shape[1]

    comm = LayerComm(wts)
    W = {}

    ds = norm_g.shape[2]
    small = jnp.concatenate([c.reshape(-1), norm_g.reshape(-1), conv_w.reshape(-1)]).astype(F32)
    n_small = small.shape[0]
    small = _pad_rows(small.reshape(-1, 1), 8 * LANES).reshape(-1, LANES)
    (small_all,) = all_gather([small], pltpu.VMEM, "gather_small")
    small_all = small_all.reshape(N_DEV, -1)[:, :n_small]
    c_all = small_all[:, :D]
    def full_rows(off, count):
        return jnp.stack([small_all[:, off + i * ds:off + (i + 1) * ds].reshape(D) for i in range(count)])

    W["norm_g"] = full_rows(D, DEPTH * 3).reshape(DEPTH, 3, D)
    W["conv_w"] = full_rows(D + DEPTH * 3 * ds, N_A_LAYERS * 3).reshape(N_A_LAYERS, 3, D)
    W["kv_norm_g"], W["final_norm_g"] = kv_norm_g, final_norm_g

    ada_b_mine = lax.dynamic_slice_in_dim(ada_b, me * nmod, nmod, axis=1).reshape(DEPTH, 1, nmod)
    kv_b_mine = lax.dynamic_slice_in_dim(kv_ada_b, me * nkv, nkv, axis=0).reshape(1, 1, nkv)
    mods_cols = mods_project(c_all, ada_w, ada_b_mine)
    kv_cols = mods_project(c_all, kv_ada_w.reshape(1, D, nkv), kv_b_mine)
    mcat = jnp.concatenate([mods_cols[l] for l in range(DEPTH)] + [kv_cols[0]], axis=1)
    wm = mcat.shape[1]
    if wm % LANES:
        mcat = jnp.concatenate([mcat, jnp.zeros((N_DEV, LANES - wm % LANES), F32)], axis=1)
    (mods_all,) = exchange_slots([mcat.reshape(N_DEV, 1, -1)], "exchange_mods")
    mods_all = mods_all.reshape(N_DEV, -1)
    mods = jnp.stack([mods_all[:, l * nmod:(l + 1) * nmod].reshape(N_MOD, D) for l in range(DEPTH)])
    kvmods = mods_all[:, DEPTH * nmod:DEPTH * nmod + nkv].reshape(2, D)

    loss_local, dx, grads = device_step(x[0], positions[0], loss_target[0], mods, kvmods, W, comm)
    loss = lax.psum(loss_local, MESH_AXES)

    dmods = grads["mods"].reshape(-1)
    dkvm = grads["kv_rows"][1:3].reshape(-1)
    vecs = jnp.concatenate([dmods, dkvm, grads["kv_rows"][0], grads["final_norm_g"], grads["norm_g"].reshape(-1),
                            grads["conv_w"].reshape(-1)])
    n_vec = vecs.shape[0]
    vecs = _pad_rows(vecs.reshape(-1, 1), 8 * LANES).reshape(-1, LANES)
    (vec_all,) = all_gather([vecs], pltpu.VMEM, "gather_vector_grads")
    vec_all = vec_all.reshape(N_DEV, -1)[:, :n_vec]
    nm_, nk_ = DEPTH * N_MOD * D, 2 * D
    dmods_all = vec_all[:, :nm_].reshape(N_DEV, DEPTH, N_MOD * D)
    dkvm_all = vec_all[:, nm_:nm_ + nk_]
    rest = vec_all[:, nm_ + nk_:]
    parts_kv_norm, parts_final = rest[:, :D].reshape(N_DEV, 1, D), rest[:, D:2 * D].reshape(N_DEV, 1, D)
    parts_norm = lax.dynamic_slice_in_dim(rest[:, 2 * D:2 * D + DEPTH * 3 * D].reshape(N_DEV, DEPTH * 3, D), me * ds, ds, axis=2)
    parts_conv = lax.dynamic_slice_in_dim(rest[:, 2 * D + DEPTH * 3 * D:].reshape(N_DEV, N_A_LAYERS * 3, D), me * ds, ds, axis=2)
    dm_cols = lax.dynamic_slice_in_dim(dmods_all, me * nmod, nmod, axis=2)
    dm_mine = jnp.stack([dm_cols[:, l] for l in range(DEPTH)])
    dkv_mine = lax.dynamic_slice_in_dim(dkvm_all, me * nkv, nkv, axis=1).reshape(1, N_DEV, nkv)
    g_ada_w = mods_weight_grad(c_all, dm_mine)
    g_kv_ada_w = mods_weight_grad(c_all, dkv_mine)[0]

    out_g, out_d, out_m, out_v = {}, {}, {}, {}

    def update(n, g, w, parts=False):
        shp = w.shape
        w2 = w.reshape(1, -1) if w.ndim == 1 else _flat2(w)
        g2 = g if parts else g.reshape(w2.shape)
        res = adam_update(g2, w2, mom[n].reshape(w2.shape), var[n].reshape(w2.shape), parts, "adam_" + n)
        out_g[n], out_d[n], out_m[n], out_v[n] = (r.reshape(shp) for r in res)

    for n in _BIG:
        shp = wts[n].shape
        shp3 = (1,) + shp if len(shp) == 2 else shp
        res = adam_update_layers(comm.parts(n), wts[n].reshape(shp3), mom[n].reshape(shp3), var[n].reshape(shp3), "adam_" + n)
        out_g[n], out_d[n], out_m[n], out_v[n] = (r.reshape(shp) for r in res)
    update("ada_w", g_ada_w, ada_w)
    update("kv_ada_w", g_kv_ada_w, kv_ada_w)
    update("ada_b", dmods_all, ada_b, True)
    update("kv_ada_b", dkvm_all.reshape(N_DEV, 1, nk_), kv_ada_b, True)
    update("kv_norm_g", parts_kv_norm, kv_norm_g, True)
    update("final_norm_g", parts_final, final_norm_g, True)
    update("norm_g", parts_norm, norm_g, True)
    update("conv_w", parts_conv, conv_w, True)

    return (loss, dx.reshape(x.shape), *[out_g[n] for n in names], *[out_d[n] for n in names],
            *[out_m[n] for n in names], *[out_v[n] for n in names])
```

```python
import functools

import jax
import jax.numpy as jnp
from jax import lax
from jax.experimental import pallas as pl
from jax.experimental.pallas import tpu as pltpu

F32, BF16 = jnp.float32, jnp.bfloat16

N_DEV = 8
MESH_AXES = ("x", "y", "c")
DEPTH = 4
N_A_LAYERS = 2
HEAD_DIM = 64
HEADS_PER_GROUP = 8
GROUP_WIDTH = HEAD_DIM * HEADS_PER_GROUP
DILATED_GROUPS = ((128, 1), (512, 4), (2048, 16))
ROPE_DIM = HEAD_DIM // 4
ROPE_THETA = 500000.0
NORM_EPS = 1e-5
FFN_RES_WEIGHT = 0.5
N_MOD = 9
ADAM_LR, ADAM_B1, ADAM_B2, ADAM_EPS, ADAM_WD, ADAM_STEP = 0.001, 0.9, 0.999, 1e-08, 0.01, 10

LANES = 128
TOKEN_TILE = 512
CONTRACT_TILE = 2048
MXU_WIDTH = 256
VMEM_LIMIT = 56 * 1024 * 1024
MESH = pl.DeviceIdType.MESH


def _cp(*sem):
    return pltpu.CompilerParams(dimension_semantics=sem, vmem_limit_bytes=VMEM_LIMIT)


def _pick(n, cap, mult=LANES):
    if n <= cap:
        return n
    best = None
    for t in range(mult, cap + 1, mult):
        if n % t == 0:
            best = t
    assert best is not None, (n, cap)
    return best


def _tok(tm, w):
    return pl.BlockSpec((tm, w), lambda i: (i, 0))


def _res(shape):
    nd = len(shape)
    return pl.BlockSpec(shape, lambda *_: (0,) * nd, pipeline_mode=pl.Buffered(1))


def _sds(shape, dt):
    return jax.ShapeDtypeStruct(shape, dt)


def _sigmoid(a):
    return 1.0 / (1.0 + jnp.exp(-a))


def _modnorm(x, g, sh, sc):
    r = lax.rsqrt(jnp.mean(x * x, axis=-1, keepdims=True) + NORM_EPS)
    return (x * r * g) * (1.0 + sc) + sh


def _dot(a, b):
    return jnp.dot(a, b, preferred_element_type=F32)


def _dot_nt(a, b):
    return lax.dot_general(a, b, (((1,), (1,)), ((), ())), preferred_element_type=F32)


def _dot_tn(a, b):
    return lax.dot_general(a, b, (((0,), (0,)), ((), ())), preferred_element_type=F32)


def _rows8(rows, d):
    pad = 8 - len(rows)
    return jnp.concatenate(list(rows) + [jnp.zeros((pad, d), F32)], axis=0)


def _acc_rows(ref, tile, first):
    @pl.when(first)
    def _():
        ref[...] = tile

    @pl.when(jnp.logical_not(first))
    def _():
        ref[...] += tile


def ffn_up(x, vec, w_in):
    T, D = x.shape
    F = w_in.shape[1] // 2
    tm, cw = min(TOKEN_TILE, T), _pick(F, MXU_WIDTH)

    def body(x_ref, vec_ref, w_ref, h_ref, a_ref, b_ref, u_ref):
        hb = _modnorm(x_ref[...], vec_ref[0:1], vec_ref[1:2], vec_ref[2:3]).astype(BF16)
        h_ref[...] = hb
        for c in range(F // cw):
            lo, hi = c * cw, (c + 1) * cw
            a = _dot(hb, w_ref[:, lo:hi])
            b = _dot(hb, w_ref[:, F + lo:F + hi])
            a_ref[:, lo:hi] = a.astype(BF16)
            b_ref[:, lo:hi] = b.astype(BF16)
            u_ref[:, lo:hi] = (a * _sigmoid(a) * b).astype(BF16)

    return pl.pallas_call(
        body, grid=(T // tm,),
        in_specs=[_tok(tm, D), _res((8, D)), _res((D, 2 * F))],
        out_specs=[_tok(tm, D), _tok(tm, F), _tok(tm, F), _tok(tm, F)],
        out_shape=[_sds((T, D), BF16), _sds((T, F), BF16), _sds((T, F), BF16), _sds((T, F), BF16)],
        compiler_params=_cp("arbitrary"), name="ffn_up")(x, vec, w_in)


def proj_out(u, x, vec, w_out, res_weight, name):
    T, D = x.shape
    K = u.shape[1]
    tm = min(TOKEN_TILE, T)

    def body(u_ref, x_ref, vec_ref, w_ref, xn_ref, y_ref):
        y = _dot(u_ref[...], w_ref[...])
        y_ref[...] = y.astype(BF16)
        xn_ref[...] = x_ref[...] + (res_weight * (1.0 + vec_ref[3:4])) * y

    return pl.pallas_call(
        body, grid=(T // tm,),
        in_specs=[_tok(tm, K), _tok(tm, D), _res((8, D)), _res((K, D))],
        out_specs=[_tok(tm, D), _tok(tm, D)],
        out_shape=[_sds((T, D), F32), _sds((T, D), BF16)],
        compiler_params=_cp("arbitrary"), name=name)(u, x, vec, w_out)


def ffn_down_bwd(dxo, y, vec, w_out, a, b):
    T, D = dxo.shape
    F = a.shape[1]
    tm, cw = min(TOKEN_TILE, T), _pick(F, MXU_WIDTH)

    def body(dxo_ref, y_ref, vec_ref, w_ref, a_ref, b_ref, dy_ref, da_ref, db_ref, part_ref):
        dxo_t = dxo_ref[...]
        dyb = (dxo_t * (FFN_RES_WEIGHT * (1.0 + vec_ref[3:4]))).astype(BF16)
        dy_ref[...] = dyb
        dgate = FFN_RES_WEIGHT * jnp.sum(dxo_t * y_ref[...].astype(F32), axis=0, keepdims=True)
        _acc_rows(part_ref, _rows8([dgate], D), pl.program_id(0) == 0)
        for c in range(F // cw):
            lo, hi = c * cw, (c + 1) * cw
            du = _dot_nt(dyb, w_ref[lo:hi, :])
            av = a_ref[:, lo:hi].astype(F32)
            bv = b_ref[:, lo:hi].astype(F32)
            sg = _sigmoid(av)
            da_ref[:, lo:hi] = (du * bv * (sg * (1.0 + av * (1.0 - sg)))).astype(BF16)
            db_ref[:, lo:hi] = (du * (av * sg)).astype(BF16)

    return pl.pallas_call(
        body, grid=(T // tm,),
        in_specs=[_tok(tm, D), _tok(tm, D), _res((8, D)), _res((F, D)), _tok(tm, F), _tok(tm, F)],
        out_specs=[_tok(tm, D), _tok(tm, F), _tok(tm, F), pl.BlockSpec((8, D), lambda i: (0, 0))],
        out_shape=[_sds((T, D), BF16), _sds((T, F), BF16), _sds((T, F), BF16), _sds((8, D), F32)],
        compiler_params=_cp("arbitrary"), name="ffn_down_bwd")(dxo, y, vec, w_out, a, b)


def ffn_up_bwd(da, db, w_in, x, dxo, vec):
    T, D = x.shape
    F = da.shape[1]
    tm = min(TOKEN_TILE, T)

    def body(da_ref, db_ref, w_ref, x_ref, dxo_ref, vec_ref, dx_ref, part_ref):
        dh = _dot_nt(da_ref[...], w_ref[:, 0:F]) + _dot_nt(db_ref[...], w_ref[:, F:2 * F])
        _, vjp = jax.vjp(_modnorm, x_ref[...], vec_ref[0:1], vec_ref[1:2], vec_ref[2:3])
        dx, dg, dsh, dsc = vjp(dh)
        dx_ref[...] = dxo_ref[...] + dx
        _acc_rows(part_ref, _rows8([dg, dsh, dsc], D), pl.program_id(0) == 0)

    return pl.pallas_call(
        body, grid=(T // tm,),
        in_specs=[_tok(tm, F), _tok(tm, F), _res((D, 2 * F)), _tok(tm, D), _tok(tm, D), _res((8, D))],
        out_specs=[_tok(tm, D), pl.BlockSpec((8, D), lambda i: (0, 0))],
        out_shape=[_sds((T, D), F32), _sds((8, D), F32)],
        compiler_params=_cp("arbitrary"), name="ffn_up_bwd")(da, db, w_in, x, dxo, vec)


def mm_tn(a, b, name):
    T, M = a.shape
    N = b.shape[1]
    tk = min(CONTRACT_TILE, T)
    tmm, tn = _pick(M, 1408), _pick(N, 1536)

    def body(a_ref, b_ref, o_ref):
        t = _dot_tn(a_ref[...], b_ref[...])

        @pl.when(pl.program_id(2) == 0)
        def _():
            o_ref[...] = t

        @pl.when(pl.program_id(2) > 0)
        def _():
            o_ref[...] += t

    return pl.pallas_call(
        body, grid=(M // tmm, N // tn, T // tk),
        in_specs=[pl.BlockSpec((tk, tmm), lambda i, j, k: (k, i)), pl.BlockSpec((tk, tn), lambda i, j, k: (k, j))],
        out_specs=pl.BlockSpec((tmm, tn), lambda i, j, k: (i, j)),
        out_shape=_sds((M, N), F32),
        compiler_params=_cp("arbitrary", "arbitrary", "arbitrary"), name=name)(a, b)


def conv_fwd(x, vec, cw, w_in, w_out):
    T, D = x.shape
    tm = min(TOKEN_TILE, T)

    def body(x_ref, vec_ref, cw_ref, wi_ref, wo_ref, xn_ref, h_ref, bcu_ref, cv_ref, z_ref, y_ref, vbuf):
        @pl.when(pl.program_id(0) == 0)
        def _():
            vbuf[0:8, :] = jnp.zeros((8, D), F32)

        x_t = x_ref[...]
        hb = _modnorm(x_t, vec_ref[0:1], vec_ref[1:2], vec_ref[2:3]).astype(BF16)
        h_ref[...] = hb
        bcu = _dot(hb, wi_ref[...])
        bcu_ref[...] = bcu.astype(BF16)
        bg, v = bcu[:, 0:D], bcu[:, D:2 * D] * bcu[:, 2 * D:3 * D]
        vbuf[8:8 + tm, :] = v
        conv = cw_ref[0:1] * vbuf[6:6 + tm, :] + cw_ref[1:2] * vbuf[7:7 + tm, :] + cw_ref[2:3] * v
        cv_ref[...] = conv.astype(BF16)
        zb = (bg * conv).astype(BF16)
        z_ref[...] = zb
        y = _dot(zb, wo_ref[...])
        y_ref[...] = y.astype(BF16)
        xn_ref[...] = x_t + (1.0 + vec_ref[3:4]) * y
        vbuf[0:8, :] = vbuf[tm:tm + 8, :]

    return pl.pallas_call(
        body, grid=(T // tm,),
        in_specs=[_tok(tm, D), _res((8, D)), _res((8, D)), _res((D, 3 * D)), _res((D, D))],
        out_specs=[_tok(tm, D), _tok(tm, D), _tok(tm, 3 * D), _tok(tm, D), _tok(tm, D), _tok(tm, D)],
        out_shape=[_sds((T, D), F32), _sds((T, D), BF16), _sds((T, 3 * D), BF16), _sds((T, D), BF16),
                   _sds((T, D), BF16), _sds((T, D), BF16)],
        scratch_shapes=[pltpu.VMEM((tm + 8, D), F32)],
        compiler_params=_cp("arbitrary"), name="conv_fwd")(x, vec, cw, w_in, w_out)


def conv_bwd(dxo, x, y, bcu, cv, vec, cw, w_in, w_out):
    T, D = x.shape
    tm = min(TOKEN_TILE, T)
    nt = T // tm

    def body(dxo_ref, x_ref, y_ref, bcu_ref, cv_ref, vec_ref, cw_ref, wi_ref, wo_ref,
             dx_ref, dy_ref, dbcu_ref, part_ref, dcw_ref, dcbuf):
        first = pl.program_id(0) == 0

        @pl.when(first)
        def _():
            dcbuf[tm:tm + 8, :] = jnp.zeros((8, D), F32)

        dxo_t = dxo_ref[...]
        dyb = (dxo_t * (1.0 + vec_ref[3:4])).astype(BF16)
        dy_ref[...] = dyb
        dgate = jnp.sum(dxo_t * y_ref[...].astype(F32), axis=0, keepdims=True)
        dz = _dot_nt(dyb, wo_ref[...])
        bcu_t = bcu_ref[...].astype(F32)
        bg, cg, ug = bcu_t[:, 0:D], bcu_t[:, D:2 * D], bcu_t[:, 2 * D:3 * D]
        dconv = dz * bg
        dbg = dz * cv_ref[...].astype(F32)
        dcbuf[0:tm, :] = dconv
        d1, d2 = dcbuf[1:tm + 1, :], dcbuf[2:tm + 2, :]
        dv = cw_ref[2:3] * dconv + cw_ref[1:2] * d1 + cw_ref[0:1] * d2
        v = cg * ug
        dcw = _rows8([jnp.sum(d2 * v, axis=0, keepdims=True), jnp.sum(d1 * v, axis=0, keepdims=True),
                      jnp.sum(dconv * v, axis=0, keepdims=True)], D)
        dbcu = jnp.concatenate([dbg, dv * ug, dv * cg], axis=1).astype(BF16)
        dbcu_ref[...] = dbcu
        dh = _dot_nt(dbcu, wi_ref[...])
        _, vjp = jax.vjp(_modnorm, x_ref[...], vec_ref[0:1], vec_ref[1:2], vec_ref[2:3])
        dx, dg, dsh, dsc = vjp(dh)
        dx_ref[...] = dxo_t + dx
        _acc_rows(part_ref, _rows8([dg, dsh, dsc, dgate], D), first)
        _acc_rows(dcw_ref, dcw, first)
        dcbuf[tm:tm + 8, :] = dcbuf[0:8, :]

    def rev(w):
        return pl.BlockSpec((tm, w), lambda i: (nt - 1 - i, 0))

    return pl.pallas_call(
        body, grid=(nt,),
        in_specs=[rev(D), rev(D), rev(D), rev(3 * D), rev(D), _res((8, D)), _res((8, D)), _res((D, 3 * D)), _res((D, D))],
        out_specs=[rev(D), rev(D), rev(3 * D), pl.BlockSpec((8, D), lambda i: (0, 0)), pl.BlockSpec((8, D), lambda i: (0, 0))],
        out_shape=[_sds((T, D), F32), _sds((T, D), BF16), _sds((T, 3 * D), BF16), _sds((8, D), F32), _sds((8, D), F32)],
        scratch_shapes=[pltpu.VMEM((tm + 8, D), F32)],
        compiler_params=_cp("arbitrary"), name="conv_bwd")(dxo, x, y, bcu, cv, vec, cw, w_in, w_out)


def rope_tables(pos, lane_rows):
    T = pos.shape[0]
    tm = min(TOKEN_TILE, T)

    def body(p_ref, lr_ref, c_ref, sp_ref, sm_ref):
        ang = p_ref[...].astype(F32) * lr_ref[0:1]
        cs, sn = jnp.cos(ang), jnp.sin(ang)
        c_ref[...] = jnp.where(lr_ref[1:2] > 0.5, cs, 1.0)
        sp_ref[...] = jnp.where(lr_ref[2:3] > 0.5, sn, 0.0)
        sm_ref[...] = jnp.where(lr_ref[3:4] > 0.5, -sn, 0.0)

    return pl.pallas_call(
        body, grid=(T // tm,),
        in_specs=[_tok(tm, 1), _res((8, LANES))],
        out_specs=[_tok(tm, LANES)] * 3,
        out_shape=[_sds((T, LANES), F32)] * 3,
        compiler_params=_cp("arbitrary"), name="rope_tables")(pos, lane_rows)


def _rope(t, c, sp, sm):
    w = t.shape[1]
    reps = w // LANES
    cf, spf, smf = jnp.tile(c, (1, reps)), jnp.tile(sp, (1, reps)), jnp.tile(sm, (1, reps))
    half = ROPE_DIM // 2
    return t * cf + pltpu.roll(t, half, axis=1) * spf + pltpu.roll(t, w - half, axis=1) * smf


def _rope_t(d, c, sp, sm):
    w = d.shape[1]
    reps = w // LANES
    cf, spf, smf = jnp.tile(c, (1, reps)), jnp.tile(sp, (1, reps)), jnp.tile(sm, (1, reps))
    half = ROPE_DIM // 2
    return d * cf + pltpu.roll(d * spf, w - half, axis=1) + pltpu.roll(d * smf, half, axis=1)


def proj_rope_fwd(x, vec, w, tabs, n_rope, name):
    T, D = x.shape
    N = w.shape[1]
    tm = min(TOKEN_TILE, T)

    def body(x_ref, vec_ref, w_ref, c_ref, sp_ref, sm_ref, h_ref, p_ref):
        hb = _modnorm(x_ref[...], vec_ref[0:1], vec_ref[1:2], vec_ref[2:3]).astype(BF16)
        h_ref[...] = hb
        p = _dot(hb, w_ref[...])
        pr = _rope(p[:, 0:n_rope], c_ref[...], sp_ref[...], sm_ref[...])
        if n_rope < N:
            pr = jnp.concatenate([pr, p[:, n_rope:N]], axis=1)
        p_ref[...] = pr.astype(BF16)

    return pl.pallas_call(
        body, grid=(T // tm,),
        in_specs=[_tok(tm, D), _res((8, D)), _res((D, N))] + [_tok(tm, LANES)] * 3,
        out_specs=[_tok(tm, D), _tok(tm, N)],
        out_shape=[_sds((T, D), BF16), _sds((T, N), BF16)],
        compiler_params=_cp("arbitrary"), name=name)(x, vec, w, *tabs)


def proj_rope_bwd(dparts, x, dxo, vec, w, tabs, n_rope, name):
    T, D = x.shape
    N = w.shape[1]
    tm = min(TOKEN_TILE, T)
    npart = len(dparts)

    def body(*refs):
        d_refs = refs[:npart]
        x_ref, dxo_ref, vec_ref, w_ref, c_ref, sp_ref, sm_ref, dx_ref, dp_ref, part_ref = refs[npart:]
        d = jnp.concatenate([r[...].astype(F32) for r in d_refs], axis=1)
        dr = _rope_t(d[:, 0:n_rope], c_ref[...], sp_ref[...], sm_ref[...])
        if n_rope < N:
            dr = jnp.concatenate([dr, d[:, n_rope:N]], axis=1)
        dpb = dr.astype(BF16)
        dp_ref[...] = dpb
        dh = _dot_nt(dpb, w_ref[...])
        _, vjp = jax.vjp(_modnorm, x_ref[...], vec_ref[0:1], vec_ref[1:2], vec_ref[2:3])
        dx, dg, dsh, dsc = vjp(dh)
        dx_ref[...] = dxo_ref[...] + dx
        _acc_rows(part_ref, _rows8([dg, dsh, dsc], D), pl.program_id(0) == 0)

    return pl.pallas_call(
        body, grid=(T // tm,),
        in_specs=[_tok(tm, p.shape[1]) for p in dparts] + [_tok(tm, D), _tok(tm, D), _res((8, D)), _res((D, N))]
        + [_tok(tm, LANES)] * 3,
        out_specs=[_tok(tm, D), _tok(tm, N), pl.BlockSpec((8, D), lambda i: (0, 0))],
        out_shape=[_sds((T, D), F32), _sds((T, N), BF16), _sds((8, D), F32)],
        compiler_params=_cp("arbitrary"), name=name)(*dparts, x, dxo, vec, w, *tabs)


def _valid_mask(n, i):
    qi = lax.broadcasted_iota(jnp.int32, (n, 2 * n), 0)
    kj = lax.broadcasted_iota(jnp.int32, (n, 2 * n), 1)
    dist = n + qi - kj
    return (dist >= 0) & (dist <= n) & ((kj >= n) | (i > 0))


def attn_core_fwd(q, k, v, g, n, d):
    T, QW = q.shape
    GW = GROUP_WIDTH
    ng = QW // GW
    M = T // d
    scale = HEAD_DIM ** -0.5

    def body(q_ref, kp_ref, kc_ref, vp_ref, vc_ref, o_ref, l_ref):
        valid = _valid_mask(n, pl.program_id(1))
        qv = q_ref[...]
        kk = jnp.concatenate([kp_ref[...], kc_ref[...]], axis=0)
        vv = jnp.concatenate([vp_ref[...], vc_ref[...]], axis=0)
        for h in range(HEADS_PER_GROUP):
            hs = slice(HEAD_DIM * h, HEAD_DIM * (h + 1))
            s = jnp.where(valid, _dot_nt(qv[:, hs], kk[:, hs]) * scale, -1e30)
            m = jnp.max(s, axis=1, keepdims=True)
            p = jnp.exp(s - m)
            den = jnp.sum(p, axis=1, keepdims=True)
            o_ref[:, hs] = _dot((p / den).astype(BF16), vv[:, hs])
            l_ref[:, hs] = jnp.broadcast_to(m + jnp.log(den), (n, HEAD_DIM))

    cur = pl.BlockSpec((n, GW), lambda r, i: (i, r * ng + g))
    prv = pl.BlockSpec((n, GW), lambda r, i: (jnp.maximum(i - 1, 0), r * ng + g))
    out = pl.BlockSpec((n, GW), lambda r, i: (i, r))
    qv, kv, vv = q.reshape(M, d * QW), k.reshape(M, d * QW), v.reshape(M, d * QW)
    o, l = pl.pallas_call(
        body, grid=(d, M // n),
        in_specs=[cur, prv, cur, prv, cur], out_specs=[out, out],
        out_shape=[_sds((M, d * GW), F32), _sds((M, d * GW), F32)],
        compiler_params=_cp("arbitrary", "arbitrary"), name=f"attn_fwd_g{g}")(qv, kv, kv, vv, vv)
    return o.reshape(T, GW), l.reshape(T, GW)


def attn_core_bwd(q, k, v, do, rr, lse, g, n, d):
    T, QW = q.shape
    GW = GROUP_WIDTH
    ng = QW // GW
    M = T // d
    scale = HEAD_DIM ** -0.5

    def body(q_ref, kp_ref, kc_ref, vp_ref, vc_ref, do_ref, r_ref, l_ref, dq_ref, dkc_ref, dkp_ref, dvc_ref, dvp_ref):
        valid = _valid_mask(n, pl.program_id(1))
        qv, dov = q_ref[...], do_ref[...]
        kk = jnp.concatenate([kp_ref[...], kc_ref[...]], axis=0)
        vv = jnp.concatenate([vp_ref[...], vc_ref[...]], axis=0)
        for h in range(HEADS_PER_GROUP):
            hs = slice(HEAD_DIM * h, HEAD_DIM * (h + 1))
            s = jnp.where(valid, _dot_nt(qv[:, hs], kk[:, hs]) * scale, -1e30)
            p = jnp.exp(s - l_ref[:, HEAD_DIM * h:HEAD_DIM * h + 1])
            dp = _dot_nt(dov[:, hs], vv[:, hs])
            delta = jnp.sum(r_ref[:, hs], axis=1, keepdims=True)
            ds = (p * (dp - delta) * scale).astype(BF16)
            dq_ref[:, hs] = _dot(ds, kk[:, hs]).astype(BF16)
            dk = _dot_tn(ds, qv[:, hs]).astype(BF16)
            dv = _dot_tn(p.astype(BF16), dov[:, hs]).astype(BF16)
            dkp_ref[:, hs], dkc_ref[:, hs] = dk[0:n], dk[n:2 * n]
            dvp_ref[:, hs], dvc_ref[:, hs] = dv[0:n], dv[n:2 * n]

    cur = pl.BlockSpec((n, GW), lambda r, i: (i, r * ng + g))
    prv = pl.BlockSpec((n, GW), lambda r, i: (jnp.maximum(i - 1, 0), r * ng + g))
    blk = pl.BlockSpec((n, GW), lambda r, i: (i, r))
    qv, kv, vv = q.reshape(M, d * QW), k.reshape(M, d * QW), v.reshape(M, d * QW)
    outs = pl.pallas_call(
        body, grid=(d, M // n),
        in_specs=[cur, prv, cur, prv, cur, blk, blk, blk], out_specs=[blk] * 5,
        out_shape=[_sds((M, d * GW), BF16)] * 5,
        compiler_params=_cp("arbitrary", "arbitrary"), name=f"attn_bwd_g{g}")(
            qv, kv, kv, vv, vv, do.reshape(M, d * GW), rr.reshape(M, d * GW), lse.reshape(M, d * GW))
    return [o.reshape(T, GW) for o in outs]


def dkv_combine(cur_prev, n, d, name):
    T, GW = cur_prev[0][0].shape
    M = T // d
    nb = M // n
    flat = [a.reshape(M, d * GW) for pair in cur_prev for a in pair]

    def body(*refs):
        o_ref = refs[-1]
        last = pl.program_id(1) == nb - 1
        acc = jnp.zeros((n, GW), F32)
        for t in range(0, len(refs) - 1, 2):
            acc = acc + refs[t][...].astype(F32) + jnp.where(last, 0.0, refs[t + 1][...].astype(F32))
        o_ref[...] = acc.astype(BF16)

    cur = pl.BlockSpec((n, GW), lambda r, i: (i, r))
    nxt = pl.BlockSpec((n, GW), lambda r, i: (jnp.minimum(i + 1, nb - 1), r))
    out = pl.pallas_call(
        body, grid=(d, nb), in_specs=[cur, nxt] * len(cur_prev), out_specs=cur,
        out_shape=_sds((M, d * GW), BF16),
        compiler_params=_cp("arbitrary", "arbitrary"), name=name)(*flat)
    return out.reshape(T, GW)


def _group_weights(ls):
    mx = functools.reduce(jnp.maximum, ls)
    es = [jnp.exp(l - mx) for l in ls]
    tot = functools.reduce(lambda a, b: a + b, es)
    return [e / tot for e in es]


def attn_mix_out(os_, ls, x, vec, w_o):
    T, D = x.shape
    GW = GROUP_WIDTH
    tm = min(TOKEN_TILE, T)
    ng = len(os_)

    def body(*refs):
        o_refs, l_refs = refs[:ng], refs[ng:2 * ng]
        x_ref, vec_ref, w_ref, xn_ref, mix_ref, y_ref = refs[2 * ng:]
        ws = _group_weights([r[...] for r in l_refs])
        mixed = functools.reduce(lambda a, b: a + b, [w * r[...] for w, r in zip(ws, o_refs)])
        mb = mixed.astype(BF16)
        mix_ref[...] = mb
        y = _dot(mb, w_ref[...])
        y_ref[...] = y.astype(BF16)
        xn_ref[...] = x_ref[...] + (1.0 + vec_ref[3:4]) * y

    return pl.pallas_call(
        body, grid=(T // tm,),
        in_specs=[_tok(tm, GW)] * (2 * ng) + [_tok(tm, D), _res((8, D)), _res((GW, D))],
        out_specs=[_tok(tm, D), _tok(tm, GW), _tok(tm, D)],
        out_shape=[_sds((T, D), F32), _sds((T, GW), BF16), _sds((T, D), BF16)],
        compiler_params=_cp("arbitrary"), name="attn_mix_out")(*os_, *ls, x, vec, w_o)


def attn_mix_bwd(dxo, y, vec, w_o, os_, ls):
    T, D = dxo.shape
    GW = GROUP_WIDTH
    tm = min(TOKEN_TILE, T)
    ng = len(os_)

    def body(*refs):
        dxo_ref, y_ref, vec_ref, w_ref = refs[:4]
        o_refs, l_refs = refs[4:4 + ng], refs[4 + ng:4 + 2 * ng]
        dy_ref = refs[4 + 2 * ng]
        do_refs = refs[5 + 2 * ng:5 + 3 * ng]
        r_refs = refs[5 + 3 * ng:5 + 4 * ng]
        part_ref = refs[5 + 4 * ng]
        dxo_t = dxo_ref[...]
        dyb = (dxo_t * (1.0 + vec_ref[3:4])).astype(BF16)
        dy_ref[...] = dyb
        dgate = jnp.sum(dxo_t * y_ref[...].astype(F32), axis=0, keepdims=True)
        _acc_rows(part_ref, _rows8([dgate], D), pl.program_id(0) == 0)
        dmix = _dot_nt(dyb, w_ref[...])
        ws = _group_weights([r[...] for r in l_refs])
        ov = [r[...] for r in o_refs]
        mixed = functools.reduce(lambda a, b: a + b, [w * o for w, o in zip(ws, ov)])
        for gi in range(ng):
            do = ws[gi] * dmix
            do_refs[gi][...] = do.astype(BF16)
            r_refs[gi][...] = do * mixed

    return pl.pallas_call(
        body, grid=(T // tm,),
        in_specs=[_tok(tm, D), _tok(tm, D), _res((8, D)), _res((GW, D))] + [_tok(tm, GW)] * (2 * ng),
        out_specs=[_tok(tm, D)] + [_tok(tm, GW)] * (2 * ng) + [pl.BlockSpec((8, D), lambda i: (0, 0))],
        out_shape=[_sds((T, D), BF16)] + [_sds((T, GW), BF16)] * ng + [_sds((T, GW), F32)] * ng + [_sds((8, D), F32)],
        compiler_params=_cp("arbitrary"), name="attn_mix_bwd")(dxo, y, vec, w_o, *os_, *ls)


def final_loss(x, gvec, target):
    T, D = x.shape
    tm = min(TOKEN_TILE, T)

    def norm(xv, g):
        return xv * lax.rsqrt(jnp.mean(xv * xv, axis=-1, keepdims=True) + NORM_EPS) * g

    def body(x_ref, g_ref, t_ref, dx_ref, part_ref, loss_ref):
        first = pl.program_id(0) == 0
        yv, vjp = jax.vjp(norm, x_ref[...], g_ref[0:1])
        err = yv - t_ref[...]
        dx, dg = vjp(err * (1.0 / D))
        dx_ref[...] = dx
        _acc_rows(part_ref, _rows8([dg], D), first)
        tile_loss = 0.5 * jnp.sum(jnp.sum(err * err, axis=1, keepdims=True) * (1.0 / D), axis=0, keepdims=True)
        _acc_rows(loss_ref, jnp.broadcast_to(tile_loss, (8, LANES)), first)

    return pl.pallas_call(
        body, grid=(T // tm,),
        in_specs=[_tok(tm, D), _res((8, D)), _tok(tm, D)],
        out_specs=[_tok(tm, D), pl.BlockSpec((8, D), lambda i: (0, 0)), pl.BlockSpec((8, LANES), lambda i: (0, 0))],
        out_shape=[_sds((T, D), F32), _sds((8, D), F32), _sds((8, LANES), F32)],
        compiler_params=_cp("arbitrary"), name="final_loss")(x, gvec, target)


def mods_project(c_all, w, b):
    B, D = c_all.shape
    L, _, N = w.shape

    def body(c_ref, w_ref, b_ref, o_ref):
        cv = c_ref[...]
        cond = cv * _sigmoid(cv)
        o_ref[0] = jnp.dot(cond, w_ref[0], preferred_element_type=F32, precision=lax.Precision.HIGHEST) + b_ref[0]

    return pl.pallas_call(
        body, grid=(L,),
        in_specs=[pl.BlockSpec((B, D), lambda l: (0, 0)), pl.BlockSpec((1, D, N), lambda l: (l, 0, 0)),
                  pl.BlockSpec((1, 1, N), lambda l: (l, 0, 0))],
        out_specs=pl.BlockSpec((1, B, N), lambda l: (l, 0, 0)),
        out_shape=_sds((L, B, N), F32),
        compiler_params=_cp("arbitrary"), name="mods_project")(c_all, w, b)


def mods_weight_grad(c_all, dm):
    B, D = c_all.shape
    L, _, N = dm.shape

    def body(c_ref, d_ref, o_ref):
        cv = c_ref[...]
        cond = cv * _sigmoid(cv)
        o_ref[0] = lax.dot_general(cond, d_ref[0], (((0,), (0,)), ((), ())), preferred_element_type=F32,
                                   precision=lax.Precision.HIGHEST)

    return pl.pallas_call(
        body, grid=(L,),
        in_specs=[pl.BlockSpec((B, D), lambda l: (0, 0)), pl.BlockSpec((1, B, N), lambda l: (l, 0, 0))],
        out_specs=pl.BlockSpec((1, D, N), lambda l: (l, 0, 0)),
        out_shape=_sds((L, D, N), F32),
        compiler_params=_cp("arbitrary"), name="mods_weight_grad")(c_all, dm)


def _adam_math(g, w, m, v):
    m2 = ADAM_B1 * m + (1.0 - ADAM_B1) * g
    v2 = ADAM_B2 * v + (1.0 - ADAM_B2) * (g * g)
    m_hat = m2 / (1.0 - ADAM_B1 ** ADAM_STEP)
    v_hat = v2 / (1.0 - ADAM_B2 ** ADAM_STEP)
    delta = -ADAM_LR * (m_hat / (jnp.sqrt(v_hat) + ADAM_EPS) + ADAM_WD * w)
    return delta, m2, v2


def adam_update(g, w, m, v, parts, name):
    R, C = w.shape
    tr = _pick(R, 256, 8)

    def body(g_ref, w_ref, m_ref, v_ref, go_ref, d_ref, mo_ref, vo_ref):
        if parts:
            gv = g_ref[0].astype(F32)
            for s in range(1, N_DEV):
                gv = gv + g_ref[s].astype(F32)
        else:
            gv = g_ref[...]
        go_ref[...] = gv
        d_ref[...], mo_ref[...], vo_ref[...] = _adam_math(gv, w_ref[...], m_ref[...], v_ref[...])

    gspec = pl.BlockSpec((N_DEV, tr, C), lambda i: (0, i, 0)) if parts else _tok(tr, C)
    return pl.pallas_call(
        body, grid=(R // tr,),
        in_specs=[gspec, _tok(tr, C), _tok(tr, C), _tok(tr, C)],
        out_specs=[_tok(tr, C)] * 4, out_shape=[_sds((R, C), F32)] * 4,
        compiler_params=_cp("arbitrary"), name=name)(g, w, m, v)


def adam_update_layers(parts, w, m, v, name):
    L, R, C = w.shape
    tr = _pick(R, 256, 8)
    nb = R // tr

    def body(*refs):
        p_refs = refs[:L]
        w_ref, m_ref, v_ref, go_ref, d_ref, mo_ref, vo_ref = refs[L:]
        for a in range(L):
            @pl.when(pl.program_id(0) == a)
            def _(a=a):
                gv = p_refs[a][0].astype(F32)
                for s in range(1, N_DEV):
                    gv = gv + p_refs[a][s].astype(F32)
                go_ref[...] = gv
                d_ref[...], mo_ref[...], vo_ref[...] = _adam_math(gv, w_ref[...], m_ref[...], v_ref[...])

    def part_spec(a):
        return pl.BlockSpec((N_DEV, tr, C), lambda l, i: (0, jnp.where(l == a, i, jnp.where(l < a, 0, nb - 1)), 0))

    lay = pl.BlockSpec((None, tr, C), lambda l, i: (l, i, 0))
    return pl.pallas_call(
        body, grid=(L, nb),
        in_specs=[part_spec(a) for a in range(L)] + [lay, lay, lay],
        out_specs=[lay] * 4, out_shape=[_sds((L, R, C), F32)] * 4,
        compiler_params=_cp("arbitrary", "arbitrary"), name=name)(*parts, w, m, v)


def _my_id():
    return 4 * lax.axis_index("x") + 2 * lax.axis_index("y") + lax.axis_index("c")


def _peer(s):
    x, y, c = lax.axis_index("x"), lax.axis_index("y"), lax.axis_index("c")
    px = (1 - x) if s & 4 else x
    py = (1 - y) if s & 2 else y
    pc = (1 - c) if s & 1 else c
    return (px, py, pc), 4 * px + 2 * py + pc


def all_gather(xs, space, name):
    na = len(xs)

    def body(*refs):
        x_refs, o_refs = refs[:na], refs[na:2 * na]
        send_sems, recv_sems, local_sems = refs[2 * na:]
        me = _my_id()
        locals_, sends = [], []
        for a in range(na):
            cp = pltpu.make_async_copy(x_refs[a], o_refs[a].at[me], local_sems.at[a])
            cp.start()
            locals_.append(cp)
        for s in range(1, N_DEV):
            peer, _ = _peer(s)
            for a in range(na):
                cp = pltpu.make_async_remote_copy(
                    src_ref=x_refs[a], dst_ref=o_refs[a].at[me], send_sem=send_sems.at[a, s - 1],
                    recv_sem=recv_sems.at[a, s - 1], device_id=peer, device_id_type=MESH)
                cp.start()
                sends.append(cp)
        for s in range(1, N_DEV):
            peer, pid = _peer(s)
            for a in range(na):
                pltpu.make_async_remote_copy(
                    src_ref=x_refs[a], dst_ref=o_refs[a].at[pid], send_sem=send_sems.at[a, s - 1],
                    recv_sem=recv_sems.at[a, s - 1], device_id=peer, device_id_type=MESH).wait_recv()
        for cp in sends:
            cp.wait_send()
        for cp in locals_:
            cp.wait()

    spec = pl.BlockSpec(memory_space=space)
    return pl.pallas_call(
        body, in_specs=[spec] * na, out_specs=[spec] * na,
        out_shape=[_sds((N_DEV,) + x.shape, x.dtype) for x in xs],
        scratch_shapes=[pltpu.SemaphoreType.DMA((na, N_DEV - 1)), pltpu.SemaphoreType.DMA((na, N_DEV - 1)),
                        pltpu.SemaphoreType.DMA((na,))],
        compiler_params=pltpu.CompilerParams(vmem_limit_bytes=VMEM_LIMIT), name=name)(*xs)


def exchange_slots(xs, name):
    na = len(xs)

    def body(*refs):
        x_refs, o_refs = refs[:na], refs[na:2 * na]
        send_sems, recv_sems, local_sems = refs[2 * na:]
        me = _my_id()
        locals_, sends = [], []
        for a in range(na):
            cp = pltpu.make_async_copy(x_refs[a].at[me], o_refs[a].at[me], local_sems.at[a])
            cp.start()
            locals_.append(cp)
        for s in range(1, N_DEV):
            peer, pid = _peer(s)
            for a in range(na):
                cp = pltpu.make_async_remote_copy(
                    src_ref=x_refs[a].at[pid], dst_ref=o_refs[a].at[me], send_sem=send_sems.at[a, s - 1],
                    recv_sem=recv_sems.at[a, s - 1], device_id=peer, device_id_type=MESH)
                cp.start()
                sends.append(cp)
        for s in range(1, N_DEV):
            peer, pid = _peer(s)
            for a in range(na):
                pltpu.make_async_remote_copy(
                    src_ref=x_refs[a].at[pid], dst_ref=o_refs[a].at[pid], send_sem=send_sems.at[a, s - 1],
                    recv_sem=recv_sems.at[a, s - 1], device_id=peer, device_id_type=MESH).wait_recv()
        for cp in sends:
            cp.wait_send()
        for cp in locals_:
            cp.wait()

    spec = pl.BlockSpec(memory_space=pl.ANY)
    return pl.pallas_call(
        body, in_specs=[spec] * na, out_specs=[spec] * na,
        out_shape=[_sds(x.shape, x.dtype) for x in xs],
        scratch_shapes=[pltpu.SemaphoreType.DMA((na, N_DEV - 1)), pltpu.SemaphoreType.DMA((na, N_DEV - 1)),
                        pltpu.SemaphoreType.DMA((na,))],
        compiler_params=pltpu.CompilerParams(vmem_limit_bytes=VMEM_LIMIT), name=name)(*xs)


_HBM = pl.BlockSpec(memory_space=pltpu.HBM)
_SEM = pl.BlockSpec(memory_space=pltpu.SEMAPHORE)
_EFFECT = pltpu.SideEffectType.DATAFLOW_SIDE_EFFECTING


def _split_copy(x_ref, land_ref, s, send_sem, recv_sem, scatter):
    peer, pid = _peer(s)
    src = x_ref.at[pid] if scatter else x_ref
    return pltpu.make_async_remote_copy(src_ref=src, dst_ref=land_ref.at[_my_id()], send_sem=send_sem, recv_sem=recv_sem,
                                        device_id=peer, device_id_type=MESH)


def comm_start(xs, scatter, name):
    na = len(xs)
    me = _my_id()
    lands = []
    for x in xs:
        shape = x.shape if scatter else (N_DEV,) + x.shape
        own = lax.dynamic_slice_in_dim(x, me, 1, 0) if scatter else x[None]
        lands.append(lax.dynamic_update_slice(lax.empty(shape, x.dtype), own, (me,) + (0,) * (len(shape) - 1)))

    def body(*refs):
        x_refs, land_refs = refs[:na], refs[na:2 * na]
        send_sem, recv_sem = refs[2 * na], refs[2 * na + 1]
        token = refs[-1]
        for s in range(1, N_DEV):
            for a in range(na):
                _split_copy(x_refs[a], land_refs[a], s, send_sem, recv_sem, scatter).start()
        token[...] = jnp.zeros_like(token)

    outs = pl.pallas_call(
        body, name=name,
        out_shape=(pltpu.SemaphoreType.DMA(()), pltpu.SemaphoreType.DMA(()))
        + tuple(pltpu.HBM(x.shape, x.dtype) for x in xs) + tuple(pltpu.HBM(l.shape, l.dtype) for l in lands)
        + (_sds((8, LANES), F32),),
        in_specs=(_HBM,) * (2 * na),
        out_specs=(_SEM, _SEM) + (_HBM,) * (2 * na) + (pl.BlockSpec(memory_space=pltpu.VMEM),),
        input_output_aliases={a: 2 + a for a in range(2 * na)},
        compiler_params=pltpu.CompilerParams(has_side_effects=_EFFECT),
    )(*[pltpu.with_memory_space_constraint(x, pltpu.HBM) for x in xs],
      *[pltpu.with_memory_space_constraint(l, pltpu.HBM) for l in lands])
    return dict(sems=outs[0:2], xs=outs[2:2 + na], lands=outs[2 + na:2 + 2 * na], token=outs[-1], scatter=scatter)


def comm_wait(started, after, name):
    xs, lands = started["xs"], started["lands"]
    scatter = started["scatter"]
    na = len(xs)

    def body(*refs):
        x_refs, land_refs = refs[:na], refs[na:2 * na]
        send_sem, recv_sem = refs[2 * na], refs[2 * na + 1]
        for s in range(1, N_DEV):
            for a in range(na):
                cp = _split_copy(x_refs[a], land_refs[a], s, send_sem, recv_sem, scatter)
                cp.wait_send()
                cp.wait_recv()

    outs = pl.pallas_call(
        body, name=name,
        out_shape=tuple(pltpu.HBM(x.shape, x.dtype) for x in xs) + tuple(pltpu.HBM(l.shape, l.dtype) for l in lands),
        in_specs=(_HBM,) * (2 * na) + (_SEM, _SEM, pl.BlockSpec(memory_space=pl.ANY)),
        out_specs=(_HBM,) * (2 * na),
        input_output_aliases={a: a for a in range(2 * na)},
        compiler_params=pltpu.CompilerParams(has_side_effects=_EFFECT),
    )(*xs, *lands, *started["sems"], after)
    return list(outs[na:])


def _cols_to_natural(g):
    return jnp.concatenate([g[k] for k in range(N_DEV)], axis=1)


def _cols_to_slots(w):
    ns = w.shape[1] // N_DEV
    return jnp.stack([w[:, k * ns:(k + 1) * ns] for k in range(N_DEV)])


def _vec8(rows, d):
    rows = [r.reshape(1, d).astype(F32) for r in rows]
    return jnp.concatenate(rows + [jnp.zeros((8 - len(rows), d), F32)], axis=0)


def _ffn_forward(x, vec, w_in, w_out):
    h, a, b, u = ffn_up(x, vec, w_in)
    xn, y = proj_out(u, x, vec, w_out, FFN_RES_WEIGHT, "ffn_down")
    return xn, (x, h, a, b, u, y)


def _ffn_backward(dxo, saved, vec, w_in, w_out):
    x, h, a, b, u, y = saved
    dy, da, db, part_gate = ffn_down_bwd(dxo, y, vec, w_out, a, b)
    dx, part_norm = ffn_up_bwd(da, db, w_in, x, dxo, vec)
    g_out = mm_tn(u, dy, "ffn_dw_out")
    g_in = jnp.concatenate([mm_tn(h, da, "ffn_dw_a"), mm_tn(h, db, "ffn_dw_b")], axis=1)
    rows = jnp.concatenate([part_norm[0:3], part_gate[0:1]], axis=0)
    return dx, g_in, g_out, rows


def _layer_weight_names(layer):
    names = ["ffn1_w_in", "ffn1_w_out", "ffn2_w_in", "ffn2_w_out"]
    names += ["conv_w_in", "conv_w_out"] if layer < N_A_LAYERS else ["attn_w_q", "attn_w_o"]
    if layer == N_A_LAYERS:
        names.append("w_kv")
    return names


class LayerComm:
    def __init__(self, wts):
        self.wts = wts
        self.received = {}

    def _shard(self, name, layer):
        w = self.wts[name]
        if name == "w_kv":
            return w
        return w[layer - N_A_LAYERS] if name.startswith("attn") else w[layer]

    def start_gather(self, layer):
        xs = [self._shard(n, layer).astype(BF16) for n in _layer_weight_names(layer)]
        return comm_start(xs, False, f"gather_start_l{layer}")

    def finish_gather(self, layer, started, after):
        lands = comm_wait(started, after, f"gather_wait_l{layer}")
        W = {}
        for n, g in zip(_layer_weight_names(layer), lands):
            W[n] = _cols_to_natural(g) if n in _COL_SHARDED else g.reshape(-1, g.shape[2])
        return W

    def start_exchange(self, layer, G):
        xs = []
        for n in _layer_weight_names(layer):
            g = G[n]
            xs.append((_cols_to_slots(g) if n in _COL_SHARDED else g.reshape(N_DEV, -1, g.shape[1])).astype(BF16))
        return comm_start(xs, True, f"exchange_start_l{layer}")

    def finish_exchange(self, layer, started, after):
        lands = comm_wait(started, after, f"exchange_wait_l{layer}")
        for n, r in zip(_layer_weight_names(layer), lands):
            self.received[(n, layer)] = r

    def parts(self, name):
        if name == "w_kv":
            return [self.received[(name, N_A_LAYERS)]]
        layers = range(N_A_LAYERS) if name.startswith("conv") else (
            range(N_A_LAYERS, DEPTH) if name.startswith("attn") else range(DEPTH))
        return [self.received[(name, l)] for l in layers]


def device_step(x, positions, target, mods, kvmods, small, comm):
    T, D = x.shape
    groups = DILATED_GROUPS
    lane = jnp.arange(LANES) % HEAD_DIM
    inv = ROPE_THETA ** (-jnp.arange(0, ROPE_DIM, 2, dtype=F32) / ROPE_DIM)
    lane_rows = _vec8([jnp.where(lane < ROPE_DIM, inv[lane % (ROPE_DIM // 2)], 0.0), lane < ROPE_DIM,
                       (lane >= ROPE_DIM // 2) & (lane < ROPE_DIM), lane < ROPE_DIM // 2], LANES)
    tabs = rope_tables(positions.reshape(T, 1), lane_rows)

    def after_token(v, token):
        return v if token is None else v + token[0, 0]

    def vec_of(layer, sub, token=None):
        return after_token(_vec8([small["norm_g"][layer, sub], mods[layer, 3 * sub], mods[layer, 3 * sub + 1],
                                  mods[layer, 3 * sub + 2]], D), token)

    saved = []
    kv_saved = None
    k_sh = v_sh = None
    qw = GROUP_WIDTH * len(groups)
    started = comm.start_gather(0)
    weights = [None] * DEPTH
    weights[0] = comm.finish_gather(0, started, started["token"])
    for layer in range(DEPTH):
        W = weights[layer]
        token = None
        if layer + 1 < DEPTH:
            started = comm.start_gather(layer + 1)
            token = started["token"]
        if layer == N_A_LAYERS:
            kv_vec = after_token(_vec8([small["kv_norm_g"], kvmods[0], kvmods[1]], D), token)
            h_kv, kvp = proj_rope_fwd(x, kv_vec, W["w_kv"], tabs, qw, "kv_fwd")
            k_sh, v_sh = kvp[:, :qw], kvp[:, qw:]
            kv_saved = (x, h_kv, kv_vec)
        rec = {}
        v1 = vec_of(layer, 0, token)
        x, rec["ffn1"] = _ffn_forward(x, v1, W["ffn1_w_in"], W["ffn1_w_out"])
        v2 = vec_of(layer, 1)
        if layer < N_A_LAYERS:
            cw = _vec8(list(small["conv_w"][layer]), D)
            x_in = x
            x, h, bcu, cv, z, y = conv_fwd(x, v2, cw, W["conv_w_in"], W["conv_w_out"])
            rec["mix"] = (x_in, h, bcu, cv, z, y, cw)
        else:
            x_in = x
            h, q = proj_rope_fwd(x, v2, W["attn_w_q"], tabs, qw, "q_fwd")
            os_, ls = [], []
            for g, (win, dil) in enumerate(groups):
                o, l = attn_core_fwd(q, k_sh, v_sh, g, win // dil, dil)
                os_.append(o)
                ls.append(l)
            x, mixed, y = attn_mix_out(os_, ls, x, v2, W["attn_w_o"])
            rec["mix"] = (x_in, h, q, os_, ls, mixed, y)
        v3 = vec_of(layer, 2)
        x, rec["ffn2"] = _ffn_forward(x, v3, W["ffn2_w_in"], W["ffn2_w_out"])
        rec["vecs"] = (v1, v2, v3)
        saved.append(rec)
        if layer + 1 < DEPTH:
            weights[layer + 1] = comm.finish_gather(layer + 1, started, x)

    dx, part_final, loss_tile = final_loss(x, _vec8([small["final_norm_g"]], D), target)
    loss = loss_tile[0, 0]

    conv_rows = [None] * N_A_LAYERS
    kv_rows = None
    mod_rows = [[None] * 3 for _ in range(DEPTH)]
    dkv_pairs = [{"k": [], "v": []} for _ in groups]
    pending = None
    for layer in reversed(range(DEPTH)):
        rec = saved[layer]
        W = weights[layer]
        G = {}
        v1, v2, v3 = rec["vecs"]
        if pending is not None:
            v3 = after_token(v3, pending[1]["token"])
        dx, G["ffn2_w_in"], G["ffn2_w_out"], mod_rows[layer][2] = _ffn_backward(dx, rec["ffn2"], v3, W["ffn2_w_in"], W["ffn2_w_out"])
        if layer < N_A_LAYERS:
            x_in, h, bcu, cv, z, y, cw = rec["mix"]
            dx, dy, dbcu, part, dcw = conv_bwd(dx, x_in, y, bcu, cv, v2, cw, W["conv_w_in"], W["conv_w_out"])
            G["conv_w_out"] = mm_tn(z, dy, "conv_dw_out")
            G["conv_w_in"] = mm_tn(h, dbcu, "conv_dw_in")
            conv_rows[layer] = dcw[0:3]
            mod_rows[layer][1] = part[0:4]
        else:
            x_in, h, q, os_, ls, mixed, y = rec["mix"]
            outs = attn_mix_bwd(dx, y, v2, W["attn_w_o"], os_, ls)
            ng = len(groups)
            dy, dos, rrs, part_gate = outs[0], outs[1:1 + ng], outs[1 + ng:1 + 2 * ng], outs[1 + 2 * ng]
            G["attn_w_o"] = mm_tn(mixed, dy, "attn_dw_o")
            dqs = []
            for g, (win, dil) in enumerate(groups):
                dq, dkc, dkp, dvc, dvp = attn_core_bwd(q, k_sh, v_sh, dos[g], rrs[g], ls[g], g, win // dil, dil)
                dqs.append(dq)
                dkv_pairs[g]["k"].append((dkc, dkp))
                dkv_pairs[g]["v"].append((dvc, dvp))
            dx, dqr, part_norm = proj_rope_bwd(dqs, x_in, dx, v2, W["attn_w_q"], tabs, qw, "q_bwd")
            G["attn_w_q"] = mm_tn(h, dqr, "attn_dw_q")
            mod_rows[layer][1] = jnp.concatenate([part_norm[0:3], part_gate[0:1]], axis=0)
        dx, G["ffn1_w_in"], G["ffn1_w_out"], mod_rows[layer][0] = _ffn_backward(dx, rec["ffn1"], v1, W["ffn1_w_in"], W["ffn1_w_out"])
        if layer == N_A_LAYERS:
            x_kv, h_kv, kv_vec = kv_saved
            dparts = [dkv_combine(dkv_pairs[g]["k"], win // dil, dil, f"dk_combine_g{g}") for g, (win, dil) in enumerate(groups)]
            dparts += [dkv_combine(dkv_pairs[g]["v"], win // dil, dil, f"dv_combine_g{g}") for g, (win, dil) in enumerate(groups)]
            dx, dkvp, part_kv = proj_rope_bwd(dparts, x_kv, dx, kv_vec, W["w_kv"], tabs, qw, "kv_bwd")
            G["w_kv"] = mm_tn(h_kv, dkvp, "kv_dw")
            kv_rows = part_kv[0:3]
        if pending is not None:
            comm.finish_exchange(pending[0], pending[1], dx)
        pending = (layer, comm.start_exchange(layer, G))
    comm.finish_exchange(pending[0], pending[1], dx)

    grads = {"conv_w": jnp.stack(conv_rows), "kv_rows": kv_rows}
    grads["final_norm_g"] = part_final[0]
    rows = jnp.stack([jnp.stack(r) for r in mod_rows])
    grads["norm_g"] = rows[:, :, 0]
    grads["mods"] = rows[:, :, 1:4].reshape(DEPTH, N_MOD, D)
    return loss, dx, grads


_COL_SHARDED = ("ffn1_w_in", "ffn2_w_in", "conv_w_in", "w_kv", "attn_w_q", "attn_w_o")
_ROW_SHARDED = ("ffn1_w_out", "ffn2_w_out", "conv_w_out")
_BIG = _COL_SHARDED + _ROW_SHARDED


def _flat2(a):
    return a.reshape(-1, a.shape[-1])


def _pad_rows(a, mult):
    r = a.shape[0]
    pad = (-r) % mult
    return a if pad == 0 else jnp.concatenate([a, jnp.zeros((pad,) + a.shape[1:], a.dtype)], axis=0)


def kernel(x, c, positions, norm_g, ada_w, ada_b, ffn1_w_in, ffn1_w_out, ffn2_w_in, ffn2_w_out, conv_w_in, conv_w, conv_w_out, kv_norm_g, kv_ada_w, kv_ada_b, w_kv, attn_w_q, attn_w_o, final_norm_g, loss_target, m_norm_g, m_ada_w, m_ada_b, m_ffn1_w_in, m_ffn1_w_out, m_ffn2_w_in, m_ffn2_w_out, m_conv_w_in, m_conv_w, m_conv_w_out, m_kv_norm_g, m_kv_ada_w, m_kv_ada_b, m_w_kv, m_attn_w_q, m_attn_w_o, m_final_norm_g, v_norm_g, v_ada_w, v_ada_b, v_ffn1_w_in, v_ffn1_w_out, v_ffn2_w_in, v_ffn2_w_out, v_conv_w_in, v_conv_w, v_conv_w_out, v_kv_norm_g, v_kv_ada_w, v_kv_ada_b, v_w_kv, v_attn_w_q, v_attn_w_o, v_final_norm_g):
    names = ("norm_g", "ada_w", "ada_b", "ffn1_w_in", "ffn1_w_out", "ffn2_w_in", "ffn2_w_out", "conv_w_in", "conv_w",
             "conv_w_out", "kv_norm_g", "kv_ada_w", "kv_ada_b", "w_kv", "attn_w_q", "attn_w_o", "final_norm_g")
    wts = dict(zip(names, (norm_g, ada_w, ada_b, ffn1_w_in, ffn1_w_out, ffn2_w_in, ffn2_w_out, conv_w_in, conv_w, conv_w_out,
                           kv_norm_g, kv_ada_w, kv_ada_b, w_kv, attn_w_q, attn_w_o, final_norm_g)))
    mom = dict(zip(names, (m_norm_g, m_ada_w, m_ada_b, m_ffn1_w_in, m_ffn1_w_out, m_ffn2_w_in, m_ffn2_w_out, m_conv_w_in,
                           m_conv_w, m_conv_w_out, m_kv_norm_g, m_kv_ada_w, m_kv_ada_b, m_w_kv, m_attn_w_q, m_attn_w_o,
                           m_final_norm_g)))
    var = dict(zip(names, (v_norm_g, v_ada_w, v_ada_b, v_ffn1_w_in, v_ffn1_w_out, v_ffn2_w_in, v_ffn2_w_out, v_conv_w_in,
                           v_conv_w, v_conv_w_out, v_kv_norm_g, v_kv_ada_w, v_kv_ada_b, v_w_kv, v_attn_w_q, v_attn_w_o,
                           v_final_norm_g)))
    T, D = x.shape[1], x.shape[2]
    me = _my_id()
    nmod = ada_w.shape[2]
    nkv = kv_ada_w.shape[1]

    comm = LayerComm(wts)
    W = {}

    ds = norm_g.shape[2]
    small = jnp.concatenate([c.reshape(-1), norm_g.reshape(-1), conv_w.reshape(-1)]).astype(F32)
    n_small = small.shape[0]
    small = _pad_rows(small.reshape(-1, 1), 8 * LANES).reshape(-1, LANES)
    (small_all,) = all_gather([small], pltpu.VMEM, "gather_small")
    small_all = small_all.reshape(N_DEV, -1)[:, :n_small]
    c_all = small_all[:, :D]
    def full_rows(off, count):
        return jnp.stack([small_all[:, off + i * ds:off + (i + 1) * ds].reshape(D) for i in range(count)])

    W["norm_g"] = full_rows(D, DEPTH * 3).reshape(DEPTH, 3, D)
    W["conv_w"] = full_rows(D + DEPTH * 3 * ds, N_A_LAYERS * 3).reshape(N_A_LAYERS, 3, D)
    W["kv_norm_g"], W["final_norm_g"] = kv_norm_g, final_norm_g

    ada_b_mine = lax.dynamic_slice_in_dim(ada_b, me * nmod, nmod, axis=1).reshape(DEPTH, 1, nmod)
    kv_b_mine = lax.dynamic_slice_in_dim(kv_ada_b, me * nkv, nkv, axis=0).reshape(1, 1, nkv)
    mods_cols = mods_project(c_all, ada_w, ada_b_mine)
    kv_cols = mods_project(c_all, kv_ada_w.reshape(1, D, nkv), kv_b_mine)
    mcat = jnp.concatenate([mods_cols[l] for l in range(DEPTH)] + [kv_cols[0]], axis=1)
    wm = mcat.shape[1]
    if wm % LANES:
        mcat = jnp.concatenate([mcat, jnp.zeros((N_DEV, LANES - wm % LANES), F32)], axis=1)
    (mods_all,) = exchange_slots([mcat.reshape(N_DEV, 1, -1)], "exchange_mods")
    mods_all = mods_all.reshape(N_DEV, -1)
    mods = jnp.stack([mods_all[:, l * nmod:(l + 1) * nmod].reshape(N_MOD, D) for l in range(DEPTH)])
    kvmods = mods_all[:, DEPTH * nmod:DEPTH * nmod + nkv].reshape(2, D)

    loss_local, dx, grads = device_step(x[0], positions[0], loss_target[0], mods, kvmods, W, comm)
    loss = lax.psum(loss_local, MESH_AXES)

    dmods = grads["mods"].reshape(-1)
    dkvm = grads["kv_rows"][1:3].reshape(-1)
    vecs = jnp.concatenate([dmods, dkvm, grads["kv_rows"][0], grads["final_norm_g"], grads["norm_g"].reshape(-1),
                            grads["conv_w"].reshape(-1)])
    n_vec = vecs.shape[0]
    vecs = _pad_rows(vecs.reshape(-1, 1), 8 * LANES).reshape(-1, LANES)
    (vec_all,) = all_gather([vecs], pltpu.VMEM, "gather_vector_grads")
    vec_all = vec_all.reshape(N_DEV, -1)[:, :n_vec]
    nm_, nk_ = DEPTH * N_MOD * D, 2 * D
    dmods_all = vec_all[:, :nm_].reshape(N_DEV, DEPTH, N_MOD * D)
    dkvm_all = vec_all[:, nm_:nm_ + nk_]
    rest = vec_all[:, nm_ + nk_:]
    parts_kv_norm, parts_final = rest[:, :D].reshape(N_DEV, 1, D), rest[:, D:2 * D].reshape(N_DEV, 1, D)
    parts_norm = lax.dynamic_slice_in_dim(rest[:, 2 * D:2 * D + DEPTH * 3 * D].reshape(N_DEV, DEPTH * 3, D), me * ds, ds, axis=2)
    parts_conv = lax.dynamic_slice_in_dim(rest[:, 2 * D + DEPTH * 3 * D:].reshape(N_DEV, N_A_LAYERS * 3, D), me * ds, ds, axis=2)
    dm_cols = lax.dynamic_slice_in_dim(dmods_all, me * nmod, nmod, axis=2)
    dm_mine = jnp.stack([dm_cols[:, l] for l in range(DEPTH)])
    dkv_mine = lax.dynamic_slice_in_dim(dkvm_all, me * nkv, nkv, axis=1).reshape(1, N_DEV, nkv)
    g_ada_w = mods_weight_grad(c_all, dm_mine)
    g_kv_ada_w = mods_weight_grad(c_all, dkv_mine)[0]

    out_g, out_d, out_m, out_v = {}, {}, {}, {}

    def update(n, g, w, parts=False):
        shp = w.shape
        w2 = w.reshape(1, -1) if w.ndim == 1 else _flat2(w)
        g2 = g if parts else g.reshape(w2.shape)
        res = adam_update(g2, w2, mom[n].reshape(w2.shape), var[n].reshape(w2.shape), parts, "adam_" + n)
        out_g[n], out_d[n], out_m[n], out_v[n] = (r.reshape(shp) for r in res)

    for n in _BIG:
        shp = wts[n].shape
        shp3 = (1,) + shp if len(shp) == 2 else shp
        res = adam_update_layers(comm.parts(n), wts[n].reshape(shp3), mom[n].reshape(shp3), var[n].reshape(shp3), "adam_" + n)
        out_g[n], out_d[n], out_m[n], out_v[n] = (r.reshape(shp) for r in res)
    update("ada_w", g_ada_w, ada_w)
    update("kv_ada_w", g_kv_ada_w, kv_ada_w)
    update("ada_b", dmods_all, ada_b, True)
    update("kv_ada_b", dkvm_all.reshape(N_DEV, 1, nk_), kv_ada_b, True)
    update("kv_norm_g", parts_kv_norm, kv_norm_g, True)
    update("final_norm_g", parts_final, final_norm_g, True)
    update("norm_g", parts_norm, norm_g, True)
    update("conv_w", parts_conv, conv_w, True)

    return (loss, dx.reshape(x.shape), *[out_g[n] for n in names], *[out_d[n] for n in names],
            *[out_m[n] for n in names], *[out_v[n] for n in names])
```

```python
import functools

import jax
import jax.numpy as jnp
from jax import lax
from jax.experimental import pallas as pl
from jax.experimental.pallas import tpu as pltpu

F32, BF16 = jnp.float32, jnp.bfloat16

N_DEV = 8
MESH_AXES = ("x", "y", "c")
DEPTH = 4
N_A_LAYERS = 2
HEAD_DIM = 64
HEADS_PER_GROUP = 8
GROUP_WIDTH = HEAD_DIM * HEADS_PER_GROUP
DILATED_GROUPS = ((128, 1), (512, 4), (2048, 16))
ROPE_DIM = HEAD_DIM // 4
ROPE_THETA = 500000.0
NORM_EPS = 1e-5
FFN_RES_WEIGHT = 0.5
N_MOD = 9
ADAM_LR, ADAM_B1, ADAM_B2, ADAM_EPS, ADAM_WD, ADAM_STEP = 0.001, 0.9, 0.999, 1e-08, 0.01, 10

LANES = 128
TOKEN_TILE = 512
CONTRACT_TILE = 2048
MXU_WIDTH = 256
VMEM_LIMIT = 56 * 1024 * 1024
MESH = pl.DeviceIdType.MESH


def _cp(*sem):
    return pltpu.CompilerParams(dimension_semantics=sem, vmem_limit_bytes=VMEM_LIMIT)


def _pick(n, cap, mult=LANES):
    if n <= cap:
        return n
    best = None
    for t in range(mult, cap + 1, mult):
        if n % t == 0:
            best = t
    assert best is not None, (n, cap)
    return best


def _tok(tm, w):
    return pl.BlockSpec((tm, w), lambda i: (i, 0))


def _res(shape):
    nd = len(shape)
    return pl.BlockSpec(shape, lambda *_: (0,) * nd, pipeline_mode=pl.Buffered(1))


def _sds(shape, dt):
    return jax.ShapeDtypeStruct(shape, dt)


def _sigmoid(a):
    return 1.0 / (1.0 + jnp.exp(-a))


def _modnorm(x, g, sh, sc):
    r = lax.rsqrt(jnp.mean(x * x, axis=-1, keepdims=True) + NORM_EPS)
    return (x * r * g) * (1.0 + sc) + sh


def _dot(a, b):
    return jnp.dot(a, b, preferred_element_type=F32)


def _dot_nt(a, b):
    return lax.dot_general(a, b, (((1,), (1,)), ((), ())), preferred_element_type=F32)


def _dot_tn(a, b):
    return lax.dot_general(a, b, (((0,), (0,)), ((), ())), preferred_element_type=F32)


def _rows8(rows, d):
    pad = 8 - len(rows)
    return jnp.concatenate(list(rows) + [jnp.zeros((pad, d), F32)], axis=0)


def _acc_rows(ref, tile, first):
    @pl.when(first)
    def _():
        ref[...] = tile

    @pl.when(jnp.logical_not(first))
    def _():
        ref[...] += tile


def ffn_up(x, vec, w_in_t):
    T, D = x.shape
    F = w_in_t.shape[0] // 2
    tm, cw = min(TOKEN_TILE, T), _pick(F, MXU_WIDTH)

    def body(x_ref, vec_ref, w_ref, h_ref, a_ref, b_ref, u_ref):
        hb = _modnorm(x_ref[...], vec_ref[0:1], vec_ref[1:2], vec_ref[2:3]).astype(BF16)
        h_ref[...] = hb
        for c in range(F // cw):
            lo, hi = c * cw, (c + 1) * cw
            a = _dot_nt(hb, w_ref[lo:hi, :])
            b = _dot_nt(hb, w_ref[F + lo:F + hi, :])
            a_ref[:, lo:hi] = a.astype(BF16)
            b_ref[:, lo:hi] = b.astype(BF16)
            u_ref[:, lo:hi] = (a * _sigmoid(a) * b).astype(BF16)

    return pl.pallas_call(
        body, grid=(T // tm,),
        in_specs=[_tok(tm, D), _res((8, D)), _res((2 * F, D))],
        out_specs=[_tok(tm, D), _tok(tm, F), _tok(tm, F), _tok(tm, F)],
        out_shape=[_sds((T, D), BF16), _sds((T, F), BF16), _sds((T, F), BF16), _sds((T, F), BF16)],
        compiler_params=_cp("arbitrary"), name="ffn_up")(x, vec, w_in_t)


def proj_out(u, x, vec, w_out, res_weight, name):
    T, D = x.shape
    K = u.shape[1]
    tm = min(TOKEN_TILE, T)

    def body(u_ref, x_ref, vec_ref, w_ref, xn_ref, y_ref):
        y = _dot(u_ref[...], w_ref[...])
        y_ref[...] = y.astype(BF16)
        xn_ref[...] = x_ref[...] + (res_weight * (1.0 + vec_ref[3:4])) * y

    return pl.pallas_call(
        body, grid=(T // tm,),
        in_specs=[_tok(tm, K), _tok(tm, D), _res((8, D)), _res((K, D))],
        out_specs=[_tok(tm, D), _tok(tm, D)],
        out_shape=[_sds((T, D), F32), _sds((T, D), BF16)],
        compiler_params=_cp("arbitrary"), name=name)(u, x, vec, w_out)


def ffn_down_bwd(dxo, y, vec, w_out, a, b):
    T, D = dxo.shape
    F = a.shape[1]
    tm, cw = min(TOKEN_TILE, T), _pick(F, MXU_WIDTH)

    def body(dxo_ref, y_ref, vec_ref, w_ref, a_ref, b_ref, dy_ref, dab_ref, part_ref):
        dxo_t = dxo_ref[...]
        dyb = (dxo_t * (FFN_RES_WEIGHT * (1.0 + vec_ref[3:4]))).astype(BF16)
        dy_ref[...] = dyb
        dgate = FFN_RES_WEIGHT * jnp.sum(dxo_t * y_ref[...].astype(F32), axis=0, keepdims=True)
        _acc_rows(part_ref, _rows8([dgate], D), pl.program_id(0) == 0)
        for c in range(F // cw):
            lo, hi = c * cw, (c + 1) * cw
            du = _dot_nt(dyb, w_ref[lo:hi, :])
            av = a_ref[:, lo:hi].astype(F32)
            bv = b_ref[:, lo:hi].astype(F32)
            sg = _sigmoid(av)
            dab_ref[:, lo:hi] = (du * bv * (sg * (1.0 + av * (1.0 - sg)))).astype(BF16)
            dab_ref[:, F + lo:F + hi] = (du * (av * sg)).astype(BF16)

    return pl.pallas_call(
        body, grid=(T // tm,),
        in_specs=[_tok(tm, D), _tok(tm, D), _res((8, D)), _res((F, D)), _tok(tm, F), _tok(tm, F)],
        out_specs=[_tok(tm, D), _tok(tm, 2 * F), pl.BlockSpec((8, D), lambda i: (0, 0))],
        out_shape=[_sds((T, D), BF16), _sds((T, 2 * F), BF16), _sds((8, D), F32)],
        compiler_params=_cp("arbitrary"), name="ffn_down_bwd")(dxo, y, vec, w_out, a, b)


def ffn_up_bwd(dab, w_in_t, x, dxo, vec):
    T, D = x.shape
    F2 = dab.shape[1]
    tm = min(TOKEN_TILE, T)

    def body(dab_ref, w_ref, x_ref, dxo_ref, vec_ref, dx_ref, part_ref):
        dh = _dot(dab_ref[...], w_ref[...])
        _, vjp = jax.vjp(_modnorm, x_ref[...], vec_ref[0:1], vec_ref[1:2], vec_ref[2:3])
        dx, dg, dsh, dsc = vjp(dh)
        dx_ref[...] = dxo_ref[...] + dx
        _acc_rows(part_ref, _rows8([dg, dsh, dsc], D), pl.program_id(0) == 0)

    return pl.pallas_call(
        body, grid=(T // tm,),
        in_specs=[_tok(tm, F2), _res((F2, D)), _tok(tm, D), _tok(tm, D), _res((8, D))],
        out_specs=[_tok(tm, D), pl.BlockSpec((8, D), lambda i: (0, 0))],
        out_shape=[_sds((T, D), F32), _sds((8, D), F32)],
        compiler_params=_cp("arbitrary"), name="ffn_up_bwd")(dab, w_in_t, x, dxo, vec)


def grad_slots(a, b, name, col_slots=False):
    T, M = a.shape
    N = b.shape[1]
    tk = min(CONTRACT_TILE, T)
    nk = T // tk
    tmm = _pick(M, 1408)
    if col_slots:
        ns = N // N_DEV
        sp = max(s for s in (1, 2, 4, 8) if ns * s <= 1536)
        tn = ns * sp
    else:
        tn = _pick(N, 1536)

    def body(a_ref, b_ref, o_ref, acc):
        k = pl.program_id(2)
        t = _dot_tn(a_ref[...], b_ref[...])

        @pl.when(k == 0)
        def _():
            acc[...] = t

        @pl.when(k > 0)
        def _():
            acc[...] += t

        @pl.when(k == nk - 1)
        def _():
            if col_slots:
                for s in range(sp):
                    o_ref[s] = acc[:, s * ns:(s + 1) * ns].astype(BF16)
            else:
                o_ref[...] = acc[...].astype(BF16)

    if col_slots:
        out_spec, out_shape = pl.BlockSpec((sp, tmm, ns), lambda i, j, k: (j, i, 0)), _sds((N_DEV, M, ns), BF16)
    else:
        out_spec, out_shape = pl.BlockSpec((tmm, tn), lambda i, j, k: (i, j)), _sds((M, N), BF16)
    out = pl.pallas_call(
        body, grid=(M // tmm, N // tn, nk),
        in_specs=[pl.BlockSpec((tk, tmm), lambda i, j, k: (k, i)), pl.BlockSpec((tk, tn), lambda i, j, k: (k, j))],
        out_specs=out_spec, out_shape=out_shape,
        scratch_shapes=[pltpu.VMEM((tmm, tn), F32)],
        compiler_params=_cp("arbitrary", "arbitrary", "arbitrary"), name=name)(a, b)
    return out if col_slots else out.reshape(N_DEV, M // N_DEV, N)


def conv_fwd(x, vec, cw, w_in, w_out):
    T, D = x.shape
    tm = min(TOKEN_TILE, T)

    def body(x_ref, vec_ref, cw_ref, wi_ref, wo_ref, xn_ref, h_ref, bcu_ref, cv_ref, z_ref, y_ref, vbuf):
        @pl.when(pl.program_id(0) == 0)
        def _():
            vbuf[0:8, :] = jnp.zeros((8, D), F32)

        x_t = x_ref[...]
        hb = _modnorm(x_t, vec_ref[0:1], vec_ref[1:2], vec_ref[2:3]).astype(BF16)
        h_ref[...] = hb
        bcu = _dot(hb, wi_ref[...])
        bcu_ref[...] = bcu.astype(BF16)
        bg, v = bcu[:, 0:D], bcu[:, D:2 * D] * bcu[:, 2 * D:3 * D]
        vbuf[8:8 + tm, :] = v
        conv = cw_ref[0:1] * vbuf[6:6 + tm, :] + cw_ref[1:2] * vbuf[7:7 + tm, :] + cw_ref[2:3] * v
        cv_ref[...] = conv.astype(BF16)
        zb = (bg * conv).astype(BF16)
        z_ref[...] = zb
        y = _dot(zb, wo_ref[...])
        y_ref[...] = y.astype(BF16)
        xn_ref[...] = x_t + (1.0 + vec_ref[3:4]) * y
        vbuf[0:8, :] = vbuf[tm:tm + 8, :]

    return pl.pallas_call(
        body, grid=(T // tm,),
        in_specs=[_tok(tm, D), _res((8, D)), _res((8, D)), _res((D, 3 * D)), _res((D, D))],
        out_specs=[_tok(tm, D), _tok(tm, D), _tok(tm, 3 * D), _tok(tm, D), _tok(tm, D), _tok(tm, D)],
        out_shape=[_sds((T, D), F32), _sds((T, D), BF16), _sds((T, 3 * D), BF16), _sds((T, D), BF16),
                   _sds((T, D), BF16), _sds((T, D), BF16)],
        scratch_shapes=[pltpu.VMEM((tm + 8, D), F32)],
        compiler_params=_cp("arbitrary"), name="conv_fwd")(x, vec, cw, w_in, w_out)


def conv_bwd(dxo, x, y, bcu, cv, vec, cw, w_in, w_out):
    T, D = x.shape
    tm = min(TOKEN_TILE, T)
    nt = T // tm

    def body(dxo_ref, x_ref, y_ref, bcu_ref, cv_ref, vec_ref, cw_ref, wi_ref, wo_ref,
             dx_ref, dy_ref, dbcu_ref, part_ref, dcw_ref, dcbuf):
        first = pl.program_id(0) == 0

        @pl.when(first)
        def _():
            dcbuf[tm:tm + 8, :] = jnp.zeros((8, D), F32)

        dxo_t = dxo_ref[...]
        dyb = (dxo_t * (1.0 + vec_ref[3:4])).astype(BF16)
        dy_ref[...] = dyb
        dgate = jnp.sum(dxo_t * y_ref[...].astype(F32), axis=0, keepdims=True)
        dz = _dot_nt(dyb, wo_ref[...])
        bcu_t = bcu_ref[...].astype(F32)
        bg, cg, ug = bcu_t[:, 0:D], bcu_t[:, D:2 * D], bcu_t[:, 2 * D:3 * D]
        dconv = dz * bg
        dbg = dz * cv_ref[...].astype(F32)
        dcbuf[0:tm, :] = dconv
        d1, d2 = dcbuf[1:tm + 1, :], dcbuf[2:tm + 2, :]
        dv = cw_ref[2:3] * dconv + cw_ref[1:2] * d1 + cw_ref[0:1] * d2
        v = cg * ug
        dcw = _rows8([jnp.sum(d2 * v, axis=0, keepdims=True), jnp.sum(d1 * v, axis=0, keepdims=True),
                      jnp.sum(dconv * v, axis=0, keepdims=True)], D)
        dbcu = jnp.concatenate([dbg, dv * ug, dv * cg], axis=1).astype(BF16)
        dbcu_ref[...] = dbcu
        dh = _dot_nt(dbcu, wi_ref[...])
        _, vjp = jax.vjp(_modnorm, x_ref[...], vec_ref[0:1], vec_ref[1:2], vec_ref[2:3])
        dx, dg, dsh, dsc = vjp(dh)
        dx_ref[...] = dxo_t + dx
        _acc_rows(part_ref, _rows8([dg, dsh, dsc, dgate], D), first)
        _acc_rows(dcw_ref, dcw, first)
        dcbuf[tm:tm + 8, :] = dcbuf[0:8, :]

    def rev(w):
        return pl.BlockSpec((tm, w), lambda i: (nt - 1 - i, 0))

    return pl.pallas_call(
        body, grid=(nt,),
        in_specs=[rev(D), rev(D), rev(D), rev(3 * D), rev(D), _res((8, D)), _res((8, D)), _res((D, 3 * D)), _res((D, D))],
        out_specs=[rev(D), rev(D), rev(3 * D), pl.BlockSpec((8, D), lambda i: (0, 0)), pl.BlockSpec((8, D), lambda i: (0, 0))],
        out_shape=[_sds((T, D), F32), _sds((T, D), BF16), _sds((T, 3 * D), BF16), _sds((8, D), F32), _sds((8, D), F32)],
        scratch_shapes=[pltpu.VMEM((tm + 8, D), F32)],
        compiler_params=_cp("arbitrary"), name="conv_bwd")(dxo, x, y, bcu, cv, vec, cw, w_in, w_out)


def rope_tables(pos, lane_rows):
    T = pos.shape[0]
    tm = min(TOKEN_TILE, T)

    def body(p_ref, lr_ref, c_ref, sp_ref, sm_ref):
        ang = p_ref[...].astype(F32) * lr_ref[0:1]
        cs, sn = jnp.cos(ang), jnp.sin(ang)
        c_ref[...] = jnp.where(lr_ref[1:2] > 0.5, cs, 1.0)
        sp_ref[...] = jnp.where(lr_ref[2:3] > 0.5, sn, 0.0)
        sm_ref[...] = jnp.where(lr_ref[3:4] > 0.5, -sn, 0.0)

    return pl.pallas_call(
        body, grid=(T // tm,),
        in_specs=[_tok(tm, 1), _res((8, LANES))],
        out_specs=[_tok(tm, LANES)] * 3,
        out_shape=[_sds((T, LANES), F32)] * 3,
        compiler_params=_cp("arbitrary"), name="rope_tables")(pos, lane_rows)


def _rope(t, c, sp, sm):
    w = t.shape[1]
    reps = w // LANES
    cf, spf, smf = jnp.tile(c, (1, reps)), jnp.tile(sp, (1, reps)), jnp.tile(sm, (1, reps))
    half = ROPE_DIM // 2
    return t * cf + pltpu.roll(t, half, axis=1) * spf + pltpu.roll(t, w - half, axis=1) * smf


def _rope_t(d, c, sp, sm):
    w = d.shape[1]
    reps = w // LANES
    cf, spf, smf = jnp.tile(c, (1, reps)), jnp.tile(sp, (1, reps)), jnp.tile(sm, (1, reps))
    half = ROPE_DIM // 2
    return d * cf + pltpu.roll(d * spf, w - half, axis=1) + pltpu.roll(d * smf, half, axis=1)


def proj_rope_fwd(x, vec, w, tabs, n_rope, transposed, name):
    T, D = x.shape
    N = w.shape[0] if transposed else w.shape[1]
    tm = min(TOKEN_TILE, T)

    def body(x_ref, vec_ref, w_ref, c_ref, sp_ref, sm_ref, h_ref, pr_ref, *rest_ref):
        hb = _modnorm(x_ref[...], vec_ref[0:1], vec_ref[1:2], vec_ref[2:3]).astype(BF16)
        h_ref[...] = hb
        p = _dot_nt(hb, w_ref[...]) if transposed else _dot(hb, w_ref[...])
        pr_ref[...] = _rope(p[:, 0:n_rope], c_ref[...], sp_ref[...], sm_ref[...]).astype(BF16)
        if rest_ref:
            rest_ref[0][...] = p[:, n_rope:N].astype(BF16)

    widths = [n_rope] + ([N - n_rope] if n_rope < N else [])
    return pl.pallas_call(
        body, grid=(T // tm,),
        in_specs=[_tok(tm, D), _res((8, D)), _res(w.shape)] + [_tok(tm, LANES)] * 3,
        out_specs=[_tok(tm, D)] + [_tok(tm, wd) for wd in widths],
        out_shape=[_sds((T, D), BF16)] + [_sds((T, wd), BF16) for wd in widths],
        compiler_params=_cp("arbitrary"), name=name)(x, vec, w, *tabs)


def proj_rope_bwd(dparts, x, dxo, vec, w, tabs, n_rope, transposed, name):
    T, D = x.shape
    N = w.shape[0] if transposed else w.shape[1]
    tm = min(TOKEN_TILE, T)
    npart = len(dparts)

    def body(*refs):
        d_refs = refs[:npart]
        x_ref, dxo_ref, vec_ref, w_ref, c_ref, sp_ref, sm_ref, dx_ref, dp_ref, part_ref = refs[npart:]
        d = jnp.concatenate([r[...].astype(F32) for r in d_refs], axis=1)
        dr = _rope_t(d[:, 0:n_rope], c_ref[...], sp_ref[...], sm_ref[...])
        if n_rope < N:
            dr = jnp.concatenate([dr, d[:, n_rope:N]], axis=1)
        dpb = dr.astype(BF16)
        dp_ref[...] = dpb
        dh = _dot(dpb, w_ref[...]) if transposed else _dot_nt(dpb, w_ref[...])
        _, vjp = jax.vjp(_modnorm, x_ref[...], vec_ref[0:1], vec_ref[1:2], vec_ref[2:3])
        dx, dg, dsh, dsc = vjp(dh)
        dx_ref[...] = dxo_ref[...] + dx
        _acc_rows(part_ref, _rows8([dg, dsh, dsc], D), pl.program_id(0) == 0)

    return pl.pallas_call(
        body, grid=(T // tm,),
        in_specs=[_tok(tm, p.shape[1]) for p in dparts] + [_tok(tm, D), _tok(tm, D), _res((8, D)), _res(w.shape)]
        + [_tok(tm, LANES)] * 3,
        out_specs=[_tok(tm, D), _tok(tm, N), pl.BlockSpec((8, D), lambda i: (0, 0))],
        out_shape=[_sds((T, D), F32), _sds((T, N), BF16), _sds((8, D), F32)],
        compiler_params=_cp("arbitrary"), name=name)(*dparts, x, dxo, vec, w, *tabs)


def _valid_mask(n, i):
    qi = lax.broadcasted_iota(jnp.int32, (n, 2 * n), 0)
    kj = lax.broadcasted_iota(jnp.int32, (n, 2 * n), 1)
    dist = n + qi - kj
    return (dist >= 0) & (dist <= n) & ((kj >= n) | (i > 0))


def attn_core_fwd(q, k, v, g, n, d):
    T, QW = q.shape
    GW = GROUP_WIDTH
    ng = QW // GW
    M = T // d
    scale = HEAD_DIM ** -0.5

    def body(q_ref, kp_ref, kc_ref, vp_ref, vc_ref, o_ref, l_ref):
        valid = _valid_mask(n, pl.program_id(1))
        qv = q_ref[...]
        kk = jnp.concatenate([kp_ref[...], kc_ref[...]], axis=0)
        vv = jnp.concatenate([vp_ref[...], vc_ref[...]], axis=0)
        for h in range(HEADS_PER_GROUP):
            hs = slice(HEAD_DIM * h, HEAD_DIM * (h + 1))
            s = jnp.where(valid, _dot_nt(qv[:, hs], kk[:, hs]) * scale, -1e30)
            m = jnp.max(s, axis=1, keepdims=True)
            p = jnp.exp(s - m)
            den = jnp.sum(p, axis=1, keepdims=True)
            o_ref[:, hs] = _dot((p / den).astype(BF16), vv[:, hs])
            l_ref[:, hs] = jnp.broadcast_to(m + jnp.log(den), (n, HEAD_DIM))

    cur = pl.BlockSpec((n, GW), lambda r, i: (i, r * ng + g))
    prv = pl.BlockSpec((n, GW), lambda r, i: (jnp.maximum(i - 1, 0), r * ng + g))
    out = pl.BlockSpec((n, GW), lambda r, i: (i, r))
    qv, kv, vv = q.reshape(M, d * QW), k.reshape(M, d * QW), v.reshape(M, d * QW)
    o, l = pl.pallas_call(
        body, grid=(d, M // n),
        in_specs=[cur, prv, cur, prv, cur], out_specs=[out, out],
        out_shape=[_sds((M, d * GW), F32), _sds((M, d * GW), F32)],
        compiler_params=_cp("arbitrary", "arbitrary"), name=f"attn_fwd_g{g}")(qv, kv, kv, vv, vv)
    return o.reshape(T, GW), l.reshape(T, GW)


def attn_core_bwd(q, k, v, do, rr, lse, g, n, d):
    T, QW = q.shape
    GW = GROUP_WIDTH
    ng = QW // GW
    M = T // d
    scale = HEAD_DIM ** -0.5

    def body(q_ref, kp_ref, kc_ref, vp_ref, vc_ref, do_ref, r_ref, l_ref, dq_ref, dkc_ref, dkp_ref, dvc_ref, dvp_ref):
        valid = _valid_mask(n, pl.program_id(1))
        qv, dov = q_ref[...], do_ref[...]
        kk = jnp.concatenate([kp_ref[...], kc_ref[...]], axis=0)
        vv = jnp.concatenate([vp_ref[...], vc_ref[...]], axis=0)
        for h in range(HEADS_PER_GROUP):
            hs = slice(HEAD_DIM * h, HEAD_DIM * (h + 1))
            s = jnp.where(valid, _dot_nt(qv[:, hs], kk[:, hs]) * scale, -1e30)
            p = jnp.exp(s - l_ref[:, HEAD_DIM * h:HEAD_DIM * h + 1])
            dp = _dot_nt(dov[:, hs], vv[:, hs])
            delta = jnp.sum(r_ref[:, hs], axis=1, keepdims=True)
            ds = (p * (dp - delta) * scale).astype(BF16)
            dq_ref[:, hs] = _dot(ds, kk[:, hs]).astype(BF16)
            dk = _dot_tn(ds, qv[:, hs]).astype(BF16)
            dv = _dot_tn(p.astype(BF16), dov[:, hs]).astype(BF16)
            dkp_ref[:, hs], dkc_ref[:, hs] = dk[0:n], dk[n:2 * n]
            dvp_ref[:, hs], dvc_ref[:, hs] = dv[0:n], dv[n:2 * n]

    cur = pl.BlockSpec((n, GW), lambda r, i: (i, r * ng + g))
    prv = pl.BlockSpec((n, GW), lambda r, i: (jnp.maximum(i - 1, 0), r * ng + g))
    blk = pl.BlockSpec((n, GW), lambda r, i: (i, r))
    qv, kv, vv = q.reshape(M, d * QW), k.reshape(M, d * QW), v.reshape(M, d * QW)
    outs = pl.pallas_call(
        body, grid=(d, M // n),
        in_specs=[cur, prv, cur, prv, cur, blk, blk, blk], out_specs=[blk] * 5,
        out_shape=[_sds((M, d * GW), BF16)] * 5,
        compiler_params=_cp("arbitrary", "arbitrary"), name=f"attn_bwd_g{g}")(
            qv, kv, kv, vv, vv, do.reshape(M, d * GW), rr.reshape(M, d * GW), lse.reshape(M, d * GW))
    return [o.reshape(T, GW) for o in outs]


def dkv_combine(cur_prev, n, d, name):
    T, GW = cur_prev[0][0].shape
    M = T // d
    nb = M // n
    flat = [a.reshape(M, d * GW) for pair in cur_prev for a in pair]

    def body(*refs):
        o_ref = refs[-1]
        last = pl.program_id(1) == nb - 1
        acc = jnp.zeros((n, GW), F32)
        for t in range(0, len(refs) - 1, 2):
            acc = acc + refs[t][...].astype(F32) + jnp.where(last, 0.0, refs[t + 1][...].astype(F32))
        o_ref[...] = acc.astype(BF16)

    cur = pl.BlockSpec((n, GW), lambda r, i: (i, r))
    nxt = pl.BlockSpec((n, GW), lambda r, i: (jnp.minimum(i + 1, nb - 1), r))
    out = pl.pallas_call(
        body, grid=(d, nb), in_specs=[cur, nxt] * len(cur_prev), out_specs=cur,
        out_shape=_sds((M, d * GW), BF16),
        compiler_params=_cp("arbitrary", "arbitrary"), name=name)(*flat)
    return out.reshape(T, GW)


def _group_weights(ls):
    mx = functools.reduce(jnp.maximum, ls)
    es = [jnp.exp(l - mx) for l in ls]
    tot = functools.reduce(lambda a, b: a + b, es)
    return [e / tot for e in es]


def attn_mix_out(os_, ls, x, vec, w_o):
    T, D = x.shape
    GW = GROUP_WIDTH
    tm = min(TOKEN_TILE, T)
    ng = len(os_)

    def body(*refs):
        o_refs, l_refs = refs[:ng], refs[ng:2 * ng]
        x_ref, vec_ref, w_ref, xn_ref, mix_ref, y_ref = refs[2 * ng:]
        ws = _group_weights([r[...] for r in l_refs])
        mixed = functools.reduce(lambda a, b: a + b, [w * r[...] for w, r in zip(ws, o_refs)])
        mb = mixed.astype(BF16)
        mix_ref[...] = mb
        y = _dot(mb, w_ref[...])
        y_ref[...] = y.astype(BF16)
        xn_ref[...] = x_ref[...] + (1.0 + vec_ref[3:4]) * y

    return pl.pallas_call(
        body, grid=(T // tm,),
        in_specs=[_tok(tm, GW)] * (2 * ng) + [_tok(tm, D), _res((8, D)), _res((GW, D))],
        out_specs=[_tok(tm, D), _tok(tm, GW), _tok(tm, D)],
        out_shape=[_sds((T, D), F32), _sds((T, GW), BF16), _sds((T, D), BF16)],
        compiler_params=_cp("arbitrary"), name="attn_mix_out")(*os_, *ls, x, vec, w_o)


def attn_mix_bwd(dxo, y, vec, w_o, os_, ls):
    T, D = dxo.shape
    GW = GROUP_WIDTH
    tm = min(TOKEN_TILE, T)
    ng = len(os_)

    def body(*refs):
        dxo_ref, y_ref, vec_ref, w_ref = refs[:4]
        o_refs, l_refs = refs[4:4 + ng], refs[4 + ng:4 + 2 * ng]
        dy_ref = refs[4 + 2 * ng]
        do_refs = refs[5 + 2 * ng:5 + 3 * ng]
        r_refs = refs[5 + 3 * ng:5 + 4 * ng]
        part_ref = refs[5 + 4 * ng]
        dxo_t = dxo_ref[...]
        dyb = (dxo_t * (1.0 + vec_ref[3:4])).astype(BF16)
        dy_ref[...] = dyb
        dgate = jnp.sum(dxo_t * y_ref[...].astype(F32), axis=0, keepdims=True)
        _acc_rows(part_ref, _rows8([dgate], D), pl.program_id(0) == 0)
        dmix = _dot_nt(dyb, w_ref[...])
        ws = _group_weights([r[...] for r in l_refs])
        ov = [r[...] for r in o_refs]
        mixed = functools.reduce(lambda a, b: a + b, [w * o for w, o in zip(ws, ov)])
        for gi in range(ng):
            do = ws[gi] * dmix
            do_refs[gi][...] = do.astype(BF16)
            r_refs[gi][...] = do * mixed

    return pl.pallas_call(
        body, grid=(T // tm,),
        in_specs=[_tok(tm, D), _tok(tm, D), _res((8, D)), _res((GW, D))] + [_tok(tm, GW)] * (2 * ng),
        out_specs=[_tok(tm, D)] + [_tok(tm, GW)] * (2 * ng) + [pl.BlockSpec((8, D), lambda i: (0, 0))],
        out_shape=[_sds((T, D), BF16)] + [_sds((T, GW), BF16)] * ng + [_sds((T, GW), F32)] * ng + [_sds((8, D), F32)],
        compiler_params=_cp("arbitrary"), name="attn_mix_bwd")(dxo, y, vec, w_o, *os_, *ls)


def final_loss(x, gvec, target):
    T, D = x.shape
    tm = min(TOKEN_TILE, T)

    def norm(xv, g):
        return xv * lax.rsqrt(jnp.mean(xv * xv, axis=-1, keepdims=True) + NORM_EPS) * g

    def body(x_ref, g_ref, t_ref, dx_ref, part_ref, loss_ref):
        first = pl.program_id(0) == 0
        yv, vjp = jax.vjp(norm, x_ref[...], g_ref[0:1])
        err = yv - t_ref[...]
        dx, dg = vjp(err * (1.0 / D))
        dx_ref[...] = dx
        _acc_rows(part_ref, _rows8([dg], D), first)
        tile_loss = 0.5 * jnp.sum(jnp.sum(err * err, axis=1, keepdims=True) * (1.0 / D), axis=0, keepdims=True)
        _acc_rows(loss_ref, jnp.broadcast_to(tile_loss, (8, LANES)), first)

    return pl.pallas_call(
        body, grid=(T // tm,),
        in_specs=[_tok(tm, D), _res((8, D)), _tok(tm, D)],
        out_specs=[_tok(tm, D), pl.BlockSpec((8, D), lambda i: (0, 0)), pl.BlockSpec((8, LANES), lambda i: (0, 0))],
        out_shape=[_sds((T, D), F32), _sds((8, D), F32), _sds((8, LANES), F32)],
        compiler_params=_cp("arbitrary"), name="final_loss")(x, gvec, target)


def mods_project(c_all, w, b):
    B, D = c_all.shape
    L, _, N = w.shape

    def body(c_ref, w_ref, b_ref, o_ref):
        cv = c_ref[...]
        cond = cv * _sigmoid(cv)
        o_ref[0] = jnp.dot(cond, w_ref[0], preferred_element_type=F32, precision=lax.Precision.HIGHEST) + b_ref[0]

    return pl.pallas_call(
        body, grid=(L,),
        in_specs=[pl.BlockSpec((B, D), lambda l: (0, 0)), pl.BlockSpec((1, D, N), lambda l: (l, 0, 0)),
                  pl.BlockSpec((1, 1, N), lambda l: (l, 0, 0))],
        out_specs=pl.BlockSpec((1, B, N), lambda l: (l, 0, 0)),
        out_shape=_sds((L, B, N), F32),
        compiler_params=_cp("arbitrary"), name="mods_project")(c_all, w, b)


def mods_weight_grad(c_all, dm):
    B, D = c_all.shape
    L, _, N = dm.shape

    def body(c_ref, d_ref, o_ref):
        cv = c_ref[...]
        cond = cv * _sigmoid(cv)
        o_ref[0] = lax.dot_general(cond, d_ref[0], (((0,), (0,)), ((), ())), preferred_element_type=F32,
                                   precision=lax.Precision.HIGHEST)

    return pl.pallas_call(
        body, grid=(L,),
        in_specs=[pl.BlockSpec((B, D), lambda l: (0, 0)), pl.BlockSpec((1, B, N), lambda l: (l, 0, 0))],
        out_specs=pl.BlockSpec((1, D, N), lambda l: (l, 0, 0)),
        out_shape=_sds((L, D, N), F32),
        compiler_params=_cp("arbitrary"), name="mods_weight_grad")(c_all, dm)


def _adam_math(g, w, m, v):
    m2 = ADAM_B1 * m + (1.0 - ADAM_B1) * g
    v2 = ADAM_B2 * v + (1.0 - ADAM_B2) * (g * g)
    m_hat = m2 / (1.0 - ADAM_B1 ** ADAM_STEP)
    v_hat = v2 / (1.0 - ADAM_B2 ** ADAM_STEP)
    delta = -ADAM_LR * (m_hat / (jnp.sqrt(v_hat) + ADAM_EPS) + ADAM_WD * w)
    return delta, m2, v2


def adam_update(g, w, m, v, parts, name):
    R, C = w.shape
    tr = _pick(R, 256, 8)

    def body(g_ref, w_ref, m_ref, v_ref, go_ref, d_ref, mo_ref, vo_ref):
        if parts:
            gv = g_ref[0].astype(F32)
            for s in range(1, N_DEV):
                gv = gv + g_ref[s].astype(F32)
        else:
            gv = g_ref[...]
        go_ref[...] = gv
        d_ref[...], mo_ref[...], vo_ref[...] = _adam_math(gv, w_ref[...], m_ref[...], v_ref[...])

    gspec = pl.BlockSpec((N_DEV, tr, C), lambda i: (0, i, 0)) if parts else _tok(tr, C)
    return pl.pallas_call(
        body, grid=(R // tr,),
        in_specs=[gspec, _tok(tr, C), _tok(tr, C), _tok(tr, C)],
        out_specs=[_tok(tr, C)] * 4, out_shape=[_sds((R, C), F32)] * 4,
        compiler_params=_cp("arbitrary"), name=name)(g, w, m, v)


def adam_layer(parts, w, m, v, prev, layer, name):
    L, R, C = w.shape
    tr = _pick(R, 256, 8)

    def body(p_ref, w_ref, m_ref, v_ref, *rest):
        go_ref, d_ref, mo_ref, vo_ref = rest[-4:]
        gv = p_ref[0].astype(F32)
        for s in range(1, N_DEV):
            gv = gv + p_ref[s].astype(F32)
        go_ref[...] = gv
        d_ref[...], mo_ref[...], vo_ref[...] = _adam_math(gv, w_ref[...], m_ref[...], v_ref[...])

    lay = pl.BlockSpec((None, tr, C), lambda i: (layer, i, 0))
    prev = list(prev) if prev is not None else []
    return pl.pallas_call(
        body, grid=(R // tr,),
        in_specs=[pl.BlockSpec((N_DEV, tr, C), lambda i: (0, i, 0)), lay, lay, lay] + [pl.BlockSpec(memory_space=pl.ANY)] * len(prev),
        out_specs=[lay] * 4, out_shape=[_sds((L, R, C), F32)] * 4,
        input_output_aliases={4 + k: k for k in range(len(prev))},
        compiler_params=_cp("arbitrary"), name=name)(parts, w, m, v, *prev)


def _my_id():
    return 4 * lax.axis_index("x") + 2 * lax.axis_index("y") + lax.axis_index("c")


def _peer(s):
    x, y, c = lax.axis_index("x"), lax.axis_index("y"), lax.axis_index("c")
    px = (1 - x) if s & 4 else x
    py = (1 - y) if s & 2 else y
    pc = (1 - c) if s & 1 else c
    return (px, py, pc), 4 * px + 2 * py + pc


def all_gather(xs, space, name):
    na = len(xs)

    def body(*refs):
        x_refs, o_refs = refs[:na], refs[na:2 * na]
        send_sems, recv_sems, local_sems = refs[2 * na:]
        me = _my_id()
        locals_, sends = [], []
        for a in range(na):
            cp = pltpu.make_async_copy(x_refs[a], o_refs[a].at[me], local_sems.at[a])
            cp.start()
            locals_.append(cp)
        for s in range(1, N_DEV):
            peer, _ = _peer(s)
            for a in range(na):
                cp = pltpu.make_async_remote_copy(
                    src_ref=x_refs[a], dst_ref=o_refs[a].at[me], send_sem=send_sems.at[a, s - 1],
                    recv_sem=recv_sems.at[a, s - 1], device_id=peer, device_id_type=MESH)
                cp.start()
                sends.append(cp)
        for s in range(1, N_DEV):
            peer, pid = _peer(s)
            for a in range(na):
                pltpu.make_async_remote_copy(
                    src_ref=x_refs[a], dst_ref=o_refs[a].at[pid], send_sem=send_sems.at[a, s - 1],
                    recv_sem=recv_sems.at[a, s - 1], device_id=peer, device_id_type=MESH).wait_recv()
        for cp in sends:
            cp.wait_send()
        for cp in locals_:
            cp.wait()

    spec = pl.BlockSpec(memory_space=space)
    return pl.pallas_call(
        body, in_specs=[spec] * na, out_specs=[spec] * na,
        out_shape=[_sds((N_DEV,) + x.shape, x.dtype) for x in xs],
        scratch_shapes=[pltpu.SemaphoreType.DMA((na, N_DEV - 1)), pltpu.SemaphoreType.DMA((na, N_DEV - 1)),
                        pltpu.SemaphoreType.DMA((na,))],
        compiler_params=pltpu.CompilerParams(vmem_limit_bytes=VMEM_LIMIT), name=name)(*xs)


def exchange_slots(xs, name):
    na = len(xs)

    def body(*refs):
        x_refs, o_refs = refs[:na], refs[na:2 * na]
        send_sems, recv_sems, local_sems = refs[2 * na:]
        me = _my_id()
        locals_, sends = [], []
        for a in range(na):
            cp = pltpu.make_async_copy(x_refs[a].at[me], o_refs[a].at[me], local_sems.at[a])
            cp.start()
            locals_.append(cp)
        for s in range(1, N_DEV):
            peer, pid = _peer(s)
            for a in range(na):
                cp = pltpu.make_async_remote_copy(
                    src_ref=x_refs[a].at[pid], dst_ref=o_refs[a].at[me], send_sem=send_sems.at[a, s - 1],
                    recv_sem=recv_sems.at[a, s - 1], device_id=peer, device_id_type=MESH)
                cp.start()
                sends.append(cp)
        for s in range(1, N_DEV):
            peer, pid = _peer(s)
            for a in range(na):
                pltpu.make_async_remote_copy(
                    src_ref=x_refs[a].at[pid], dst_ref=o_refs[a].at[pid], send_sem=send_sems.at[a, s - 1],
                    recv_sem=recv_sems.at[a, s - 1], device_id=peer, device_id_type=MESH).wait_recv()
        for cp in sends:
            cp.wait_send()
        for cp in locals_:
            cp.wait()

    spec = pl.BlockSpec(memory_space=pl.ANY)
    return pl.pallas_call(
        body, in_specs=[spec] * na, out_specs=[spec] * na,
        out_shape=[_sds(x.shape, x.dtype) for x in xs],
        scratch_shapes=[pltpu.SemaphoreType.DMA((na, N_DEV - 1)), pltpu.SemaphoreType.DMA((na, N_DEV - 1)),
                        pltpu.SemaphoreType.DMA((na,))],
        compiler_params=pltpu.CompilerParams(vmem_limit_bytes=VMEM_LIMIT), name=name)(*xs)


_HBM = pl.BlockSpec(memory_space=pltpu.HBM)
_SEM = pl.BlockSpec(memory_space=pltpu.SEMAPHORE)
_EFFECT = pltpu.SideEffectType.DATAFLOW_SIDE_EFFECTING


def _split_copy(x_ref, land_ref, s, send_sem, recv_sem, scatter):
    peer, pid = _peer(s)
    src = x_ref.at[pid] if scatter else x_ref
    return pltpu.make_async_remote_copy(src_ref=src, dst_ref=land_ref.at[_my_id()], send_sem=send_sem, recv_sem=recv_sem,
                                        device_id=peer, device_id_type=MESH)


def comm_start(xs, scatter, after, name):
    na = len(xs)
    extra = [] if after is None else [after]
    me = _my_id()
    lands = []
    for x in xs:
        shape = x.shape if scatter else (N_DEV,) + x.shape
        own = lax.dynamic_slice_in_dim(x, me, 1, 0) if scatter else x[None]
        lands.append(lax.dynamic_update_slice(lax.empty(shape, x.dtype), own, (me,) + (0,) * (len(shape) - 1)))

    def body(*refs):
        x_refs, land_refs = refs[:na], refs[na:2 * na]
        send_sem, recv_sem = refs[2 * na + len(extra)], refs[2 * na + len(extra) + 1]
        token = refs[-1]
        for s in range(1, N_DEV):
            for a in range(na):
                _split_copy(x_refs[a], land_refs[a], s, send_sem, recv_sem, scatter).start()
        token[...] = jnp.zeros_like(token)

    outs = pl.pallas_call(
        body, name=name,
        out_shape=(pltpu.SemaphoreType.DMA(()), pltpu.SemaphoreType.DMA(()))
        + tuple(pltpu.HBM(x.shape, x.dtype) for x in xs) + tuple(pltpu.HBM(l.shape, l.dtype) for l in lands)
        + (_sds((8, LANES), F32),),
        in_specs=(_HBM,) * (2 * na) + (pl.BlockSpec(memory_space=pl.ANY),) * len(extra),
        out_specs=(_SEM, _SEM) + (_HBM,) * (2 * na) + (pl.BlockSpec(memory_space=pltpu.VMEM),),
        input_output_aliases={a: 2 + a for a in range(2 * na)},
        compiler_params=pltpu.CompilerParams(has_side_effects=_EFFECT),
    )(*[pltpu.with_memory_space_constraint(x, pltpu.HBM) for x in xs],
      *[pltpu.with_memory_space_constraint(l, pltpu.HBM) for l in lands], *extra)
    return dict(sems=outs[0:2], xs=outs[2:2 + na], lands=outs[2 + na:2 + 2 * na], token=outs[-1], scatter=scatter)


def comm_wait(started, after, name):
    xs, lands = started["xs"], started["lands"]
    scatter = started["scatter"]
    na = len(xs)

    def body(*refs):
        x_refs, land_refs = refs[:na], refs[na:2 * na]
        send_sem, recv_sem = refs[2 * na], refs[2 * na + 1]
        for s in range(1, N_DEV):
            for a in range(na):
                cp = _split_copy(x_refs[a], land_refs[a], s, send_sem, recv_sem, scatter)
                cp.wait_send()
                cp.wait_recv()

    outs = pl.pallas_call(
        body, name=name,
        out_shape=tuple(pltpu.HBM(x.shape, x.dtype) for x in xs) + tuple(pltpu.HBM(l.shape, l.dtype) for l in lands),
        in_specs=(_HBM,) * (2 * na) + (_SEM, _SEM, pl.BlockSpec(memory_space=pl.ANY)),
        out_specs=(_HBM,) * (2 * na),
        input_output_aliases={a: a for a in range(2 * na)},
        compiler_params=pltpu.CompilerParams(has_side_effects=_EFFECT),
    )(*xs, *lands, *started["sems"], after)
    return list(outs[na:])


def _cols_to_natural(g):
    return jnp.concatenate([g[k] for k in range(N_DEV)], axis=1)


def _cols_to_slots(w):
    ns = w.shape[1] // N_DEV
    return jnp.stack([w[:, k * ns:(k + 1) * ns] for k in range(N_DEV)])


def _vec8(rows, d):
    rows = [r.reshape(1, d).astype(F32) for r in rows]
    return jnp.concatenate(rows + [jnp.zeros((8 - len(rows), d), F32)], axis=0)


def _ffn_forward(x, vec, w_in_t, w_out):
    h, a, b, u = ffn_up(x, vec, w_in_t)
    xn, y = proj_out(u, x, vec, w_out, FFN_RES_WEIGHT, "ffn_down")
    return xn, (x, h, a, b, u, y)


def _ffn_backward(dxo, saved, vec, w_in_t, w_out):
    x, h, a, b, u, y = saved
    dy, dab, part_gate = ffn_down_bwd(dxo, y, vec, w_out, a, b)
    dx, part_norm = ffn_up_bwd(dab, w_in_t, x, dxo, vec)
    g_out = grad_slots(u, dy, "ffn_dw_out")
    g_in_t = grad_slots(dab, h, "ffn_dw_in")
    rows = jnp.concatenate([part_norm[0:3], part_gate[0:1]], axis=0)
    return dx, g_in_t, g_out, rows


_TRANSPOSED = ("ffn1_w_in", "ffn2_w_in", "attn_w_q")
_COL_NATURAL = ("conv_w_in", "w_kv", "attn_w_o")
_ROW_SHARDED = ("ffn1_w_out", "ffn2_w_out", "conv_w_out")
_BIG = _TRANSPOSED + _COL_NATURAL + _ROW_SHARDED


def weight_chunks():
    chunks = []
    for layer in range(DEPTH):
        first = [("ffn1_w_in", layer), ("ffn1_w_out", layer)]
        if layer == N_A_LAYERS:
            first = [("w_kv", layer)] + first
        mixer = [("conv_w_in", layer), ("conv_w_out", layer)] if layer < N_A_LAYERS else [("attn_w_q", layer), ("attn_w_o", layer)]
        rest = mixer + [("ffn2_w_in", layer), ("ffn2_w_out", layer)]
        chunks += [first, rest] if layer == 0 else [first + rest]
    return chunks


def stacked_index(name, layer):
    if name == "w_kv":
        return None
    return layer - N_A_LAYERS if name.startswith("attn") else layer


class ChunkComm:
    def __init__(self, shards):
        self.shards = shards
        self.chunks = weight_chunks()

    def _shard(self, name, layer):
        idx = stacked_index(name, layer)
        return self.shards[name][0 if idx is None else idx]

    def start_gathers(self):
        started, after = [], None
        for ci, chunk in enumerate(self.chunks):
            xs = [self._shard(n, l).astype(BF16) for n, l in chunk]
            started.append(comm_start(xs, False, after, f"gather_start_{ci}"))
            after = started[-1]["token"]
        return started

    def finish_gather(self, ci, started, after):
        lands = comm_wait(started, after, f"gather_wait_{ci}")
        W = {}
        for key, g in zip(self.chunks[ci], lands):
            W[key] = _cols_to_natural(g) if key[0] in _COL_NATURAL else g.reshape(-1, g.shape[2])
        return W

    def start_exchange(self, ci, slots, after):
        return comm_start([slots[key] for key in self.chunks[ci]], True, after, f"exchange_start_{ci}")

    def finish_exchange(self, ci, started, after):
        lands = comm_wait(started, after, f"exchange_wait_{ci}")
        return dict(zip(self.chunks[ci], lands))


def device_step(x, positions, target, mods, kvmods, small, comm, gathers):
    T, D = x.shape
    groups = DILATED_GROUPS
    lane = jnp.arange(LANES) % HEAD_DIM
    inv = ROPE_THETA ** (-jnp.arange(0, ROPE_DIM, 2, dtype=F32) / ROPE_DIM)
    lane_rows = _vec8([jnp.where(lane < ROPE_DIM, inv[lane % (ROPE_DIM // 2)], 0.0), lane < ROPE_DIM,
                       (lane >= ROPE_DIM // 2) & (lane < ROPE_DIM), lane < ROPE_DIM // 2], LANES)
    tabs = rope_tables(positions.reshape(T, 1), lane_rows)

    def after_token(v, token):
        return v if token is None else v + token[0, 0]

    def vec_of(layer, sub, token=None):
        return after_token(_vec8([small["norm_g"][layer, sub], mods[layer, 3 * sub], mods[layer, 3 * sub + 1],
                                  mods[layer, 3 * sub + 2]], D), token)

    saved = []
    kv_saved = None
    k_sh = v_sh = None
    qw = GROUP_WIDTH * len(groups)
    chunk_of = {key: ci for ci, chunk in enumerate(comm.chunks) for key in chunk}
    W = {}

    def need(key, after):
        if key not in W:
            ci = chunk_of[key]
            W.update(comm.finish_gather(ci, gathers[ci], after))
        return W[key]

    for layer in range(DEPTH):
        if layer == N_A_LAYERS:
            kv_vec = _vec8([small["kv_norm_g"], kvmods[0], kvmods[1]], D)
            h_kv, k_sh, v_sh = proj_rope_fwd(x, kv_vec, need(("w_kv", layer), x), tabs, qw, False, "kv_fwd")
            kv_saved = (x, h_kv, kv_vec)
        rec = {}
        v1 = vec_of(layer, 0)
        x, rec["ffn1"] = _ffn_forward(x, v1, need(("ffn1_w_in", layer), x), need(("ffn1_w_out", layer), x))
        v2 = vec_of(layer, 1)
        if layer < N_A_LAYERS:
            cw = _vec8(list(small["conv_w"][layer]), D)
            x_in = x
            x, h, bcu, cv, z, y = conv_fwd(x, v2, cw, need(("conv_w_in", layer), x), need(("conv_w_out", layer), x))
            rec["mix"] = (x_in, h, bcu, cv, z, y, cw)
        else:
            x_in = x
            h, q = proj_rope_fwd(x, v2, need(("attn_w_q", layer), x), tabs, qw, True, "q_fwd")
            os_, ls = [], []
            for g, (win, dil) in enumerate(groups):
                o, l = attn_core_fwd(q, k_sh, v_sh, g, win // dil, dil)
                os_.append(o)
                ls.append(l)
            x, mixed, y = attn_mix_out(os_, ls, x, v2, need(("attn_w_o", layer), x))
            rec["mix"] = (x_in, h, q, os_, ls, mixed, y)
        v3 = vec_of(layer, 2)
        x, rec["ffn2"] = _ffn_forward(x, v3, need(("ffn2_w_in", layer), x), need(("ffn2_w_out", layer), x))
        rec["vecs"] = (v1, v2, v3)
        saved.append(rec)

    dx, part_final, loss_tile = final_loss(x, _vec8([small["final_norm_g"]], D), target)
    loss = loss_tile[0, 0]

    conv_rows = [None] * N_A_LAYERS
    kv_rows = None
    mod_rows = [[None] * 3 for _ in range(DEPTH)]
    dkv_pairs = [{"k": [], "v": []} for _ in groups]
    slots = {}
    exchanges = []
    token = None

    def send_ready_chunks():
        nonlocal token
        for ci in reversed(range(len(comm.chunks))):
            if ci not in [e[0] for e in exchanges] and all(key in slots for key in comm.chunks[ci]):
                started = comm.start_exchange(ci, slots, token)
                exchanges.append((ci, started))
                token = started["token"]

    for layer in reversed(range(DEPTH)):
        rec = saved[layer]
        v1, v2, v3 = rec["vecs"]
        dx, slots[("ffn2_w_in", layer)], slots[("ffn2_w_out", layer)], mod_rows[layer][2] = _ffn_backward(
            dx, rec["ffn2"], after_token(v3, token), W[("ffn2_w_in", layer)], W[("ffn2_w_out", layer)])
        if layer < N_A_LAYERS:
            x_in, h, bcu, cv, z, y, cw = rec["mix"]
            dx, dy, dbcu, part, dcw = conv_bwd(dx, x_in, y, bcu, cv, v2, cw, W[("conv_w_in", layer)], W[("conv_w_out", layer)])
            slots[("conv_w_out", layer)] = grad_slots(z, dy, "conv_dw_out")
            slots[("conv_w_in", layer)] = grad_slots(h, dbcu, "conv_dw_in", col_slots=True)
            conv_rows[layer] = dcw[0:3]
            mod_rows[layer][1] = part[0:4]
        else:
            x_in, h, q, os_, ls, mixed, y = rec["mix"]
            outs = attn_mix_bwd(dx, y, v2, W[("attn_w_o", layer)], os_, ls)
            ng = len(groups)
            dy, dos, rrs, part_gate = outs[0], outs[1:1 + ng], outs[1 + ng:1 + 2 * ng], outs[1 + 2 * ng]
            slots[("attn_w_o", layer)] = grad_slots(mixed, dy, "attn_dw_o", col_slots=True)
            dqs = []
            for g, (win, dil) in enumerate(groups):
                dq, dkc, dkp, dvc, dvp = attn_core_bwd(q, k_sh, v_sh, dos[g], rrs[g], ls[g], g, win // dil, dil)
                dqs.append(dq)
                dkv_pairs[g]["k"].append((dkc, dkp))
                dkv_pairs[g]["v"].append((dvc, dvp))
            dx, dqr, part_norm = proj_rope_bwd(dqs, x_in, dx, v2, W[("attn_w_q", layer)], tabs, qw, True, "q_bwd")
            slots[("attn_w_q", layer)] = grad_slots(dqr, h, "attn_dw_q")
            mod_rows[layer][1] = jnp.concatenate([part_norm[0:3], part_gate[0:1]], axis=0)
        send_ready_chunks()
        dx, slots[("ffn1_w_in", layer)], slots[("ffn1_w_out", layer)], mod_rows[layer][0] = _ffn_backward(
            dx, rec["ffn1"], after_token(v1, token), W[("ffn1_w_in", layer)], W[("ffn1_w_out", layer)])
        if layer == N_A_LAYERS:
            x_kv, h_kv, kv_vec = kv_saved
            dparts = [dkv_combine(dkv_pairs[g]["k"], win // dil, dil, f"dk_combine_g{g}") for g, (win, dil) in enumerate(groups)]
            dparts += [dkv_combine(dkv_pairs[g]["v"], win // dil, dil, f"dv_combine_g{g}") for g, (win, dil) in enumerate(groups)]
            dx, dkvp, part_kv = proj_rope_bwd(dparts, x_kv, dx, kv_vec, W[("w_kv", layer)], tabs, qw, False, "kv_bwd")
            slots[("w_kv", layer)] = grad_slots(h_kv, dkvp, "kv_dw", col_slots=True)
            kv_rows = part_kv[0:3]
        send_ready_chunks()

    grads = {"conv_w": jnp.stack(conv_rows), "kv_rows": kv_rows, "exchanges": exchanges}
    grads["final_norm_g"] = part_final[0]
    rows = jnp.stack([jnp.stack(r) for r in mod_rows])
    grads["norm_g"] = rows[:, :, 0]
    grads["mods"] = rows[:, :, 1:4].reshape(DEPTH, N_MOD, D)
    return loss, dx, grads


def _flat2(a):
    return a.reshape(-1, a.shape[-1])


def _pad_rows(a, mult):
    r = a.shape[0]
    pad = (-r) % mult
    return a if pad == 0 else jnp.concatenate([a, jnp.zeros((pad,) + a.shape[1:], a.dtype)], axis=0)


def kernel(x, c, positions, norm_g, ada_w, ada_b, ffn1_w_in, ffn1_w_out, ffn2_w_in, ffn2_w_out, conv_w_in, conv_w, conv_w_out, kv_norm_g, kv_ada_w, kv_ada_b, w_kv, attn_w_q, attn_w_o, final_norm_g, loss_target, m_norm_g, m_ada_w, m_ada_b, m_ffn1_w_in, m_ffn1_w_out, m_ffn2_w_in, m_ffn2_w_out, m_conv_w_in, m_conv_w, m_conv_w_out, m_kv_norm_g, m_kv_ada_w, m_kv_ada_b, m_w_kv, m_attn_w_q, m_attn_w_o, m_final_norm_g, v_norm_g, v_ada_w, v_ada_b, v_ffn1_w_in, v_ffn1_w_out, v_ffn2_w_in, v_ffn2_w_out, v_conv_w_in, v_conv_w, v_conv_w_out, v_kv_norm_g, v_kv_ada_w, v_kv_ada_b, v_w_kv, v_attn_w_q, v_attn_w_o, v_final_norm_g):
    names = ("norm_g", "ada_w", "ada_b", "ffn1_w_in", "ffn1_w_out", "ffn2_w_in", "ffn2_w_out", "conv_w_in", "conv_w",
             "conv_w_out", "kv_norm_g", "kv_ada_w", "kv_ada_b", "w_kv", "attn_w_q", "attn_w_o", "final_norm_g")
    wts = dict(zip(names, (norm_g, ada_w, ada_b, ffn1_w_in, ffn1_w_out, ffn2_w_in, ffn2_w_out, conv_w_in, conv_w, conv_w_out,
                           kv_norm_g, kv_ada_w, kv_ada_b, w_kv, attn_w_q, attn_w_o, final_norm_g)))
    mom = dict(zip(names, (m_norm_g, m_ada_w, m_ada_b, m_ffn1_w_in, m_ffn1_w_out, m_ffn2_w_in, m_ffn2_w_out, m_conv_w_in,
                           m_conv_w, m_conv_w_out, m_kv_norm_g, m_kv_ada_w, m_kv_ada_b, m_w_kv, m_attn_w_q, m_attn_w_o,
                           m_final_norm_g)))
    var = dict(zip(names, (v_norm_g, v_ada_w, v_ada_b, v_ffn1_w_in, v_ffn1_w_out, v_ffn2_w_in, v_ffn2_w_out, v_conv_w_in,
                           v_conv_w, v_conv_w_out, v_kv_norm_g, v_kv_ada_w, v_kv_ada_b, v_w_kv, v_attn_w_q, v_attn_w_o,
                           v_final_norm_g)))
    T, D = x.shape[1], x.shape[2]
    me = _my_id()
    nmod = ada_w.shape[2]
    nkv = kv_ada_w.shape[1]

    def stacked(w, n):
        w = w if w.ndim == 3 else w[None]
        return jnp.swapaxes(w, 1, 2) if n in _TRANSPOSED else w

    comm = ChunkComm({n: stacked(wts[n], n) for n in _BIG})
    gathers = comm.start_gathers()
    W = {}

    ds = norm_g.shape[2]
    small = jnp.concatenate([c.reshape(-1), norm_g.reshape(-1), conv_w.reshape(-1)]).astype(F32)
    small = small + gathers[-1]["token"][0, 0]
    n_small = small.shape[0]
    small = _pad_rows(small.reshape(-1, 1), 8 * LANES).reshape(-1, LANES)
    (small_all,) = all_gather([small], pltpu.VMEM, "gather_small")
    small_all = small_all.reshape(N_DEV, -1)[:, :n_small]
    c_all = small_all[:, :D]
    def full_rows(off, count):
        return jnp.stack([small_all[:, off + i * ds:off + (i + 1) * ds].reshape(D) for i in range(count)])

    W["norm_g"] = full_rows(D, DEPTH * 3).reshape(DEPTH, 3, D)
    W["conv_w"] = full_rows(D + DEPTH * 3 * ds, N_A_LAYERS * 3).reshape(N_A_LAYERS, 3, D)
    W["kv_norm_g"], W["final_norm_g"] = kv_norm_g, final_norm_g

    ada_b_mine = lax.dynamic_slice_in_dim(ada_b, me * nmod, nmod, axis=1).reshape(DEPTH, 1, nmod)
    kv_b_mine = lax.dynamic_slice_in_dim(kv_ada_b, me * nkv, nkv, axis=0).reshape(1, 1, nkv)
    mods_cols = mods_project(c_all, ada_w, ada_b_mine)
    kv_cols = mods_project(c_all, kv_ada_w.reshape(1, D, nkv), kv_b_mine)
    mcat = jnp.concatenate([mods_cols[l] for l in range(DEPTH)] + [kv_cols[0]], axis=1)
    wm = mcat.shape[1]
    if wm % LANES:
        mcat = jnp.concatenate([mcat, jnp.zeros((N_DEV, LANES - wm % LANES), F32)], axis=1)
    (mods_all,) = exchange_slots([mcat.reshape(N_DEV, 1, -1)], "exchange_mods")
    mods_all = mods_all.reshape(N_DEV, -1)
    mods = jnp.stack([mods_all[:, l * nmod:(l + 1) * nmod].reshape(N_MOD, D) for l in range(DEPTH)])
    kvmods = mods_all[:, DEPTH * nmod:DEPTH * nmod + nkv].reshape(2, D)

    loss_local, dx, grads = device_step(x[0], positions[0], loss_target[0], mods, kvmods, W, comm, gathers)
    loss = lax.psum(loss_local, MESH_AXES)

    dmods = grads["mods"].reshape(-1)
    dkvm = grads["kv_rows"][1:3].reshape(-1)
    vecs = jnp.concatenate([dmods, dkvm, grads["kv_rows"][0], grads["final_norm_g"], grads["norm_g"].reshape(-1),
                            grads["conv_w"].reshape(-1)])
    n_vec = vecs.shape[0]
    vecs = _pad_rows(vecs.reshape(-1, 1), 8 * LANES).reshape(-1, LANES)
    (vec_all,) = all_gather([vecs], pltpu.VMEM, "gather_vector_grads")
    vec_all = vec_all.reshape(N_DEV, -1)[:, :n_vec]
    nm_, nk_ = DEPTH * N_MOD * D, 2 * D
    dmods_all = vec_all[:, :nm_].reshape(N_DEV, DEPTH, N_MOD * D)
    dkvm_all = vec_all[:, nm_:nm_ + nk_]
    rest = vec_all[:, nm_ + nk_:]
    parts_kv_norm, parts_final = rest[:, :D].reshape(N_DEV, 1, D), rest[:, D:2 * D].reshape(N_DEV, 1, D)
    parts_norm = lax.dynamic_slice_in_dim(rest[:, 2 * D:2 * D + DEPTH * 3 * D].reshape(N_DEV, DEPTH * 3, D), me * ds, ds, axis=2)
    parts_conv = lax.dynamic_slice_in_dim(rest[:, 2 * D + DEPTH * 3 * D:].reshape(N_DEV, N_A_LAYERS * 3, D), me * ds, ds, axis=2)
    dm_cols = lax.dynamic_slice_in_dim(dmods_all, me * nmod, nmod, axis=2)
    dm_mine = jnp.stack([dm_cols[:, l] for l in range(DEPTH)])
    dkv_mine = lax.dynamic_slice_in_dim(dkvm_all, me * nkv, nkv, axis=1).reshape(1, N_DEV, nkv)
    g_ada_w = mods_weight_grad(c_all, dm_mine)
    g_kv_ada_w = mods_weight_grad(c_all, dkv_mine)[0]

    out_g, out_d, out_m, out_v = {}, {}, {}, {}

    def update(n, g, w, parts=False):
        shp = w.shape
        w2 = w.reshape(1, -1) if w.ndim == 1 else _flat2(w)
        g2 = g if parts else g.reshape(w2.shape)
        res = adam_update(g2, w2, mom[n].reshape(w2.shape), var[n].reshape(w2.shape), parts, "adam_" + n)
        out_g[n], out_d[n], out_m[n], out_v[n] = (r.reshape(shp) for r in res)

    moms = {n: stacked(mom[n], n) for n in _BIG}
    vars_ = {n: stacked(var[n], n) for n in _BIG}
    results = {}
    after = dx
    for ci, started in grads["exchanges"]:
        for (n, layer), parts in comm.finish_exchange(ci, started, after).items():
            idx = stacked_index(n, layer)
            results[n] = adam_layer(parts, comm.shards[n], moms[n], vars_[n], results.get(n), 0 if idx is None else idx,
                                    f"adam_{n}_{layer}")
            after = results[n][1]
    for n in _BIG:
        res = [jnp.swapaxes(r, 1, 2) if n in _TRANSPOSED else r for r in results[n]]
        out_g[n], out_d[n], out_m[n], out_v[n] = (r.reshape(wts[n].shape) for r in res)
    update("ada_w", g_ada_w, ada_w)
    update("kv_ada_w", g_kv_ada_w, kv_ada_w)
    update("ada_b", dmods_all, ada_b, True)
    update("kv_ada_b", dkvm_all.reshape(N_DEV, 1, nk_), kv_ada_b, True)
    update("kv_norm_g", parts_kv_norm, kv_norm_g, True)
    update("final_norm_g", parts_final, final_norm_g, True)
    update("norm_g", parts_norm, norm_g, True)
    update("conv_w", parts_conv, conv_w, True)

    return (loss, dx.reshape(x.shape), *[out_g[n] for n in names], *[out_d[n] for n in names],
            *[out_m[n] for n in names], *[out_v[n] for n in names])
```

```python
import functools

import jax
import jax.numpy as jnp
from jax import lax
from jax.experimental import pallas as pl
from jax.experimental.pallas import tpu as pltpu

F32, BF16 = jnp.float32, jnp.bfloat16

N_DEV = 8
MESH_AXES = ("x", "y", "c")
DEPTH = 4
N_A_LAYERS = 2
HEAD_DIM = 64
HEADS_PER_GROUP = 8
GROUP_WIDTH = HEAD_DIM * HEADS_PER_GROUP
DILATED_GROUPS = ((128, 1), (512, 4), (2048, 16))
ROPE_DIM = HEAD_DIM // 4
ROPE_THETA = 500000.0
NORM_EPS = 1e-5
FFN_RES_WEIGHT = 0.5
N_MOD = 9
ADAM_LR, ADAM_B1, ADAM_B2, ADAM_EPS, ADAM_WD, ADAM_STEP = 0.001, 0.9, 0.999, 1e-08, 0.01, 10

LANES = 128
TOKEN_TILE = 512
CONTRACT_TILE = 2048
MXU_WIDTH = 256
VMEM_LIMIT = 56 * 1024 * 1024
MESH = pl.DeviceIdType.MESH


def _cp(*sem):
    return pltpu.CompilerParams(dimension_semantics=sem, vmem_limit_bytes=VMEM_LIMIT)


def _pick(n, cap, mult=LANES):
    if n <= cap:
        return n
    best = None
    for t in range(mult, cap + 1, mult):
        if n % t == 0:
            best = t
    assert best is not None, (n, cap)
    return best


def _tok(tm, w):
    return pl.BlockSpec((tm, w), lambda i: (i, 0))


def _res(shape):
    nd = len(shape)
    return pl.BlockSpec(shape, lambda *_: (0,) * nd, pipeline_mode=pl.Buffered(1))


def _sds(shape, dt):
    return jax.ShapeDtypeStruct(shape, dt)


def _sigmoid(a):
    return 1.0 / (1.0 + jnp.exp(-a))


def _modnorm(x, g, sh, sc):
    r = lax.rsqrt(jnp.mean(x * x, axis=-1, keepdims=True) + NORM_EPS)
    return (x * r * g) * (1.0 + sc) + sh


def _dot(a, b):
    return jnp.dot(a, b, preferred_element_type=F32)


def _dot_nt(a, b):
    return lax.dot_general(a, b, (((1,), (1,)), ((), ())), preferred_element_type=F32)


def _dot_tn(a, b):
    return lax.dot_general(a, b, (((0,), (0,)), ((), ())), preferred_element_type=F32)


def _rows8(rows, d):
    pad = 8 - len(rows)
    return jnp.concatenate(list(rows) + [jnp.zeros((pad, d), F32)], axis=0)


def _acc_rows(ref, tile, first):
    @pl.when(first)
    def _():
        ref[...] = tile

    @pl.when(jnp.logical_not(first))
    def _():
        ref[...] += tile


def ffn_up(x, vec, w_in_t):
    T, D = x.shape
    F = w_in_t.shape[0] // 2
    tm, cw = min(TOKEN_TILE, T), _pick(F, MXU_WIDTH)

    def body(x_ref, vec_ref, w_ref, h_ref, ga_ref, gb_ref, u_ref):
        hb = _modnorm(x_ref[...], vec_ref[0:1], vec_ref[1:2], vec_ref[2:3]).astype(BF16)
        h_ref[...] = hb
        for c in range(F // cw):
            lo, hi = c * cw, (c + 1) * cw
            a = _dot_nt(hb, w_ref[lo:hi, :])
            b = _dot_nt(hb, w_ref[F + lo:F + hi, :])
            sg = _sigmoid(a)
            silu = a * sg
            ga_ref[:, lo:hi] = (b * (sg + silu * (1.0 - sg))).astype(BF16)
            gb_ref[:, lo:hi] = silu.astype(BF16)
            u_ref[:, lo:hi] = (silu * b).astype(BF16)

    return pl.pallas_call(
        body, grid=(T // tm,),
        in_specs=[_tok(tm, D), _res((8, D)), _res((2 * F, D))],
        out_specs=[_tok(tm, D), _tok(tm, F), _tok(tm, F), _tok(tm, F)],
        out_shape=[_sds((T, D), BF16), _sds((T, F), BF16), _sds((T, F), BF16), _sds((T, F), BF16)],
        compiler_params=_cp("arbitrary"), name="ffn_up")(x, vec, w_in_t)


def proj_out(u, x, vec, w_out, res_weight, name):
    T, D = x.shape
    K = u.shape[1]
    tm = min(TOKEN_TILE, T)

    def body(u_ref, x_ref, vec_ref, w_ref, xn_ref, y_ref):
        y = _dot(u_ref[...], w_ref[...])
        y_ref[...] = y.astype(BF16)
        xn_ref[...] = x_ref[...] + (res_weight * (1.0 + vec_ref[3:4])) * y

    return pl.pallas_call(
        body, grid=(T // tm,),
        in_specs=[_tok(tm, K), _tok(tm, D), _res((8, D)), _res((K, D))],
        out_specs=[_tok(tm, D), _tok(tm, D)],
        out_shape=[_sds((T, D), F32), _sds((T, D), BF16)],
        compiler_params=_cp("arbitrary"), name=name)(u, x, vec, w_out)


def ffn_down_bwd(dxo, y, vec, w_out, a, b):
    T, D = dxo.shape
    F = a.shape[1]
    tm, cw = min(TOKEN_TILE, T), _pick(F, MXU_WIDTH)

    def body(dxo_ref, y_ref, vec_ref, w_ref, a_ref, b_ref, dy_ref, dab_ref, part_ref):
        dxo_t = dxo_ref[...]
        dyb = (dxo_t * (FFN_RES_WEIGHT * (1.0 + vec_ref[3:4]))).astype(BF16)
        dy_ref[...] = dyb
        dgate = FFN_RES_WEIGHT * jnp.sum(dxo_t * y_ref[...].astype(F32), axis=0, keepdims=True)
        _acc_rows(part_ref, _rows8([dgate], D), pl.program_id(0) == 0)
        for c in range(F // cw):
            lo, hi = c * cw, (c + 1) * cw
            du = _dot_nt(dyb, w_ref[lo:hi, :])
            dab_ref[:, lo:hi] = (du * a_ref[:, lo:hi].astype(F32)).astype(BF16)
            dab_ref[:, F + lo:F + hi] = (du * b_ref[:, lo:hi].astype(F32)).astype(BF16)

    return pl.pallas_call(
        body, grid=(T // tm,),
        in_specs=[_tok(tm, D), _tok(tm, D), _res((8, D)), _res((F, D)), _tok(tm, F), _tok(tm, F)],
        out_specs=[_tok(tm, D), _tok(tm, 2 * F), pl.BlockSpec((8, D), lambda i: (0, 0))],
        out_shape=[_sds((T, D), BF16), _sds((T, 2 * F), BF16), _sds((8, D), F32)],
        compiler_params=_cp("arbitrary"), name="ffn_down_bwd")(dxo, y, vec, w_out, a, b)


def ffn_up_bwd(dab, w_in_t, x, dxo, vec):
    T, D = x.shape
    F2 = dab.shape[1]
    tm = min(TOKEN_TILE, T)

    def body(dab_ref, w_ref, x_ref, dxo_ref, vec_ref, dx_ref, part_ref):
        dh = _dot(dab_ref[...], w_ref[...])
        _, vjp = jax.vjp(_modnorm, x_ref[...], vec_ref[0:1], vec_ref[1:2], vec_ref[2:3])
        dx, dg, dsh, dsc = vjp(dh)
        dx_ref[...] = dxo_ref[...] + dx
        _acc_rows(part_ref, _rows8([dg, dsh, dsc], D), pl.program_id(0) == 0)

    return pl.pallas_call(
        body, grid=(T // tm,),
        in_specs=[_tok(tm, F2), _res((F2, D)), _tok(tm, D), _tok(tm, D), _res((8, D))],
        out_specs=[_tok(tm, D), pl.BlockSpec((8, D), lambda i: (0, 0))],
        out_shape=[_sds((T, D), F32), _sds((8, D), F32)],
        compiler_params=_cp("arbitrary"), name="ffn_up_bwd")(dab, w_in_t, x, dxo, vec)


def grad_slots(a, b, name, col_slots=False):
    T, M = a.shape
    N = b.shape[1]
    tk = min(CONTRACT_TILE, T)
    nk = T // tk
    tmm = _pick(M, 1408)
    if col_slots:
        ns = N // N_DEV
        sp = max(s for s in (1, 2, 4, 8) if ns * s <= 1536)
        tn = ns * sp
    else:
        tn = _pick(N, 1536)

    def body(a_ref, b_ref, o_ref, acc):
        k = pl.program_id(2)
        t = _dot_tn(a_ref[...], b_ref[...])

        @pl.when(k == 0)
        def _():
            acc[...] = t

        @pl.when(k > 0)
        def _():
            acc[...] += t

        @pl.when(k == nk - 1)
        def _():
            if col_slots:
                for s in range(sp):
                    o_ref[s] = acc[:, s * ns:(s + 1) * ns].astype(BF16)
            else:
                o_ref[...] = acc[...].astype(BF16)

    if col_slots:
        out_spec, out_shape = pl.BlockSpec((sp, tmm, ns), lambda i, j, k: (j, i, 0)), _sds((N_DEV, M, ns), BF16)
    else:
        out_spec, out_shape = pl.BlockSpec((tmm, tn), lambda i, j, k: (i, j)), _sds((M, N), BF16)
    out = pl.pallas_call(
        body, grid=(M // tmm, N // tn, nk),
        in_specs=[pl.BlockSpec((tk, tmm), lambda i, j, k: (k, i)), pl.BlockSpec((tk, tn), lambda i, j, k: (k, j))],
        out_specs=out_spec, out_shape=out_shape,
        scratch_shapes=[pltpu.VMEM((tmm, tn), F32)],
        compiler_params=_cp("arbitrary", "arbitrary", "arbitrary"), name=name)(a, b)
    return out if col_slots else out.reshape(N_DEV, M // N_DEV, N)


def conv_fwd(x, vec, cw, w_in, w_out):
    T, D = x.shape
    tm = min(TOKEN_TILE, T)

    def body(x_ref, vec_ref, cw_ref, wi_ref, wo_ref, xn_ref, h_ref, bcu_ref, cv_ref, z_ref, y_ref, vbuf):
        @pl.when(pl.program_id(0) == 0)
        def _():
            vbuf[0:8, :] = jnp.zeros((8, D), F32)

        x_t = x_ref[...]
        hb = _modnorm(x_t, vec_ref[0:1], vec_ref[1:2], vec_ref[2:3]).astype(BF16)
        h_ref[...] = hb
        bcu = _dot(hb, wi_ref[...])
        bcu_ref[...] = bcu.astype(BF16)
        bg, v = bcu[:, 0:D], bcu[:, D:2 * D] * bcu[:, 2 * D:3 * D]
        vbuf[8:8 + tm, :] = v
        conv = cw_ref[0:1] * vbuf[6:6 + tm, :] + cw_ref[1:2] * vbuf[7:7 + tm, :] + cw_ref[2:3] * v
        cv_ref[...] = conv.astype(BF16)
        zb = (bg * conv).astype(BF16)
        z_ref[...] = zb
        y = _dot(zb, wo_ref[...])
        y_ref[...] = y.astype(BF16)
        xn_ref[...] = x_t + (1.0 + vec_ref[3:4]) * y
        vbuf[0:8, :] = vbuf[tm:tm + 8, :]

    return pl.pallas_call(
        body, grid=(T // tm,),
        in_specs=[_tok(tm, D), _res((8, D)), _res((8, D)), _res((D, 3 * D)), _res((D, D))],
        out_specs=[_tok(tm, D), _tok(tm, D), _tok(tm, 3 * D), _tok(tm, D), _tok(tm, D), _tok(tm, D)],
        out_shape=[_sds((T, D), F32), _sds((T, D), BF16), _sds((T, 3 * D), BF16), _sds((T, D), BF16),
                   _sds((T, D), BF16), _sds((T, D), BF16)],
        scratch_shapes=[pltpu.VMEM((tm + 8, D), F32)],
        compiler_params=_cp("arbitrary"), name="conv_fwd")(x, vec, cw, w_in, w_out)


def conv_bwd(dxo, x, y, bcu, cv, vec, cw, w_in, w_out):
    T, D = x.shape
    tm = min(TOKEN_TILE, T)
    nt = T // tm

    def body(dxo_ref, x_ref, y_ref, bcu_ref, cv_ref, vec_ref, cw_ref, wi_ref, wo_ref,
             dx_ref, dy_ref, dbcu_ref, part_ref, dcw_ref, dcbuf):
        first = pl.program_id(0) == 0

        @pl.when(first)
        def _():
            dcbuf[tm:tm + 8, :] = jnp.zeros((8, D), F32)

        dxo_t = dxo_ref[...]
        dyb = (dxo_t * (1.0 + vec_ref[3:4])).astype(BF16)
        dy_ref[...] = dyb
        dgate = jnp.sum(dxo_t * y_ref[...].astype(F32), axis=0, keepdims=True)
        dz = _dot_nt(dyb, wo_ref[...])
        bcu_t = bcu_ref[...].astype(F32)
        bg, cg, ug = bcu_t[:, 0:D], bcu_t[:, D:2 * D], bcu_t[:, 2 * D:3 * D]
        dconv = dz * bg
        dbg = dz * cv_ref[...].astype(F32)
        dcbuf[0:tm, :] = dconv
        d1, d2 = dcbuf[1:tm + 1, :], dcbuf[2:tm + 2, :]
        dv = cw_ref[2:3] * dconv + cw_ref[1:2] * d1 + cw_ref[0:1] * d2
        v = cg * ug
        dcw = _rows8([jnp.sum(d2 * v, axis=0, keepdims=True), jnp.sum(d1 * v, axis=0, keepdims=True),
                      jnp.sum(dconv * v, axis=0, keepdims=True)], D)
        dbcu = jnp.concatenate([dbg, dv * ug, dv * cg], axis=1).astype(BF16)
        dbcu_ref[...] = dbcu
        dh = _dot_nt(dbcu, wi_ref[...])
        _, vjp = jax.vjp(_modnorm, x_ref[...], vec_ref[0:1], vec_ref[1:2], vec_ref[2:3])
        dx, dg, dsh, dsc = vjp(dh)
        dx_ref[...] = dxo_t + dx
        _acc_rows(part_ref, _rows8([dg, dsh, dsc, dgate], D), first)
        _acc_rows(dcw_ref, dcw, first)
        dcbuf[tm:tm + 8, :] = dcbuf[0:8, :]

    def rev(w):
        return pl.BlockSpec((tm, w), lambda i: (nt - 1 - i, 0))

    return pl.pallas_call(
        body, grid=(nt,),
        in_specs=[rev(D), rev(D), rev(D), rev(3 * D), rev(D), _res((8, D)), _res((8, D)), _res((D, 3 * D)), _res((D, D))],
        out_specs=[rev(D), rev(D), rev(3 * D), pl.BlockSpec((8, D), lambda i: (0, 0)), pl.BlockSpec((8, D), lambda i: (0, 0))],
        out_shape=[_sds((T, D), F32), _sds((T, D), BF16), _sds((T, 3 * D), BF16), _sds((8, D), F32), _sds((8, D), F32)],
        scratch_shapes=[pltpu.VMEM((tm + 8, D), F32)],
        compiler_params=_cp("arbitrary"), name="conv_bwd")(dxo, x, y, bcu, cv, vec, cw, w_in, w_out)


def rope_tables(pos, lane_rows):
    T = pos.shape[0]
    tm = min(TOKEN_TILE, T)

    def body(p_ref, lr_ref, c_ref, sp_ref, sm_ref):
        ang = p_ref[...].astype(F32) * lr_ref[0:1]
        cs, sn = jnp.cos(ang), jnp.sin(ang)
        c_ref[...] = jnp.where(lr_ref[1:2] > 0.5, cs, 1.0)
        sp_ref[...] = jnp.where(lr_ref[2:3] > 0.5, sn, 0.0)
        sm_ref[...] = jnp.where(lr_ref[3:4] > 0.5, -sn, 0.0)

    return pl.pallas_call(
        body, grid=(T // tm,),
        in_specs=[_tok(tm, 1), _res((8, LANES))],
        out_specs=[_tok(tm, LANES)] * 3,
        out_shape=[_sds((T, LANES), F32)] * 3,
        compiler_params=_cp("arbitrary"), name="rope_tables")(pos, lane_rows)


def _rope(t, c, sp, sm):
    w = t.shape[1]
    reps = w // LANES
    cf, spf, smf = jnp.tile(c, (1, reps)), jnp.tile(sp, (1, reps)), jnp.tile(sm, (1, reps))
    half = ROPE_DIM // 2
    return t * cf + pltpu.roll(t, half, axis=1) * spf + pltpu.roll(t, w - half, axis=1) * smf


def _rope_t(d, c, sp, sm):
    w = d.shape[1]
    reps = w // LANES
    cf, spf, smf = jnp.tile(c, (1, reps)), jnp.tile(sp, (1, reps)), jnp.tile(sm, (1, reps))
    half = ROPE_DIM // 2
    return d * cf + pltpu.roll(d * spf, w - half, axis=1) + pltpu.roll(d * smf, half, axis=1)


def proj_rope_fwd(x, vec, w, tabs, n_rope, transposed, name):
    T, D = x.shape
    N = w.shape[0] if transposed else w.shape[1]
    tm = min(TOKEN_TILE, T)

    def body(x_ref, vec_ref, w_ref, c_ref, sp_ref, sm_ref, h_ref, pr_ref, *rest_ref):
        hb = _modnorm(x_ref[...], vec_ref[0:1], vec_ref[1:2], vec_ref[2:3]).astype(BF16)
        h_ref[...] = hb
        p = _dot_nt(hb, w_ref[...]) if transposed else _dot(hb, w_ref[...])
        pr_ref[...] = _rope(p[:, 0:n_rope], c_ref[...], sp_ref[...], sm_ref[...]).astype(BF16)
        if rest_ref:
            rest_ref[0][...] = p[:, n_rope:N].astype(BF16)

    widths = [n_rope] + ([N - n_rope] if n_rope < N else [])
    return pl.pallas_call(
        body, grid=(T // tm,),
        in_specs=[_tok(tm, D), _res((8, D)), _res(w.shape)] + [_tok(tm, LANES)] * 3,
        out_specs=[_tok(tm, D)] + [_tok(tm, wd) for wd in widths],
        out_shape=[_sds((T, D), BF16)] + [_sds((T, wd), BF16) for wd in widths],
        compiler_params=_cp("arbitrary"), name=name)(x, vec, w, *tabs)


def proj_rope_bwd(dparts, x, dxo, vec, w, tabs, n_rope, transposed, name):
    T, D = x.shape
    N = w.shape[0] if transposed else w.shape[1]
    tm = min(TOKEN_TILE, T)
    npart = len(dparts)

    def body(*refs):
        d_refs = refs[:npart]
        x_ref, dxo_ref, vec_ref, w_ref, c_ref, sp_ref, sm_ref, dx_ref, dp_ref, part_ref = refs[npart:]
        d = jnp.concatenate([r[...].astype(F32) for r in d_refs], axis=1)
        dr = _rope_t(d[:, 0:n_rope], c_ref[...], sp_ref[...], sm_ref[...])
        if n_rope < N:
            dr = jnp.concatenate([dr, d[:, n_rope:N]], axis=1)
        dpb = dr.astype(BF16)
        dp_ref[...] = dpb
        dh = _dot(dpb, w_ref[...]) if transposed else _dot_nt(dpb, w_ref[...])
        _, vjp = jax.vjp(_modnorm, x_ref[...], vec_ref[0:1], vec_ref[1:2], vec_ref[2:3])
        dx, dg, dsh, dsc = vjp(dh)
        dx_ref[...] = dxo_ref[...] + dx
        _acc_rows(part_ref, _rows8([dg, dsh, dsc], D), pl.program_id(0) == 0)

    return pl.pallas_call(
        body, grid=(T // tm,),
        in_specs=[_tok(tm, p.shape[1]) for p in dparts] + [_tok(tm, D), _tok(tm, D), _res((8, D)), _res(w.shape)]
        + [_tok(tm, LANES)] * 3,
        out_specs=[_tok(tm, D), _tok(tm, N), pl.BlockSpec((8, D), lambda i: (0, 0))],
        out_shape=[_sds((T, D), F32), _sds((T, N), BF16), _sds((8, D), F32)],
        compiler_params=_cp("arbitrary"), name=name)(*dparts, x, dxo, vec, w, *tabs)


def _valid_mask(n, i):
    qi = lax.broadcasted_iota(jnp.int32, (n, 2 * n), 0)
    kj = lax.broadcasted_iota(jnp.int32, (n, 2 * n), 1)
    dist = n + qi - kj
    return (dist >= 0) & (dist <= n) & ((kj >= n) | (i > 0))


def attn_core_fwd(q, k, v, g, n, d):
    T, QW = q.shape
    GW = GROUP_WIDTH
    ng = QW // GW
    M = T // d
    scale = HEAD_DIM ** -0.5

    def body(q_ref, kp_ref, kc_ref, vp_ref, vc_ref, o_ref, l_ref):
        valid = _valid_mask(n, pl.program_id(1))
        qv = q_ref[...]
        kk = jnp.concatenate([kp_ref[...], kc_ref[...]], axis=0)
        vv = jnp.concatenate([vp_ref[...], vc_ref[...]], axis=0)
        for h in range(HEADS_PER_GROUP):
            hs = slice(HEAD_DIM * h, HEAD_DIM * (h + 1))
            s = jnp.where(valid, _dot_nt(qv[:, hs], kk[:, hs]) * scale, -1e30)
            m = jnp.max(s, axis=1, keepdims=True)
            p = jnp.exp(s - m)
            den = jnp.sum(p, axis=1, keepdims=True)
            o_ref[:, hs] = _dot((p / den).astype(BF16), vv[:, hs])
            l_ref[:, hs] = jnp.broadcast_to(m + jnp.log(den), (n, HEAD_DIM))

    cur = pl.BlockSpec((n, GW), lambda r, i: (i, r * ng + g))
    prv = pl.BlockSpec((n, GW), lambda r, i: (jnp.maximum(i - 1, 0), r * ng + g))
    out = pl.BlockSpec((n, GW), lambda r, i: (i, r))
    qv, kv, vv = q.reshape(M, d * QW), k.reshape(M, d * QW), v.reshape(M, d * QW)
    o, l = pl.pallas_call(
        body, grid=(d, M // n),
        in_specs=[cur, prv, cur, prv, cur], out_specs=[out, out],
        out_shape=[_sds((M, d * GW), F32), _sds((M, d * GW), F32)],
        compiler_params=_cp("arbitrary", "arbitrary"), name=f"attn_fwd_g{g}")(qv, kv, kv, vv, vv)
    return o.reshape(T, GW), l.reshape(T, GW)


def attn_core_bwd(q, k, v, do, rr, lse, g, n, d):
    T, QW = q.shape
    GW = GROUP_WIDTH
    ng = QW // GW
    M = T // d
    scale = HEAD_DIM ** -0.5

    def body(q_ref, kp_ref, kc_ref, vp_ref, vc_ref, do_ref, r_ref, l_ref, dq_ref, dkc_ref, dkp_ref, dvc_ref, dvp_ref):
        valid = _valid_mask(n, pl.program_id(1))
        qv, dov = q_ref[...], do_ref[...]
        kk = jnp.concatenate([kp_ref[...], kc_ref[...]], axis=0)
        vv = jnp.concatenate([vp_ref[...], vc_ref[...]], axis=0)
        for h in range(HEADS_PER_GROUP):
            hs = slice(HEAD_DIM * h, HEAD_DIM * (h + 1))
            s = jnp.where(valid, _dot_nt(qv[:, hs], kk[:, hs]) * scale, -1e30)
            p = jnp.exp(s - l_ref[:, HEAD_DIM * h:HEAD_DIM * h + 1])
            dp = _dot_nt(dov[:, hs], vv[:, hs])
            delta = jnp.sum(r_ref[:, hs], axis=1, keepdims=True)
            ds = (p * (dp - delta) * scale).astype(BF16)
            dq_ref[:, hs] = _dot(ds, kk[:, hs]).astype(BF16)
            dk = _dot_tn(ds, qv[:, hs]).astype(BF16)
            dv = _dot_tn(p.astype(BF16), dov[:, hs]).astype(BF16)
            dkp_ref[:, hs], dkc_ref[:, hs] = dk[0:n], dk[n:2 * n]
            dvp_ref[:, hs], dvc_ref[:, hs] = dv[0:n], dv[n:2 * n]

    cur = pl.BlockSpec((n, GW), lambda r, i: (i, r * ng + g))
    prv = pl.BlockSpec((n, GW), lambda r, i: (jnp.maximum(i - 1, 0), r * ng + g))
    blk = pl.BlockSpec((n, GW), lambda r, i: (i, r))
    qv, kv, vv = q.reshape(M, d * QW), k.reshape(M, d * QW), v.reshape(M, d * QW)
    outs = pl.pallas_call(
        body, grid=(d, M // n),
        in_specs=[cur, prv, cur, prv, cur, blk, blk, blk], out_specs=[blk] * 5,
        out_shape=[_sds((M, d * GW), BF16)] * 5,
        compiler_params=_cp("arbitrary", "arbitrary"), name=f"attn_bwd_g{g}")(
            qv, kv, kv, vv, vv, do.reshape(M, d * GW), rr.reshape(M, d * GW), lse.reshape(M, d * GW))
    return [o.reshape(T, GW) for o in outs]


def dkv_combine(cur_prev, n, d, name):
    T, GW = cur_prev[0][0].shape
    M = T // d
    nb = M // n
    flat = [a.reshape(M, d * GW) for pair in cur_prev for a in pair]

    def body(*refs):
        o_ref = refs[-1]
        last = pl.program_id(1) == nb - 1
        acc = jnp.zeros((n, GW), F32)
        for t in range(0, len(refs) - 1, 2):
            acc = acc + refs[t][...].astype(F32) + jnp.where(last, 0.0, refs[t + 1][...].astype(F32))
        o_ref[...] = acc.astype(BF16)

    cur = pl.BlockSpec((n, GW), lambda r, i: (i, r))
    nxt = pl.BlockSpec((n, GW), lambda r, i: (jnp.minimum(i + 1, nb - 1), r))
    out = pl.pallas_call(
        body, grid=(d, nb), in_specs=[cur, nxt] * len(cur_prev), out_specs=cur,
        out_shape=_sds((M, d * GW), BF16),
        compiler_params=_cp("arbitrary", "arbitrary"), name=name)(*flat)
    return out.reshape(T, GW)


def _group_weights(ls):
    mx = functools.reduce(jnp.maximum, ls)
    es = [jnp.exp(l - mx) for l in ls]
    tot = functools.reduce(lambda a, b: a + b, es)
    return [e / tot for e in es]


def attn_mix_out(os_, ls, x, vec, w_o):
    T, D = x.shape
    GW = GROUP_WIDTH
    tm = min(TOKEN_TILE, T)
    ng = len(os_)

    def body(*refs):
        o_refs, l_refs = refs[:ng], refs[ng:2 * ng]
        x_ref, vec_ref, w_ref, xn_ref, mix_ref, y_ref = refs[2 * ng:]
        ws = _group_weights([r[...] for r in l_refs])
        mixed = functools.reduce(lambda a, b: a + b, [w * r[...] for w, r in zip(ws, o_refs)])
        mb = mixed.astype(BF16)
        mix_ref[...] = mb
        y = _dot(mb, w_ref[...])
        y_ref[...] = y.astype(BF16)
        xn_ref[...] = x_ref[...] + (1.0 + vec_ref[3:4]) * y

    return pl.pallas_call(
        body, grid=(T // tm,),
        in_specs=[_tok(tm, GW)] * (2 * ng) + [_tok(tm, D), _res((8, D)), _res((GW, D))],
        out_specs=[_tok(tm, D), _tok(tm, GW), _tok(tm, D)],
        out_shape=[_sds((T, D), F32), _sds((T, GW), BF16), _sds((T, D), BF16)],
        compiler_params=_cp("arbitrary"), name="attn_mix_out")(*os_, *ls, x, vec, w_o)


def attn_mix_bwd(dxo, y, vec, w_o, os_, ls):
    T, D = dxo.shape
    GW = GROUP_WIDTH
    tm = min(TOKEN_TILE, T)
    ng = len(os_)

    def body(*refs):
        dxo_ref, y_ref, vec_ref, w_ref = refs[:4]
        o_refs, l_refs = refs[4:4 + ng], refs[4 + ng:4 + 2 * ng]
        dy_ref = refs[4 + 2 * ng]
        do_refs = refs[5 + 2 * ng:5 + 3 * ng]
        r_refs = refs[5 + 3 * ng:5 + 4 * ng]
        part_ref = refs[5 + 4 * ng]
        dxo_t = dxo_ref[...]
        dyb = (dxo_t * (1.0 + vec_ref[3:4])).astype(BF16)
        dy_ref[...] = dyb
        dgate = jnp.sum(dxo_t * y_ref[...].astype(F32), axis=0, keepdims=True)
        _acc_rows(part_ref, _rows8([dgate], D), pl.program_id(0) == 0)
        dmix = _dot_nt(dyb, w_ref[...])
        ws = _group_weights([r[...] for r in l_refs])
        ov = [r[...] for r in o_refs]
        mixed = functools.reduce(lambda a, b: a + b, [w * o for w, o in zip(ws, ov)])
        for gi in range(ng):
            do = ws[gi] * dmix
            do_refs[gi][...] = do.astype(BF16)
            r_refs[gi][...] = do * mixed

    return pl.pallas_call(
        body, grid=(T // tm,),
        in_specs=[_tok(tm, D), _tok(tm, D), _res((8, D)), _res((GW, D))] + [_tok(tm, GW)] * (2 * ng),
        out_specs=[_tok(tm, D)] + [_tok(tm, GW)] * (2 * ng) + [pl.BlockSpec((8, D), lambda i: (0, 0))],
        out_shape=[_sds((T, D), BF16)] + [_sds((T, GW), BF16)] * ng + [_sds((T, GW), F32)] * ng + [_sds((8, D), F32)],
        compiler_params=_cp("arbitrary"), name="attn_mix_bwd")(dxo, y, vec, w_o, *os_, *ls)


def final_loss(x, gvec, target):
    T, D = x.shape
    tm = min(TOKEN_TILE, T)

    def norm(xv, g):
        return xv * lax.rsqrt(jnp.mean(xv * xv, axis=-1, keepdims=True) + NORM_EPS) * g

    def body(x_ref, g_ref, t_ref, dx_ref, part_ref, loss_ref):
        first = pl.program_id(0) == 0
        yv, vjp = jax.vjp(norm, x_ref[...], g_ref[0:1])
        err = yv - t_ref[...]
        dx, dg = vjp(err * (1.0 / D))
        dx_ref[...] = dx
        _acc_rows(part_ref, _rows8([dg], D), first)
        tile_loss = 0.5 * jnp.sum(jnp.sum(err * err, axis=1, keepdims=True) * (1.0 / D), axis=0, keepdims=True)
        _acc_rows(loss_ref, jnp.broadcast_to(tile_loss, (8, LANES)), first)

    return pl.pallas_call(
        body, grid=(T // tm,),
        in_specs=[_tok(tm, D), _res((8, D)), _tok(tm, D)],
        out_specs=[_tok(tm, D), pl.BlockSpec((8, D), lambda i: (0, 0)), pl.BlockSpec((8, LANES), lambda i: (0, 0))],
        out_shape=[_sds((T, D), F32), _sds((8, D), F32), _sds((8, LANES), F32)],
        compiler_params=_cp("arbitrary"), name="final_loss")(x, gvec, target)


def mods_project(c_all, w, b):
    B, D = c_all.shape
    L, _, N = w.shape

    def body(c_ref, w_ref, b_ref, o_ref):
        cv = c_ref[...]
        cond = cv * _sigmoid(cv)
        o_ref[0] = jnp.dot(cond, w_ref[0], preferred_element_type=F32, precision=lax.Precision.HIGHEST) + b_ref[0]

    return pl.pallas_call(
        body, grid=(L,),
        in_specs=[pl.BlockSpec((B, D), lambda l: (0, 0)), pl.BlockSpec((1, D, N), lambda l: (l, 0, 0)),
                  pl.BlockSpec((1, 1, N), lambda l: (l, 0, 0))],
        out_specs=pl.BlockSpec((1, B, N), lambda l: (l, 0, 0)),
        out_shape=_sds((L, B, N), F32),
        compiler_params=_cp("arbitrary"), name="mods_project")(c_all, w, b)


def mods_weight_grad(c_all, dm):
    B, D = c_all.shape
    L, _, N = dm.shape

    def body(c_ref, d_ref, o_ref):
        cv = c_ref[...]
        cond = cv * _sigmoid(cv)
        o_ref[0] = lax.dot_general(cond, d_ref[0], (((0,), (0,)), ((), ())), preferred_element_type=F32,
                                   precision=lax.Precision.HIGHEST)

    return pl.pallas_call(
        body, grid=(L,),
        in_specs=[pl.BlockSpec((B, D), lambda l: (0, 0)), pl.BlockSpec((1, B, N), lambda l: (l, 0, 0))],
        out_specs=pl.BlockSpec((1, D, N), lambda l: (l, 0, 0)),
        out_shape=_sds((L, D, N), F32),
        compiler_params=_cp("arbitrary"), name="mods_weight_grad")(c_all, dm)


def _adam_math(g, w, m, v):
    m2 = ADAM_B1 * m + (1.0 - ADAM_B1) * g
    v2 = ADAM_B2 * v + (1.0 - ADAM_B2) * (g * g)
    m_hat = m2 / (1.0 - ADAM_B1 ** ADAM_STEP)
    v_hat = v2 / (1.0 - ADAM_B2 ** ADAM_STEP)
    delta = -ADAM_LR * (m_hat / (jnp.sqrt(v_hat) + ADAM_EPS) + ADAM_WD * w)
    return delta, m2, v2


def adam_update(g, w, m, v, parts, name):
    R, C = w.shape
    tr = _pick(R, 256, 8)

    def body(g_ref, w_ref, m_ref, v_ref, go_ref, d_ref, mo_ref, vo_ref):
        if parts:
            gv = g_ref[0].astype(F32)
            for s in range(1, N_DEV):
                gv = gv + g_ref[s].astype(F32)
        else:
            gv = g_ref[...]
        go_ref[...] = gv
        d_ref[...], mo_ref[...], vo_ref[...] = _adam_math(gv, w_ref[...], m_ref[...], v_ref[...])

    gspec = pl.BlockSpec((N_DEV, tr, C), lambda i: (0, i, 0)) if parts else _tok(tr, C)
    return pl.pallas_call(
        body, grid=(R // tr,),
        in_specs=[gspec, _tok(tr, C), _tok(tr, C), _tok(tr, C)],
        out_specs=[_tok(tr, C)] * 4, out_shape=[_sds((R, C), F32)] * 4,
        compiler_params=_cp("arbitrary"), name=name)(g, w, m, v)


def adam_layer(parts, w, m, v, prev, layer, name):
    L, R, C = w.shape
    tr = _pick(R, 256, 8)

    def body(p_ref, w_ref, m_ref, v_ref, *rest):
        go_ref, d_ref, mo_ref, vo_ref = rest[-4:]
        gv = p_ref[0].astype(F32)
        for s in range(1, N_DEV):
            gv = gv + p_ref[s].astype(F32)
        go_ref[...] = gv
        d_ref[...], mo_ref[...], vo_ref[...] = _adam_math(gv, w_ref[...], m_ref[...], v_ref[...])

    lay = pl.BlockSpec((None, tr, C), lambda i: (layer, i, 0))
    prev = list(prev) if prev is not None else []
    return pl.pallas_call(
        body, grid=(R // tr,),
        in_specs=[pl.BlockSpec((N_DEV, tr, C), lambda i: (0, i, 0)), lay, lay, lay] + [pl.BlockSpec(memory_space=pl.ANY)] * len(prev),
        out_specs=[lay] * 4, out_shape=[_sds((L, R, C), F32)] * 4,
        input_output_aliases={4 + k: k for k in range(len(prev))},
        compiler_params=_cp("arbitrary"), name=name)(parts, w, m, v, *prev)


def _my_id():
    return 4 * lax.axis_index("x") + 2 * lax.axis_index("y") + lax.axis_index("c")


def _peer(s):
    x, y, c = lax.axis_index("x"), lax.axis_index("y"), lax.axis_index("c")
    px = (1 - x) if s & 4 else x
    py = (1 - y) if s & 2 else y
    pc = (1 - c) if s & 1 else c
    return (px, py, pc), 4 * px + 2 * py + pc


def all_gather(xs, space, name):
    na = len(xs)

    def body(*refs):
        x_refs, o_refs = refs[:na], refs[na:2 * na]
        send_sems, recv_sems, local_sems = refs[2 * na:]
        me = _my_id()
        locals_, sends = [], []
        for a in range(na):
            cp = pltpu.make_async_copy(x_refs[a], o_refs[a].at[me], local_sems.at[a])
            cp.start()
            locals_.append(cp)
        for s in range(1, N_DEV):
            peer, _ = _peer(s)
            for a in range(na):
                cp = pltpu.make_async_remote_copy(
                    src_ref=x_refs[a], dst_ref=o_refs[a].at[me], send_sem=send_sems.at[a, s - 1],
                    recv_sem=recv_sems.at[a, s - 1], device_id=peer, device_id_type=MESH)
                cp.start()
                sends.append(cp)
        for s in range(1, N_DEV):
            peer, pid = _peer(s)
            for a in range(na):
                pltpu.make_async_remote_copy(
                    src_ref=x_refs[a], dst_ref=o_refs[a].at[pid], send_sem=send_sems.at[a, s - 1],
                    recv_sem=recv_sems.at[a, s - 1], device_id=peer, device_id_type=MESH).wait_recv()
        for cp in sends:
            cp.wait_send()
        for cp in locals_:
            cp.wait()

    spec = pl.BlockSpec(memory_space=space)
    return pl.pallas_call(
        body, in_specs=[spec] * na, out_specs=[spec] * na,
        out_shape=[_sds((N_DEV,) + x.shape, x.dtype) for x in xs],
        scratch_shapes=[pltpu.SemaphoreType.DMA((na, N_DEV - 1)), pltpu.SemaphoreType.DMA((na, N_DEV - 1)),
                        pltpu.SemaphoreType.DMA((na,))],
        compiler_params=pltpu.CompilerParams(vmem_limit_bytes=VMEM_LIMIT), name=name)(*xs)


def exchange_slots(xs, name):
    na = len(xs)

    def body(*refs):
        x_refs, o_refs = refs[:na], refs[na:2 * na]
        send_sems, recv_sems, local_sems = refs[2 * na:]
        me = _my_id()
        locals_, sends = [], []
        for a in range(na):
            cp = pltpu.make_async_copy(x_refs[a].at[me], o_refs[a].at[me], local_sems.at[a])
            cp.start()
            locals_.append(cp)
        for s in range(1, N_DEV):
            peer, pid = _peer(s)
            for a in range(na):
                cp = pltpu.make_async_remote_copy(
                    src_ref=x_refs[a].at[pid], dst_ref=o_refs[a].at[me], send_sem=send_sems.at[a, s - 1],
                    recv_sem=recv_sems.at[a, s - 1], device_id=peer, device_id_type=MESH)
                cp.start()
                sends.append(cp)
        for s in range(1, N_DEV):
            peer, pid = _peer(s)
            for a in range(na):
                pltpu.make_async_remote_copy(
                    src_ref=x_refs[a].at[pid], dst_ref=o_refs[a].at[pid], send_sem=send_sems.at[a, s - 1],
                    recv_sem=recv_sems.at[a, s - 1], device_id=peer, device_id_type=MESH).wait_recv()
        for cp in sends:
            cp.wait_send()
        for cp in locals_:
            cp.wait()

    spec = pl.BlockSpec(memory_space=pl.ANY)
    return pl.pallas_call(
        body, in_specs=[spec] * na, out_specs=[spec] * na,
        out_shape=[_sds(x.shape, x.dtype) for x in xs],
        scratch_shapes=[pltpu.SemaphoreType.DMA((na, N_DEV - 1)), pltpu.SemaphoreType.DMA((na, N_DEV - 1)),
                        pltpu.SemaphoreType.DMA((na,))],
        compiler_params=pltpu.CompilerParams(vmem_limit_bytes=VMEM_LIMIT), name=name)(*xs)


_HBM = pl.BlockSpec(memory_space=pltpu.HBM)
_SEM = pl.BlockSpec(memory_space=pltpu.SEMAPHORE)
_EFFECT = pltpu.SideEffectType.DATAFLOW_SIDE_EFFECTING


def _split_copy(x_ref, land_ref, s, send_sem, recv_sem, scatter):
    peer, pid = _peer(s)
    src = x_ref.at[pid] if scatter else x_ref
    return pltpu.make_async_remote_copy(src_ref=src, dst_ref=land_ref.at[_my_id()], send_sem=send_sem, recv_sem=recv_sem,
                                        device_id=peer, device_id_type=MESH)


def comm_start(xs, scatter, after, name):
    na = len(xs)
    extra = [] if after is None else [after]
    me = _my_id()
    lands = []
    for x in xs:
        shape = x.shape if scatter else (N_DEV,) + x.shape
        own = lax.dynamic_slice_in_dim(x, me, 1, 0) if scatter else x[None]
        lands.append(lax.dynamic_update_slice(lax.empty(shape, x.dtype), own, (me,) + (0,) * (len(shape) - 1)))

    def body(*refs):
        x_refs, land_refs = refs[:na], refs[na:2 * na]
        send_sem, recv_sem = refs[2 * na + len(extra)], refs[2 * na + len(extra) + 1]
        token = refs[-1]
        for s in range(1, N_DEV):
            for a in range(na):
                _split_copy(x_refs[a], land_refs[a], s, send_sem, recv_sem, scatter).start()
        token[...] = jnp.zeros_like(token)

    outs = pl.pallas_call(
        body, name=name,
        out_shape=(pltpu.SemaphoreType.DMA(()), pltpu.SemaphoreType.DMA(()))
        + tuple(pltpu.HBM(x.shape, x.dtype) for x in xs) + tuple(pltpu.HBM(l.shape, l.dtype) for l in lands)
        + (_sds((8, LANES), F32),),
        in_specs=(_HBM,) * (2 * na) + (pl.BlockSpec(memory_space=pl.ANY),) * len(extra),
        out_specs=(_SEM, _SEM) + (_HBM,) * (2 * na) + (pl.BlockSpec(memory_space=pltpu.VMEM),),
        input_output_aliases={a: 2 + a for a in range(2 * na)},
        compiler_params=pltpu.CompilerParams(has_side_effects=_EFFECT),
    )(*[pltpu.with_memory_space_constraint(x, pltpu.HBM) for x in xs],
      *[pltpu.with_memory_space_constraint(l, pltpu.HBM) for l in lands], *extra)
    return dict(sems=outs[0:2], xs=outs[2:2 + na], lands=outs[2 + na:2 + 2 * na], token=outs[-1], scatter=scatter)


def comm_wait(started, after, name):
    xs, lands = started["xs"], started["lands"]
    scatter = started["scatter"]
    na = len(xs)

    def body(*refs):
        x_refs, land_refs = refs[:na], refs[na:2 * na]
        send_sem, recv_sem = refs[2 * na], refs[2 * na + 1]
        for s in range(1, N_DEV):
            for a in range(na):
                cp = _split_copy(x_refs[a], land_refs[a], s, send_sem, recv_sem, scatter)
                cp.wait_send()
                cp.wait_recv()

    outs = pl.pallas_call(
        body, name=name,
        out_shape=tuple(pltpu.HBM(x.shape, x.dtype) for x in xs) + tuple(pltpu.HBM(l.shape, l.dtype) for l in lands),
        in_specs=(_HBM,) * (2 * na) + (_SEM, _SEM, pl.BlockSpec(memory_space=pl.ANY)),
        out_specs=(_HBM,) * (2 * na),
        input_output_aliases={a: a for a in range(2 * na)},
        compiler_params=pltpu.CompilerParams(has_side_effects=_EFFECT),
    )(*xs, *lands, *started["sems"], after)
    return list(outs[na:])


def _cols_to_natural(g):
    return jnp.concatenate([g[k] for k in range(N_DEV)], axis=1)


def _cols_to_slots(w):
    ns = w.shape[1] // N_DEV
    return jnp.stack([w[:, k * ns:(k + 1) * ns] for k in range(N_DEV)])


def _vec8(rows, d):
    rows = [r.reshape(1, d).astype(F32) for r in rows]
    return jnp.concatenate(rows + [jnp.zeros((8 - len(rows), d), F32)], axis=0)


def _ffn_forward(x, vec, w_in_t, w_out):
    h, a, b, u = ffn_up(x, vec, w_in_t)
    xn, y = proj_out(u, x, vec, w_out, FFN_RES_WEIGHT, "ffn_down")
    return xn, (x, h, a, b, u, y)


def _ffn_backward(dxo, saved, vec, w_in_t, w_out):
    x, h, a, b, u, y = saved
    dy, dab, part_gate = ffn_down_bwd(dxo, y, vec, w_out, a, b)
    dx, part_norm = ffn_up_bwd(dab, w_in_t, x, dxo, vec)
    g_out = grad_slots(u, dy, "ffn_dw_out")
    g_in_t = grad_slots(dab, h, "ffn_dw_in")
    rows = jnp.concatenate([part_norm[0:3], part_gate[0:1]], axis=0)
    return dx, g_in_t, g_out, rows


_TRANSPOSED = ("ffn1_w_in", "ffn2_w_in", "attn_w_q")
_COL_NATURAL = ("conv_w_in", "w_kv", "attn_w_o")
_ROW_SHARDED = ("ffn1_w_out", "ffn2_w_out", "conv_w_out")
_BIG = _TRANSPOSED + _COL_NATURAL + _ROW_SHARDED


def weight_chunks():
    chunks = []
    for layer in range(DEPTH):
        first = [("ffn1_w_in", layer), ("ffn1_w_out", layer)]
        if layer == N_A_LAYERS:
            first = [("w_kv", layer)] + first
        mixer = [("conv_w_in", layer), ("conv_w_out", layer)] if layer < N_A_LAYERS else [("attn_w_q", layer), ("attn_w_o", layer)]
        rest = mixer + [("ffn2_w_in", layer), ("ffn2_w_out", layer)]
        chunks += [first, rest] if layer == 0 else [first + rest]
    return chunks


def stacked_index(name, layer):
    if name == "w_kv":
        return None
    return layer - N_A_LAYERS if name.startswith("attn") else layer


class ChunkComm:
    def __init__(self, shards):
        self.shards = shards
        self.chunks = weight_chunks()

    def _shard(self, name, layer):
        idx = stacked_index(name, layer)
        return self.shards[name][0 if idx is None else idx]

    def start_gathers(self, after):
        started = []
        for ci, chunk in enumerate(self.chunks):
            xs = [self._shard(n, l).astype(BF16) for n, l in chunk]
            started.append(comm_start(xs, False, after, f"gather_start_{ci}"))
            after = started[-1]["token"]
        return started

    def finish_gather(self, ci, started, after):
        lands = comm_wait(started, after, f"gather_wait_{ci}")
        W = {}
        for key, g in zip(self.chunks[ci], lands):
            W[key] = _cols_to_natural(g) if key[0] in _COL_NATURAL else g.reshape(-1, g.shape[2])
        return W

    def start_exchange(self, ci, slots, after):
        return comm_start([slots[key] for key in self.chunks[ci]], True, after, f"exchange_start_{ci}")

    def finish_exchange(self, ci, started, after):
        lands = comm_wait(started, after, f"exchange_wait_{ci}")
        return dict(zip(self.chunks[ci], lands))


def device_step(x, positions, target, mods, kvmods, small, comm, gathers):
    T, D = x.shape
    groups = DILATED_GROUPS
    lane = jnp.arange(LANES) % HEAD_DIM
    inv = ROPE_THETA ** (-jnp.arange(0, ROPE_DIM, 2, dtype=F32) / ROPE_DIM)
    lane_rows = _vec8([jnp.where(lane < ROPE_DIM, inv[lane % (ROPE_DIM // 2)], 0.0), lane < ROPE_DIM,
                       (lane >= ROPE_DIM // 2) & (lane < ROPE_DIM), lane < ROPE_DIM // 2], LANES)
    tabs = rope_tables(positions.reshape(T, 1), lane_rows)

    def after_token(v, token):
        return v if token is None else v + token[0, 0]

    def vec_of(layer, sub, token=None):
        return after_token(_vec8([small["norm_g"][layer, sub], mods[layer, 3 * sub], mods[layer, 3 * sub + 1],
                                  mods[layer, 3 * sub + 2]], D), token)

    saved = []
    kv_saved = None
    k_sh = v_sh = None
    qw = GROUP_WIDTH * len(groups)
    chunk_of = {key: ci for ci, chunk in enumerate(comm.chunks) for key in chunk}
    W = {}

    def need(key, after):
        if key not in W:
            ci = chunk_of[key]
            W.update(comm.finish_gather(ci, gathers[ci], after))
        return W[key]

    for layer in range(DEPTH):
        if layer == N_A_LAYERS:
            kv_vec = _vec8([small["kv_norm_g"], kvmods[0], kvmods[1]], D)
            h_kv, k_sh, v_sh = proj_rope_fwd(x, kv_vec, need(("w_kv", layer), x), tabs, qw, False, "kv_fwd")
            kv_saved = (x, h_kv, kv_vec)
        rec = {}
        v1 = vec_of(layer, 0)
        x, rec["ffn1"] = _ffn_forward(x, v1, need(("ffn1_w_in", layer), x), need(("ffn1_w_out", layer), x))
        v2 = vec_of(layer, 1)
        if layer < N_A_LAYERS:
            cw = _vec8(list(small["conv_w"][layer]), D)
            x_in = x
            x, h, bcu, cv, z, y = conv_fwd(x, v2, cw, need(("conv_w_in", layer), x), need(("conv_w_out", layer), x))
            rec["mix"] = (x_in, h, bcu, cv, z, y, cw)
        else:
            x_in = x
            h, q = proj_rope_fwd(x, v2, need(("attn_w_q", layer), x), tabs, qw, True, "q_fwd")
            os_, ls = [], []
            for g, (win, dil) in enumerate(groups):
                o, l = attn_core_fwd(q, k_sh, v_sh, g, win // dil, dil)
                os_.append(o)
                ls.append(l)
            x, mixed, y = attn_mix_out(os_, ls, x, v2, need(("attn_w_o", layer), x))
            rec["mix"] = (x_in, h, q, os_, ls, mixed, y)
        v3 = vec_of(layer, 2)
        x, rec["ffn2"] = _ffn_forward(x, v3, need(("ffn2_w_in", layer), x), need(("ffn2_w_out", layer), x))
        rec["vecs"] = (v1, v2, v3)
        saved.append(rec)

    dx, part_final, loss_tile = final_loss(x, _vec8([small["final_norm_g"]], D), target)
    loss = loss_tile[0, 0]

    conv_rows = [None] * N_A_LAYERS
    kv_rows = None
    mod_rows = [[None] * 3 for _ in range(DEPTH)]
    dkv_pairs = [{"k": [], "v": []} for _ in groups]
    slots = {}
    exchanges = []
    token = None

    def send_ready_chunks():
        nonlocal token
        for ci in reversed(range(len(comm.chunks))):
            if ci not in [e[0] for e in exchanges] and all(key in slots for key in comm.chunks[ci]):
                started = comm.start_exchange(ci, slots, token)
                exchanges.append((ci, started))
                token = started["token"]

    for layer in reversed(range(DEPTH)):
        rec = saved[layer]
        v1, v2, v3 = rec["vecs"]
        dx, slots[("ffn2_w_in", layer)], slots[("ffn2_w_out", layer)], mod_rows[layer][2] = _ffn_backward(
            dx, rec["ffn2"], after_token(v3, token), W[("ffn2_w_in", layer)], W[("ffn2_w_out", layer)])
        if layer < N_A_LAYERS:
            x_in, h, bcu, cv, z, y, cw = rec["mix"]
            dx, dy, dbcu, part, dcw = conv_bwd(dx, x_in, y, bcu, cv, v2, cw, W[("conv_w_in", layer)], W[("conv_w_out", layer)])
            slots[("conv_w_out", layer)] = grad_slots(z, dy, "conv_dw_out")
            slots[("conv_w_in", layer)] = grad_slots(h, dbcu, "conv_dw_in", col_slots=True)
            conv_rows[layer] = dcw[0:3]
            mod_rows[layer][1] = part[0:4]
        else:
            x_in, h, q, os_, ls, mixed, y = rec["mix"]
            outs = attn_mix_bwd(dx, y, v2, W[("attn_w_o", layer)], os_, ls)
            ng = len(groups)
            dy, dos, rrs, part_gate = outs[0], outs[1:1 + ng], outs[1 + ng:1 + 2 * ng], outs[1 + 2 * ng]
            slots[("attn_w_o", layer)] = grad_slots(mixed, dy, "attn_dw_o", col_slots=True)
            dqs = []
            for g, (win, dil) in enumerate(groups):
                dq, dkc, dkp, dvc, dvp = attn_core_bwd(q, k_sh, v_sh, dos[g], rrs[g], ls[g], g, win // dil, dil)
                dqs.append(dq)
                dkv_pairs[g]["k"].append((dkc, dkp))
                dkv_pairs[g]["v"].append((dvc, dvp))
            dx, dqr, part_norm = proj_rope_bwd(dqs, x_in, dx, v2, W[("attn_w_q", layer)], tabs, qw, True, "q_bwd")
            slots[("attn_w_q", layer)] = grad_slots(dqr, h, "attn_dw_q")
            mod_rows[layer][1] = jnp.concatenate([part_norm[0:3], part_gate[0:1]], axis=0)
        send_ready_chunks()
        dx, slots[("ffn1_w_in", layer)], slots[("ffn1_w_out", layer)], mod_rows[layer][0] = _ffn_backward(
            dx, rec["ffn1"], after_token(v1, token), W[("ffn1_w_in", layer)], W[("ffn1_w_out", layer)])
        if layer == N_A_LAYERS:
            x_kv, h_kv, kv_vec = kv_saved
            dparts = [dkv_combine(dkv_pairs[g]["k"], win // dil, dil, f"dk_combine_g{g}") for g, (win, dil) in enumerate(groups)]
            dparts += [dkv_combine(dkv_pairs[g]["v"], win // dil, dil, f"dv_combine_g{g}") for g, (win, dil) in enumerate(groups)]
            dx, dkvp, part_kv = proj_rope_bwd(dparts, x_kv, dx, kv_vec, W[("w_kv", layer)], tabs, qw, False, "kv_bwd")
            slots[("w_kv", layer)] = grad_slots(h_kv, dkvp, "kv_dw", col_slots=True)
            kv_rows = part_kv[0:3]
        send_ready_chunks()

    grads = {"conv_w": jnp.stack(conv_rows), "kv_rows": kv_rows, "exchanges": exchanges}
    grads["final_norm_g"] = part_final[0]
    rows = jnp.stack([jnp.stack(r) for r in mod_rows])
    grads["norm_g"] = rows[:, :, 0]
    grads["mods"] = rows[:, :, 1:4].reshape(DEPTH, N_MOD, D)
    return loss, dx, grads


def _flat2(a):
    return a.reshape(-1, a.shape[-1])


def _pad_rows(a, mult):
    r = a.shape[0]
    pad = (-r) % mult
    return a if pad == 0 else jnp.concatenate([a, jnp.zeros((pad,) + a.shape[1:], a.dtype)], axis=0)


def kernel(x, c, positions, norm_g, ada_w, ada_b, ffn1_w_in, ffn1_w_out, ffn2_w_in, ffn2_w_out, conv_w_in, conv_w, conv_w_out, kv_norm_g, kv_ada_w, kv_ada_b, w_kv, attn_w_q, attn_w_o, final_norm_g, loss_target, m_norm_g, m_ada_w, m_ada_b, m_ffn1_w_in, m_ffn1_w_out, m_ffn2_w_in, m_ffn2_w_out, m_conv_w_in, m_conv_w, m_conv_w_out, m_kv_norm_g, m_kv_ada_w, m_kv_ada_b, m_w_kv, m_attn_w_q, m_attn_w_o, m_final_norm_g, v_norm_g, v_ada_w, v_ada_b, v_ffn1_w_in, v_ffn1_w_out, v_ffn2_w_in, v_ffn2_w_out, v_conv_w_in, v_conv_w, v_conv_w_out, v_kv_norm_g, v_kv_ada_w, v_kv_ada_b, v_w_kv, v_attn_w_q, v_attn_w_o, v_final_norm_g):
    names = ("norm_g", "ada_w", "ada_b", "ffn1_w_in", "ffn1_w_out", "ffn2_w_in", "ffn2_w_out", "conv_w_in", "conv_w",
             "conv_w_out", "kv_norm_g", "kv_ada_w", "kv_ada_b", "w_kv", "attn_w_q", "attn_w_o", "final_norm_g")
    wts = dict(zip(names, (norm_g, ada_w, ada_b, ffn1_w_in, ffn1_w_out, ffn2_w_in, ffn2_w_out, conv_w_in, conv_w, conv_w_out,
                           kv_norm_g, kv_ada_w, kv_ada_b, w_kv, attn_w_q, attn_w_o, final_norm_g)))
    mom = dict(zip(names, (m_norm_g, m_ada_w, m_ada_b, m_ffn1_w_in, m_ffn1_w_out, m_ffn2_w_in, m_ffn2_w_out, m_conv_w_in,
                           m_conv_w, m_conv_w_out, m_kv_norm_g, m_kv_ada_w, m_kv_ada_b, m_w_kv, m_attn_w_q, m_attn_w_o,
                           m_final_norm_g)))
    var = dict(zip(names, (v_norm_g, v_ada_w, v_ada_b, v_ffn1_w_in, v_ffn1_w_out, v_ffn2_w_in, v_ffn2_w_out, v_conv_w_in,
                           v_conv_w, v_conv_w_out, v_kv_norm_g, v_kv_ada_w, v_kv_ada_b, v_w_kv, v_attn_w_q, v_attn_w_o,
                           v_final_norm_g)))
    T, D = x.shape[1], x.shape[2]
    me = _my_id()
    nmod = ada_w.shape[2]
    nkv = kv_ada_w.shape[1]

    def stacked(w, n):
        w = w if w.ndim == 3 else w[None]
        return jnp.swapaxes(w, 1, 2) if n in _TRANSPOSED else w

    comm = ChunkComm({n: stacked(wts[n], n) for n in _BIG})
    W = {}

    ds = norm_g.shape[2]
    small = jnp.concatenate([c.reshape(-1), norm_g.reshape(-1), conv_w.reshape(-1)]).astype(F32)
    n_small = small.shape[0]
    small = _pad_rows(small.reshape(-1, 1), 8 * LANES).reshape(-1, LANES)
    (small_all,) = all_gather([small], pltpu.VMEM, "gather_small")
    small_all = small_all.reshape(N_DEV, -1)[:, :n_small]
    c_all = small_all[:, :D]
    def full_rows(off, count):
        return jnp.stack([small_all[:, off + i * ds:off + (i + 1) * ds].reshape(D) for i in range(count)])

    W["norm_g"] = full_rows(D, DEPTH * 3).reshape(DEPTH, 3, D)
    W["conv_w"] = full_rows(D + DEPTH * 3 * ds, N_A_LAYERS * 3).reshape(N_A_LAYERS, 3, D)
    W["kv_norm_g"], W["final_norm_g"] = kv_norm_g, final_norm_g

    ada_b_mine = lax.dynamic_slice_in_dim(ada_b, me * nmod, nmod, axis=1).reshape(DEPTH, 1, nmod)
    kv_b_mine = lax.dynamic_slice_in_dim(kv_ada_b, me * nkv, nkv, axis=0).reshape(1, 1, nkv)
    mods_cols = mods_project(c_all, ada_w, ada_b_mine)
    kv_cols = mods_project(c_all, kv_ada_w.reshape(1, D, nkv), kv_b_mine)
    mcat = jnp.concatenate([mods_cols[l] for l in range(DEPTH)] + [kv_cols[0]], axis=1)
    wm = mcat.shape[1]
    if wm % LANES:
        mcat = jnp.concatenate([mcat, jnp.zeros((N_DEV, LANES - wm % LANES), F32)], axis=1)
    (mods_all,) = exchange_slots([mcat.reshape(N_DEV, 1, -1)], "exchange_mods")
    gathers = comm.start_gathers(mods_all)
    mods_all = mods_all.reshape(N_DEV, -1)
    mods = jnp.stack([mods_all[:, l * nmod:(l + 1) * nmod].reshape(N_MOD, D) for l in range(DEPTH)])
    kvmods = mods_all[:, DEPTH * nmod:DEPTH * nmod + nkv].reshape(2, D)

    loss_local, dx, grads = device_step(x[0], positions[0], loss_target[0], mods, kvmods, W, comm, gathers)
    loss = lax.psum(loss_local, MESH_AXES)

    dmods = grads["mods"].reshape(-1)
    dkvm = grads["kv_rows"][1:3].reshape(-1)
    vecs = jnp.concatenate([dmods, dkvm, grads["kv_rows"][0], grads["final_norm_g"], grads["norm_g"].reshape(-1),
                            grads["conv_w"].reshape(-1)])
    n_vec = vecs.shape[0]
    vecs = _pad_rows(vecs.reshape(-1, 1), 8 * LANES).reshape(-1, LANES)
    (vec_all,) = all_gather([vecs], pltpu.VMEM, "gather_vector_grads")
    vec_all = vec_all.reshape(N_DEV, -1)[:, :n_vec]
    nm_, nk_ = DEPTH * N_MOD * D, 2 * D
    dmods_all = vec_all[:, :nm_].reshape(N_DEV, DEPTH, N_MOD * D)
    dkvm_all = vec_all[:, nm_:nm_ + nk_]
    rest = vec_all[:, nm_ + nk_:]
    parts_kv_norm, parts_final = rest[:, :D].reshape(N_DEV, 1, D), rest[:, D:2 * D].reshape(N_DEV, 1, D)
    parts_norm = lax.dynamic_slice_in_dim(rest[:, 2 * D:2 * D + DEPTH * 3 * D].reshape(N_DEV, DEPTH * 3, D), me * ds, ds, axis=2)
    parts_conv = lax.dynamic_slice_in_dim(rest[:, 2 * D + DEPTH * 3 * D:].reshape(N_DEV, N_A_LAYERS * 3, D), me * ds, ds, axis=2)
    dm_cols = lax.dynamic_slice_in_dim(dmods_all, me * nmod, nmod, axis=2)
    dm_mine = jnp.stack([dm_cols[:, l] for l in range(DEPTH)])
    dkv_mine = lax.dynamic_slice_in_dim(dkvm_all, me * nkv, nkv, axis=1).reshape(1, N_DEV, nkv)
    g_ada_w = mods_weight_grad(c_all, dm_mine)
    g_kv_ada_w = mods_weight_grad(c_all, dkv_mine)[0]

    out_g, out_d, out_m, out_v = {}, {}, {}, {}

    def update(n, g, w, parts=False):
        shp = w.shape
        w2 = w.reshape(1, -1) if w.ndim == 1 else _flat2(w)
        g2 = g if parts else g.reshape(w2.shape)
        res = adam_update(g2, w2, mom[n].reshape(w2.shape), var[n].reshape(w2.shape), parts, "adam_" + n)
        out_g[n], out_d[n], out_m[n], out_v[n] = (r.reshape(shp) for r in res)

    moms = {n: stacked(mom[n], n) for n in _BIG}
    vars_ = {n: stacked(var[n], n) for n in _BIG}
    results = {}
    after = dx
    for ci, started in grads["exchanges"]:
        for (n, layer), parts in comm.finish_exchange(ci, started, after).items():
            idx = stacked_index(n, layer)
            results[n] = adam_layer(parts, comm.shards[n], moms[n], vars_[n], results.get(n), 0 if idx is None else idx,
                                    f"adam_{n}_{layer}")
            after = results[n][1]
    for n in _BIG:
        res = [jnp.swapaxes(r, 1, 2) if n in _TRANSPOSED else r for r in results[n]]
        out_g[n], out_d[n], out_m[n], out_v[n] = (r.reshape(wts[n].shape) for r in res)
    update("ada_w", g_ada_w, ada_w)
    update("kv_ada_w", g_kv_ada_w, kv_ada_w)
    update("ada_b", dmods_all, ada_b, True)
    update("kv_ada_b", dkvm_all.reshape(N_DEV, 1, nk_), kv_ada_b, True)
    update("kv_norm_g", parts_kv_norm, kv_norm_g, True)
    update("final_norm_g", parts_final, final_norm_g, True)
    update("norm_g", parts_norm, norm_g, True)
    update("conv_w", parts_conv, conv_w, True)

    return (loss, dx.reshape(x.shape), *[out_g[n] for n in names], *[out_d[n] for n in names],
            *[out_m[n] for n in names], *[out_v[n] for n in names])
```

```python
import functools

import jax
import jax.numpy as jnp
from jax import lax
from jax.experimental import pallas as pl
from jax.experimental.pallas import tpu as pltpu

F32, BF16 = jnp.float32, jnp.bfloat16

N_DEV = 8
MESH_AXES = ("x", "y", "c")
DEPTH = 4
N_A_LAYERS = 2
HEAD_DIM = 64
HEADS_PER_GROUP = 8
GROUP_WIDTH = HEAD_DIM * HEADS_PER_GROUP
DILATED_GROUPS = ((128, 1), (512, 4), (2048, 16))
ROPE_DIM = HEAD_DIM // 4
ROPE_THETA = 500000.0
NORM_EPS = 1e-5
FFN_RES_WEIGHT = 0.5
N_MOD = 9
ADAM_LR, ADAM_B1, ADAM_B2, ADAM_EPS, ADAM_WD, ADAM_STEP = 0.001, 0.9, 0.999, 1e-08, 0.01, 10

LANES = 128
TOKEN_TILE = 512
CONTRACT_TILE = 2048
MXU_WIDTH = 256
VMEM_LIMIT = 56 * 1024 * 1024
MESH = pl.DeviceIdType.MESH


def _cp(*sem):
    return pltpu.CompilerParams(dimension_semantics=sem, vmem_limit_bytes=VMEM_LIMIT)


def _pick(n, cap, mult=LANES):
    if n <= cap:
        return n
    best = None
    for t in range(mult, cap + 1, mult):
        if n % t == 0:
            best = t
    assert best is not None, (n, cap)
    return best


def _tok(tm, w):
    return pl.BlockSpec((tm, w), lambda i: (i, 0))


def _res(shape):
    nd = len(shape)
    return pl.BlockSpec(shape, lambda *_: (0,) * nd, pipeline_mode=pl.Buffered(1))


def _sds(shape, dt):
    return jax.ShapeDtypeStruct(shape, dt)


def _sigmoid(a):
    return 1.0 / (1.0 + jnp.exp(-a))


def _modnorm(x, g, sh, sc):
    r = lax.rsqrt(jnp.mean(x * x, axis=-1, keepdims=True) + NORM_EPS)
    return (x * r * g) * (1.0 + sc) + sh


def _dot(a, b):
    return jnp.dot(a, b, preferred_element_type=F32)


def _dot_nt(a, b):
    return lax.dot_general(a, b, (((1,), (1,)), ((), ())), preferred_element_type=F32)


def _dot_tn(a, b):
    return lax.dot_general(a, b, (((0,), (0,)), ((), ())), preferred_element_type=F32)


def _rows8(rows, d):
    pad = 8 - len(rows)
    return jnp.concatenate(list(rows) + [jnp.zeros((pad, d), F32)], axis=0)


def _acc_rows(ref, tile, first):
    @pl.when(first)
    def _():
        ref[...] = tile

    @pl.when(jnp.logical_not(first))
    def _():
        ref[...] += tile


def ffn_up(x, vec, w_in_t):
    T, D = x.shape
    F = w_in_t.shape[0] // 2
    tm, cw = min(TOKEN_TILE, T), _pick(F, MXU_WIDTH)

    def body(x_ref, vec_ref, w_ref, h_ref, ga_ref, gb_ref, u_ref):
        hb = _modnorm(x_ref[...], vec_ref[0:1], vec_ref[1:2], vec_ref[2:3]).astype(BF16)
        h_ref[...] = hb
        for c in range(F // cw):
            lo, hi = c * cw, (c + 1) * cw
            a = _dot_nt(hb, w_ref[lo:hi, :])
            b = _dot_nt(hb, w_ref[F + lo:F + hi, :])
            sg = _sigmoid(a)
            silu = a * sg
            ga_ref[:, lo:hi] = (b * (sg + silu * (1.0 - sg))).astype(BF16)
            gb_ref[:, lo:hi] = silu.astype(BF16)
            u_ref[:, lo:hi] = (silu * b).astype(BF16)

    return pl.pallas_call(
        body, grid=(T // tm,),
        in_specs=[_tok(tm, D), _res((8, D)), _res((2 * F, D))],
        out_specs=[_tok(tm, D), _tok(tm, F), _tok(tm, F), _tok(tm, F)],
        out_shape=[_sds((T, D), BF16), _sds((T, F), BF16), _sds((T, F), BF16), _sds((T, F), BF16)],
        compiler_params=_cp("arbitrary"), name="ffn_up")(x, vec, w_in_t)


def proj_out(u, x, vec, w_out, res_weight, name):
    T, D = x.shape
    K = u.shape[1]
    tm = min(TOKEN_TILE, T)

    def body(u_ref, x_ref, vec_ref, w_ref, xn_ref, y_ref):
        y = _dot(u_ref[...], w_ref[...])
        y_ref[...] = y.astype(BF16)
        xn_ref[...] = x_ref[...] + (res_weight * (1.0 + vec_ref[3:4])) * y

    return pl.pallas_call(
        body, grid=(T // tm,),
        in_specs=[_tok(tm, K), _tok(tm, D), _res((8, D)), _res((K, D))],
        out_specs=[_tok(tm, D), _tok(tm, D)],
        out_shape=[_sds((T, D), F32), _sds((T, D), BF16)],
        compiler_params=_cp("arbitrary"), name=name)(u, x, vec, w_out)


def ffn_down_bwd(dxo, y, vec, w_out, a, b):
    T, D = dxo.shape
    F = a.shape[1]
    tm, cw = min(TOKEN_TILE, T), _pick(F, MXU_WIDTH)

    def body(dxo_ref, y_ref, vec_ref, w_ref, a_ref, b_ref, dy_ref, dab_ref, part_ref):
        dxo_t = dxo_ref[...]
        dyb = (dxo_t * (FFN_RES_WEIGHT * (1.0 + vec_ref[3:4]))).astype(BF16)
        dy_ref[...] = dyb
        dgate = FFN_RES_WEIGHT * jnp.sum(dxo_t * y_ref[...].astype(F32), axis=0, keepdims=True)
        _acc_rows(part_ref, _rows8([dgate], D), pl.program_id(0) == 0)
        for c in range(F // cw):
            lo, hi = c * cw, (c + 1) * cw
            du = _dot_nt(dyb, w_ref[lo:hi, :])
            dab_ref[:, lo:hi] = (du * a_ref[:, lo:hi].astype(F32)).astype(BF16)
            dab_ref[:, F + lo:F + hi] = (du * b_ref[:, lo:hi].astype(F32)).astype(BF16)

    return pl.pallas_call(
        body, grid=(T // tm,),
        in_specs=[_tok(tm, D), _tok(tm, D), _res((8, D)), _res((F, D)), _tok(tm, F), _tok(tm, F)],
        out_specs=[_tok(tm, D), _tok(tm, 2 * F), pl.BlockSpec((8, D), lambda i: (0, 0))],
        out_shape=[_sds((T, D), BF16), _sds((T, 2 * F), BF16), _sds((8, D), F32)],
        compiler_params=_cp("arbitrary"), name="ffn_down_bwd")(dxo, y, vec, w_out, a, b)


def ffn_up_bwd(dab, w_in_t, x, dxo, vec):
    T, D = x.shape
    F2 = dab.shape[1]
    tm = min(TOKEN_TILE, T)

    def body(dab_ref, w_ref, x_ref, dxo_ref, vec_ref, dx_ref, part_ref):
        dh = _dot(dab_ref[...], w_ref[...])
        _, vjp = jax.vjp(_modnorm, x_ref[...], vec_ref[0:1], vec_ref[1:2], vec_ref[2:3])
        dx, dg, dsh, dsc = vjp(dh)
        dx_ref[...] = dxo_ref[...] + dx
        _acc_rows(part_ref, _rows8([dg, dsh, dsc], D), pl.program_id(0) == 0)

    return pl.pallas_call(
        body, grid=(T // tm,),
        in_specs=[_tok(tm, F2), _res((F2, D)), _tok(tm, D), _tok(tm, D), _res((8, D))],
        out_specs=[_tok(tm, D), pl.BlockSpec((8, D), lambda i: (0, 0))],
        out_shape=[_sds((T, D), F32), _sds((8, D), F32)],
        compiler_params=_cp("arbitrary"), name="ffn_up_bwd")(dab, w_in_t, x, dxo, vec)


def grad_slots(a, b, name, col_slots=False):
    T, M = a.shape
    N = b.shape[1]
    tk = min(CONTRACT_TILE, T)
    nk = T // tk
    tmm = _pick(M, 1408)
    if col_slots:
        ns = N // N_DEV
        sp = max(s for s in (1, 2, 4, 8) if ns * s <= 1536)
        tn = ns * sp
    else:
        tn = _pick(N, 1536)

    def body(a_ref, b_ref, o_ref, acc):
        k = pl.program_id(2)
        t = _dot_tn(a_ref[...], b_ref[...])

        @pl.when(k == 0)
        def _():
            acc[...] = t

        @pl.when(k > 0)
        def _():
            acc[...] += t

        @pl.when(k == nk - 1)
        def _():
            if col_slots:
                for s in range(sp):
                    o_ref[s] = acc[:, s * ns:(s + 1) * ns].astype(BF16)
            else:
                o_ref[...] = acc[...].astype(BF16)

    if col_slots:
        out_spec, out_shape = pl.BlockSpec((sp, tmm, ns), lambda i, j, k: (j, i, 0)), _sds((N_DEV, M, ns), BF16)
    else:
        out_spec, out_shape = pl.BlockSpec((tmm, tn), lambda i, j, k: (i, j)), _sds((M, N), BF16)
    out = pl.pallas_call(
        body, grid=(M // tmm, N // tn, nk),
        in_specs=[pl.BlockSpec((tk, tmm), lambda i, j, k: (k, i)), pl.BlockSpec((tk, tn), lambda i, j, k: (k, j))],
        out_specs=out_spec, out_shape=out_shape,
        scratch_shapes=[pltpu.VMEM((tmm, tn), F32)],
        compiler_params=_cp("arbitrary", "arbitrary", "arbitrary"), name=name)(a, b)
    return out if col_slots else out.reshape(N_DEV, M // N_DEV, N)


def conv_fwd(x, vec, cw, w_in, w_out):
    T, D = x.shape
    tm = min(TOKEN_TILE, T)

    def body(x_ref, vec_ref, cw_ref, wi_ref, wo_ref, xn_ref, h_ref, bcu_ref, cv_ref, z_ref, y_ref, vbuf):
        @pl.when(pl.program_id(0) == 0)
        def _():
            vbuf[0:8, :] = jnp.zeros((8, D), F32)

        x_t = x_ref[...]
        hb = _modnorm(x_t, vec_ref[0:1], vec_ref[1:2], vec_ref[2:3]).astype(BF16)
        h_ref[...] = hb
        bcu = _dot(hb, wi_ref[...])
        bcu_ref[...] = bcu.astype(BF16)
        bg, v = bcu[:, 0:D], bcu[:, D:2 * D] * bcu[:, 2 * D:3 * D]
        vbuf[8:8 + tm, :] = v
        conv = cw_ref[0:1] * vbuf[6:6 + tm, :] + cw_ref[1:2] * vbuf[7:7 + tm, :] + cw_ref[2:3] * v
        cv_ref[...] = conv.astype(BF16)
        zb = (bg * conv).astype(BF16)
        z_ref[...] = zb
        y = _dot(zb, wo_ref[...])
        y_ref[...] = y.astype(BF16)
        xn_ref[...] = x_t + (1.0 + vec_ref[3:4]) * y
        vbuf[0:8, :] = vbuf[tm:tm + 8, :]

    return pl.pallas_call(
        body, grid=(T // tm,),
        in_specs=[_tok(tm, D), _res((8, D)), _res((8, D)), _res((D, 3 * D)), _res((D, D))],
        out_specs=[_tok(tm, D), _tok(tm, D), _tok(tm, 3 * D), _tok(tm, D), _tok(tm, D), _tok(tm, D)],
        out_shape=[_sds((T, D), F32), _sds((T, D), BF16), _sds((T, 3 * D), BF16), _sds((T, D), BF16),
                   _sds((T, D), BF16), _sds((T, D), BF16)],
        scratch_shapes=[pltpu.VMEM((tm + 8, D), F32)],
        compiler_params=_cp("arbitrary"), name="conv_fwd")(x, vec, cw, w_in, w_out)


def conv_bwd(dxo, x, y, bcu, cv, vec, cw, w_in, w_out):
    T, D = x.shape
    tm = min(TOKEN_TILE, T)
    nt = T // tm

    def body(dxo_ref, x_ref, y_ref, bcu_ref, cv_ref, vec_ref, cw_ref, wi_ref, wo_ref,
             dx_ref, dy_ref, dbcu_ref, part_ref, dcw_ref, dcbuf):
        first = pl.program_id(0) == 0

        @pl.when(first)
        def _():
            dcbuf[tm:tm + 8, :] = jnp.zeros((8, D), F32)

        dxo_t = dxo_ref[...]
        dyb = (dxo_t * (1.0 + vec_ref[3:4])).astype(BF16)
        dy_ref[...] = dyb
        dgate = jnp.sum(dxo_t * y_ref[...].astype(F32), axis=0, keepdims=True)
        dz = _dot_nt(dyb, wo_ref[...])
        bcu_t = bcu_ref[...].astype(F32)
        bg, cg, ug = bcu_t[:, 0:D], bcu_t[:, D:2 * D], bcu_t[:, 2 * D:3 * D]
        dconv = dz * bg
        dbg = dz * cv_ref[...].astype(F32)
        dcbuf[0:tm, :] = dconv
        d1, d2 = dcbuf[1:tm + 1, :], dcbuf[2:tm + 2, :]
        dv = cw_ref[2:3] * dconv + cw_ref[1:2] * d1 + cw_ref[0:1] * d2
        v = cg * ug
        dcw = _rows8([jnp.sum(d2 * v, axis=0, keepdims=True), jnp.sum(d1 * v, axis=0, keepdims=True),
                      jnp.sum(dconv * v, axis=0, keepdims=True)], D)
        dbcu = jnp.concatenate([dbg, dv * ug, dv * cg], axis=1).astype(BF16)
        dbcu_ref[...] = dbcu
        dh = _dot_nt(dbcu, wi_ref[...])
        _, vjp = jax.vjp(_modnorm, x_ref[...], vec_ref[0:1], vec_ref[1:2], vec_ref[2:3])
        dx, dg, dsh, dsc = vjp(dh)
        dx_ref[...] = dxo_t + dx
        _acc_rows(part_ref, _rows8([dg, dsh, dsc, dgate], D), first)
        _acc_rows(dcw_ref, dcw, first)
        dcbuf[tm:tm + 8, :] = dcbuf[0:8, :]

    def rev(w):
        return pl.BlockSpec((tm, w), lambda i: (nt - 1 - i, 0))

    return pl.pallas_call(
        body, grid=(nt,),
        in_specs=[rev(D), rev(D), rev(D), rev(3 * D), rev(D), _res((8, D)), _res((8, D)), _res((D, 3 * D)), _res((D, D))],
        out_specs=[rev(D), rev(D), rev(3 * D), pl.BlockSpec((8, D), lambda i: (0, 0)), pl.BlockSpec((8, D), lambda i: (0, 0))],
        out_shape=[_sds((T, D), F32), _sds((T, D), BF16), _sds((T, 3 * D), BF16), _sds((8, D), F32), _sds((8, D), F32)],
        scratch_shapes=[pltpu.VMEM((tm + 8, D), F32)],
        compiler_params=_cp("arbitrary"), name="conv_bwd")(dxo, x, y, bcu, cv, vec, cw, w_in, w_out)


def rope_tables(pos, lane_rows):
    T = pos.shape[0]
    tm = min(TOKEN_TILE, T)

    def body(p_ref, lr_ref, c_ref, sp_ref, sm_ref):
        ang = p_ref[...].astype(F32) * lr_ref[0:1]
        cs, sn = jnp.cos(ang), jnp.sin(ang)
        c_ref[...] = jnp.where(lr_ref[1:2] > 0.5, cs, 1.0)
        sp_ref[...] = jnp.where(lr_ref[2:3] > 0.5, sn, 0.0)
        sm_ref[...] = jnp.where(lr_ref[3:4] > 0.5, -sn, 0.0)

    return pl.pallas_call(
        body, grid=(T // tm,),
        in_specs=[_tok(tm, 1), _res((8, LANES))],
        out_specs=[_tok(tm, LANES)] * 3,
        out_shape=[_sds((T, LANES), F32)] * 3,
        compiler_params=_cp("arbitrary"), name="rope_tables")(pos, lane_rows)


def _rope(t, c, sp, sm):
    w = t.shape[1]
    reps = w // LANES
    cf, spf, smf = jnp.tile(c, (1, reps)), jnp.tile(sp, (1, reps)), jnp.tile(sm, (1, reps))
    half = ROPE_DIM // 2
    return t * cf + pltpu.roll(t, half, axis=1) * spf + pltpu.roll(t, w - half, axis=1) * smf


def _rope_t(d, c, sp, sm):
    w = d.shape[1]
    reps = w // LANES
    cf, spf, smf = jnp.tile(c, (1, reps)), jnp.tile(sp, (1, reps)), jnp.tile(sm, (1, reps))
    half = ROPE_DIM // 2
    return d * cf + pltpu.roll(d * spf, w - half, axis=1) + pltpu.roll(d * smf, half, axis=1)


def proj_rope_fwd(x, vec, w, tabs, n_rope, transposed, name):
    T, D = x.shape
    N = w.shape[0] if transposed else w.shape[1]
    tm = min(TOKEN_TILE, T)

    def body(x_ref, vec_ref, w_ref, c_ref, sp_ref, sm_ref, h_ref, pr_ref, *rest_ref):
        hb = _modnorm(x_ref[...], vec_ref[0:1], vec_ref[1:2], vec_ref[2:3]).astype(BF16)
        h_ref[...] = hb
        p = _dot_nt(hb, w_ref[...]) if transposed else _dot(hb, w_ref[...])
        pr_ref[...] = _rope(p[:, 0:n_rope], c_ref[...], sp_ref[...], sm_ref[...]).astype(BF16)
        if rest_ref:
            rest_ref[0][...] = p[:, n_rope:N].astype(BF16)

    widths = [n_rope] + ([N - n_rope] if n_rope < N else [])
    return pl.pallas_call(
        body, grid=(T // tm,),
        in_specs=[_tok(tm, D), _res((8, D)), _res(w.shape)] + [_tok(tm, LANES)] * 3,
        out_specs=[_tok(tm, D)] + [_tok(tm, wd) for wd in widths],
        out_shape=[_sds((T, D), BF16)] + [_sds((T, wd), BF16) for wd in widths],
        compiler_params=_cp("arbitrary"), name=name)(x, vec, w, *tabs)


def proj_rope_bwd(dparts, x, dxo, vec, w, tabs, n_rope, transposed, name):
    T, D = x.shape
    N = w.shape[0] if transposed else w.shape[1]
    tm = min(TOKEN_TILE, T)
    npart = len(dparts)

    def body(*refs):
        d_refs = refs[:npart]
        x_ref, dxo_ref, vec_ref, w_ref, c_ref, sp_ref, sm_ref, dx_ref, dp_ref, part_ref = refs[npart:]
        d = jnp.concatenate([r[...].astype(F32) for r in d_refs], axis=1)
        dr = _rope_t(d[:, 0:n_rope], c_ref[...], sp_ref[...], sm_ref[...])
        if n_rope < N:
            dr = jnp.concatenate([dr, d[:, n_rope:N]], axis=1)
        dpb = dr.astype(BF16)
        dp_ref[...] = dpb
        dh = _dot(dpb, w_ref[...]) if transposed else _dot_nt(dpb, w_ref[...])
        _, vjp = jax.vjp(_modnorm, x_ref[...], vec_ref[0:1], vec_ref[1:2], vec_ref[2:3])
        dx, dg, dsh, dsc = vjp(dh)
        dx_ref[...] = dxo_ref[...] + dx
        _acc_rows(part_ref, _rows8([dg, dsh, dsc], D), pl.program_id(0) == 0)

    return pl.pallas_call(
        body, grid=(T // tm,),
        in_specs=[_tok(tm, p.shape[1]) for p in dparts] + [_tok(tm, D), _tok(tm, D), _res((8, D)), _res(w.shape)]
        + [_tok(tm, LANES)] * 3,
        out_specs=[_tok(tm, D), _tok(tm, N), pl.BlockSpec((8, D), lambda i: (0, 0))],
        out_shape=[_sds((T, D), F32), _sds((T, N), BF16), _sds((8, D), F32)],
        compiler_params=_cp("arbitrary"), name=name)(*dparts, x, dxo, vec, w, *tabs)


def _valid_mask(n, i):
    qi = lax.broadcasted_iota(jnp.int32, (n, 2 * n), 0)
    kj = lax.broadcasted_iota(jnp.int32, (n, 2 * n), 1)
    dist = n + qi - kj
    return (dist >= 0) & (dist <= n) & ((kj >= n) | (i > 0))


def attn_core_fwd(q, k, v, g, n, d):
    T, QW = q.shape
    GW = GROUP_WIDTH
    ng = QW // GW
    M = T // d
    scale = HEAD_DIM ** -0.5

    def body(q_ref, kp_ref, kc_ref, vp_ref, vc_ref, o_ref, l_ref):
        valid = _valid_mask(n, pl.program_id(1))
        qv = q_ref[...]
        kk = jnp.concatenate([kp_ref[...], kc_ref[...]], axis=0)
        vv = jnp.concatenate([vp_ref[...], vc_ref[...]], axis=0)
        for h in range(HEADS_PER_GROUP):
            hs = slice(HEAD_DIM * h, HEAD_DIM * (h + 1))
            s = jnp.where(valid, _dot_nt(qv[:, hs], kk[:, hs]) * scale, -1e30)
            m = jnp.max(s, axis=1, keepdims=True)
            p = jnp.exp(s - m)
            den = jnp.sum(p, axis=1, keepdims=True)
            o_ref[:, hs] = _dot((p / den).astype(BF16), vv[:, hs])
            l_ref[:, hs] = jnp.broadcast_to(m + jnp.log(den), (n, HEAD_DIM))

    cur = pl.BlockSpec((n, GW), lambda r, i: (i, r * ng + g))
    prv = pl.BlockSpec((n, GW), lambda r, i: (jnp.maximum(i - 1, 0), r * ng + g))
    out = pl.BlockSpec((n, GW), lambda r, i: (i, r))
    qv, kv, vv = q.reshape(M, d * QW), k.reshape(M, d * QW), v.reshape(M, d * QW)
    o, l = pl.pallas_call(
        body, grid=(d, M // n),
        in_specs=[cur, prv, cur, prv, cur], out_specs=[out, out],
        out_shape=[_sds((M, d * GW), F32), _sds((M, d * GW), F32)],
        compiler_params=_cp("arbitrary", "arbitrary"), name=f"attn_fwd_g{g}")(qv, kv, kv, vv, vv)
    return o.reshape(T, GW), l.reshape(T, GW)


def attn_core_bwd(q, k, v, do, rr, lse, g, n, d):
    T, QW = q.shape
    GW = GROUP_WIDTH
    ng = QW // GW
    M = T // d
    scale = HEAD_DIM ** -0.5

    def body(q_ref, kp_ref, kc_ref, vp_ref, vc_ref, do_ref, r_ref, l_ref, dq_ref, dkc_ref, dkp_ref, dvc_ref, dvp_ref):
        valid = _valid_mask(n, pl.program_id(1))
        qv, dov = q_ref[...], do_ref[...]
        kk = jnp.concatenate([kp_ref[...], kc_ref[...]], axis=0)
        vv = jnp.concatenate([vp_ref[...], vc_ref[...]], axis=0)
        for h in range(HEADS_PER_GROUP):
            hs = slice(HEAD_DIM * h, HEAD_DIM * (h + 1))
            s = jnp.where(valid, _dot_nt(qv[:, hs], kk[:, hs]) * scale, -1e30)
            p = jnp.exp(s - l_ref[:, HEAD_DIM * h:HEAD_DIM * h + 1])
            dp = _dot_nt(dov[:, hs], vv[:, hs])
            delta = jnp.sum(r_ref[:, hs], axis=1, keepdims=True)
            ds = (p * (dp - delta) * scale).astype(BF16)
            dq_ref[:, hs] = _dot(ds, kk[:, hs]).astype(BF16)
            dk = _dot_tn(ds, qv[:, hs]).astype(BF16)
            dv = _dot_tn(p.astype(BF16), dov[:, hs]).astype(BF16)
            dkp_ref[:, hs], dkc_ref[:, hs] = dk[0:n], dk[n:2 * n]
            dvp_ref[:, hs], dvc_ref[:, hs] = dv[0:n], dv[n:2 * n]

    cur = pl.BlockSpec((n, GW), lambda r, i: (i, r * ng + g))
    prv = pl.BlockSpec((n, GW), lambda r, i: (jnp.maximum(i - 1, 0), r * ng + g))
    blk = pl.BlockSpec((n, GW), lambda r, i: (i, r))
    qv, kv, vv = q.reshape(M, d * QW), k.reshape(M, d * QW), v.reshape(M, d * QW)
    outs = pl.pallas_call(
        body, grid=(d, M // n),
        in_specs=[cur, prv, cur, prv, cur, blk, blk, blk], out_specs=[blk] * 5,
        out_shape=[_sds((M, d * GW), BF16)] * 5,
        compiler_params=_cp("arbitrary", "arbitrary"), name=f"attn_bwd_g{g}")(
            qv, kv, kv, vv, vv, do.reshape(M, d * GW), rr.reshape(M, d * GW), lse.reshape(M, d * GW))
    return [o.reshape(T, GW) for o in outs]


def dkv_combine(cur_prev, n, d, name):
    T, GW = cur_prev[0][0].shape
    M = T // d
    nb = M // n
    flat = [a.reshape(M, d * GW) for pair in cur_prev for a in pair]

    def body(*refs):
        o_ref = refs[-1]
        last = pl.program_id(1) == nb - 1
        acc = jnp.zeros((n, GW), F32)
        for t in range(0, len(refs) - 1, 2):
            acc = acc + refs[t][...].astype(F32) + jnp.where(last, 0.0, refs[t + 1][...].astype(F32))
        o_ref[...] = acc.astype(BF16)

    cur = pl.BlockSpec((n, GW), lambda r, i: (i, r))
    nxt = pl.BlockSpec((n, GW), lambda r, i: (jnp.minimum(i + 1, nb - 1), r))
    out = pl.pallas_call(
        body, grid=(d, nb), in_specs=[cur, nxt] * len(cur_prev), out_specs=cur,
        out_shape=_sds((M, d * GW), BF16),
        compiler_params=_cp("arbitrary", "arbitrary"), name=name)(*flat)
    return out.reshape(T, GW)


def _group_weights(ls):
    mx = functools.reduce(jnp.maximum, ls)
    es = [jnp.exp(l - mx) for l in ls]
    tot = functools.reduce(lambda a, b: a + b, es)
    return [e / tot for e in es]


def attn_mix_out(os_, ls, x, vec, w_o):
    T, D = x.shape
    GW = GROUP_WIDTH
    tm = min(TOKEN_TILE, T)
    ng = len(os_)

    def body(*refs):
        o_refs, l_refs = refs[:ng], refs[ng:2 * ng]
        x_ref, vec_ref, w_ref, xn_ref, mix_ref, y_ref = refs[2 * ng:]
        ws = _group_weights([r[...] for r in l_refs])
        mixed = functools.reduce(lambda a, b: a + b, [w * r[...] for w, r in zip(ws, o_refs)])
        mb = mixed.astype(BF16)
        mix_ref[...] = mb
        y = _dot(mb, w_ref[...])
        y_ref[...] = y.astype(BF16)
        xn_ref[...] = x_ref[...] + (1.0 + vec_ref[3:4]) * y

    return pl.pallas_call(
        body, grid=(T // tm,),
        in_specs=[_tok(tm, GW)] * (2 * ng) + [_tok(tm, D), _res((8, D)), _res((GW, D))],
        out_specs=[_tok(tm, D), _tok(tm, GW), _tok(tm, D)],
        out_shape=[_sds((T, D), F32), _sds((T, GW), BF16), _sds((T, D), BF16)],
        compiler_params=_cp("arbitrary"), name="attn_mix_out")(*os_, *ls, x, vec, w_o)


def attn_mix_bwd(dxo, y, vec, w_o, os_, ls):
    T, D = dxo.shape
    GW = GROUP_WIDTH
    tm = min(TOKEN_TILE, T)
    ng = len(os_)

    def body(*refs):
        dxo_ref, y_ref, vec_ref, w_ref = refs[:4]
        o_refs, l_refs = refs[4:4 + ng], refs[4 + ng:4 + 2 * ng]
        dy_ref = refs[4 + 2 * ng]
        do_refs = refs[5 + 2 * ng:5 + 3 * ng]
        r_refs = refs[5 + 3 * ng:5 + 4 * ng]
        part_ref = refs[5 + 4 * ng]
        dxo_t = dxo_ref[...]
        dyb = (dxo_t * (1.0 + vec_ref[3:4])).astype(BF16)
        dy_ref[...] = dyb
        dgate = jnp.sum(dxo_t * y_ref[...].astype(F32), axis=0, keepdims=True)
        _acc_rows(part_ref, _rows8([dgate], D), pl.program_id(0) == 0)
        dmix = _dot_nt(dyb, w_ref[...])
        ws = _group_weights([r[...] for r in l_refs])
        ov = [r[...] for r in o_refs]
        mixed = functools.reduce(lambda a, b: a + b, [w * o for w, o in zip(ws, ov)])
        for gi in range(ng):
            do = ws[gi] * dmix
            do_refs[gi][...] = do.astype(BF16)
            r_refs[gi][...] = do * mixed

    return pl.pallas_call(
        body, grid=(T // tm,),
        in_specs=[_tok(tm, D), _tok(tm, D), _res((8, D)), _res((GW, D))] + [_tok(tm, GW)] * (2 * ng),
        out_specs=[_tok(tm, D)] + [_tok(tm, GW)] * (2 * ng) + [pl.BlockSpec((8, D), lambda i: (0, 0))],
        out_shape=[_sds((T, D), BF16)] + [_sds((T, GW), BF16)] * ng + [_sds((T, GW), F32)] * ng + [_sds((8, D), F32)],
        compiler_params=_cp("arbitrary"), name="attn_mix_bwd")(dxo, y, vec, w_o, *os_, *ls)


def final_loss(x, gvec, target):
    T, D = x.shape
    tm = min(TOKEN_TILE, T)

    def norm(xv, g):
        return xv * lax.rsqrt(jnp.mean(xv * xv, axis=-1, keepdims=True) + NORM_EPS) * g

    def body(x_ref, g_ref, t_ref, dx_ref, part_ref, loss_ref):
        first = pl.program_id(0) == 0
        yv, vjp = jax.vjp(norm, x_ref[...], g_ref[0:1])
        err = yv - t_ref[...]
        dx, dg = vjp(err * (1.0 / D))
        dx_ref[...] = dx
        _acc_rows(part_ref, _rows8([dg], D), first)
        tile_loss = 0.5 * jnp.sum(jnp.sum(err * err, axis=1, keepdims=True) * (1.0 / D), axis=0, keepdims=True)
        _acc_rows(loss_ref, jnp.broadcast_to(tile_loss, (8, LANES)), first)

    return pl.pallas_call(
        body, grid=(T // tm,),
        in_specs=[_tok(tm, D), _res((8, D)), _tok(tm, D)],
        out_specs=[_tok(tm, D), pl.BlockSpec((8, D), lambda i: (0, 0)), pl.BlockSpec((8, LANES), lambda i: (0, 0))],
        out_shape=[_sds((T, D), F32), _sds((8, D), F32), _sds((8, LANES), F32)],
        compiler_params=_cp("arbitrary"), name="final_loss")(x, gvec, target)


def mods_project(c_all, w, b):
    B, D = c_all.shape
    L, _, N = w.shape

    def body(c_ref, w_ref, b_ref, o_ref):
        cv = c_ref[...]
        cond = cv * _sigmoid(cv)
        o_ref[0] = jnp.dot(cond, w_ref[0], preferred_element_type=F32, precision=lax.Precision.HIGHEST) + b_ref[0]

    return pl.pallas_call(
        body, grid=(L,),
        in_specs=[pl.BlockSpec((B, D), lambda l: (0, 0)), pl.BlockSpec((1, D, N), lambda l: (l, 0, 0)),
                  pl.BlockSpec((1, 1, N), lambda l: (l, 0, 0))],
        out_specs=pl.BlockSpec((1, B, N), lambda l: (l, 0, 0)),
        out_shape=_sds((L, B, N), F32),
        compiler_params=_cp("arbitrary"), name="mods_project")(c_all, w, b)


def mods_weight_grad(c_all, dm):
    B, D = c_all.shape
    L, _, N = dm.shape

    def body(c_ref, d_ref, o_ref):
        cv = c_ref[...]
        cond = cv * _sigmoid(cv)
        o_ref[0] = lax.dot_general(cond, d_ref[0], (((0,), (0,)), ((), ())), preferred_element_type=F32,
                                   precision=lax.Precision.HIGHEST)

    return pl.pallas_call(
        body, grid=(L,),
        in_specs=[pl.BlockSpec((B, D), lambda l: (0, 0)), pl.BlockSpec((1, B, N), lambda l: (l, 0, 0))],
        out_specs=pl.BlockSpec((1, D, N), lambda l: (l, 0, 0)),
        out_shape=_sds((L, D, N), F32),
        compiler_params=_cp("arbitrary"), name="mods_weight_grad")(c_all, dm)


def _adam_math(g, w, m, v):
    m2 = ADAM_B1 * m + (1.0 - ADAM_B1) * g
    v2 = ADAM_B2 * v + (1.0 - ADAM_B2) * (g * g)
    m_hat = m2 / (1.0 - ADAM_B1 ** ADAM_STEP)
    v_hat = v2 / (1.0 - ADAM_B2 ** ADAM_STEP)
    delta = -ADAM_LR * (m_hat / (jnp.sqrt(v_hat) + ADAM_EPS) + ADAM_WD * w)
    return delta, m2, v2


def adam_update(g, w, m, v, parts, name):
    R, C = w.shape
    tr = _pick(R, 256, 8)

    def body(g_ref, w_ref, m_ref, v_ref, go_ref, d_ref, mo_ref, vo_ref):
        if parts:
            gv = g_ref[0].astype(F32)
            for s in range(1, N_DEV):
                gv = gv + g_ref[s].astype(F32)
        else:
            gv = g_ref[...]
        go_ref[...] = gv
        d_ref[...], mo_ref[...], vo_ref[...] = _adam_math(gv, w_ref[...], m_ref[...], v_ref[...])

    gspec = pl.BlockSpec((N_DEV, tr, C), lambda i: (0, i, 0)) if parts else _tok(tr, C)
    return pl.pallas_call(
        body, grid=(R // tr,),
        in_specs=[gspec, _tok(tr, C), _tok(tr, C), _tok(tr, C)],
        out_specs=[_tok(tr, C)] * 4, out_shape=[_sds((R, C), F32)] * 4,
        compiler_params=_cp("arbitrary"), name=name)(g, w, m, v)


def adam_layer(parts, w, m, v, prev, layer, after, name):
    L, R, C = w.shape
    tr = _pick(R, 256, 8)
    prev = (list(prev) if prev is not None else []) + [after]

    def body(p_ref, w_ref, m_ref, v_ref, *rest):
        go_ref, d_ref, mo_ref, vo_ref = rest[-4:]
        gv = p_ref[0].astype(F32)
        for s in range(1, N_DEV):
            gv = gv + p_ref[s].astype(F32)
        go_ref[...] = gv
        d_ref[...], mo_ref[...], vo_ref[...] = _adam_math(gv, w_ref[...], m_ref[...], v_ref[...])

    lay = pl.BlockSpec((None, tr, C), lambda i: (layer, i, 0))
    return pl.pallas_call(
        body, grid=(R // tr,),
        in_specs=[pl.BlockSpec((N_DEV, tr, C), lambda i: (0, i, 0)), lay, lay, lay] + [pl.BlockSpec(memory_space=pl.ANY)] * len(prev),
        out_specs=[lay] * 4, out_shape=[_sds((L, R, C), F32)] * 4,
        input_output_aliases={4 + k: k for k in range(len(prev) - 1)},
        compiler_params=_cp("arbitrary"), name=name)(parts, w, m, v, *prev)


def _my_id():
    return 4 * lax.axis_index("x") + 2 * lax.axis_index("y") + lax.axis_index("c")


def _peer(s):
    x, y, c = lax.axis_index("x"), lax.axis_index("y"), lax.axis_index("c")
    px = (1 - x) if s & 4 else x
    py = (1 - y) if s & 2 else y
    pc = (1 - c) if s & 1 else c
    return (px, py, pc), 4 * px + 2 * py + pc


def all_gather(xs, space, name):
    na = len(xs)

    def body(*refs):
        x_refs, o_refs = refs[:na], refs[na:2 * na]
        send_sems, recv_sems, local_sems = refs[2 * na:]
        me = _my_id()
        locals_, sends = [], []
        for a in range(na):
            cp = pltpu.make_async_copy(x_refs[a], o_refs[a].at[me], local_sems.at[a])
            cp.start()
            locals_.append(cp)
        for s in range(1, N_DEV):
            peer, _ = _peer(s)
            for a in range(na):
                cp = pltpu.make_async_remote_copy(
                    src_ref=x_refs[a], dst_ref=o_refs[a].at[me], send_sem=send_sems.at[a, s - 1],
                    recv_sem=recv_sems.at[a, s - 1], device_id=peer, device_id_type=MESH)
                cp.start()
                sends.append(cp)
        for s in range(1, N_DEV):
            peer, pid = _peer(s)
            for a in range(na):
                pltpu.make_async_remote_copy(
                    src_ref=x_refs[a], dst_ref=o_refs[a].at[pid], send_sem=send_sems.at[a, s - 1],
                    recv_sem=recv_sems.at[a, s - 1], device_id=peer, device_id_type=MESH).wait_recv()
        for cp in sends:
            cp.wait_send()
        for cp in locals_:
            cp.wait()

    spec = pl.BlockSpec(memory_space=space)
    return pl.pallas_call(
        body, in_specs=[spec] * na, out_specs=[spec] * na,
        out_shape=[_sds((N_DEV,) + x.shape, x.dtype) for x in xs],
        scratch_shapes=[pltpu.SemaphoreType.DMA((na, N_DEV - 1)), pltpu.SemaphoreType.DMA((na, N_DEV - 1)),
                        pltpu.SemaphoreType.DMA((na,))],
        compiler_params=pltpu.CompilerParams(vmem_limit_bytes=VMEM_LIMIT), name=name)(*xs)


def exchange_slots(xs, name):
    na = len(xs)

    def body(*refs):
        x_refs, o_refs = refs[:na], refs[na:2 * na]
        send_sems, recv_sems, local_sems = refs[2 * na:]
        me = _my_id()
        locals_, sends = [], []
        for a in range(na):
            cp = pltpu.make_async_copy(x_refs[a].at[me], o_refs[a].at[me], local_sems.at[a])
            cp.start()
            locals_.append(cp)
        for s in range(1, N_DEV):
            peer, pid = _peer(s)
            for a in range(na):
                cp = pltpu.make_async_remote_copy(
                    src_ref=x_refs[a].at[pid], dst_ref=o_refs[a].at[me], send_sem=send_sems.at[a, s - 1],
                    recv_sem=recv_sems.at[a, s - 1], device_id=peer, device_id_type=MESH)
                cp.start()
                sends.append(cp)
        for s in range(1, N_DEV):
            peer, pid = _peer(s)
            for a in range(na):
                pltpu.make_async_remote_copy(
                    src_ref=x_refs[a].at[pid], dst_ref=o_refs[a].at[pid], send_sem=send_sems.at[a, s - 1],
                    recv_sem=recv_sems.at[a, s - 1], device_id=peer, device_id_type=MESH).wait_recv()
        for cp in sends:
            cp.wait_send()
        for cp in locals_:
            cp.wait()

    spec = pl.BlockSpec(memory_space=pl.ANY)
    return pl.pallas_call(
        body, in_specs=[spec] * na, out_specs=[spec] * na,
        out_shape=[_sds(x.shape, x.dtype) for x in xs],
        scratch_shapes=[pltpu.SemaphoreType.DMA((na, N_DEV - 1)), pltpu.SemaphoreType.DMA((na, N_DEV - 1)),
                        pltpu.SemaphoreType.DMA((na,))],
        compiler_params=pltpu.CompilerParams(vmem_limit_bytes=VMEM_LIMIT), name=name)(*xs)


_HBM = pl.BlockSpec(memory_space=pltpu.HBM)
_SEM = pl.BlockSpec(memory_space=pltpu.SEMAPHORE)
_EFFECT = pltpu.SideEffectType.DATAFLOW_SIDE_EFFECTING


def _split_copy(x_ref, land_ref, s, send_sem, recv_sem, scatter):
    peer, pid = _peer(s)
    src = x_ref.at[pid] if scatter else x_ref
    return pltpu.make_async_remote_copy(src_ref=src, dst_ref=land_ref.at[_my_id()], send_sem=send_sem, recv_sem=recv_sem,
                                        device_id=peer, device_id_type=MESH)


def comm_start(xs, scatter, after, name):
    na = len(xs)
    extra = [] if after is None else [after]
    me = _my_id()
    lands = []
    for x in xs:
        shape = x.shape if scatter else (N_DEV,) + x.shape
        own = lax.dynamic_slice_in_dim(x, me, 1, 0) if scatter else x[None]
        lands.append(lax.dynamic_update_slice(lax.empty(shape, x.dtype), own, (me,) + (0,) * (len(shape) - 1)))

    def body(*refs):
        x_refs, land_refs = refs[:na], refs[na:2 * na]
        send_sem, recv_sem = refs[2 * na + len(extra)], refs[2 * na + len(extra) + 1]
        token = refs[-1]
        for s in range(1, N_DEV):
            for a in range(na):
                _split_copy(x_refs[a], land_refs[a], s, send_sem, recv_sem, scatter).start()
        token[...] = jnp.zeros_like(token)

    outs = pl.pallas_call(
        body, name=name,
        out_shape=(pltpu.SemaphoreType.DMA(()), pltpu.SemaphoreType.DMA(()))
        + tuple(pltpu.HBM(x.shape, x.dtype) for x in xs) + tuple(pltpu.HBM(l.shape, l.dtype) for l in lands)
        + (_sds((8, LANES), F32),),
        in_specs=(_HBM,) * (2 * na) + (pl.BlockSpec(memory_space=pl.ANY),) * len(extra),
        out_specs=(_SEM, _SEM) + (_HBM,) * (2 * na) + (pl.BlockSpec(memory_space=pltpu.VMEM),),
        input_output_aliases={a: 2 + a for a in range(2 * na)},
        compiler_params=pltpu.CompilerParams(has_side_effects=_EFFECT),
    )(*[pltpu.with_memory_space_constraint(x, pltpu.HBM) for x in xs],
      *[pltpu.with_memory_space_constraint(l, pltpu.HBM) for l in lands], *extra)
    return dict(sems=outs[0:2], xs=outs[2:2 + na], lands=outs[2 + na:2 + 2 * na], token=outs[-1], scatter=scatter)


def comm_wait(started, after, name):
    xs, lands = started["xs"], started["lands"]
    scatter = started["scatter"]
    na = len(xs)

    def body(*refs):
        x_refs, land_refs = refs[:na], refs[na:2 * na]
        send_sem, recv_sem = refs[2 * na], refs[2 * na + 1]
        for s in range(1, N_DEV):
            for a in range(na):
                cp = _split_copy(x_refs[a], land_refs[a], s, send_sem, recv_sem, scatter)
                cp.wait_send()
                cp.wait_recv()

    outs = pl.pallas_call(
        body, name=name,
        out_shape=tuple(pltpu.HBM(x.shape, x.dtype) for x in xs) + tuple(pltpu.HBM(l.shape, l.dtype) for l in lands),
        in_specs=(_HBM,) * (2 * na) + (_SEM, _SEM, pl.BlockSpec(memory_space=pl.ANY)),
        out_specs=(_HBM,) * (2 * na),
        input_output_aliases={a: a for a in range(2 * na)},
        compiler_params=pltpu.CompilerParams(has_side_effects=_EFFECT),
    )(*xs, *lands, *started["sems"], after)
    return list(outs[na:])


def _cols_to_natural(g):
    return jnp.concatenate([g[k] for k in range(N_DEV)], axis=1)


def _cols_to_slots(w):
    ns = w.shape[1] // N_DEV
    return jnp.stack([w[:, k * ns:(k + 1) * ns] for k in range(N_DEV)])


def _vec8(rows, d):
    rows = [r.reshape(1, d).astype(F32) for r in rows]
    return jnp.concatenate(rows + [jnp.zeros((8 - len(rows), d), F32)], axis=0)


def _ffn_forward(x, vec, w_in_t, w_out):
    h, a, b, u = ffn_up(x, vec, w_in_t)
    xn, y = proj_out(u, x, vec, w_out, FFN_RES_WEIGHT, "ffn_down")
    return xn, (x, h, a, b, u, y)


def _ffn_backward(dxo, saved, vec, w_in_t, w_out):
    x, h, a, b, u, y = saved
    dy, dab, part_gate = ffn_down_bwd(dxo, y, vec, w_out, a, b)
    dx, part_norm = ffn_up_bwd(dab, w_in_t, x, dxo, vec)
    g_out = grad_slots(u, dy, "ffn_dw_out")
    g_in_t = grad_slots(dab, h, "ffn_dw_in")
    rows = jnp.concatenate([part_norm[0:3], part_gate[0:1]], axis=0)
    return dx, g_in_t, g_out, rows


_TRANSPOSED = ("ffn1_w_in", "ffn2_w_in", "attn_w_q")
_COL_NATURAL = ("conv_w_in", "w_kv", "attn_w_o")
_ROW_SHARDED = ("ffn1_w_out", "ffn2_w_out", "conv_w_out")
_BIG = _TRANSPOSED + _COL_NATURAL + _ROW_SHARDED


def weight_chunks():
    chunks = []
    for layer in range(DEPTH):
        first = [("ffn1_w_in", layer), ("ffn1_w_out", layer)]
        if layer == N_A_LAYERS:
            first = [("w_kv", layer)] + first
        mixer = [("conv_w_in", layer), ("conv_w_out", layer)] if layer < N_A_LAYERS else [("attn_w_q", layer), ("attn_w_o", layer)]
        rest = mixer + [("ffn2_w_in", layer), ("ffn2_w_out", layer)]
        chunks += [first, rest] if layer == 0 else [first + rest]
    return chunks


def stacked_index(name, layer):
    if name == "w_kv":
        return None
    return layer - N_A_LAYERS if name.startswith("attn") else layer


class ChunkComm:
    def __init__(self, shards):
        self.shards = shards
        self.chunks = weight_chunks()

    def _shard(self, name, layer):
        idx = stacked_index(name, layer)
        return self.shards[name][0 if idx is None else idx]

    def start_gather(self, ci, after):
        xs = [self._shard(n, l).astype(BF16) for n, l in self.chunks[ci]]
        return comm_start(xs, False, after, f"gather_start_{ci}")

    def finish_gather(self, ci, started, after):
        lands = comm_wait(started, after, f"gather_wait_{ci}")
        W = {}
        for key, g in zip(self.chunks[ci], lands):
            W[key] = _cols_to_natural(g) if key[0] in _COL_NATURAL else g.reshape(-1, g.shape[2])
        return W, lands[0]

    def start_exchange(self, ci, slots, after):
        return comm_start([slots[key] for key in self.chunks[ci]], True, after, f"exchange_start_{ci}")

    def finish_exchange(self, ci, started, after):
        lands = comm_wait(started, after, f"exchange_wait_{ci}")
        return dict(zip(self.chunks[ci], lands))


def device_step(x, positions, target, mods, kvmods, small, comm, gather0):
    T, D = x.shape
    groups = DILATED_GROUPS
    lane = jnp.arange(LANES) % HEAD_DIM
    inv = ROPE_THETA ** (-jnp.arange(0, ROPE_DIM, 2, dtype=F32) / ROPE_DIM)
    lane_rows = _vec8([jnp.where(lane < ROPE_DIM, inv[lane % (ROPE_DIM // 2)], 0.0), lane < ROPE_DIM,
                       (lane >= ROPE_DIM // 2) & (lane < ROPE_DIM), lane < ROPE_DIM // 2], LANES)
    tabs = rope_tables(positions.reshape(T, 1), lane_rows)

    def after_token(v, token):
        return v if token is None else v + token[0, 0]

    def vec_of(layer, sub, token=None):
        return after_token(_vec8([small["norm_g"][layer, sub], mods[layer, 3 * sub], mods[layer, 3 * sub + 1],
                                  mods[layer, 3 * sub + 2]], D), token)

    saved = []
    kv_saved = None
    k_sh = v_sh = None
    qw = GROUP_WIDTH * len(groups)
    chunk_of = {key: ci for ci, chunk in enumerate(comm.chunks) for key in chunk}
    W = {}
    flight = {"ci": 0, "started": gather0, "token": None}

    def need(key, after):
        if key not in W:
            ci = chunk_of[key]
            assert ci == flight["ci"], (key, ci)
            got, landed = comm.finish_gather(ci, flight["started"], after)
            W.update(got)
            if ci + 1 < len(comm.chunks):
                flight.update(ci=ci + 1, started=comm.start_gather(ci + 1, landed))
                flight["token"] = flight["started"]["token"]
        return W[key]

    def behind_start(v):
        token, flight["token"] = flight["token"], None
        return after_token(v, token)

    for layer in range(DEPTH):
        if layer == N_A_LAYERS:
            w_kv = need(("w_kv", layer), x)
            kv_vec = behind_start(_vec8([small["kv_norm_g"], kvmods[0], kvmods[1]], D))
            h_kv, k_sh, v_sh = proj_rope_fwd(x, kv_vec, w_kv, tabs, qw, False, "kv_fwd")
            kv_saved = (x, h_kv, kv_vec)
        rec = {}
        w_in, w_out = need(("ffn1_w_in", layer), x), need(("ffn1_w_out", layer), x)
        v1 = behind_start(vec_of(layer, 0))
        x, rec["ffn1"] = _ffn_forward(x, v1, w_in, w_out)
        if layer < N_A_LAYERS:
            w_in, w_out = need(("conv_w_in", layer), x), need(("conv_w_out", layer), x)
            v2 = behind_start(vec_of(layer, 1))
            cw = _vec8(list(small["conv_w"][layer]), D)
            x_in = x
            x, h, bcu, cv, z, y = conv_fwd(x, v2, cw, w_in, w_out)
            rec["mix"] = (x_in, h, bcu, cv, z, y, cw)
        else:
            w_q, w_o = need(("attn_w_q", layer), x), need(("attn_w_o", layer), x)
            v2 = behind_start(vec_of(layer, 1))
            x_in = x
            h, q = proj_rope_fwd(x, v2, w_q, tabs, qw, True, "q_fwd")
            os_, ls = [], []
            for g, (win, dil) in enumerate(groups):
                o, l = attn_core_fwd(q, k_sh, v_sh, g, win // dil, dil)
                os_.append(o)
                ls.append(l)
            x, mixed, y = attn_mix_out(os_, ls, x, v2, w_o)
            rec["mix"] = (x_in, h, q, os_, ls, mixed, y)
        w_in, w_out = need(("ffn2_w_in", layer), x), need(("ffn2_w_out", layer), x)
        v3 = behind_start(vec_of(layer, 2))
        x, rec["ffn2"] = _ffn_forward(x, v3, w_in, w_out)
        rec["vecs"] = (v1, v2, v3)
        saved.append(rec)

    dx, part_final, loss_tile = final_loss(x, _vec8([small["final_norm_g"]], D), target)
    loss = loss_tile[0, 0]

    conv_rows = [None] * N_A_LAYERS
    kv_rows = None
    mod_rows = [[None] * 3 for _ in range(DEPTH)]
    dkv_pairs = [{"k": [], "v": []} for _ in groups]
    slots = {}
    exchanges = []
    token = None

    def send_ready_chunks():
        nonlocal token
        for ci in reversed(range(len(comm.chunks))):
            if ci not in [e[0] for e in exchanges] and all(key in slots for key in comm.chunks[ci]):
                started = comm.start_exchange(ci, slots, token)
                exchanges.append((ci, started))
                token = started["token"]

    for layer in reversed(range(DEPTH)):
        rec = saved[layer]
        v1, v2, v3 = rec["vecs"]
        dx, slots[("ffn2_w_in", layer)], slots[("ffn2_w_out", layer)], mod_rows[layer][2] = _ffn_backward(
            dx, rec["ffn2"], after_token(v3, token), W[("ffn2_w_in", layer)], W[("ffn2_w_out", layer)])
        if layer < N_A_LAYERS:
            x_in, h, bcu, cv, z, y, cw = rec["mix"]
            dx, dy, dbcu, part, dcw = conv_bwd(dx, x_in, y, bcu, cv, v2, cw, W[("conv_w_in", layer)], W[("conv_w_out", layer)])
            slots[("conv_w_out", layer)] = grad_slots(z, dy, "conv_dw_out")
            slots[("conv_w_in", layer)] = grad_slots(h, dbcu, "conv_dw_in", col_slots=True)
            conv_rows[layer] = dcw[0:3]
            mod_rows[layer][1] = part[0:4]
        else:
            x_in, h, q, os_, ls, mixed, y = rec["mix"]
            outs = attn_mix_bwd(dx, y, v2, W[("attn_w_o", layer)], os_, ls)
            ng = len(groups)
            dy, dos, rrs, part_gate = outs[0], outs[1:1 + ng], outs[1 + ng:1 + 2 * ng], outs[1 + 2 * ng]
            slots[("attn_w_o", layer)] = grad_slots(mixed, dy, "attn_dw_o", col_slots=True)
            dqs = []
            for g, (win, dil) in enumerate(groups):
                dq, dkc, dkp, dvc, dvp = attn_core_bwd(q, k_sh, v_sh, dos[g], rrs[g], ls[g], g, win // dil, dil)
                dqs.append(dq)
                dkv_pairs[g]["k"].append((dkc, dkp))
                dkv_pairs[g]["v"].append((dvc, dvp))
            dx, dqr, part_norm = proj_rope_bwd(dqs, x_in, dx, v2, W[("attn_w_q", layer)], tabs, qw, True, "q_bwd")
            slots[("attn_w_q", layer)] = grad_slots(dqr, h, "attn_dw_q")
            mod_rows[layer][1] = jnp.concatenate([part_norm[0:3], part_gate[0:1]], axis=0)
        send_ready_chunks()
        dx, slots[("ffn1_w_in", layer)], slots[("ffn1_w_out", layer)], mod_rows[layer][0] = _ffn_backward(
            dx, rec["ffn1"], after_token(v1, token), W[("ffn1_w_in", layer)], W[("ffn1_w_out", layer)])
        if layer == N_A_LAYERS:
            x_kv, h_kv, kv_vec = kv_saved
            dparts = [dkv_combine(dkv_pairs[g]["k"], win // dil, dil, f"dk_combine_g{g}") for g, (win, dil) in enumerate(groups)]
            dparts += [dkv_combine(dkv_pairs[g]["v"], win // dil, dil, f"dv_combine_g{g}") for g, (win, dil) in enumerate(groups)]
            dx, dkvp, part_kv = proj_rope_bwd(dparts, x_kv, dx, kv_vec, W[("w_kv", layer)], tabs, qw, False, "kv_bwd")
            slots[("w_kv", layer)] = grad_slots(h_kv, dkvp, "kv_dw", col_slots=True)
            kv_rows = part_kv[0:3]
        send_ready_chunks()

    grads = {"conv_w": jnp.stack(conv_rows), "kv_rows": kv_rows, "exchanges": exchanges}
    grads["final_norm_g"] = part_final[0]
    rows = jnp.stack([jnp.stack(r) for r in mod_rows])
    grads["norm_g"] = rows[:, :, 0]
    grads["mods"] = rows[:, :, 1:4].reshape(DEPTH, N_MOD, D)
    return loss, dx, grads


def _flat2(a):
    return a.reshape(-1, a.shape[-1])


def _pad_rows(a, mult):
    r = a.shape[0]
    pad = (-r) % mult
    return a if pad == 0 else jnp.concatenate([a, jnp.zeros((pad,) + a.shape[1:], a.dtype)], axis=0)


def kernel(x, c, positions, norm_g, ada_w, ada_b, ffn1_w_in, ffn1_w_out, ffn2_w_in, ffn2_w_out, conv_w_in, conv_w, conv_w_out, kv_norm_g, kv_ada_w, kv_ada_b, w_kv, attn_w_q, attn_w_o, final_norm_g, loss_target, m_norm_g, m_ada_w, m_ada_b, m_ffn1_w_in, m_ffn1_w_out, m_ffn2_w_in, m_ffn2_w_out, m_conv_w_in, m_conv_w, m_conv_w_out, m_kv_norm_g, m_kv_ada_w, m_kv_ada_b, m_w_kv, m_attn_w_q, m_attn_w_o, m_final_norm_g, v_norm_g, v_ada_w, v_ada_b, v_ffn1_w_in, v_ffn1_w_out, v_ffn2_w_in, v_ffn2_w_out, v_conv_w_in, v_conv_w, v_conv_w_out, v_kv_norm_g, v_kv_ada_w, v_kv_ada_b, v_w_kv, v_attn_w_q, v_attn_w_o, v_final_norm_g):
    names = ("norm_g", "ada_w", "ada_b", "ffn1_w_in", "ffn1_w_out", "ffn2_w_in", "ffn2_w_out", "conv_w_in", "conv_w",
             "conv_w_out", "kv_norm_g", "kv_ada_w", "kv_ada_b", "w_kv", "attn_w_q", "attn_w_o", "final_norm_g")
    wts = dict(zip(names, (norm_g, ada_w, ada_b, ffn1_w_in, ffn1_w_out, ffn2_w_in, ffn2_w_out, conv_w_in, conv_w, conv_w_out,
                           kv_norm_g, kv_ada_w, kv_ada_b, w_kv, attn_w_q, attn_w_o, final_norm_g)))
    mom = dict(zip(names, (m_norm_g, m_ada_w, m_ada_b, m_ffn1_w_in, m_ffn1_w_out, m_ffn2_w_in, m_ffn2_w_out, m_conv_w_in,
                           m_conv_w, m_conv_w_out, m_kv_norm_g, m_kv_ada_w, m_kv_ada_b, m_w_kv, m_attn_w_q, m_attn_w_o,
                           m_final_norm_g)))
    var = dict(zip(names, (v_norm_g, v_ada_w, v_ada_b, v_ffn1_w_in, v_ffn1_w_out, v_ffn2_w_in, v_ffn2_w_out, v_conv_w_in,
                           v_conv_w, v_conv_w_out, v_kv_norm_g, v_kv_ada_w, v_kv_ada_b, v_w_kv, v_attn_w_q, v_attn_w_o,
                           v_final_norm_g)))
    T, D = x.shape[1], x.shape[2]
    me = _my_id()
    nmod = ada_w.shape[2]
    nkv = kv_ada_w.shape[1]

    def stacked(w, n):
        w = w if w.ndim == 3 else w[None]
        return jnp.swapaxes(w, 1, 2) if n in _TRANSPOSED else w

    comm = ChunkComm({n: stacked(wts[n], n) for n in _BIG})
    W = {}

    ds = norm_g.shape[2]
    small = jnp.concatenate([c.reshape(-1), norm_g.reshape(-1), conv_w.reshape(-1)]).astype(F32)
    n_small = small.shape[0]
    small = _pad_rows(small.reshape(-1, 1), 8 * LANES).reshape(-1, LANES)
    (small_all,) = all_gather([small], pltpu.VMEM, "gather_small")
    small_all = small_all.reshape(N_DEV, -1)[:, :n_small]
    c_all = small_all[:, :D]
    def full_rows(off, count):
        return jnp.stack([small_all[:, off + i * ds:off + (i + 1) * ds].reshape(D) for i in range(count)])

    W["norm_g"] = full_rows(D, DEPTH * 3).reshape(DEPTH, 3, D)
    W["conv_w"] = full_rows(D + DEPTH * 3 * ds, N_A_LAYERS * 3).reshape(N_A_LAYERS, 3, D)
    W["kv_norm_g"], W["final_norm_g"] = kv_norm_g, final_norm_g

    ada_b_mine = lax.dynamic_slice_in_dim(ada_b, me * nmod, nmod, axis=1).reshape(DEPTH, 1, nmod)
    kv_b_mine = lax.dynamic_slice_in_dim(kv_ada_b, me * nkv, nkv, axis=0).reshape(1, 1, nkv)
    mods_cols = mods_project(c_all, ada_w, ada_b_mine)
    kv_cols = mods_project(c_all, kv_ada_w.reshape(1, D, nkv), kv_b_mine)
    mcat = jnp.concatenate([mods_cols[l] for l in range(DEPTH)] + [kv_cols[0]], axis=1)
    wm = mcat.shape[1]
    if wm % LANES:
        mcat = jnp.concatenate([mcat, jnp.zeros((N_DEV, LANES - wm % LANES), F32)], axis=1)
    (mods_all,) = exchange_slots([mcat.reshape(N_DEV, 1, -1)], "exchange_mods")
    gather0 = comm.start_gather(0, mods_all)
    mods_all = mods_all.reshape(N_DEV, -1)
    mods = jnp.stack([mods_all[:, l * nmod:(l + 1) * nmod].reshape(N_MOD, D) for l in range(DEPTH)])
    kvmods = mods_all[:, DEPTH * nmod:DEPTH * nmod + nkv].reshape(2, D)

    loss_local, dx, grads = device_step(x[0], positions[0], loss_target[0], mods, kvmods, W, comm, gather0)
    loss = lax.psum(loss_local, MESH_AXES)

    dmods = grads["mods"].reshape(-1)
    dkvm = grads["kv_rows"][1:3].reshape(-1)
    vecs = jnp.concatenate([dmods, dkvm, grads["kv_rows"][0], grads["final_norm_g"], grads["norm_g"].reshape(-1),
                            grads["conv_w"].reshape(-1)])
    n_vec = vecs.shape[0]
    vecs = _pad_rows(vecs.reshape(-1, 1), 8 * LANES).reshape(-1, LANES)
    (vec_all,) = all_gather([vecs], pltpu.VMEM, "gather_vector_grads")
    vec_all = vec_all.reshape(N_DEV, -1)[:, :n_vec]
    nm_, nk_ = DEPTH * N_MOD * D, 2 * D
    dmods_all = vec_all[:, :nm_].reshape(N_DEV, DEPTH, N_MOD * D)
    dkvm_all = vec_all[:, nm_:nm_ + nk_]
    rest = vec_all[:, nm_ + nk_:]
    parts_kv_norm, parts_final = rest[:, :D].reshape(N_DEV, 1, D), rest[:, D:2 * D].reshape(N_DEV, 1, D)
    parts_norm = lax.dynamic_slice_in_dim(rest[:, 2 * D:2 * D + DEPTH * 3 * D].reshape(N_DEV, DEPTH * 3, D), me * ds, ds, axis=2)
    parts_conv = lax.dynamic_slice_in_dim(rest[:, 2 * D + DEPTH * 3 * D:].reshape(N_DEV, N_A_LAYERS * 3, D), me * ds, ds, axis=2)
    dm_cols = lax.dynamic_slice_in_dim(dmods_all, me * nmod, nmod, axis=2)
    dm_mine = jnp.stack([dm_cols[:, l] for l in range(DEPTH)])
    dkv_mine = lax.dynamic_slice_in_dim(dkvm_all, me * nkv, nkv, axis=1).reshape(1, N_DEV, nkv)
    g_ada_w = mods_weight_grad(c_all, dm_mine)
    g_kv_ada_w = mods_weight_grad(c_all, dkv_mine)[0]

    out_g, out_d, out_m, out_v = {}, {}, {}, {}

    def update(n, g, w, parts=False):
        shp = w.shape
        w2 = w.reshape(1, -1) if w.ndim == 1 else _flat2(w)
        g2 = g if parts else g.reshape(w2.shape)
        res = adam_update(g2, w2, mom[n].reshape(w2.shape), var[n].reshape(w2.shape), parts, "adam_" + n)
        out_g[n], out_d[n], out_m[n], out_v[n] = (r.reshape(shp) for r in res)

    moms = {n: stacked(mom[n], n) for n in _BIG}
    vars_ = {n: stacked(var[n], n) for n in _BIG}
    results = {}
    after = dx
    for ci, started in grads["exchanges"]:
        for (n, layer), parts in comm.finish_exchange(ci, started, after).items():
            idx = stacked_index(n, layer)
            results[n] = adam_layer(parts, comm.shards[n], moms[n], vars_[n], results.get(n), 0 if idx is None else idx,
                                    after, f"adam_{n}_{layer}")
            after = results[n][1]
    for n in _BIG:
        res = [jnp.swapaxes(r, 1, 2) if n in _TRANSPOSED else r for r in results[n]]
        out_g[n], out_d[n], out_m[n], out_v[n] = (r.reshape(wts[n].shape) for r in res)
    update("ada_w", g_ada_w, ada_w)
    update("kv_ada_w", g_kv_ada_w, kv_ada_w)
    update("ada_b", dmods_all, ada_b, True)
    update("kv_ada_b", dkvm_all.reshape(N_DEV, 1, nk_), kv_ada_b, True)
    update("kv_norm_g", parts_kv_norm, kv_norm_g, True)
    update("final_norm_g", parts_final, final_norm_g, True)
    update("norm_g", parts_norm, norm_g, True)
    update("conv_w", parts_conv, conv_w, True)

    return (loss, dx.reshape(x.shape), *[out_g[n] for n in names], *[out_d[n] for n in names],
            *[out_m[n] for n in names], *[out_v[n] for n in names])
```

```python
import functools

import jax
import jax.numpy as jnp
from jax import lax
from jax.experimental import pallas as pl
from jax.experimental.pallas import tpu as pltpu

F32, BF16 = jnp.float32, jnp.bfloat16

N_DEV = 8
MESH_AXES = ("x", "y", "c")
DEPTH = 4
N_A_LAYERS = 2
HEAD_DIM = 64
HEADS_PER_GROUP = 8
GROUP_WIDTH = HEAD_DIM * HEADS_PER_GROUP
DILATED_GROUPS = ((128, 1), (512, 4), (2048, 16))
ROPE_DIM = HEAD_DIM // 4
ROPE_THETA = 500000.0
NORM_EPS = 1e-5
FFN_RES_WEIGHT = 0.5
N_MOD = 9
ADAM_LR, ADAM_B1, ADAM_B2, ADAM_EPS, ADAM_WD, ADAM_STEP = 0.001, 0.9, 0.999, 1e-08, 0.01, 10

LANES = 128
TOKEN_TILE = 512
CONTRACT_TILE = 2048
MXU_WIDTH = 256
VMEM_LIMIT = 56 * 1024 * 1024
MESH = pl.DeviceIdType.MESH


def _cp(*sem):
    return pltpu.CompilerParams(dimension_semantics=sem, vmem_limit_bytes=VMEM_LIMIT)


def _pick(n, cap, mult=LANES):
    if n <= cap:
        return n
    best = None
    for t in range(mult, cap + 1, mult):
        if n % t == 0:
            best = t
    assert best is not None, (n, cap)
    return best


def _tok(tm, w):
    return pl.BlockSpec((tm, w), lambda i: (i, 0))


def _res(shape):
    nd = len(shape)
    return pl.BlockSpec(shape, lambda *_: (0,) * nd, pipeline_mode=pl.Buffered(1))


def _sds(shape, dt):
    return jax.ShapeDtypeStruct(shape, dt)


def _sigmoid(a):
    return 1.0 / (1.0 + jnp.exp(-a))


def _modnorm(x, g, sh, sc):
    r = lax.rsqrt(jnp.mean(x * x, axis=-1, keepdims=True) + NORM_EPS)
    return (x * r * g) * (1.0 + sc) + sh


def _dot(a, b):
    return jnp.dot(a, b, preferred_element_type=F32)


def _dot_nt(a, b):
    return lax.dot_general(a, b, (((1,), (1,)), ((), ())), preferred_element_type=F32)


def _dot_tn(a, b):
    return lax.dot_general(a, b, (((0,), (0,)), ((), ())), preferred_element_type=F32)


def _rows8(rows, d):
    pad = 8 - len(rows)
    return jnp.concatenate(list(rows) + [jnp.zeros((pad, d), F32)], axis=0)


def _acc_rows(ref, tile, first):
    @pl.when(first)
    def _():
        ref[...] = tile

    @pl.when(jnp.logical_not(first))
    def _():
        ref[...] += tile


def ffn_up(x, vec, w_in_t):
    T, D = x.shape
    F = w_in_t.shape[0] // 2
    tm, cw = min(TOKEN_TILE, T), _pick(F, MXU_WIDTH)

    def body(x_ref, vec_ref, w_ref, h_ref, ga_ref, gb_ref, u_ref):
        hb = _modnorm(x_ref[...], vec_ref[0:1], vec_ref[1:2], vec_ref[2:3]).astype(BF16)
        h_ref[...] = hb
        for c in range(F // cw):
            lo, hi = c * cw, (c + 1) * cw
            a = _dot_nt(hb, w_ref[lo:hi, :])
            b = _dot_nt(hb, w_ref[F + lo:F + hi, :])
            sg = _sigmoid(a)
            silu = a * sg
            ga_ref[:, lo:hi] = (b * (sg + silu * (1.0 - sg))).astype(BF16)
            gb_ref[:, lo:hi] = silu.astype(BF16)
            u_ref[:, lo:hi] = (silu * b).astype(BF16)

    return pl.pallas_call(
        body, grid=(T // tm,),
        in_specs=[_tok(tm, D), _res((8, D)), _res((2 * F, D))],
        out_specs=[_tok(tm, D), _tok(tm, F), _tok(tm, F), _tok(tm, F)],
        out_shape=[_sds((T, D), BF16), _sds((T, F), BF16), _sds((T, F), BF16), _sds((T, F), BF16)],
        compiler_params=_cp("arbitrary"), name="ffn_up")(x, vec, w_in_t)


def proj_out(u, x, vec, w_out, res_weight, name):
    T, D = x.shape
    K = u.shape[1]
    tm = min(TOKEN_TILE, T)

    def body(u_ref, x_ref, vec_ref, w_ref, xn_ref, y_ref):
        y = _dot(u_ref[...], w_ref[...])
        y_ref[...] = y.astype(BF16)
        xn_ref[...] = x_ref[...] + (res_weight * (1.0 + vec_ref[3:4])) * y

    return pl.pallas_call(
        body, grid=(T // tm,),
        in_specs=[_tok(tm, K), _tok(tm, D), _res((8, D)), _res((K, D))],
        out_specs=[_tok(tm, D), _tok(tm, D)],
        out_shape=[_sds((T, D), F32), _sds((T, D), BF16)],
        compiler_params=_cp("arbitrary"), name=name)(u, x, vec, w_out)


def ffn_down_bwd(dxo, y, vec, w_out, a, b):
    T, D = dxo.shape
    F = a.shape[1]
    tm, cw = min(TOKEN_TILE, T), _pick(F, MXU_WIDTH)

    def body(dxo_ref, y_ref, vec_ref, w_ref, a_ref, b_ref, dy_ref, dab_ref, part_ref):
        dxo_t = dxo_ref[...]
        dyb = (dxo_t * (FFN_RES_WEIGHT * (1.0 + vec_ref[3:4]))).astype(BF16)
        dy_ref[...] = dyb
        dgate = FFN_RES_WEIGHT * jnp.sum(dxo_t * y_ref[...].astype(F32), axis=0, keepdims=True)
        _acc_rows(part_ref, _rows8([dgate], D), pl.program_id(0) == 0)
        for c in range(F // cw):
            lo, hi = c * cw, (c + 1) * cw
            du = _dot_nt(dyb, w_ref[lo:hi, :])
            dab_ref[:, lo:hi] = (du * a_ref[:, lo:hi].astype(F32)).astype(BF16)
            dab_ref[:, F + lo:F + hi] = (du * b_ref[:, lo:hi].astype(F32)).astype(BF16)

    return pl.pallas_call(
        body, grid=(T // tm,),
        in_specs=[_tok(tm, D), _tok(tm, D), _res((8, D)), _res((F, D)), _tok(tm, F), _tok(tm, F)],
        out_specs=[_tok(tm, D), _tok(tm, 2 * F), pl.BlockSpec((8, D), lambda i: (0, 0))],
        out_shape=[_sds((T, D), BF16), _sds((T, 2 * F), BF16), _sds((8, D), F32)],
        compiler_params=_cp("arbitrary"), name="ffn_down_bwd")(dxo, y, vec, w_out, a, b)


def ffn_up_bwd(dab, w_in_t, x, dxo, vec):
    T, D = x.shape
    F2 = dab.shape[1]
    tm = min(TOKEN_TILE, T)

    def body(dab_ref, w_ref, x_ref, dxo_ref, vec_ref, dx_ref, part_ref):
        dh = _dot(dab_ref[...], w_ref[...])
        _, vjp = jax.vjp(_modnorm, x_ref[...], vec_ref[0:1], vec_ref[1:2], vec_ref[2:3])
        dx, dg, dsh, dsc = vjp(dh)
        dx_ref[...] = dxo_ref[...] + dx
        _acc_rows(part_ref, _rows8([dg, dsh, dsc], D), pl.program_id(0) == 0)

    return pl.pallas_call(
        body, grid=(T // tm,),
        in_specs=[_tok(tm, F2), _res((F2, D)), _tok(tm, D), _tok(tm, D), _res((8, D))],
        out_specs=[_tok(tm, D), pl.BlockSpec((8, D), lambda i: (0, 0))],
        out_shape=[_sds((T, D), F32), _sds((8, D), F32)],
        compiler_params=_cp("arbitrary"), name="ffn_up_bwd")(dab, w_in_t, x, dxo, vec)


def grad_slots(a, b, name, col_slots=False):
    T, M = a.shape
    N = b.shape[1]
    tk = min(CONTRACT_TILE, T)
    nk = T // tk
    tmm = _pick(M, 1408)
    if col_slots:
        ns = N // N_DEV
        sp = max(s for s in (1, 2, 4, 8) if ns * s <= 1536)
        tn = ns * sp
    else:
        tn = _pick(N, 1536)

    def body(a_ref, b_ref, o_ref, acc):
        k = pl.program_id(2)
        t = _dot_tn(a_ref[...], b_ref[...])

        @pl.when(k == 0)
        def _():
            acc[...] = t

        @pl.when(k > 0)
        def _():
            acc[...] += t

        @pl.when(k == nk - 1)
        def _():
            if col_slots:
                for s in range(sp):
                    o_ref[s] = acc[:, s * ns:(s + 1) * ns].astype(BF16)
            else:
                o_ref[...] = acc[...].astype(BF16)

    if col_slots:
        out_spec, out_shape = pl.BlockSpec((sp, tmm, ns), lambda i, j, k: (j, i, 0)), _sds((N_DEV, M, ns), BF16)
    else:
        out_spec, out_shape = pl.BlockSpec((tmm, tn), lambda i, j, k: (i, j)), _sds((M, N), BF16)
    out = pl.pallas_call(
        body, grid=(M // tmm, N // tn, nk),
        in_specs=[pl.BlockSpec((tk, tmm), lambda i, j, k: (k, i)), pl.BlockSpec((tk, tn), lambda i, j, k: (k, j))],
        out_specs=out_spec, out_shape=out_shape,
        scratch_shapes=[pltpu.VMEM((tmm, tn), F32)],
        compiler_params=_cp("arbitrary", "arbitrary", "arbitrary"), name=name)(a, b)
    return out if col_slots else out.reshape(N_DEV, M // N_DEV, N)


def conv_fwd(x, vec, cw, w_in, w_out):
    T, D = x.shape
    tm = min(TOKEN_TILE, T)

    def body(x_ref, vec_ref, cw_ref, wi_ref, wo_ref, xn_ref, h_ref, bcu_ref, cv_ref, z_ref, y_ref, vbuf):
        @pl.when(pl.program_id(0) == 0)
        def _():
            vbuf[0:8, :] = jnp.zeros((8, D), F32)

        x_t = x_ref[...]
        hb = _modnorm(x_t, vec_ref[0:1], vec_ref[1:2], vec_ref[2:3]).astype(BF16)
        h_ref[...] = hb
        bcu = _dot(hb, wi_ref[...])
        bcu_ref[...] = bcu.astype(BF16)
        bg, v = bcu[:, 0:D], bcu[:, D:2 * D] * bcu[:, 2 * D:3 * D]
        vbuf[8:8 + tm, :] = v
        conv = cw_ref[0:1] * vbuf[6:6 + tm, :] + cw_ref[1:2] * vbuf[7:7 + tm, :] + cw_ref[2:3] * v
        cv_ref[...] = conv.astype(BF16)
        zb = (bg * conv).astype(BF16)
        z_ref[...] = zb
        y = _dot(zb, wo_ref[...])
        y_ref[...] = y.astype(BF16)
        xn_ref[...] = x_t + (1.0 + vec_ref[3:4]) * y
        vbuf[0:8, :] = vbuf[tm:tm + 8, :]

    return pl.pallas_call(
        body, grid=(T // tm,),
        in_specs=[_tok(tm, D), _res((8, D)), _res((8, D)), _res((D, 3 * D)), _res((D, D))],
        out_specs=[_tok(tm, D), _tok(tm, D), _tok(tm, 3 * D), _tok(tm, D), _tok(tm, D), _tok(tm, D)],
        out_shape=[_sds((T, D), F32), _sds((T, D), BF16), _sds((T, 3 * D), BF16), _sds((T, D), BF16),
                   _sds((T, D), BF16), _sds((T, D), BF16)],
        scratch_shapes=[pltpu.VMEM((tm + 8, D), F32)],
        compiler_params=_cp("arbitrary"), name="conv_fwd")(x, vec, cw, w_in, w_out)


def conv_bwd(dxo, x, y, bcu, cv, vec, cw, w_in, w_out):
    T, D = x.shape
    tm = min(TOKEN_TILE, T)
    nt = T // tm

    def body(dxo_ref, x_ref, y_ref, bcu_ref, cv_ref, vec_ref, cw_ref, wi_ref, wo_ref,
             dx_ref, dy_ref, dbcu_ref, part_ref, dcw_ref, dcbuf):
        first = pl.program_id(0) == 0

        @pl.when(first)
        def _():
            dcbuf[tm:tm + 8, :] = jnp.zeros((8, D), F32)

        dxo_t = dxo_ref[...]
        dyb = (dxo_t * (1.0 + vec_ref[3:4])).astype(BF16)
        dy_ref[...] = dyb
        dgate = jnp.sum(dxo_t * y_ref[...].astype(F32), axis=0, keepdims=True)
        dz = _dot_nt(dyb, wo_ref[...])
        bcu_t = bcu_ref[...].astype(F32)
        bg, cg, ug = bcu_t[:, 0:D], bcu_t[:, D:2 * D], bcu_t[:, 2 * D:3 * D]
        dconv = dz * bg
        dbg = dz * cv_ref[...].astype(F32)
        dcbuf[0:tm, :] = dconv
        d1, d2 = dcbuf[1:tm + 1, :], dcbuf[2:tm + 2, :]
        dv = cw_ref[2:3] * dconv + cw_ref[1:2] * d1 + cw_ref[0:1] * d2
        v = cg * ug
        dcw = _rows8([jnp.sum(d2 * v, axis=0, keepdims=True), jnp.sum(d1 * v, axis=0, keepdims=True),
                      jnp.sum(dconv * v, axis=0, keepdims=True)], D)
        dbcu = jnp.concatenate([dbg, dv * ug, dv * cg], axis=1).astype(BF16)
        dbcu_ref[...] = dbcu
        dh = _dot_nt(dbcu, wi_ref[...])
        _, vjp = jax.vjp(_modnorm, x_ref[...], vec_ref[0:1], vec_ref[1:2], vec_ref[2:3])
        dx, dg, dsh, dsc = vjp(dh)
        dx_ref[...] = dxo_t + dx
        _acc_rows(part_ref, _rows8([dg, dsh, dsc, dgate], D), first)
        _acc_rows(dcw_ref, dcw, first)
        dcbuf[tm:tm + 8, :] = dcbuf[0:8, :]

    def rev(w):
        return pl.BlockSpec((tm, w), lambda i: (nt - 1 - i, 0))

    return pl.pallas_call(
        body, grid=(nt,),
        in_specs=[rev(D), rev(D), rev(D), rev(3 * D), rev(D), _res((8, D)), _res((8, D)), _res((D, 3 * D)), _res((D, D))],
        out_specs=[rev(D), rev(D), rev(3 * D), pl.BlockSpec((8, D), lambda i: (0, 0)), pl.BlockSpec((8, D), lambda i: (0, 0))],
        out_shape=[_sds((T, D), F32), _sds((T, D), BF16), _sds((T, 3 * D), BF16), _sds((8, D), F32), _sds((8, D), F32)],
        scratch_shapes=[pltpu.VMEM((tm + 8, D), F32)],
        compiler_params=_cp("arbitrary"), name="conv_bwd")(dxo, x, y, bcu, cv, vec, cw, w_in, w_out)


def rope_tables(pos, lane_rows):
    T = pos.shape[0]
    tm = min(TOKEN_TILE, T)

    def body(p_ref, lr_ref, c_ref, sp_ref, sm_ref):
        ang = p_ref[...].astype(F32) * lr_ref[0:1]
        cs, sn = jnp.cos(ang), jnp.sin(ang)
        c_ref[...] = jnp.where(lr_ref[1:2] > 0.5, cs, 1.0)
        sp_ref[...] = jnp.where(lr_ref[2:3] > 0.5, sn, 0.0)
        sm_ref[...] = jnp.where(lr_ref[3:4] > 0.5, -sn, 0.0)

    return pl.pallas_call(
        body, grid=(T // tm,),
        in_specs=[_tok(tm, 1), _res((8, LANES))],
        out_specs=[_tok(tm, LANES)] * 3,
        out_shape=[_sds((T, LANES), F32)] * 3,
        compiler_params=_cp("arbitrary"), name="rope_tables")(pos, lane_rows)


def _rope(t, c, sp, sm):
    w = t.shape[1]
    reps = w // LANES
    cf, spf, smf = jnp.tile(c, (1, reps)), jnp.tile(sp, (1, reps)), jnp.tile(sm, (1, reps))
    half = ROPE_DIM // 2
    return t * cf + pltpu.roll(t, half, axis=1) * spf + pltpu.roll(t, w - half, axis=1) * smf


def _rope_t(d, c, sp, sm):
    w = d.shape[1]
    reps = w // LANES
    cf, spf, smf = jnp.tile(c, (1, reps)), jnp.tile(sp, (1, reps)), jnp.tile(sm, (1, reps))
    half = ROPE_DIM // 2
    return d * cf + pltpu.roll(d * spf, w - half, axis=1) + pltpu.roll(d * smf, half, axis=1)


def _split_residues(v, d, stage):
    tm, width = v.shape
    if d == 1:
        return [v]
    nj = width // LANES
    for j in range(nj):
        stage[j] = v[:, j * LANES:(j + 1) * LANES]
    return [jnp.concatenate([stage[j, pl.ds(r, tm // d, stride=d), :] for j in range(nj)], axis=1) for r in range(d)]


def _merge_residues(piece, d, tm, width, stage):
    if d == 1:
        return piece(0)
    nj = width // LANES
    for r in range(d):
        p = piece(r)
        for j in range(nj):
            stage[j, pl.ds(r, tm // d, stride=d), :] = p[:, j * LANES:(j + 1) * LANES]
    return jnp.concatenate([stage[j] for j in range(nj)], axis=1)


def _residue_spec(d, tm):
    return pl.BlockSpec((d, tm // d, GROUP_WIDTH), lambda i: (0, i, 0))


def _stage_scratch(tm):
    return pltpu.VMEM((GROUP_WIDTH // LANES, tm, LANES), F32)


def proj_rope_fwd(x, vec, w, tabs, n_rope, transposed, dils, name):
    T, D = x.shape
    N = w.shape[0] if transposed else w.shape[1]
    tm = min(TOKEN_TILE, T)
    GW = GROUP_WIDTH
    piece_dils = [dils[j % len(dils)] for j in range(N // GW)]

    def body(x_ref, vec_ref, w_ref, c_ref, sp_ref, sm_ref, h_ref, *rest):
        out_refs, stage = rest[:-1], rest[-1]
        hb = _modnorm(x_ref[...], vec_ref[0:1], vec_ref[1:2], vec_ref[2:3]).astype(BF16)
        h_ref[...] = hb
        p = _dot_nt(hb, w_ref[...]) if transposed else _dot(hb, w_ref[...])
        pr = _rope(p[:, 0:n_rope], c_ref[...], sp_ref[...], sm_ref[...])
        for j, d in enumerate(piece_dils):
            src = pr if (j + 1) * GW <= n_rope else p
            for r, rows in enumerate(_split_residues(src[:, j * GW:(j + 1) * GW], d, stage)):
                out_refs[j][r] = rows.astype(BF16)

    return pl.pallas_call(
        body, grid=(T // tm,),
        in_specs=[_tok(tm, D), _res((8, D)), _res(w.shape)] + [_tok(tm, LANES)] * 3,
        out_specs=[_tok(tm, D)] + [_residue_spec(d, tm) for d in piece_dils],
        out_shape=[_sds((T, D), BF16)] + [_sds((d, T // d, GW), BF16) for d in piece_dils],
        scratch_shapes=[_stage_scratch(tm)],
        compiler_params=_cp("arbitrary"), name=name)(x, vec, w, *tabs)


def proj_rope_bwd(dparts, dils, x, dxo, vec, w, tabs, n_rope, transposed, name):
    T, D = x.shape
    N = w.shape[0] if transposed else w.shape[1]
    tm = min(TOKEN_TILE, T)
    GW = GROUP_WIDTH
    npart = len(dparts)
    piece_dils = [dils[j % len(dils)] for j in range(npart)]

    def body(*refs):
        d_refs = refs[:npart]
        x_ref, dxo_ref, vec_ref, w_ref, c_ref, sp_ref, sm_ref, dx_ref, dp_ref, part_ref, stage = refs[npart:]
        d = jnp.concatenate([_merge_residues(lambda r, ref=ref: ref[r].astype(F32), dd, tm, GW, stage)
                             for ref, dd in zip(d_refs, piece_dils)], axis=1)
        dr = _rope_t(d[:, 0:n_rope], c_ref[...], sp_ref[...], sm_ref[...])
        if n_rope < N:
            dr = jnp.concatenate([dr, d[:, n_rope:N]], axis=1)
        dpb = dr.astype(BF16)
        dp_ref[...] = dpb
        dh = _dot(dpb, w_ref[...]) if transposed else _dot_nt(dpb, w_ref[...])
        _, vjp = jax.vjp(_modnorm, x_ref[...], vec_ref[0:1], vec_ref[1:2], vec_ref[2:3])
        dx, dg, dsh, dsc = vjp(dh)
        dx_ref[...] = dxo_ref[...] + dx
        _acc_rows(part_ref, _rows8([dg, dsh, dsc], D), pl.program_id(0) == 0)

    return pl.pallas_call(
        body, grid=(T // tm,),
        in_specs=[_residue_spec(d, tm) for d in piece_dils] + [_tok(tm, D), _tok(tm, D), _res((8, D)), _res(w.shape)]
        + [_tok(tm, LANES)] * 3,
        out_specs=[_tok(tm, D), _tok(tm, N), pl.BlockSpec((8, D), lambda i: (0, 0))],
        out_shape=[_sds((T, D), F32), _sds((T, N), BF16), _sds((8, D), F32)],
        scratch_shapes=[_stage_scratch(tm)],
        compiler_params=_cp("arbitrary"), name=name)(*dparts, x, dxo, vec, w, *tabs)


def _valid_mask(n, i):
    qi = lax.broadcasted_iota(jnp.int32, (n, 2 * n), 0)
    kj = lax.broadcasted_iota(jnp.int32, (n, 2 * n), 1)
    dist = n + qi - kj
    return (dist >= 0) & (dist <= n) & ((kj >= n) | (i > 0))


def _band_specs(n):
    cur = pl.BlockSpec((None, n, GROUP_WIDTH), lambda r, i: (r, i, 0))
    prv = pl.BlockSpec((None, n, GROUP_WIDTH), lambda r, i: (r, jnp.maximum(i - 1, 0), 0))
    return cur, prv


def attn_core_fwd(q, k, v, g, n):
    d, M, GW = q.shape
    scale = HEAD_DIM ** -0.5

    def body(q_ref, kp_ref, kc_ref, vp_ref, vc_ref, o_ref, l_ref):
        valid = _valid_mask(n, pl.program_id(1))
        qv = q_ref[...]
        kk = jnp.concatenate([kp_ref[...], kc_ref[...]], axis=0)
        vv = jnp.concatenate([vp_ref[...], vc_ref[...]], axis=0)
        for h in range(HEADS_PER_GROUP):
            hs = slice(HEAD_DIM * h, HEAD_DIM * (h + 1))
            s = jnp.where(valid, _dot_nt(qv[:, hs], kk[:, hs]) * scale, -1e30)
            m = jnp.max(s, axis=1, keepdims=True)
            p = jnp.exp(s - m)
            den = jnp.sum(p, axis=1, keepdims=True)
            o_ref[:, hs] = _dot((p / den).astype(BF16), vv[:, hs])
            l_ref[:, hs] = jnp.broadcast_to(m + jnp.log(den), (n, HEAD_DIM))

    cur, prv = _band_specs(n)
    return pl.pallas_call(
        body, grid=(d, M // n),
        in_specs=[cur, prv, cur, prv, cur], out_specs=[cur, cur],
        out_shape=[_sds((d, M, GW), F32), _sds((d, M, GW), F32)],
        compiler_params=_cp("arbitrary", "arbitrary"), name=f"attn_fwd_g{g}")(q, k, k, v, v)


def attn_core_bwd(q, k, v, do, rr, lse, g, n):
    d, M, GW = q.shape
    scale = HEAD_DIM ** -0.5

    def body(q_ref, kp_ref, kc_ref, vp_ref, vc_ref, do_ref, r_ref, l_ref, dq_ref, dkc_ref, dkp_ref, dvc_ref, dvp_ref):
        valid = _valid_mask(n, pl.program_id(1))
        qv, dov = q_ref[...], do_ref[...]
        kk = jnp.concatenate([kp_ref[...], kc_ref[...]], axis=0)
        vv = jnp.concatenate([vp_ref[...], vc_ref[...]], axis=0)
        for h in range(HEADS_PER_GROUP):
            hs = slice(HEAD_DIM * h, HEAD_DIM * (h + 1))
            s = jnp.where(valid, _dot_nt(qv[:, hs], kk[:, hs]) * scale, -1e30)
            p = jnp.exp(s - l_ref[:, HEAD_DIM * h:HEAD_DIM * h + 1])
            dp = _dot_nt(dov[:, hs], vv[:, hs])
            delta = jnp.sum(r_ref[:, hs], axis=1, keepdims=True)
            ds = (p * (dp - delta) * scale).astype(BF16)
            dq_ref[:, hs] = _dot(ds, kk[:, hs]).astype(BF16)
            dk = _dot_tn(ds, qv[:, hs]).astype(BF16)
            dv = _dot_tn(p.astype(BF16), dov[:, hs]).astype(BF16)
            dkp_ref[:, hs], dkc_ref[:, hs] = dk[0:n], dk[n:2 * n]
            dvp_ref[:, hs], dvc_ref[:, hs] = dv[0:n], dv[n:2 * n]

    cur, prv = _band_specs(n)
    return pl.pallas_call(
        body, grid=(d, M // n),
        in_specs=[cur, prv, cur, prv, cur, cur, cur, cur], out_specs=[cur] * 5,
        out_shape=[_sds((d, M, GW), BF16)] * 5,
        compiler_params=_cp("arbitrary", "arbitrary"), name=f"attn_bwd_g{g}")(q, k, k, v, v, do, rr, lse)


def dkv_combine(cur_prev, n, name):
    d, M, GW = cur_prev[0][0].shape
    nb = M // n
    flat = [a for pair in cur_prev for a in pair]

    def body(*refs):
        o_ref = refs[-1]
        last = pl.program_id(1) == nb - 1
        acc = jnp.zeros((n, GW), F32)
        for t in range(0, len(refs) - 1, 2):
            acc = acc + refs[t][...].astype(F32) + jnp.where(last, 0.0, refs[t + 1][...].astype(F32))
        o_ref[...] = acc.astype(BF16)

    cur = pl.BlockSpec((None, n, GW), lambda r, i: (r, i, 0))
    nxt = pl.BlockSpec((None, n, GW), lambda r, i: (r, jnp.minimum(i + 1, nb - 1), 0))
    return pl.pallas_call(
        body, grid=(d, nb), in_specs=[cur, nxt] * len(cur_prev), out_specs=cur,
        out_shape=_sds((d, M, GW), BF16),
        compiler_params=_cp("arbitrary", "arbitrary"), name=name)(*flat)


def _group_weights(ls):
    mx = functools.reduce(jnp.maximum, ls)
    es = [jnp.exp(l - mx) for l in ls]
    tot = functools.reduce(lambda a, b: a + b, es)
    return [e / tot for e in es]


def attn_mix_out(os_, ls, dils, x, vec, w_o):
    T, D = x.shape
    GW = GROUP_WIDTH
    tm = min(TOKEN_TILE, T)
    ng = len(os_)

    def body(*refs):
        o_refs, l_refs = refs[:ng], refs[ng:2 * ng]
        x_ref, vec_ref, w_ref, xn_ref, mix_ref, y_ref, stage = refs[2 * ng:]
        natural = lambda ref, d: _merge_residues(lambda r: ref[r], d, tm, GW, stage)
        ws = _group_weights([natural(r, d) for r, d in zip(l_refs, dils)])
        mixed = functools.reduce(lambda a, b: a + b, [w * natural(r, d) for w, r, d in zip(ws, o_refs, dils)])
        mb = mixed.astype(BF16)
        mix_ref[...] = mb
        y = _dot(mb, w_ref[...])
        y_ref[...] = y.astype(BF16)
        xn_ref[...] = x_ref[...] + (1.0 + vec_ref[3:4]) * y

    res = [_residue_spec(d, tm) for d in dils]
    return pl.pallas_call(
        body, grid=(T // tm,),
        in_specs=res + res + [_tok(tm, D), _res((8, D)), _res((GW, D))],
        out_specs=[_tok(tm, D), _tok(tm, GW), _tok(tm, D)],
        out_shape=[_sds((T, D), F32), _sds((T, GW), BF16), _sds((T, D), BF16)],
        scratch_shapes=[_stage_scratch(tm)],
        compiler_params=_cp("arbitrary"), name="attn_mix_out")(*os_, *ls, x, vec, w_o)


def attn_mix_bwd(dxo, y, vec, w_o, os_, ls, dils):
    T, D = dxo.shape
    GW = GROUP_WIDTH
    tm = min(TOKEN_TILE, T)
    ng = len(os_)

    def body(*refs):
        dxo_ref, y_ref, vec_ref, w_ref = refs[:4]
        o_refs, l_refs = refs[4:4 + ng], refs[4 + ng:4 + 2 * ng]
        dy_ref = refs[4 + 2 * ng]
        do_refs = refs[5 + 2 * ng:5 + 3 * ng]
        r_refs = refs[5 + 3 * ng:5 + 4 * ng]
        part_ref, stage = refs[5 + 4 * ng], refs[6 + 4 * ng]
        natural = lambda ref, d: _merge_residues(lambda r: ref[r], d, tm, GW, stage)
        dxo_t = dxo_ref[...]
        dyb = (dxo_t * (1.0 + vec_ref[3:4])).astype(BF16)
        dy_ref[...] = dyb
        dgate = jnp.sum(dxo_t * y_ref[...].astype(F32), axis=0, keepdims=True)
        _acc_rows(part_ref, _rows8([dgate], D), pl.program_id(0) == 0)
        dmix = _dot_nt(dyb, w_ref[...])
        ws = _group_weights([natural(r, d) for r, d in zip(l_refs, dils)])
        mixed = functools.reduce(lambda a, b: a + b, [w * natural(r, d) for w, r, d in zip(ws, o_refs, dils)])
        for gi in range(ng):
            do = ws[gi] * dmix
            for r, rows in enumerate(_split_residues(do, dils[gi], stage)):
                do_refs[gi][r] = rows.astype(BF16)
            for r, rows in enumerate(_split_residues(do * mixed, dils[gi], stage)):
                r_refs[gi][r] = rows

    res = [_residue_spec(d, tm) for d in dils]
    return pl.pallas_call(
        body, grid=(T // tm,),
        in_specs=[_tok(tm, D), _tok(tm, D), _res((8, D)), _res((GW, D))] + res + res,
        out_specs=[_tok(tm, D)] + res + res + [pl.BlockSpec((8, D), lambda i: (0, 0))],
        out_shape=[_sds((T, D), BF16)] + [_sds((d, T // d, GW), BF16) for d in dils]
        + [_sds((d, T // d, GW), F32) for d in dils] + [_sds((8, D), F32)],
        scratch_shapes=[_stage_scratch(tm)],
        compiler_params=_cp("arbitrary"), name="attn_mix_bwd")(dxo, y, vec, w_o, *os_, *ls)


def final_loss(x, gvec, target):
    T, D = x.shape
    tm = min(TOKEN_TILE, T)

    def norm(xv, g):
        return xv * lax.rsqrt(jnp.mean(xv * xv, axis=-1, keepdims=True) + NORM_EPS) * g

    def body(x_ref, g_ref, t_ref, dx_ref, part_ref, loss_ref):
        first = pl.program_id(0) == 0
        yv, vjp = jax.vjp(norm, x_ref[...], g_ref[0:1])
        err = yv - t_ref[...]
        dx, dg = vjp(err * (1.0 / D))
        dx_ref[...] = dx
        _acc_rows(part_ref, _rows8([dg], D), first)
        tile_loss = 0.5 * jnp.sum(jnp.sum(err * err, axis=1, keepdims=True) * (1.0 / D), axis=0, keepdims=True)
        _acc_rows(loss_ref, jnp.broadcast_to(tile_loss, (8, LANES)), first)

    return pl.pallas_call(
        body, grid=(T // tm,),
        in_specs=[_tok(tm, D), _res((8, D)), _tok(tm, D)],
        out_specs=[_tok(tm, D), pl.BlockSpec((8, D), lambda i: (0, 0)), pl.BlockSpec((8, LANES), lambda i: (0, 0))],
        out_shape=[_sds((T, D), F32), _sds((8, D), F32), _sds((8, LANES), F32)],
        compiler_params=_cp("arbitrary"), name="final_loss")(x, gvec, target)


def mods_project(c_all, w, b):
    B, D = c_all.shape
    L, _, N = w.shape

    def body(c_ref, w_ref, b_ref, o_ref):
        cv = c_ref[...]
        cond = cv * _sigmoid(cv)
        o_ref[0] = jnp.dot(cond, w_ref[0], preferred_element_type=F32, precision=lax.Precision.HIGHEST) + b_ref[0]

    return pl.pallas_call(
        body, grid=(L,),
        in_specs=[pl.BlockSpec((B, D), lambda l: (0, 0)), pl.BlockSpec((1, D, N), lambda l: (l, 0, 0)),
                  pl.BlockSpec((1, 1, N), lambda l: (l, 0, 0))],
        out_specs=pl.BlockSpec((1, B, N), lambda l: (l, 0, 0)),
        out_shape=_sds((L, B, N), F32),
        compiler_params=_cp("arbitrary"), name="mods_project")(c_all, w, b)


def mods_weight_grad(c_all, dm):
    B, D = c_all.shape
    L, _, N = dm.shape

    def body(c_ref, d_ref, o_ref):
        cv = c_ref[...]
        cond = cv * _sigmoid(cv)
        o_ref[0] = lax.dot_general(cond, d_ref[0], (((0,), (0,)), ((), ())), preferred_element_type=F32,
                                   precision=lax.Precision.HIGHEST)

    return pl.pallas_call(
        body, grid=(L,),
        in_specs=[pl.BlockSpec((B, D), lambda l: (0, 0)), pl.BlockSpec((1, B, N), lambda l: (l, 0, 0))],
        out_specs=pl.BlockSpec((1, D, N), lambda l: (l, 0, 0)),
        out_shape=_sds((L, D, N), F32),
        compiler_params=_cp("arbitrary"), name="mods_weight_grad")(c_all, dm)


def _adam_math(g, w, m, v):
    m2 = ADAM_B1 * m + (1.0 - ADAM_B1) * g
    v2 = ADAM_B2 * v + (1.0 - ADAM_B2) * (g * g)
    m_hat = m2 / (1.0 - ADAM_B1 ** ADAM_STEP)
    v_hat = v2 / (1.0 - ADAM_B2 ** ADAM_STEP)
    delta = -ADAM_LR * (m_hat / (jnp.sqrt(v_hat) + ADAM_EPS) + ADAM_WD * w)
    return delta, m2, v2


def adam_update(g, w, m, v, parts, name):
    R, C = w.shape
    tr = _pick(R, 256, 8)

    def body(g_ref, w_ref, m_ref, v_ref, go_ref, d_ref, mo_ref, vo_ref):
        if parts:
            gv = g_ref[0].astype(F32)
            for s in range(1, N_DEV):
                gv = gv + g_ref[s].astype(F32)
        else:
            gv = g_ref[...]
        go_ref[...] = gv
        d_ref[...], mo_ref[...], vo_ref[...] = _adam_math(gv, w_ref[...], m_ref[...], v_ref[...])

    gspec = pl.BlockSpec((N_DEV, tr, C), lambda i: (0, i, 0)) if parts else _tok(tr, C)
    return pl.pallas_call(
        body, grid=(R // tr,),
        in_specs=[gspec, _tok(tr, C), _tok(tr, C), _tok(tr, C)],
        out_specs=[_tok(tr, C)] * 4, out_shape=[_sds((R, C), F32)] * 4,
        compiler_params=_cp("arbitrary"), name=name)(g, w, m, v)


def adam_layer(parts, w, m, v, prev, layer, after, name):
    L, R, C = w.shape
    tr = _pick(R, 256, 8)
    prev = (list(prev) if prev is not None else []) + [after]

    def body(p_ref, w_ref, m_ref, v_ref, *rest):
        go_ref, d_ref, mo_ref, vo_ref = rest[-4:]
        gv = p_ref[0].astype(F32)
        for s in range(1, N_DEV):
            gv = gv + p_ref[s].astype(F32)
        go_ref[...] = gv
        d_ref[...], mo_ref[...], vo_ref[...] = _adam_math(gv, w_ref[...], m_ref[...], v_ref[...])

    lay = pl.BlockSpec((None, tr, C), lambda i: (layer, i, 0))
    return pl.pallas_call(
        body, grid=(R // tr,),
        in_specs=[pl.BlockSpec((N_DEV, tr, C), lambda i: (0, i, 0)), lay, lay, lay] + [pl.BlockSpec(memory_space=pl.ANY)] * len(prev),
        out_specs=[lay] * 4, out_shape=[_sds((L, R, C), F32)] * 4,
        input_output_aliases={4 + k: k for k in range(len(prev) - 1)},
        compiler_params=_cp("arbitrary"), name=name)(parts, w, m, v, *prev)


def _my_id():
    return 4 * lax.axis_index("x") + 2 * lax.axis_index("y") + lax.axis_index("c")


def _peer(s):
    x, y, c = lax.axis_index("x"), lax.axis_index("y"), lax.axis_index("c")
    px = (1 - x) if s & 4 else x
    py = (1 - y) if s & 2 else y
    pc = (1 - c) if s & 1 else c
    return (px, py, pc), 4 * px + 2 * py + pc


def all_gather(xs, space, name):
    na = len(xs)

    def body(*refs):
        x_refs, o_refs = refs[:na], refs[na:2 * na]
        send_sems, recv_sems, local_sems = refs[2 * na:]
        me = _my_id()
        locals_, sends = [], []
        for a in range(na):
            cp = pltpu.make_async_copy(x_refs[a], o_refs[a].at[me], local_sems.at[a])
            cp.start()
            locals_.append(cp)
        for s in range(1, N_DEV):
            peer, _ = _peer(s)
            for a in range(na):
                cp = pltpu.make_async_remote_copy(
                    src_ref=x_refs[a], dst_ref=o_refs[a].at[me], send_sem=send_sems.at[a, s - 1],
                    recv_sem=recv_sems.at[a, s - 1], device_id=peer, device_id_type=MESH)
                cp.start()
                sends.append(cp)
        for s in range(1, N_DEV):
            peer, pid = _peer(s)
            for a in range(na):
                pltpu.make_async_remote_copy(
                    src_ref=x_refs[a], dst_ref=o_refs[a].at[pid], send_sem=send_sems.at[a, s - 1],
                    recv_sem=recv_sems.at[a, s - 1], device_id=peer, device_id_type=MESH).wait_recv()
        for cp in sends:
            cp.wait_send()
        for cp in locals_:
            cp.wait()

    spec = pl.BlockSpec(memory_space=space)
    return pl.pallas_call(
        body, in_specs=[spec] * na, out_specs=[spec] * na,
        out_shape=[_sds((N_DEV,) + x.shape, x.dtype) for x in xs],
        scratch_shapes=[pltpu.SemaphoreType.DMA((na, N_DEV - 1)), pltpu.SemaphoreType.DMA((na, N_DEV - 1)),
                        pltpu.SemaphoreType.DMA((na,))],
        compiler_params=pltpu.CompilerParams(vmem_limit_bytes=VMEM_LIMIT), name=name)(*xs)


def exchange_slots(xs, name):
    na = len(xs)

    def body(*refs):
        x_refs, o_refs = refs[:na], refs[na:2 * na]
        send_sems, recv_sems, local_sems = refs[2 * na:]
        me = _my_id()
        locals_, sends = [], []
        for a in range(na):
            cp = pltpu.make_async_copy(x_refs[a].at[me], o_refs[a].at[me], local_sems.at[a])
            cp.start()
            locals_.append(cp)
        for s in range(1, N_DEV):
            peer, pid = _peer(s)
            for a in range(na):
                cp = pltpu.make_async_remote_copy(
                    src_ref=x_refs[a].at[pid], dst_ref=o_refs[a].at[me], send_sem=send_sems.at[a, s - 1],
                    recv_sem=recv_sems.at[a, s - 1], device_id=peer, device_id_type=MESH)
                cp.start()
                sends.append(cp)
        for s in range(1, N_DEV):
            peer, pid = _peer(s)
            for a in range(na):
                pltpu.make_async_remote_copy(
                    src_ref=x_refs[a].at[pid], dst_ref=o_refs[a].at[pid], send_sem=send_sems.at[a, s - 1],
                    recv_sem=recv_sems.at[a, s - 1], device_id=peer, device_id_type=MESH).wait_recv()
        for cp in sends:
            cp.wait_send()
        for cp in locals_:
            cp.wait()

    spec = pl.BlockSpec(memory_space=pl.ANY)
    return pl.pallas_call(
        body, in_specs=[spec] * na, out_specs=[spec] * na,
        out_shape=[_sds(x.shape, x.dtype) for x in xs],
        scratch_shapes=[pltpu.SemaphoreType.DMA((na, N_DEV - 1)), pltpu.SemaphoreType.DMA((na, N_DEV - 1)),
                        pltpu.SemaphoreType.DMA((na,))],
        compiler_params=pltpu.CompilerParams(vmem_limit_bytes=VMEM_LIMIT), name=name)(*xs)


_HBM = pl.BlockSpec(memory_space=pltpu.HBM)
_SEM = pl.BlockSpec(memory_space=pltpu.SEMAPHORE)
_EFFECT = pltpu.SideEffectType.DATAFLOW_SIDE_EFFECTING


def _split_copy(x_ref, land_ref, s, send_sem, recv_sem, scatter):
    peer, pid = _peer(s)
    src = x_ref.at[pid] if scatter else x_ref
    return pltpu.make_async_remote_copy(src_ref=src, dst_ref=land_ref.at[_my_id()], send_sem=send_sem, recv_sem=recv_sem,
                                        device_id=peer, device_id_type=MESH)


def comm_start(xs, scatter, after, name):
    na = len(xs)
    extra = [] if after is None else [after]
    me = _my_id()
    lands = []
    for x in xs:
        shape = x.shape if scatter else (N_DEV,) + x.shape
        own = lax.dynamic_slice_in_dim(x, me, 1, 0) if scatter else x[None]
        lands.append(lax.dynamic_update_slice(lax.empty(shape, x.dtype), own, (me,) + (0,) * (len(shape) - 1)))

    def body(*refs):
        x_refs, land_refs = refs[:na], refs[na:2 * na]
        send_sem, recv_sem = refs[2 * na + len(extra)], refs[2 * na + len(extra) + 1]
        token = refs[-1]
        for s in range(1, N_DEV):
            for a in range(na):
                _split_copy(x_refs[a], land_refs[a], s, send_sem, recv_sem, scatter).start()
        token[...] = jnp.zeros_like(token)

    outs = pl.pallas_call(
        body, name=name,
        out_shape=(pltpu.SemaphoreType.DMA(()), pltpu.SemaphoreType.DMA(()))
        + tuple(pltpu.HBM(x.shape, x.dtype) for x in xs) + tuple(pltpu.HBM(l.shape, l.dtype) for l in lands)
        + (_sds((8, LANES), F32),),
        in_specs=(_HBM,) * (2 * na) + (pl.BlockSpec(memory_space=pl.ANY),) * len(extra),
        out_specs=(_SEM, _SEM) + (_HBM,) * (2 * na) + (pl.BlockSpec(memory_space=pltpu.VMEM),),
        input_output_aliases={a: 2 + a for a in range(2 * na)},
        compiler_params=pltpu.CompilerParams(has_side_effects=_EFFECT),
    )(*[pltpu.with_memory_space_constraint(x, pltpu.HBM) for x in xs],
      *[pltpu.with_memory_space_constraint(l, pltpu.HBM) for l in lands], *extra)
    return dict(sems=outs[0:2], xs=outs[2:2 + na], lands=outs[2 + na:2 + 2 * na], token=outs[-1], scatter=scatter)


def comm_wait(started, after, name):
    xs, lands = started["xs"], started["lands"]
    scatter = started["scatter"]
    na = len(xs)

    def body(*refs):
        x_refs, land_refs = refs[:na], refs[na:2 * na]
        send_sem, recv_sem = refs[2 * na], refs[2 * na + 1]
        for s in range(1, N_DEV):
            for a in range(na):
                cp = _split_copy(x_refs[a], land_refs[a], s, send_sem, recv_sem, scatter)
                cp.wait_send()
                cp.wait_recv()

    outs = pl.pallas_call(
        body, name=name,
        out_shape=tuple(pltpu.HBM(x.shape, x.dtype) for x in xs) + tuple(pltpu.HBM(l.shape, l.dtype) for l in lands),
        in_specs=(_HBM,) * (2 * na) + (_SEM, _SEM, pl.BlockSpec(memory_space=pl.ANY)),
        out_specs=(_HBM,) * (2 * na),
        input_output_aliases={a: a for a in range(2 * na)},
        compiler_params=pltpu.CompilerParams(has_side_effects=_EFFECT),
    )(*xs, *lands, *started["sems"], after)
    return list(outs[na:])


def _cols_to_natural(g):
    return jnp.concatenate([g[k] for k in range(N_DEV)], axis=1)


def _cols_to_slots(w):
    ns = w.shape[1] // N_DEV
    return jnp.stack([w[:, k * ns:(k + 1) * ns] for k in range(N_DEV)])


def _vec8(rows, d):
    rows = [r.reshape(1, d).astype(F32) for r in rows]
    return jnp.concatenate(rows + [jnp.zeros((8 - len(rows), d), F32)], axis=0)


def _ffn_forward(x, vec, w_in_t, w_out):
    h, a, b, u = ffn_up(x, vec, w_in_t)
    xn, y = proj_out(u, x, vec, w_out, FFN_RES_WEIGHT, "ffn_down")
    return xn, (x, h, a, b, u, y)


def _ffn_backward(dxo, saved, vec, w_in_t, w_out):
    x, h, a, b, u, y = saved
    dy, dab, part_gate = ffn_down_bwd(dxo, y, vec, w_out, a, b)
    dx, part_norm = ffn_up_bwd(dab, w_in_t, x, dxo, vec)
    g_out = grad_slots(u, dy, "ffn_dw_out")
    g_in_t = grad_slots(dab, h, "ffn_dw_in")
    rows = jnp.concatenate([part_norm[0:3], part_gate[0:1]], axis=0)
    return dx, g_in_t, g_out, rows


_TRANSPOSED = ("ffn1_w_in", "ffn2_w_in", "attn_w_q")
_COL_NATURAL = ("conv_w_in", "w_kv", "attn_w_o")
_ROW_SHARDED = ("ffn1_w_out", "ffn2_w_out", "conv_w_out")
_BIG = _TRANSPOSED + _COL_NATURAL + _ROW_SHARDED


def weight_chunks():
    chunks = []
    for layer in range(DEPTH):
        first = [("ffn1_w_in", layer), ("ffn1_w_out", layer)]
        if layer == N_A_LAYERS:
            first = [("w_kv", layer)] + first
        mixer = [("conv_w_in", layer), ("conv_w_out", layer)] if layer < N_A_LAYERS else [("attn_w_q", layer), ("attn_w_o", layer)]
        rest = mixer + [("ffn2_w_in", layer), ("ffn2_w_out", layer)]
        chunks += [first, rest] if layer == 0 else [first + rest]
    return chunks


def stacked_index(name, layer):
    if name == "w_kv":
        return None
    return layer - N_A_LAYERS if name.startswith("attn") else layer


class ChunkComm:
    def __init__(self, shards):
        self.shards = shards
        self.chunks = weight_chunks()

    def _shard(self, name, layer):
        idx = stacked_index(name, layer)
        return self.shards[name][0 if idx is None else idx]

    def start_gather(self, ci, after):
        xs = [self._shard(n, l).astype(BF16) for n, l in self.chunks[ci]]
        return comm_start(xs, False, after, f"gather_start_{ci}")

    def finish_gather(self, ci, started, after):
        lands = comm_wait(started, after, f"gather_wait_{ci}")
        W = {}
        for key, g in zip(self.chunks[ci], lands):
            W[key] = _cols_to_natural(g) if key[0] in _COL_NATURAL else g.reshape(-1, g.shape[2])
        return W, lands[0]

    def start_exchange(self, ci, slots, after):
        return comm_start([slots[key] for key in self.chunks[ci]], True, after, f"exchange_start_{ci}")

    def finish_exchange(self, ci, started, after):
        lands = comm_wait(started, after, f"exchange_wait_{ci}")
        return dict(zip(self.chunks[ci], lands))


def device_step(x, positions, target, mods, kvmods, small, comm, gather0):
    T, D = x.shape
    groups = DILATED_GROUPS
    dils = [dil for _, dil in groups]
    lane = jnp.arange(LANES) % HEAD_DIM
    inv = ROPE_THETA ** (-jnp.arange(0, ROPE_DIM, 2, dtype=F32) / ROPE_DIM)
    lane_rows = _vec8([jnp.where(lane < ROPE_DIM, inv[lane % (ROPE_DIM // 2)], 0.0), lane < ROPE_DIM,
                       (lane >= ROPE_DIM // 2) & (lane < ROPE_DIM), lane < ROPE_DIM // 2], LANES)
    tabs = rope_tables(positions.reshape(T, 1), lane_rows)

    def after_token(v, token):
        return v if token is None else v + token[0, 0]

    def vec_of(layer, sub, token=None):
        return after_token(_vec8([small["norm_g"][layer, sub], mods[layer, 3 * sub], mods[layer, 3 * sub + 1],
                                  mods[layer, 3 * sub + 2]], D), token)

    saved = []
    kv_saved = None
    k_sh = v_sh = None
    qw = GROUP_WIDTH * len(groups)
    chunk_of = {key: ci for ci, chunk in enumerate(comm.chunks) for key in chunk}
    W = {}
    flight = {"ci": 0, "started": gather0, "token": None}

    def need(key, after):
        if key not in W:
            ci = chunk_of[key]
            assert ci == flight["ci"], (key, ci)
            got, landed = comm.finish_gather(ci, flight["started"], after)
            W.update(got)
            if ci + 1 < len(comm.chunks):
                flight.update(ci=ci + 1, started=comm.start_gather(ci + 1, landed))
                flight["token"] = flight["started"]["token"]
        return W[key]

    def behind_start(v):
        token, flight["token"] = flight["token"], None
        return after_token(v, token)

    for layer in range(DEPTH):
        if layer == N_A_LAYERS:
            w_kv = need(("w_kv", layer), x)
            kv_vec = behind_start(_vec8([small["kv_norm_g"], kvmods[0], kvmods[1]], D))
            h_kv, *kv_pieces = proj_rope_fwd(x, kv_vec, w_kv, tabs, qw, False, dils, "kv_fwd")
            k_sh, v_sh = kv_pieces[:len(groups)], kv_pieces[len(groups):]
            kv_saved = (x, h_kv, kv_vec)
        rec = {}
        w_in, w_out = need(("ffn1_w_in", layer), x), need(("ffn1_w_out", layer), x)
        v1 = behind_start(vec_of(layer, 0))
        x, rec["ffn1"] = _ffn_forward(x, v1, w_in, w_out)
        if layer < N_A_LAYERS:
            w_in, w_out = need(("conv_w_in", layer), x), need(("conv_w_out", layer), x)
            v2 = behind_start(vec_of(layer, 1))
            cw = _vec8(list(small["conv_w"][layer]), D)
            x_in = x
            x, h, bcu, cv, z, y = conv_fwd(x, v2, cw, w_in, w_out)
            rec["mix"] = (x_in, h, bcu, cv, z, y, cw)
        else:
            w_q, w_o = need(("attn_w_q", layer), x), need(("attn_w_o", layer), x)
            v2 = behind_start(vec_of(layer, 1))
            x_in = x
            h, *q = proj_rope_fwd(x, v2, w_q, tabs, qw, True, dils, "q_fwd")
            os_, ls = [], []
            for g, (win, dil) in enumerate(groups):
                o, l = attn_core_fwd(q[g], k_sh[g], v_sh[g], g, win // dil)
                os_.append(o)
                ls.append(l)
            x, mixed, y = attn_mix_out(os_, ls, dils, x, v2, w_o)
            rec["mix"] = (x_in, h, q, os_, ls, mixed, y)
        w_in, w_out = need(("ffn2_w_in", layer), x), need(("ffn2_w_out", layer), x)
        v3 = behind_start(vec_of(layer, 2))
        x, rec["ffn2"] = _ffn_forward(x, v3, w_in, w_out)
        rec["vecs"] = (v1, v2, v3)
        saved.append(rec)

    dx, part_final, loss_tile = final_loss(x, _vec8([small["final_norm_g"]], D), target)
    loss = loss_tile[0, 0]

    conv_rows = [None] * N_A_LAYERS
    kv_rows = None
    mod_rows = [[None] * 3 for _ in range(DEPTH)]
    dkv_pairs = [{"k": [], "v": []} for _ in groups]
    slots = {}
    exchanges = []
    token = None

    def send_ready_chunks():
        nonlocal token
        for ci in reversed(range(len(comm.chunks))):
            if ci not in [e[0] for e in exchanges] and all(key in slots for key in comm.chunks[ci]):
                started = comm.start_exchange(ci, slots, token)
                exchanges.append((ci, started))
                token = started["token"]

    for layer in reversed(range(DEPTH)):
        rec = saved[layer]
        v1, v2, v3 = rec["vecs"]
        dx, slots[("ffn2_w_in", layer)], slots[("ffn2_w_out", layer)], mod_rows[layer][2] = _ffn_backward(
            dx, rec["ffn2"], after_token(v3, token), W[("ffn2_w_in", layer)], W[("ffn2_w_out", layer)])
        if layer < N_A_LAYERS:
            x_in, h, bcu, cv, z, y, cw = rec["mix"]
            dx, dy, dbcu, part, dcw = conv_bwd(dx, x_in, y, bcu, cv, v2, cw, W[("conv_w_in", layer)], W[("conv_w_out", layer)])
            slots[("conv_w_out", layer)] = grad_slots(z, dy, "conv_dw_out")
            slots[("conv_w_in", layer)] = grad_slots(h, dbcu, "conv_dw_in", col_slots=True)
            conv_rows[layer] = dcw[0:3]
            mod_rows[layer][1] = part[0:4]
        else:
            x_in, h, q, os_, ls, mixed, y = rec["mix"]
            outs = attn_mix_bwd(dx, y, v2, W[("attn_w_o", layer)], os_, ls, dils)
            ng = len(groups)
            dy, dos, rrs, part_gate = outs[0], outs[1:1 + ng], outs[1 + ng:1 + 2 * ng], outs[1 + 2 * ng]
            slots[("attn_w_o", layer)] = grad_slots(mixed, dy, "attn_dw_o", col_slots=True)
            dqs = []
            for g, (win, dil) in enumerate(groups):
                dq, dkc, dkp, dvc, dvp = attn_core_bwd(q[g], k_sh[g], v_sh[g], dos[g], rrs[g], ls[g], g, win // dil)
                dqs.append(dq)
                dkv_pairs[g]["k"].append((dkc, dkp))
                dkv_pairs[g]["v"].append((dvc, dvp))
            dx, dqr, part_norm = proj_rope_bwd(dqs, dils, x_in, dx, v2, W[("attn_w_q", layer)], tabs, qw, True, "q_bwd")
            slots[("attn_w_q", layer)] = grad_slots(dqr, h, "attn_dw_q")
            mod_rows[layer][1] = jnp.concatenate([part_norm[0:3], part_gate[0:1]], axis=0)
        send_ready_chunks()
        dx, slots[("ffn1_w_in", layer)], slots[("ffn1_w_out", layer)], mod_rows[layer][0] = _ffn_backward(
            dx, rec["ffn1"], after_token(v1, token), W[("ffn1_w_in", layer)], W[("ffn1_w_out", layer)])
        if layer == N_A_LAYERS:
            x_kv, h_kv, kv_vec = kv_saved
            dparts = [dkv_combine(dkv_pairs[g]["k"], win // dil, f"dk_combine_g{g}") for g, (win, dil) in enumerate(groups)]
            dparts += [dkv_combine(dkv_pairs[g]["v"], win // dil, f"dv_combine_g{g}") for g, (win, dil) in enumerate(groups)]
            dx, dkvp, part_kv = proj_rope_bwd(dparts, dils, x_kv, dx, kv_vec, W[("w_kv", layer)], tabs, qw, False, "kv_bwd")
            slots[("w_kv", layer)] = grad_slots(h_kv, dkvp, "kv_dw", col_slots=True)
            kv_rows = part_kv[0:3]
        send_ready_chunks()

    grads = {"conv_w": jnp.stack(conv_rows), "kv_rows": kv_rows, "exchanges": exchanges}
    grads["final_norm_g"] = part_final[0]
    rows = jnp.stack([jnp.stack(r) for r in mod_rows])
    grads["norm_g"] = rows[:, :, 0]
    grads["mods"] = rows[:, :, 1:4].reshape(DEPTH, N_MOD, D)
    return loss, dx, grads


def _flat2(a):
    return a.reshape(-1, a.shape[-1])


def _pad_rows(a, mult):
    r = a.shape[0]
    pad = (-r) % mult
    return a if pad == 0 else jnp.concatenate([a, jnp.zeros((pad,) + a.shape[1:], a.dtype)], axis=0)


def kernel(x, c, positions, norm_g, ada_w, ada_b, ffn1_w_in, ffn1_w_out, ffn2_w_in, ffn2_w_out, conv_w_in, conv_w, conv_w_out, kv_norm_g, kv_ada_w, kv_ada_b, w_kv, attn_w_q, attn_w_o, final_norm_g, loss_target, m_norm_g, m_ada_w, m_ada_b, m_ffn1_w_in, m_ffn1_w_out, m_ffn2_w_in, m_ffn2_w_out, m_conv_w_in, m_conv_w, m_conv_w_out, m_kv_norm_g, m_kv_ada_w, m_kv_ada_b, m_w_kv, m_attn_w_q, m_attn_w_o, m_final_norm_g, v_norm_g, v_ada_w, v_ada_b, v_ffn1_w_in, v_ffn1_w_out, v_ffn2_w_in, v_ffn2_w_out, v_conv_w_in, v_conv_w, v_conv_w_out, v_kv_norm_g, v_kv_ada_w, v_kv_ada_b, v_w_kv, v_attn_w_q, v_attn_w_o, v_final_norm_g):
    names = ("norm_g", "ada_w", "ada_b", "ffn1_w_in", "ffn1_w_out", "ffn2_w_in", "ffn2_w_out", "conv_w_in", "conv_w",
             "conv_w_out", "kv_norm_g", "kv_ada_w", "kv_ada_b", "w_kv", "attn_w_q", "attn_w_o", "final_norm_g")
    wts = dict(zip(names, (norm_g, ada_w, ada_b, ffn1_w_in, ffn1_w_out, ffn2_w_in, ffn2_w_out, conv_w_in, conv_w, conv_w_out,
                           kv_norm_g, kv_ada_w, kv_ada_b, w_kv, attn_w_q, attn_w_o, final_norm_g)))
    mom = dict(zip(names, (m_norm_g, m_ada_w, m_ada_b, m_ffn1_w_in, m_ffn1_w_out, m_ffn2_w_in, m_ffn2_w_out, m_conv_w_in,
                           m_conv_w, m_conv_w_out, m_kv_norm_g, m_kv_ada_w, m_kv_ada_b, m_w_kv, m_attn_w_q, m_attn_w_o,
                           m_final_norm_g)))
    var = dict(zip(names, (v_norm_g, v_ada_w, v_ada_b, v_ffn1_w_in, v_ffn1_w_out, v_ffn2_w_in, v_ffn2_w_out, v_conv_w_in,
                           v_conv_w, v_conv_w_out, v_kv_norm_g, v_kv_ada_w, v_kv_ada_b, v_w_kv, v_attn_w_q, v_attn_w_o,
                           v_final_norm_g)))
    T, D = x.shape[1], x.shape[2]
    me = _my_id()
    nmod = ada_w.shape[2]
    nkv = kv_ada_w.shape[1]

    def stacked(w, n):
        w = w if w.ndim == 3 else w[None]
        return jnp.swapaxes(w, 1, 2) if n in _TRANSPOSED else w

    comm = ChunkComm({n: stacked(wts[n], n) for n in _BIG})
    W = {}

    ds = norm_g.shape[2]
    small = jnp.concatenate([c.reshape(-1), norm_g.reshape(-1), conv_w.reshape(-1)]).astype(F32)
    n_small = small.shape[0]
    small = _pad_rows(small.reshape(-1, 1), 8 * LANES).reshape(-1, LANES)
    (small_all,) = all_gather([small], pltpu.VMEM, "gather_small")
    small_all = small_all.reshape(N_DEV, -1)[:, :n_small]
    c_all = small_all[:, :D]
    def full_rows(off, count):
        return jnp.stack([small_all[:, off + i * ds:off + (i + 1) * ds].reshape(D) for i in range(count)])

    W["norm_g"] = full_rows(D, DEPTH * 3).reshape(DEPTH, 3, D)
    W["conv_w"] = full_rows(D + DEPTH * 3 * ds, N_A_LAYERS * 3).reshape(N_A_LAYERS, 3, D)
    W["kv_norm_g"], W["final_norm_g"] = kv_norm_g, final_norm_g

    ada_b_mine = lax.dynamic_slice_in_dim(ada_b, me * nmod, nmod, axis=1).reshape(DEPTH, 1, nmod)
    kv_b_mine = lax.dynamic_slice_in_dim(kv_ada_b, me * nkv, nkv, axis=0).reshape(1, 1, nkv)
    mods_cols = mods_project(c_all, ada_w, ada_b_mine)
    kv_cols = mods_project(c_all, kv_ada_w.reshape(1, D, nkv), kv_b_mine)
    mcat = jnp.concatenate([mods_cols[l] for l in range(DEPTH)] + [kv_cols[0]], axis=1)
    wm = mcat.shape[1]
    if wm % LANES:
        mcat = jnp.concatenate([mcat, jnp.zeros((N_DEV, LANES - wm % LANES), F32)], axis=1)
    (mods_all,) = exchange_slots([mcat.reshape(N_DEV, 1, -1)], "exchange_mods")
    gather0 = comm.start_gather(0, mods_all)
    mods_all = mods_all.reshape(N_DEV, -1)
    mods = jnp.stack([mods_all[:, l * nmod:(l + 1) * nmod].reshape(N_MOD, D) for l in range(DEPTH)])
    kvmods = mods_all[:, DEPTH * nmod:DEPTH * nmod + nkv].reshape(2, D)

    loss_local, dx, grads = device_step(x[0], positions[0], loss_target[0], mods, kvmods, W, comm, gather0)
    loss = lax.psum(loss_local, MESH_AXES)

    dmods = grads["mods"].reshape(-1)
    dkvm = grads["kv_rows"][1:3].reshape(-1)
    vecs = jnp.concatenate([dmods, dkvm, grads["kv_rows"][0], grads["final_norm_g"], grads["norm_g"].reshape(-1),
                            grads["conv_w"].reshape(-1)])
    n_vec = vecs.shape[0]
    vecs = _pad_rows(vecs.reshape(-1, 1), 8 * LANES).reshape(-1, LANES)
    (vec_all,) = all_gather([vecs], pltpu.VMEM, "gather_vector_grads")
    vec_all = vec_all.reshape(N_DEV, -1)[:, :n_vec]
    nm_, nk_ = DEPTH * N_MOD * D, 2 * D
    dmods_all = vec_all[:, :nm_].reshape(N_DEV, DEPTH, N_MOD * D)
    dkvm_all = vec_all[:, nm_:nm_ + nk_]
    rest = vec_all[:, nm_ + nk_:]
    parts_kv_norm, parts_final = rest[:, :D].reshape(N_DEV, 1, D), rest[:, D:2 * D].reshape(N_DEV, 1, D)
    parts_norm = lax.dynamic_slice_in_dim(rest[:, 2 * D:2 * D + DEPTH * 3 * D].reshape(N_DEV, DEPTH * 3, D), me * ds, ds, axis=2)
    parts_conv = lax.dynamic_slice_in_dim(rest[:, 2 * D + DEPTH * 3 * D:].reshape(N_DEV, N_A_LAYERS * 3, D), me * ds, ds, axis=2)
    dm_cols = lax.dynamic_slice_in_dim(dmods_all, me * nmod, nmod, axis=2)
    dm_mine = jnp.stack([dm_cols[:, l] for l in range(DEPTH)])
    dkv_mine = lax.dynamic_slice_in_dim(dkvm_all, me * nkv, nkv, axis=1).reshape(1, N_DEV, nkv)
    g_ada_w = mods_weight_grad(c_all, dm_mine)
    g_kv_ada_w = mods_weight_grad(c_all, dkv_mine)[0]

    out_g, out_d, out_m, out_v = {}, {}, {}, {}

    def update(n, g, w, parts=False):
        shp = w.shape
        w2 = w.reshape(1, -1) if w.ndim == 1 else _flat2(w)
        g2 = g if parts else g.reshape(w2.shape)
        res = adam_update(g2, w2, mom[n].reshape(w2.shape), var[n].reshape(w2.shape), parts, "adam_" + n)
        out_g[n], out_d[n], out_m[n], out_v[n] = (r.reshape(shp) for r in res)

    moms = {n: stacked(mom[n], n) for n in _BIG}
    vars_ = {n: stacked(var[n], n) for n in _BIG}
    results = {}
    after = dx
    for ci, started in grads["exchanges"]:
        for (n, layer), parts in comm.finish_exchange(ci, started, after).items():
            idx = stacked_index(n, layer)
            results[n] = adam_layer(parts, comm.shards[n], moms[n], vars_[n], results.get(n), 0 if idx is None else idx,
                                    after, f"adam_{n}_{layer}")
            after = results[n][1]
    for n in _BIG:
        res = [jnp.swapaxes(r, 1, 2) if n in _TRANSPOSED else r for r in results[n]]
        out_g[n], out_d[n], out_m[n], out_v[n] = (r.reshape(wts[n].shape) for r in res)
    update("ada_w", g_ada_w, ada_w)
    update("kv_ada_w", g_kv_ada_w, kv_ada_w)
    update("ada_b", dmods_all, ada_b, True)
    update("kv_ada_b", dkvm_all.reshape(N_DEV, 1, nk_), kv_ada_b, True)
    update("kv_norm_g", parts_kv_norm, kv_norm_g, True)
    update("final_norm_g", parts_final, final_norm_g, True)
    update("norm_g", parts_norm, norm_g, True)
    update("conv_w", parts_conv, conv_w, True)

    return (loss, dx.reshape(x.shape), *[out_g[n] for n in names], *[out_d[n] for n in names],
            *[out_m[n] for n in names], *[out_v[n] for n in names])
```

```python
import functools

import jax
import jax.numpy as jnp
from jax import lax
from jax.experimental import pallas as pl
from jax.experimental.pallas import tpu as pltpu

F32, BF16 = jnp.float32, jnp.bfloat16

N_DEV = 8
MESH_AXES = ("x", "y", "c")
DEPTH = 4
N_A_LAYERS = 2
HEAD_DIM = 64
HEADS_PER_GROUP = 8
GROUP_WIDTH = HEAD_DIM * HEADS_PER_GROUP
DILATED_GROUPS = ((128, 1), (512, 4), (2048, 16))
ROPE_DIM = HEAD_DIM // 4
ROPE_THETA = 500000.0
NORM_EPS = 1e-5
FFN_RES_WEIGHT = 0.5
N_MOD = 9
ADAM_LR, ADAM_B1, ADAM_B2, ADAM_EPS, ADAM_WD, ADAM_STEP = 0.001, 0.9, 0.999, 1e-08, 0.01, 10

LANES = 128
TOKEN_TILE = 512
CONTRACT_TILE = 2048
MXU_WIDTH = 256
VMEM_LIMIT = 56 * 1024 * 1024
MESH = pl.DeviceIdType.MESH


def _cp(*sem):
    return pltpu.CompilerParams(dimension_semantics=sem, vmem_limit_bytes=VMEM_LIMIT)


def _pick(n, cap, mult=LANES):
    if n <= cap:
        return n
    best = None
    for t in range(mult, cap + 1, mult):
        if n % t == 0:
            best = t
    assert best is not None, (n, cap)
    return best


def _tok(tm, w):
    return pl.BlockSpec((tm, w), lambda i: (i, 0))


def _res(shape):
    nd = len(shape)
    return pl.BlockSpec(shape, lambda *_: (0,) * nd, pipeline_mode=pl.Buffered(1))


def _sds(shape, dt):
    return jax.ShapeDtypeStruct(shape, dt)


def _sigmoid(a):
    return 1.0 / (1.0 + jnp.exp(-a))


def _modnorm(x, g, sh, sc):
    r = lax.rsqrt(jnp.mean(x * x, axis=-1, keepdims=True) + NORM_EPS)
    return (x * r * g) * (1.0 + sc) + sh


def _dot(a, b):
    return jnp.dot(a, b, preferred_element_type=F32)


def _dot_nt(a, b):
    return lax.dot_general(a, b, (((1,), (1,)), ((), ())), preferred_element_type=F32)


def _dot_tn(a, b):
    return lax.dot_general(a, b, (((0,), (0,)), ((), ())), preferred_element_type=F32)


def _rows8(rows, d):
    pad = 8 - len(rows)
    return jnp.concatenate(list(rows) + [jnp.zeros((pad, d), F32)], axis=0)


def _acc_rows(ref, tile, first):
    @pl.when(first)
    def _():
        ref[...] = tile

    @pl.when(jnp.logical_not(first))
    def _():
        ref[...] += tile


def ffn_up(x, vec, w_in_t):
    T, D = x.shape
    F = w_in_t.shape[0] // 2
    tm, cw = min(TOKEN_TILE, T), _pick(F, MXU_WIDTH)

    def body(x_ref, vec_ref, w_ref, h_ref, ga_ref, gb_ref, u_ref):
        hb = _modnorm(x_ref[...], vec_ref[0:1], vec_ref[1:2], vec_ref[2:3]).astype(BF16)
        h_ref[...] = hb
        for c in range(F // cw):
            lo, hi = c * cw, (c + 1) * cw
            a = _dot_nt(hb, w_ref[lo:hi, :])
            b = _dot_nt(hb, w_ref[F + lo:F + hi, :])
            sg = _sigmoid(a)
            silu = a * sg
            ga_ref[:, lo:hi] = (b * (sg + silu * (1.0 - sg))).astype(BF16)
            gb_ref[:, lo:hi] = silu.astype(BF16)
            u_ref[:, lo:hi] = (silu * b).astype(BF16)

    return pl.pallas_call(
        body, grid=(T // tm,),
        in_specs=[_tok(tm, D), _res((8, D)), _res((2 * F, D))],
        out_specs=[_tok(tm, D), _tok(tm, F), _tok(tm, F), _tok(tm, F)],
        out_shape=[_sds((T, D), BF16), _sds((T, F), BF16), _sds((T, F), BF16), _sds((T, F), BF16)],
        compiler_params=_cp("arbitrary"), name="ffn_up")(x, vec, w_in_t)


def proj_out(u, x, vec, w_out, res_weight, name):
    T, D = x.shape
    K = u.shape[1]
    tm = min(TOKEN_TILE, T)

    def body(u_ref, x_ref, vec_ref, w_ref, xn_ref, y_ref):
        y = _dot(u_ref[...], w_ref[...])
        y_ref[...] = y.astype(BF16)
        xn_ref[...] = x_ref[...] + (res_weight * (1.0 + vec_ref[3:4])) * y

    return pl.pallas_call(
        body, grid=(T // tm,),
        in_specs=[_tok(tm, K), _tok(tm, D), _res((8, D)), _res((K, D))],
        out_specs=[_tok(tm, D), _tok(tm, D)],
        out_shape=[_sds((T, D), F32), _sds((T, D), BF16)],
        compiler_params=_cp("arbitrary"), name=name)(u, x, vec, w_out)


def ffn_down_bwd(dxo, y, vec, w_out, a, b):
    T, D = dxo.shape
    F = a.shape[1]
    tm, cw = min(TOKEN_TILE, T), _pick(F, MXU_WIDTH)

    def body(dxo_ref, y_ref, vec_ref, w_ref, a_ref, b_ref, dy_ref, dab_ref, part_ref):
        dxo_t = dxo_ref[...]
        dyb = (dxo_t * (FFN_RES_WEIGHT * (1.0 + vec_ref[3:4]))).astype(BF16)
        dy_ref[...] = dyb
        dgate = FFN_RES_WEIGHT * jnp.sum(dxo_t * y_ref[...].astype(F32), axis=0, keepdims=True)
        _acc_rows(part_ref, _rows8([dgate], D), pl.program_id(0) == 0)
        for c in range(F // cw):
            lo, hi = c * cw, (c + 1) * cw
            du = _dot_nt(dyb, w_ref[lo:hi, :])
            dab_ref[:, lo:hi] = (du * a_ref[:, lo:hi].astype(F32)).astype(BF16)
            dab_ref[:, F + lo:F + hi] = (du * b_ref[:, lo:hi].astype(F32)).astype(BF16)

    return pl.pallas_call(
        body, grid=(T // tm,),
        in_specs=[_tok(tm, D), _tok(tm, D), _res((8, D)), _res((F, D)), _tok(tm, F), _tok(tm, F)],
        out_specs=[_tok(tm, D), _tok(tm, 2 * F), pl.BlockSpec((8, D), lambda i: (0, 0))],
        out_shape=[_sds((T, D), BF16), _sds((T, 2 * F), BF16), _sds((8, D), F32)],
        compiler_params=_cp("arbitrary"), name="ffn_down_bwd")(dxo, y, vec, w_out, a, b)


def ffn_up_bwd(dab, w_in_t, x, dxo, vec):
    T, D = x.shape
    F2 = dab.shape[1]
    tm = min(TOKEN_TILE, T)

    def body(dab_ref, w_ref, x_ref, dxo_ref, vec_ref, dx_ref, part_ref):
        dh = _dot(dab_ref[...], w_ref[...])
        _, vjp = jax.vjp(_modnorm, x_ref[...], vec_ref[0:1], vec_ref[1:2], vec_ref[2:3])
        dx, dg, dsh, dsc = vjp(dh)
        dx_ref[...] = dxo_ref[...] + dx
        _acc_rows(part_ref, _rows8([dg, dsh, dsc], D), pl.program_id(0) == 0)

    return pl.pallas_call(
        body, grid=(T // tm,),
        in_specs=[_tok(tm, F2), _res((F2, D)), _tok(tm, D), _tok(tm, D), _res((8, D))],
        out_specs=[_tok(tm, D), pl.BlockSpec((8, D), lambda i: (0, 0))],
        out_shape=[_sds((T, D), F32), _sds((8, D), F32)],
        compiler_params=_cp("arbitrary"), name="ffn_up_bwd")(dab, w_in_t, x, dxo, vec)


def grad_slots(a, b, name, col_slots=False):
    T, M = a.shape
    N = b.shape[1]
    tk = min(CONTRACT_TILE, T)
    nk = T // tk
    tmm = _pick(M, 1408)
    if col_slots:
        ns = N // N_DEV
        sp = max(s for s in (1, 2, 4, 8) if ns * s <= 1536)
        tn = ns * sp
    else:
        tn = _pick(N, 1536)

    def body(a_ref, b_ref, o_ref, acc):
        k = pl.program_id(2)
        t = _dot_tn(a_ref[...], b_ref[...])

        @pl.when(k == 0)
        def _():
            acc[...] = t

        @pl.when(k > 0)
        def _():
            acc[...] += t

        @pl.when(k == nk - 1)
        def _():
            if col_slots:
                for s in range(sp):
                    o_ref[s] = acc[:, s * ns:(s + 1) * ns].astype(BF16)
            else:
                o_ref[...] = acc[...].astype(BF16)

    if col_slots:
        out_spec, out_shape = pl.BlockSpec((sp, tmm, ns), lambda i, j, k: (j, i, 0)), _sds((N_DEV, M, ns), BF16)
    else:
        out_spec, out_shape = pl.BlockSpec((tmm, tn), lambda i, j, k: (i, j)), _sds((M, N), BF16)
    out = pl.pallas_call(
        body, grid=(M // tmm, N // tn, nk),
        in_specs=[pl.BlockSpec((tk, tmm), lambda i, j, k: (k, i)), pl.BlockSpec((tk, tn), lambda i, j, k: (k, j))],
        out_specs=out_spec, out_shape=out_shape,
        scratch_shapes=[pltpu.VMEM((tmm, tn), F32)],
        compiler_params=_cp("arbitrary", "arbitrary", "arbitrary"), name=name)(a, b)
    return out if col_slots else out.reshape(N_DEV, M // N_DEV, N)


def conv_fwd(x, vec, cw, w_in, w_out):
    T, D = x.shape
    tm = min(TOKEN_TILE, T)

    def body(x_ref, vec_ref, cw_ref, wi_ref, wo_ref, xn_ref, h_ref, bcu_ref, cv_ref, z_ref, y_ref, vbuf):
        @pl.when(pl.program_id(0) == 0)
        def _():
            vbuf[0:8, :] = jnp.zeros((8, D), F32)

        x_t = x_ref[...]
        hb = _modnorm(x_t, vec_ref[0:1], vec_ref[1:2], vec_ref[2:3]).astype(BF16)
        h_ref[...] = hb
        bcu = _dot(hb, wi_ref[...])
        bcu_ref[...] = bcu.astype(BF16)
        bg, v = bcu[:, 0:D], bcu[:, D:2 * D] * bcu[:, 2 * D:3 * D]
        vbuf[8:8 + tm, :] = v
        conv = cw_ref[0:1] * vbuf[6:6 + tm, :] + cw_ref[1:2] * vbuf[7:7 + tm, :] + cw_ref[2:3] * v
        cv_ref[...] = conv.astype(BF16)
        zb = (bg * conv).astype(BF16)
        z_ref[...] = zb
        y = _dot(zb, wo_ref[...])
        y_ref[...] = y.astype(BF16)
        xn_ref[...] = x_t + (1.0 + vec_ref[3:4]) * y
        vbuf[0:8, :] = vbuf[tm:tm + 8, :]

    return pl.pallas_call(
        body, grid=(T // tm,),
        in_specs=[_tok(tm, D), _res((8, D)), _res((8, D)), _res((D, 3 * D)), _res((D, D))],
        out_specs=[_tok(tm, D), _tok(tm, D), _tok(tm, 3 * D), _tok(tm, D), _tok(tm, D), _tok(tm, D)],
        out_shape=[_sds((T, D), F32), _sds((T, D), BF16), _sds((T, 3 * D), BF16), _sds((T, D), BF16),
                   _sds((T, D), BF16), _sds((T, D), BF16)],
        scratch_shapes=[pltpu.VMEM((tm + 8, D), F32)],
        compiler_params=_cp("arbitrary"), name="conv_fwd")(x, vec, cw, w_in, w_out)


def conv_bwd(dxo, x, y, bcu, cv, vec, cw, w_in, w_out):
    T, D = x.shape
    tm = min(TOKEN_TILE, T)
    nt = T // tm

    def body(dxo_ref, x_ref, y_ref, bcu_ref, cv_ref, vec_ref, cw_ref, wi_ref, wo_ref,
             dx_ref, dy_ref, dbcu_ref, part_ref, dcw_ref, dcbuf):
        first = pl.program_id(0) == 0

        @pl.when(first)
        def _():
            dcbuf[tm:tm + 8, :] = jnp.zeros((8, D), F32)

        dxo_t = dxo_ref[...]
        dyb = (dxo_t * (1.0 + vec_ref[3:4])).astype(BF16)
        dy_ref[...] = dyb
        dgate = jnp.sum(dxo_t * y_ref[...].astype(F32), axis=0, keepdims=True)
        dz = _dot_nt(dyb, wo_ref[...])
        bcu_t = bcu_ref[...].astype(F32)
        bg, cg, ug = bcu_t[:, 0:D], bcu_t[:, D:2 * D], bcu_t[:, 2 * D:3 * D]
        dconv = dz * bg
        dbg = dz * cv_ref[...].astype(F32)
        dcbuf[0:tm, :] = dconv
        d1, d2 = dcbuf[1:tm + 1, :], dcbuf[2:tm + 2, :]
        dv = cw_ref[2:3] * dconv + cw_ref[1:2] * d1 + cw_ref[0:1] * d2
        v = cg * ug
        dcw = _rows8([jnp.sum(d2 * v, axis=0, keepdims=True), jnp.sum(d1 * v, axis=0, keepdims=True),
                      jnp.sum(dconv * v, axis=0, keepdims=True)], D)
        dbcu = jnp.concatenate([dbg, dv * ug, dv * cg], axis=1).astype(BF16)
        dbcu_ref[...] = dbcu
        dh = _dot_nt(dbcu, wi_ref[...])
        _, vjp = jax.vjp(_modnorm, x_ref[...], vec_ref[0:1], vec_ref[1:2], vec_ref[2:3])
        dx, dg, dsh, dsc = vjp(dh)
        dx_ref[...] = dxo_t + dx
        _acc_rows(part_ref, _rows8([dg, dsh, dsc, dgate], D), first)
        _acc_rows(dcw_ref, dcw, first)
        dcbuf[tm:tm + 8, :] = dcbuf[0:8, :]

    def rev(w):
        return pl.BlockSpec((tm, w), lambda i: (nt - 1 - i, 0))

    return pl.pallas_call(
        body, grid=(nt,),
        in_specs=[rev(D), rev(D), rev(D), rev(3 * D), rev(D), _res((8, D)), _res((8, D)), _res((D, 3 * D)), _res((D, D))],
        out_specs=[rev(D), rev(D), rev(3 * D), pl.BlockSpec((8, D), lambda i: (0, 0)), pl.BlockSpec((8, D), lambda i: (0, 0))],
        out_shape=[_sds((T, D), F32), _sds((T, D), BF16), _sds((T, 3 * D), BF16), _sds((8, D), F32), _sds((8, D), F32)],
        scratch_shapes=[pltpu.VMEM((tm + 8, D), F32)],
        compiler_params=_cp("arbitrary"), name="conv_bwd")(dxo, x, y, bcu, cv, vec, cw, w_in, w_out)


def rope_tables(pos, lane_rows):
    T = pos.shape[0]
    tm = min(TOKEN_TILE, T)

    def body(p_ref, lr_ref, c_ref, sp_ref, sm_ref):
        ang = p_ref[...].astype(F32) * lr_ref[0:1]
        cs, sn = jnp.cos(ang), jnp.sin(ang)
        c_ref[...] = jnp.where(lr_ref[1:2] > 0.5, cs, 1.0)
        sp_ref[...] = jnp.where(lr_ref[2:3] > 0.5, sn, 0.0)
        sm_ref[...] = jnp.where(lr_ref[3:4] > 0.5, -sn, 0.0)

    return pl.pallas_call(
        body, grid=(T // tm,),
        in_specs=[_tok(tm, 1), _res((8, LANES))],
        out_specs=[_tok(tm, LANES)] * 3,
        out_shape=[_sds((T, LANES), F32)] * 3,
        compiler_params=_cp("arbitrary"), name="rope_tables")(pos, lane_rows)


def _rope(t, c, sp, sm):
    w = t.shape[1]
    reps = w // LANES
    cf, spf, smf = jnp.tile(c, (1, reps)), jnp.tile(sp, (1, reps)), jnp.tile(sm, (1, reps))
    half = ROPE_DIM // 2
    return t * cf + pltpu.roll(t, half, axis=1) * spf + pltpu.roll(t, w - half, axis=1) * smf


def _rope_t(d, c, sp, sm):
    w = d.shape[1]
    reps = w // LANES
    cf, spf, smf = jnp.tile(c, (1, reps)), jnp.tile(sp, (1, reps)), jnp.tile(sm, (1, reps))
    half = ROPE_DIM // 2
    return d * cf + pltpu.roll(d * spf, w - half, axis=1) + pltpu.roll(d * smf, half, axis=1)


def _split_residues(v, d, stage):
    tm, width = v.shape
    if d == 1:
        return [v]
    nj = width // LANES
    for j in range(nj):
        stage[j] = v[:, j * LANES:(j + 1) * LANES]
    return [jnp.concatenate([stage[j, pl.ds(r, tm // d, stride=d), :] for j in range(nj)], axis=1) for r in range(d)]


def _merge_residues(piece, d, tm, width, stage):
    if d == 1:
        return piece(0)
    nj = width // LANES
    for r in range(d):
        p = piece(r)
        for j in range(nj):
            stage[j, pl.ds(r, tm // d, stride=d), :] = p[:, j * LANES:(j + 1) * LANES]
    return jnp.concatenate([stage[j] for j in range(nj)], axis=1)


def _residue_spec(d, tm):
    return pl.BlockSpec((d, tm // d, GROUP_WIDTH), lambda i: (0, i, 0))


def _stage_scratch(tm):
    return pltpu.VMEM((GROUP_WIDTH // LANES, tm, LANES), F32)


def proj_rope_fwd(x, vec, w, tabs, n_rope, transposed, dils, name):
    T, D = x.shape
    N = w.shape[0] if transposed else w.shape[1]
    tm = min(TOKEN_TILE, T)
    GW = GROUP_WIDTH
    piece_dils = [dils[j % len(dils)] for j in range(N // GW)]

    def body(x_ref, vec_ref, w_ref, c_ref, sp_ref, sm_ref, h_ref, *rest):
        out_refs, stage = rest[:-1], rest[-1]
        hb = _modnorm(x_ref[...], vec_ref[0:1], vec_ref[1:2], vec_ref[2:3]).astype(BF16)
        h_ref[...] = hb
        p = _dot_nt(hb, w_ref[...]) if transposed else _dot(hb, w_ref[...])
        pr = _rope(p[:, 0:n_rope], c_ref[...], sp_ref[...], sm_ref[...])
        for j, d in enumerate(piece_dils):
            src = pr if (j + 1) * GW <= n_rope else p
            for r, rows in enumerate(_split_residues(src[:, j * GW:(j + 1) * GW], d, stage)):
                out_refs[j][r] = rows.astype(BF16)

    return pl.pallas_call(
        body, grid=(T // tm,),
        in_specs=[_tok(tm, D), _res((8, D)), _res(w.shape)] + [_tok(tm, LANES)] * 3,
        out_specs=[_tok(tm, D)] + [_residue_spec(d, tm) for d in piece_dils],
        out_shape=[_sds((T, D), BF16)] + [_sds((d, T // d, GW), BF16) for d in piece_dils],
        scratch_shapes=[_stage_scratch(tm)],
        compiler_params=_cp("arbitrary"), name=name)(x, vec, w, *tabs)


def proj_rope_bwd(dparts, dils, x, dxo, vec, w, tabs, n_rope, transposed, name):
    T, D = x.shape
    N = w.shape[0] if transposed else w.shape[1]
    tm = min(TOKEN_TILE, T)
    GW = GROUP_WIDTH
    npart = len(dparts)
    piece_dils = [dils[j % len(dils)] for j in range(npart)]

    def body(*refs):
        d_refs = refs[:npart]
        x_ref, dxo_ref, vec_ref, w_ref, c_ref, sp_ref, sm_ref, dx_ref, dp_ref, part_ref, stage = refs[npart:]
        d = jnp.concatenate([_merge_residues(lambda r, ref=ref: ref[r].astype(F32), dd, tm, GW, stage)
                             for ref, dd in zip(d_refs, piece_dils)], axis=1)
        dr = _rope_t(d[:, 0:n_rope], c_ref[...], sp_ref[...], sm_ref[...])
        if n_rope < N:
            dr = jnp.concatenate([dr, d[:, n_rope:N]], axis=1)
        dpb = dr.astype(BF16)
        dp_ref[...] = dpb
        dh = _dot(dpb, w_ref[...]) if transposed else _dot_nt(dpb, w_ref[...])
        _, vjp = jax.vjp(_modnorm, x_ref[...], vec_ref[0:1], vec_ref[1:2], vec_ref[2:3])
        dx, dg, dsh, dsc = vjp(dh)
        dx_ref[...] = dxo_ref[...] + dx
        _acc_rows(part_ref, _rows8([dg, dsh, dsc], D), pl.program_id(0) == 0)

    return pl.pallas_call(
        body, grid=(T // tm,),
        in_specs=[_residue_spec(d, tm) for d in piece_dils] + [_tok(tm, D), _tok(tm, D), _res((8, D)), _res(w.shape)]
        + [_tok(tm, LANES)] * 3,
        out_specs=[_tok(tm, D), _tok(tm, N), pl.BlockSpec((8, D), lambda i: (0, 0))],
        out_shape=[_sds((T, D), F32), _sds((T, N), BF16), _sds((8, D), F32)],
        scratch_shapes=[_stage_scratch(tm)],
        compiler_params=_cp("arbitrary"), name=name)(*dparts, x, dxo, vec, w, *tabs)


def _valid_mask(n, i):
    qi = lax.broadcasted_iota(jnp.int32, (n, 2 * n), 0)
    kj = lax.broadcasted_iota(jnp.int32, (n, 2 * n), 1)
    dist = n + qi - kj
    return (dist >= 0) & (dist <= n) & ((kj >= n) | (i > 0))


def _band_specs(n):
    cur = pl.BlockSpec((None, n, GROUP_WIDTH), lambda r, i: (r, i, 0))
    prv = pl.BlockSpec((None, n, GROUP_WIDTH), lambda r, i: (r, jnp.maximum(i - 1, 0), 0))
    return cur, prv


def attn_core_fwd(q, k, v, g, n):
    d, M, GW = q.shape
    scale = HEAD_DIM ** -0.5

    def body(q_ref, kp_ref, kc_ref, vp_ref, vc_ref, o_ref, l_ref):
        valid = _valid_mask(n, pl.program_id(1))
        first = lax.broadcasted_iota(jnp.int32, (1, LANES), 1) < HEAD_DIM
        for pair in range(HEADS_PER_GROUP * HEAD_DIM // LANES):
            ps = slice(LANES * pair, LANES * (pair + 1))
            q2 = q_ref[:, ps]
            k2 = jnp.concatenate([kp_ref[:, ps], kc_ref[:, ps]], axis=0)
            v2 = jnp.concatenate([vp_ref[:, ps], vc_ref[:, ps]], axis=0)
            o2, l2 = [], []
            for sel in (first, jnp.logical_not(first)):
                s = jnp.where(valid, _dot_nt(jnp.where(sel, q2, jnp.zeros_like(q2)), k2) * scale, -1e30)
                m = jnp.max(s, axis=1, keepdims=True)
                p = jnp.exp(s - m)
                den = jnp.sum(p, axis=1, keepdims=True)
                o2.append(_dot((p / den).astype(BF16), v2))
                l2.append(m + jnp.log(den))
            o_ref[:, ps] = jnp.where(first, o2[0], o2[1])
            l_ref[:, ps] = jnp.where(first, l2[0], l2[1])

    cur, prv = _band_specs(n)
    return pl.pallas_call(
        body, grid=(d, M // n),
        in_specs=[cur, prv, cur, prv, cur], out_specs=[cur, cur],
        out_shape=[_sds((d, M, GW), F32), _sds((d, M, GW), F32)],
        compiler_params=_cp("arbitrary", "arbitrary"), name=f"attn_fwd_g{g}")(q, k, k, v, v)


def attn_core_bwd(q, k, v, do, rr, lse, g, n):
    d, M, GW = q.shape
    scale = HEAD_DIM ** -0.5

    def body(q_ref, kp_ref, kc_ref, vp_ref, vc_ref, do_ref, r_ref, l_ref, dq_ref, dkc_ref, dkp_ref, dvc_ref, dvp_ref):
        valid = _valid_mask(n, pl.program_id(1))
        first = lax.broadcasted_iota(jnp.int32, (1, LANES), 1) < HEAD_DIM
        for pair in range(HEADS_PER_GROUP * HEAD_DIM // LANES):
            ps = slice(LANES * pair, LANES * (pair + 1))
            q2, do2, r2 = q_ref[:, ps], do_ref[:, ps], r_ref[:, ps]
            k2 = jnp.concatenate([kp_ref[:, ps], kc_ref[:, ps]], axis=0)
            v2 = jnp.concatenate([vp_ref[:, ps], vc_ref[:, ps]], axis=0)
            dq2, dk, dv = [], None, None
            for half, sel in enumerate((first, jnp.logical_not(first))):
                qm = jnp.where(sel, q2, jnp.zeros_like(q2))
                dom = jnp.where(sel, do2, jnp.zeros_like(do2))
                s = jnp.where(valid, _dot_nt(qm, k2) * scale, -1e30)
                lane0 = LANES * pair + HEAD_DIM * half
                p = jnp.exp(s - l_ref[:, lane0:lane0 + 1])
                dp = _dot_nt(dom, v2)
                delta = jnp.sum(jnp.where(sel, r2, 0.0), axis=1, keepdims=True)
                ds = (p * (dp - delta) * scale).astype(BF16)
                dq2.append(_dot(ds, k2))
                dkh = _dot_tn(ds, qm)
                dvh = _dot_tn(p.astype(BF16), dom)
                dk = dkh if dk is None else dk + dkh
                dv = dvh if dv is None else dv + dvh
            dq_ref[:, ps] = jnp.where(first, dq2[0], dq2[1]).astype(BF16)
            dk, dv = dk.astype(BF16), dv.astype(BF16)
            dkp_ref[:, ps], dkc_ref[:, ps] = dk[0:n], dk[n:2 * n]
            dvp_ref[:, ps], dvc_ref[:, ps] = dv[0:n], dv[n:2 * n]

    cur, prv = _band_specs(n)
    return pl.pallas_call(
        body, grid=(d, M // n),
        in_specs=[cur, prv, cur, prv, cur, cur, cur, cur], out_specs=[cur] * 5,
        out_shape=[_sds((d, M, GW), BF16)] * 5,
        compiler_params=_cp("arbitrary", "arbitrary"), name=f"attn_bwd_g{g}")(q, k, k, v, v, do, rr, lse)


def dkv_combine(cur_prev, n, name):
    d, M, GW = cur_prev[0][0].shape
    nb = M // n
    flat = [a for pair in cur_prev for a in pair]

    def body(*refs):
        o_ref = refs[-1]
        last = pl.program_id(1) == nb - 1
        acc = jnp.zeros((n, GW), F32)
        for t in range(0, len(refs) - 1, 2):
            acc = acc + refs[t][...].astype(F32) + jnp.where(last, 0.0, refs[t + 1][...].astype(F32))
        o_ref[...] = acc.astype(BF16)

    cur = pl.BlockSpec((None, n, GW), lambda r, i: (r, i, 0))
    nxt = pl.BlockSpec((None, n, GW), lambda r, i: (r, jnp.minimum(i + 1, nb - 1), 0))
    return pl.pallas_call(
        body, grid=(d, nb), in_specs=[cur, nxt] * len(cur_prev), out_specs=cur,
        out_shape=_sds((d, M, GW), BF16),
        compiler_params=_cp("arbitrary", "arbitrary"), name=name)(*flat)


def _group_weights(ls):
    mx = functools.reduce(jnp.maximum, ls)
    es = [jnp.exp(l - mx) for l in ls]
    tot = functools.reduce(lambda a, b: a + b, es)
    return [e / tot for e in es]


def attn_mix_out(os_, ls, dils, x, vec, w_o):
    T, D = x.shape
    GW = GROUP_WIDTH
    tm = min(TOKEN_TILE, T)
    ng = len(os_)

    def body(*refs):
        o_refs, l_refs = refs[:ng], refs[ng:2 * ng]
        x_ref, vec_ref, w_ref, xn_ref, mix_ref, y_ref, stage = refs[2 * ng:]
        natural = lambda ref, d: _merge_residues(lambda r: ref[r], d, tm, GW, stage)
        ws = _group_weights([natural(r, d) for r, d in zip(l_refs, dils)])
        mixed = functools.reduce(lambda a, b: a + b, [w * natural(r, d) for w, r, d in zip(ws, o_refs, dils)])
        mb = mixed.astype(BF16)
        mix_ref[...] = mb
        y = _dot(mb, w_ref[...])
        y_ref[...] = y.astype(BF16)
        xn_ref[...] = x_ref[...] + (1.0 + vec_ref[3:4]) * y

    res = [_residue_spec(d, tm) for d in dils]
    return pl.pallas_call(
        body, grid=(T // tm,),
        in_specs=res + res + [_tok(tm, D), _res((8, D)), _res((GW, D))],
        out_specs=[_tok(tm, D), _tok(tm, GW), _tok(tm, D)],
        out_shape=[_sds((T, D), F32), _sds((T, GW), BF16), _sds((T, D), BF16)],
        scratch_shapes=[_stage_scratch(tm)],
        compiler_params=_cp("arbitrary"), name="attn_mix_out")(*os_, *ls, x, vec, w_o)


def attn_mix_bwd(dxo, y, vec, w_o, os_, ls, dils):
    T, D = dxo.shape
    GW = GROUP_WIDTH
    tm = min(TOKEN_TILE, T)
    ng = len(os_)

    def body(*refs):
        dxo_ref, y_ref, vec_ref, w_ref = refs[:4]
        o_refs, l_refs = refs[4:4 + ng], refs[4 + ng:4 + 2 * ng]
        dy_ref = refs[4 + 2 * ng]
        do_refs = refs[5 + 2 * ng:5 + 3 * ng]
        r_refs = refs[5 + 3 * ng:5 + 4 * ng]
        part_ref, stage = refs[5 + 4 * ng], refs[6 + 4 * ng]
        natural = lambda ref, d: _merge_residues(lambda r: ref[r], d, tm, GW, stage)
        dxo_t = dxo_ref[...]
        dyb = (dxo_t * (1.0 + vec_ref[3:4])).astype(BF16)
        dy_ref[...] = dyb
        dgate = jnp.sum(dxo_t * y_ref[...].astype(F32), axis=0, keepdims=True)
        _acc_rows(part_ref, _rows8([dgate], D), pl.program_id(0) == 0)
        dmix = _dot_nt(dyb, w_ref[...])
        ws = _group_weights([natural(r, d) for r, d in zip(l_refs, dils)])
        mixed = functools.reduce(lambda a, b: a + b, [w * natural(r, d) for w, r, d in zip(ws, o_refs, dils)])
        for gi in range(ng):
            do = ws[gi] * dmix
            for r, rows in enumerate(_split_residues(do, dils[gi], stage)):
                do_refs[gi][r] = rows.astype(BF16)
            for r, rows in enumerate(_split_residues(do * mixed, dils[gi], stage)):
                r_refs[gi][r] = rows

    res = [_residue_spec(d, tm) for d in dils]
    return pl.pallas_call(
        body, grid=(T // tm,),
        in_specs=[_tok(tm, D), _tok(tm, D), _res((8, D)), _res((GW, D))] + res + res,
        out_specs=[_tok(tm, D)] + res + res + [pl.BlockSpec((8, D), lambda i: (0, 0))],
        out_shape=[_sds((T, D), BF16)] + [_sds((d, T // d, GW), BF16) for d in dils]
        + [_sds((d, T // d, GW), F32) for d in dils] + [_sds((8, D), F32)],
        scratch_shapes=[_stage_scratch(tm)],
        compiler_params=_cp("arbitrary"), name="attn_mix_bwd")(dxo, y, vec, w_o, *os_, *ls)


def final_loss(x, gvec, target):
    T, D = x.shape
    tm = min(TOKEN_TILE, T)

    def norm(xv, g):
        return xv * lax.rsqrt(jnp.mean(xv * xv, axis=-1, keepdims=True) + NORM_EPS) * g

    def body(x_ref, g_ref, t_ref, dx_ref, part_ref, loss_ref):
        first = pl.program_id(0) == 0
        yv, vjp = jax.vjp(norm, x_ref[...], g_ref[0:1])
        err = yv - t_ref[...]
        dx, dg = vjp(err * (1.0 / D))
        dx_ref[...] = dx
        _acc_rows(part_ref, _rows8([dg], D), first)
        tile_loss = 0.5 * jnp.sum(jnp.sum(err * err, axis=1, keepdims=True) * (1.0 / D), axis=0, keepdims=True)
        _acc_rows(loss_ref, jnp.broadcast_to(tile_loss, (8, LANES)), first)

    return pl.pallas_call(
        body, grid=(T // tm,),
        in_specs=[_tok(tm, D), _res((8, D)), _tok(tm, D)],
        out_specs=[_tok(tm, D), pl.BlockSpec((8, D), lambda i: (0, 0)), pl.BlockSpec((8, LANES), lambda i: (0, 0))],
        out_shape=[_sds((T, D), F32), _sds((8, D), F32), _sds((8, LANES), F32)],
        compiler_params=_cp("arbitrary"), name="final_loss")(x, gvec, target)


def mods_project(c_all, w, b):
    B, D = c_all.shape
    L, _, N = w.shape

    def body(c_ref, w_ref, b_ref, o_ref):
        cv = c_ref[...]
        cond = cv * _sigmoid(cv)
        o_ref[0] = jnp.dot(cond, w_ref[0], preferred_element_type=F32, precision=lax.Precision.HIGHEST) + b_ref[0]

    return pl.pallas_call(
        body, grid=(L,),
        in_specs=[pl.BlockSpec((B, D), lambda l: (0, 0)), pl.BlockSpec((1, D, N), lambda l: (l, 0, 0)),
                  pl.BlockSpec((1, 1, N), lambda l: (l, 0, 0))],
        out_specs=pl.BlockSpec((1, B, N), lambda l: (l, 0, 0)),
        out_shape=_sds((L, B, N), F32),
        compiler_params=_cp("arbitrary"), name="mods_project")(c_all, w, b)


def mods_weight_grad(c_all, dm):
    B, D = c_all.shape
    L, _, N = dm.shape

    def body(c_ref, d_ref, o_ref):
        cv = c_ref[...]
        cond = cv * _sigmoid(cv)
        o_ref[0] = lax.dot_general(cond, d_ref[0], (((0,), (0,)), ((), ())), preferred_element_type=F32,
                                   precision=lax.Precision.HIGHEST)

    return pl.pallas_call(
        body, grid=(L,),
        in_specs=[pl.BlockSpec((B, D), lambda l: (0, 0)), pl.BlockSpec((1, B, N), lambda l: (l, 0, 0))],
        out_specs=pl.BlockSpec((1, D, N), lambda l: (l, 0, 0)),
        out_shape=_sds((L, D, N), F32),
        compiler_params=_cp("arbitrary"), name="mods_weight_grad")(c_all, dm)


def _adam_math(g, w, m, v):
    m2 = ADAM_B1 * m + (1.0 - ADAM_B1) * g
    v2 = ADAM_B2 * v + (1.0 - ADAM_B2) * (g * g)
    m_hat = m2 / (1.0 - ADAM_B1 ** ADAM_STEP)
    v_hat = v2 / (1.0 - ADAM_B2 ** ADAM_STEP)
    delta = -ADAM_LR * (m_hat / (jnp.sqrt(v_hat) + ADAM_EPS) + ADAM_WD * w)
    return delta, m2, v2


def adam_update(g, w, m, v, parts, name):
    R, C = w.shape
    tr = _pick(R, 256, 8)

    def body(g_ref, w_ref, m_ref, v_ref, go_ref, d_ref, mo_ref, vo_ref):
        if parts:
            gv = g_ref[0].astype(F32)
            for s in range(1, N_DEV):
                gv = gv + g_ref[s].astype(F32)
        else:
            gv = g_ref[...]
        go_ref[...] = gv
        d_ref[...], mo_ref[...], vo_ref[...] = _adam_math(gv, w_ref[...], m_ref[...], v_ref[...])

    gspec = pl.BlockSpec((N_DEV, tr, C), lambda i: (0, i, 0)) if parts else _tok(tr, C)
    return pl.pallas_call(
        body, grid=(R // tr,),
        in_specs=[gspec, _tok(tr, C), _tok(tr, C), _tok(tr, C)],
        out_specs=[_tok(tr, C)] * 4, out_shape=[_sds((R, C), F32)] * 4,
        compiler_params=_cp("arbitrary"), name=name)(g, w, m, v)


def adam_layer(parts, w, m, v, prev, layer, after, name):
    L, R, C = w.shape
    tr = _pick(R, 256, 8)
    prev = (list(prev) if prev is not None else []) + [after]

    def body(p_ref, w_ref, m_ref, v_ref, *rest):
        go_ref, d_ref, mo_ref, vo_ref = rest[-4:]
        gv = p_ref[0].astype(F32)
        for s in range(1, N_DEV):
            gv = gv + p_ref[s].astype(F32)
        go_ref[...] = gv
        d_ref[...], mo_ref[...], vo_ref[...] = _adam_math(gv, w_ref[...], m_ref[...], v_ref[...])

    lay = pl.BlockSpec((None, tr, C), lambda i: (layer, i, 0))
    return pl.pallas_call(
        body, grid=(R // tr,),
        in_specs=[pl.BlockSpec((N_DEV, tr, C), lambda i: (0, i, 0)), lay, lay, lay] + [pl.BlockSpec(memory_space=pl.ANY)] * len(prev),
        out_specs=[lay] * 4, out_shape=[_sds((L, R, C), F32)] * 4,
        input_output_aliases={4 + k: k for k in range(len(prev) - 1)},
        compiler_params=_cp("arbitrary"), name=name)(parts, w, m, v, *prev)


def _my_id():
    return 4 * lax.axis_index("x") + 2 * lax.axis_index("y") + lax.axis_index("c")


def _peer(s):
    x, y, c = lax.axis_index("x"), lax.axis_index("y"), lax.axis_index("c")
    px = (1 - x) if s & 4 else x
    py = (1 - y) if s & 2 else y
    pc = (1 - c) if s & 1 else c
    return (px, py, pc), 4 * px + 2 * py + pc


def all_gather(xs, space, name):
    na = len(xs)

    def body(*refs):
        x_refs, o_refs = refs[:na], refs[na:2 * na]
        send_sems, recv_sems, local_sems = refs[2 * na:]
        me = _my_id()
        locals_, sends = [], []
        for a in range(na):
            cp = pltpu.make_async_copy(x_refs[a], o_refs[a].at[me], local_sems.at[a])
            cp.start()
            locals_.append(cp)
        for s in range(1, N_DEV):
            peer, _ = _peer(s)
            for a in range(na):
                cp = pltpu.make_async_remote_copy(
                    src_ref=x_refs[a], dst_ref=o_refs[a].at[me], send_sem=send_sems.at[a, s - 1],
                    recv_sem=recv_sems.at[a, s - 1], device_id=peer, device_id_type=MESH)
                cp.start()
                sends.append(cp)
        for s in range(1, N_DEV):
            peer, pid = _peer(s)
            for a in range(na):
                pltpu.make_async_remote_copy(
                    src_ref=x_refs[a], dst_ref=o_refs[a].at[pid], send_sem=send_sems.at[a, s - 1],
                    recv_sem=recv_sems.at[a, s - 1], device_id=peer, device_id_type=MESH).wait_recv()
        for cp in sends:
            cp.wait_send()
        for cp in locals_:
            cp.wait()

    spec = pl.BlockSpec(memory_space=space)
    return pl.pallas_call(
        body, in_specs=[spec] * na, out_specs=[spec] * na,
        out_shape=[_sds((N_DEV,) + x.shape, x.dtype) for x in xs],
        scratch_shapes=[pltpu.SemaphoreType.DMA((na, N_DEV - 1)), pltpu.SemaphoreType.DMA((na, N_DEV - 1)),
                        pltpu.SemaphoreType.DMA((na,))],
        compiler_params=pltpu.CompilerParams(vmem_limit_bytes=VMEM_LIMIT), name=name)(*xs)


def exchange_slots(xs, name):
    na = len(xs)

    def body(*refs):
        x_refs, o_refs = refs[:na], refs[na:2 * na]
        send_sems, recv_sems, local_sems = refs[2 * na:]
        me = _my_id()
        locals_, sends = [], []
        for a in range(na):
            cp = pltpu.make_async_copy(x_refs[a].at[me], o_refs[a].at[me], local_sems.at[a])
            cp.start()
            locals_.append(cp)
        for s in range(1, N_DEV):
            peer, pid = _peer(s)
            for a in range(na):
                cp = pltpu.make_async_remote_copy(
                    src_ref=x_refs[a].at[pid], dst_ref=o_refs[a].at[me], send_sem=send_sems.at[a, s - 1],
                    recv_sem=recv_sems.at[a, s - 1], device_id=peer, device_id_type=MESH)
                cp.start()
                sends.append(cp)
        for s in range(1, N_DEV):
            peer, pid = _peer(s)
            for a in range(na):
                pltpu.make_async_remote_copy(
                    src_ref=x_refs[a].at[pid], dst_ref=o_refs[a].at[pid], send_sem=send_sems.at[a, s - 1],
                    recv_sem=recv_sems.at[a, s - 1], device_id=peer, device_id_type=MESH).wait_recv()
        for cp in sends:
            cp.wait_send()
        for cp in locals_:
            cp.wait()

    spec = pl.BlockSpec(memory_space=pl.ANY)
    return pl.pallas_call(
        body, in_specs=[spec] * na, out_specs=[spec] * na,
        out_shape=[_sds(x.shape, x.dtype) for x in xs],
        scratch_shapes=[pltpu.SemaphoreType.DMA((na, N_DEV - 1)), pltpu.SemaphoreType.DMA((na, N_DEV - 1)),
                        pltpu.SemaphoreType.DMA((na,))],
        compiler_params=pltpu.CompilerParams(vmem_limit_bytes=VMEM_LIMIT), name=name)(*xs)


_HBM = pl.BlockSpec(memory_space=pltpu.HBM)
_SEM = pl.BlockSpec(memory_space=pltpu.SEMAPHORE)
_EFFECT = pltpu.SideEffectType.DATAFLOW_SIDE_EFFECTING


def _split_copy(x_ref, land_ref, s, send_sem, recv_sem, scatter):
    peer, pid = _peer(s)
    src = x_ref.at[pid] if scatter else x_ref
    return pltpu.make_async_remote_copy(src_ref=src, dst_ref=land_ref.at[_my_id()], send_sem=send_sem, recv_sem=recv_sem,
                                        device_id=peer, device_id_type=MESH)


def comm_start(xs, scatter, after, name):
    na = len(xs)
    extra = [] if after is None else [after]
    me = _my_id()
    lands = []
    for x in xs:
        shape = x.shape if scatter else (N_DEV,) + x.shape
        own = lax.dynamic_slice_in_dim(x, me, 1, 0) if scatter else x[None]
        lands.append(lax.dynamic_update_slice(lax.empty(shape, x.dtype), own, (me,) + (0,) * (len(shape) - 1)))

    def body(*refs):
        x_refs, land_refs = refs[:na], refs[na:2 * na]
        send_sem, recv_sem = refs[2 * na + len(extra)], refs[2 * na + len(extra) + 1]
        token = refs[-1]
        for s in range(1, N_DEV):
            for a in range(na):
                _split_copy(x_refs[a], land_refs[a], s, send_sem, recv_sem, scatter).start()
        token[...] = jnp.zeros_like(token)

    outs = pl.pallas_call(
        body, name=name,
        out_shape=(pltpu.SemaphoreType.DMA(()), pltpu.SemaphoreType.DMA(()))
        + tuple(pltpu.HBM(x.shape, x.dtype) for x in xs) + tuple(pltpu.HBM(l.shape, l.dtype) for l in lands)
        + (_sds((8, LANES), F32),),
        in_specs=(_HBM,) * (2 * na) + (pl.BlockSpec(memory_space=pl.ANY),) * len(extra),
        out_specs=(_SEM, _SEM) + (_HBM,) * (2 * na) + (pl.BlockSpec(memory_space=pltpu.VMEM),),
        input_output_aliases={a: 2 + a for a in range(2 * na)},
        compiler_params=pltpu.CompilerParams(has_side_effects=_EFFECT),
    )(*[pltpu.with_memory_space_constraint(x, pltpu.HBM) for x in xs],
      *[pltpu.with_memory_space_constraint(l, pltpu.HBM) for l in lands], *extra)
    return dict(sems=outs[0:2], xs=outs[2:2 + na], lands=outs[2 + na:2 + 2 * na], token=outs[-1], scatter=scatter)


def comm_wait(started, after, name):
    xs, lands = started["xs"], started["lands"]
    scatter = started["scatter"]
    na = len(xs)

    def body(*refs):
        x_refs, land_refs = refs[:na], refs[na:2 * na]
        send_sem, recv_sem = refs[2 * na], refs[2 * na + 1]
        for s in range(1, N_DEV):
            for a in range(na):
                cp = _split_copy(x_refs[a], land_refs[a], s, send_sem, recv_sem, scatter)
                cp.wait_send()
                cp.wait_recv()

    outs = pl.pallas_call(
        body, name=name,
        out_shape=tuple(pltpu.HBM(x.shape, x.dtype) for x in xs) + tuple(pltpu.HBM(l.shape, l.dtype) for l in lands),
        in_specs=(_HBM,) * (2 * na) + (_SEM, _SEM, pl.BlockSpec(memory_space=pl.ANY)),
        out_specs=(_HBM,) * (2 * na),
        input_output_aliases={a: a for a in range(2 * na)},
        compiler_params=pltpu.CompilerParams(has_side_effects=_EFFECT),
    )(*xs, *lands, *started["sems"], after)
    return list(outs[na:])


def _cols_to_natural(g):
    return jnp.concatenate([g[k] for k in range(N_DEV)], axis=1)


def _cols_to_slots(w):
    ns = w.shape[1] // N_DEV
    return jnp.stack([w[:, k * ns:(k + 1) * ns] for k in range(N_DEV)])


def _vec8(rows, d):
    rows = [r.reshape(1, d).astype(F32) for r in rows]
    return jnp.concatenate(rows + [jnp.zeros((8 - len(rows), d), F32)], axis=0)


def _ffn_forward(x, vec, w_in_t, w_out):
    h, a, b, u = ffn_up(x, vec, w_in_t)
    xn, y = proj_out(u, x, vec, w_out, FFN_RES_WEIGHT, "ffn_down")
    return xn, (x, h, a, b, u, y)


def _ffn_backward(dxo, saved, vec, w_in_t, w_out):
    x, h, a, b, u, y = saved
    dy, dab, part_gate = ffn_down_bwd(dxo, y, vec, w_out, a, b)
    dx, part_norm = ffn_up_bwd(dab, w_in_t, x, dxo, vec)
    g_out = grad_slots(u, dy, "ffn_dw_out")
    g_in_t = grad_slots(dab, h, "ffn_dw_in")
    rows = jnp.concatenate([part_norm[0:3], part_gate[0:1]], axis=0)
    return dx, g_in_t, g_out, rows


_TRANSPOSED = ("ffn1_w_in", "ffn2_w_in", "attn_w_q")
_COL_NATURAL = ("conv_w_in", "w_kv", "attn_w_o")
_ROW_SHARDED = ("ffn1_w_out", "ffn2_w_out", "conv_w_out")
_BIG = _TRANSPOSED + _COL_NATURAL + _ROW_SHARDED


def weight_chunks():
    chunks = []
    for layer in range(DEPTH):
        first = [("ffn1_w_in", layer), ("ffn1_w_out", layer)]
        if layer == N_A_LAYERS:
            first = [("w_kv", layer)] + first
        mixer = [("conv_w_in", layer), ("conv_w_out", layer)] if layer < N_A_LAYERS else [("attn_w_q", layer), ("attn_w_o", layer)]
        rest = mixer + [("ffn2_w_in", layer), ("ffn2_w_out", layer)]
        chunks += [first, rest] if layer == 0 else [first + rest]
    return chunks


def stacked_index(name, layer):
    if name == "w_kv":
        return None
    return layer - N_A_LAYERS if name.startswith("attn") else layer


class ChunkComm:
    def __init__(self, shards):
        self.shards = shards
        self.chunks = weight_chunks()

    def _shard(self, name, layer):
        idx = stacked_index(name, layer)
        return self.shards[name][0 if idx is None else idx]

    def start_gather(self, ci, after):
        xs = [self._shard(n, l).astype(BF16) for n, l in self.chunks[ci]]
        return comm_start(xs, False, after, f"gather_start_{ci}")

    def finish_gather(self, ci, started, after):
        lands = comm_wait(started, after, f"gather_wait_{ci}")
        W = {}
        for key, g in zip(self.chunks[ci], lands):
            W[key] = _cols_to_natural(g) if key[0] in _COL_NATURAL else g.reshape(-1, g.shape[2])
        return W, lands[0]

    def start_exchange(self, ci, slots, after):
        return comm_start([slots[key] for key in self.chunks[ci]], True, after, f"exchange_start_{ci}")

    def finish_exchange(self, ci, started, after):
        lands = comm_wait(started, after, f"exchange_wait_{ci}")
        return dict(zip(self.chunks[ci], lands))


def device_step(x, positions, target, mods, kvmods, small, comm, gather0):
    T, D = x.shape
    groups = DILATED_GROUPS
    dils = [dil for _, dil in groups]
    lane = jnp.arange(LANES) % HEAD_DIM
    inv = ROPE_THETA ** (-jnp.arange(0, ROPE_DIM, 2, dtype=F32) / ROPE_DIM)
    lane_rows = _vec8([jnp.where(lane < ROPE_DIM, inv[lane % (ROPE_DIM // 2)], 0.0), lane < ROPE_DIM,
                       (lane >= ROPE_DIM // 2) & (lane < ROPE_DIM), lane < ROPE_DIM // 2], LANES)
    tabs = rope_tables(positions.reshape(T, 1), lane_rows)

    def after_token(v, token):
        return v if token is None else v + token[0, 0]

    def vec_of(layer, sub, token=None):
        return after_token(_vec8([small["norm_g"][layer, sub], mods[layer, 3 * sub], mods[layer, 3 * sub + 1],
                                  mods[layer, 3 * sub + 2]], D), token)

    saved = []
    kv_saved = None
    k_sh = v_sh = None
    qw = GROUP_WIDTH * len(groups)
    chunk_of = {key: ci for ci, chunk in enumerate(comm.chunks) for key in chunk}
    W = {}
    flight = {"ci": 0, "started": gather0, "token": None}

    def need(key, after):
        if key not in W:
            ci = chunk_of[key]
            assert ci == flight["ci"], (key, ci)
            got, landed = comm.finish_gather(ci, flight["started"], after)
            W.update(got)
            if ci + 1 < len(comm.chunks):
                flight.update(ci=ci + 1, started=comm.start_gather(ci + 1, landed))
                flight["token"] = flight["started"]["token"]
        return W[key]

    def behind_start(v):
        token, flight["token"] = flight["token"], None
        return after_token(v, token)

    for layer in range(DEPTH):
        if layer == N_A_LAYERS:
            w_kv = need(("w_kv", layer), x)
            kv_vec = behind_start(_vec8([small["kv_norm_g"], kvmods[0], kvmods[1]], D))
            h_kv, *kv_pieces = proj_rope_fwd(x, kv_vec, w_kv, tabs, qw, False, dils, "kv_fwd")
            k_sh, v_sh = kv_pieces[:len(groups)], kv_pieces[len(groups):]
            kv_saved = (x, h_kv, kv_vec)
        rec = {}
        w_in, w_out = need(("ffn1_w_in", layer), x), need(("ffn1_w_out", layer), x)
        v1 = behind_start(vec_of(layer, 0))
        x, rec["ffn1"] = _ffn_forward(x, v1, w_in, w_out)
        if layer < N_A_LAYERS:
            w_in, w_out = need(("conv_w_in", layer), x), need(("conv_w_out", layer), x)
            v2 = behind_start(vec_of(layer, 1))
            cw = _vec8(list(small["conv_w"][layer]), D)
            x_in = x
            x, h, bcu, cv, z, y = conv_fwd(x, v2, cw, w_in, w_out)
            rec["mix"] = (x_in, h, bcu, cv, z, y, cw)
        else:
            w_q, w_o = need(("attn_w_q", layer), x), need(("attn_w_o", layer), x)
            v2 = behind_start(vec_of(layer, 1))
            x_in = x
            h, *q = proj_rope_fwd(x, v2, w_q, tabs, qw, True, dils, "q_fwd")
            os_, ls = [], []
            for g, (win, dil) in enumerate(groups):
                o, l = attn_core_fwd(q[g], k_sh[g], v_sh[g], g, win // dil)
                os_.append(o)
                ls.append(l)
            x, mixed, y = attn_mix_out(os_, ls, dils, x, v2, w_o)
            rec["mix"] = (x_in, h, q, os_, ls, mixed, y)
        w_in, w_out = need(("ffn2_w_in", layer), x), need(("ffn2_w_out", layer), x)
        v3 = behind_start(vec_of(layer, 2))
        x, rec["ffn2"] = _ffn_forward(x, v3, w_in, w_out)
        rec["vecs"] = (v1, v2, v3)
        saved.append(rec)

    dx, part_final, loss_tile = final_loss(x, _vec8([small["final_norm_g"]], D), target)
    loss = loss_tile[0, 0]

    conv_rows = [None] * N_A_LAYERS
    kv_rows = None
    mod_rows = [[None] * 3 for _ in range(DEPTH)]
    dkv_pairs = [{"k": [], "v": []} for _ in groups]
    slots = {}
    exchanges = []
    token = None

    def send_ready_chunks():
        nonlocal token
        for ci in reversed(range(len(comm.chunks))):
            if ci not in [e[0] for e in exchanges] and all(key in slots for key in comm.chunks[ci]):
                started = comm.start_exchange(ci, slots, token)
                exchanges.append((ci, started))
                token = started["token"]

    for layer in reversed(range(DEPTH)):
        rec = saved[layer]
        v1, v2, v3 = rec["vecs"]
        dx, slots[("ffn2_w_in", layer)], slots[("ffn2_w_out", layer)], mod_rows[layer][2] = _ffn_backward(
            dx, rec["ffn2"], after_token(v3, token), W[("ffn2_w_in", layer)], W[("ffn2_w_out", layer)])
        if layer < N_A_LAYERS:
            x_in, h, bcu, cv, z, y, cw = rec["mix"]
            dx, dy, dbcu, part, dcw = conv_bwd(dx, x_in, y, bcu, cv, v2, cw, W[("conv_w_in", layer)], W[("conv_w_out", layer)])
            slots[("conv_w_out", layer)] = grad_slots(z, dy, "conv_dw_out")
            slots[("conv_w_in", layer)] = grad_slots(h, dbcu, "conv_dw_in", col_slots=True)
            conv_rows[layer] = dcw[0:3]
            mod_rows[layer][1] = part[0:4]
        else:
            x_in, h, q, os_, ls, mixed, y = rec["mix"]
            outs = attn_mix_bwd(dx, y, v2, W[("attn_w_o", layer)], os_, ls, dils)
            ng = len(groups)
            dy, dos, rrs, part_gate = outs[0], outs[1:1 + ng], outs[1 + ng:1 + 2 * ng], outs[1 + 2 * ng]
            slots[("attn_w_o", layer)] = grad_slots(mixed, dy, "attn_dw_o", col_slots=True)
            dqs = []
            for g, (win, dil) in enumerate(groups):
                dq, dkc, dkp, dvc, dvp = attn_core_bwd(q[g], k_sh[g], v_sh[g], dos[g], rrs[g], ls[g], g, win // dil)
                dqs.append(dq)
                dkv_pairs[g]["k"].append((dkc, dkp))
                dkv_pairs[g]["v"].append((dvc, dvp))
            dx, dqr, part_norm = proj_rope_bwd(dqs, dils, x_in, dx, v2, W[("attn_w_q", layer)], tabs, qw, True, "q_bwd")
            slots[("attn_w_q", layer)] = grad_slots(dqr, h, "attn_dw_q")
            mod_rows[layer][1] = jnp.concatenate([part_norm[0:3], part_gate[0:1]], axis=0)
        send_ready_chunks()
        dx, slots[("ffn1_w_in", layer)], slots[("ffn1_w_out", layer)], mod_rows[layer][0] = _ffn_backward(
            dx, rec["ffn1"], after_token(v1, token), W[("ffn1_w_in", layer)], W[("ffn1_w_out", layer)])
        if layer == N_A_LAYERS:
            x_kv, h_kv, kv_vec = kv_saved
            dparts = [dkv_combine(dkv_pairs[g]["k"], win // dil, f"dk_combine_g{g}") for g, (win, dil) in enumerate(groups)]
            dparts += [dkv_combine(dkv_pairs[g]["v"], win // dil, f"dv_combine_g{g}") for g, (win, dil) in enumerate(groups)]
            dx, dkvp, part_kv = proj_rope_bwd(dparts, dils, x_kv, dx, kv_vec, W[("w_kv", layer)], tabs, qw, False, "kv_bwd")
            slots[("w_kv", layer)] = grad_slots(h_kv, dkvp, "kv_dw", col_slots=True)
            kv_rows = part_kv[0:3]
        send_ready_chunks()

    grads = {"conv_w": jnp.stack(conv_rows), "kv_rows": kv_rows, "exchanges": exchanges}
    grads["final_norm_g"] = part_final[0]
    rows = jnp.stack([jnp.stack(r) for r in mod_rows])
    grads["norm_g"] = rows[:, :, 0]
    grads["mods"] = rows[:, :, 1:4].reshape(DEPTH, N_MOD, D)
    return loss, dx, grads


def _flat2(a):
    return a.reshape(-1, a.shape[-1])


def _pad_rows(a, mult):
    r = a.shape[0]
    pad = (-r) % mult
    return a if pad == 0 else jnp.concatenate([a, jnp.zeros((pad,) + a.shape[1:], a.dtype)], axis=0)


def kernel(x, c, positions, norm_g, ada_w, ada_b, ffn1_w_in, ffn1_w_out, ffn2_w_in, ffn2_w_out, conv_w_in, conv_w, conv_w_out, kv_norm_g, kv_ada_w, kv_ada_b, w_kv, attn_w_q, attn_w_o, final_norm_g, loss_target, m_norm_g, m_ada_w, m_ada_b, m_ffn1_w_in, m_ffn1_w_out, m_ffn2_w_in, m_ffn2_w_out, m_conv_w_in, m_conv_w, m_conv_w_out, m_kv_norm_g, m_kv_ada_w, m_kv_ada_b, m_w_kv, m_attn_w_q, m_attn_w_o, m_final_norm_g, v_norm_g, v_ada_w, v_ada_b, v_ffn1_w_in, v_ffn1_w_out, v_ffn2_w_in, v_ffn2_w_out, v_conv_w_in, v_conv_w, v_conv_w_out, v_kv_norm_g, v_kv_ada_w, v_kv_ada_b, v_w_kv, v_attn_w_q, v_attn_w_o, v_final_norm_g):
    names = ("norm_g", "ada_w", "ada_b", "ffn1_w_in", "ffn1_w_out", "ffn2_w_in", "ffn2_w_out", "conv_w_in", "conv_w",
             "conv_w_out", "kv_norm_g", "kv_ada_w", "kv_ada_b", "w_kv", "attn_w_q", "attn_w_o", "final_norm_g")
    wts = dict(zip(names, (norm_g, ada_w, ada_b, ffn1_w_in, ffn1_w_out, ffn2_w_in, ffn2_w_out, conv_w_in, conv_w, conv_w_out,
                           kv_norm_g, kv_ada_w, kv_ada_b, w_kv, attn_w_q, attn_w_o, final_norm_g)))
    mom = dict(zip(names, (m_norm_g, m_ada_w, m_ada_b, m_ffn1_w_in, m_ffn1_w_out, m_ffn2_w_in, m_ffn2_w_out, m_conv_w_in,
                           m_conv_w, m_conv_w_out, m_kv_norm_g, m_kv_ada_w, m_kv_ada_b, m_w_kv, m_attn_w_q, m_attn_w_o,
                           m_final_norm_g)))
    var = dict(zip(names, (v_norm_g, v_ada_w, v_ada_b, v_ffn1_w_in, v_ffn1_w_out, v_ffn2_w_in, v_ffn2_w_out, v_conv_w_in,
                           v_conv_w, v_conv_w_out, v_kv_norm_g, v_kv_ada_w, v_kv_ada_b, v_w_kv, v_attn_w_q, v_attn_w_o,
                           v_final_norm_g)))
    T, D = x.shape[1], x.shape[2]
    me = _my_id()
    nmod = ada_w.shape[2]
    nkv = kv_ada_w.shape[1]

    def stacked(w, n):
        w = w if w.ndim == 3 else w[None]
        return jnp.swapaxes(w, 1, 2) if n in _TRANSPOSED else w

    comm = ChunkComm({n: stacked(wts[n], n) for n in _BIG})
    W = {}

    ds = norm_g.shape[2]
    small = jnp.concatenate([c.reshape(-1), norm_g.reshape(-1), conv_w.reshape(-1)]).astype(F32)
    n_small = small.shape[0]
    small = _pad_rows(small.reshape(-1, 1), 8 * LANES).reshape(-1, LANES)
    (small_all,) = all_gather([small], pltpu.VMEM, "gather_small")
    small_all = small_all.reshape(N_DEV, -1)[:, :n_small]
    c_all = small_all[:, :D]
    def full_rows(off, count):
        return jnp.stack([small_all[:, off + i * ds:off + (i + 1) * ds].reshape(D) for i in range(count)])

    W["norm_g"] = full_rows(D, DEPTH * 3).reshape(DEPTH, 3, D)
    W["conv_w"] = full_rows(D + DEPTH * 3 * ds, N_A_LAYERS * 3).reshape(N_A_LAYERS, 3, D)
    W["kv_norm_g"], W["final_norm_g"] = kv_norm_g, final_norm_g

    ada_b_mine = lax.dynamic_slice_in_dim(ada_b, me * nmod, nmod, axis=1).reshape(DEPTH, 1, nmod)
    kv_b_mine = lax.dynamic_slice_in_dim(kv_ada_b, me * nkv, nkv, axis=0).reshape(1, 1, nkv)
    mods_cols = mods_project(c_all, ada_w, ada_b_mine)
    kv_cols = mods_project(c_all, kv_ada_w.reshape(1, D, nkv), kv_b_mine)
    mcat = jnp.concatenate([mods_cols[l] for l in range(DEPTH)] + [kv_cols[0]], axis=1)
    wm = mcat.shape[1]
    if wm % LANES:
        mcat = jnp.concatenate([mcat, jnp.zeros((N_DEV, LANES - wm % LANES), F32)], axis=1)
    (mods_all,) = exchange_slots([mcat.reshape(N_DEV, 1, -1)], "exchange_mods")
    gather0 = comm.start_gather(0, mods_all)
    mods_all = mods_all.reshape(N_DEV, -1)
    mods = jnp.stack([mods_all[:, l * nmod:(l + 1) * nmod].reshape(N_MOD, D) for l in range(DEPTH)])
    kvmods = mods_all[:, DEPTH * nmod:DEPTH * nmod + nkv].reshape(2, D)

    loss_local, dx, grads = device_step(x[0], positions[0], loss_target[0], mods, kvmods, W, comm, gather0)
    loss = lax.psum(loss_local, MESH_AXES)

    dmods = grads["mods"].reshape(-1)
    dkvm = grads["kv_rows"][1:3].reshape(-1)
    vecs = jnp.concatenate([dmods, dkvm, grads["kv_rows"][0], grads["final_norm_g"], grads["norm_g"].reshape(-1),
                            grads["conv_w"].reshape(-1)])
    n_vec = vecs.shape[0]
    vecs = _pad_rows(vecs.reshape(-1, 1), 8 * LANES).reshape(-1, LANES)
    (vec_all,) = all_gather([vecs], pltpu.VMEM, "gather_vector_grads")
    vec_all = vec_all.reshape(N_DEV, -1)[:, :n_vec]
    nm_, nk_ = DEPTH * N_MOD * D, 2 * D
    dmods_all = vec_all[:, :nm_].reshape(N_DEV, DEPTH, N_MOD * D)
    dkvm_all = vec_all[:, nm_:nm_ + nk_]
    rest = vec_all[:, nm_ + nk_:]
    parts_kv_norm, parts_final = rest[:, :D].reshape(N_DEV, 1, D), rest[:, D:2 * D].reshape(N_DEV, 1, D)
    parts_norm = lax.dynamic_slice_in_dim(rest[:, 2 * D:2 * D + DEPTH * 3 * D].reshape(N_DEV, DEPTH * 3, D), me * ds, ds, axis=2)
    parts_conv = lax.dynamic_slice_in_dim(rest[:, 2 * D + DEPTH * 3 * D:].reshape(N_DEV, N_A_LAYERS * 3, D), me * ds, ds, axis=2)
    dm_cols = lax.dynamic_slice_in_dim(dmods_all, me * nmod, nmod, axis=2)
    dm_mine = jnp.stack([dm_cols[:, l] for l in range(DEPTH)])
    dkv_mine = lax.dynamic_slice_in_dim(dkvm_all, me * nkv, nkv, axis=1).reshape(1, N_DEV, nkv)
    g_ada_w = mods_weight_grad(c_all, dm_mine)
    g_kv_ada_w = mods_weight_grad(c_all, dkv_mine)[0]

    out_g, out_d, out_m, out_v = {}, {}, {}, {}

    def update(n, g, w, parts=False):
        shp = w.shape
        w2 = w.reshape(1, -1) if w.ndim == 1 else _flat2(w)
        g2 = g if parts else g.reshape(w2.shape)
        res = adam_update(g2, w2, mom[n].reshape(w2.shape), var[n].reshape(w2.shape), parts, "adam_" + n)
        out_g[n], out_d[n], out_m[n], out_v[n] = (r.reshape(shp) for r in res)

    moms = {n: stacked(mom[n], n) for n in _BIG}
    vars_ = {n: stacked(var[n], n) for n in _BIG}
    results = {}
    after = dx
    for ci, started in grads["exchanges"]:
        for (n, layer), parts in comm.finish_exchange(ci, started, after).items():
            idx = stacked_index(n, layer)
            results[n] = adam_layer(parts, comm.shards[n], moms[n], vars_[n], results.get(n), 0 if idx is None else idx,
                                    after, f"adam_{n}_{layer}")
            after = results[n][1]
    for n in _BIG:
        res = [jnp.swapaxes(r, 1, 2) if n in _TRANSPOSED else r for r in results[n]]
        out_g[n], out_d[n], out_m[n], out_v[n] = (r.reshape(wts[n].shape) for r in res)
    update("ada_w", g_ada_w, ada_w)
    update("kv_ada_w", g_kv_ada_w, kv_ada_w)
    update("ada_b", dmods_all, ada_b, True)
    update("kv_ada_b", dkvm_all.reshape(N_DEV, 1, nk_), kv_ada_b, True)
    update("kv_norm_g", parts_kv_norm, kv_norm_g, True)
    update("final_norm_g", parts_final, final_norm_g, True)
    update("norm_g", parts_norm, norm_g, True)
    update("conv_w", parts_conv, conv_w, True)

    return (loss, dx.reshape(x.shape), *[out_g[n] for n in names], *[out_d[n] for n in names],
            *[out_m[n] for n in names], *[out_v[n] for n in names])
```

```python
import functools

import jax
import jax.numpy as jnp
from jax import lax
from jax.experimental import pallas as pl
from jax.experimental.pallas import tpu as pltpu

F32, BF16 = jnp.float32, jnp.bfloat16

N_DEV = 8
MESH_AXES = ("x", "y", "c")
DEPTH = 4
N_A_LAYERS = 2
HEAD_DIM = 64
HEADS_PER_GROUP = 8
GROUP_WIDTH = HEAD_DIM * HEADS_PER_GROUP
DILATED_GROUPS = ((128, 1), (512, 4), (2048, 16))
ROPE_DIM = HEAD_DIM // 4
ROPE_THETA = 500000.0
NORM_EPS = 1e-5
FFN_RES_WEIGHT = 0.5
N_MOD = 9
ADAM_LR, ADAM_B1, ADAM_B2, ADAM_EPS, ADAM_WD, ADAM_STEP = 0.001, 0.9, 0.999, 1e-08, 0.01, 10

LANES = 128
TOKEN_TILE = 512
CONTRACT_TILE = 2048
MXU_WIDTH = 256
VMEM_LIMIT = 56 * 1024 * 1024
MESH = pl.DeviceIdType.MESH


def _cp(*sem):
    return pltpu.CompilerParams(dimension_semantics=sem, vmem_limit_bytes=VMEM_LIMIT)


def _pick(n, cap, mult=LANES):
    if n <= cap:
        return n
    best = None
    for t in range(mult, cap + 1, mult):
        if n % t == 0:
            best = t
    assert best is not None, (n, cap)
    return best


def _tok(tm, w):
    return pl.BlockSpec((tm, w), lambda i: (i, 0))


def _res(shape):
    nd = len(shape)
    return pl.BlockSpec(shape, lambda *_: (0,) * nd, pipeline_mode=pl.Buffered(1))


def _sds(shape, dt):
    return jax.ShapeDtypeStruct(shape, dt)


def _sigmoid(a):
    return 1.0 / (1.0 + jnp.exp(-a))


def _modnorm(x, g, sh, sc):
    r = lax.rsqrt(jnp.mean(x * x, axis=-1, keepdims=True) + NORM_EPS)
    return (x * r * g) * (1.0 + sc) + sh


def _dot(a, b):
    return jnp.dot(a, b, preferred_element_type=F32)


def _dot_nt(a, b):
    return lax.dot_general(a, b, (((1,), (1,)), ((), ())), preferred_element_type=F32)


def _dot_tn(a, b):
    return lax.dot_general(a, b, (((0,), (0,)), ((), ())), preferred_element_type=F32)


def _rows8(rows, d):
    pad = 8 - len(rows)
    return jnp.concatenate(list(rows) + [jnp.zeros((pad, d), F32)], axis=0)


def _acc_rows(ref, tile, first):
    @pl.when(first)
    def _():
        ref[...] = tile

    @pl.when(jnp.logical_not(first))
    def _():
        ref[...] += tile


def ffn_up(x, vec, w_in_t):
    T, D = x.shape
    F = w_in_t.shape[0] // 2
    tm, cw = min(TOKEN_TILE, T), _pick(F, MXU_WIDTH)

    def body(x_ref, vec_ref, w_ref, h_ref, ga_ref, gb_ref, u_ref):
        hb = _modnorm(x_ref[...], vec_ref[0:1], vec_ref[1:2], vec_ref[2:3]).astype(BF16)
        h_ref[...] = hb
        for c in range(F // cw):
            lo, hi = c * cw, (c + 1) * cw
            a = _dot_nt(hb, w_ref[lo:hi, :])
            b = _dot_nt(hb, w_ref[F + lo:F + hi, :])
            sg = _sigmoid(a)
            silu = a * sg
            ga_ref[:, lo:hi] = (b * (sg + silu * (1.0 - sg))).astype(BF16)
            gb_ref[:, lo:hi] = silu.astype(BF16)
            u_ref[:, lo:hi] = (silu * b).astype(BF16)

    return pl.pallas_call(
        body, grid=(T // tm,),
        in_specs=[_tok(tm, D), _res((8, D)), _res((2 * F, D))],
        out_specs=[_tok(tm, D), _tok(tm, F), _tok(tm, F), _tok(tm, F)],
        out_shape=[_sds((T, D), BF16), _sds((T, F), BF16), _sds((T, F), BF16), _sds((T, F), BF16)],
        compiler_params=_cp("arbitrary"), name="ffn_up")(x, vec, w_in_t)


def proj_out(u, x, vec, w_out, res_weight, name):
    T, D = x.shape
    K = u.shape[1]
    tm = min(TOKEN_TILE, T)

    def body(u_ref, x_ref, vec_ref, w_ref, xn_ref, y_ref):
        y = _dot(u_ref[...], w_ref[...])
        y_ref[...] = y.astype(BF16)
        xn_ref[...] = x_ref[...] + (res_weight * (1.0 + vec_ref[3:4])) * y

    return pl.pallas_call(
        body, grid=(T // tm,),
        in_specs=[_tok(tm, K), _tok(tm, D), _res((8, D)), _res((K, D))],
        out_specs=[_tok(tm, D), _tok(tm, D)],
        out_shape=[_sds((T, D), F32), _sds((T, D), BF16)],
        compiler_params=_cp("arbitrary"), name=name)(u, x, vec, w_out)


def ffn_down_bwd(dxo, y, vec, w_out, a, b):
    T, D = dxo.shape
    F = a.shape[1]
    tm, cw = min(TOKEN_TILE, T), _pick(F, MXU_WIDTH)

    def body(dxo_ref, y_ref, vec_ref, w_ref, a_ref, b_ref, dy_ref, dab_ref, part_ref):
        dxo_t = dxo_ref[...]
        dyb = (dxo_t * (FFN_RES_WEIGHT * (1.0 + vec_ref[3:4]))).astype(BF16)
        dy_ref[...] = dyb
        dgate = FFN_RES_WEIGHT * jnp.sum(dxo_t * y_ref[...].astype(F32), axis=0, keepdims=True)
        _acc_rows(part_ref, _rows8([dgate], D), pl.program_id(0) == 0)
        for c in range(F // cw):
            lo, hi = c * cw, (c + 1) * cw
            du = _dot_nt(dyb, w_ref[lo:hi, :])
            dab_ref[:, lo:hi] = (du * a_ref[:, lo:hi].astype(F32)).astype(BF16)
            dab_ref[:, F + lo:F + hi] = (du * b_ref[:, lo:hi].astype(F32)).astype(BF16)

    return pl.pallas_call(
        body, grid=(T // tm,),
        in_specs=[_tok(tm, D), _tok(tm, D), _res((8, D)), _res((F, D)), _tok(tm, F), _tok(tm, F)],
        out_specs=[_tok(tm, D), _tok(tm, 2 * F), pl.BlockSpec((8, D), lambda i: (0, 0))],
        out_shape=[_sds((T, D), BF16), _sds((T, 2 * F), BF16), _sds((8, D), F32)],
        compiler_params=_cp("arbitrary"), name="ffn_down_bwd")(dxo, y, vec, w_out, a, b)


def ffn_up_bwd(dab, w_in_t, x, dxo, vec):
    T, D = x.shape
    F2 = dab.shape[1]
    tm = min(TOKEN_TILE, T)

    def body(dab_ref, w_ref, x_ref, dxo_ref, vec_ref, dx_ref, part_ref):
        dh = _dot(dab_ref[...], w_ref[...])
        _, vjp = jax.vjp(_modnorm, x_ref[...], vec_ref[0:1], vec_ref[1:2], vec_ref[2:3])
        dx, dg, dsh, dsc = vjp(dh)
        dx_ref[...] = dxo_ref[...] + dx
        _acc_rows(part_ref, _rows8([dg, dsh, dsc], D), pl.program_id(0) == 0)

    return pl.pallas_call(
        body, grid=(T // tm,),
        in_specs=[_tok(tm, F2), _res((F2, D)), _tok(tm, D), _tok(tm, D), _res((8, D))],
        out_specs=[_tok(tm, D), pl.BlockSpec((8, D), lambda i: (0, 0))],
        out_shape=[_sds((T, D), F32), _sds((8, D), F32)],
        compiler_params=_cp("arbitrary"), name="ffn_up_bwd")(dab, w_in_t, x, dxo, vec)


def grad_slots(a, b, name, col_slots=False):
    T, M = a.shape
    N = b.shape[1]
    tk = min(CONTRACT_TILE, T)
    nk = T // tk
    tmm = _pick(M, 1408)
    if col_slots:
        ns = N // N_DEV
        sp = max(s for s in (1, 2, 4, 8) if ns * s <= 1536)
        tn = ns * sp
    else:
        tn = _pick(N, 1536)

    def body(a_ref, b_ref, o_ref, acc):
        k = pl.program_id(2)
        t = _dot_tn(a_ref[...], b_ref[...])

        @pl.when(k == 0)
        def _():
            acc[...] = t

        @pl.when(k > 0)
        def _():
            acc[...] += t

        @pl.when(k == nk - 1)
        def _():
            if col_slots:
                for s in range(sp):
                    o_ref[s] = acc[:, s * ns:(s + 1) * ns].astype(BF16)
            else:
                o_ref[...] = acc[...].astype(BF16)

    if col_slots:
        out_spec, out_shape = pl.BlockSpec((sp, tmm, ns), lambda i, j, k: (j, i, 0)), _sds((N_DEV, M, ns), BF16)
    else:
        out_spec, out_shape = pl.BlockSpec((tmm, tn), lambda i, j, k: (i, j)), _sds((M, N), BF16)
    out = pl.pallas_call(
        body, grid=(M // tmm, N // tn, nk),
        in_specs=[pl.BlockSpec((tk, tmm), lambda i, j, k: (k, i)), pl.BlockSpec((tk, tn), lambda i, j, k: (k, j))],
        out_specs=out_spec, out_shape=out_shape,
        scratch_shapes=[pltpu.VMEM((tmm, tn), F32)],
        compiler_params=_cp("arbitrary", "arbitrary", "arbitrary"), name=name)(a, b)
    return out if col_slots else out.reshape(N_DEV, M // N_DEV, N)


def conv_fwd(x, vec, cw, w_in, w_out):
    T, D = x.shape
    tm = min(TOKEN_TILE, T)

    def body(x_ref, vec_ref, cw_ref, wi_ref, wo_ref, xn_ref, h_ref, bcu_ref, cv_ref, z_ref, y_ref, vbuf):
        @pl.when(pl.program_id(0) == 0)
        def _():
            vbuf[0:8, :] = jnp.zeros((8, D), F32)

        x_t = x_ref[...]
        hb = _modnorm(x_t, vec_ref[0:1], vec_ref[1:2], vec_ref[2:3]).astype(BF16)
        h_ref[...] = hb
        bcu = _dot(hb, wi_ref[...])
        bcu_ref[...] = bcu.astype(BF16)
        bg, v = bcu[:, 0:D], bcu[:, D:2 * D] * bcu[:, 2 * D:3 * D]
        vbuf[8:8 + tm, :] = v
        conv = cw_ref[0:1] * vbuf[6:6 + tm, :] + cw_ref[1:2] * vbuf[7:7 + tm, :] + cw_ref[2:3] * v
        cv_ref[...] = conv.astype(BF16)
        zb = (bg * conv).astype(BF16)
        z_ref[...] = zb
        y = _dot(zb, wo_ref[...])
        y_ref[...] = y.astype(BF16)
        xn_ref[...] = x_t + (1.0 + vec_ref[3:4]) * y
        vbuf[0:8, :] = vbuf[tm:tm + 8, :]

    return pl.pallas_call(
        body, grid=(T // tm,),
        in_specs=[_tok(tm, D), _res((8, D)), _res((8, D)), _res((D, 3 * D)), _res((D, D))],
        out_specs=[_tok(tm, D), _tok(tm, D), _tok(tm, 3 * D), _tok(tm, D), _tok(tm, D), _tok(tm, D)],
        out_shape=[_sds((T, D), F32), _sds((T, D), BF16), _sds((T, 3 * D), BF16), _sds((T, D), BF16),
                   _sds((T, D), BF16), _sds((T, D), BF16)],
        scratch_shapes=[pltpu.VMEM((tm + 8, D), F32)],
        compiler_params=_cp("arbitrary"), name="conv_fwd")(x, vec, cw, w_in, w_out)


def conv_bwd(dxo, x, y, bcu, cv, vec, cw, w_in, w_out):
    T, D = x.shape
    tm = min(TOKEN_TILE, T)
    nt = T // tm

    def body(dxo_ref, x_ref, y_ref, bcu_ref, cv_ref, vec_ref, cw_ref, wi_ref, wo_ref,
             dx_ref, dy_ref, dbcu_ref, part_ref, dcw_ref, dcbuf):
        first = pl.program_id(0) == 0

        @pl.when(first)
        def _():
            dcbuf[tm:tm + 8, :] = jnp.zeros((8, D), F32)

        dxo_t = dxo_ref[...]
        dyb = (dxo_t * (1.0 + vec_ref[3:4])).astype(BF16)
        dy_ref[...] = dyb
        dgate = jnp.sum(dxo_t * y_ref[...].astype(F32), axis=0, keepdims=True)
        dz = _dot_nt(dyb, wo_ref[...])
        bcu_t = bcu_ref[...].astype(F32)
        bg, cg, ug = bcu_t[:, 0:D], bcu_t[:, D:2 * D], bcu_t[:, 2 * D:3 * D]
        dconv = dz * bg
        dbg = dz * cv_ref[...].astype(F32)
        dcbuf[0:tm, :] = dconv
        d1, d2 = dcbuf[1:tm + 1, :], dcbuf[2:tm + 2, :]
        dv = cw_ref[2:3] * dconv + cw_ref[1:2] * d1 + cw_ref[0:1] * d2
        v = cg * ug
        dcw = _rows8([jnp.sum(d2 * v, axis=0, keepdims=True), jnp.sum(d1 * v, axis=0, keepdims=True),
                      jnp.sum(dconv * v, axis=0, keepdims=True)], D)
        dbcu = jnp.concatenate([dbg, dv * ug, dv * cg], axis=1).astype(BF16)
        dbcu_ref[...] = dbcu
        dh = _dot_nt(dbcu, wi_ref[...])
        _, vjp = jax.vjp(_modnorm, x_ref[...], vec_ref[0:1], vec_ref[1:2], vec_ref[2:3])
        dx, dg, dsh, dsc = vjp(dh)
        dx_ref[...] = dxo_t + dx
        _acc_rows(part_ref, _rows8([dg, dsh, dsc, dgate], D), first)
        _acc_rows(dcw_ref, dcw, first)
        dcbuf[tm:tm + 8, :] = dcbuf[0:8, :]

    def rev(w):
        return pl.BlockSpec((tm, w), lambda i: (nt - 1 - i, 0))

    return pl.pallas_call(
        body, grid=(nt,),
        in_specs=[rev(D), rev(D), rev(D), rev(3 * D), rev(D), _res((8, D)), _res((8, D)), _res((D, 3 * D)), _res((D, D))],
        out_specs=[rev(D), rev(D), rev(3 * D), pl.BlockSpec((8, D), lambda i: (0, 0)), pl.BlockSpec((8, D), lambda i: (0, 0))],
        out_shape=[_sds((T, D), F32), _sds((T, D), BF16), _sds((T, 3 * D), BF16), _sds((8, D), F32), _sds((8, D), F32)],
        scratch_shapes=[pltpu.VMEM((tm + 8, D), F32)],
        compiler_params=_cp("arbitrary"), name="conv_bwd")(dxo, x, y, bcu, cv, vec, cw, w_in, w_out)


def rope_tables(pos, lane_rows):
    T = pos.shape[0]
    tm = min(TOKEN_TILE, T)

    def body(p_ref, lr_ref, c_ref, sp_ref, sm_ref):
        ang = p_ref[...].astype(F32) * lr_ref[0:1]
        cs, sn = jnp.cos(ang), jnp.sin(ang)
        c_ref[...] = jnp.where(lr_ref[1:2] > 0.5, cs, 1.0)
        sp_ref[...] = jnp.where(lr_ref[2:3] > 0.5, sn, 0.0)
        sm_ref[...] = jnp.where(lr_ref[3:4] > 0.5, -sn, 0.0)

    return pl.pallas_call(
        body, grid=(T // tm,),
        in_specs=[_tok(tm, 1), _res((8, LANES))],
        out_specs=[_tok(tm, LANES)] * 3,
        out_shape=[_sds((T, LANES), F32)] * 3,
        compiler_params=_cp("arbitrary"), name="rope_tables")(pos, lane_rows)


def _rope(t, c, sp, sm):
    w = t.shape[1]
    reps = w // LANES
    cf, spf, smf = jnp.tile(c, (1, reps)), jnp.tile(sp, (1, reps)), jnp.tile(sm, (1, reps))
    half = ROPE_DIM // 2
    return t * cf + pltpu.roll(t, half, axis=1) * spf + pltpu.roll(t, w - half, axis=1) * smf


def _rope_t(d, c, sp, sm):
    w = d.shape[1]
    reps = w // LANES
    cf, spf, smf = jnp.tile(c, (1, reps)), jnp.tile(sp, (1, reps)), jnp.tile(sm, (1, reps))
    half = ROPE_DIM // 2
    return d * cf + pltpu.roll(d * spf, w - half, axis=1) + pltpu.roll(d * smf, half, axis=1)


def _split_residues(v, d, stage):
    tm, width = v.shape
    if d == 1:
        return [v]
    nj = width // LANES
    for j in range(nj):
        stage[j] = v[:, j * LANES:(j + 1) * LANES]
    return [jnp.concatenate([stage[j, pl.ds(r, tm // d, stride=d), :] for j in range(nj)], axis=1) for r in range(d)]


def _merge_residues(piece, d, tm, width, stage):
    if d == 1:
        return piece(0)
    nj = width // LANES
    for r in range(d):
        p = piece(r)
        for j in range(nj):
            stage[j, pl.ds(r, tm // d, stride=d), :] = p[:, j * LANES:(j + 1) * LANES]
    return jnp.concatenate([stage[j] for j in range(nj)], axis=1)


def _residue_spec(d, tm):
    return pl.BlockSpec((d, tm // d, GROUP_WIDTH), lambda i: (0, i, 0))


def _stage_scratch(tm):
    return pltpu.VMEM((GROUP_WIDTH // LANES, tm, LANES), F32)


def proj_rope_fwd(x, vec, w, tabs, n_rope, transposed, dils, name):
    T, D = x.shape
    N = w.shape[0] if transposed else w.shape[1]
    tm = min(TOKEN_TILE, T)
    GW = GROUP_WIDTH
    piece_dils = [dils[j % len(dils)] for j in range(N // GW)]

    def body(x_ref, vec_ref, w_ref, c_ref, sp_ref, sm_ref, h_ref, *rest):
        out_refs, stage = rest[:-1], rest[-1]
        hb = _modnorm(x_ref[...], vec_ref[0:1], vec_ref[1:2], vec_ref[2:3]).astype(BF16)
        h_ref[...] = hb
        p = _dot_nt(hb, w_ref[...]) if transposed else _dot(hb, w_ref[...])
        pr = _rope(p[:, 0:n_rope], c_ref[...], sp_ref[...], sm_ref[...])
        for j, d in enumerate(piece_dils):
            src = pr if (j + 1) * GW <= n_rope else p
            for r, rows in enumerate(_split_residues(src[:, j * GW:(j + 1) * GW], d, stage)):
                out_refs[j][r] = rows.astype(BF16)

    return pl.pallas_call(
        body, grid=(T // tm,),
        in_specs=[_tok(tm, D), _res((8, D)), _res(w.shape)] + [_tok(tm, LANES)] * 3,
        out_specs=[_tok(tm, D)] + [_residue_spec(d, tm) for d in piece_dils],
        out_shape=[_sds((T, D), BF16)] + [_sds((d, T // d, GW), BF16) for d in piece_dils],
        scratch_shapes=[_stage_scratch(tm)],
        compiler_params=_cp("arbitrary"), name=name)(x, vec, w, *tabs)


def proj_rope_bwd(dparts, dils, x, dxo, vec, w, tabs, n_rope, transposed, name):
    T, D = x.shape
    N = w.shape[0] if transposed else w.shape[1]
    tm = min(TOKEN_TILE, T)
    GW = GROUP_WIDTH
    npart = len(dparts)
    piece_dils = [dils[j % len(dils)] for j in range(npart)]

    def body(*refs):
        d_refs = refs[:npart]
        x_ref, dxo_ref, vec_ref, w_ref, c_ref, sp_ref, sm_ref, dx_ref, dp_ref, part_ref, stage = refs[npart:]
        d = jnp.concatenate([_merge_residues(lambda r, ref=ref: ref[r].astype(F32), dd, tm, GW, stage)
                             for ref, dd in zip(d_refs, piece_dils)], axis=1)
        dr = _rope_t(d[:, 0:n_rope], c_ref[...], sp_ref[...], sm_ref[...])
        if n_rope < N:
            dr = jnp.concatenate([dr, d[:, n_rope:N]], axis=1)
        dpb = dr.astype(BF16)
        dp_ref[...] = dpb
        dh = _dot(dpb, w_ref[...]) if transposed else _dot_nt(dpb, w_ref[...])
        _, vjp = jax.vjp(_modnorm, x_ref[...], vec_ref[0:1], vec_ref[1:2], vec_ref[2:3])
        dx, dg, dsh, dsc = vjp(dh)
        dx_ref[...] = dxo_ref[...] + dx
        _acc_rows(part_ref, _rows8([dg, dsh, dsc], D), pl.program_id(0) == 0)

    return pl.pallas_call(
        body, grid=(T // tm,),
        in_specs=[_residue_spec(d, tm) for d in piece_dils] + [_tok(tm, D), _tok(tm, D), _res((8, D)), _res(w.shape)]
        + [_tok(tm, LANES)] * 3,
        out_specs=[_tok(tm, D), _tok(tm, N), pl.BlockSpec((8, D), lambda i: (0, 0))],
        out_shape=[_sds((T, D), F32), _sds((T, N), BF16), _sds((8, D), F32)],
        scratch_shapes=[_stage_scratch(tm)],
        compiler_params=_cp("arbitrary"), name=name)(*dparts, x, dxo, vec, w, *tabs)


def _valid_mask(n, i):
    qi = lax.broadcasted_iota(jnp.int32, (n, 2 * n), 0)
    kj = lax.broadcasted_iota(jnp.int32, (n, 2 * n), 1)
    dist = n + qi - kj
    return (dist >= 0) & (dist <= n) & ((kj >= n) | (i > 0))


def _band_specs(n):
    two = pl.BlockSpec((None, 2 * n, GROUP_WIDTH), lambda r, i: (r, i, 0))
    prv = pl.BlockSpec((None, n, GROUP_WIDTH), lambda r, i: (r, jnp.maximum(2 * i - 1, 0), 0))
    one = pl.BlockSpec((None, n, GROUP_WIDTH), lambda r, i: (r, i, 0))
    return two, prv, one


def _pair_keys(prev_ref, two_ref, ps, n):
    cur2 = two_ref[:, ps]
    return jnp.concatenate([prev_ref[:, ps], cur2[0:n]], axis=0), cur2


def attn_core_fwd(q, k, v, g, n):
    d, M, GW = q.shape
    scale = HEAD_DIM ** -0.5

    def body(q_ref, kp_ref, kc_ref, vp_ref, vc_ref, o_ref, l_ref):
        masks = (_valid_mask(n, pl.program_id(1)), _valid_mask(n, 1))
        first = lax.broadcasted_iota(jnp.int32, (1, LANES), 1) < HEAD_DIM
        for pair in range(HEADS_PER_GROUP * HEAD_DIM // LANES):
            ps = slice(LANES * pair, LANES * (pair + 1))
            keys, vals = _pair_keys(kp_ref, kc_ref, ps, n), _pair_keys(vp_ref, vc_ref, ps, n)
            for blk in range(2):
                rows = slice(blk * n, (blk + 1) * n)
                q2 = q_ref[rows, ps]
                o2, l2 = [], []
                for sel in (first, jnp.logical_not(first)):
                    s = jnp.where(masks[blk], _dot_nt(jnp.where(sel, q2, jnp.zeros_like(q2)), keys[blk]) * scale, -1e30)
                    m = jnp.max(s, axis=1, keepdims=True)
                    p = jnp.exp(s - m)
                    den = jnp.sum(p, axis=1, keepdims=True)
                    o2.append(_dot((p / den).astype(BF16), vals[blk]))
                    l2.append(m + jnp.log(den))
                o_ref[rows, ps] = jnp.where(first, o2[0], o2[1])
                l_ref[rows, ps] = jnp.where(first, l2[0], l2[1])

    two, prv, _ = _band_specs(n)
    return pl.pallas_call(
        body, grid=(d, M // (2 * n)),
        in_specs=[two, prv, two, prv, two], out_specs=[two, two],
        out_shape=[_sds((d, M, GW), F32), _sds((d, M, GW), F32)],
        compiler_params=_cp("arbitrary", "arbitrary"), name=f"attn_fwd_g{g}")(q, k, k, v, v)


def attn_core_bwd(q, k, v, do, rr, lse, g, n):
    d, M, GW = q.shape
    scale = HEAD_DIM ** -0.5

    def body(q_ref, kp_ref, kc_ref, vp_ref, vc_ref, do_ref, r_ref, l_ref, dq_ref, dkc_ref, dkp_ref, dvc_ref, dvp_ref):
        masks = (_valid_mask(n, pl.program_id(1)), _valid_mask(n, 1))
        first = lax.broadcasted_iota(jnp.int32, (1, LANES), 1) < HEAD_DIM
        for pair in range(HEADS_PER_GROUP * HEAD_DIM // LANES):
            ps = slice(LANES * pair, LANES * (pair + 1))
            keys, vals = _pair_keys(kp_ref, kc_ref, ps, n), _pair_keys(vp_ref, vc_ref, ps, n)
            own = []
            for blk in range(2):
                rows = slice(blk * n, (blk + 1) * n)
                q2, do2, r2 = q_ref[rows, ps], do_ref[rows, ps], r_ref[rows, ps]
                dq2, dk, dv = [], None, None
                for half, sel in enumerate((first, jnp.logical_not(first))):
                    qm = jnp.where(sel, q2, jnp.zeros_like(q2))
                    dom = jnp.where(sel, do2, jnp.zeros_like(do2))
                    s = jnp.where(masks[blk], _dot_nt(qm, keys[blk]) * scale, -1e30)
                    lane0 = LANES * pair + HEAD_DIM * half
                    p = jnp.exp(s - l_ref[rows, lane0:lane0 + 1])
                    dp = _dot_nt(dom, vals[blk])
                    delta = jnp.sum(jnp.where(sel, r2, 0.0), axis=1, keepdims=True)
                    ds = (p * (dp - delta) * scale).astype(BF16)
                    dq2.append(_dot(ds, keys[blk]))
                    dkh = _dot_tn(ds, qm)
                    dvh = _dot_tn(p.astype(BF16), dom)
                    dk = dkh if dk is None else dk + dkh
                    dv = dvh if dv is None else dv + dvh
                dq_ref[rows, ps] = jnp.where(first, dq2[0], dq2[1]).astype(BF16)
                own.append((dk, dv))
            for t, (c_ref, p_ref) in enumerate(((dkc_ref, dkp_ref), (dvc_ref, dvp_ref))):
                a, b = own[0][t], own[1][t]
                p_ref[:, ps] = a[0:n].astype(BF16)
                c_ref[0:n, ps] = (a[n:2 * n] + b[0:n]).astype(BF16)
                c_ref[n:2 * n, ps] = b[n:2 * n].astype(BF16)

    two, prv, one = _band_specs(n)
    return pl.pallas_call(
        body, grid=(d, M // (2 * n)),
        in_specs=[two, prv, two, prv, two, two, two, two], out_specs=[two, two, one, two, one],
        out_shape=[_sds((d, M, GW), BF16), _sds((d, M, GW), BF16), _sds((d, M // 2, GW), BF16),
                   _sds((d, M, GW), BF16), _sds((d, M // 2, GW), BF16)],
        compiler_params=_cp("arbitrary", "arbitrary"), name=f"attn_bwd_g{g}")(q, k, k, v, v, do, rr, lse)


def dkv_combine(cur_prev, n, name):
    d, M, GW = cur_prev[0][0].shape
    rows = min(M, 1024)
    pairs = rows // (2 * n)
    steps = M // rows
    flat = [a for pair in cur_prev for a in pair]

    def body(*refs):
        o_ref = refs[-1]
        last = pl.program_id(1) == steps - 1
        acc = None
        shifted = None
        for t in range(0, len(refs) - 1, 3):
            c = refs[t][...].astype(F32)
            nxt = jnp.where(last, 0.0, refs[t + 2][...].astype(F32))
            s = nxt if pairs == 1 else jnp.concatenate([refs[t + 1][n:pairs * n, :].astype(F32), nxt], axis=0)
            acc = c if acc is None else acc + c
            shifted = s if shifted is None else shifted + s
        for m in range(pairs):
            lo = 2 * m * n
            o_ref[lo:lo + n, :] = acc[lo:lo + n].astype(BF16)
            o_ref[lo + n:lo + 2 * n, :] = (acc[lo + n:lo + 2 * n] + shifted[m * n:(m + 1) * n]).astype(BF16)

    cur = pl.BlockSpec((None, rows, GW), lambda r, i: (r, i, 0))
    same = pl.BlockSpec((None, pairs * n, GW), lambda r, i: (r, i, 0))
    nxt = pl.BlockSpec((None, n, GW), lambda r, i: (r, jnp.minimum((i + 1) * pairs, M // (2 * n) - 1), 0))
    args = []
    for c, p in cur_prev:
        args += [c, p, p]
    return pl.pallas_call(
        body, grid=(d, steps), in_specs=[cur, same, nxt] * len(cur_prev), out_specs=cur,
        out_shape=_sds((d, M, GW), BF16),
        compiler_params=_cp("arbitrary", "arbitrary"), name=name)(*args)


def _group_weights(ls):
    mx = functools.reduce(jnp.maximum, ls)
    es = [jnp.exp(l - mx) for l in ls]
    tot = functools.reduce(lambda a, b: a + b, es)
    return [e / tot for e in es]


def attn_mix_out(os_, ls, dils, x, vec, w_o):
    T, D = x.shape
    GW = GROUP_WIDTH
    tm = min(TOKEN_TILE, T)
    ng = len(os_)

    def body(*refs):
        o_refs, l_refs = refs[:ng], refs[ng:2 * ng]
        x_ref, vec_ref, w_ref, xn_ref, mix_ref, y_ref, stage = refs[2 * ng:]
        natural = lambda ref, d: _merge_residues(lambda r: ref[r], d, tm, GW, stage)
        ws = _group_weights([natural(r, d) for r, d in zip(l_refs, dils)])
        mixed = functools.reduce(lambda a, b: a + b, [w * natural(r, d) for w, r, d in zip(ws, o_refs, dils)])
        mb = mixed.astype(BF16)
        mix_ref[...] = mb
        y = _dot(mb, w_ref[...])
        y_ref[...] = y.astype(BF16)
        xn_ref[...] = x_ref[...] + (1.0 + vec_ref[3:4]) * y

    res = [_residue_spec(d, tm) for d in dils]
    return pl.pallas_call(
        body, grid=(T // tm,),
        in_specs=res + res + [_tok(tm, D), _res((8, D)), _res((GW, D))],
        out_specs=[_tok(tm, D), _tok(tm, GW), _tok(tm, D)],
        out_shape=[_sds((T, D), F32), _sds((T, GW), BF16), _sds((T, D), BF16)],
        scratch_shapes=[_stage_scratch(tm)],
        compiler_params=_cp("arbitrary"), name="attn_mix_out")(*os_, *ls, x, vec, w_o)


def attn_mix_bwd(dxo, y, vec, w_o, os_, ls, dils):
    T, D = dxo.shape
    GW = GROUP_WIDTH
    tm = min(TOKEN_TILE, T)
    ng = len(os_)

    def body(*refs):
        dxo_ref, y_ref, vec_ref, w_ref = refs[:4]
        o_refs, l_refs = refs[4:4 + ng], refs[4 + ng:4 + 2 * ng]
        dy_ref = refs[4 + 2 * ng]
        do_refs = refs[5 + 2 * ng:5 + 3 * ng]
        r_refs = refs[5 + 3 * ng:5 + 4 * ng]
        part_ref, stage = refs[5 + 4 * ng], refs[6 + 4 * ng]
        natural = lambda ref, d: _merge_residues(lambda r: ref[r], d, tm, GW, stage)
        dxo_t = dxo_ref[...]
        dyb = (dxo_t * (1.0 + vec_ref[3:4])).astype(BF16)
        dy_ref[...] = dyb
        dgate = jnp.sum(dxo_t * y_ref[...].astype(F32), axis=0, keepdims=True)
        _acc_rows(part_ref, _rows8([dgate], D), pl.program_id(0) == 0)
        dmix = _dot_nt(dyb, w_ref[...])
        ws = _group_weights([natural(r, d) for r, d in zip(l_refs, dils)])
        mixed = functools.reduce(lambda a, b: a + b, [w * natural(r, d) for w, r, d in zip(ws, o_refs, dils)])
        for gi in range(ng):
            do = ws[gi] * dmix
            for r, rows in enumerate(_split_residues(do, dils[gi], stage)):
                do_refs[gi][r] = rows.astype(BF16)
            for r, rows in enumerate(_split_residues(do * mixed, dils[gi], stage)):
                r_refs[gi][r] = rows

    res = [_residue_spec(d, tm) for d in dils]
    return pl.pallas_call(
        body, grid=(T // tm,),
        in_specs=[_tok(tm, D), _tok(tm, D), _res((8, D)), _res((GW, D))] + res + res,
        out_specs=[_tok(tm, D)] + res + res + [pl.BlockSpec((8, D), lambda i: (0, 0))],
        out_shape=[_sds((T, D), BF16)] + [_sds((d, T // d, GW), BF16) for d in dils]
        + [_sds((d, T // d, GW), F32) for d in dils] + [_sds((8, D), F32)],
        scratch_shapes=[_stage_scratch(tm)],
        compiler_params=_cp("arbitrary"), name="attn_mix_bwd")(dxo, y, vec, w_o, *os_, *ls)


def final_loss(x, gvec, target):
    T, D = x.shape
    tm = min(TOKEN_TILE, T)

    def norm(xv, g):
        return xv * lax.rsqrt(jnp.mean(xv * xv, axis=-1, keepdims=True) + NORM_EPS) * g

    def body(x_ref, g_ref, t_ref, dx_ref, part_ref, loss_ref):
        first = pl.program_id(0) == 0
        yv, vjp = jax.vjp(norm, x_ref[...], g_ref[0:1])
        err = yv - t_ref[...]
        dx, dg = vjp(err * (1.0 / D))
        dx_ref[...] = dx
        _acc_rows(part_ref, _rows8([dg], D), first)
        tile_loss = 0.5 * jnp.sum(jnp.sum(err * err, axis=1, keepdims=True) * (1.0 / D), axis=0, keepdims=True)
        _acc_rows(loss_ref, jnp.broadcast_to(tile_loss, (8, LANES)), first)

    return pl.pallas_call(
        body, grid=(T // tm,),
        in_specs=[_tok(tm, D), _res((8, D)), _tok(tm, D)],
        out_specs=[_tok(tm, D), pl.BlockSpec((8, D), lambda i: (0, 0)), pl.BlockSpec((8, LANES), lambda i: (0, 0))],
        out_shape=[_sds((T, D), F32), _sds((8, D), F32), _sds((8, LANES), F32)],
        compiler_params=_cp("arbitrary"), name="final_loss")(x, gvec, target)


def mods_project(c_all, w, b):
    B, D = c_all.shape
    L, _, N = w.shape

    def body(c_ref, w_ref, b_ref, o_ref):
        cv = c_ref[...]
        cond = cv * _sigmoid(cv)
        o_ref[0] = jnp.dot(cond, w_ref[0], preferred_element_type=F32, precision=lax.Precision.HIGHEST) + b_ref[0]

    return pl.pallas_call(
        body, grid=(L,),
        in_specs=[pl.BlockSpec((B, D), lambda l: (0, 0)), pl.BlockSpec((1, D, N), lambda l: (l, 0, 0)),
                  pl.BlockSpec((1, 1, N), lambda l: (l, 0, 0))],
        out_specs=pl.BlockSpec((1, B, N), lambda l: (l, 0, 0)),
        out_shape=_sds((L, B, N), F32),
        compiler_params=_cp("arbitrary"), name="mods_project")(c_all, w, b)


def mods_weight_grad(c_all, dm):
    B, D = c_all.shape
    L, _, N = dm.shape

    def body(c_ref, d_ref, o_ref):
        cv = c_ref[...]
        cond = cv * _sigmoid(cv)
        o_ref[0] = lax.dot_general(cond, d_ref[0], (((0,), (0,)), ((), ())), preferred_element_type=F32,
                                   precision=lax.Precision.HIGHEST)

    return pl.pallas_call(
        body, grid=(L,),
        in_specs=[pl.BlockSpec((B, D), lambda l: (0, 0)), pl.BlockSpec((1, B, N), lambda l: (l, 0, 0))],
        out_specs=pl.BlockSpec((1, D, N), lambda l: (l, 0, 0)),
        out_shape=_sds((L, D, N), F32),
        compiler_params=_cp("arbitrary"), name="mods_weight_grad")(c_all, dm)


def _adam_math(g, w, m, v):
    m2 = ADAM_B1 * m + (1.0 - ADAM_B1) * g
    v2 = ADAM_B2 * v + (1.0 - ADAM_B2) * (g * g)
    m_hat = m2 / (1.0 - ADAM_B1 ** ADAM_STEP)
    v_hat = v2 / (1.0 - ADAM_B2 ** ADAM_STEP)
    delta = -ADAM_LR * (m_hat / (jnp.sqrt(v_hat) + ADAM_EPS) + ADAM_WD * w)
    return delta, m2, v2


def adam_update(g, w, m, v, parts, name):
    R, C = w.shape
    tr = _pick(R, 256, 8)

    def body(g_ref, w_ref, m_ref, v_ref, go_ref, d_ref, mo_ref, vo_ref):
        if parts:
            gv = g_ref[0].astype(F32)
            for s in range(1, N_DEV):
                gv = gv + g_ref[s].astype(F32)
        else:
            gv = g_ref[...]
        go_ref[...] = gv
        d_ref[...], mo_ref[...], vo_ref[...] = _adam_math(gv, w_ref[...], m_ref[...], v_ref[...])

    gspec = pl.BlockSpec((N_DEV, tr, C), lambda i: (0, i, 0)) if parts else _tok(tr, C)
    return pl.pallas_call(
        body, grid=(R // tr,),
        in_specs=[gspec, _tok(tr, C), _tok(tr, C), _tok(tr, C)],
        out_specs=[_tok(tr, C)] * 4, out_shape=[_sds((R, C), F32)] * 4,
        compiler_params=_cp("arbitrary"), name=name)(g, w, m, v)


def adam_layer(parts, w, m, v, prev, layer, after, name):
    L, R, C = w.shape
    tr = _pick(R, 256, 8)
    prev = (list(prev) if prev is not None else []) + [after]

    def body(p_ref, w_ref, m_ref, v_ref, *rest):
        go_ref, d_ref, mo_ref, vo_ref = rest[-4:]
        gv = p_ref[0].astype(F32)
        for s in range(1, N_DEV):
            gv = gv + p_ref[s].astype(F32)
        go_ref[...] = gv
        d_ref[...], mo_ref[...], vo_ref[...] = _adam_math(gv, w_ref[...], m_ref[...], v_ref[...])

    lay = pl.BlockSpec((None, tr, C), lambda i: (layer, i, 0))
    return pl.pallas_call(
        body, grid=(R // tr,),
        in_specs=[pl.BlockSpec((N_DEV, tr, C), lambda i: (0, i, 0)), lay, lay, lay] + [pl.BlockSpec(memory_space=pl.ANY)] * len(prev),
        out_specs=[lay] * 4, out_shape=[_sds((L, R, C), F32)] * 4,
        input_output_aliases={4 + k: k for k in range(len(prev) - 1)},
        compiler_params=_cp("arbitrary"), name=name)(parts, w, m, v, *prev)


def _my_id():
    return 4 * lax.axis_index("x") + 2 * lax.axis_index("y") + lax.axis_index("c")


def _peer(s):
    x, y, c = lax.axis_index("x"), lax.axis_index("y"), lax.axis_index("c")
    px = (1 - x) if s & 4 else x
    py = (1 - y) if s & 2 else y
    pc = (1 - c) if s & 1 else c
    return (px, py, pc), 4 * px + 2 * py + pc


def all_gather(xs, space, name):
    na = len(xs)

    def body(*refs):
        x_refs, o_refs = refs[:na], refs[na:2 * na]
        send_sems, recv_sems, local_sems = refs[2 * na:]
        me = _my_id()
        locals_, sends = [], []
        for a in range(na):
            cp = pltpu.make_async_copy(x_refs[a], o_refs[a].at[me], local_sems.at[a])
            cp.start()
            locals_.append(cp)
        for s in range(1, N_DEV):
            peer, _ = _peer(s)
            for a in range(na):
                cp = pltpu.make_async_remote_copy(
                    src_ref=x_refs[a], dst_ref=o_refs[a].at[me], send_sem=send_sems.at[a, s - 1],
                    recv_sem=recv_sems.at[a, s - 1], device_id=peer, device_id_type=MESH)
                cp.start()
                sends.append(cp)
        for s in range(1, N_DEV):
            peer, pid = _peer(s)
            for a in range(na):
                pltpu.make_async_remote_copy(
                    src_ref=x_refs[a], dst_ref=o_refs[a].at[pid], send_sem=send_sems.at[a, s - 1],
                    recv_sem=recv_sems.at[a, s - 1], device_id=peer, device_id_type=MESH).wait_recv()
        for cp in sends:
            cp.wait_send()
        for cp in locals_:
            cp.wait()

    spec = pl.BlockSpec(memory_space=space)
    return pl.pallas_call(
        body, in_specs=[spec] * na, out_specs=[spec] * na,
        out_shape=[_sds((N_DEV,) + x.shape, x.dtype) for x in xs],
        scratch_shapes=[pltpu.SemaphoreType.DMA((na, N_DEV - 1)), pltpu.SemaphoreType.DMA((na, N_DEV - 1)),
                        pltpu.SemaphoreType.DMA((na,))],
        compiler_params=pltpu.CompilerParams(vmem_limit_bytes=VMEM_LIMIT), name=name)(*xs)


def exchange_slots(xs, name):
    na = len(xs)

    def body(*refs):
        x_refs, o_refs = refs[:na], refs[na:2 * na]
        send_sems, recv_sems, local_sems = refs[2 * na:]
        me = _my_id()
        locals_, sends = [], []
        for a in range(na):
            cp = pltpu.make_async_copy(x_refs[a].at[me], o_refs[a].at[me], local_sems.at[a])
            cp.start()
            locals_.append(cp)
        for s in range(1, N_DEV):
            peer, pid = _peer(s)
            for a in range(na):
                cp = pltpu.make_async_remote_copy(
                    src_ref=x_refs[a].at[pid], dst_ref=o_refs[a].at[me], send_sem=send_sems.at[a, s - 1],
                    recv_sem=recv_sems.at[a, s - 1], device_id=peer, device_id_type=MESH)
                cp.start()
                sends.append(cp)
        for s in range(1, N_DEV):
            peer, pid = _peer(s)
            for a in range(na):
                pltpu.make_async_remote_copy(
                    src_ref=x_refs[a].at[pid], dst_ref=o_refs[a].at[pid], send_sem=send_sems.at[a, s - 1],
                    recv_sem=recv_sems.at[a, s - 1], device_id=peer, device_id_type=MESH).wait_recv()
        for cp in sends:
            cp.wait_send()
        for cp in locals_:
            cp.wait()

    spec = pl.BlockSpec(memory_space=pl.ANY)
    return pl.pallas_call(
        body, in_specs=[spec] * na, out_specs=[spec] * na,
        out_shape=[_sds(x.shape, x.dtype) for x in xs],
        scratch_shapes=[pltpu.SemaphoreType.DMA((na, N_DEV - 1)), pltpu.SemaphoreType.DMA((na, N_DEV - 1)),
                        pltpu.SemaphoreType.DMA((na,))],
        compiler_params=pltpu.CompilerParams(vmem_limit_bytes=VMEM_LIMIT), name=name)(*xs)


_HBM = pl.BlockSpec(memory_space=pltpu.HBM)
_SEM = pl.BlockSpec(memory_space=pltpu.SEMAPHORE)
_EFFECT = pltpu.SideEffectType.DATAFLOW_SIDE_EFFECTING


def _split_copy(x_ref, land_ref, s, send_sem, recv_sem, scatter):
    peer, pid = _peer(s)
    src = x_ref.at[pid] if scatter else x_ref
    return pltpu.make_async_remote_copy(src_ref=src, dst_ref=land_ref.at[_my_id()], send_sem=send_sem, recv_sem=recv_sem,
                                        device_id=peer, device_id_type=MESH)


def comm_start(xs, scatter, after, name):
    na = len(xs)
    extra = [] if after is None else [after]
    me = _my_id()
    lands = []
    for x in xs:
        shape = x.shape if scatter else (N_DEV,) + x.shape
        own = lax.dynamic_slice_in_dim(x, me, 1, 0) if scatter else x[None]
        lands.append(lax.dynamic_update_slice(lax.empty(shape, x.dtype), own, (me,) + (0,) * (len(shape) - 1)))

    def body(*refs):
        x_refs, land_refs = refs[:na], refs[na:2 * na]
        send_sem, recv_sem = refs[2 * na + len(extra)], refs[2 * na + len(extra) + 1]
        token = refs[-1]
        for s in range(1, N_DEV):
            for a in range(na):
                _split_copy(x_refs[a], land_refs[a], s, send_sem, recv_sem, scatter).start()
        token[...] = jnp.zeros_like(token)

    outs = pl.pallas_call(
        body, name=name,
        out_shape=(pltpu.SemaphoreType.DMA(()), pltpu.SemaphoreType.DMA(()))
        + tuple(pltpu.HBM(x.shape, x.dtype) for x in xs) + tuple(pltpu.HBM(l.shape, l.dtype) for l in lands)
        + (_sds((8, LANES), F32),),
        in_specs=(_HBM,) * (2 * na) + (pl.BlockSpec(memory_space=pl.ANY),) * len(extra),
        out_specs=(_SEM, _SEM) + (_HBM,) * (2 * na) + (pl.BlockSpec(memory_space=pltpu.VMEM),),
        input_output_aliases={a: 2 + a for a in range(2 * na)},
        compiler_params=pltpu.CompilerParams(has_side_effects=_EFFECT),
    )(*[pltpu.with_memory_space_constraint(x, pltpu.HBM) for x in xs],
      *[pltpu.with_memory_space_constraint(l, pltpu.HBM) for l in lands], *extra)
    return dict(sems=outs[0:2], xs=outs[2:2 + na], lands=outs[2 + na:2 + 2 * na], token=outs[-1], scatter=scatter)


def comm_wait(started, after, name):
    xs, lands = started["xs"], started["lands"]
    scatter = started["scatter"]
    na = len(xs)

    def body(*refs):
        x_refs, land_refs = refs[:na], refs[na:2 * na]
        send_sem, recv_sem = refs[2 * na], refs[2 * na + 1]
        for s in range(1, N_DEV):
            for a in range(na):
                cp = _split_copy(x_refs[a], land_refs[a], s, send_sem, recv_sem, scatter)
                cp.wait_send()
                cp.wait_recv()

    outs = pl.pallas_call(
        body, name=name,
        out_shape=tuple(pltpu.HBM(x.shape, x.dtype) for x in xs) + tuple(pltpu.HBM(l.shape, l.dtype) for l in lands),
        in_specs=(_HBM,) * (2 * na) + (_SEM, _SEM, pl.BlockSpec(memory_space=pl.ANY)),
        out_specs=(_HBM,) * (2 * na),
        input_output_aliases={a: a for a in range(2 * na)},
        compiler_params=pltpu.CompilerParams(has_side_effects=_EFFECT),
    )(*xs, *lands, *started["sems"], after)
    return list(outs[na:])


def _cols_to_natural(g):
    return jnp.concatenate([g[k] for k in range(N_DEV)], axis=1)


def _cols_to_slots(w):
    ns = w.shape[1] // N_DEV
    return jnp.stack([w[:, k * ns:(k + 1) * ns] for k in range(N_DEV)])


def _vec8(rows, d):
    rows = [r.reshape(1, d).astype(F32) for r in rows]
    return jnp.concatenate(rows + [jnp.zeros((8 - len(rows), d), F32)], axis=0)


def _ffn_forward(x, vec, w_in_t, w_out):
    h, a, b, u = ffn_up(x, vec, w_in_t)
    xn, y = proj_out(u, x, vec, w_out, FFN_RES_WEIGHT, "ffn_down")
    return xn, (x, h, a, b, u, y)


def _ffn_backward(dxo, saved, vec, w_in_t, w_out):
    x, h, a, b, u, y = saved
    dy, dab, part_gate = ffn_down_bwd(dxo, y, vec, w_out, a, b)
    dx, part_norm = ffn_up_bwd(dab, w_in_t, x, dxo, vec)
    g_out = grad_slots(u, dy, "ffn_dw_out")
    g_in_t = grad_slots(dab, h, "ffn_dw_in")
    rows = jnp.concatenate([part_norm[0:3], part_gate[0:1]], axis=0)
    return dx, g_in_t, g_out, rows


_TRANSPOSED = ("ffn1_w_in", "ffn2_w_in", "attn_w_q")
_COL_NATURAL = ("conv_w_in", "w_kv", "attn_w_o")
_ROW_SHARDED = ("ffn1_w_out", "ffn2_w_out", "conv_w_out")
_BIG = _TRANSPOSED + _COL_NATURAL + _ROW_SHARDED


def weight_chunks():
    chunks = []
    for layer in range(DEPTH):
        first = [("ffn1_w_in", layer), ("ffn1_w_out", layer)]
        if layer == N_A_LAYERS:
            first = [("w_kv", layer)] + first
        mixer = [("conv_w_in", layer), ("conv_w_out", layer)] if layer < N_A_LAYERS else [("attn_w_q", layer), ("attn_w_o", layer)]
        rest = mixer + [("ffn2_w_in", layer), ("ffn2_w_out", layer)]
        chunks += [first, rest] if layer == 0 else [first + rest]
    return chunks


def stacked_index(name, layer):
    if name == "w_kv":
        return None
    return layer - N_A_LAYERS if name.startswith("attn") else layer


class ChunkComm:
    def __init__(self, shards):
        self.shards = shards
        self.chunks = weight_chunks()

    def _shard(self, name, layer):
        idx = stacked_index(name, layer)
        return self.shards[name][0 if idx is None else idx]

    def start_gather(self, ci, after):
        xs = [self._shard(n, l).astype(BF16) for n, l in self.chunks[ci]]
        return comm_start(xs, False, after, f"gather_start_{ci}")

    def finish_gather(self, ci, started, after):
        lands = comm_wait(started, after, f"gather_wait_{ci}")
        W = {}
        for key, g in zip(self.chunks[ci], lands):
            W[key] = _cols_to_natural(g) if key[0] in _COL_NATURAL else g.reshape(-1, g.shape[2])
        return W, lands[0]

    def start_exchange(self, ci, slots, after):
        return comm_start([slots[key] for key in self.chunks[ci]], True, after, f"exchange_start_{ci}")

    def finish_exchange(self, ci, started, after):
        lands = comm_wait(started, after, f"exchange_wait_{ci}")
        return dict(zip(self.chunks[ci], lands))


def device_step(x, positions, target, mods, kvmods, small, comm, gather0):
    T, D = x.shape
    groups = DILATED_GROUPS
    dils = [dil for _, dil in groups]
    lane = jnp.arange(LANES) % HEAD_DIM
    inv = ROPE_THETA ** (-jnp.arange(0, ROPE_DIM, 2, dtype=F32) / ROPE_DIM)
    lane_rows = _vec8([jnp.where(lane < ROPE_DIM, inv[lane % (ROPE_DIM // 2)], 0.0), lane < ROPE_DIM,
                       (lane >= ROPE_DIM // 2) & (lane < ROPE_DIM), lane < ROPE_DIM // 2], LANES)
    tabs = rope_tables(positions.reshape(T, 1), lane_rows)

    def after_token(v, token):
        return v if token is None else v + token[0, 0]

    def vec_of(layer, sub, token=None):
        return after_token(_vec8([small["norm_g"][layer, sub], mods[layer, 3 * sub], mods[layer, 3 * sub + 1],
                                  mods[layer, 3 * sub + 2]], D), token)

    saved = []
    kv_saved = None
    k_sh = v_sh = None
    qw = GROUP_WIDTH * len(groups)
    chunk_of = {key: ci for ci, chunk in enumerate(comm.chunks) for key in chunk}
    W = {}
    flight = {"ci": 0, "started": gather0, "token": None}

    def need(key, after):
        if key not in W:
            ci = chunk_of[key]
            assert ci == flight["ci"], (key, ci)
            got, landed = comm.finish_gather(ci, flight["started"], after)
            W.update(got)
            if ci + 1 < len(comm.chunks):
                flight.update(ci=ci + 1, started=comm.start_gather(ci + 1, landed))
                flight["token"] = flight["started"]["token"]
        return W[key]

    def behind_start(v):
        token, flight["token"] = flight["token"], None
        return after_token(v, token)

    for layer in range(DEPTH):
        if layer == N_A_LAYERS:
            w_kv = need(("w_kv", layer), x)
            kv_vec = behind_start(_vec8([small["kv_norm_g"], kvmods[0], kvmods[1]], D))
            h_kv, *kv_pieces = proj_rope_fwd(x, kv_vec, w_kv, tabs, qw, False, dils, "kv_fwd")
            k_sh, v_sh = kv_pieces[:len(groups)], kv_pieces[len(groups):]
            kv_saved = (x, h_kv, kv_vec)
        rec = {}
        w_in, w_out = need(("ffn1_w_in", layer), x), need(("ffn1_w_out", layer), x)
        v1 = behind_start(vec_of(layer, 0))
        x, rec["ffn1"] = _ffn_forward(x, v1, w_in, w_out)
        if layer < N_A_LAYERS:
            w_in, w_out = need(("conv_w_in", layer), x), need(("conv_w_out", layer), x)
            v2 = behind_start(vec_of(layer, 1))
            cw = _vec8(list(small["conv_w"][layer]), D)
            x_in = x
            x, h, bcu, cv, z, y = conv_fwd(x, v2, cw, w_in, w_out)
            rec["mix"] = (x_in, h, bcu, cv, z, y, cw)
        else:
            w_q, w_o = need(("attn_w_q", layer), x), need(("attn_w_o", layer), x)
            v2 = behind_start(vec_of(layer, 1))
            x_in = x
            h, *q = proj_rope_fwd(x, v2, w_q, tabs, qw, True, dils, "q_fwd")
            os_, ls = [], []
            for g, (win, dil) in enumerate(groups):
                o, l = attn_core_fwd(q[g], k_sh[g], v_sh[g], g, win // dil)
                os_.append(o)
                ls.append(l)
            x, mixed, y = attn_mix_out(os_, ls, dils, x, v2, w_o)
            rec["mix"] = (x_in, h, q, os_, ls, mixed, y)
        w_in, w_out = need(("ffn2_w_in", layer), x), need(("ffn2_w_out", layer), x)
        v3 = behind_start(vec_of(layer, 2))
        x, rec["ffn2"] = _ffn_forward(x, v3, w_in, w_out)
        rec["vecs"] = (v1, v2, v3)
        saved.append(rec)

    dx, part_final, loss_tile = final_loss(x, _vec8([small["final_norm_g"]], D), target)
    loss = loss_tile[0, 0]

    conv_rows = [None] * N_A_LAYERS
    kv_rows = None
    mod_rows = [[None] * 3 for _ in range(DEPTH)]
    dkv_pairs = [{"k": [], "v": []} for _ in groups]
    slots = {}
    exchanges = []
    token = None

    def send_ready_chunks():
        nonlocal token
        for ci in reversed(range(len(comm.chunks))):
            if ci not in [e[0] for e in exchanges] and all(key in slots for key in comm.chunks[ci]):
                started = comm.start_exchange(ci, slots, token)
                exchanges.append((ci, started))
                token = started["token"]

    for layer in reversed(range(DEPTH)):
        rec = saved[layer]
        v1, v2, v3 = rec["vecs"]
        dx, slots[("ffn2_w_in", layer)], slots[("ffn2_w_out", layer)], mod_rows[layer][2] = _ffn_backward(
            dx, rec["ffn2"], after_token(v3, token), W[("ffn2_w_in", layer)], W[("ffn2_w_out", layer)])
        if layer < N_A_LAYERS:
            x_in, h, bcu, cv, z, y, cw = rec["mix"]
            dx, dy, dbcu, part, dcw = conv_bwd(dx, x_in, y, bcu, cv, v2, cw, W[("conv_w_in", layer)], W[("conv_w_out", layer)])
            slots[("conv_w_out", layer)] = grad_slots(z, dy, "conv_dw_out")
            slots[("conv_w_in", layer)] = grad_slots(h, dbcu, "conv_dw_in", col_slots=True)
            conv_rows[layer] = dcw[0:3]
            mod_rows[layer][1] = part[0:4]
        else:
            x_in, h, q, os_, ls, mixed, y = rec["mix"]
            outs = attn_mix_bwd(dx, y, v2, W[("attn_w_o", layer)], os_, ls, dils)
            ng = len(groups)
            dy, dos, rrs, part_gate = outs[0], outs[1:1 + ng], outs[1 + ng:1 + 2 * ng], outs[1 + 2 * ng]
            slots[("attn_w_o", layer)] = grad_slots(mixed, dy, "attn_dw_o", col_slots=True)
            dqs = []
            for g, (win, dil) in enumerate(groups):
                dq, dkc, dkp, dvc, dvp = attn_core_bwd(q[g], k_sh[g], v_sh[g], dos[g], rrs[g], ls[g], g, win // dil)
                dqs.append(dq)
                dkv_pairs[g]["k"].append((dkc, dkp))
                dkv_pairs[g]["v"].append((dvc, dvp))
            dx, dqr, part_norm = proj_rope_bwd(dqs, dils, x_in, dx, v2, W[("attn_w_q", layer)], tabs, qw, True, "q_bwd")
            slots[("attn_w_q", layer)] = grad_slots(dqr, h, "attn_dw_q")
            mod_rows[layer][1] = jnp.concatenate([part_norm[0:3], part_gate[0:1]], axis=0)
        send_ready_chunks()
        dx, slots[("ffn1_w_in", layer)], slots[("ffn1_w_out", layer)], mod_rows[layer][0] = _ffn_backward(
            dx, rec["ffn1"], after_token(v1, token), W[("ffn1_w_in", layer)], W[("ffn1_w_out", layer)])
        if layer == N_A_LAYERS:
            x_kv, h_kv, kv_vec = kv_saved
            dparts = [dkv_combine(dkv_pairs[g]["k"], win // dil, f"dk_combine_g{g}") for g, (win, dil) in enumerate(groups)]
            dparts += [dkv_combine(dkv_pairs[g]["v"], win // dil, f"dv_combine_g{g}") for g, (win, dil) in enumerate(groups)]
            dx, dkvp, part_kv = proj_rope_bwd(dparts, dils, x_kv, dx, kv_vec, W[("w_kv", layer)], tabs, qw, False, "kv_bwd")
            slots[("w_kv", layer)] = grad_slots(h_kv, dkvp, "kv_dw", col_slots=True)
            kv_rows = part_kv[0:3]
        send_ready_chunks()

    grads = {"conv_w": jnp.stack(conv_rows), "kv_rows": kv_rows, "exchanges": exchanges}
    grads["final_norm_g"] = part_final[0]
    rows = jnp.stack([jnp.stack(r) for r in mod_rows])
    grads["norm_g"] = rows[:, :, 0]
    grads["mods"] = rows[:, :, 1:4].reshape(DEPTH, N_MOD, D)
    return loss, dx, grads


def _flat2(a):
    return a.reshape(-1, a.shape[-1])


def _pad_rows(a, mult):
    r = a.shape[0]
    pad = (-r) % mult
    return a if pad == 0 else jnp.concatenate([a, jnp.zeros((pad,) + a.shape[1:], a.dtype)], axis=0)


def kernel(x, c, positions, norm_g, ada_w, ada_b, ffn1_w_in, ffn1_w_out, ffn2_w_in, ffn2_w_out, conv_w_in, conv_w, conv_w_out, kv_norm_g, kv_ada_w, kv_ada_b, w_kv, attn_w_q, attn_w_o, final_norm_g, loss_target, m_norm_g, m_ada_w, m_ada_b, m_ffn1_w_in, m_ffn1_w_out, m_ffn2_w_in, m_ffn2_w_out, m_conv_w_in, m_conv_w, m_conv_w_out, m_kv_norm_g, m_kv_ada_w, m_kv_ada_b, m_w_kv, m_attn_w_q, m_attn_w_o, m_final_norm_g, v_norm_g, v_ada_w, v_ada_b, v_ffn1_w_in, v_ffn1_w_out, v_ffn2_w_in, v_ffn2_w_out, v_conv_w_in, v_conv_w, v_conv_w_out, v_kv_norm_g, v_kv_ada_w, v_kv_ada_b, v_w_kv, v_attn_w_q, v_attn_w_o, v_final_norm_g):
    names = ("norm_g", "ada_w", "ada_b", "ffn1_w_in", "ffn1_w_out", "ffn2_w_in", "ffn2_w_out", "conv_w_in", "conv_w",
             "conv_w_out", "kv_norm_g", "kv_ada_w", "kv_ada_b", "w_kv", "attn_w_q", "attn_w_o", "final_norm_g")
    wts = dict(zip(names, (norm_g, ada_w, ada_b, ffn1_w_in, ffn1_w_out, ffn2_w_in, ffn2_w_out, conv_w_in, conv_w, conv_w_out,
                           kv_norm_g, kv_ada_w, kv_ada_b, w_kv, attn_w_q, attn_w_o, final_norm_g)))
    mom = dict(zip(names, (m_norm_g, m_ada_w, m_ada_b, m_ffn1_w_in, m_ffn1_w_out, m_ffn2_w_in, m_ffn2_w_out, m_conv_w_in,
                           m_conv_w, m_conv_w_out, m_kv_norm_g, m_kv_ada_w, m_kv_ada_b, m_w_kv, m_attn_w_q, m_attn_w_o,
                           m_final_norm_g)))
    var = dict(zip(names, (v_norm_g, v_ada_w, v_ada_b, v_ffn1_w_in, v_ffn1_w_out, v_ffn2_w_in, v_ffn2_w_out, v_conv_w_in,
                           v_conv_w, v_conv_w_out, v_kv_norm_g, v_kv_ada_w, v_kv_ada_b, v_w_kv, v_attn_w_q, v_attn_w_o,
                           v_final_norm_g)))
    T, D = x.shape[1], x.shape[2]
    me = _my_id()
    nmod = ada_w.shape[2]
    nkv = kv_ada_w.shape[1]

    def stacked(w, n):
        w = w if w.ndim == 3 else w[None]
        return jnp.swapaxes(w, 1, 2) if n in _TRANSPOSED else w

    comm = ChunkComm({n: stacked(wts[n], n) for n in _BIG})
    W = {}

    ds = norm_g.shape[2]
    small = jnp.concatenate([c.reshape(-1), norm_g.reshape(-1), conv_w.reshape(-1)]).astype(F32)
    n_small = small.shape[0]
    small = _pad_rows(small.reshape(-1, 1), 8 * LANES).reshape(-1, LANES)
    (small_all,) = all_gather([small], pltpu.VMEM, "gather_small")
    small_all = small_all.reshape(N_DEV, -1)[:, :n_small]
    c_all = small_all[:, :D]
    def full_rows(off, count):
        return jnp.stack([small_all[:, off + i * ds:off + (i + 1) * ds].reshape(D) for i in range(count)])

    W["norm_g"] = full_rows(D, DEPTH * 3).reshape(DEPTH, 3, D)
    W["conv_w"] = full_rows(D + DEPTH * 3 * ds, N_A_LAYERS * 3).reshape(N_A_LAYERS, 3, D)
    W["kv_norm_g"], W["final_norm_g"] = kv_norm_g, final_norm_g

    ada_b_mine = lax.dynamic_slice_in_dim(ada_b, me * nmod, nmod, axis=1).reshape(DEPTH, 1, nmod)
    kv_b_mine = lax.dynamic_slice_in_dim(kv_ada_b, me * nkv, nkv, axis=0).reshape(1, 1, nkv)
    mods_cols = mods_project(c_all, ada_w, ada_b_mine)
    kv_cols = mods_project(c_all, kv_ada_w.reshape(1, D, nkv), kv_b_mine)
    mcat = jnp.concatenate([mods_cols[l] for l in range(DEPTH)] + [kv_cols[0]], axis=1)
    wm = mcat.shape[1]
    if wm % LANES:
        mcat = jnp.concatenate([mcat, jnp.zeros((N_DEV, LANES - wm % LANES), F32)], axis=1)
    (mods_all,) = exchange_slots([mcat.reshape(N_DEV, 1, -1)], "exchange_mods")
    gather0 = comm.start_gather(0, mods_all)
    mods_all = mods_all.reshape(N_DEV, -1)
    mods = jnp.stack([mods_all[:, l * nmod:(l + 1) * nmod].reshape(N_MOD, D) for l in range(DEPTH)])
    kvmods = mods_all[:, DEPTH * nmod:DEPTH * nmod + nkv].reshape(2, D)

    loss_local, dx, grads = device_step(x[0], positions[0], loss_target[0], mods, kvmods, W, comm, gather0)
    loss = lax.psum(loss_local, MESH_AXES)

    dmods = grads["mods"].reshape(-1)
    dkvm = grads["kv_rows"][1:3].reshape(-1)
    vecs = jnp.concatenate([dmods, dkvm, grads["kv_rows"][0], grads["final_norm_g"], grads["norm_g"].reshape(-1),
                            grads["conv_w"].reshape(-1)])
    n_vec = vecs.shape[0]
    vecs = _pad_rows(vecs.reshape(-1, 1), 8 * LANES).reshape(-1, LANES)
    (vec_all,) = all_gather([vecs], pltpu.VMEM, "gather_vector_grads")
    vec_all = vec_all.reshape(N_DEV, -1)[:, :n_vec]
    nm_, nk_ = DEPTH * N_MOD * D, 2 * D
    dmods_all = vec_all[:, :nm_].reshape(N_DEV, DEPTH, N_MOD * D)
    dkvm_all = vec_all[:, nm_:nm_ + nk_]
    rest = vec_all[:, nm_ + nk_:]
    parts_kv_norm, parts_final = rest[:, :D].reshape(N_DEV, 1, D), rest[:, D:2 * D].reshape(N_DEV, 1, D)
    parts_norm = lax.dynamic_slice_in_dim(rest[:, 2 * D:2 * D + DEPTH * 3 * D].reshape(N_DEV, DEPTH * 3, D), me * ds, ds, axis=2)
    parts_conv = lax.dynamic_slice_in_dim(rest[:, 2 * D + DEPTH * 3 * D:].reshape(N_DEV, N_A_LAYERS * 3, D), me * ds, ds, axis=2)
    dm_cols = lax.dynamic_slice_in_dim(dmods_all, me * nmod, nmod, axis=2)
    dm_mine = jnp.stack([dm_cols[:, l] for l in range(DEPTH)])
    dkv_mine = lax.dynamic_slice_in_dim(dkvm_all, me * nkv, nkv, axis=1).reshape(1, N_DEV, nkv)
    g_ada_w = mods_weight_grad(c_all, dm_mine)
    g_kv_ada_w = mods_weight_grad(c_all, dkv_mine)[0]

    out_g, out_d, out_m, out_v = {}, {}, {}, {}

    def update(n, g, w, parts=False):
        shp = w.shape
        w2 = w.reshape(1, -1) if w.ndim == 1 else _flat2(w)
        g2 = g if parts else g.reshape(w2.shape)
        res = adam_update(g2, w2, mom[n].reshape(w2.shape), var[n].reshape(w2.shape), parts, "adam_" + n)
        out_g[n], out_d[n], out_m[n], out_v[n] = (r.reshape(shp) for r in res)

    moms = {n: stacked(mom[n], n) for n in _BIG}
    vars_ = {n: stacked(var[n], n) for n in _BIG}
    results = {}
    after = dx
    for ci, started in grads["exchanges"]:
        for (n, layer), parts in comm.finish_exchange(ci, started, after).items():
            idx = stacked_index(n, layer)
            results[n] = adam_layer(parts, comm.shards[n], moms[n], vars_[n], results.get(n), 0 if idx is None else idx,
                                    after, f"adam_{n}_{layer}")
            after = results[n][1]
    for n in _BIG:
        res = [jnp.swapaxes(r, 1, 2) if n in _TRANSPOSED else r for r in results[n]]
        out_g[n], out_d[n], out_m[n], out_v[n] = (r.reshape(wts[n].shape) for r in res)
    update("ada_w", g_ada_w, ada_w)
    update("kv_ada_w", g_kv_ada_w, kv_ada_w)
    update("ada_b", dmods_all, ada_b, True)
    update("kv_ada_b", dkvm_all.reshape(N_DEV, 1, nk_), kv_ada_b, True)
    update("kv_norm_g", parts_kv_norm, kv_norm_g, True)
    update("final_norm_g", parts_final, final_norm_g, True)
    update("norm_g", parts_norm, norm_g, True)
    update("conv_w", parts_conv, conv_w, True)

    return (loss, dx.reshape(x.shape), *[out_g[n] for n in names], *[out_d[n] for n in names],
            *[out_m[n] for n in names], *[out_v[n] for n in names])
```

```python
import functools

import jax
import jax.numpy as jnp
from jax import lax
from jax.experimental import pallas as pl
from jax.experimental.pallas import tpu as pltpu

F32, BF16 = jnp.float32, jnp.bfloat16

N_DEV = 8
MESH_AXES = ("x", "y", "c")
DEPTH = 4
N_A_LAYERS = 2
HEAD_DIM = 64
HEADS_PER_GROUP = 8
GROUP_WIDTH = HEAD_DIM * HEADS_PER_GROUP
DILATED_GROUPS = ((128, 1), (512, 4), (2048, 16))
ROPE_DIM = HEAD_DIM // 4
ROPE_THETA = 500000.0
NORM_EPS = 1e-5
FFN_RES_WEIGHT = 0.5
N_MOD = 9
ADAM_LR, ADAM_B1, ADAM_B2, ADAM_EPS, ADAM_WD, ADAM_STEP = 0.001, 0.9, 0.999, 1e-08, 0.01, 10

LANES = 128
TOKEN_TILE = 512
FFN_BWD_TILE = 256
CONTRACT_TILE = 2048
MXU_WIDTH = 256
VMEM_LIMIT = 56 * 1024 * 1024
MESH = pl.DeviceIdType.MESH


def _cp(*sem):
    return pltpu.CompilerParams(dimension_semantics=sem, vmem_limit_bytes=VMEM_LIMIT)


def _pick(n, cap, mult=LANES):
    if n <= cap:
        return n
    best = None
    for t in range(mult, cap + 1, mult):
        if n % t == 0:
            best = t
    assert best is not None, (n, cap)
    return best


def _tok(tm, w):
    return pl.BlockSpec((tm, w), lambda i: (i, 0))


def _res(shape):
    nd = len(shape)
    return pl.BlockSpec(shape, lambda *_: (0,) * nd, pipeline_mode=pl.Buffered(1))


def _sds(shape, dt):
    return jax.ShapeDtypeStruct(shape, dt)


def _sigmoid(a):
    return 1.0 / (1.0 + jnp.exp(-a))


def _modnorm(x, g, sh, sc):
    r = lax.rsqrt(jnp.mean(x * x, axis=-1, keepdims=True) + NORM_EPS)
    return (x * r * g) * (1.0 + sc) + sh


def _dot(a, b):
    return jnp.dot(a, b, preferred_element_type=F32)


def _dot_nt(a, b):
    return lax.dot_general(a, b, (((1,), (1,)), ((), ())), preferred_element_type=F32)


def _dot_tn(a, b):
    return lax.dot_general(a, b, (((0,), (0,)), ((), ())), preferred_element_type=F32)


def _rows8(rows, d):
    pad = 8 - len(rows)
    return jnp.concatenate(list(rows) + [jnp.zeros((pad, d), F32)], axis=0)


def _acc_rows(ref, tile, first):
    @pl.when(first)
    def _():
        ref[...] = tile

    @pl.when(jnp.logical_not(first))
    def _():
        ref[...] += tile


def ffn_up(x, vec, w_in_t):
    T, D = x.shape
    F = w_in_t.shape[0] // 2
    tm, cw = min(TOKEN_TILE, T), _pick(F, MXU_WIDTH)

    def body(x_ref, vec_ref, w_ref, h_ref, ga_ref, gb_ref, u_ref):
        hb = _modnorm(x_ref[...], vec_ref[0:1], vec_ref[1:2], vec_ref[2:3]).astype(BF16)
        h_ref[...] = hb
        for c in range(F // cw):
            lo, hi = c * cw, (c + 1) * cw
            a = _dot_nt(hb, w_ref[lo:hi, :])
            b = _dot_nt(hb, w_ref[F + lo:F + hi, :])
            sg = _sigmoid(a)
            silu = a * sg
            ga_ref[:, lo:hi] = (b * (sg + silu * (1.0 - sg))).astype(BF16)
            gb_ref[:, lo:hi] = silu.astype(BF16)
            u_ref[:, lo:hi] = (silu * b).astype(BF16)

    return pl.pallas_call(
        body, grid=(T // tm,),
        in_specs=[_tok(tm, D), _res((8, D)), _res((2 * F, D))],
        out_specs=[_tok(tm, D), _tok(tm, F), _tok(tm, F), _tok(tm, F)],
        out_shape=[_sds((T, D), BF16), _sds((T, F), BF16), _sds((T, F), BF16), _sds((T, F), BF16)],
        compiler_params=_cp("arbitrary"), name="ffn_up")(x, vec, w_in_t)


def proj_out(u, x, vec, w_out, res_weight, name):
    T, D = x.shape
    K = u.shape[1]
    tm = min(TOKEN_TILE, T)

    def body(u_ref, x_ref, vec_ref, w_ref, xn_ref, y_ref):
        y = _dot(u_ref[...], w_ref[...])
        y_ref[...] = y.astype(BF16)
        xn_ref[...] = x_ref[...] + (res_weight * (1.0 + vec_ref[3:4])) * y

    return pl.pallas_call(
        body, grid=(T // tm,),
        in_specs=[_tok(tm, K), _tok(tm, D), _res((8, D)), _res((K, D))],
        out_specs=[_tok(tm, D), _tok(tm, D)],
        out_shape=[_sds((T, D), F32), _sds((T, D), BF16)],
        compiler_params=_cp("arbitrary"), name=name)(u, x, vec, w_out)


def ffn_down_bwd(dxo, y, vec, w_out, a, b):
    T, D = dxo.shape
    F = a.shape[1]
    tm, cw = min(TOKEN_TILE, T), _pick(F, MXU_WIDTH)

    def body(dxo_ref, y_ref, vec_ref, w_ref, a_ref, b_ref, dy_ref, dab_ref, part_ref):
        dxo_t = dxo_ref[...]
        dyb = (dxo_t * (FFN_RES_WEIGHT * (1.0 + vec_ref[3:4]))).astype(BF16)
        dy_ref[...] = dyb
        dgate = FFN_RES_WEIGHT * jnp.sum(dxo_t * y_ref[...].astype(F32), axis=0, keepdims=True)
        _acc_rows(part_ref, _rows8([dgate], D), pl.program_id(0) == 0)
        for c in range(F // cw):
            lo, hi = c * cw, (c + 1) * cw
            du = _dot_nt(dyb, w_ref[lo:hi, :])
            dab_ref[:, lo:hi] = (du * a_ref[:, lo:hi].astype(F32)).astype(BF16)
            dab_ref[:, F + lo:F + hi] = (du * b_ref[:, lo:hi].astype(F32)).astype(BF16)

    return pl.pallas_call(
        body, grid=(T // tm,),
        in_specs=[_tok(tm, D), _tok(tm, D), _res((8, D)), _res((F, D)), _tok(tm, F), _tok(tm, F)],
        out_specs=[_tok(tm, D), _tok(tm, 2 * F), pl.BlockSpec((8, D), lambda i: (0, 0))],
        out_shape=[_sds((T, D), BF16), _sds((T, 2 * F), BF16), _sds((8, D), F32)],
        compiler_params=_cp("arbitrary"), name="ffn_down_bwd")(dxo, y, vec, w_out, a, b)


def ffn_up_bwd(dab, w_in_t, x, dxo, vec):
    T, D = x.shape
    F2 = dab.shape[1]
    tm = min(TOKEN_TILE, T)

    def body(dab_ref, w_ref, x_ref, dxo_ref, vec_ref, dx_ref, part_ref):
        dh = _dot(dab_ref[...], w_ref[...])
        _, vjp = jax.vjp(_modnorm, x_ref[...], vec_ref[0:1], vec_ref[1:2], vec_ref[2:3])
        dx, dg, dsh, dsc = vjp(dh)
        dx_ref[...] = dxo_ref[...] + dx
        _acc_rows(part_ref, _rows8([dg, dsh, dsc], D), pl.program_id(0) == 0)

    return pl.pallas_call(
        body, grid=(T // tm,),
        in_specs=[_tok(tm, F2), _res((F2, D)), _tok(tm, D), _tok(tm, D), _res((8, D))],
        out_specs=[_tok(tm, D), pl.BlockSpec((8, D), lambda i: (0, 0))],
        out_shape=[_sds((T, D), F32), _sds((8, D), F32)],
        compiler_params=_cp("arbitrary"), name="ffn_up_bwd")(dab, w_in_t, x, dxo, vec)


def ffn_bwd(dxo, y, vec, w_out, w_in_t, a, b, x):
    T, D = x.shape
    F = a.shape[1]
    tm, cw = min(FFN_BWD_TILE, T), _pick(F, MXU_WIDTH)

    def body(dxo_ref, y_ref, vec_ref, wo_ref, wi_ref, a_ref, b_ref, x_ref, dy_ref, dab_ref, dx_ref, part_ref):
        dxo_t = dxo_ref[...]
        dyb = (dxo_t * (FFN_RES_WEIGHT * (1.0 + vec_ref[3:4]))).astype(BF16)
        dy_ref[...] = dyb
        dgate = FFN_RES_WEIGHT * jnp.sum(dxo_t * y_ref[...].astype(F32), axis=0, keepdims=True)
        for c in range(F // cw):
            lo, hi = c * cw, (c + 1) * cw
            du = _dot_nt(dyb, wo_ref[lo:hi, :])
            dab_ref[:, lo:hi] = (du * a_ref[:, lo:hi].astype(F32)).astype(BF16)
            dab_ref[:, F + lo:F + hi] = (du * b_ref[:, lo:hi].astype(F32)).astype(BF16)
        dh = _dot(dab_ref[...], wi_ref[...])
        _, vjp = jax.vjp(_modnorm, x_ref[...], vec_ref[0:1], vec_ref[1:2], vec_ref[2:3])
        dx, dg, dsh, dsc = vjp(dh)
        dx_ref[...] = dxo_t + dx
        _acc_rows(part_ref, _rows8([dg, dsh, dsc, dgate], D), pl.program_id(0) == 0)

    return pl.pallas_call(
        body, grid=(T // tm,),
        in_specs=[_tok(tm, D), _tok(tm, D), _res((8, D)), _res((F, D)), _res((2 * F, D)), _tok(tm, F), _tok(tm, F), _tok(tm, D)],
        out_specs=[_tok(tm, D), _tok(tm, 2 * F), _tok(tm, D), pl.BlockSpec((8, D), lambda i: (0, 0))],
        out_shape=[_sds((T, D), BF16), _sds((T, 2 * F), BF16), _sds((T, D), F32), _sds((8, D), F32)],
        compiler_params=_cp("arbitrary"), name="ffn_bwd")(dxo, y, vec, w_out, w_in_t, a, b, x)


def grad_slots(a, b, name, col_slots=False):
    T, M = a.shape
    N = b.shape[1]
    tk = min(CONTRACT_TILE, T)
    nk = T // tk
    tmm = _pick(M, 1408)
    if col_slots:
        ns = N // N_DEV
        sp = max(s for s in (1, 2, 4, 8) if ns * s <= 1536)
        tn = ns * sp
    else:
        tn = _pick(N, 1536)

    def body(a_ref, b_ref, o_ref, acc):
        k = pl.program_id(2)
        t = _dot_tn(a_ref[...], b_ref[...])

        @pl.when(k == 0)
        def _():
            acc[...] = t

        @pl.when(k > 0)
        def _():
            acc[...] += t

        @pl.when(k == nk - 1)
        def _():
            if col_slots:
                for s in range(sp):
                    o_ref[s] = acc[:, s * ns:(s + 1) * ns].astype(BF16)
            else:
                o_ref[...] = acc[...].astype(BF16)

    if col_slots:
        out_spec, out_shape = pl.BlockSpec((sp, tmm, ns), lambda i, j, k: (j, i, 0)), _sds((N_DEV, M, ns), BF16)
    else:
        out_spec, out_shape = pl.BlockSpec((tmm, tn), lambda i, j, k: (i, j)), _sds((M, N), BF16)
    out = pl.pallas_call(
        body, grid=(M // tmm, N // tn, nk),
        in_specs=[pl.BlockSpec((tk, tmm), lambda i, j, k: (k, i)), pl.BlockSpec((tk, tn), lambda i, j, k: (k, j))],
        out_specs=out_spec, out_shape=out_shape,
        scratch_shapes=[pltpu.VMEM((tmm, tn), F32)],
        compiler_params=_cp("arbitrary", "arbitrary", "arbitrary"), name=name)(a, b)
    return out if col_slots else out.reshape(N_DEV, M // N_DEV, N)


def conv_fwd(x, vec, cw, w_in, w_out):
    T, D = x.shape
    tm = min(TOKEN_TILE, T)

    def body(x_ref, vec_ref, cw_ref, wi_ref, wo_ref, xn_ref, h_ref, bcu_ref, cv_ref, z_ref, y_ref, vbuf):
        @pl.when(pl.program_id(0) == 0)
        def _():
            vbuf[0:8, :] = jnp.zeros((8, D), F32)

        x_t = x_ref[...]
        hb = _modnorm(x_t, vec_ref[0:1], vec_ref[1:2], vec_ref[2:3]).astype(BF16)
        h_ref[...] = hb
        bcu = _dot(hb, wi_ref[...])
        bcu_ref[...] = bcu.astype(BF16)
        bg, v = bcu[:, 0:D], bcu[:, D:2 * D] * bcu[:, 2 * D:3 * D]
        vbuf[8:8 + tm, :] = v
        conv = cw_ref[0:1] * vbuf[6:6 + tm, :] + cw_ref[1:2] * vbuf[7:7 + tm, :] + cw_ref[2:3] * v
        cv_ref[...] = conv.astype(BF16)
        zb = (bg * conv).astype(BF16)
        z_ref[...] = zb
        y = _dot(zb, wo_ref[...])
        y_ref[...] = y.astype(BF16)
        xn_ref[...] = x_t + (1.0 + vec_ref[3:4]) * y
        vbuf[0:8, :] = vbuf[tm:tm + 8, :]

    return pl.pallas_call(
        body, grid=(T // tm,),
        in_specs=[_tok(tm, D), _res((8, D)), _res((8, D)), _res((D, 3 * D)), _res((D, D))],
        out_specs=[_tok(tm, D), _tok(tm, D), _tok(tm, 3 * D), _tok(tm, D), _tok(tm, D), _tok(tm, D)],
        out_shape=[_sds((T, D), F32), _sds((T, D), BF16), _sds((T, 3 * D), BF16), _sds((T, D), BF16),
                   _sds((T, D), BF16), _sds((T, D), BF16)],
        scratch_shapes=[pltpu.VMEM((tm + 8, D), F32)],
        compiler_params=_cp("arbitrary"), name="conv_fwd")(x, vec, cw, w_in, w_out)


def conv_bwd(dxo, x, y, bcu, cv, vec, cw, w_in, w_out):
    T, D = x.shape
    tm = min(TOKEN_TILE, T)
    nt = T // tm

    def body(dxo_ref, x_ref, y_ref, bcu_ref, cv_ref, vec_ref, cw_ref, wi_ref, wo_ref,
             dx_ref, dy_ref, dbcu_ref, part_ref, dcw_ref, dcbuf):
        first = pl.program_id(0) == 0

        @pl.when(first)
        def _():
            dcbuf[tm:tm + 8, :] = jnp.zeros((8, D), F32)

        dxo_t = dxo_ref[...]
        dyb = (dxo_t * (1.0 + vec_ref[3:4])).astype(BF16)
        dy_ref[...] = dyb
        dgate = jnp.sum(dxo_t * y_ref[...].astype(F32), axis=0, keepdims=True)
        dz = _dot_nt(dyb, wo_ref[...])
        bcu_t = bcu_ref[...].astype(F32)
        bg, cg, ug = bcu_t[:, 0:D], bcu_t[:, D:2 * D], bcu_t[:, 2 * D:3 * D]
        dconv = dz * bg
        dbg = dz * cv_ref[...].astype(F32)
        dcbuf[0:tm, :] = dconv
        d1, d2 = dcbuf[1:tm + 1, :], dcbuf[2:tm + 2, :]
        dv = cw_ref[2:3] * dconv + cw_ref[1:2] * d1 + cw_ref[0:1] * d2
        v = cg * ug
        dcw = _rows8([jnp.sum(d2 * v, axis=0, keepdims=True), jnp.sum(d1 * v, axis=0, keepdims=True),
                      jnp.sum(dconv * v, axis=0, keepdims=True)], D)
        dbcu = jnp.concatenate([dbg, dv * ug, dv * cg], axis=1).astype(BF16)
        dbcu_ref[...] = dbcu
        dh = _dot_nt(dbcu, wi_ref[...])
        _, vjp = jax.vjp(_modnorm, x_ref[...], vec_ref[0:1], vec_ref[1:2], vec_ref[2:3])
        dx, dg, dsh, dsc = vjp(dh)
        dx_ref[...] = dxo_t + dx
        _acc_rows(part_ref, _rows8([dg, dsh, dsc, dgate], D), first)
        _acc_rows(dcw_ref, dcw, first)
        dcbuf[tm:tm + 8, :] = dcbuf[0:8, :]

    def rev(w):
        return pl.BlockSpec((tm, w), lambda i: (nt - 1 - i, 0))

    return pl.pallas_call(
        body, grid=(nt,),
        in_specs=[rev(D), rev(D), rev(D), rev(3 * D), rev(D), _res((8, D)), _res((8, D)), _res((D, 3 * D)), _res((D, D))],
        out_specs=[rev(D), rev(D), rev(3 * D), pl.BlockSpec((8, D), lambda i: (0, 0)), pl.BlockSpec((8, D), lambda i: (0, 0))],
        out_shape=[_sds((T, D), F32), _sds((T, D), BF16), _sds((T, 3 * D), BF16), _sds((8, D), F32), _sds((8, D), F32)],
        scratch_shapes=[pltpu.VMEM((tm + 8, D), F32)],
        compiler_params=_cp("arbitrary"), name="conv_bwd")(dxo, x, y, bcu, cv, vec, cw, w_in, w_out)


def rope_tables(pos, lane_rows):
    T = pos.shape[0]
    tm = min(TOKEN_TILE, T)

    def body(p_ref, lr_ref, c_ref, sp_ref, sm_ref):
        ang = p_ref[...].astype(F32) * lr_ref[0:1]
        cs, sn = jnp.cos(ang), jnp.sin(ang)
        c_ref[...] = jnp.where(lr_ref[1:2] > 0.5, cs, 1.0)
        sp_ref[...] = jnp.where(lr_ref[2:3] > 0.5, sn, 0.0)
        sm_ref[...] = jnp.where(lr_ref[3:4] > 0.5, -sn, 0.0)

    return pl.pallas_call(
        body, grid=(T // tm,),
        in_specs=[_tok(tm, 1), _res((8, LANES))],
        out_specs=[_tok(tm, LANES)] * 3,
        out_shape=[_sds((T, LANES), F32)] * 3,
        compiler_params=_cp("arbitrary"), name="rope_tables")(pos, lane_rows)


def _rope(t, c, sp, sm):
    w = t.shape[1]
    reps = w // LANES
    cf, spf, smf = jnp.tile(c, (1, reps)), jnp.tile(sp, (1, reps)), jnp.tile(sm, (1, reps))
    half = ROPE_DIM // 2
    return t * cf + pltpu.roll(t, half, axis=1) * spf + pltpu.roll(t, w - half, axis=1) * smf


def _rope_t(d, c, sp, sm):
    w = d.shape[1]
    reps = w // LANES
    cf, spf, smf = jnp.tile(c, (1, reps)), jnp.tile(sp, (1, reps)), jnp.tile(sm, (1, reps))
    half = ROPE_DIM // 2
    return d * cf + pltpu.roll(d * spf, w - half, axis=1) + pltpu.roll(d * smf, half, axis=1)


def _split_residues(v, d, stage):
    tm, width = v.shape
    if d == 1:
        return [v]
    nj = width // LANES
    for j in range(nj):
        stage[j] = v[:, j * LANES:(j + 1) * LANES]
    return [jnp.concatenate([stage[j, pl.ds(r, tm // d, stride=d), :] for j in range(nj)], axis=1) for r in range(d)]


def _merge_residues(piece, d, tm, width, stage):
    if d == 1:
        return piece(0)
    nj = width // LANES
    for r in range(d):
        p = piece(r)
        for j in range(nj):
            stage[j, pl.ds(r, tm // d, stride=d), :] = p[:, j * LANES:(j + 1) * LANES]
    return jnp.concatenate([stage[j] for j in range(nj)], axis=1)


def _residue_spec(d, tm):
    return pl.BlockSpec((d, tm // d, GROUP_WIDTH), lambda i: (0, i, 0))


def _stage_scratch(tm):
    return pltpu.VMEM((GROUP_WIDTH // LANES, tm, LANES), F32)


def proj_rope_fwd(x, vec, w, tabs, n_rope, transposed, dils, name):
    T, D = x.shape
    N = w.shape[0] if transposed else w.shape[1]
    tm = min(TOKEN_TILE, T)
    GW = GROUP_WIDTH
    piece_dils = [dils[j % len(dils)] for j in range(N // GW)]

    def body(x_ref, vec_ref, w_ref, c_ref, sp_ref, sm_ref, h_ref, *rest):
        out_refs, stage = rest[:-1], rest[-1]
        hb = _modnorm(x_ref[...], vec_ref[0:1], vec_ref[1:2], vec_ref[2:3]).astype(BF16)
        h_ref[...] = hb
        p = _dot_nt(hb, w_ref[...]) if transposed else _dot(hb, w_ref[...])
        pr = _rope(p[:, 0:n_rope], c_ref[...], sp_ref[...], sm_ref[...])
        for j, d in enumerate(piece_dils):
            src = pr if (j + 1) * GW <= n_rope else p
            for r, rows in enumerate(_split_residues(src[:, j * GW:(j + 1) * GW], d, stage)):
                out_refs[j][r] = rows.astype(BF16)

    return pl.pallas_call(
        body, grid=(T // tm,),
        in_specs=[_tok(tm, D), _res((8, D)), _res(w.shape)] + [_tok(tm, LANES)] * 3,
        out_specs=[_tok(tm, D)] + [_residue_spec(d, tm) for d in piece_dils],
        out_shape=[_sds((T, D), BF16)] + [_sds((d, T // d, GW), BF16) for d in piece_dils],
        scratch_shapes=[_stage_scratch(tm)],
        compiler_params=_cp("arbitrary"), name=name)(x, vec, w, *tabs)


def proj_rope_bwd(dparts, dils, x, dxo, vec, w, tabs, n_rope, transposed, name):
    T, D = x.shape
    N = w.shape[0] if transposed else w.shape[1]
    tm = min(TOKEN_TILE, T)
    GW = GROUP_WIDTH
    npart = len(dparts)
    piece_dils = [dils[j % len(dils)] for j in range(npart)]

    def body(*refs):
        d_refs = refs[:npart]
        x_ref, dxo_ref, vec_ref, w_ref, c_ref, sp_ref, sm_ref, dx_ref, dp_ref, part_ref, stage = refs[npart:]
        d = jnp.concatenate([_merge_residues(lambda r, ref=ref: ref[r].astype(F32), dd, tm, GW, stage)
                             for ref, dd in zip(d_refs, piece_dils)], axis=1)
        dr = _rope_t(d[:, 0:n_rope], c_ref[...], sp_ref[...], sm_ref[...])
        if n_rope < N:
            dr = jnp.concatenate([dr, d[:, n_rope:N]], axis=1)
        dpb = dr.astype(BF16)
        dp_ref[...] = dpb
        dh = _dot(dpb, w_ref[...]) if transposed else _dot_nt(dpb, w_ref[...])
        _, vjp = jax.vjp(_modnorm, x_ref[...], vec_ref[0:1], vec_ref[1:2], vec_ref[2:3])
        dx, dg, dsh, dsc = vjp(dh)
        dx_ref[...] = dxo_ref[...] + dx
        _acc_rows(part_ref, _rows8([dg, dsh, dsc], D), pl.program_id(0) == 0)

    return pl.pallas_call(
        body, grid=(T // tm,),
        in_specs=[_residue_spec(d, tm) for d in piece_dils] + [_tok(tm, D), _tok(tm, D), _res((8, D)), _res(w.shape)]
        + [_tok(tm, LANES)] * 3,
        out_specs=[_tok(tm, D), _tok(tm, N), pl.BlockSpec((8, D), lambda i: (0, 0))],
        out_shape=[_sds((T, D), F32), _sds((T, N), BF16), _sds((8, D), F32)],
        scratch_shapes=[_stage_scratch(tm)],
        compiler_params=_cp("arbitrary"), name=name)(*dparts, x, dxo, vec, w, *tabs)


def _valid_mask(n, i):
    qi = lax.broadcasted_iota(jnp.int32, (n, 2 * n), 0)
    kj = lax.broadcasted_iota(jnp.int32, (n, 2 * n), 1)
    dist = n + qi - kj
    return (dist >= 0) & (dist <= n) & ((kj >= n) | (i > 0))


def _band_specs(n):
    two = pl.BlockSpec((None, 2 * n, GROUP_WIDTH), lambda r, i: (r, i, 0))
    prv = pl.BlockSpec((None, n, GROUP_WIDTH), lambda r, i: (r, jnp.maximum(2 * i - 1, 0), 0))
    one = pl.BlockSpec((None, n, GROUP_WIDTH), lambda r, i: (r, i, 0))
    return two, prv, one


def _pair_keys(prev_ref, two_ref, ps, n):
    cur2 = two_ref[:, ps]
    return jnp.concatenate([prev_ref[:, ps], cur2[0:n]], axis=0), cur2


def attn_core_fwd(q, k, v, g, n):
    d, M, GW = q.shape
    scale = HEAD_DIM ** -0.5

    def body(q_ref, kp_ref, kc_ref, vp_ref, vc_ref, o_ref, l_ref):
        masks = (_valid_mask(n, pl.program_id(1)), _valid_mask(n, 1))
        first = lax.broadcasted_iota(jnp.int32, (1, LANES), 1) < HEAD_DIM
        for pair in range(HEADS_PER_GROUP * HEAD_DIM // LANES):
            ps = slice(LANES * pair, LANES * (pair + 1))
            keys, vals = _pair_keys(kp_ref, kc_ref, ps, n), _pair_keys(vp_ref, vc_ref, ps, n)
            for blk in range(2):
                rows = slice(blk * n, (blk + 1) * n)
                q2 = q_ref[rows, ps]
                o2, l2 = [], []
                for sel in (first, jnp.logical_not(first)):
                    s = jnp.where(masks[blk], _dot_nt(jnp.where(sel, q2, jnp.zeros_like(q2)), keys[blk]) * scale, -1e30)
                    m = jnp.max(s, axis=1, keepdims=True)
                    p = jnp.exp(s - m)
                    den = jnp.sum(p, axis=1, keepdims=True)
                    o2.append(_dot((p / den).astype(BF16), vals[blk]))
                    l2.append(m + jnp.log(den))
                o_ref[rows, ps] = jnp.where(first, o2[0], o2[1]).astype(BF16)
                l_ref[rows, ps] = jnp.where(first, l2[0], l2[1])

    two, prv, _ = _band_specs(n)
    return pl.pallas_call(
        body, grid=(d, M // (2 * n)),
        in_specs=[two, prv, two, prv, two], out_specs=[two, two],
        out_shape=[_sds((d, M, GW), BF16), _sds((d, M, GW), F32)],
        compiler_params=_cp("arbitrary", "arbitrary"), name=f"attn_fwd_g{g}")(q, k, k, v, v)


def attn_core_bwd(q, k, v, do, rr, lse, g, n):
    d, M, GW = q.shape
    scale = HEAD_DIM ** -0.5

    def body(q_ref, kp_ref, kc_ref, vp_ref, vc_ref, do_ref, r_ref, l_ref, dq_ref, dkc_ref, dkp_ref, dvc_ref, dvp_ref):
        masks = (_valid_mask(n, pl.program_id(1)), _valid_mask(n, 1))
        first = lax.broadcasted_iota(jnp.int32, (1, LANES), 1) < HEAD_DIM
        for pair in range(HEADS_PER_GROUP * HEAD_DIM // LANES):
            ps = slice(LANES * pair, LANES * (pair + 1))
            keys, vals = _pair_keys(kp_ref, kc_ref, ps, n), _pair_keys(vp_ref, vc_ref, ps, n)
            own = []
            for blk in range(2):
                rows = slice(blk * n, (blk + 1) * n)
                q2, do2, r2 = q_ref[rows, ps], do_ref[rows, ps], r_ref[rows, ps]
                dq2, dk, dv = [], None, None
                for half, sel in enumerate((first, jnp.logical_not(first))):
                    qm = jnp.where(sel, q2, jnp.zeros_like(q2))
                    dom = jnp.where(sel, do2, jnp.zeros_like(do2))
                    s = jnp.where(masks[blk], _dot_nt(qm, keys[blk]) * scale, -1e30)
                    lane0 = LANES * pair + HEAD_DIM * half
                    p = jnp.exp(s - l_ref[rows, lane0:lane0 + 1])
                    dp = _dot_nt(dom, vals[blk])
                    delta = jnp.sum(jnp.where(sel, r2, 0.0), axis=1, keepdims=True)
                    ds = (p * (dp - delta) * scale).astype(BF16)
                    dq2.append(_dot(ds, keys[blk]))
                    dkh = _dot_tn(ds, qm)
                    dvh = _dot_tn(p.astype(BF16), dom)
                    dk = dkh if dk is None else dk + dkh
                    dv = dvh if dv is None else dv + dvh
                dq_ref[rows, ps] = jnp.where(first, dq2[0], dq2[1]).astype(BF16)
                own.append((dk, dv))
            for t, (c_ref, p_ref) in enumerate(((dkc_ref, dkp_ref), (dvc_ref, dvp_ref))):
                a, b = own[0][t], own[1][t]
                p_ref[:, ps] = a[0:n].astype(BF16)
                c_ref[0:n, ps] = (a[n:2 * n] + b[0:n]).astype(BF16)
                c_ref[n:2 * n, ps] = b[n:2 * n].astype(BF16)

    two, prv, one = _band_specs(n)
    return pl.pallas_call(
        body, grid=(d, M // (2 * n)),
        in_specs=[two, prv, two, prv, two, two, two, two], out_specs=[two, two, one, two, one],
        out_shape=[_sds((d, M, GW), BF16), _sds((d, M, GW), BF16), _sds((d, M // 2, GW), BF16),
                   _sds((d, M, GW), BF16), _sds((d, M // 2, GW), BF16)],
        compiler_params=_cp("arbitrary", "arbitrary"), name=f"attn_bwd_g{g}")(q, k, k, v, v, do, rr, lse)


def dkv_combine(cur_prev, n, name):
    d, M, GW = cur_prev[0][0].shape
    rows = min(M, 1024)
    pairs = rows // (2 * n)
    steps = M // rows
    flat = [a for pair in cur_prev for a in pair]

    def body(*refs):
        o_ref = refs[-1]
        last = pl.program_id(1) == steps - 1
        acc = None
        shifted = None
        for t in range(0, len(refs) - 1, 3):
            c = refs[t][...].astype(F32)
            nxt = jnp.where(last, 0.0, refs[t + 2][...].astype(F32))
            s = nxt if pairs == 1 else jnp.concatenate([refs[t + 1][n:pairs * n, :].astype(F32), nxt], axis=0)
            acc = c if acc is None else acc + c
            shifted = s if shifted is None else shifted + s
        for m in range(pairs):
            lo = 2 * m * n
            o_ref[lo:lo + n, :] = acc[lo:lo + n].astype(BF16)
            o_ref[lo + n:lo + 2 * n, :] = (acc[lo + n:lo + 2 * n] + shifted[m * n:(m + 1) * n]).astype(BF16)

    cur = pl.BlockSpec((None, rows, GW), lambda r, i: (r, i, 0))
    same = pl.BlockSpec((None, pairs * n, GW), lambda r, i: (r, i, 0))
    nxt = pl.BlockSpec((None, n, GW), lambda r, i: (r, jnp.minimum((i + 1) * pairs, M // (2 * n) - 1), 0))
    args = []
    for c, p in cur_prev:
        args += [c, p, p]
    return pl.pallas_call(
        body, grid=(d, steps), in_specs=[cur, same, nxt] * len(cur_prev), out_specs=cur,
        out_shape=_sds((d, M, GW), BF16),
        compiler_params=_cp("arbitrary", "arbitrary"), name=name)(*args)


def _group_weights(ls):
    mx = functools.reduce(jnp.maximum, ls)
    es = [jnp.exp(l - mx) for l in ls]
    tot = functools.reduce(lambda a, b: a + b, es)
    return [e / tot for e in es]


def attn_mix_out(os_, ls, dils, x, vec, w_o):
    T, D = x.shape
    GW = GROUP_WIDTH
    tm = min(TOKEN_TILE, T)
    ng = len(os_)

    def body(*refs):
        o_refs, l_refs = refs[:ng], refs[ng:2 * ng]
        x_ref, vec_ref, w_ref, xn_ref, mix_ref, y_ref, stage = refs[2 * ng:]
        natural = lambda ref, d: _merge_residues(lambda r: ref[r].astype(F32), d, tm, GW, stage)
        ws = _group_weights([natural(r, d) for r, d in zip(l_refs, dils)])
        mixed = functools.reduce(lambda a, b: a + b, [w * natural(r, d) for w, r, d in zip(ws, o_refs, dils)])
        mb = mixed.astype(BF16)
        mix_ref[...] = mb
        y = _dot(mb, w_ref[...])
        y_ref[...] = y.astype(BF16)
        xn_ref[...] = x_ref[...] + (1.0 + vec_ref[3:4]) * y

    res = [_residue_spec(d, tm) for d in dils]
    return pl.pallas_call(
        body, grid=(T // tm,),
        in_specs=res + res + [_tok(tm, D), _res((8, D)), _res((GW, D))],
        out_specs=[_tok(tm, D), _tok(tm, GW), _tok(tm, D)],
        out_shape=[_sds((T, D), F32), _sds((T, GW), BF16), _sds((T, D), BF16)],
        scratch_shapes=[_stage_scratch(tm)],
        compiler_params=_cp("arbitrary"), name="attn_mix_out")(*os_, *ls, x, vec, w_o)


def attn_mix_bwd(dxo, y, vec, w_o, os_, ls, dils):
    T, D = dxo.shape
    GW = GROUP_WIDTH
    tm = min(TOKEN_TILE, T)
    ng = len(os_)

    def body(*refs):
        dxo_ref, y_ref, vec_ref, w_ref = refs[:4]
        o_refs, l_refs = refs[4:4 + ng], refs[4 + ng:4 + 2 * ng]
        dy_ref = refs[4 + 2 * ng]
        do_refs = refs[5 + 2 * ng:5 + 3 * ng]
        r_refs = refs[5 + 3 * ng:5 + 4 * ng]
        part_ref, stage = refs[5 + 4 * ng], refs[6 + 4 * ng]
        natural = lambda ref, d: _merge_residues(lambda r: ref[r].astype(F32), d, tm, GW, stage)
        dxo_t = dxo_ref[...]
        dyb = (dxo_t * (1.0 + vec_ref[3:4])).astype(BF16)
        dy_ref[...] = dyb
        dgate = jnp.sum(dxo_t * y_ref[...].astype(F32), axis=0, keepdims=True)
        _acc_rows(part_ref, _rows8([dgate], D), pl.program_id(0) == 0)
        dmix = _dot_nt(dyb, w_ref[...])
        ws = _group_weights([natural(r, d) for r, d in zip(l_refs, dils)])
        mixed = functools.reduce(lambda a, b: a + b, [w * natural(r, d) for w, r, d in zip(ws, o_refs, dils)])
        for gi in range(ng):
            do = ws[gi] * dmix
            for r, rows in enumerate(_split_residues(do, dils[gi], stage)):
                do_refs[gi][r] = rows.astype(BF16)
            for r, rows in enumerate(_split_residues(do * mixed, dils[gi], stage)):
                r_refs[gi][r] = rows

    res = [_residue_spec(d, tm) for d in dils]
    return pl.pallas_call(
        body, grid=(T // tm,),
        in_specs=[_tok(tm, D), _tok(tm, D), _res((8, D)), _res((GW, D))] + res + res,
        out_specs=[_tok(tm, D)] + res + res + [pl.BlockSpec((8, D), lambda i: (0, 0))],
        out_shape=[_sds((T, D), BF16)] + [_sds((d, T // d, GW), BF16) for d in dils]
        + [_sds((d, T // d, GW), F32) for d in dils] + [_sds((8, D), F32)],
        scratch_shapes=[_stage_scratch(tm)],
        compiler_params=_cp("arbitrary"), name="attn_mix_bwd")(dxo, y, vec, w_o, *os_, *ls)


def final_loss(x, gvec, target):
    T, D = x.shape
    tm = min(TOKEN_TILE, T)

    def norm(xv, g):
        return xv * lax.rsqrt(jnp.mean(xv * xv, axis=-1, keepdims=True) + NORM_EPS) * g

    def body(x_ref, g_ref, t_ref, dx_ref, part_ref, loss_ref):
        first = pl.program_id(0) == 0
        yv, vjp = jax.vjp(norm, x_ref[...], g_ref[0:1])
        err = yv - t_ref[...]
        dx, dg = vjp(err * (1.0 / D))
        dx_ref[...] = dx
        _acc_rows(part_ref, _rows8([dg], D), first)
        tile_loss = 0.5 * jnp.sum(jnp.sum(err * err, axis=1, keepdims=True) * (1.0 / D), axis=0, keepdims=True)
        _acc_rows(loss_ref, jnp.broadcast_to(tile_loss, (8, LANES)), first)

    return pl.pallas_call(
        body, grid=(T // tm,),
        in_specs=[_tok(tm, D), _res((8, D)), _tok(tm, D)],
        out_specs=[_tok(tm, D), pl.BlockSpec((8, D), lambda i: (0, 0)), pl.BlockSpec((8, LANES), lambda i: (0, 0))],
        out_shape=[_sds((T, D), F32), _sds((8, D), F32), _sds((8, LANES), F32)],
        compiler_params=_cp("arbitrary"), name="final_loss")(x, gvec, target)


def mods_project(c_all, w, b):
    B, D = c_all.shape
    L, _, N = w.shape

    def body(c_ref, w_ref, b_ref, o_ref):
        cv = c_ref[...]
        cond = cv * _sigmoid(cv)
        o_ref[0] = jnp.dot(cond, w_ref[0], preferred_element_type=F32, precision=lax.Precision.HIGHEST) + b_ref[0]

    return pl.pallas_call(
        body, grid=(L,),
        in_specs=[pl.BlockSpec((B, D), lambda l: (0, 0)), pl.BlockSpec((1, D, N), lambda l: (l, 0, 0)),
                  pl.BlockSpec((1, 1, N), lambda l: (l, 0, 0))],
        out_specs=pl.BlockSpec((1, B, N), lambda l: (l, 0, 0)),
        out_shape=_sds((L, B, N), F32),
        compiler_params=_cp("arbitrary"), name="mods_project")(c_all, w, b)


def mods_weight_grad(c_all, dm):
    B, D = c_all.shape
    L, _, N = dm.shape

    def body(c_ref, d_ref, o_ref):
        cv = c_ref[...]
        cond = cv * _sigmoid(cv)
        o_ref[0] = lax.dot_general(cond, d_ref[0], (((0,), (0,)), ((), ())), preferred_element_type=F32,
                                   precision=lax.Precision.HIGHEST)

    return pl.pallas_call(
        body, grid=(L,),
        in_specs=[pl.BlockSpec((B, D), lambda l: (0, 0)), pl.BlockSpec((1, B, N), lambda l: (l, 0, 0))],
        out_specs=pl.BlockSpec((1, D, N), lambda l: (l, 0, 0)),
        out_shape=_sds((L, D, N), F32),
        compiler_params=_cp("arbitrary"), name="mods_weight_grad")(c_all, dm)


def _adam_math(g, w, m, v):
    m2 = ADAM_B1 * m + (1.0 - ADAM_B1) * g
    v2 = ADAM_B2 * v + (1.0 - ADAM_B2) * (g * g)
    m_hat = m2 / (1.0 - ADAM_B1 ** ADAM_STEP)
    v_hat = v2 / (1.0 - ADAM_B2 ** ADAM_STEP)
    delta = -ADAM_LR * (m_hat / (jnp.sqrt(v_hat) + ADAM_EPS) + ADAM_WD * w)
    return delta, m2, v2


def adam_update(g, w, m, v, parts, name):
    R, C = w.shape
    tr = _pick(R, 256, 8)

    def body(g_ref, w_ref, m_ref, v_ref, go_ref, d_ref, mo_ref, vo_ref):
        if parts:
            gv = g_ref[0].astype(F32)
            for s in range(1, N_DEV):
                gv = gv + g_ref[s].astype(F32)
        else:
            gv = g_ref[...]
        go_ref[...] = gv
        d_ref[...], mo_ref[...], vo_ref[...] = _adam_math(gv, w_ref[...], m_ref[...], v_ref[...])

    gspec = pl.BlockSpec((N_DEV, tr, C), lambda i: (0, i, 0)) if parts else _tok(tr, C)
    return pl.pallas_call(
        body, grid=(R // tr,),
        in_specs=[gspec, _tok(tr, C), _tok(tr, C), _tok(tr, C)],
        out_specs=[_tok(tr, C)] * 4, out_shape=[_sds((R, C), F32)] * 4,
        compiler_params=_cp("arbitrary"), name=name)(g, w, m, v)


def adam_layer(parts, w, m, v, prev, layer, after, name):
    L, R, C = w.shape
    tr = _pick(R, 256, 8)
    prev = (list(prev) if prev is not None else []) + [after]

    def body(p_ref, w_ref, m_ref, v_ref, *rest):
        go_ref, d_ref, mo_ref, vo_ref = rest[-4:]
        gv = p_ref[0].astype(F32)
        for s in range(1, N_DEV):
            gv = gv + p_ref[s].astype(F32)
        go_ref[...] = gv
        d_ref[...], mo_ref[...], vo_ref[...] = _adam_math(gv, w_ref[...], m_ref[...], v_ref[...])

    lay = pl.BlockSpec((None, tr, C), lambda i: (layer, i, 0))
    return pl.pallas_call(
        body, grid=(R // tr,),
        in_specs=[pl.BlockSpec((N_DEV, tr, C), lambda i: (0, i, 0)), lay, lay, lay] + [pl.BlockSpec(memory_space=pl.ANY)] * len(prev),
        out_specs=[lay] * 4, out_shape=[_sds((L, R, C), F32)] * 4,
        input_output_aliases={4 + k: k for k in range(len(prev) - 1)},
        compiler_params=_cp("arbitrary"), name=name)(parts, w, m, v, *prev)


def _my_id():
    return 4 * lax.axis_index("x") + 2 * lax.axis_index("y") + lax.axis_index("c")


def _peer(s):
    x, y, c = lax.axis_index("x"), lax.axis_index("y"), lax.axis_index("c")
    px = (1 - x) if s & 4 else x
    py = (1 - y) if s & 2 else y
    pc = (1 - c) if s & 1 else c
    return (px, py, pc), 4 * px + 2 * py + pc


def all_gather(xs, space, name):
    na = len(xs)

    def body(*refs):
        x_refs, o_refs = refs[:na], refs[na:2 * na]
        send_sems, recv_sems, local_sems = refs[2 * na:]
        me = _my_id()
        locals_, sends = [], []
        for a in range(na):
            cp = pltpu.make_async_copy(x_refs[a], o_refs[a].at[me], local_sems.at[a])
            cp.start()
            locals_.append(cp)
        for s in range(1, N_DEV):
            peer, _ = _peer(s)
            for a in range(na):
                cp = pltpu.make_async_remote_copy(
                    src_ref=x_refs[a], dst_ref=o_refs[a].at[me], send_sem=send_sems.at[a, s - 1],
                    recv_sem=recv_sems.at[a, s - 1], device_id=peer, device_id_type=MESH)
                cp.start()
                sends.append(cp)
        for s in range(1, N_DEV):
            peer, pid = _peer(s)
            for a in range(na):
                pltpu.make_async_remote_copy(
                    src_ref=x_refs[a], dst_ref=o_refs[a].at[pid], send_sem=send_sems.at[a, s - 1],
                    recv_sem=recv_sems.at[a, s - 1], device_id=peer, device_id_type=MESH).wait_recv()
        for cp in sends:
            cp.wait_send()
        for cp in locals_:
            cp.wait()

    spec = pl.BlockSpec(memory_space=space)
    return pl.pallas_call(
        body, in_specs=[spec] * na, out_specs=[spec] * na,
        out_shape=[_sds((N_DEV,) + x.shape, x.dtype) for x in xs],
        scratch_shapes=[pltpu.SemaphoreType.DMA((na, N_DEV - 1)), pltpu.SemaphoreType.DMA((na, N_DEV - 1)),
                        pltpu.SemaphoreType.DMA((na,))],
        compiler_params=pltpu.CompilerParams(vmem_limit_bytes=VMEM_LIMIT), name=name)(*xs)


def exchange_slots(xs, name):
    na = len(xs)

    def body(*refs):
        x_refs, o_refs = refs[:na], refs[na:2 * na]
        send_sems, recv_sems, local_sems = refs[2 * na:]
        me = _my_id()
        locals_, sends = [], []
        for a in range(na):
            cp = pltpu.make_async_copy(x_refs[a].at[me], o_refs[a].at[me], local_sems.at[a])
            cp.start()
            locals_.append(cp)
        for s in range(1, N_DEV):
            peer, pid = _peer(s)
            for a in range(na):
                cp = pltpu.make_async_remote_copy(
                    src_ref=x_refs[a].at[pid], dst_ref=o_refs[a].at[me], send_sem=send_sems.at[a, s - 1],
                    recv_sem=recv_sems.at[a, s - 1], device_id=peer, device_id_type=MESH)
                cp.start()
                sends.append(cp)
        for s in range(1, N_DEV):
            peer, pid = _peer(s)
            for a in range(na):
                pltpu.make_async_remote_copy(
                    src_ref=x_refs[a].at[pid], dst_ref=o_refs[a].at[pid], send_sem=send_sems.at[a, s - 1],
                    recv_sem=recv_sems.at[a, s - 1], device_id=peer, device_id_type=MESH).wait_recv()
        for cp in sends:
            cp.wait_send()
        for cp in locals_:
            cp.wait()

    spec = pl.BlockSpec(memory_space=pl.ANY)
    return pl.pallas_call(
        body, in_specs=[spec] * na, out_specs=[spec] * na,
        out_shape=[_sds(x.shape, x.dtype) for x in xs],
        scratch_shapes=[pltpu.SemaphoreType.DMA((na, N_DEV - 1)), pltpu.SemaphoreType.DMA((na, N_DEV - 1)),
                        pltpu.SemaphoreType.DMA((na,))],
        compiler_params=pltpu.CompilerParams(vmem_limit_bytes=VMEM_LIMIT), name=name)(*xs)


_HBM = pl.BlockSpec(memory_space=pltpu.HBM)
_SEM = pl.BlockSpec(memory_space=pltpu.SEMAPHORE)
_EFFECT = pltpu.SideEffectType.DATAFLOW_SIDE_EFFECTING


def _split_copy(x_ref, land_ref, s, send_sem, recv_sem, scatter):
    peer, pid = _peer(s)
    src = x_ref.at[pid] if scatter else x_ref
    return pltpu.make_async_remote_copy(src_ref=src, dst_ref=land_ref.at[_my_id()], send_sem=send_sem, recv_sem=recv_sem,
                                        device_id=peer, device_id_type=MESH)


def comm_start(xs, scatter, after, name):
    na = len(xs)
    extra = [] if after is None else [after]
    me = _my_id()
    lands = []
    for x in xs:
        shape = x.shape if scatter else (N_DEV,) + x.shape
        own = lax.dynamic_slice_in_dim(x, me, 1, 0) if scatter else x[None]
        lands.append(lax.dynamic_update_slice(lax.empty(shape, x.dtype), own, (me,) + (0,) * (len(shape) - 1)))

    def body(*refs):
        x_refs, land_refs = refs[:na], refs[na:2 * na]
        send_sem, recv_sem = refs[2 * na + len(extra)], refs[2 * na + len(extra) + 1]
        token = refs[-1]
        for s in range(1, N_DEV):
            for a in range(na):
                _split_copy(x_refs[a], land_refs[a], s, send_sem, recv_sem, scatter).start()
        token[...] = jnp.zeros_like(token)

    outs = pl.pallas_call(
        body, name=name,
        out_shape=(pltpu.SemaphoreType.DMA(()), pltpu.SemaphoreType.DMA(()))
        + tuple(pltpu.HBM(x.shape, x.dtype) for x in xs) + tuple(pltpu.HBM(l.shape, l.dtype) for l in lands)
        + (_sds((8, LANES), F32),),
        in_specs=(_HBM,) * (2 * na) + (pl.BlockSpec(memory_space=pl.ANY),) * len(extra),
        out_specs=(_SEM, _SEM) + (_HBM,) * (2 * na) + (pl.BlockSpec(memory_space=pltpu.VMEM),),
        input_output_aliases={a: 2 + a for a in range(2 * na)},
        compiler_params=pltpu.CompilerParams(has_side_effects=_EFFECT),
    )(*[pltpu.with_memory_space_constraint(x, pltpu.HBM) for x in xs],
      *[pltpu.with_memory_space_constraint(l, pltpu.HBM) for l in lands], *extra)
    return dict(sems=outs[0:2], xs=outs[2:2 + na], lands=outs[2 + na:2 + 2 * na], token=outs[-1], scatter=scatter)


def comm_wait(started, after, name):
    xs, lands = started["xs"], started["lands"]
    scatter = started["scatter"]
    na = len(xs)

    def body(*refs):
        x_refs, land_refs = refs[:na], refs[na:2 * na]
        send_sem, recv_sem = refs[2 * na], refs[2 * na + 1]
        for s in range(1, N_DEV):
            for a in range(na):
                cp = _split_copy(x_refs[a], land_refs[a], s, send_sem, recv_sem, scatter)
                cp.wait_send()
                cp.wait_recv()

    outs = pl.pallas_call(
        body, name=name,
        out_shape=tuple(pltpu.HBM(x.shape, x.dtype) for x in xs) + tuple(pltpu.HBM(l.shape, l.dtype) for l in lands),
        in_specs=(_HBM,) * (2 * na) + (_SEM, _SEM, pl.BlockSpec(memory_space=pl.ANY)),
        out_specs=(_HBM,) * (2 * na),
        input_output_aliases={a: a for a in range(2 * na)},
        compiler_params=pltpu.CompilerParams(has_side_effects=_EFFECT),
    )(*xs, *lands, *started["sems"], after)
    return list(outs[na:])


def _cols_to_natural(g):
    return jnp.concatenate([g[k] for k in range(N_DEV)], axis=1)


def _cols_to_slots(w):
    ns = w.shape[1] // N_DEV
    return jnp.stack([w[:, k * ns:(k + 1) * ns] for k in range(N_DEV)])


def _vec8(rows, d):
    rows = [r.reshape(1, d).astype(F32) for r in rows]
    return jnp.concatenate(rows + [jnp.zeros((8 - len(rows), d), F32)], axis=0)


def _ffn_forward(x, vec, w_in_t, w_out):
    h, a, b, u = ffn_up(x, vec, w_in_t)
    xn, y = proj_out(u, x, vec, w_out, FFN_RES_WEIGHT, "ffn_down")
    return xn, (x, h, a, b, u, y)


def _ffn_backward(dxo, saved, vec, w_in_t, w_out):
    x, h, a, b, u, y = saved
    dy, dab, dx, part = ffn_bwd(dxo, y, vec, w_out, w_in_t, a, b, x)
    g_out = grad_slots(u, dy, "ffn_dw_out")
    g_in_t = grad_slots(dab, h, "ffn_dw_in")
    return dx, g_in_t, g_out, part[0:4]


_TRANSPOSED = ("ffn1_w_in", "ffn2_w_in", "attn_w_q")
_COL_NATURAL = ("conv_w_in", "w_kv", "attn_w_o")
_ROW_SHARDED = ("ffn1_w_out", "ffn2_w_out", "conv_w_out")
_BIG = _TRANSPOSED + _COL_NATURAL + _ROW_SHARDED


def weight_chunks():
    chunks = []
    for layer in range(DEPTH):
        first = [("ffn1_w_in", layer), ("ffn1_w_out", layer)]
        if layer == N_A_LAYERS:
            first = [("w_kv", layer)] + first
        mixer = [("conv_w_in", layer), ("conv_w_out", layer)] if layer < N_A_LAYERS else [("attn_w_q", layer), ("attn_w_o", layer)]
        rest = mixer + [("ffn2_w_in", layer), ("ffn2_w_out", layer)]
        chunks += [first, rest] if layer == 0 else [first + rest]
    return chunks


def stacked_index(name, layer):
    if name == "w_kv":
        return None
    return layer - N_A_LAYERS if name.startswith("attn") else layer


class ChunkComm:
    def __init__(self, shards):
        self.shards = shards
        self.chunks = weight_chunks()

    def _shard(self, name, layer):
        idx = stacked_index(name, layer)
        return self.shards[name][0 if idx is None else idx]

    def start_gather(self, ci, after):
        xs = [self._shard(n, l).astype(BF16) for n, l in self.chunks[ci]]
        return comm_start(xs, False, after, f"gather_start_{ci}")

    def finish_gather(self, ci, started, after):
        lands = comm_wait(started, after, f"gather_wait_{ci}")
        W = {}
        for key, g in zip(self.chunks[ci], lands):
            W[key] = _cols_to_natural(g) if key[0] in _COL_NATURAL else g.reshape(-1, g.shape[2])
        return W, lands[0]

    def start_exchange(self, ci, slots, after):
        return comm_start([slots[key] for key in self.chunks[ci]], True, after, f"exchange_start_{ci}")

    def finish_exchange(self, ci, started, after):
        lands = comm_wait(started, after, f"exchange_wait_{ci}")
        return dict(zip(self.chunks[ci], lands))


def device_step(x, positions, target, mods, kvmods, small, comm, gather0):
    T, D = x.shape
    groups = DILATED_GROUPS
    dils = [dil for _, dil in groups]
    lane = jnp.arange(LANES) % HEAD_DIM
    inv = ROPE_THETA ** (-jnp.arange(0, ROPE_DIM, 2, dtype=F32) / ROPE_DIM)
    lane_rows = _vec8([jnp.where(lane < ROPE_DIM, inv[lane % (ROPE_DIM // 2)], 0.0), lane < ROPE_DIM,
                       (lane >= ROPE_DIM // 2) & (lane < ROPE_DIM), lane < ROPE_DIM // 2], LANES)
    tabs = rope_tables(positions.reshape(T, 1), lane_rows)

    def after_token(v, token):
        return v if token is None else v + token[0, 0]

    def vec_of(layer, sub, token=None):
        return after_token(_vec8([small["norm_g"][layer, sub], mods[layer, 3 * sub], mods[layer, 3 * sub + 1],
                                  mods[layer, 3 * sub + 2]], D), token)

    saved = []
    kv_saved = None
    k_sh = v_sh = None
    qw = GROUP_WIDTH * len(groups)
    chunk_of = {key: ci for ci, chunk in enumerate(comm.chunks) for key in chunk}
    W = {}
    flight = {"ci": 0, "started": gather0, "token": None}

    def need(key, after):
        if key not in W:
            ci = chunk_of[key]
            assert ci == flight["ci"], (key, ci)
            got, landed = comm.finish_gather(ci, flight["started"], after)
            W.update(got)
            if ci + 1 < len(comm.chunks):
                flight.update(ci=ci + 1, started=comm.start_gather(ci + 1, landed))
                flight["token"] = flight["started"]["token"]
        return W[key]

    def behind_start(v):
        token, flight["token"] = flight["token"], None
        return after_token(v, token)

    for layer in range(DEPTH):
        if layer == N_A_LAYERS:
            w_kv = need(("w_kv", layer), x)
            kv_vec = behind_start(_vec8([small["kv_norm_g"], kvmods[0], kvmods[1]], D))
            h_kv, *kv_pieces = proj_rope_fwd(x, kv_vec, w_kv, tabs, qw, False, dils, "kv_fwd")
            k_sh, v_sh = kv_pieces[:len(groups)], kv_pieces[len(groups):]
            kv_saved = (x, h_kv, kv_vec)
        rec = {}
        w_in, w_out = need(("ffn1_w_in", layer), x), need(("ffn1_w_out", layer), x)
        v1 = behind_start(vec_of(layer, 0))
        x, rec["ffn1"] = _ffn_forward(x, v1, w_in, w_out)
        if layer < N_A_LAYERS:
            w_in, w_out = need(("conv_w_in", layer), x), need(("conv_w_out", layer), x)
            v2 = behind_start(vec_of(layer, 1))
            cw = _vec8(list(small["conv_w"][layer]), D)
            x_in = x
            x, h, bcu, cv, z, y = conv_fwd(x, v2, cw, w_in, w_out)
            rec["mix"] = (x_in, h, bcu, cv, z, y, cw)
        else:
            w_q, w_o = need(("attn_w_q", layer), x), need(("attn_w_o", layer), x)
            v2 = behind_start(vec_of(layer, 1))
            x_in = x
            h, *q = proj_rope_fwd(x, v2, w_q, tabs, qw, True, dils, "q_fwd")
            os_, ls = [], []
            for g, (win, dil) in enumerate(groups):
                o, l = attn_core_fwd(q[g], k_sh[g], v_sh[g], g, win // dil)
                os_.append(o)
                ls.append(l)
            x, mixed, y = attn_mix_out(os_, ls, dils, x, v2, w_o)
            rec["mix"] = (x_in, h, q, os_, ls, mixed, y)
        w_in, w_out = need(("ffn2_w_in", layer), x), need(("ffn2_w_out", layer), x)
        v3 = behind_start(vec_of(layer, 2))
        x, rec["ffn2"] = _ffn_forward(x, v3, w_in, w_out)
        rec["vecs"] = (v1, v2, v3)
        saved.append(rec)

    dx, part_final, loss_tile = final_loss(x, _vec8([small["final_norm_g"]], D), target)
    loss = loss_tile[0, 0]

    conv_rows = [None] * N_A_LAYERS
    kv_rows = None
    mod_rows = [[None] * 3 for _ in range(DEPTH)]
    dkv_pairs = [{"k": [], "v": []} for _ in groups]
    slots = {}
    exchanges = []
    token = None

    def send_ready_chunks():
        nonlocal token
        for ci in reversed(range(len(comm.chunks))):
            if ci not in [e[0] for e in exchanges] and all(key in slots for key in comm.chunks[ci]):
                started = comm.start_exchange(ci, slots, token)
                exchanges.append((ci, started))
                token = started["token"]

    for layer in reversed(range(DEPTH)):
        rec = saved[layer]
        v1, v2, v3 = rec["vecs"]
        dx, slots[("ffn2_w_in", layer)], slots[("ffn2_w_out", layer)], mod_rows[layer][2] = _ffn_backward(
            dx, rec["ffn2"], after_token(v3, token), W[("ffn2_w_in", layer)], W[("ffn2_w_out", layer)])
        if layer < N_A_LAYERS:
            x_in, h, bcu, cv, z, y, cw = rec["mix"]
            dx, dy, dbcu, part, dcw = conv_bwd(dx, x_in, y, bcu, cv, v2, cw, W[("conv_w_in", layer)], W[("conv_w_out", layer)])
            slots[("conv_w_out", layer)] = grad_slots(z, dy, "conv_dw_out")
            slots[("conv_w_in", layer)] = grad_slots(h, dbcu, "conv_dw_in", col_slots=True)
            conv_rows[layer] = dcw[0:3]
            mod_rows[layer][1] = part[0:4]
        else:
            x_in, h, q, os_, ls, mixed, y = rec["mix"]
            outs = attn_mix_bwd(dx, y, v2, W[("attn_w_o", layer)], os_, ls, dils)
            ng = len(groups)
            dy, dos, rrs, part_gate = outs[0], outs[1:1 + ng], outs[1 + ng:1 + 2 * ng], outs[1 + 2 * ng]
            slots[("attn_w_o", layer)] = grad_slots(mixed, dy, "attn_dw_o", col_slots=True)
            dqs = []
            for g, (win, dil) in enumerate(groups):
                dq, dkc, dkp, dvc, dvp = attn_core_bwd(q[g], k_sh[g], v_sh[g], dos[g], rrs[g], ls[g], g, win // dil)
                dqs.append(dq)
                dkv_pairs[g]["k"].append((dkc, dkp))
                dkv_pairs[g]["v"].append((dvc, dvp))
            dx, dqr, part_norm = proj_rope_bwd(dqs, dils, x_in, dx, v2, W[("attn_w_q", layer)], tabs, qw, True, "q_bwd")
            slots[("attn_w_q", layer)] = grad_slots(dqr, h, "attn_dw_q")
            mod_rows[layer][1] = jnp.concatenate([part_norm[0:3], part_gate[0:1]], axis=0)
        send_ready_chunks()
        dx, slots[("ffn1_w_in", layer)], slots[("ffn1_w_out", layer)], mod_rows[layer][0] = _ffn_backward(
            dx, rec["ffn1"], after_token(v1, token), W[("ffn1_w_in", layer)], W[("ffn1_w_out", layer)])
        if layer == N_A_LAYERS:
            x_kv, h_kv, kv_vec = kv_saved
            dparts = [dkv_combine(dkv_pairs[g]["k"], win // dil, f"dk_combine_g{g}") for g, (win, dil) in enumerate(groups)]
            dparts += [dkv_combine(dkv_pairs[g]["v"], win // dil, f"dv_combine_g{g}") for g, (win, dil) in enumerate(groups)]
            dx, dkvp, part_kv = proj_rope_bwd(dparts, dils, x_kv, dx, kv_vec, W[("w_kv", layer)], tabs, qw, False, "kv_bwd")
            slots[("w_kv", layer)] = grad_slots(h_kv, dkvp, "kv_dw", col_slots=True)
            kv_rows = part_kv[0:3]
        send_ready_chunks()

    grads = {"conv_w": jnp.stack(conv_rows), "kv_rows": kv_rows, "exchanges": exchanges}
    grads["final_norm_g"] = part_final[0]
    rows = jnp.stack([jnp.stack(r) for r in mod_rows])
    grads["norm_g"] = rows[:, :, 0]
    grads["mods"] = rows[:, :, 1:4].reshape(DEPTH, N_MOD, D)
    return loss, dx, grads


def _flat2(a):
    return a.reshape(-1, a.shape[-1])


def _pad_rows(a, mult):
    r = a.shape[0]
    pad = (-r) % mult
    return a if pad == 0 else jnp.concatenate([a, jnp.zeros((pad,) + a.shape[1:], a.dtype)], axis=0)


def kernel(x, c, positions, norm_g, ada_w, ada_b, ffn1_w_in, ffn1_w_out, ffn2_w_in, ffn2_w_out, conv_w_in, conv_w, conv_w_out, kv_norm_g, kv_ada_w, kv_ada_b, w_kv, attn_w_q, attn_w_o, final_norm_g, loss_target, m_norm_g, m_ada_w, m_ada_b, m_ffn1_w_in, m_ffn1_w_out, m_ffn2_w_in, m_ffn2_w_out, m_conv_w_in, m_conv_w, m_conv_w_out, m_kv_norm_g, m_kv_ada_w, m_kv_ada_b, m_w_kv, m_attn_w_q, m_attn_w_o, m_final_norm_g, v_norm_g, v_ada_w, v_ada_b, v_ffn1_w_in, v_ffn1_w_out, v_ffn2_w_in, v_ffn2_w_out, v_conv_w_in, v_conv_w, v_conv_w_out, v_kv_norm_g, v_kv_ada_w, v_kv_ada_b, v_w_kv, v_attn_w_q, v_attn_w_o, v_final_norm_g):
    names = ("norm_g", "ada_w", "ada_b", "ffn1_w_in", "ffn1_w_out", "ffn2_w_in", "ffn2_w_out", "conv_w_in", "conv_w",
             "conv_w_out", "kv_norm_g", "kv_ada_w", "kv_ada_b", "w_kv", "attn_w_q", "attn_w_o", "final_norm_g")
    wts = dict(zip(names, (norm_g, ada_w, ada_b, ffn1_w_in, ffn1_w_out, ffn2_w_in, ffn2_w_out, conv_w_in, conv_w, conv_w_out,
                           kv_norm_g, kv_ada_w, kv_ada_b, w_kv, attn_w_q, attn_w_o, final_norm_g)))
    mom = dict(zip(names, (m_norm_g, m_ada_w, m_ada_b, m_ffn1_w_in, m_ffn1_w_out, m_ffn2_w_in, m_ffn2_w_out, m_conv_w_in,
                           m_conv_w, m_conv_w_out, m_kv_norm_g, m_kv_ada_w, m_kv_ada_b, m_w_kv, m_attn_w_q, m_attn_w_o,
                           m_final_norm_g)))
    var = dict(zip(names, (v_norm_g, v_ada_w, v_ada_b, v_ffn1_w_in, v_ffn1_w_out, v_ffn2_w_in, v_ffn2_w_out, v_conv_w_in,
                           v_conv_w, v_conv_w_out, v_kv_norm_g, v_kv_ada_w, v_kv_ada_b, v_w_kv, v_attn_w_q, v_attn_w_o,
                           v_final_norm_g)))
    T, D = x.shape[1], x.shape[2]
    me = _my_id()
    nmod = ada_w.shape[2]
    nkv = kv_ada_w.shape[1]

    def stacked(w, n):
        w = w if w.ndim == 3 else w[None]
        return jnp.swapaxes(w, 1, 2) if n in _TRANSPOSED else w

    comm = ChunkComm({n: stacked(wts[n], n) for n in _BIG})
    W = {}

    ds = norm_g.shape[2]
    small = jnp.concatenate([c.reshape(-1), norm_g.reshape(-1), conv_w.reshape(-1)]).astype(F32)
    n_small = small.shape[0]
    small = _pad_rows(small.reshape(-1, 1), 8 * LANES).reshape(-1, LANES)
    (small_all,) = all_gather([small], pltpu.VMEM, "gather_small")
    small_all = small_all.reshape(N_DEV, -1)[:, :n_small]
    c_all = small_all[:, :D]
    def full_rows(off, count):
        return jnp.stack([small_all[:, off + i * ds:off + (i + 1) * ds].reshape(D) for i in range(count)])

    W["norm_g"] = full_rows(D, DEPTH * 3).reshape(DEPTH, 3, D)
    W["conv_w"] = full_rows(D + DEPTH * 3 * ds, N_A_LAYERS * 3).reshape(N_A_LAYERS, 3, D)
    W["kv_norm_g"], W["final_norm_g"] = kv_norm_g, final_norm_g

    ada_b_mine = lax.dynamic_slice_in_dim(ada_b, me * nmod, nmod, axis=1).reshape(DEPTH, 1, nmod)
    kv_b_mine = lax.dynamic_slice_in_dim(kv_ada_b, me * nkv, nkv, axis=0).reshape(1, 1, nkv)
    mods_cols = mods_project(c_all, ada_w, ada_b_mine)
    kv_cols = mods_project(c_all, kv_ada_w.reshape(1, D, nkv), kv_b_mine)
    mcat = jnp.concatenate([mods_cols[l] for l in range(DEPTH)] + [kv_cols[0]], axis=1)
    wm = mcat.shape[1]
    if wm % LANES:
        mcat = jnp.concatenate([mcat, jnp.zeros((N_DEV, LANES - wm % LANES), F32)], axis=1)
    (mods_all,) = exchange_slots([mcat.reshape(N_DEV, 1, -1)], "exchange_mods")
    gather0 = comm.start_gather(0, mods_all)
    mods_all = mods_all.reshape(N_DEV, -1)
    mods = jnp.stack([mods_all[:, l * nmod:(l + 1) * nmod].reshape(N_MOD, D) for l in range(DEPTH)])
    kvmods = mods_all[:, DEPTH * nmod:DEPTH * nmod + nkv].reshape(2, D)

    loss_local, dx, grads = device_step(x[0], positions[0], loss_target[0], mods, kvmods, W, comm, gather0)
    loss = lax.psum(loss_local, MESH_AXES)

    dmods = grads["mods"].reshape(-1)
    dkvm = grads["kv_rows"][1:3].reshape(-1)
    vecs = jnp.concatenate([dmods, dkvm, grads["kv_rows"][0], grads["final_norm_g"], grads["norm_g"].reshape(-1),
                            grads["conv_w"].reshape(-1)])
    n_vec = vecs.shape[0]
    vecs = _pad_rows(vecs.reshape(-1, 1), 8 * LANES).reshape(-1, LANES)
    (vec_all,) = all_gather([vecs], pltpu.VMEM, "gather_vector_grads")
    vec_all = vec_all.reshape(N_DEV, -1)[:, :n_vec]
    nm_, nk_ = DEPTH * N_MOD * D, 2 * D
    dmods_all = vec_all[:, :nm_].reshape(N_DEV, DEPTH, N_MOD * D)
    dkvm_all = vec_all[:, nm_:nm_ + nk_]
    rest = vec_all[:, nm_ + nk_:]
    parts_kv_norm, parts_final = rest[:, :D].reshape(N_DEV, 1, D), rest[:, D:2 * D].reshape(N_DEV, 1, D)
    parts_norm = lax.dynamic_slice_in_dim(rest[:, 2 * D:2 * D + DEPTH * 3 * D].reshape(N_DEV, DEPTH * 3, D), me * ds, ds, axis=2)
    parts_conv = lax.dynamic_slice_in_dim(rest[:, 2 * D + DEPTH * 3 * D:].reshape(N_DEV, N_A_LAYERS * 3, D), me * ds, ds, axis=2)
    dm_cols = lax.dynamic_slice_in_dim(dmods_all, me * nmod, nmod, axis=2)
    dm_mine = jnp.stack([dm_cols[:, l] for l in range(DEPTH)])
    dkv_mine = lax.dynamic_slice_in_dim(dkvm_all, me * nkv, nkv, axis=1).reshape(1, N_DEV, nkv)
    g_ada_w = mods_weight_grad(c_all, dm_mine)
    g_kv_ada_w = mods_weight_grad(c_all, dkv_mine)[0]

    out_g, out_d, out_m, out_v = {}, {}, {}, {}

    def update(n, g, w, parts=False):
        shp = w.shape
        w2 = w.reshape(1, -1) if w.ndim == 1 else _flat2(w)
        g2 = g if parts else g.reshape(w2.shape)
        res = adam_update(g2, w2, mom[n].reshape(w2.shape), var[n].reshape(w2.shape), parts, "adam_" + n)
        out_g[n], out_d[n], out_m[n], out_v[n] = (r.reshape(shp) for r in res)

    moms = {n: stacked(mom[n], n) for n in _BIG}
    vars_ = {n: stacked(var[n], n) for n in _BIG}
    results = {}
    after = dx
    for ci, started in grads["exchanges"]:
        for (n, layer), parts in comm.finish_exchange(ci, started, after).items():
            idx = stacked_index(n, layer)
            results[n] = adam_layer(parts, comm.shards[n], moms[n], vars_[n], results.get(n), 0 if idx is None else idx,
                                    after, f"adam_{n}_{layer}")
            after = results[n][1]
    for n in _BIG:
        res = [jnp.swapaxes(r, 1, 2) if n in _TRANSPOSED else r for r in results[n]]
        out_g[n], out_d[n], out_m[n], out_v[n] = (r.reshape(wts[n].shape) for r in res)
    update("ada_w", g_ada_w, ada_w)
    update("kv_ada_w", g_kv_ada_w, kv_ada_w)
    update("ada_b", dmods_all, ada_b, True)
    update("kv_ada_b", dkvm_all.reshape(N_DEV, 1, nk_), kv_ada_b, True)
    update("kv_norm_g", parts_kv_norm, kv_norm_g, True)
    update("final_norm_g", parts_final, final_norm_g, True)
    update("norm_g", parts_norm, norm_g, True)
    update("conv_w", parts_conv, conv_w, True)

    return (loss, dx.reshape(x.shape), *[out_g[n] for n in names], *[out_d[n] for n in names],
            *[out_m[n] for n in names], *[out_v[n] for n in names])
```

```python
import functools

import jax
import jax.numpy as jnp
from jax import lax
from jax.experimental import pallas as pl
from jax.experimental.pallas import tpu as pltpu

F32, BF16 = jnp.float32, jnp.bfloat16

N_DEV = 8
MESH_AXES = ("x", "y", "c")
DEPTH = 4
N_A_LAYERS = 2
HEAD_DIM = 64
HEADS_PER_GROUP = 8
GROUP_WIDTH = HEAD_DIM * HEADS_PER_GROUP
DILATED_GROUPS = ((128, 1), (512, 4), (2048, 16))
ROPE_DIM = HEAD_DIM // 4
ROPE_THETA = 500000.0
NORM_EPS = 1e-5
FFN_RES_WEIGHT = 0.5
N_MOD = 9
ADAM_LR, ADAM_B1, ADAM_B2, ADAM_EPS, ADAM_WD, ADAM_STEP = 0.001, 0.9, 0.999, 1e-08, 0.01, 10

LANES = 128
TOKEN_TILE = 512
FFN_BWD_TILE = 256
CONTRACT_TILE = 2048
MXU_WIDTH = 256
VMEM_LIMIT = 56 * 1024 * 1024
MESH = pl.DeviceIdType.MESH


def _cp(*sem):
    return pltpu.CompilerParams(dimension_semantics=sem, vmem_limit_bytes=VMEM_LIMIT)


def _pick(n, cap, mult=LANES):
    if n <= cap:
        return n
    best = None
    for t in range(mult, cap + 1, mult):
        if n % t == 0:
            best = t
    assert best is not None, (n, cap)
    return best


def _tok(tm, w):
    return pl.BlockSpec((tm, w), lambda i: (i, 0))


def _res(shape):
    nd = len(shape)
    return pl.BlockSpec(shape, lambda *_: (0,) * nd, pipeline_mode=pl.Buffered(1))


def _sds(shape, dt):
    return jax.ShapeDtypeStruct(shape, dt)


def _sigmoid(a):
    return 1.0 / (1.0 + jnp.exp(-a))


def _modnorm(x, g, sh, sc):
    r = lax.rsqrt(jnp.mean(x * x, axis=-1, keepdims=True) + NORM_EPS)
    return (x * r * g) * (1.0 + sc) + sh


def _dot(a, b):
    return jnp.dot(a, b, preferred_element_type=F32)


def _dot_nt(a, b):
    return lax.dot_general(a, b, (((1,), (1,)), ((), ())), preferred_element_type=F32)


def _dot_tn(a, b):
    return lax.dot_general(a, b, (((0,), (0,)), ((), ())), preferred_element_type=F32)


def _rows8(rows, d):
    pad = 8 - len(rows)
    return jnp.concatenate(list(rows) + [jnp.zeros((pad, d), F32)], axis=0)


def _acc_rows(ref, tile, first):
    @pl.when(first)
    def _():
        ref[...] = tile

    @pl.when(jnp.logical_not(first))
    def _():
        ref[...] += tile


def ffn_up(x, vec, w_in_t):
    T, D = x.shape
    F = w_in_t.shape[0] // 2
    tm, cw = min(TOKEN_TILE, T), _pick(F, MXU_WIDTH)

    def body(x_ref, vec_ref, w_ref, h_ref, ga_ref, gb_ref, u_ref):
        hb = _modnorm(x_ref[...], vec_ref[0:1], vec_ref[1:2], vec_ref[2:3]).astype(BF16)
        h_ref[...] = hb
        for c in range(F // cw):
            lo, hi = c * cw, (c + 1) * cw
            a = _dot_nt(hb, w_ref[lo:hi, :])
            b = _dot_nt(hb, w_ref[F + lo:F + hi, :])
            sg = _sigmoid(a)
            silu = a * sg
            ga_ref[:, lo:hi] = (b * (sg + silu * (1.0 - sg))).astype(BF16)
            gb_ref[:, lo:hi] = silu.astype(BF16)
            u_ref[:, lo:hi] = (silu * b).astype(BF16)

    return pl.pallas_call(
        body, grid=(T // tm,),
        in_specs=[_tok(tm, D), _res((8, D)), _res((2 * F, D))],
        out_specs=[_tok(tm, D), _tok(tm, F), _tok(tm, F), _tok(tm, F)],
        out_shape=[_sds((T, D), BF16), _sds((T, F), BF16), _sds((T, F), BF16), _sds((T, F), BF16)],
        compiler_params=_cp("arbitrary"), name="ffn_up")(x, vec, w_in_t)


def ffn_fwd(x, vec, w_in_t, w_out):
    T, D = x.shape
    F = w_in_t.shape[0] // 2
    tm, cw = min(TOKEN_TILE, T), _pick(F, MXU_WIDTH)

    def body(x_ref, vec_ref, wi_ref, wo_ref, xn_ref, h_ref, ga_ref, gb_ref, u_ref, y_ref):
        x_t = x_ref[...]
        hb = _modnorm(x_t, vec_ref[0:1], vec_ref[1:2], vec_ref[2:3]).astype(BF16)
        h_ref[...] = hb
        for c in range(F // cw):
            lo, hi = c * cw, (c + 1) * cw
            a = _dot_nt(hb, wi_ref[lo:hi, :])
            b = _dot_nt(hb, wi_ref[F + lo:F + hi, :])
            sg = _sigmoid(a)
            silu = a * sg
            ga_ref[:, lo:hi] = (b * (sg + silu * (1.0 - sg))).astype(BF16)
            gb_ref[:, lo:hi] = silu.astype(BF16)
            u_ref[:, lo:hi] = (silu * b).astype(BF16)
        y = _dot(u_ref[...], wo_ref[...])
        y_ref[...] = y.astype(BF16)
        xn_ref[...] = x_t + (FFN_RES_WEIGHT * (1.0 + vec_ref[3:4])) * y

    return pl.pallas_call(
        body, grid=(T // tm,),
        in_specs=[_tok(tm, D), _res((8, D)), _res((2 * F, D)), _res((F, D))],
        out_specs=[_tok(tm, D), _tok(tm, D), _tok(tm, F), _tok(tm, F), _tok(tm, F), _tok(tm, D)],
        out_shape=[_sds((T, D), F32), _sds((T, D), BF16), _sds((T, F), BF16), _sds((T, F), BF16), _sds((T, F), BF16),
                   _sds((T, D), BF16)],
        compiler_params=_cp("arbitrary"), name="ffn_fwd")(x, vec, w_in_t, w_out)


def proj_out(u, x, vec, w_out, res_weight, name):
    T, D = x.shape
    K = u.shape[1]
    tm = min(TOKEN_TILE, T)

    def body(u_ref, x_ref, vec_ref, w_ref, xn_ref, y_ref):
        y = _dot(u_ref[...], w_ref[...])
        y_ref[...] = y.astype(BF16)
        xn_ref[...] = x_ref[...] + (res_weight * (1.0 + vec_ref[3:4])) * y

    return pl.pallas_call(
        body, grid=(T // tm,),
        in_specs=[_tok(tm, K), _tok(tm, D), _res((8, D)), _res((K, D))],
        out_specs=[_tok(tm, D), _tok(tm, D)],
        out_shape=[_sds((T, D), F32), _sds((T, D), BF16)],
        compiler_params=_cp("arbitrary"), name=name)(u, x, vec, w_out)


def ffn_down_bwd(dxo, y, vec, w_out, a, b):
    T, D = dxo.shape
    F = a.shape[1]
    tm, cw = min(TOKEN_TILE, T), _pick(F, MXU_WIDTH)

    def body(dxo_ref, y_ref, vec_ref, w_ref, a_ref, b_ref, dy_ref, dab_ref, part_ref):
        dxo_t = dxo_ref[...]
        dyb = (dxo_t * (FFN_RES_WEIGHT * (1.0 + vec_ref[3:4]))).astype(BF16)
        dy_ref[...] = dyb
        dgate = FFN_RES_WEIGHT * jnp.sum(dxo_t * y_ref[...].astype(F32), axis=0, keepdims=True)
        _acc_rows(part_ref, _rows8([dgate], D), pl.program_id(0) == 0)
        for c in range(F // cw):
            lo, hi = c * cw, (c + 1) * cw
            du = _dot_nt(dyb, w_ref[lo:hi, :])
            dab_ref[:, lo:hi] = (du * a_ref[:, lo:hi].astype(F32)).astype(BF16)
            dab_ref[:, F + lo:F + hi] = (du * b_ref[:, lo:hi].astype(F32)).astype(BF16)

    return pl.pallas_call(
        body, grid=(T // tm,),
        in_specs=[_tok(tm, D), _tok(tm, D), _res((8, D)), _res((F, D)), _tok(tm, F), _tok(tm, F)],
        out_specs=[_tok(tm, D), _tok(tm, 2 * F), pl.BlockSpec((8, D), lambda i: (0, 0))],
        out_shape=[_sds((T, D), BF16), _sds((T, 2 * F), BF16), _sds((8, D), F32)],
        compiler_params=_cp("arbitrary"), name="ffn_down_bwd")(dxo, y, vec, w_out, a, b)


def ffn_up_bwd(dab, w_in_t, x, dxo, vec):
    T, D = x.shape
    F2 = dab.shape[1]
    tm = min(TOKEN_TILE, T)

    def body(dab_ref, w_ref, x_ref, dxo_ref, vec_ref, dx_ref, part_ref):
        dh = _dot(dab_ref[...], w_ref[...])
        _, vjp = jax.vjp(_modnorm, x_ref[...], vec_ref[0:1], vec_ref[1:2], vec_ref[2:3])
        dx, dg, dsh, dsc = vjp(dh)
        dx_ref[...] = dxo_ref[...] + dx
        _acc_rows(part_ref, _rows8([dg, dsh, dsc], D), pl.program_id(0) == 0)

    return pl.pallas_call(
        body, grid=(T // tm,),
        in_specs=[_tok(tm, F2), _res((F2, D)), _tok(tm, D), _tok(tm, D), _res((8, D))],
        out_specs=[_tok(tm, D), pl.BlockSpec((8, D), lambda i: (0, 0))],
        out_shape=[_sds((T, D), F32), _sds((8, D), F32)],
        compiler_params=_cp("arbitrary"), name="ffn_up_bwd")(dab, w_in_t, x, dxo, vec)


def ffn_bwd(dxo, y, vec, w_out, w_in_t, a, b, x):
    T, D = x.shape
    F = a.shape[1]
    tm, cw = min(FFN_BWD_TILE, T), _pick(F, MXU_WIDTH)

    def body(dxo_ref, y_ref, vec_ref, wo_ref, wi_ref, a_ref, b_ref, x_ref, dy_ref, dab_ref, dx_ref, part_ref):
        dxo_t = dxo_ref[...]
        dyb = (dxo_t * (FFN_RES_WEIGHT * (1.0 + vec_ref[3:4]))).astype(BF16)
        dy_ref[...] = dyb
        dgate = FFN_RES_WEIGHT * jnp.sum(dxo_t * y_ref[...].astype(F32), axis=0, keepdims=True)
        for c in range(F // cw):
            lo, hi = c * cw, (c + 1) * cw
            du = _dot_nt(dyb, wo_ref[lo:hi, :])
            dab_ref[:, lo:hi] = (du * a_ref[:, lo:hi].astype(F32)).astype(BF16)
            dab_ref[:, F + lo:F + hi] = (du * b_ref[:, lo:hi].astype(F32)).astype(BF16)
        dh = _dot(dab_ref[...], wi_ref[...])
        _, vjp = jax.vjp(_modnorm, x_ref[...], vec_ref[0:1], vec_ref[1:2], vec_ref[2:3])
        dx, dg, dsh, dsc = vjp(dh)
        dx_ref[...] = dxo_t + dx
        _acc_rows(part_ref, _rows8([dg, dsh, dsc, dgate], D), pl.program_id(0) == 0)

    return pl.pallas_call(
        body, grid=(T // tm,),
        in_specs=[_tok(tm, D), _tok(tm, D), _res((8, D)), _res((F, D)), _res((2 * F, D)), _tok(tm, F), _tok(tm, F), _tok(tm, D)],
        out_specs=[_tok(tm, D), _tok(tm, 2 * F), _tok(tm, D), pl.BlockSpec((8, D), lambda i: (0, 0))],
        out_shape=[_sds((T, D), BF16), _sds((T, 2 * F), BF16), _sds((T, D), F32), _sds((8, D), F32)],
        compiler_params=_cp("arbitrary"), name="ffn_bwd")(dxo, y, vec, w_out, w_in_t, a, b, x)


def grad_slots(a, b, name, col_slots=False, after=None):
    T, M = a.shape
    extra = [] if after is None else [after]
    N = b.shape[1]
    tk = min(CONTRACT_TILE, T)
    nk = T // tk
    tmm = _pick(M, 1408)
    if col_slots:
        ns = N // N_DEV
        sp = max(s for s in (1, 2, 4, 8) if ns * s <= 1536)
        tn = ns * sp
    else:
        tn = _pick(N, 1536)

    def body(a_ref, b_ref, *rest):
        o_ref, acc = rest[-2:]
        k = pl.program_id(2)
        t = _dot_tn(a_ref[...], b_ref[...])

        @pl.when(k == 0)
        def _():
            acc[...] = t

        @pl.when(k > 0)
        def _():
            acc[...] += t

        @pl.when(k == nk - 1)
        def _():
            if col_slots:
                for s in range(sp):
                    o_ref[s] = acc[:, s * ns:(s + 1) * ns].astype(BF16)
            else:
                o_ref[...] = acc[...].astype(BF16)

    if col_slots:
        out_spec, out_shape = pl.BlockSpec((sp, tmm, ns), lambda i, j, k: (j, i, 0)), _sds((N_DEV, M, ns), BF16)
    else:
        out_spec, out_shape = pl.BlockSpec((tmm, tn), lambda i, j, k: (i, j)), _sds((M, N), BF16)
    out = pl.pallas_call(
        body, grid=(M // tmm, N // tn, nk),
        in_specs=[pl.BlockSpec((tk, tmm), lambda i, j, k: (k, i)), pl.BlockSpec((tk, tn), lambda i, j, k: (k, j))]
        + [pl.BlockSpec(memory_space=pl.ANY)] * len(extra),
        out_specs=out_spec, out_shape=out_shape,
        scratch_shapes=[pltpu.VMEM((tmm, tn), F32)],
        compiler_params=_cp("arbitrary", "arbitrary", "arbitrary"), name=name)(a, b, *extra)
    return out if col_slots else out.reshape(N_DEV, M // N_DEV, N)


def conv_fwd(x, vec, cw, w_in, w_out):
    T, D = x.shape
    tm = min(TOKEN_TILE, T)

    def body(x_ref, vec_ref, cw_ref, wi_ref, wo_ref, xn_ref, h_ref, bcu_ref, cv_ref, z_ref, y_ref, vbuf):
        @pl.when(pl.program_id(0) == 0)
        def _():
            vbuf[0:8, :] = jnp.zeros((8, D), F32)

        x_t = x_ref[...]
        hb = _modnorm(x_t, vec_ref[0:1], vec_ref[1:2], vec_ref[2:3]).astype(BF16)
        h_ref[...] = hb
        bcu = _dot(hb, wi_ref[...])
        bcu_ref[...] = bcu.astype(BF16)
        bg, v = bcu[:, 0:D], bcu[:, D:2 * D] * bcu[:, 2 * D:3 * D]
        vbuf[8:8 + tm, :] = v
        conv = cw_ref[0:1] * vbuf[6:6 + tm, :] + cw_ref[1:2] * vbuf[7:7 + tm, :] + cw_ref[2:3] * v
        cv_ref[...] = conv.astype(BF16)
        zb = (bg * conv).astype(BF16)
        z_ref[...] = zb
        y = _dot(zb, wo_ref[...])
        y_ref[...] = y.astype(BF16)
        xn_ref[...] = x_t + (1.0 + vec_ref[3:4]) * y
        vbuf[0:8, :] = vbuf[tm:tm + 8, :]

    return pl.pallas_call(
        body, grid=(T // tm,),
        in_specs=[_tok(tm, D), _res((8, D)), _res((8, D)), _res((D, 3 * D)), _res((D, D))],
        out_specs=[_tok(tm, D), _tok(tm, D), _tok(tm, 3 * D), _tok(tm, D), _tok(tm, D), _tok(tm, D)],
        out_shape=[_sds((T, D), F32), _sds((T, D), BF16), _sds((T, 3 * D), BF16), _sds((T, D), BF16),
                   _sds((T, D), BF16), _sds((T, D), BF16)],
        scratch_shapes=[pltpu.VMEM((tm + 8, D), F32)],
        compiler_params=_cp("arbitrary"), name="conv_fwd")(x, vec, cw, w_in, w_out)


def conv_bwd(dxo, x, y, bcu, cv, vec, cw, w_in, w_out):
    T, D = x.shape
    tm = min(TOKEN_TILE, T)
    nt = T // tm

    def body(dxo_ref, x_ref, y_ref, bcu_ref, cv_ref, vec_ref, cw_ref, wi_ref, wo_ref,
             dx_ref, dy_ref, dbcu_ref, part_ref, dcw_ref, dcbuf):
        first = pl.program_id(0) == 0

        @pl.when(first)
        def _():
            dcbuf[tm:tm + 8, :] = jnp.zeros((8, D), F32)

        dxo_t = dxo_ref[...]
        dyb = (dxo_t * (1.0 + vec_ref[3:4])).astype(BF16)
        dy_ref[...] = dyb
        dgate = jnp.sum(dxo_t * y_ref[...].astype(F32), axis=0, keepdims=True)
        dz = _dot_nt(dyb, wo_ref[...])
        bcu_t = bcu_ref[...].astype(F32)
        bg, cg, ug = bcu_t[:, 0:D], bcu_t[:, D:2 * D], bcu_t[:, 2 * D:3 * D]
        dconv = dz * bg
        dbg = dz * cv_ref[...].astype(F32)
        dcbuf[0:tm, :] = dconv
        d1, d2 = dcbuf[1:tm + 1, :], dcbuf[2:tm + 2, :]
        dv = cw_ref[2:3] * dconv + cw_ref[1:2] * d1 + cw_ref[0:1] * d2
        v = cg * ug
        dcw = _rows8([jnp.sum(d2 * v, axis=0, keepdims=True), jnp.sum(d1 * v, axis=0, keepdims=True),
                      jnp.sum(dconv * v, axis=0, keepdims=True)], D)
        dbcu = jnp.concatenate([dbg, dv * ug, dv * cg], axis=1).astype(BF16)
        dbcu_ref[...] = dbcu
        dh = _dot_nt(dbcu, wi_ref[...])
        _, vjp = jax.vjp(_modnorm, x_ref[...], vec_ref[0:1], vec_ref[1:2], vec_ref[2:3])
        dx, dg, dsh, dsc = vjp(dh)
        dx_ref[...] = dxo_t + dx
        _acc_rows(part_ref, _rows8([dg, dsh, dsc, dgate], D), first)
        _acc_rows(dcw_ref, dcw, first)
        dcbuf[tm:tm + 8, :] = dcbuf[0:8, :]

    def rev(w):
        return pl.BlockSpec((tm, w), lambda i: (nt - 1 - i, 0))

    return pl.pallas_call(
        body, grid=(nt,),
        in_specs=[rev(D), rev(D), rev(D), rev(3 * D), rev(D), _res((8, D)), _res((8, D)), _res((D, 3 * D)), _res((D, D))],
        out_specs=[rev(D), rev(D), rev(3 * D), pl.BlockSpec((8, D), lambda i: (0, 0)), pl.BlockSpec((8, D), lambda i: (0, 0))],
        out_shape=[_sds((T, D), F32), _sds((T, D), BF16), _sds((T, 3 * D), BF16), _sds((8, D), F32), _sds((8, D), F32)],
        scratch_shapes=[pltpu.VMEM((tm + 8, D), F32)],
        compiler_params=_cp("arbitrary"), name="conv_bwd")(dxo, x, y, bcu, cv, vec, cw, w_in, w_out)


def rope_tables(pos, lane_rows):
    T = pos.shape[0]
    tm = min(TOKEN_TILE, T)

    def body(p_ref, lr_ref, c_ref, sp_ref, sm_ref):
        ang = p_ref[...].astype(F32) * lr_ref[0:1]
        cs, sn = jnp.cos(ang), jnp.sin(ang)
        c_ref[...] = jnp.where(lr_ref[1:2] > 0.5, cs, 1.0)
        sp_ref[...] = jnp.where(lr_ref[2:3] > 0.5, sn, 0.0)
        sm_ref[...] = jnp.where(lr_ref[3:4] > 0.5, -sn, 0.0)

    return pl.pallas_call(
        body, grid=(T // tm,),
        in_specs=[_tok(tm, 1), _res((8, LANES))],
        out_specs=[_tok(tm, LANES)] * 3,
        out_shape=[_sds((T, LANES), F32)] * 3,
        compiler_params=_cp("arbitrary"), name="rope_tables")(pos, lane_rows)


def _rope(t, c, sp, sm):
    w = t.shape[1]
    reps = w // LANES
    cf, spf, smf = jnp.tile(c, (1, reps)), jnp.tile(sp, (1, reps)), jnp.tile(sm, (1, reps))
    half = ROPE_DIM // 2
    return t * cf + pltpu.roll(t, half, axis=1) * spf + pltpu.roll(t, w - half, axis=1) * smf


def _rope_t(d, c, sp, sm):
    w = d.shape[1]
    reps = w // LANES
    cf, spf, smf = jnp.tile(c, (1, reps)), jnp.tile(sp, (1, reps)), jnp.tile(sm, (1, reps))
    half = ROPE_DIM // 2
    return d * cf + pltpu.roll(d * spf, w - half, axis=1) + pltpu.roll(d * smf, half, axis=1)


def _split_residues(v, d, stage):
    tm, width = v.shape
    if d == 1:
        return [v]
    nj = width // LANES
    for j in range(nj):
        stage[j] = v[:, j * LANES:(j + 1) * LANES]
    return [jnp.concatenate([stage[j, pl.ds(r, tm // d, stride=d), :] for j in range(nj)], axis=1) for r in range(d)]


def _merge_residues(piece, d, tm, width, stage):
    if d == 1:
        return piece(0)
    nj = width // LANES
    for r in range(d):
        p = piece(r)
        for j in range(nj):
            stage[j, pl.ds(r, tm // d, stride=d), :] = p[:, j * LANES:(j + 1) * LANES]
    return jnp.concatenate([stage[j] for j in range(nj)], axis=1)


def _residue_spec(d, tm):
    return pl.BlockSpec((d, tm // d, GROUP_WIDTH), lambda i: (0, i, 0))


def _stage_scratch(tm):
    return pltpu.VMEM((GROUP_WIDTH // LANES, tm, LANES), F32)


def proj_rope_fwd(x, vec, w, tabs, n_rope, transposed, dils, name):
    T, D = x.shape
    N = w.shape[0] if transposed else w.shape[1]
    tm = min(TOKEN_TILE, T)
    GW = GROUP_WIDTH
    piece_dils = [dils[j % len(dils)] for j in range(N // GW)]

    def body(x_ref, vec_ref, w_ref, c_ref, sp_ref, sm_ref, h_ref, *rest):
        out_refs, stage = rest[:-1], rest[-1]
        hb = _modnorm(x_ref[...], vec_ref[0:1], vec_ref[1:2], vec_ref[2:3]).astype(BF16)
        h_ref[...] = hb
        p = _dot_nt(hb, w_ref[...]) if transposed else _dot(hb, w_ref[...])
        pr = _rope(p[:, 0:n_rope], c_ref[...], sp_ref[...], sm_ref[...])
        for j, d in enumerate(piece_dils):
            src = pr if (j + 1) * GW <= n_rope else p
            for r, rows in enumerate(_split_residues(src[:, j * GW:(j + 1) * GW], d, stage)):
                out_refs[j][r] = rows.astype(BF16)

    return pl.pallas_call(
        body, grid=(T // tm,),
        in_specs=[_tok(tm, D), _res((8, D)), _res(w.shape)] + [_tok(tm, LANES)] * 3,
        out_specs=[_tok(tm, D)] + [_residue_spec(d, tm) for d in piece_dils],
        out_shape=[_sds((T, D), BF16)] + [_sds((d, T // d, GW), BF16) for d in piece_dils],
        scratch_shapes=[_stage_scratch(tm)],
        compiler_params=_cp("arbitrary"), name=name)(x, vec, w, *tabs)


def proj_rope_bwd(dparts, dils, x, dxo, vec, w, tabs, n_rope, transposed, name):
    T, D = x.shape
    N = w.shape[0] if transposed else w.shape[1]
    tm = min(TOKEN_TILE, T)
    GW = GROUP_WIDTH
    npart = len(dparts)
    piece_dils = [dils[j % len(dils)] for j in range(npart)]

    def body(*refs):
        d_refs = refs[:npart]
        x_ref, dxo_ref, vec_ref, w_ref, c_ref, sp_ref, sm_ref, dx_ref, dp_ref, part_ref, stage = refs[npart:]
        d = jnp.concatenate([_merge_residues(lambda r, ref=ref: ref[r].astype(F32), dd, tm, GW, stage)
                             for ref, dd in zip(d_refs, piece_dils)], axis=1)
        dr = _rope_t(d[:, 0:n_rope], c_ref[...], sp_ref[...], sm_ref[...])
        if n_rope < N:
            dr = jnp.concatenate([dr, d[:, n_rope:N]], axis=1)
        dpb = dr.astype(BF16)
        dp_ref[...] = dpb
        dh = _dot(dpb, w_ref[...]) if transposed else _dot_nt(dpb, w_ref[...])
        _, vjp = jax.vjp(_modnorm, x_ref[...], vec_ref[0:1], vec_ref[1:2], vec_ref[2:3])
        dx, dg, dsh, dsc = vjp(dh)
        dx_ref[...] = dxo_ref[...] + dx
        _acc_rows(part_ref, _rows8([dg, dsh, dsc], D), pl.program_id(0) == 0)

    return pl.pallas_call(
        body, grid=(T // tm,),
        in_specs=[_residue_spec(d, tm) for d in piece_dils] + [_tok(tm, D), _tok(tm, D), _res((8, D)), _res(w.shape)]
        + [_tok(tm, LANES)] * 3,
        out_specs=[_tok(tm, D), _tok(tm, N), pl.BlockSpec((8, D), lambda i: (0, 0))],
        out_shape=[_sds((T, D), F32), _sds((T, N), BF16), _sds((8, D), F32)],
        scratch_shapes=[_stage_scratch(tm)],
        compiler_params=_cp("arbitrary"), name=name)(*dparts, x, dxo, vec, w, *tabs)


def _valid_mask(n, i):
    qi = lax.broadcasted_iota(jnp.int32, (n, 2 * n), 0)
    kj = lax.broadcasted_iota(jnp.int32, (n, 2 * n), 1)
    dist = n + qi - kj
    return (dist >= 0) & (dist <= n) & ((kj >= n) | (i > 0))


def _band_specs(n):
    two = pl.BlockSpec((None, 2 * n, GROUP_WIDTH), lambda r, i: (r, i, 0))
    prv = pl.BlockSpec((None, n, GROUP_WIDTH), lambda r, i: (r, jnp.maximum(2 * i - 1, 0), 0))
    one = pl.BlockSpec((None, n, GROUP_WIDTH), lambda r, i: (r, i, 0))
    return two, prv, one


def _pair_keys(prev_ref, two_ref, ps, n):
    cur2 = two_ref[:, ps]
    return jnp.concatenate([prev_ref[:, ps], cur2[0:n]], axis=0), cur2


def attn_core_fwd(q, k, v, g, n):
    d, M, GW = q.shape
    scale = HEAD_DIM ** -0.5

    def body(q_ref, kp_ref, kc_ref, vp_ref, vc_ref, o_ref, l_ref):
        masks = (_valid_mask(n, pl.program_id(1)), _valid_mask(n, 1))
        first = lax.broadcasted_iota(jnp.int32, (1, LANES), 1) < HEAD_DIM
        for pair in range(HEADS_PER_GROUP * HEAD_DIM // LANES):
            ps = slice(LANES * pair, LANES * (pair + 1))
            keys, vals = _pair_keys(kp_ref, kc_ref, ps, n), _pair_keys(vp_ref, vc_ref, ps, n)
            for blk in range(2):
                rows = slice(blk * n, (blk + 1) * n)
                q2 = q_ref[rows, ps]
                o2, l2 = [], []
                for sel in (first, jnp.logical_not(first)):
                    s = jnp.where(masks[blk], _dot_nt(jnp.where(sel, q2, jnp.zeros_like(q2)), keys[blk]) * scale, -1e30)
                    m = jnp.max(s, axis=1, keepdims=True)
                    p = jnp.exp(s - m)
                    den = jnp.sum(p, axis=1, keepdims=True)
                    o2.append(_dot((p / den).astype(BF16), vals[blk]))
                    l2.append(m + jnp.log(den))
                o_ref[rows, ps] = jnp.where(first, o2[0], o2[1]).astype(BF16)
                l_ref[rows, ps] = jnp.where(first, l2[0], l2[1])

    two, prv, _ = _band_specs(n)
    return pl.pallas_call(
        body, grid=(d, M // (2 * n)),
        in_specs=[two, prv, two, prv, two], out_specs=[two, two],
        out_shape=[_sds((d, M, GW), BF16), _sds((d, M, GW), F32)],
        compiler_params=_cp("arbitrary", "arbitrary"), name=f"attn_fwd_g{g}")(q, k, k, v, v)


def attn_core_bwd(q, k, v, do, rr, lse, g, n):
    d, M, GW = q.shape
    scale = HEAD_DIM ** -0.5

    def body(q_ref, kp_ref, kc_ref, vp_ref, vc_ref, do_ref, r_ref, l_ref, dq_ref, dkc_ref, dkp_ref, dvc_ref, dvp_ref):
        masks = (_valid_mask(n, pl.program_id(1)), _valid_mask(n, 1))
        first = lax.broadcasted_iota(jnp.int32, (1, LANES), 1) < HEAD_DIM
        for pair in range(HEADS_PER_GROUP * HEAD_DIM // LANES):
            ps = slice(LANES * pair, LANES * (pair + 1))
            keys, vals = _pair_keys(kp_ref, kc_ref, ps, n), _pair_keys(vp_ref, vc_ref, ps, n)
            own = []
            for blk in range(2):
                rows = slice(blk * n, (blk + 1) * n)
                q2, do2, r2 = q_ref[rows, ps], do_ref[rows, ps], r_ref[rows, ps]
                dq2, dk, dv = [], None, None
                for half, sel in enumerate((first, jnp.logical_not(first))):
                    qm = jnp.where(sel, q2, jnp.zeros_like(q2))
                    dom = jnp.where(sel, do2, jnp.zeros_like(do2))
                    s = jnp.where(masks[blk], _dot_nt(qm, keys[blk]) * scale, -1e30)
                    lane0 = LANES * pair + HEAD_DIM * half
                    p = jnp.exp(s - l_ref[rows, lane0:lane0 + 1])
                    dp = _dot_nt(dom, vals[blk])
                    delta = jnp.sum(jnp.where(sel, r2, 0.0), axis=1, keepdims=True)
                    ds = (p * (dp - delta) * scale).astype(BF16)
                    dq2.append(_dot(ds, keys[blk]))
                    dkh = _dot_tn(ds, qm)
                    dvh = _dot_tn(p.astype(BF16), dom)
                    dk = dkh if dk is None else dk + dkh
                    dv = dvh if dv is None else dv + dvh
                dq_ref[rows, ps] = jnp.where(first, dq2[0], dq2[1]).astype(BF16)
                own.append((dk, dv))
            for t, (c_ref, p_ref) in enumerate(((dkc_ref, dkp_ref), (dvc_ref, dvp_ref))):
                a, b = own[0][t], own[1][t]
                p_ref[:, ps] = a[0:n].astype(BF16)
                c_ref[0:n, ps] = (a[n:2 * n] + b[0:n]).astype(BF16)
                c_ref[n:2 * n, ps] = b[n:2 * n].astype(BF16)

    two, prv, one = _band_specs(n)
    return pl.pallas_call(
        body, grid=(d, M // (2 * n)),
        in_specs=[two, prv, two, prv, two, two, two, two], out_specs=[two, two, one, two, one],
        out_shape=[_sds((d, M, GW), BF16), _sds((d, M, GW), BF16), _sds((d, M // 2, GW), BF16),
                   _sds((d, M, GW), BF16), _sds((d, M // 2, GW), BF16)],
        compiler_params=_cp("arbitrary", "arbitrary"), name=f"attn_bwd_g{g}")(q, k, k, v, v, do, rr, lse)


def dkv_combine(cur_prev, n, name):
    d, M, GW = cur_prev[0][0].shape
    rows = min(M, 1024)
    pairs = rows // (2 * n)
    steps = M // rows
    flat = [a for pair in cur_prev for a in pair]

    def body(*refs):
        o_ref = refs[-1]
        last = pl.program_id(1) == steps - 1
        acc = None
        shifted = None
        for t in range(0, len(refs) - 1, 3):
            c = refs[t][...].astype(F32)
            nxt = jnp.where(last, 0.0, refs[t + 2][...].astype(F32))
            s = nxt if pairs == 1 else jnp.concatenate([refs[t + 1][n:pairs * n, :].astype(F32), nxt], axis=0)
            acc = c if acc is None else acc + c
            shifted = s if shifted is None else shifted + s
        for m in range(pairs):
            lo = 2 * m * n
            o_ref[lo:lo + n, :] = acc[lo:lo + n].astype(BF16)
            o_ref[lo + n:lo + 2 * n, :] = (acc[lo + n:lo + 2 * n] + shifted[m * n:(m + 1) * n]).astype(BF16)

    cur = pl.BlockSpec((None, rows, GW), lambda r, i: (r, i, 0))
    same = pl.BlockSpec((None, pairs * n, GW), lambda r, i: (r, i, 0))
    nxt = pl.BlockSpec((None, n, GW), lambda r, i: (r, jnp.minimum((i + 1) * pairs, M // (2 * n) - 1), 0))
    args = []
    for c, p in cur_prev:
        args += [c, p, p]
    return pl.pallas_call(
        body, grid=(d, steps), in_specs=[cur, same, nxt] * len(cur_prev), out_specs=cur,
        out_shape=_sds((d, M, GW), BF16),
        compiler_params=_cp("arbitrary", "arbitrary"), name=name)(*args)


def _group_weights(ls):
    mx = functools.reduce(jnp.maximum, ls)
    es = [jnp.exp(l - mx) for l in ls]
    tot = functools.reduce(lambda a, b: a + b, es)
    return [e / tot for e in es]


def attn_mix_out(os_, ls, dils, x, vec, w_o):
    T, D = x.shape
    GW = GROUP_WIDTH
    tm = min(TOKEN_TILE, T)
    ng = len(os_)

    def body(*refs):
        o_refs, l_refs = refs[:ng], refs[ng:2 * ng]
        x_ref, vec_ref, w_ref, xn_ref, mix_ref, y_ref, stage = refs[2 * ng:]
        natural = lambda ref, d: _merge_residues(lambda r: ref[r].astype(F32), d, tm, GW, stage)
        ws = _group_weights([natural(r, d) for r, d in zip(l_refs, dils)])
        mixed = functools.reduce(lambda a, b: a + b, [w * natural(r, d) for w, r, d in zip(ws, o_refs, dils)])
        mb = mixed.astype(BF16)
        mix_ref[...] = mb
        y = _dot(mb, w_ref[...])
        y_ref[...] = y.astype(BF16)
        xn_ref[...] = x_ref[...] + (1.0 + vec_ref[3:4]) * y

    res = [_residue_spec(d, tm) for d in dils]
    return pl.pallas_call(
        body, grid=(T // tm,),
        in_specs=res + res + [_tok(tm, D), _res((8, D)), _res((GW, D))],
        out_specs=[_tok(tm, D), _tok(tm, GW), _tok(tm, D)],
        out_shape=[_sds((T, D), F32), _sds((T, GW), BF16), _sds((T, D), BF16)],
        scratch_shapes=[_stage_scratch(tm)],
        compiler_params=_cp("arbitrary"), name="attn_mix_out")(*os_, *ls, x, vec, w_o)


def attn_mix_bwd(dxo, y, vec, w_o, os_, ls, dils):
    T, D = dxo.shape
    GW = GROUP_WIDTH
    tm = min(TOKEN_TILE, T)
    ng = len(os_)

    def body(*refs):
        dxo_ref, y_ref, vec_ref, w_ref = refs[:4]
        o_refs, l_refs = refs[4:4 + ng], refs[4 + ng:4 + 2 * ng]
        dy_ref = refs[4 + 2 * ng]
        do_refs = refs[5 + 2 * ng:5 + 3 * ng]
        r_refs = refs[5 + 3 * ng:5 + 4 * ng]
        part_ref, stage = refs[5 + 4 * ng], refs[6 + 4 * ng]
        natural = lambda ref, d: _merge_residues(lambda r: ref[r].astype(F32), d, tm, GW, stage)
        dxo_t = dxo_ref[...]
        dyb = (dxo_t * (1.0 + vec_ref[3:4])).astype(BF16)
        dy_ref[...] = dyb
        dgate = jnp.sum(dxo_t * y_ref[...].astype(F32), axis=0, keepdims=True)
        _acc_rows(part_ref, _rows8([dgate], D), pl.program_id(0) == 0)
        dmix = _dot_nt(dyb, w_ref[...])
        ws = _group_weights([natural(r, d) for r, d in zip(l_refs, dils)])
        mixed = functools.reduce(lambda a, b: a + b, [w * natural(r, d) for w, r, d in zip(ws, o_refs, dils)])
        for gi in range(ng):
            do = ws[gi] * dmix
            for r, rows in enumerate(_split_residues(do, dils[gi], stage)):
                do_refs[gi][r] = rows.astype(BF16)
            for r, rows in enumerate(_split_residues(do * mixed, dils[gi], stage)):
                r_refs[gi][r] = rows

    res = [_residue_spec(d, tm) for d in dils]
    return pl.pallas_call(
        body, grid=(T // tm,),
        in_specs=[_tok(tm, D), _tok(tm, D), _res((8, D)), _res((GW, D))] + res + res,
        out_specs=[_tok(tm, D)] + res + res + [pl.BlockSpec((8, D), lambda i: (0, 0))],
        out_shape=[_sds((T, D), BF16)] + [_sds((d, T // d, GW), BF16) for d in dils]
        + [_sds((d, T // d, GW), F32) for d in dils] + [_sds((8, D), F32)],
        scratch_shapes=[_stage_scratch(tm)],
        compiler_params=_cp("arbitrary"), name="attn_mix_bwd")(dxo, y, vec, w_o, *os_, *ls)


def final_loss(x, gvec, target):
    T, D = x.shape
    tm = min(TOKEN_TILE, T)

    def norm(xv, g):
        return xv * lax.rsqrt(jnp.mean(xv * xv, axis=-1, keepdims=True) + NORM_EPS) * g

    def body(x_ref, g_ref, t_ref, dx_ref, part_ref, loss_ref):
        first = pl.program_id(0) == 0
        yv, vjp = jax.vjp(norm, x_ref[...], g_ref[0:1])
        err = yv - t_ref[...]
        dx, dg = vjp(err * (1.0 / D))
        dx_ref[...] = dx
        _acc_rows(part_ref, _rows8([dg], D), first)
        tile_loss = 0.5 * jnp.sum(jnp.sum(err * err, axis=1, keepdims=True) * (1.0 / D), axis=0, keepdims=True)
        _acc_rows(loss_ref, jnp.broadcast_to(tile_loss, (8, LANES)), first)

    return pl.pallas_call(
        body, grid=(T // tm,),
        in_specs=[_tok(tm, D), _res((8, D)), _tok(tm, D)],
        out_specs=[_tok(tm, D), pl.BlockSpec((8, D), lambda i: (0, 0)), pl.BlockSpec((8, LANES), lambda i: (0, 0))],
        out_shape=[_sds((T, D), F32), _sds((8, D), F32), _sds((8, LANES), F32)],
        compiler_params=_cp("arbitrary"), name="final_loss")(x, gvec, target)


def mods_project(c_all, w, b):
    B, D = c_all.shape
    L, _, N = w.shape

    def body(c_ref, w_ref, b_ref, o_ref):
        cv = c_ref[...]
        cond = cv * _sigmoid(cv)
        o_ref[0] = jnp.dot(cond, w_ref[0], preferred_element_type=F32, precision=lax.Precision.HIGHEST) + b_ref[0]

    return pl.pallas_call(
        body, grid=(L,),
        in_specs=[pl.BlockSpec((B, D), lambda l: (0, 0)), pl.BlockSpec((1, D, N), lambda l: (l, 0, 0)),
                  pl.BlockSpec((1, 1, N), lambda l: (l, 0, 0))],
        out_specs=pl.BlockSpec((1, B, N), lambda l: (l, 0, 0)),
        out_shape=_sds((L, B, N), F32),
        compiler_params=_cp("arbitrary"), name="mods_project")(c_all, w, b)


def mods_weight_grad(c_all, dm):
    B, D = c_all.shape
    L, _, N = dm.shape

    def body(c_ref, d_ref, o_ref):
        cv = c_ref[...]
        cond = cv * _sigmoid(cv)
        o_ref[0] = lax.dot_general(cond, d_ref[0], (((0,), (0,)), ((), ())), preferred_element_type=F32,
                                   precision=lax.Precision.HIGHEST)

    return pl.pallas_call(
        body, grid=(L,),
        in_specs=[pl.BlockSpec((B, D), lambda l: (0, 0)), pl.BlockSpec((1, B, N), lambda l: (l, 0, 0))],
        out_specs=pl.BlockSpec((1, D, N), lambda l: (l, 0, 0)),
        out_shape=_sds((L, D, N), F32),
        compiler_params=_cp("arbitrary"), name="mods_weight_grad")(c_all, dm)


def _adam_math(g, w, m, v):
    m2 = ADAM_B1 * m + (1.0 - ADAM_B1) * g
    v2 = ADAM_B2 * v + (1.0 - ADAM_B2) * (g * g)
    m_hat = m2 / (1.0 - ADAM_B1 ** ADAM_STEP)
    v_hat = v2 / (1.0 - ADAM_B2 ** ADAM_STEP)
    delta = -ADAM_LR * (m_hat / (jnp.sqrt(v_hat) + ADAM_EPS) + ADAM_WD * w)
    return delta, m2, v2


def adam_update(g, w, m, v, parts, name):
    R, C = w.shape
    tr = _pick(R, 256, 8)

    def body(g_ref, w_ref, m_ref, v_ref, go_ref, d_ref, mo_ref, vo_ref):
        if parts:
            gv = g_ref[0].astype(F32)
            for s in range(1, N_DEV):
                gv = gv + g_ref[s].astype(F32)
        else:
            gv = g_ref[...]
        go_ref[...] = gv
        d_ref[...], mo_ref[...], vo_ref[...] = _adam_math(gv, w_ref[...], m_ref[...], v_ref[...])

    gspec = pl.BlockSpec((N_DEV, tr, C), lambda i: (0, i, 0)) if parts else _tok(tr, C)
    return pl.pallas_call(
        body, grid=(R // tr,),
        in_specs=[gspec, _tok(tr, C), _tok(tr, C), _tok(tr, C)],
        out_specs=[_tok(tr, C)] * 4, out_shape=[_sds((R, C), F32)] * 4,
        compiler_params=_cp("arbitrary"), name=name)(g, w, m, v)


def adam_layer(parts, w, m, v, prev, layer, after, name):
    L, R, C = w.shape
    tr = _pick(R, 256, 8)
    prev = (list(prev) if prev is not None else []) + [after]

    def body(p_ref, w_ref, m_ref, v_ref, *rest):
        go_ref, d_ref, mo_ref, vo_ref = rest[-4:]
        gv = p_ref[0].astype(F32)
        for s in range(1, N_DEV):
            gv = gv + p_ref[s].astype(F32)
        go_ref[...] = gv
        d_ref[...], mo_ref[...], vo_ref[...] = _adam_math(gv, w_ref[...], m_ref[...], v_ref[...])

    lay = pl.BlockSpec((None, tr, C), lambda i: (layer, i, 0))
    return pl.pallas_call(
        body, grid=(R // tr,),
        in_specs=[pl.BlockSpec((N_DEV, tr, C), lambda i: (0, i, 0)), lay, lay, lay] + [pl.BlockSpec(memory_space=pl.ANY)] * len(prev),
        out_specs=[lay] * 4, out_shape=[_sds((L, R, C), F32)] * 4,
        input_output_aliases={4 + k: k for k in range(len(prev) - 1)},
        compiler_params=_cp("arbitrary"), name=name)(parts, w, m, v, *prev)


def _my_id():
    return 4 * lax.axis_index("x") + 2 * lax.axis_index("y") + lax.axis_index("c")


def _peer(s):
    x, y, c = lax.axis_index("x"), lax.axis_index("y"), lax.axis_index("c")
    px = (1 - x) if s & 4 else x
    py = (1 - y) if s & 2 else y
    pc = (1 - c) if s & 1 else c
    return (px, py, pc), 4 * px + 2 * py + pc


def all_gather(xs, space, name):
    na = len(xs)

    def body(*refs):
        x_refs, o_refs = refs[:na], refs[na:2 * na]
        send_sems, recv_sems, local_sems = refs[2 * na:]
        me = _my_id()
        locals_, sends = [], []
        for a in range(na):
            cp = pltpu.make_async_copy(x_refs[a], o_refs[a].at[me], local_sems.at[a])
            cp.start()
            locals_.append(cp)
        for s in range(1, N_DEV):
            peer, _ = _peer(s)
            for a in range(na):
                cp = pltpu.make_async_remote_copy(
                    src_ref=x_refs[a], dst_ref=o_refs[a].at[me], send_sem=send_sems.at[a, s - 1],
                    recv_sem=recv_sems.at[a, s - 1], device_id=peer, device_id_type=MESH)
                cp.start()
                sends.append(cp)
        for s in range(1, N_DEV):
            peer, pid = _peer(s)
            for a in range(na):
                pltpu.make_async_remote_copy(
                    src_ref=x_refs[a], dst_ref=o_refs[a].at[pid], send_sem=send_sems.at[a, s - 1],
                    recv_sem=recv_sems.at[a, s - 1], device_id=peer, device_id_type=MESH).wait_recv()
        for cp in sends:
            cp.wait_send()
        for cp in locals_:
            cp.wait()

    spec = pl.BlockSpec(memory_space=space)
    return pl.pallas_call(
        body, in_specs=[spec] * na, out_specs=[spec] * na,
        out_shape=[_sds((N_DEV,) + x.shape, x.dtype) for x in xs],
        scratch_shapes=[pltpu.SemaphoreType.DMA((na, N_DEV - 1)), pltpu.SemaphoreType.DMA((na, N_DEV - 1)),
                        pltpu.SemaphoreType.DMA((na,))],
        compiler_params=pltpu.CompilerParams(vmem_limit_bytes=VMEM_LIMIT), name=name)(*xs)


def exchange_slots(xs, name):
    na = len(xs)

    def body(*refs):
        x_refs, o_refs = refs[:na], refs[na:2 * na]
        send_sems, recv_sems, local_sems = refs[2 * na:]
        me = _my_id()
        locals_, sends = [], []
        for a in range(na):
            cp = pltpu.make_async_copy(x_refs[a].at[me], o_refs[a].at[me], local_sems.at[a])
            cp.start()
            locals_.append(cp)
        for s in range(1, N_DEV):
            peer, pid = _peer(s)
            for a in range(na):
                cp = pltpu.make_async_remote_copy(
                    src_ref=x_refs[a].at[pid], dst_ref=o_refs[a].at[me], send_sem=send_sems.at[a, s - 1],
                    recv_sem=recv_sems.at[a, s - 1], device_id=peer, device_id_type=MESH)
                cp.start()
                sends.append(cp)
        for s in range(1, N_DEV):
            peer, pid = _peer(s)
            for a in range(na):
                pltpu.make_async_remote_copy(
                    src_ref=x_refs[a].at[pid], dst_ref=o_refs[a].at[pid], send_sem=send_sems.at[a, s - 1],
                    recv_sem=recv_sems.at[a, s - 1], device_id=peer, device_id_type=MESH).wait_recv()
        for cp in sends:
            cp.wait_send()
        for cp in locals_:
            cp.wait()

    spec = pl.BlockSpec(memory_space=pl.ANY)
    return pl.pallas_call(
        body, in_specs=[spec] * na, out_specs=[spec] * na,
        out_shape=[_sds(x.shape, x.dtype) for x in xs],
        scratch_shapes=[pltpu.SemaphoreType.DMA((na, N_DEV - 1)), pltpu.SemaphoreType.DMA((na, N_DEV - 1)),
                        pltpu.SemaphoreType.DMA((na,))],
        compiler_params=pltpu.CompilerParams(vmem_limit_bytes=VMEM_LIMIT), name=name)(*xs)


_HBM = pl.BlockSpec(memory_space=pltpu.HBM)
_SEM = pl.BlockSpec(memory_space=pltpu.SEMAPHORE)
_EFFECT = pltpu.SideEffectType.DATAFLOW_SIDE_EFFECTING


def _split_copy(x_ref, land_ref, s, send_sem, recv_sem, scatter):
    peer, pid = _peer(s)
    src = x_ref.at[pid] if scatter else x_ref
    return pltpu.make_async_remote_copy(src_ref=src, dst_ref=land_ref.at[_my_id()], send_sem=send_sem, recv_sem=recv_sem,
                                        device_id=peer, device_id_type=MESH)


def comm_start(xs, scatter, after, name):
    na = len(xs)
    extra = [] if after is None else [after]
    me = _my_id()
    lands = []
    for x in xs:
        shape = x.shape if scatter else (N_DEV,) + x.shape
        own = lax.dynamic_slice_in_dim(x, me, 1, 0) if scatter else x[None]
        lands.append(lax.dynamic_update_slice(lax.empty(shape, x.dtype), own, (me,) + (0,) * (len(shape) - 1)))

    def body(*refs):
        x_refs, land_refs = refs[:na], refs[na:2 * na]
        send_sem, recv_sem = refs[2 * na + len(extra)], refs[2 * na + len(extra) + 1]
        token = refs[-1]
        for s in range(1, N_DEV):
            for a in range(na):
                _split_copy(x_refs[a], land_refs[a], s, send_sem, recv_sem, scatter).start()
        token[...] = jnp.zeros_like(token)

    outs = pl.pallas_call(
        body, name=name,
        out_shape=(pltpu.SemaphoreType.DMA(()), pltpu.SemaphoreType.DMA(()))
        + tuple(pltpu.HBM(x.shape, x.dtype) for x in xs) + tuple(pltpu.HBM(l.shape, l.dtype) for l in lands)
        + (_sds((8, LANES), F32),),
        in_specs=(_HBM,) * (2 * na) + (pl.BlockSpec(memory_space=pl.ANY),) * len(extra),
        out_specs=(_SEM, _SEM) + (_HBM,) * (2 * na) + (pl.BlockSpec(memory_space=pltpu.VMEM),),
        input_output_aliases={a: 2 + a for a in range(2 * na)},
        compiler_params=pltpu.CompilerParams(has_side_effects=_EFFECT),
    )(*[pltpu.with_memory_space_constraint(x, pltpu.HBM) for x in xs],
      *[pltpu.with_memory_space_constraint(l, pltpu.HBM) for l in lands], *extra)
    return dict(sems=outs[0:2], xs=outs[2:2 + na], lands=outs[2 + na:2 + 2 * na], token=outs[-1], scatter=scatter)


def comm_wait(started, after, name):
    xs, lands = started["xs"], started["lands"]
    scatter = started["scatter"]
    na = len(xs)

    def body(*refs):
        x_refs, land_refs = refs[:na], refs[na:2 * na]
        send_sem, recv_sem = refs[2 * na], refs[2 * na + 1]
        for s in range(1, N_DEV):
            for a in range(na):
                cp = _split_copy(x_refs[a], land_refs[a], s, send_sem, recv_sem, scatter)
                cp.wait_send()
                cp.wait_recv()

    outs = pl.pallas_call(
        body, name=name,
        out_shape=tuple(pltpu.HBM(x.shape, x.dtype) for x in xs) + tuple(pltpu.HBM(l.shape, l.dtype) for l in lands),
        in_specs=(_HBM,) * (2 * na) + (_SEM, _SEM, pl.BlockSpec(memory_space=pl.ANY)),
        out_specs=(_HBM,) * (2 * na),
        input_output_aliases={a: a for a in range(2 * na)},
        compiler_params=pltpu.CompilerParams(has_side_effects=_EFFECT),
    )(*xs, *lands, *started["sems"], after)
    return list(outs[na:])


def _cols_to_natural(g):
    return jnp.concatenate([g[k] for k in range(N_DEV)], axis=1)


def _cols_to_slots(w):
    ns = w.shape[1] // N_DEV
    return jnp.stack([w[:, k * ns:(k + 1) * ns] for k in range(N_DEV)])


def _vec8(rows, d):
    rows = [r.reshape(1, d).astype(F32) for r in rows]
    return jnp.concatenate(rows + [jnp.zeros((8 - len(rows), d), F32)], axis=0)


def _ffn_forward(x, vec, w_in_t, w_out):
    xn, h, a, b, u, y = ffn_fwd(x, vec, w_in_t, w_out)
    return xn, (x, h, a, b, u, y)


def _ffn_backward(dxo, saved, vec, w_in_t, w_out, on_rows=None):
    x, h, a, b, u, y = saved
    dy, dab, dx, part = ffn_bwd(dxo, y, vec, w_out, w_in_t, a, b, x)
    rows = part[0:4]
    token = on_rows(rows) if on_rows is not None else None
    g_out = grad_slots(u, dy, "ffn_dw_out", after=token)
    g_in_t = grad_slots(dab, h, "ffn_dw_in", after=token)
    return dx, g_in_t, g_out, rows


_TRANSPOSED = ("ffn1_w_in", "ffn2_w_in", "attn_w_q")
_COL_NATURAL = ("conv_w_in", "w_kv", "attn_w_o")
_ROW_SHARDED = ("ffn1_w_out", "ffn2_w_out", "conv_w_out")
_BIG = _TRANSPOSED + _COL_NATURAL + _ROW_SHARDED


def weight_chunks():
    chunks = []
    for layer in range(DEPTH):
        first = [("ffn1_w_in", layer), ("ffn1_w_out", layer)]
        if layer == N_A_LAYERS:
            first = [("w_kv", layer)] + first
        mixer = [("conv_w_in", layer), ("conv_w_out", layer)] if layer < N_A_LAYERS else [("attn_w_q", layer), ("attn_w_o", layer)]
        rest = mixer + [("ffn2_w_in", layer), ("ffn2_w_out", layer)]
        chunks += [first, rest] if layer == 0 else [first + rest]
    return chunks


def stacked_index(name, layer):
    if name == "w_kv":
        return None
    return layer - N_A_LAYERS if name.startswith("attn") else layer


class ChunkComm:
    def __init__(self, shards):
        self.shards = shards
        self.chunks = weight_chunks()

    def _shard(self, name, layer):
        idx = stacked_index(name, layer)
        return self.shards[name][0 if idx is None else idx]

    def start_gather(self, ci, after):
        xs = [self._shard(n, l).astype(BF16) for n, l in self.chunks[ci]]
        return comm_start(xs, False, after, f"gather_start_{ci}")

    def finish_gather(self, ci, started, after):
        lands = comm_wait(started, after, f"gather_wait_{ci}")
        W = {}
        for key, g in zip(self.chunks[ci], lands):
            W[key] = _cols_to_natural(g) if key[0] in _COL_NATURAL else g.reshape(-1, g.shape[2])
        return W, lands[0]

    def start_exchange(self, ci, slots, after):
        return comm_start([slots[key] for key in self.chunks[ci]], True, after, f"exchange_start_{ci}")

    def finish_exchange(self, ci, started, after):
        lands = comm_wait(started, after, f"exchange_wait_{ci}")
        return dict(zip(self.chunks[ci], lands))


def device_step(x, positions, target, mods, kvmods, small, comm, gather0):
    T, D = x.shape
    groups = DILATED_GROUPS
    dils = [dil for _, dil in groups]
    lane = jnp.arange(LANES) % HEAD_DIM
    inv = ROPE_THETA ** (-jnp.arange(0, ROPE_DIM, 2, dtype=F32) / ROPE_DIM)
    lane_rows = _vec8([jnp.where(lane < ROPE_DIM, inv[lane % (ROPE_DIM // 2)], 0.0), lane < ROPE_DIM,
                       (lane >= ROPE_DIM // 2) & (lane < ROPE_DIM), lane < ROPE_DIM // 2], LANES)
    tabs = rope_tables(positions.reshape(T, 1), lane_rows)

    def after_token(v, token):
        return v if token is None else v + token[0, 0]

    def vec_of(layer, sub, token=None):
        return after_token(_vec8([small["norm_g"][layer, sub], mods[layer, 3 * sub], mods[layer, 3 * sub + 1],
                                  mods[layer, 3 * sub + 2]], D), token)

    saved = []
    kv_saved = None
    k_sh = v_sh = None
    qw = GROUP_WIDTH * len(groups)
    chunk_of = {key: ci for ci, chunk in enumerate(comm.chunks) for key in chunk}
    W = {}
    flight = {"ci": 0, "started": gather0, "token": None}

    def need(key, after):
        if key not in W:
            ci = chunk_of[key]
            assert ci == flight["ci"], (key, ci)
            got, landed = comm.finish_gather(ci, flight["started"], after)
            W.update(got)
            if ci + 1 < len(comm.chunks):
                flight.update(ci=ci + 1, started=comm.start_gather(ci + 1, landed))
                flight["token"] = flight["started"]["token"]
        return W[key]

    def behind_start(v):
        token, flight["token"] = flight["token"], None
        return after_token(v, token)

    for layer in range(DEPTH):
        if layer == N_A_LAYERS:
            w_kv = need(("w_kv", layer), x)
            kv_vec = behind_start(_vec8([small["kv_norm_g"], kvmods[0], kvmods[1]], D))
            h_kv, *kv_pieces = proj_rope_fwd(x, kv_vec, w_kv, tabs, qw, False, dils, "kv_fwd")
            k_sh, v_sh = kv_pieces[:len(groups)], kv_pieces[len(groups):]
            kv_saved = (x, h_kv, kv_vec)
        rec = {}
        w_in, w_out = need(("ffn1_w_in", layer), x), need(("ffn1_w_out", layer), x)
        v1 = behind_start(vec_of(layer, 0))
        x, rec["ffn1"] = _ffn_forward(x, v1, w_in, w_out)
        if layer < N_A_LAYERS:
            w_in, w_out = need(("conv_w_in", layer), x), need(("conv_w_out", layer), x)
            v2 = behind_start(vec_of(layer, 1))
            cw = _vec8(list(small["conv_w"][layer]), D)
            x_in = x
            x, h, bcu, cv, z, y = conv_fwd(x, v2, cw, w_in, w_out)
            rec["mix"] = (x_in, h, bcu, cv, z, y, cw)
        else:
            w_q, w_o = need(("attn_w_q", layer), x), need(("attn_w_o", layer), x)
            v2 = behind_start(vec_of(layer, 1))
            x_in = x
            h, *q = proj_rope_fwd(x, v2, w_q, tabs, qw, True, dils, "q_fwd")
            os_, ls = [], []
            for g, (win, dil) in enumerate(groups):
                o, l = attn_core_fwd(q[g], k_sh[g], v_sh[g], g, win // dil)
                os_.append(o)
                ls.append(l)
            x, mixed, y = attn_mix_out(os_, ls, dils, x, v2, w_o)
            rec["mix"] = (x_in, h, q, os_, ls, mixed, y)
        w_in, w_out = need(("ffn2_w_in", layer), x), need(("ffn2_w_out", layer), x)
        v3 = behind_start(vec_of(layer, 2))
        x, rec["ffn2"] = _ffn_forward(x, v3, w_in, w_out)
        rec["vecs"] = (v1, v2, v3)
        saved.append(rec)

    dx, part_final, loss_tile = final_loss(x, _vec8([small["final_norm_g"]], D), target)
    loss = loss_tile[0, 0]

    conv_rows = [None] * N_A_LAYERS
    kv_rows = None
    mod_rows = [[None] * 3 for _ in range(DEPTH)]
    dkv_pairs = [{"k": [], "v": []} for _ in groups]
    slots = {}
    exchanges = []
    token = None

    def send_ready_chunks():
        nonlocal token
        for ci in reversed(range(len(comm.chunks))):
            if ci not in [e[0] for e in exchanges] and all(key in slots for key in comm.chunks[ci]):
                started = comm.start_exchange(ci, slots, token)
                exchanges.append((ci, started))
                token = started["token"]

    vector_gather = {}

    def start_vector_gather(rows0):
        mod_rows[0][0] = rows0
        rows = jnp.stack([jnp.stack(r) for r in mod_rows])
        vecs = jnp.concatenate([rows[:, :, 1:4].reshape(-1), kv_rows[1:3].reshape(-1), kv_rows[0], part_final[0],
                                rows[:, :, 0].reshape(-1), jnp.stack(conv_rows).reshape(-1)])
        vector_gather["count"] = vecs.shape[0]
        vecs = _pad_rows(vecs.reshape(-1, 1), 8 * LANES).reshape(-1, LANES)
        vector_gather["started"] = comm_start([vecs], False, None, "vector_grads_start")
        return vector_gather["started"]["token"]

    for layer in reversed(range(DEPTH)):
        rec = saved[layer]
        v1, v2, v3 = rec["vecs"]
        dx, slots[("ffn2_w_in", layer)], slots[("ffn2_w_out", layer)], mod_rows[layer][2] = _ffn_backward(
            dx, rec["ffn2"], after_token(v3, token), W[("ffn2_w_in", layer)], W[("ffn2_w_out", layer)])
        if layer < N_A_LAYERS:
            x_in, h, bcu, cv, z, y, cw = rec["mix"]
            dx, dy, dbcu, part, dcw = conv_bwd(dx, x_in, y, bcu, cv, v2, cw, W[("conv_w_in", layer)], W[("conv_w_out", layer)])
            slots[("conv_w_out", layer)] = grad_slots(z, dy, "conv_dw_out")
            slots[("conv_w_in", layer)] = grad_slots(h, dbcu, "conv_dw_in", col_slots=True)
            conv_rows[layer] = dcw[0:3]
            mod_rows[layer][1] = part[0:4]
        else:
            x_in, h, q, os_, ls, mixed, y = rec["mix"]
            outs = attn_mix_bwd(dx, y, v2, W[("attn_w_o", layer)], os_, ls, dils)
            ng = len(groups)
            dy, dos, rrs, part_gate = outs[0], outs[1:1 + ng], outs[1 + ng:1 + 2 * ng], outs[1 + 2 * ng]
            slots[("attn_w_o", layer)] = grad_slots(mixed, dy, "attn_dw_o", col_slots=True)
            dqs = []
            for g, (win, dil) in enumerate(groups):
                dq, dkc, dkp, dvc, dvp = attn_core_bwd(q[g], k_sh[g], v_sh[g], dos[g], rrs[g], ls[g], g, win // dil)
                dqs.append(dq)
                dkv_pairs[g]["k"].append((dkc, dkp))
                dkv_pairs[g]["v"].append((dvc, dvp))
            dx, dqr, part_norm = proj_rope_bwd(dqs, dils, x_in, dx, v2, W[("attn_w_q", layer)], tabs, qw, True, "q_bwd")
            slots[("attn_w_q", layer)] = grad_slots(dqr, h, "attn_dw_q")
            mod_rows[layer][1] = jnp.concatenate([part_norm[0:3], part_gate[0:1]], axis=0)
        send_ready_chunks()
        dx, slots[("ffn1_w_in", layer)], slots[("ffn1_w_out", layer)], mod_rows[layer][0] = _ffn_backward(
            dx, rec["ffn1"], after_token(v1, token), W[("ffn1_w_in", layer)], W[("ffn1_w_out", layer)],
            on_rows=start_vector_gather if layer == 0 else None)
        if layer == N_A_LAYERS:
            x_kv, h_kv, kv_vec = kv_saved
            dparts = [dkv_combine(dkv_pairs[g]["k"], win // dil, f"dk_combine_g{g}") for g, (win, dil) in enumerate(groups)]
            dparts += [dkv_combine(dkv_pairs[g]["v"], win // dil, f"dv_combine_g{g}") for g, (win, dil) in enumerate(groups)]
            dx, dkvp, part_kv = proj_rope_bwd(dparts, dils, x_kv, dx, kv_vec, W[("w_kv", layer)], tabs, qw, False, "kv_bwd")
            slots[("w_kv", layer)] = grad_slots(h_kv, dkvp, "kv_dw", col_slots=True)
            kv_rows = part_kv[0:3]
        send_ready_chunks()

    return loss, dx, {"exchanges": exchanges, "vector_gather": vector_gather}


def _flat2(a):
    return a.reshape(-1, a.shape[-1])


def _pad_rows(a, mult):
    r = a.shape[0]
    pad = (-r) % mult
    return a if pad == 0 else jnp.concatenate([a, jnp.zeros((pad,) + a.shape[1:], a.dtype)], axis=0)


def kernel(x, c, positions, norm_g, ada_w, ada_b, ffn1_w_in, ffn1_w_out, ffn2_w_in, ffn2_w_out, conv_w_in, conv_w, conv_w_out, kv_norm_g, kv_ada_w, kv_ada_b, w_kv, attn_w_q, attn_w_o, final_norm_g, loss_target, m_norm_g, m_ada_w, m_ada_b, m_ffn1_w_in, m_ffn1_w_out, m_ffn2_w_in, m_ffn2_w_out, m_conv_w_in, m_conv_w, m_conv_w_out, m_kv_norm_g, m_kv_ada_w, m_kv_ada_b, m_w_kv, m_attn_w_q, m_attn_w_o, m_final_norm_g, v_norm_g, v_ada_w, v_ada_b, v_ffn1_w_in, v_ffn1_w_out, v_ffn2_w_in, v_ffn2_w_out, v_conv_w_in, v_conv_w, v_conv_w_out, v_kv_norm_g, v_kv_ada_w, v_kv_ada_b, v_w_kv, v_attn_w_q, v_attn_w_o, v_final_norm_g):
    names = ("norm_g", "ada_w", "ada_b", "ffn1_w_in", "ffn1_w_out", "ffn2_w_in", "ffn2_w_out", "conv_w_in", "conv_w",
             "conv_w_out", "kv_norm_g", "kv_ada_w", "kv_ada_b", "w_kv", "attn_w_q", "attn_w_o", "final_norm_g")
    wts = dict(zip(names, (norm_g, ada_w, ada_b, ffn1_w_in, ffn1_w_out, ffn2_w_in, ffn2_w_out, conv_w_in, conv_w, conv_w_out,
                           kv_norm_g, kv_ada_w, kv_ada_b, w_kv, attn_w_q, attn_w_o, final_norm_g)))
    mom = dict(zip(names, (m_norm_g, m_ada_w, m_ada_b, m_ffn1_w_in, m_ffn1_w_out, m_ffn2_w_in, m_ffn2_w_out, m_conv_w_in,
                           m_conv_w, m_conv_w_out, m_kv_norm_g, m_kv_ada_w, m_kv_ada_b, m_w_kv, m_attn_w_q, m_attn_w_o,
                           m_final_norm_g)))
    var = dict(zip(names, (v_norm_g, v_ada_w, v_ada_b, v_ffn1_w_in, v_ffn1_w_out, v_ffn2_w_in, v_ffn2_w_out, v_conv_w_in,
                           v_conv_w, v_conv_w_out, v_kv_norm_g, v_kv_ada_w, v_kv_ada_b, v_w_kv, v_attn_w_q, v_attn_w_o,
                           v_final_norm_g)))
    T, D = x.shape[1], x.shape[2]
    me = _my_id()
    nmod = ada_w.shape[2]
    nkv = kv_ada_w.shape[1]

    def stacked(w, n):
        w = w if w.ndim == 3 else w[None]
        return jnp.swapaxes(w, 1, 2) if n in _TRANSPOSED else w

    comm = ChunkComm({n: stacked(wts[n], n) for n in _BIG})
    W = {}

    ds = norm_g.shape[2]
    small = jnp.concatenate([c.reshape(-1), norm_g.reshape(-1), conv_w.reshape(-1)]).astype(F32)
    n_small = small.shape[0]
    small = _pad_rows(small.reshape(-1, 1), 8 * LANES).reshape(-1, LANES)
    (small_all,) = all_gather([small], pltpu.VMEM, "gather_small")
    small_all = small_all.reshape(N_DEV, -1)[:, :n_small]
    c_all = small_all[:, :D]
    def full_rows(off, count):
        return jnp.stack([small_all[:, off + i * ds:off + (i + 1) * ds].reshape(D) for i in range(count)])

    W["norm_g"] = full_rows(D, DEPTH * 3).reshape(DEPTH, 3, D)
    W["conv_w"] = full_rows(D + DEPTH * 3 * ds, N_A_LAYERS * 3).reshape(N_A_LAYERS, 3, D)
    W["kv_norm_g"], W["final_norm_g"] = kv_norm_g, final_norm_g

    ada_b_mine = lax.dynamic_slice_in_dim(ada_b, me * nmod, nmod, axis=1).reshape(DEPTH, 1, nmod)
    kv_b_mine = lax.dynamic_slice_in_dim(kv_ada_b, me * nkv, nkv, axis=0).reshape(1, 1, nkv)
    mods_cols = mods_project(c_all, ada_w, ada_b_mine)
    kv_cols = mods_project(c_all, kv_ada_w.reshape(1, D, nkv), kv_b_mine)
    mcat = jnp.concatenate([mods_cols[l] for l in range(DEPTH)] + [kv_cols[0]], axis=1)
    wm = mcat.shape[1]
    if wm % LANES:
        mcat = jnp.concatenate([mcat, jnp.zeros((N_DEV, LANES - wm % LANES), F32)], axis=1)
    (mods_all,) = exchange_slots([mcat.reshape(N_DEV, 1, -1)], "exchange_mods")
    gather0 = comm.start_gather(0, mods_all)
    mods_all = mods_all.reshape(N_DEV, -1)
    mods = jnp.stack([mods_all[:, l * nmod:(l + 1) * nmod].reshape(N_MOD, D) for l in range(DEPTH)])
    kvmods = mods_all[:, DEPTH * nmod:DEPTH * nmod + nkv].reshape(2, D)

    loss_local, dx, grads = device_step(x[0], positions[0], loss_target[0], mods, kvmods, W, comm, gather0)
    loss = lax.psum(loss_local, MESH_AXES)

    (vec_all,) = comm_wait(grads["vector_gather"]["started"], grads["exchanges"][-1][1]["token"], "vector_grads_wait")
    vec_all = vec_all.reshape(N_DEV, -1)[:, :grads["vector_gather"]["count"]]
    nm_, nk_ = DEPTH * N_MOD * D, 2 * D
    dmods_all = vec_all[:, :nm_].reshape(N_DEV, DEPTH, N_MOD * D)
    dkvm_all = vec_all[:, nm_:nm_ + nk_]
    rest = vec_all[:, nm_ + nk_:]
    parts_kv_norm, parts_final = rest[:, :D].reshape(N_DEV, 1, D), rest[:, D:2 * D].reshape(N_DEV, 1, D)
    parts_norm = lax.dynamic_slice_in_dim(rest[:, 2 * D:2 * D + DEPTH * 3 * D].reshape(N_DEV, DEPTH * 3, D), me * ds, ds, axis=2)
    parts_conv = lax.dynamic_slice_in_dim(rest[:, 2 * D + DEPTH * 3 * D:].reshape(N_DEV, N_A_LAYERS * 3, D), me * ds, ds, axis=2)
    dm_cols = lax.dynamic_slice_in_dim(dmods_all, me * nmod, nmod, axis=2)
    dm_mine = jnp.stack([dm_cols[:, l] for l in range(DEPTH)])
    dkv_mine = lax.dynamic_slice_in_dim(dkvm_all, me * nkv, nkv, axis=1).reshape(1, N_DEV, nkv)
    g_ada_w = mods_weight_grad(c_all, dm_mine)
    g_kv_ada_w = mods_weight_grad(c_all, dkv_mine)[0]

    out_g, out_d, out_m, out_v = {}, {}, {}, {}

    def update(n, g, w, parts=False):
        shp = w.shape
        w2 = w.reshape(1, -1) if w.ndim == 1 else _flat2(w)
        g2 = g if parts else g.reshape(w2.shape)
        res = adam_update(g2, w2, mom[n].reshape(w2.shape), var[n].reshape(w2.shape), parts, "adam_" + n)
        out_g[n], out_d[n], out_m[n], out_v[n] = (r.reshape(shp) for r in res)

    moms = {n: stacked(mom[n], n) for n in _BIG}
    vars_ = {n: stacked(var[n], n) for n in _BIG}
    results = {}
    after = dx
    for ci, started in grads["exchanges"]:
        for (n, layer), parts in comm.finish_exchange(ci, started, after).items():
            idx = stacked_index(n, layer)
            results[n] = adam_layer(parts, comm.shards[n], moms[n], vars_[n], results.get(n), 0 if idx is None else idx,
                                    after, f"adam_{n}_{layer}")
            after = results[n][1]
    for n in _BIG:
        res = [jnp.swapaxes(r, 1, 2) if n in _TRANSPOSED else r for r in results[n]]
        out_g[n], out_d[n], out_m[n], out_v[n] = (r.reshape(wts[n].shape) for r in res)
    update("ada_w", g_ada_w, ada_w)
    update("kv_ada_w", g_kv_ada_w, kv_ada_w)
    update("ada_b", dmods_all, ada_b, True)
    update("kv_ada_b", dkvm_all.reshape(N_DEV, 1, nk_), kv_ada_b, True)
    update("kv_norm_g", parts_kv_norm, kv_norm_g, True)
    update("final_norm_g", parts_final, final_norm_g, True)
    update("norm_g", parts_norm, norm_g, True)
    update("conv_w", parts_conv, conv_w, True)

    return (loss, dx.reshape(x.shape), *[out_g[n] for n in names], *[out_d[n] for n in names],
            *[out_m[n] for n in names], *[out_v[n] for n in names])
```

```python
import functools

import jax
import jax.numpy as jnp
from jax import lax
from jax.experimental import pallas as pl
from jax.experimental.pallas import tpu as pltpu

F32, BF16 = jnp.float32, jnp.bfloat16

N_DEV = 8
MESH_AXES = ("x", "y", "c")
DEPTH = 4
N_A_LAYERS = 2
HEAD_DIM = 64
HEADS_PER_GROUP = 8
GROUP_WIDTH = HEAD_DIM * HEADS_PER_GROUP
DILATED_GROUPS = ((128, 1), (512, 4), (2048, 16))
ROPE_DIM = HEAD_DIM // 4
ROPE_THETA = 500000.0
NORM_EPS = 1e-5
FFN_RES_WEIGHT = 0.5
N_MOD = 9
ADAM_LR, ADAM_B1, ADAM_B2, ADAM_EPS, ADAM_WD, ADAM_STEP = 0.001, 0.9, 0.999, 1e-08, 0.01, 10

LANES = 128
TOKEN_TILE = 512
FFN_BWD_TILE = 256
CONTRACT_TILE = 2048
MXU_WIDTH = 256
VMEM_LIMIT = 56 * 1024 * 1024
MESH = pl.DeviceIdType.MESH


def _cp(*sem):
    return pltpu.CompilerParams(dimension_semantics=sem, vmem_limit_bytes=VMEM_LIMIT)


def _pick(n, cap, mult=LANES):
    if n <= cap:
        return n
    best = None
    for t in range(mult, cap + 1, mult):
        if n % t == 0:
            best = t
    assert best is not None, (n, cap)
    return best


def _tok(tm, w):
    return pl.BlockSpec((tm, w), lambda i: (i, 0))


def _res(shape):
    nd = len(shape)
    return pl.BlockSpec(shape, lambda *_: (0,) * nd, pipeline_mode=pl.Buffered(1))


def _sds(shape, dt):
    return jax.ShapeDtypeStruct(shape, dt)


def _sigmoid(a):
    return 1.0 / (1.0 + jnp.exp(-a))


def _modnorm(x, g, sh, sc):
    r = lax.rsqrt(jnp.mean(x * x, axis=-1, keepdims=True) + NORM_EPS)
    return (x * r * g) * (1.0 + sc) + sh


def _dot(a, b):
    return jnp.dot(a, b, preferred_element_type=F32)


def _dot_nt(a, b):
    return lax.dot_general(a, b, (((1,), (1,)), ((), ())), preferred_element_type=F32)


def _dot_tn(a, b):
    return lax.dot_general(a, b, (((0,), (0,)), ((), ())), preferred_element_type=F32)


def _rows8(rows, d):
    pad = 8 - len(rows)
    return jnp.concatenate(list(rows) + [jnp.zeros((pad, d), F32)], axis=0)


def _acc_rows(ref, tile, first):
    @pl.when(first)
    def _():
        ref[...] = tile

    @pl.when(jnp.logical_not(first))
    def _():
        ref[...] += tile


def ffn_up(x, vec, w_in_t):
    T, D = x.shape
    F = w_in_t.shape[0] // 2
    tm, cw = min(TOKEN_TILE, T), _pick(F, MXU_WIDTH)

    def body(x_ref, vec_ref, w_ref, h_ref, ga_ref, gb_ref, u_ref):
        hb = _modnorm(x_ref[...], vec_ref[0:1], vec_ref[1:2], vec_ref[2:3]).astype(BF16)
        h_ref[...] = hb
        for c in range(F // cw):
            lo, hi = c * cw, (c + 1) * cw
            a = _dot_nt(hb, w_ref[lo:hi, :])
            b = _dot_nt(hb, w_ref[F + lo:F + hi, :])
            sg = _sigmoid(a)
            silu = a * sg
            ga_ref[:, lo:hi] = (b * (sg + silu * (1.0 - sg))).astype(BF16)
            gb_ref[:, lo:hi] = silu.astype(BF16)
            u_ref[:, lo:hi] = (silu * b).astype(BF16)

    return pl.pallas_call(
        body, grid=(T // tm,),
        in_specs=[_tok(tm, D), _res((8, D)), _res((2 * F, D))],
        out_specs=[_tok(tm, D), _tok(tm, F), _tok(tm, F), _tok(tm, F)],
        out_shape=[_sds((T, D), BF16), _sds((T, F), BF16), _sds((T, F), BF16), _sds((T, F), BF16)],
        compiler_params=_cp("arbitrary"), name="ffn_up")(x, vec, w_in_t)


def ffn_fwd(x, vec, w_in_t, w_out):
    T, D = x.shape
    F = w_in_t.shape[0] // 2
    tm, cw = min(TOKEN_TILE, T), _pick(F, MXU_WIDTH)

    def body(x_ref, vec_ref, wi_ref, wo_ref, xn_ref, h_ref, ga_ref, gb_ref, u_ref, y_ref):
        x_t = x_ref[...]
        hb = _modnorm(x_t, vec_ref[0:1], vec_ref[1:2], vec_ref[2:3]).astype(BF16)
        h_ref[...] = hb
        for c in range(F // cw):
            lo, hi = c * cw, (c + 1) * cw
            a = _dot_nt(hb, wi_ref[lo:hi, :])
            b = _dot_nt(hb, wi_ref[F + lo:F + hi, :])
            sg = _sigmoid(a)
            silu = a * sg
            ga_ref[:, lo:hi] = (b * (sg + silu * (1.0 - sg))).astype(BF16)
            gb_ref[:, lo:hi] = silu.astype(BF16)
            u_ref[:, lo:hi] = (silu * b).astype(BF16)
        y = _dot(u_ref[...], wo_ref[...])
        y_ref[...] = y.astype(BF16)
        xn_ref[...] = x_t + (FFN_RES_WEIGHT * (1.0 + vec_ref[3:4])) * y

    return pl.pallas_call(
        body, grid=(T // tm,),
        in_specs=[_tok(tm, D), _res((8, D)), _res((2 * F, D)), _res((F, D))],
        out_specs=[_tok(tm, D), _tok(tm, D), _tok(tm, F), _tok(tm, F), _tok(tm, F), _tok(tm, D)],
        out_shape=[_sds((T, D), F32), _sds((T, D), BF16), _sds((T, F), BF16), _sds((T, F), BF16), _sds((T, F), BF16),
                   _sds((T, D), BF16)],
        compiler_params=_cp("arbitrary"), name="ffn_fwd")(x, vec, w_in_t, w_out)


def proj_out(u, x, vec, w_out, res_weight, name):
    T, D = x.shape
    K = u.shape[1]
    tm = min(TOKEN_TILE, T)

    def body(u_ref, x_ref, vec_ref, w_ref, xn_ref, y_ref):
        y = _dot(u_ref[...], w_ref[...])
        y_ref[...] = y.astype(BF16)
        xn_ref[...] = x_ref[...] + (res_weight * (1.0 + vec_ref[3:4])) * y

    return pl.pallas_call(
        body, grid=(T // tm,),
        in_specs=[_tok(tm, K), _tok(tm, D), _res((8, D)), _res((K, D))],
        out_specs=[_tok(tm, D), _tok(tm, D)],
        out_shape=[_sds((T, D), F32), _sds((T, D), BF16)],
        compiler_params=_cp("arbitrary"), name=name)(u, x, vec, w_out)


def ffn_down_bwd(dxo, y, vec, w_out, a, b):
    T, D = dxo.shape
    F = a.shape[1]
    tm, cw = min(TOKEN_TILE, T), _pick(F, MXU_WIDTH)

    def body(dxo_ref, y_ref, vec_ref, w_ref, a_ref, b_ref, dy_ref, dab_ref, part_ref):
        dxo_t = dxo_ref[...]
        dyb = (dxo_t * (FFN_RES_WEIGHT * (1.0 + vec_ref[3:4]))).astype(BF16)
        dy_ref[...] = dyb
        dgate = FFN_RES_WEIGHT * jnp.sum(dxo_t * y_ref[...].astype(F32), axis=0, keepdims=True)
        _acc_rows(part_ref, _rows8([dgate], D), pl.program_id(0) == 0)
        for c in range(F // cw):
            lo, hi = c * cw, (c + 1) * cw
            du = _dot_nt(dyb, w_ref[lo:hi, :])
            dab_ref[:, lo:hi] = (du * a_ref[:, lo:hi].astype(F32)).astype(BF16)
            dab_ref[:, F + lo:F + hi] = (du * b_ref[:, lo:hi].astype(F32)).astype(BF16)

    return pl.pallas_call(
        body, grid=(T // tm,),
        in_specs=[_tok(tm, D), _tok(tm, D), _res((8, D)), _res((F, D)), _tok(tm, F), _tok(tm, F)],
        out_specs=[_tok(tm, D), _tok(tm, 2 * F), pl.BlockSpec((8, D), lambda i: (0, 0))],
        out_shape=[_sds((T, D), BF16), _sds((T, 2 * F), BF16), _sds((8, D), F32)],
        compiler_params=_cp("arbitrary"), name="ffn_down_bwd")(dxo, y, vec, w_out, a, b)


def ffn_up_bwd(dab, w_in_t, x, dxo, vec):
    T, D = x.shape
    F2 = dab.shape[1]
    tm = min(TOKEN_TILE, T)

    def body(dab_ref, w_ref, x_ref, dxo_ref, vec_ref, dx_ref, part_ref):
        dh = _dot(dab_ref[...], w_ref[...])
        _, vjp = jax.vjp(_modnorm, x_ref[...], vec_ref[0:1], vec_ref[1:2], vec_ref[2:3])
        dx, dg, dsh, dsc = vjp(dh)
        dx_ref[...] = dxo_ref[...] + dx
        _acc_rows(part_ref, _rows8([dg, dsh, dsc], D), pl.program_id(0) == 0)

    return pl.pallas_call(
        body, grid=(T // tm,),
        in_specs=[_tok(tm, F2), _res((F2, D)), _tok(tm, D), _tok(tm, D), _res((8, D))],
        out_specs=[_tok(tm, D), pl.BlockSpec((8, D), lambda i: (0, 0))],
        out_shape=[_sds((T, D), F32), _sds((8, D), F32)],
        compiler_params=_cp("arbitrary"), name="ffn_up_bwd")(dab, w_in_t, x, dxo, vec)


def ffn_bwd(dxo, y, vec, w_out, w_in_t, a, b, x):
    T, D = x.shape
    F = a.shape[1]
    tm, cw = min(FFN_BWD_TILE, T), _pick(F, MXU_WIDTH)

    def body(dxo_ref, y_ref, vec_ref, wo_ref, wi_ref, a_ref, b_ref, x_ref, dy_ref, dab_ref, dx_ref, part_ref):
        dxo_t = dxo_ref[...]
        dyb = (dxo_t * (FFN_RES_WEIGHT * (1.0 + vec_ref[3:4]))).astype(BF16)
        dy_ref[...] = dyb
        dgate = FFN_RES_WEIGHT * jnp.sum(dxo_t * y_ref[...].astype(F32), axis=0, keepdims=True)
        for c in range(F // cw):
            lo, hi = c * cw, (c + 1) * cw
            du = _dot_nt(dyb, wo_ref[lo:hi, :])
            dab_ref[:, lo:hi] = (du * a_ref[:, lo:hi].astype(F32)).astype(BF16)
            dab_ref[:, F + lo:F + hi] = (du * b_ref[:, lo:hi].astype(F32)).astype(BF16)
        dh = _dot(dab_ref[...], wi_ref[...])
        _, vjp = jax.vjp(_modnorm, x_ref[...], vec_ref[0:1], vec_ref[1:2], vec_ref[2:3])
        dx, dg, dsh, dsc = vjp(dh)
        dx_ref[...] = dxo_t + dx
        _acc_rows(part_ref, _rows8([dg, dsh, dsc, dgate], D), pl.program_id(0) == 0)

    return pl.pallas_call(
        body, grid=(T // tm,),
        in_specs=[_tok(tm, D), _tok(tm, D), _res((8, D)), _res((F, D)), _res((2 * F, D)), _tok(tm, F), _tok(tm, F), _tok(tm, D)],
        out_specs=[_tok(tm, D), _tok(tm, 2 * F), _tok(tm, D), pl.BlockSpec((8, D), lambda i: (0, 0))],
        out_shape=[_sds((T, D), BF16), _sds((T, 2 * F), BF16), _sds((T, D), F32), _sds((8, D), F32)],
        compiler_params=_cp("arbitrary"), name="ffn_bwd")(dxo, y, vec, w_out, w_in_t, a, b, x)


def grad_slots(a, b, name, col_slots=False, after=None):
    T, M = a.shape
    extra = [] if after is None else [after]
    N = b.shape[1]
    tk = min(CONTRACT_TILE, T)
    nk = T // tk
    tmm = _pick(M, 1408)
    if col_slots:
        ns = N // N_DEV
        sp = max(s for s in (1, 2, 4, 8) if ns * s <= 1536)
        tn = ns * sp
    else:
        tn = _pick(N, 1536)

    def body(a_ref, b_ref, *rest):
        o_ref, acc = rest[-2:]
        k = pl.program_id(2)
        t = _dot_tn(a_ref[...], b_ref[...])

        @pl.when(k == 0)
        def _():
            acc[...] = t

        @pl.when(k > 0)
        def _():
            acc[...] += t

        @pl.when(k == nk - 1)
        def _():
            if col_slots:
                for s in range(sp):
                    o_ref[s] = acc[:, s * ns:(s + 1) * ns].astype(BF16)
            else:
                o_ref[...] = acc[...].astype(BF16)

    if col_slots:
        out_spec, out_shape = pl.BlockSpec((sp, tmm, ns), lambda i, j, k: (j, i, 0)), _sds((N_DEV, M, ns), BF16)
    else:
        out_spec, out_shape = pl.BlockSpec((tmm, tn), lambda i, j, k: (i, j)), _sds((M, N), BF16)
    out = pl.pallas_call(
        body, grid=(M // tmm, N // tn, nk),
        in_specs=[pl.BlockSpec((tk, tmm), lambda i, j, k: (k, i)), pl.BlockSpec((tk, tn), lambda i, j, k: (k, j))]
        + [pl.BlockSpec(memory_space=pl.ANY)] * len(extra),
        out_specs=out_spec, out_shape=out_shape,
        scratch_shapes=[pltpu.VMEM((tmm, tn), F32)],
        compiler_params=_cp("arbitrary", "arbitrary", "arbitrary"), name=name)(a, b, *extra)
    return out if col_slots else out.reshape(N_DEV, M // N_DEV, N)


def conv_fwd(x, vec, cw, w_in, w_out):
    T, D = x.shape
    tm = min(TOKEN_TILE, T)

    def body(x_ref, vec_ref, cw_ref, wi_ref, wo_ref, xn_ref, h_ref, bcu_ref, cv_ref, z_ref, y_ref, vbuf):
        @pl.when(pl.program_id(0) == 0)
        def _():
            vbuf[0:8, :] = jnp.zeros((8, D), F32)

        x_t = x_ref[...]
        hb = _modnorm(x_t, vec_ref[0:1], vec_ref[1:2], vec_ref[2:3]).astype(BF16)
        h_ref[...] = hb
        bcu = _dot(hb, wi_ref[...])
        bcu_ref[...] = bcu.astype(BF16)
        bg, v = bcu[:, 0:D], bcu[:, D:2 * D] * bcu[:, 2 * D:3 * D]
        vbuf[8:8 + tm, :] = v
        conv = cw_ref[0:1] * vbuf[6:6 + tm, :] + cw_ref[1:2] * vbuf[7:7 + tm, :] + cw_ref[2:3] * v
        cv_ref[...] = conv.astype(BF16)
        zb = (bg * conv).astype(BF16)
        z_ref[...] = zb
        y = _dot(zb, wo_ref[...])
        y_ref[...] = y.astype(BF16)
        xn_ref[...] = x_t + (1.0 + vec_ref[3:4]) * y
        vbuf[0:8, :] = vbuf[tm:tm + 8, :]

    return pl.pallas_call(
        body, grid=(T // tm,),
        in_specs=[_tok(tm, D), _res((8, D)), _res((8, D)), _res((D, 3 * D)), _res((D, D))],
        out_specs=[_tok(tm, D), _tok(tm, D), _tok(tm, 3 * D), _tok(tm, D), _tok(tm, D), _tok(tm, D)],
        out_shape=[_sds((T, D), F32), _sds((T, D), BF16), _sds((T, 3 * D), BF16), _sds((T, D), BF16),
                   _sds((T, D), BF16), _sds((T, D), BF16)],
        scratch_shapes=[pltpu.VMEM((tm + 8, D), F32)],
        compiler_params=_cp("arbitrary"), name="conv_fwd")(x, vec, cw, w_in, w_out)


def conv_bwd(dxo, x, y, bcu, cv, vec, cw, w_in, w_out):
    T, D = x.shape
    tm = min(TOKEN_TILE, T)
    nt = T // tm

    def body(dxo_ref, x_ref, y_ref, bcu_ref, cv_ref, vec_ref, cw_ref, wi_ref, wo_ref,
             dx_ref, dy_ref, dbcu_ref, part_ref, dcw_ref, dcbuf):
        first = pl.program_id(0) == 0

        @pl.when(first)
        def _():
            dcbuf[tm:tm + 8, :] = jnp.zeros((8, D), F32)

        dxo_t = dxo_ref[...]
        dyb = (dxo_t * (1.0 + vec_ref[3:4])).astype(BF16)
        dy_ref[...] = dyb
        dgate = jnp.sum(dxo_t * y_ref[...].astype(F32), axis=0, keepdims=True)
        dz = _dot_nt(dyb, wo_ref[...])
        bcu_t = bcu_ref[...].astype(F32)
        bg, cg, ug = bcu_t[:, 0:D], bcu_t[:, D:2 * D], bcu_t[:, 2 * D:3 * D]
        dconv = dz * bg
        dbg = dz * cv_ref[...].astype(F32)
        dcbuf[0:tm, :] = dconv
        d1, d2 = dcbuf[1:tm + 1, :], dcbuf[2:tm + 2, :]
        dv = cw_ref[2:3] * dconv + cw_ref[1:2] * d1 + cw_ref[0:1] * d2
        v = cg * ug
        dcw = _rows8([jnp.sum(d2 * v, axis=0, keepdims=True), jnp.sum(d1 * v, axis=0, keepdims=True),
                      jnp.sum(dconv * v, axis=0, keepdims=True)], D)
        dbcu = jnp.concatenate([dbg, dv * ug, dv * cg], axis=1).astype(BF16)
        dbcu_ref[...] = dbcu
        dh = _dot_nt(dbcu, wi_ref[...])
        _, vjp = jax.vjp(_modnorm, x_ref[...], vec_ref[0:1], vec_ref[1:2], vec_ref[2:3])
        dx, dg, dsh, dsc = vjp(dh)
        dx_ref[...] = dxo_t + dx
        _acc_rows(part_ref, _rows8([dg, dsh, dsc, dgate], D), first)
        _acc_rows(dcw_ref, dcw, first)
        dcbuf[tm:tm + 8, :] = dcbuf[0:8, :]

    def rev(w):
        return pl.BlockSpec((tm, w), lambda i: (nt - 1 - i, 0))

    return pl.pallas_call(
        body, grid=(nt,),
        in_specs=[rev(D), rev(D), rev(D), rev(3 * D), rev(D), _res((8, D)), _res((8, D)), _res((D, 3 * D)), _res((D, D))],
        out_specs=[rev(D), rev(D), rev(3 * D), pl.BlockSpec((8, D), lambda i: (0, 0)), pl.BlockSpec((8, D), lambda i: (0, 0))],
        out_shape=[_sds((T, D), F32), _sds((T, D), BF16), _sds((T, 3 * D), BF16), _sds((8, D), F32), _sds((8, D), F32)],
        scratch_shapes=[pltpu.VMEM((tm + 8, D), F32)],
        compiler_params=_cp("arbitrary"), name="conv_bwd")(dxo, x, y, bcu, cv, vec, cw, w_in, w_out)


def rope_tables(pos, lane_rows):
    T = pos.shape[0]
    tm = min(TOKEN_TILE, T)

    def body(p_ref, lr_ref, c_ref, sp_ref, sm_ref):
        ang = p_ref[...].astype(F32) * lr_ref[0:1]
        cs, sn = jnp.cos(ang), jnp.sin(ang)
        c_ref[...] = jnp.where(lr_ref[1:2] > 0.5, cs, 1.0)
        sp_ref[...] = jnp.where(lr_ref[2:3] > 0.5, sn, 0.0)
        sm_ref[...] = jnp.where(lr_ref[3:4] > 0.5, -sn, 0.0)

    return pl.pallas_call(
        body, grid=(T // tm,),
        in_specs=[_tok(tm, 1), _res((8, LANES))],
        out_specs=[_tok(tm, LANES)] * 3,
        out_shape=[_sds((T, LANES), F32)] * 3,
        compiler_params=_cp("arbitrary"), name="rope_tables")(pos, lane_rows)


def _rope(t, c, sp, sm):
    w = t.shape[1]
    reps = w // LANES
    cf, spf, smf = jnp.tile(c, (1, reps)), jnp.tile(sp, (1, reps)), jnp.tile(sm, (1, reps))
    half = ROPE_DIM // 2
    return t * cf + pltpu.roll(t, half, axis=1) * spf + pltpu.roll(t, w - half, axis=1) * smf


def _rope_t(d, c, sp, sm):
    w = d.shape[1]
    reps = w // LANES
    cf, spf, smf = jnp.tile(c, (1, reps)), jnp.tile(sp, (1, reps)), jnp.tile(sm, (1, reps))
    half = ROPE_DIM // 2
    return d * cf + pltpu.roll(d * spf, w - half, axis=1) + pltpu.roll(d * smf, half, axis=1)


def _split_residues(v, d, stage):
    tm, width = v.shape
    if d == 1:
        return [v]
    nj = width // LANES
    for j in range(nj):
        stage[j] = v[:, j * LANES:(j + 1) * LANES]
    return [jnp.concatenate([stage[j, pl.ds(r, tm // d, stride=d), :] for j in range(nj)], axis=1) for r in range(d)]


def _merge_residues(piece, d, tm, width, stage):
    if d == 1:
        return piece(0)
    nj = width // LANES
    for r in range(d):
        p = piece(r)
        for j in range(nj):
            stage[j, pl.ds(r, tm // d, stride=d), :] = p[:, j * LANES:(j + 1) * LANES]
    return jnp.concatenate([stage[j] for j in range(nj)], axis=1)


def _residue_spec(d, tm):
    return pl.BlockSpec((d, tm // d, GROUP_WIDTH), lambda i: (0, i, 0))


def _stage_scratch(tm):
    return pltpu.VMEM((GROUP_WIDTH // LANES, tm, LANES), F32)


def proj_rope_fwd(x, vec, w, tabs, n_rope, transposed, dils, name):
    T, D = x.shape
    N = w.shape[0] if transposed else w.shape[1]
    tm = min(TOKEN_TILE, T)
    GW = GROUP_WIDTH
    piece_dils = [dils[j % len(dils)] for j in range(N // GW)]

    def body(x_ref, vec_ref, w_ref, c_ref, sp_ref, sm_ref, h_ref, *rest):
        out_refs, stage = rest[:-1], rest[-1]
        hb = _modnorm(x_ref[...], vec_ref[0:1], vec_ref[1:2], vec_ref[2:3]).astype(BF16)
        h_ref[...] = hb
        p = _dot_nt(hb, w_ref[...]) if transposed else _dot(hb, w_ref[...])
        pr = _rope(p[:, 0:n_rope], c_ref[...], sp_ref[...], sm_ref[...])
        for j, d in enumerate(piece_dils):
            src = pr if (j + 1) * GW <= n_rope else p
            for r, rows in enumerate(_split_residues(src[:, j * GW:(j + 1) * GW], d, stage)):
                out_refs[j][r] = rows.astype(BF16)

    return pl.pallas_call(
        body, grid=(T // tm,),
        in_specs=[_tok(tm, D), _res((8, D)), _res(w.shape)] + [_tok(tm, LANES)] * 3,
        out_specs=[_tok(tm, D)] + [_residue_spec(d, tm) for d in piece_dils],
        out_shape=[_sds((T, D), BF16)] + [_sds((d, T // d, GW), BF16) for d in piece_dils],
        scratch_shapes=[_stage_scratch(tm)],
        compiler_params=_cp("arbitrary"), name=name)(x, vec, w, *tabs)


def proj_rope_bwd(dparts, dils, x, dxo, vec, w, tabs, n_rope, transposed, name):
    T, D = x.shape
    N = w.shape[0] if transposed else w.shape[1]
    tm = min(TOKEN_TILE, T)
    GW = GROUP_WIDTH
    npart = len(dparts)
    piece_dils = [dils[j % len(dils)] for j in range(npart)]

    def body(*refs):
        d_refs = refs[:npart]
        x_ref, dxo_ref, vec_ref, w_ref, c_ref, sp_ref, sm_ref, dx_ref, dp_ref, part_ref, stage = refs[npart:]
        d = jnp.concatenate([_merge_residues(lambda r, ref=ref: ref[r].astype(F32), dd, tm, GW, stage)
                             for ref, dd in zip(d_refs, piece_dils)], axis=1)
        dr = _rope_t(d[:, 0:n_rope], c_ref[...], sp_ref[...], sm_ref[...])
        if n_rope < N:
            dr = jnp.concatenate([dr, d[:, n_rope:N]], axis=1)
        dpb = dr.astype(BF16)
        dp_ref[...] = dpb
        dh = _dot(dpb, w_ref[...]) if transposed else _dot_nt(dpb, w_ref[...])
        _, vjp = jax.vjp(_modnorm, x_ref[...], vec_ref[0:1], vec_ref[1:2], vec_ref[2:3])
        dx, dg, dsh, dsc = vjp(dh)
        dx_ref[...] = dxo_ref[...] + dx
        _acc_rows(part_ref, _rows8([dg, dsh, dsc], D), pl.program_id(0) == 0)

    return pl.pallas_call(
        body, grid=(T // tm,),
        in_specs=[_residue_spec(d, tm) for d in piece_dils] + [_tok(tm, D), _tok(tm, D), _res((8, D)), _res(w.shape)]
        + [_tok(tm, LANES)] * 3,
        out_specs=[_tok(tm, D), _tok(tm, N), pl.BlockSpec((8, D), lambda i: (0, 0))],
        out_shape=[_sds((T, D), F32), _sds((T, N), BF16), _sds((8, D), F32)],
        scratch_shapes=[_stage_scratch(tm)],
        compiler_params=_cp("arbitrary"), name=name)(*dparts, x, dxo, vec, w, *tabs)


def _valid_mask(n, i):
    qi = lax.broadcasted_iota(jnp.int32, (n, 2 * n), 0)
    kj = lax.broadcasted_iota(jnp.int32, (n, 2 * n), 1)
    dist = n + qi - kj
    return (dist >= 0) & (dist <= n) & ((kj >= n) | (i > 0))


def _band_specs(n):
    two = pl.BlockSpec((None, 2 * n, GROUP_WIDTH), lambda r, i: (r, i, 0))
    prv = pl.BlockSpec((None, n, GROUP_WIDTH), lambda r, i: (r, jnp.maximum(2 * i - 1, 0), 0))
    one = pl.BlockSpec((None, n, GROUP_WIDTH), lambda r, i: (r, i, 0))
    return two, prv, one


def _pair_keys(prev_ref, two_ref, ps, n):
    cur2 = two_ref[:, ps]
    return jnp.concatenate([prev_ref[:, ps], cur2[0:n]], axis=0), cur2


def attn_core_fwd(q, k, v, g, n):
    d, M, GW = q.shape
    scale = HEAD_DIM ** -0.5

    def body(q_ref, kp_ref, kc_ref, vp_ref, vc_ref, o_ref, l_ref):
        masks = (_valid_mask(n, pl.program_id(1)), _valid_mask(n, 1))
        first = lax.broadcasted_iota(jnp.int32, (1, LANES), 1) < HEAD_DIM
        for pair in range(HEADS_PER_GROUP * HEAD_DIM // LANES):
            ps = slice(LANES * pair, LANES * (pair + 1))
            keys, vals = _pair_keys(kp_ref, kc_ref, ps, n), _pair_keys(vp_ref, vc_ref, ps, n)
            for blk in range(2):
                rows = slice(blk * n, (blk + 1) * n)
                q2 = q_ref[rows, ps]
                o2, l2 = [], []
                for sel in (first, jnp.logical_not(first)):
                    s = jnp.where(masks[blk], _dot_nt(jnp.where(sel, q2, jnp.zeros_like(q2)), keys[blk]) * scale, -1e30)
                    m = jnp.max(s, axis=1, keepdims=True)
                    p = jnp.exp(s - m)
                    den = jnp.sum(p, axis=1, keepdims=True)
                    o2.append(_dot((p / den).astype(BF16), vals[blk]))
                    l2.append(m + jnp.log(den))
                o_ref[rows, ps] = jnp.where(first, o2[0], o2[1]).astype(BF16)
                l_ref[rows, ps] = jnp.where(first, l2[0], l2[1])

    two, prv, _ = _band_specs(n)
    return pl.pallas_call(
        body, grid=(d, M // (2 * n)),
        in_specs=[two, prv, two, prv, two], out_specs=[two, two],
        out_shape=[_sds((d, M, GW), BF16), _sds((d, M, GW), F32)],
        compiler_params=_cp("arbitrary", "arbitrary"), name=f"attn_fwd_g{g}")(q, k, k, v, v)


def attn_core_bwd(q, k, v, do, rr, lse, g, n):
    d, M, GW = q.shape
    scale = HEAD_DIM ** -0.5

    def body(q_ref, kp_ref, kc_ref, vp_ref, vc_ref, do_ref, r_ref, l_ref, dq_ref, dkc_ref, dkp_ref, dvc_ref, dvp_ref):
        masks = (_valid_mask(n, pl.program_id(1)), _valid_mask(n, 1))
        first = lax.broadcasted_iota(jnp.int32, (1, LANES), 1) < HEAD_DIM
        for pair in range(HEADS_PER_GROUP * HEAD_DIM // LANES):
            ps = slice(LANES * pair, LANES * (pair + 1))
            keys, vals = _pair_keys(kp_ref, kc_ref, ps, n), _pair_keys(vp_ref, vc_ref, ps, n)
            own = []
            for blk in range(2):
                rows = slice(blk * n, (blk + 1) * n)
                q2, do2, r2 = q_ref[rows, ps], do_ref[rows, ps], r_ref[rows, ps]
                dq2, dk, dv = [], None, None
                for half, sel in enumerate((first, jnp.logical_not(first))):
                    qm = jnp.where(sel, q2, jnp.zeros_like(q2))
                    dom = jnp.where(sel, do2, jnp.zeros_like(do2))
                    s = jnp.where(masks[blk], _dot_nt(qm, keys[blk]) * scale, -1e30)
                    lane0 = LANES * pair + HEAD_DIM * half
                    p = jnp.exp(s - l_ref[rows, lane0:lane0 + 1])
                    dp = _dot_nt(dom, vals[blk])
                    delta = jnp.sum(jnp.where(sel, r2, 0.0), axis=1, keepdims=True)
                    ds = (p * (dp - delta) * scale).astype(BF16)
                    dq2.append(_dot(ds, keys[blk]))
                    dkh = _dot_tn(ds, qm)
                    dvh = _dot_tn(p.astype(BF16), dom)
                    dk = dkh if dk is None else dk + dkh
                    dv = dvh if dv is None else dv + dvh
                dq_ref[rows, ps] = jnp.where(first, dq2[0], dq2[1]).astype(BF16)
                own.append((dk, dv))
            for t, (c_ref, p_ref) in enumerate(((dkc_ref, dkp_ref), (dvc_ref, dvp_ref))):
                a, b = own[0][t], own[1][t]
                p_ref[:, ps] = a[0:n].astype(BF16)
                c_ref[0:n, ps] = (a[n:2 * n] + b[0:n]).astype(BF16)
                c_ref[n:2 * n, ps] = b[n:2 * n].astype(BF16)

    two, prv, one = _band_specs(n)
    return pl.pallas_call(
        body, grid=(d, M // (2 * n)),
        in_specs=[two, prv, two, prv, two, two, two, two], out_specs=[two, two, one, two, one],
        out_shape=[_sds((d, M, GW), BF16), _sds((d, M, GW), BF16), _sds((d, M // 2, GW), BF16),
                   _sds((d, M, GW), BF16), _sds((d, M // 2, GW), BF16)],
        compiler_params=_cp("arbitrary", "arbitrary"), name=f"attn_bwd_g{g}")(q, k, k, v, v, do, rr, lse)


def dkv_combine(cur_prev, n, name):
    d, M, GW = cur_prev[0][0].shape
    rows = min(M, 1024)
    pairs = rows // (2 * n)
    steps = M // rows
    flat = [a for pair in cur_prev for a in pair]

    def body(*refs):
        o_ref = refs[-1]
        last = pl.program_id(1) == steps - 1
        acc = None
        shifted = None
        for t in range(0, len(refs) - 1, 3):
            c = refs[t][...].astype(F32)
            nxt = jnp.where(last, 0.0, refs[t + 2][...].astype(F32))
            s = nxt if pairs == 1 else jnp.concatenate([refs[t + 1][n:pairs * n, :].astype(F32), nxt], axis=0)
            acc = c if acc is None else acc + c
            shifted = s if shifted is None else shifted + s
        for m in range(pairs):
            lo = 2 * m * n
            o_ref[lo:lo + n, :] = acc[lo:lo + n].astype(BF16)
            o_ref[lo + n:lo + 2 * n, :] = (acc[lo + n:lo + 2 * n] + shifted[m * n:(m + 1) * n]).astype(BF16)

    cur = pl.BlockSpec((None, rows, GW), lambda r, i: (r, i, 0))
    same = pl.BlockSpec((None, pairs * n, GW), lambda r, i: (r, i, 0))
    nxt = pl.BlockSpec((None, n, GW), lambda r, i: (r, jnp.minimum((i + 1) * pairs, M // (2 * n) - 1), 0))
    args = []
    for c, p in cur_prev:
        args += [c, p, p]
    return pl.pallas_call(
        body, grid=(d, steps), in_specs=[cur, same, nxt] * len(cur_prev), out_specs=cur,
        out_shape=_sds((d, M, GW), BF16),
        compiler_params=_cp("arbitrary", "arbitrary"), name=name)(*args)


def _group_weights(ls):
    mx = functools.reduce(jnp.maximum, ls)
    es = [jnp.exp(l - mx) for l in ls]
    tot = functools.reduce(lambda a, b: a + b, es)
    return [e / tot for e in es]


def attn_mix_out(os_, ls, dils, x, vec, w_o):
    T, D = x.shape
    GW = GROUP_WIDTH
    tm = min(TOKEN_TILE, T)
    ng = len(os_)

    def body(*refs):
        o_refs, l_refs = refs[:ng], refs[ng:2 * ng]
        x_ref, vec_ref, w_ref, xn_ref, mix_ref, y_ref, stage = refs[2 * ng:]
        natural = lambda ref, d: _merge_residues(lambda r: ref[r].astype(F32), d, tm, GW, stage)
        ws = _group_weights([natural(r, d) for r, d in zip(l_refs, dils)])
        mixed = functools.reduce(lambda a, b: a + b, [w * natural(r, d) for w, r, d in zip(ws, o_refs, dils)])
        mb = mixed.astype(BF16)
        mix_ref[...] = mb
        y = _dot(mb, w_ref[...])
        y_ref[...] = y.astype(BF16)
        xn_ref[...] = x_ref[...] + (1.0 + vec_ref[3:4]) * y

    res = [_residue_spec(d, tm) for d in dils]
    return pl.pallas_call(
        body, grid=(T // tm,),
        in_specs=res + res + [_tok(tm, D), _res((8, D)), _res((GW, D))],
        out_specs=[_tok(tm, D), _tok(tm, GW), _tok(tm, D)],
        out_shape=[_sds((T, D), F32), _sds((T, GW), BF16), _sds((T, D), BF16)],
        scratch_shapes=[_stage_scratch(tm)],
        compiler_params=_cp("arbitrary"), name="attn_mix_out")(*os_, *ls, x, vec, w_o)


def attn_mix_bwd(dxo, y, vec, w_o, os_, ls, dils):
    T, D = dxo.shape
    GW = GROUP_WIDTH
    tm = min(TOKEN_TILE, T)
    ng = len(os_)

    def body(*refs):
        dxo_ref, y_ref, vec_ref, w_ref = refs[:4]
        o_refs, l_refs = refs[4:4 + ng], refs[4 + ng:4 + 2 * ng]
        dy_ref = refs[4 + 2 * ng]
        do_refs = refs[5 + 2 * ng:5 + 3 * ng]
        r_refs = refs[5 + 3 * ng:5 + 4 * ng]
        part_ref, stage = refs[5 + 4 * ng], refs[6 + 4 * ng]
        natural = lambda ref, d: _merge_residues(lambda r: ref[r].astype(F32), d, tm, GW, stage)
        dxo_t = dxo_ref[...]
        dyb = (dxo_t * (1.0 + vec_ref[3:4])).astype(BF16)
        dy_ref[...] = dyb
        dgate = jnp.sum(dxo_t * y_ref[...].astype(F32), axis=0, keepdims=True)
        _acc_rows(part_ref, _rows8([dgate], D), pl.program_id(0) == 0)
        dmix = _dot_nt(dyb, w_ref[...])
        ws = _group_weights([natural(r, d) for r, d in zip(l_refs, dils)])
        mixed = functools.reduce(lambda a, b: a + b, [w * natural(r, d) for w, r, d in zip(ws, o_refs, dils)])
        for gi in range(ng):
            do = ws[gi] * dmix
            for r, rows in enumerate(_split_residues(do, dils[gi], stage)):
                do_refs[gi][r] = rows.astype(BF16)
            for r, rows in enumerate(_split_residues(do * mixed, dils[gi], stage)):
                r_refs[gi][r] = rows

    res = [_residue_spec(d, tm) for d in dils]
    return pl.pallas_call(
        body, grid=(T // tm,),
        in_specs=[_tok(tm, D), _tok(tm, D), _res((8, D)), _res((GW, D))] + res + res,
        out_specs=[_tok(tm, D)] + res + res + [pl.BlockSpec((8, D), lambda i: (0, 0))],
        out_shape=[_sds((T, D), BF16)] + [_sds((d, T // d, GW), BF16) for d in dils]
        + [_sds((d, T // d, GW), F32) for d in dils] + [_sds((8, D), F32)],
        scratch_shapes=[_stage_scratch(tm)],
        compiler_params=_cp("arbitrary"), name="attn_mix_bwd")(dxo, y, vec, w_o, *os_, *ls)


def final_loss(x, gvec, target):
    T, D = x.shape
    tm = min(TOKEN_TILE, T)

    def norm(xv, g):
        return xv * lax.rsqrt(jnp.mean(xv * xv, axis=-1, keepdims=True) + NORM_EPS) * g

    def body(x_ref, g_ref, t_ref, dx_ref, part_ref, loss_ref):
        first = pl.program_id(0) == 0
        yv, vjp = jax.vjp(norm, x_ref[...], g_ref[0:1])
        err = yv - t_ref[...]
        dx, dg = vjp(err * (1.0 / D))
        dx_ref[...] = dx
        _acc_rows(part_ref, _rows8([dg], D), first)
        tile_loss = 0.5 * jnp.sum(jnp.sum(err * err, axis=1, keepdims=True) * (1.0 / D), axis=0, keepdims=True)
        _acc_rows(loss_ref, jnp.broadcast_to(tile_loss, (8, LANES)), first)

    return pl.pallas_call(
        body, grid=(T // tm,),
        in_specs=[_tok(tm, D), _res((8, D)), _tok(tm, D)],
        out_specs=[_tok(tm, D), pl.BlockSpec((8, D), lambda i: (0, 0)), pl.BlockSpec((8, LANES), lambda i: (0, 0))],
        out_shape=[_sds((T, D), F32), _sds((8, D), F32), _sds((8, LANES), F32)],
        compiler_params=_cp("arbitrary"), name="final_loss")(x, gvec, target)


def mods_project(c_all, w, b):
    B, D = c_all.shape
    L, _, N = w.shape

    def body(c_ref, w_ref, b_ref, o_ref):
        cv = c_ref[...]
        cond = cv * _sigmoid(cv)
        o_ref[0] = jnp.dot(cond, w_ref[0], preferred_element_type=F32, precision=lax.Precision.HIGHEST) + b_ref[0]

    return pl.pallas_call(
        body, grid=(L,),
        in_specs=[pl.BlockSpec((B, D), lambda l: (0, 0)), pl.BlockSpec((1, D, N), lambda l: (l, 0, 0)),
                  pl.BlockSpec((1, 1, N), lambda l: (l, 0, 0))],
        out_specs=pl.BlockSpec((1, B, N), lambda l: (l, 0, 0)),
        out_shape=_sds((L, B, N), F32),
        compiler_params=_cp("arbitrary"), name="mods_project")(c_all, w, b)


def mods_weight_grad(c_all, dm):
    B, D = c_all.shape
    L, _, N = dm.shape

    def body(c_ref, d_ref, o_ref):
        cv = c_ref[...]
        cond = cv * _sigmoid(cv)
        o_ref[0] = lax.dot_general(cond, d_ref[0], (((0,), (0,)), ((), ())), preferred_element_type=F32,
                                   precision=lax.Precision.HIGHEST)

    return pl.pallas_call(
        body, grid=(L,),
        in_specs=[pl.BlockSpec((B, D), lambda l: (0, 0)), pl.BlockSpec((1, B, N), lambda l: (l, 0, 0))],
        out_specs=pl.BlockSpec((1, D, N), lambda l: (l, 0, 0)),
        out_shape=_sds((L, D, N), F32),
        compiler_params=_cp("arbitrary"), name="mods_weight_grad")(c_all, dm)


def _adam_math(g, w, m, v):
    m2 = ADAM_B1 * m + (1.0 - ADAM_B1) * g
    v2 = ADAM_B2 * v + (1.0 - ADAM_B2) * (g * g)
    m_hat = m2 / (1.0 - ADAM_B1 ** ADAM_STEP)
    v_hat = v2 / (1.0 - ADAM_B2 ** ADAM_STEP)
    delta = -ADAM_LR * (m_hat / (jnp.sqrt(v_hat) + ADAM_EPS) + ADAM_WD * w)
    return delta, m2, v2


def adam_update(g, w, m, v, parts, name):
    R, C = w.shape
    tr = _pick(R, 256, 8)

    def body(g_ref, w_ref, m_ref, v_ref, go_ref, d_ref, mo_ref, vo_ref):
        if parts:
            gv = g_ref[0].astype(F32)
            for s in range(1, N_DEV):
                gv = gv + g_ref[s].astype(F32)
        else:
            gv = g_ref[...]
        go_ref[...] = gv
        d_ref[...], mo_ref[...], vo_ref[...] = _adam_math(gv, w_ref[...], m_ref[...], v_ref[...])

    gspec = pl.BlockSpec((N_DEV, tr, C), lambda i: (0, i, 0)) if parts else _tok(tr, C)
    return pl.pallas_call(
        body, grid=(R // tr,),
        in_specs=[gspec, _tok(tr, C), _tok(tr, C), _tok(tr, C)],
        out_specs=[_tok(tr, C)] * 4, out_shape=[_sds((R, C), F32)] * 4,
        compiler_params=_cp("arbitrary"), name=name)(g, w, m, v)


def adam_layer(parts, w, m, v, prev, layer, after, name):
    L, R, C = w.shape
    tr = _pick(R, 256, 8)
    prev = (list(prev) if prev is not None else []) + [after]

    def body(p_ref, w_ref, m_ref, v_ref, *rest):
        go_ref, d_ref, mo_ref, vo_ref = rest[-4:]
        gv = p_ref[0].astype(F32)
        for s in range(1, N_DEV):
            gv = gv + p_ref[s].astype(F32)
        go_ref[...] = gv
        d_ref[...], mo_ref[...], vo_ref[...] = _adam_math(gv, w_ref[...], m_ref[...], v_ref[...])

    lay = pl.BlockSpec((None, tr, C), lambda i: (layer, i, 0))
    return pl.pallas_call(
        body, grid=(R // tr,),
        in_specs=[pl.BlockSpec((N_DEV, tr, C), lambda i: (0, i, 0)), lay, lay, lay] + [pl.BlockSpec(memory_space=pl.ANY)] * len(prev),
        out_specs=[lay] * 4, out_shape=[_sds((L, R, C), F32)] * 4,
        input_output_aliases={4 + k: k for k in range(len(prev) - 1)},
        compiler_params=_cp("arbitrary"), name=name)(parts, w, m, v, *prev)


def _my_id():
    return 4 * lax.axis_index("x") + 2 * lax.axis_index("y") + lax.axis_index("c")


def _peer(s):
    x, y, c = lax.axis_index("x"), lax.axis_index("y"), lax.axis_index("c")
    px = (1 - x) if s & 4 else x
    py = (1 - y) if s & 2 else y
    pc = (1 - c) if s & 1 else c
    return (px, py, pc), 4 * px + 2 * py + pc


def all_gather(xs, space, name):
    na = len(xs)

    def body(*refs):
        x_refs, o_refs = refs[:na], refs[na:2 * na]
        send_sems, recv_sems, local_sems = refs[2 * na:]
        me = _my_id()
        locals_, sends = [], []
        for a in range(na):
            cp = pltpu.make_async_copy(x_refs[a], o_refs[a].at[me], local_sems.at[a])
            cp.start()
            locals_.append(cp)
        for s in range(1, N_DEV):
            peer, _ = _peer(s)
            for a in range(na):
                cp = pltpu.make_async_remote_copy(
                    src_ref=x_refs[a], dst_ref=o_refs[a].at[me], send_sem=send_sems.at[a, s - 1],
                    recv_sem=recv_sems.at[a, s - 1], device_id=peer, device_id_type=MESH)
                cp.start()
                sends.append(cp)
        for s in range(1, N_DEV):
            peer, pid = _peer(s)
            for a in range(na):
                pltpu.make_async_remote_copy(
                    src_ref=x_refs[a], dst_ref=o_refs[a].at[pid], send_sem=send_sems.at[a, s - 1],
                    recv_sem=recv_sems.at[a, s - 1], device_id=peer, device_id_type=MESH).wait_recv()
        for cp in sends:
            cp.wait_send()
        for cp in locals_:
            cp.wait()

    spec = pl.BlockSpec(memory_space=space)
    return pl.pallas_call(
        body, in_specs=[spec] * na, out_specs=[spec] * na,
        out_shape=[_sds((N_DEV,) + x.shape, x.dtype) for x in xs],
        scratch_shapes=[pltpu.SemaphoreType.DMA((na, N_DEV - 1)), pltpu.SemaphoreType.DMA((na, N_DEV - 1)),
                        pltpu.SemaphoreType.DMA((na,))],
        compiler_params=pltpu.CompilerParams(vmem_limit_bytes=VMEM_LIMIT), name=name)(*xs)


def exchange_slots(xs, name):
    na = len(xs)

    def body(*refs):
        x_refs, o_refs = refs[:na], refs[na:2 * na]
        send_sems, recv_sems, local_sems = refs[2 * na:]
        me = _my_id()
        locals_, sends = [], []
        for a in range(na):
            cp = pltpu.make_async_copy(x_refs[a].at[me], o_refs[a].at[me], local_sems.at[a])
            cp.start()
            locals_.append(cp)
        for s in range(1, N_DEV):
            peer, pid = _peer(s)
            for a in range(na):
                cp = pltpu.make_async_remote_copy(
                    src_ref=x_refs[a].at[pid], dst_ref=o_refs[a].at[me], send_sem=send_sems.at[a, s - 1],
                    recv_sem=recv_sems.at[a, s - 1], device_id=peer, device_id_type=MESH)
                cp.start()
                sends.append(cp)
        for s in range(1, N_DEV):
            peer, pid = _peer(s)
            for a in range(na):
                pltpu.make_async_remote_copy(
                    src_ref=x_refs[a].at[pid], dst_ref=o_refs[a].at[pid], send_sem=send_sems.at[a, s - 1],
                    recv_sem=recv_sems.at[a, s - 1], device_id=peer, device_id_type=MESH).wait_recv()
        for cp in sends:
            cp.wait_send()
        for cp in locals_:
            cp.wait()

    spec = pl.BlockSpec(memory_space=pl.ANY)
    return pl.pallas_call(
        body, in_specs=[spec] * na, out_specs=[spec] * na,
        out_shape=[_sds(x.shape, x.dtype) for x in xs],
        scratch_shapes=[pltpu.SemaphoreType.DMA((na, N_DEV - 1)), pltpu.SemaphoreType.DMA((na, N_DEV - 1)),
                        pltpu.SemaphoreType.DMA((na,))],
        compiler_params=pltpu.CompilerParams(vmem_limit_bytes=VMEM_LIMIT), name=name)(*xs)


_HBM = pl.BlockSpec(memory_space=pltpu.HBM)
_SEM = pl.BlockSpec(memory_space=pltpu.SEMAPHORE)
_EFFECT = pltpu.SideEffectType.DATAFLOW_SIDE_EFFECTING


def _split_copies(pattern, x_ref, land_ref, send_sem, recv_sem):
    me = _my_id()
    if pattern in ("gather", "scatter"):
        plan = []
        for s in range(1, N_DEV):
            peer, pid = _peer(s)
            plan.append((x_ref.at[pid] if pattern == "scatter" else x_ref, land_ref.at[me], peer))
    elif pattern == "to_chips":
        plan = [(x_ref, land_ref.at[me], _peer(s)[0]) for s in (1, 2, 4, 6)]
    else:
        sibling = _peer(1)[0]
        plan = [(land_ref.at[_peer(s)[1]], land_ref.at[_peer(s)[1]], sibling) for s in (2, 4, 6)]
    return [pltpu.make_async_remote_copy(src_ref=src, dst_ref=dst, send_sem=send_sem, recv_sem=recv_sem,
                                         device_id=dev, device_id_type=MESH) for src, dst, dev in plan]


def comm_start(xs, pattern, after, name, lands=None):
    na = len(xs)
    extra = [] if after is None else [after]
    me = _my_id()
    if lands is None:
        lands = []
        for x in xs:
            shape = x.shape if pattern == "scatter" else (N_DEV,) + x.shape
            own = lax.dynamic_slice_in_dim(x, me, 1, 0) if pattern == "scatter" else x[None]
            lands.append(lax.dynamic_update_slice(lax.empty(shape, x.dtype), own, (me,) + (0,) * (len(shape) - 1)))

    def body(*refs):
        x_refs, land_refs = refs[:na], refs[na:2 * na]
        send_sem, recv_sem = refs[2 * na + len(extra)], refs[2 * na + len(extra) + 1]
        token = refs[-1]
        for a in range(na):
            for cp in _split_copies(pattern, x_refs[a], land_refs[a], send_sem, recv_sem):
                cp.start()
        token[...] = jnp.zeros_like(token)

    outs = pl.pallas_call(
        body, name=name,
        out_shape=(pltpu.SemaphoreType.DMA(()), pltpu.SemaphoreType.DMA(()))
        + tuple(pltpu.HBM(x.shape, x.dtype) for x in xs) + tuple(pltpu.HBM(l.shape, l.dtype) for l in lands)
        + (_sds((8, LANES), F32),),
        in_specs=(_HBM,) * (2 * na) + (pl.BlockSpec(memory_space=pl.ANY),) * len(extra),
        out_specs=(_SEM, _SEM) + (_HBM,) * (2 * na) + (pl.BlockSpec(memory_space=pltpu.VMEM),),
        input_output_aliases={a: 2 + a for a in range(2 * na)},
        compiler_params=pltpu.CompilerParams(has_side_effects=_EFFECT),
    )(*[pltpu.with_memory_space_constraint(x, pltpu.HBM) for x in xs],
      *[pltpu.with_memory_space_constraint(l, pltpu.HBM) for l in lands], *extra)
    return dict(sems=outs[0:2], xs=outs[2:2 + na], lands=outs[2 + na:2 + 2 * na], token=outs[-1], pattern=pattern)


def comm_wait(started, after, name, with_xs=False):
    xs, lands = started["xs"], started["lands"]
    pattern = started["pattern"]
    na = len(xs)

    def body(*refs):
        x_refs, land_refs = refs[:na], refs[na:2 * na]
        send_sem, recv_sem = refs[2 * na], refs[2 * na + 1]
        for a in range(na):
            for cp in _split_copies(pattern, x_refs[a], land_refs[a], send_sem, recv_sem):
                cp.wait_send()
                cp.wait_recv()

    outs = pl.pallas_call(
        body, name=name,
        out_shape=tuple(pltpu.HBM(x.shape, x.dtype) for x in xs) + tuple(pltpu.HBM(l.shape, l.dtype) for l in lands),
        in_specs=(_HBM,) * (2 * na) + (_SEM, _SEM, pl.BlockSpec(memory_space=pl.ANY)),
        out_specs=(_HBM,) * (2 * na),
        input_output_aliases={a: a for a in range(2 * na)},
        compiler_params=pltpu.CompilerParams(has_side_effects=_EFFECT),
    )(*xs, *lands, *started["sems"], after)
    return (list(outs[na:]), list(outs[:na])) if with_xs else list(outs[na:])


def _cols_to_natural(g):
    return jnp.concatenate([g[k] for k in range(N_DEV)], axis=1)


def _cols_to_slots(w):
    ns = w.shape[1] // N_DEV
    return jnp.stack([w[:, k * ns:(k + 1) * ns] for k in range(N_DEV)])


def _vec8(rows, d):
    rows = [r.reshape(1, d).astype(F32) for r in rows]
    return jnp.concatenate(rows + [jnp.zeros((8 - len(rows), d), F32)], axis=0)


def _ffn_forward(x, vec, w_in_t, w_out):
    xn, h, a, b, u, y = ffn_fwd(x, vec, w_in_t, w_out)
    return xn, (x, h, a, b, u, y)


def _ffn_backward(dxo, saved, vec, w_in_t, w_out, on_rows=None):
    x, h, a, b, u, y = saved
    dy, dab, dx, part = ffn_bwd(dxo, y, vec, w_out, w_in_t, a, b, x)
    rows = part[0:4]
    token = on_rows(rows) if on_rows is not None else None
    g_out = grad_slots(u, dy, "ffn_dw_out", after=token)
    g_in_t = grad_slots(dab, h, "ffn_dw_in", after=token)
    return dx, g_in_t, g_out, rows


_TRANSPOSED = ("ffn1_w_in", "ffn2_w_in", "attn_w_q")
_COL_NATURAL = ("conv_w_in", "w_kv", "attn_w_o")
_ROW_SHARDED = ("ffn1_w_out", "ffn2_w_out", "conv_w_out")
_BIG = _TRANSPOSED + _COL_NATURAL + _ROW_SHARDED


def weight_chunks():
    chunks = []
    for layer in range(DEPTH):
        first = [("ffn1_w_in", layer), ("ffn1_w_out", layer)]
        if layer == N_A_LAYERS:
            first = [("w_kv", layer)] + first
        mixer = [("conv_w_in", layer), ("conv_w_out", layer)] if layer < N_A_LAYERS else [("attn_w_q", layer), ("attn_w_o", layer)]
        rest = mixer + [("ffn2_w_in", layer), ("ffn2_w_out", layer)]
        chunks += [first, rest] if layer == 0 else [first + rest]
    return chunks


def stacked_index(name, layer):
    if name == "w_kv":
        return None
    return layer - N_A_LAYERS if name.startswith("attn") else layer


class ChunkComm:
    def __init__(self, shards):
        self.shards = shards
        self.chunks = weight_chunks()

    def _shard(self, name, layer):
        idx = stacked_index(name, layer)
        return self.shards[name][0 if idx is None else idx]

    def start_gather(self, ci, after):
        xs = [self._shard(n, l).astype(BF16) for n, l in self.chunks[ci]]
        return comm_start(xs, "to_chips", after, f"gather_start_{ci}")

    def relay_gather(self, ci, started, after):
        lands, xs = comm_wait(started, after, f"gather_wait_{ci}", with_xs=True)
        return comm_start(xs, "relay", None, f"gather_relay_{ci}", lands=lands)

    def finish_gather(self, ci, relayed, after):
        lands = comm_wait(relayed, after, f"gather_done_{ci}")
        W = {}
        for key, g in zip(self.chunks[ci], lands):
            W[key] = _cols_to_natural(g) if key[0] in _COL_NATURAL else g.reshape(-1, g.shape[2])
        return W

    def start_exchange(self, ci, slots, after):
        return comm_start([slots[key] for key in self.chunks[ci]], "scatter", after, f"exchange_start_{ci}")

    def finish_exchange(self, ci, started, after):
        lands = comm_wait(started, after, f"exchange_wait_{ci}")
        return dict(zip(self.chunks[ci], lands))


def device_step(x, positions, target, mods, kvmods, small, comm, gather0):
    T, D = x.shape
    groups = DILATED_GROUPS
    dils = [dil for _, dil in groups]
    lane = jnp.arange(LANES) % HEAD_DIM
    inv = ROPE_THETA ** (-jnp.arange(0, ROPE_DIM, 2, dtype=F32) / ROPE_DIM)
    lane_rows = _vec8([jnp.where(lane < ROPE_DIM, inv[lane % (ROPE_DIM // 2)], 0.0), lane < ROPE_DIM,
                       (lane >= ROPE_DIM // 2) & (lane < ROPE_DIM), lane < ROPE_DIM // 2], LANES)
    tabs = rope_tables(positions.reshape(T, 1), lane_rows)

    def after_token(v, token):
        return v if token is None else v + token[0, 0]

    def vec_of(layer, sub, token=None):
        return after_token(_vec8([small["norm_g"][layer, sub], mods[layer, 3 * sub], mods[layer, 3 * sub + 1],
                                  mods[layer, 3 * sub + 2]], D), token)

    saved = []
    kv_saved = None
    k_sh = v_sh = None
    qw = GROUP_WIDTH * len(groups)
    chunk_of = {key: ci for ci, chunk in enumerate(comm.chunks) for key in chunk}
    W = {}
    flight = {"ci": 0, "started": gather0, "token": None}

    def need(key, after):
        if key not in W:
            ci = chunk_of[key]
            assert ci == flight["ci"], (key, ci)
            relayed = comm.relay_gather(ci, flight["started"], after)
            token = relayed["token"]
            if ci + 1 < len(comm.chunks):
                flight.update(ci=ci + 1, started=comm.start_gather(ci + 1, token))
                token = flight["started"]["token"]
            W.update(comm.finish_gather(ci, relayed, token))
        return W[key]

    def behind_start(v):
        token, flight["token"] = flight["token"], None
        return after_token(v, token)

    for layer in range(DEPTH):
        if layer == N_A_LAYERS:
            w_kv = need(("w_kv", layer), x)
            kv_vec = behind_start(_vec8([small["kv_norm_g"], kvmods[0], kvmods[1]], D))
            h_kv, *kv_pieces = proj_rope_fwd(x, kv_vec, w_kv, tabs, qw, False, dils, "kv_fwd")
            k_sh, v_sh = kv_pieces[:len(groups)], kv_pieces[len(groups):]
            kv_saved = (x, h_kv, kv_vec)
        rec = {}
        w_in, w_out = need(("ffn1_w_in", layer), x), need(("ffn1_w_out", layer), x)
        v1 = behind_start(vec_of(layer, 0))
        x, rec["ffn1"] = _ffn_forward(x, v1, w_in, w_out)
        if layer < N_A_LAYERS:
            w_in, w_out = need(("conv_w_in", layer), x), need(("conv_w_out", layer), x)
            v2 = behind_start(vec_of(layer, 1))
            cw = _vec8(list(small["conv_w"][layer]), D)
            x_in = x
            x, h, bcu, cv, z, y = conv_fwd(x, v2, cw, w_in, w_out)
            rec["mix"] = (x_in, h, bcu, cv, z, y, cw)
        else:
            w_q, w_o = need(("attn_w_q", layer), x), need(("attn_w_o", layer), x)
            v2 = behind_start(vec_of(layer, 1))
            x_in = x
            h, *q = proj_rope_fwd(x, v2, w_q, tabs, qw, True, dils, "q_fwd")
            os_, ls = [], []
            for g, (win, dil) in enumerate(groups):
                o, l = attn_core_fwd(q[g], k_sh[g], v_sh[g], g, win // dil)
                os_.append(o)
                ls.append(l)
            x, mixed, y = attn_mix_out(os_, ls, dils, x, v2, w_o)
            rec["mix"] = (x_in, h, q, os_, ls, mixed, y)
        w_in, w_out = need(("ffn2_w_in", layer), x), need(("ffn2_w_out", layer), x)
        v3 = behind_start(vec_of(layer, 2))
        x, rec["ffn2"] = _ffn_forward(x, v3, w_in, w_out)
        rec["vecs"] = (v1, v2, v3)
        saved.append(rec)

    dx, part_final, loss_tile = final_loss(x, _vec8([small["final_norm_g"]], D), target)
    loss = loss_tile[0, 0]

    conv_rows = [None] * N_A_LAYERS
    kv_rows = None
    mod_rows = [[None] * 3 for _ in range(DEPTH)]
    dkv_pairs = [{"k": [], "v": []} for _ in groups]
    slots = {}
    exchanges = []
    token = None

    def send_ready_chunks():
        nonlocal token
        for ci in reversed(range(len(comm.chunks))):
            if ci not in [e[0] for e in exchanges] and all(key in slots for key in comm.chunks[ci]):
                started = comm.start_exchange(ci, slots, token)
                exchanges.append((ci, started))
                token = started["token"]

    vector_gather = {}

    def start_vector_gather(rows0):
        mod_rows[0][0] = rows0
        rows = jnp.stack([jnp.stack(r) for r in mod_rows])
        vecs = jnp.concatenate([rows[:, :, 1:4].reshape(-1), kv_rows[1:3].reshape(-1), kv_rows[0], part_final[0],
                                rows[:, :, 0].reshape(-1), jnp.stack(conv_rows).reshape(-1)])
        vector_gather["count"] = vecs.shape[0]
        vecs = _pad_rows(vecs.reshape(-1, 1), 8 * LANES).reshape(-1, LANES)
        vector_gather["started"] = comm_start([vecs], "gather", None, "vector_grads_start")
        return vector_gather["started"]["token"]

    for layer in reversed(range(DEPTH)):
        rec = saved[layer]
        v1, v2, v3 = rec["vecs"]
        dx, slots[("ffn2_w_in", layer)], slots[("ffn2_w_out", layer)], mod_rows[layer][2] = _ffn_backward(
            dx, rec["ffn2"], after_token(v3, token), W[("ffn2_w_in", layer)], W[("ffn2_w_out", layer)])
        if layer < N_A_LAYERS:
            x_in, h, bcu, cv, z, y, cw = rec["mix"]
            dx, dy, dbcu, part, dcw = conv_bwd(dx, x_in, y, bcu, cv, v2, cw, W[("conv_w_in", layer)], W[("conv_w_out", layer)])
            slots[("conv_w_out", layer)] = grad_slots(z, dy, "conv_dw_out")
            slots[("conv_w_in", layer)] = grad_slots(h, dbcu, "conv_dw_in", col_slots=True)
            conv_rows[layer] = dcw[0:3]
            mod_rows[layer][1] = part[0:4]
        else:
            x_in, h, q, os_, ls, mixed, y = rec["mix"]
            outs = attn_mix_bwd(dx, y, v2, W[("attn_w_o", layer)], os_, ls, dils)
            ng = len(groups)
            dy, dos, rrs, part_gate = outs[0], outs[1:1 + ng], outs[1 + ng:1 + 2 * ng], outs[1 + 2 * ng]
            slots[("attn_w_o", layer)] = grad_slots(mixed, dy, "attn_dw_o", col_slots=True)
            dqs = []
            for g, (win, dil) in enumerate(groups):
                dq, dkc, dkp, dvc, dvp = attn_core_bwd(q[g], k_sh[g], v_sh[g], dos[g], rrs[g], ls[g], g, win // dil)
                dqs.append(dq)
                dkv_pairs[g]["k"].append((dkc, dkp))
                dkv_pairs[g]["v"].append((dvc, dvp))
            dx, dqr, part_norm = proj_rope_bwd(dqs, dils, x_in, dx, v2, W[("attn_w_q", layer)], tabs, qw, True, "q_bwd")
            slots[("attn_w_q", layer)] = grad_slots(dqr, h, "attn_dw_q")
            mod_rows[layer][1] = jnp.concatenate([part_norm[0:3], part_gate[0:1]], axis=0)
        send_ready_chunks()
        dx, slots[("ffn1_w_in", layer)], slots[("ffn1_w_out", layer)], mod_rows[layer][0] = _ffn_backward(
            dx, rec["ffn1"], after_token(v1, token), W[("ffn1_w_in", layer)], W[("ffn1_w_out", layer)],
            on_rows=start_vector_gather if layer == 0 else None)
        if layer == N_A_LAYERS:
            x_kv, h_kv, kv_vec = kv_saved
            dparts = [dkv_combine(dkv_pairs[g]["k"], win // dil, f"dk_combine_g{g}") for g, (win, dil) in enumerate(groups)]
            dparts += [dkv_combine(dkv_pairs[g]["v"], win // dil, f"dv_combine_g{g}") for g, (win, dil) in enumerate(groups)]
            dx, dkvp, part_kv = proj_rope_bwd(dparts, dils, x_kv, dx, kv_vec, W[("w_kv", layer)], tabs, qw, False, "kv_bwd")
            slots[("w_kv", layer)] = grad_slots(h_kv, dkvp, "kv_dw", col_slots=True)
            kv_rows = part_kv[0:3]
        send_ready_chunks()

    return loss, dx, {"exchanges": exchanges, "vector_gather": vector_gather}


def _flat2(a):
    return a.reshape(-1, a.shape[-1])


def _pad_rows(a, mult):
    r = a.shape[0]
    pad = (-r) % mult
    return a if pad == 0 else jnp.concatenate([a, jnp.zeros((pad,) + a.shape[1:], a.dtype)], axis=0)


def kernel(x, c, positions, norm_g, ada_w, ada_b, ffn1_w_in, ffn1_w_out, ffn2_w_in, ffn2_w_out, conv_w_in, conv_w, conv_w_out, kv_norm_g, kv_ada_w, kv_ada_b, w_kv, attn_w_q, attn_w_o, final_norm_g, loss_target, m_norm_g, m_ada_w, m_ada_b, m_ffn1_w_in, m_ffn1_w_out, m_ffn2_w_in, m_ffn2_w_out, m_conv_w_in, m_conv_w, m_conv_w_out, m_kv_norm_g, m_kv_ada_w, m_kv_ada_b, m_w_kv, m_attn_w_q, m_attn_w_o, m_final_norm_g, v_norm_g, v_ada_w, v_ada_b, v_ffn1_w_in, v_ffn1_w_out, v_ffn2_w_in, v_ffn2_w_out, v_conv_w_in, v_conv_w, v_conv_w_out, v_kv_norm_g, v_kv_ada_w, v_kv_ada_b, v_w_kv, v_attn_w_q, v_attn_w_o, v_final_norm_g):
    names = ("norm_g", "ada_w", "ada_b", "ffn1_w_in", "ffn1_w_out", "ffn2_w_in", "ffn2_w_out", "conv_w_in", "conv_w",
             "conv_w_out", "kv_norm_g", "kv_ada_w", "kv_ada_b", "w_kv", "attn_w_q", "attn_w_o", "final_norm_g")
    wts = dict(zip(names, (norm_g, ada_w, ada_b, ffn1_w_in, ffn1_w_out, ffn2_w_in, ffn2_w_out, conv_w_in, conv_w, conv_w_out,
                           kv_norm_g, kv_ada_w, kv_ada_b, w_kv, attn_w_q, attn_w_o, final_norm_g)))
    mom = dict(zip(names, (m_norm_g, m_ada_w, m_ada_b, m_ffn1_w_in, m_ffn1_w_out, m_ffn2_w_in, m_ffn2_w_out, m_conv_w_in,
                           m_conv_w, m_conv_w_out, m_kv_norm_g, m_kv_ada_w, m_kv_ada_b, m_w_kv, m_attn_w_q, m_attn_w_o,
                           m_final_norm_g)))
    var = dict(zip(names, (v_norm_g, v_ada_w, v_ada_b, v_ffn1_w_in, v_ffn1_w_out, v_ffn2_w_in, v_ffn2_w_out, v_conv_w_in,
                           v_conv_w, v_conv_w_out, v_kv_norm_g, v_kv_ada_w, v_kv_ada_b, v_w_kv, v_attn_w_q, v_attn_w_o,
                           v_final_norm_g)))
    T, D = x.shape[1], x.shape[2]
    me = _my_id()
    nmod = ada_w.shape[2]
    nkv = kv_ada_w.shape[1]

    def stacked(w, n):
        w = w if w.ndim == 3 else w[None]
        return jnp.swapaxes(w, 1, 2) if n in _TRANSPOSED else w

    comm = ChunkComm({n: stacked(wts[n], n) for n in _BIG})
    W = {}

    ds = norm_g.shape[2]
    small = jnp.concatenate([c.reshape(-1), norm_g.reshape(-1), conv_w.reshape(-1)]).astype(F32)
    n_small = small.shape[0]
    small = _pad_rows(small.reshape(-1, 1), 8 * LANES).reshape(-1, LANES)
    (small_all,) = all_gather([small], pltpu.VMEM, "gather_small")
    small_all = small_all.reshape(N_DEV, -1)[:, :n_small]
    c_all = small_all[:, :D]
    def full_rows(off, count):
        return jnp.stack([small_all[:, off + i * ds:off + (i + 1) * ds].reshape(D) for i in range(count)])

    W["norm_g"] = full_rows(D, DEPTH * 3).reshape(DEPTH, 3, D)
    W["conv_w"] = full_rows(D + DEPTH * 3 * ds, N_A_LAYERS * 3).reshape(N_A_LAYERS, 3, D)
    W["kv_norm_g"], W["final_norm_g"] = kv_norm_g, final_norm_g

    ada_b_mine = lax.dynamic_slice_in_dim(ada_b, me * nmod, nmod, axis=1).reshape(DEPTH, 1, nmod)
    kv_b_mine = lax.dynamic_slice_in_dim(kv_ada_b, me * nkv, nkv, axis=0).reshape(1, 1, nkv)
    mods_cols = mods_project(c_all, ada_w, ada_b_mine)
    kv_cols = mods_project(c_all, kv_ada_w.reshape(1, D, nkv), kv_b_mine)
    mcat = jnp.concatenate([mods_cols[l] for l in range(DEPTH)] + [kv_cols[0]], axis=1)
    wm = mcat.shape[1]
    if wm % LANES:
        mcat = jnp.concatenate([mcat, jnp.zeros((N_DEV, LANES - wm % LANES), F32)], axis=1)
    (mods_all,) = exchange_slots([mcat.reshape(N_DEV, 1, -1)], "exchange_mods")
    gather0 = comm.start_gather(0, mods_all)
    mods_all = mods_all.reshape(N_DEV, -1)
    mods = jnp.stack([mods_all[:, l * nmod:(l + 1) * nmod].reshape(N_MOD, D) for l in range(DEPTH)])
    kvmods = mods_all[:, DEPTH * nmod:DEPTH * nmod + nkv].reshape(2, D)

    loss_local, dx, grads = device_step(x[0], positions[0], loss_target[0], mods, kvmods, W, comm, gather0)
    loss = lax.psum(loss_local, MESH_AXES)

    (vec_all,) = comm_wait(grads["vector_gather"]["started"], grads["exchanges"][-1][1]["token"], "vector_grads_wait")
    vec_all = vec_all.reshape(N_DEV, -1)[:, :grads["vector_gather"]["count"]]
    nm_, nk_ = DEPTH * N_MOD * D, 2 * D
    dmods_all = vec_all[:, :nm_].reshape(N_DEV, DEPTH, N_MOD * D)
    dkvm_all = vec_all[:, nm_:nm_ + nk_]
    rest = vec_all[:, nm_ + nk_:]
    parts_kv_norm, parts_final = rest[:, :D].reshape(N_DEV, 1, D), rest[:, D:2 * D].reshape(N_DEV, 1, D)
    parts_norm = lax.dynamic_slice_in_dim(rest[:, 2 * D:2 * D + DEPTH * 3 * D].reshape(N_DEV, DEPTH * 3, D), me * ds, ds, axis=2)
    parts_conv = lax.dynamic_slice_in_dim(rest[:, 2 * D + DEPTH * 3 * D:].reshape(N_DEV, N_A_LAYERS * 3, D), me * ds, ds, axis=2)
    dm_cols = lax.dynamic_slice_in_dim(dmods_all, me * nmod, nmod, axis=2)
    dm_mine = jnp.stack([dm_cols[:, l] for l in range(DEPTH)])
    dkv_mine = lax.dynamic_slice_in_dim(dkvm_all, me * nkv, nkv, axis=1).reshape(1, N_DEV, nkv)
    g_ada_w = mods_weight_grad(c_all, dm_mine)
    g_kv_ada_w = mods_weight_grad(c_all, dkv_mine)[0]

    out_g, out_d, out_m, out_v = {}, {}, {}, {}

    def update(n, g, w, parts=False):
        shp = w.shape
        w2 = w.reshape(1, -1) if w.ndim == 1 else _flat2(w)
        g2 = g if parts else g.reshape(w2.shape)
        res = adam_update(g2, w2, mom[n].reshape(w2.shape), var[n].reshape(w2.shape), parts, "adam_" + n)
        out_g[n], out_d[n], out_m[n], out_v[n] = (r.reshape(shp) for r in res)

    moms = {n: stacked(mom[n], n) for n in _BIG}
    vars_ = {n: stacked(var[n], n) for n in _BIG}
    results = {}
    after = dx
    for ci, started in grads["exchanges"]:
        for (n, layer), parts in comm.finish_exchange(ci, started, after).items():
            idx = stacked_index(n, layer)
            results[n] = adam_layer(parts, comm.shards[n], moms[n], vars_[n], results.get(n), 0 if idx is None else idx,
                                    after, f"adam_{n}_{layer}")
            after = results[n][1]
    for n in _BIG:
        res = [jnp.swapaxes(r, 1, 2) if n in _TRANSPOSED else r for r in results[n]]
        out_g[n], out_d[n], out_m[n], out_v[n] = (r.reshape(wts[n].shape) for r in res)
    update("ada_w", g_ada_w, ada_w)
    update("kv_ada_w", g_kv_ada_w, kv_ada_w)
    update("ada_b", dmods_all, ada_b, True)
    update("kv_ada_b", dkvm_all.reshape(N_DEV, 1, nk_), kv_ada_b, True)
    update("kv_norm_g", parts_kv_norm, kv_norm_g, True)
    update("final_norm_g", parts_final, final_norm_g, True)
    update("norm_g", parts_norm, norm_g, True)
    update("conv_w", parts_conv, conv_w, True)

    return (loss, dx.reshape(x.shape), *[out_g[n] for n in names], *[out_d[n] for n in names],
            *[out_m[n] for n in names], *[out_v[n] for n in names])
```

```python
import functools

import jax
import jax.numpy as jnp
from jax import lax
from jax.experimental import pallas as pl
from jax.experimental.pallas import tpu as pltpu

F32, BF16 = jnp.float32, jnp.bfloat16

N_DEV = 8
MESH_AXES = ("x", "y", "c")
DEPTH = 4
N_A_LAYERS = 2
HEAD_DIM = 64
HEADS_PER_GROUP = 8
GROUP_WIDTH = HEAD_DIM * HEADS_PER_GROUP
DILATED_GROUPS = ((128, 1), (512, 4), (2048, 16))
ROPE_DIM = HEAD_DIM // 4
ROPE_THETA = 500000.0
NORM_EPS = 1e-5
FFN_RES_WEIGHT = 0.5
N_MOD = 9
ADAM_LR, ADAM_B1, ADAM_B2, ADAM_EPS, ADAM_WD, ADAM_STEP = 0.001, 0.9, 0.999, 1e-08, 0.01, 10

LANES = 128
TOKEN_TILE = 512
FFN_BWD_TILE = 256
CONTRACT_TILE = 2048
MXU_WIDTH = 256
VMEM_LIMIT = 56 * 1024 * 1024
MESH = pl.DeviceIdType.MESH


def _cp(*sem):
    return pltpu.CompilerParams(dimension_semantics=sem, vmem_limit_bytes=VMEM_LIMIT)


def _pick(n, cap, mult=LANES):
    if n <= cap:
        return n
    best = None
    for t in range(mult, cap + 1, mult):
        if n % t == 0:
            best = t
    assert best is not None, (n, cap)
    return best


def _tok(tm, w):
    return pl.BlockSpec((tm, w), lambda i: (i, 0))


def _res(shape):
    nd = len(shape)
    return pl.BlockSpec(shape, lambda *_: (0,) * nd, pipeline_mode=pl.Buffered(1))


def _sds(shape, dt):
    return jax.ShapeDtypeStruct(shape, dt)


def _sigmoid(a):
    return 1.0 / (1.0 + jnp.exp(-a))


def _modnorm(x, g, sh, sc):
    r = lax.rsqrt(jnp.mean(x * x, axis=-1, keepdims=True) + NORM_EPS)
    return (x * r * g) * (1.0 + sc) + sh


def _dot(a, b):
    return jnp.dot(a, b, preferred_element_type=F32)


def _dot_nt(a, b):
    return lax.dot_general(a, b, (((1,), (1,)), ((), ())), preferred_element_type=F32)


def _dot_tn(a, b):
    return lax.dot_general(a, b, (((0,), (0,)), ((), ())), preferred_element_type=F32)


def _rows8(rows, d):
    pad = 8 - len(rows)
    return jnp.concatenate(list(rows) + [jnp.zeros((pad, d), F32)], axis=0)


def _acc_rows(ref, tile, first):
    @pl.when(first)
    def _():
        ref[...] = tile

    @pl.when(jnp.logical_not(first))
    def _():
        ref[...] += tile


def ffn_fwd(x, vec, w_in_t, w_out):
    T, D = x.shape
    F = w_in_t.shape[0] // 2
    tm, cw = min(TOKEN_TILE, T), _pick(F, MXU_WIDTH)

    def body(x_ref, vec_ref, wi_ref, wo_ref, xn_ref, h_ref, ga_ref, gb_ref, u_ref, y_ref):
        x_t = x_ref[...]
        hb = _modnorm(x_t, vec_ref[0:1], vec_ref[1:2], vec_ref[2:3]).astype(BF16)
        h_ref[...] = hb
        for c in range(F // cw):
            lo, hi = c * cw, (c + 1) * cw
            a = _dot_nt(hb, wi_ref[lo:hi, :])
            b = _dot_nt(hb, wi_ref[F + lo:F + hi, :])
            sg = _sigmoid(a)
            silu = a * sg
            ga_ref[:, lo:hi] = (b * (sg + silu * (1.0 - sg))).astype(BF16)
            gb_ref[:, lo:hi] = silu.astype(BF16)
            u_ref[:, lo:hi] = (silu * b).astype(BF16)
        y = _dot(u_ref[...], wo_ref[...])
        y_ref[...] = y.astype(BF16)
        xn_ref[...] = x_t + (FFN_RES_WEIGHT * (1.0 + vec_ref[3:4])) * y

    return pl.pallas_call(
        body, grid=(T // tm,),
        in_specs=[_tok(tm, D), _res((8, D)), _res((2 * F, D)), _res((F, D))],
        out_specs=[_tok(tm, D), _tok(tm, D), _tok(tm, F), _tok(tm, F), _tok(tm, F), _tok(tm, D)],
        out_shape=[_sds((T, D), F32), _sds((T, D), BF16), _sds((T, F), BF16), _sds((T, F), BF16), _sds((T, F), BF16),
                   _sds((T, D), BF16)],
        compiler_params=_cp("arbitrary"), name="ffn_fwd")(x, vec, w_in_t, w_out)


def ffn_bwd(dxo, y, vec, w_out, w_in_t, a, b, x):
    T, D = x.shape
    F = a.shape[1]
    tm, cw = min(FFN_BWD_TILE, T), _pick(F, MXU_WIDTH)

    def body(dxo_ref, y_ref, vec_ref, wo_ref, wi_ref, a_ref, b_ref, x_ref, dy_ref, dab_ref, dx_ref, part_ref):
        dxo_t = dxo_ref[...]
        dyb = (dxo_t * (FFN_RES_WEIGHT * (1.0 + vec_ref[3:4]))).astype(BF16)
        dy_ref[...] = dyb
        dgate = FFN_RES_WEIGHT * jnp.sum(dxo_t * y_ref[...].astype(F32), axis=0, keepdims=True)
        for c in range(F // cw):
            lo, hi = c * cw, (c + 1) * cw
            du = _dot_nt(dyb, wo_ref[lo:hi, :])
            dab_ref[:, lo:hi] = (du * a_ref[:, lo:hi].astype(F32)).astype(BF16)
            dab_ref[:, F + lo:F + hi] = (du * b_ref[:, lo:hi].astype(F32)).astype(BF16)
        dh = _dot(dab_ref[...], wi_ref[...])
        _, vjp = jax.vjp(_modnorm, x_ref[...], vec_ref[0:1], vec_ref[1:2], vec_ref[2:3])
        dx, dg, dsh, dsc = vjp(dh)
        dx_ref[...] = dxo_t + dx
        _acc_rows(part_ref, _rows8([dg, dsh, dsc, dgate], D), pl.program_id(0) == 0)

    return pl.pallas_call(
        body, grid=(T // tm,),
        in_specs=[_tok(tm, D), _tok(tm, D), _res((8, D)), _res((F, D)), _res((2 * F, D)), _tok(tm, F), _tok(tm, F), _tok(tm, D)],
        out_specs=[_tok(tm, D), _tok(tm, 2 * F), _tok(tm, D), pl.BlockSpec((8, D), lambda i: (0, 0))],
        out_shape=[_sds((T, D), BF16), _sds((T, 2 * F), BF16), _sds((T, D), F32), _sds((8, D), F32)],
        compiler_params=_cp("arbitrary"), name="ffn_bwd")(dxo, y, vec, w_out, w_in_t, a, b, x)


def grad_slots(a, b, name, col_slots=False, after=None):
    T, M = a.shape
    extra = [] if after is None else [after]
    N = b.shape[1]
    tk = min(CONTRACT_TILE, T)
    nk = T // tk
    tmm = _pick(M, 1408)
    if col_slots:
        ns = N // N_DEV
        sp = max(s for s in (1, 2, 4, 8) if ns * s <= 1536)
        tn = ns * sp
    else:
        tn = _pick(N, 1536)

    def body(a_ref, b_ref, *rest):
        o_ref, acc = rest[-2:]
        k = pl.program_id(2)
        t = _dot_tn(a_ref[...], b_ref[...])

        @pl.when(k == 0)
        def _():
            acc[...] = t

        @pl.when(k > 0)
        def _():
            acc[...] += t

        @pl.when(k == nk - 1)
        def _():
            if col_slots:
                for s in range(sp):
                    o_ref[s] = acc[:, s * ns:(s + 1) * ns].astype(BF16)
            else:
                o_ref[...] = acc[...].astype(BF16)

    if col_slots:
        out_spec, out_shape = pl.BlockSpec((sp, tmm, ns), lambda i, j, k: (j, i, 0)), _sds((N_DEV, M, ns), BF16)
    else:
        out_spec, out_shape = pl.BlockSpec((tmm, tn), lambda i, j, k: (i, j)), _sds((M, N), BF16)
    out = pl.pallas_call(
        body, grid=(M // tmm, N // tn, nk),
        in_specs=[pl.BlockSpec((tk, tmm), lambda i, j, k: (k, i)), pl.BlockSpec((tk, tn), lambda i, j, k: (k, j))]
        + [pl.BlockSpec(memory_space=pl.ANY)] * len(extra),
        out_specs=out_spec, out_shape=out_shape,
        scratch_shapes=[pltpu.VMEM((tmm, tn), F32)],
        compiler_params=_cp("arbitrary", "arbitrary", "arbitrary"), name=name)(a, b, *extra)
    return out if col_slots else out.reshape(N_DEV, M // N_DEV, N)


def conv_fwd(x, vec, cw, w_in, w_out):
    T, D = x.shape
    tm = min(TOKEN_TILE, T)

    def body(x_ref, vec_ref, cw_ref, wi_ref, wo_ref, xn_ref, h_ref, bcu_ref, cv_ref, z_ref, y_ref, vbuf):
        @pl.when(pl.program_id(0) == 0)
        def _():
            vbuf[0:8, :] = jnp.zeros((8, D), F32)

        x_t = x_ref[...]
        hb = _modnorm(x_t, vec_ref[0:1], vec_ref[1:2], vec_ref[2:3]).astype(BF16)
        h_ref[...] = hb
        bcu = _dot(hb, wi_ref[...])
        bcu_ref[...] = bcu.astype(BF16)
        bg, v = bcu[:, 0:D], bcu[:, D:2 * D] * bcu[:, 2 * D:3 * D]
        vbuf[8:8 + tm, :] = v
        conv = cw_ref[0:1] * vbuf[6:6 + tm, :] + cw_ref[1:2] * vbuf[7:7 + tm, :] + cw_ref[2:3] * v
        cv_ref[...] = conv.astype(BF16)
        zb = (bg * conv).astype(BF16)
        z_ref[...] = zb
        y = _dot(zb, wo_ref[...])
        y_ref[...] = y.astype(BF16)
        xn_ref[...] = x_t + (1.0 + vec_ref[3:4]) * y
        vbuf[0:8, :] = vbuf[tm:tm + 8, :]

    return pl.pallas_call(
        body, grid=(T // tm,),
        in_specs=[_tok(tm, D), _res((8, D)), _res((8, D)), _res((D, 3 * D)), _res((D, D))],
        out_specs=[_tok(tm, D), _tok(tm, D), _tok(tm, 3 * D), _tok(tm, D), _tok(tm, D), _tok(tm, D)],
        out_shape=[_sds((T, D), F32), _sds((T, D), BF16), _sds((T, 3 * D), BF16), _sds((T, D), BF16),
                   _sds((T, D), BF16), _sds((T, D), BF16)],
        scratch_shapes=[pltpu.VMEM((tm + 8, D), F32)],
        compiler_params=_cp("arbitrary"), name="conv_fwd")(x, vec, cw, w_in, w_out)


def conv_bwd(dxo, x, y, bcu, cv, vec, cw, w_in, w_out):
    T, D = x.shape
    tm = min(TOKEN_TILE, T)
    nt = T // tm

    def body(dxo_ref, x_ref, y_ref, bcu_ref, cv_ref, vec_ref, cw_ref, wi_ref, wo_ref,
             dx_ref, dy_ref, dbcu_ref, part_ref, dcw_ref, dcbuf):
        first = pl.program_id(0) == 0

        @pl.when(first)
        def _():
            dcbuf[tm:tm + 8, :] = jnp.zeros((8, D), F32)

        dxo_t = dxo_ref[...]
        dyb = (dxo_t * (1.0 + vec_ref[3:4])).astype(BF16)
        dy_ref[...] = dyb
        dgate = jnp.sum(dxo_t * y_ref[...].astype(F32), axis=0, keepdims=True)
        dz = _dot_nt(dyb, wo_ref[...])
        bcu_t = bcu_ref[...].astype(F32)
        bg, cg, ug = bcu_t[:, 0:D], bcu_t[:, D:2 * D], bcu_t[:, 2 * D:3 * D]
        dconv = dz * bg
        dbg = dz * cv_ref[...].astype(F32)
        dcbuf[0:tm, :] = dconv
        d1, d2 = dcbuf[1:tm + 1, :], dcbuf[2:tm + 2, :]
        dv = cw_ref[2:3] * dconv + cw_ref[1:2] * d1 + cw_ref[0:1] * d2
        v = cg * ug
        dcw = _rows8([jnp.sum(d2 * v, axis=0, keepdims=True), jnp.sum(d1 * v, axis=0, keepdims=True),
                      jnp.sum(dconv * v, axis=0, keepdims=True)], D)
        dbcu = jnp.concatenate([dbg, dv * ug, dv * cg], axis=1).astype(BF16)
        dbcu_ref[...] = dbcu
        dh = _dot_nt(dbcu, wi_ref[...])
        _, vjp = jax.vjp(_modnorm, x_ref[...], vec_ref[0:1], vec_ref[1:2], vec_ref[2:3])
        dx, dg, dsh, dsc = vjp(dh)
        dx_ref[...] = dxo_t + dx
        _acc_rows(part_ref, _rows8([dg, dsh, dsc, dgate], D), first)
        _acc_rows(dcw_ref, dcw, first)
        dcbuf[tm:tm + 8, :] = dcbuf[0:8, :]

    def rev(w):
        return pl.BlockSpec((tm, w), lambda i: (nt - 1 - i, 0))

    return pl.pallas_call(
        body, grid=(nt,),
        in_specs=[rev(D), rev(D), rev(D), rev(3 * D), rev(D), _res((8, D)), _res((8, D)), _res((D, 3 * D)), _res((D, D))],
        out_specs=[rev(D), rev(D), rev(3 * D), pl.BlockSpec((8, D), lambda i: (0, 0)), pl.BlockSpec((8, D), lambda i: (0, 0))],
        out_shape=[_sds((T, D), F32), _sds((T, D), BF16), _sds((T, 3 * D), BF16), _sds((8, D), F32), _sds((8, D), F32)],
        scratch_shapes=[pltpu.VMEM((tm + 8, D), F32)],
        compiler_params=_cp("arbitrary"), name="conv_bwd")(dxo, x, y, bcu, cv, vec, cw, w_in, w_out)


def rope_tables(pos, lane_rows):
    T = pos.shape[0]
    tm = min(TOKEN_TILE, T)

    def body(p_ref, lr_ref, c_ref, sp_ref, sm_ref):
        ang = p_ref[...].astype(F32) * lr_ref[0:1]
        cs, sn = jnp.cos(ang), jnp.sin(ang)
        c_ref[...] = jnp.where(lr_ref[1:2] > 0.5, cs, 1.0)
        sp_ref[...] = jnp.where(lr_ref[2:3] > 0.5, sn, 0.0)
        sm_ref[...] = jnp.where(lr_ref[3:4] > 0.5, -sn, 0.0)

    return pl.pallas_call(
        body, grid=(T // tm,),
        in_specs=[_tok(tm, 1), _res((8, LANES))],
        out_specs=[_tok(tm, LANES)] * 3,
        out_shape=[_sds((T, LANES), F32)] * 3,
        compiler_params=_cp("arbitrary"), name="rope_tables")(pos, lane_rows)


def _rope(t, c, sp, sm):
    w = t.shape[1]
    reps = w // LANES
    cf, spf, smf = jnp.tile(c, (1, reps)), jnp.tile(sp, (1, reps)), jnp.tile(sm, (1, reps))
    half = ROPE_DIM // 2
    return t * cf + pltpu.roll(t, half, axis=1) * spf + pltpu.roll(t, w - half, axis=1) * smf


def _rope_t(d, c, sp, sm):
    w = d.shape[1]
    reps = w // LANES
    cf, spf, smf = jnp.tile(c, (1, reps)), jnp.tile(sp, (1, reps)), jnp.tile(sm, (1, reps))
    half = ROPE_DIM // 2
    return d * cf + pltpu.roll(d * spf, w - half, axis=1) + pltpu.roll(d * smf, half, axis=1)


def _split_residues(v, d, stage):
    tm, width = v.shape
    if d == 1:
        return [v]
    nj = width // LANES
    for j in range(nj):
        stage[j] = v[:, j * LANES:(j + 1) * LANES]
    return [jnp.concatenate([stage[j, pl.ds(r, tm // d, stride=d), :] for j in range(nj)], axis=1) for r in range(d)]


def _merge_residues(piece, d, tm, width, stage):
    if d == 1:
        return piece(0)
    nj = width // LANES
    for r in range(d):
        p = piece(r)
        for j in range(nj):
            stage[j, pl.ds(r, tm // d, stride=d), :] = p[:, j * LANES:(j + 1) * LANES]
    return jnp.concatenate([stage[j] for j in range(nj)], axis=1)


def _residue_spec(d, tm, width=GROUP_WIDTH):
    return pl.BlockSpec((d, tm // d, width), lambda i: (0, i, 0))


def _stage_scratch(tm):
    return pltpu.VMEM((GROUP_WIDTH // LANES, tm, LANES), F32)


def proj_rope_fwd(x, vec, w, tabs, n_rope, transposed, dils, name):
    T, D = x.shape
    N = w.shape[0] if transposed else w.shape[1]
    tm = min(TOKEN_TILE, T)
    GW = GROUP_WIDTH
    piece_dils = [dils[j % len(dils)] for j in range(N // GW)]

    def body(x_ref, vec_ref, w_ref, c_ref, sp_ref, sm_ref, h_ref, *rest):
        out_refs, stage = rest[:-1], rest[-1]
        hb = _modnorm(x_ref[...], vec_ref[0:1], vec_ref[1:2], vec_ref[2:3]).astype(BF16)
        h_ref[...] = hb
        p = _dot_nt(hb, w_ref[...]) if transposed else _dot(hb, w_ref[...])
        pr = _rope(p[:, 0:n_rope], c_ref[...], sp_ref[...], sm_ref[...])
        for j, d in enumerate(piece_dils):
            src = pr if (j + 1) * GW <= n_rope else p
            for r, rows in enumerate(_split_residues(src[:, j * GW:(j + 1) * GW], d, stage)):
                out_refs[j][r] = rows.astype(BF16)

    return pl.pallas_call(
        body, grid=(T // tm,),
        in_specs=[_tok(tm, D), _res((8, D)), _res(w.shape)] + [_tok(tm, LANES)] * 3,
        out_specs=[_tok(tm, D)] + [_residue_spec(d, tm) for d in piece_dils],
        out_shape=[_sds((T, D), BF16)] + [_sds((d, T // d, GW), BF16) for d in piece_dils],
        scratch_shapes=[_stage_scratch(tm)],
        compiler_params=_cp("arbitrary"), name=name)(x, vec, w, *tabs)


def proj_rope_bwd(dparts, dils, x, dxo, vec, w, tabs, n_rope, transposed, name):
    T, D = x.shape
    N = w.shape[0] if transposed else w.shape[1]
    tm = min(TOKEN_TILE, T)
    GW = GROUP_WIDTH
    npart = len(dparts)
    piece_dils = [dils[j % len(dils)] for j in range(npart)]

    def body(*refs):
        d_refs = refs[:npart]
        x_ref, dxo_ref, vec_ref, w_ref, c_ref, sp_ref, sm_ref, dx_ref, dp_ref, part_ref, stage = refs[npart:]
        d = jnp.concatenate([_merge_residues(lambda r, ref=ref: ref[r].astype(F32), dd, tm, GW, stage)
                             for ref, dd in zip(d_refs, piece_dils)], axis=1)
        dr = _rope_t(d[:, 0:n_rope], c_ref[...], sp_ref[...], sm_ref[...])
        if n_rope < N:
            dr = jnp.concatenate([dr, d[:, n_rope:N]], axis=1)
        dpb = dr.astype(BF16)
        dp_ref[...] = dpb
        dh = _dot(dpb, w_ref[...]) if transposed else _dot_nt(dpb, w_ref[...])
        _, vjp = jax.vjp(_modnorm, x_ref[...], vec_ref[0:1], vec_ref[1:2], vec_ref[2:3])
        dx, dg, dsh, dsc = vjp(dh)
        dx_ref[...] = dxo_ref[...] + dx
        _acc_rows(part_ref, _rows8([dg, dsh, dsc], D), pl.program_id(0) == 0)

    return pl.pallas_call(
        body, grid=(T // tm,),
        in_specs=[_residue_spec(d, tm) for d in piece_dils] + [_tok(tm, D), _tok(tm, D), _res((8, D)), _res(w.shape)]
        + [_tok(tm, LANES)] * 3,
        out_specs=[_tok(tm, D), _tok(tm, N), pl.BlockSpec((8, D), lambda i: (0, 0))],
        out_shape=[_sds((T, D), F32), _sds((T, N), BF16), _sds((8, D), F32)],
        scratch_shapes=[_stage_scratch(tm)],
        compiler_params=_cp("arbitrary"), name=name)(*dparts, x, dxo, vec, w, *tabs)


def _valid_mask(n, i):
    qi = lax.broadcasted_iota(jnp.int32, (n, 2 * n), 0)
    kj = lax.broadcasted_iota(jnp.int32, (n, 2 * n), 1)
    dist = n + qi - kj
    return (dist >= 0) & (dist <= n) & ((kj >= n) | (i > 0))


def _band_specs(n):
    two = pl.BlockSpec((None, 2 * n, GROUP_WIDTH), lambda r, i: (r, i, 0))
    prv = pl.BlockSpec((None, n, GROUP_WIDTH), lambda r, i: (r, jnp.maximum(2 * i - 1, 0), 0))
    one = pl.BlockSpec((None, n, GROUP_WIDTH), lambda r, i: (r, i, 0))
    return two, prv, one


def _pair_keys(prev_ref, two_ref, ps, n):
    cur2 = two_ref[:, ps]
    return jnp.concatenate([prev_ref[:, ps], cur2[0:n]], axis=0), cur2


STAT_STRIDE = LANES // HEADS_PER_GROUP


def _head_of_lane():
    return lax.broadcasted_iota(jnp.int32, (1, LANES), 1) // STAT_STRIDE


def attn_core_fwd(q, k, v, g, n):
    d, M, GW = q.shape
    scale = HEAD_DIM ** -0.5

    def body(q_ref, kp_ref, kc_ref, vp_ref, vc_ref, o_ref, l_ref):
        masks = (_valid_mask(n, pl.program_id(1)), _valid_mask(n, 1))
        first = lax.broadcasted_iota(jnp.int32, (1, LANES), 1) < HEAD_DIM
        head_of_lane = _head_of_lane()
        lse = [jnp.zeros((n, LANES), F32), jnp.zeros((n, LANES), F32)]
        for pair in range(HEADS_PER_GROUP * HEAD_DIM // LANES):
            ps = slice(LANES * pair, LANES * (pair + 1))
            keys, vals = _pair_keys(kp_ref, kc_ref, ps, n), _pair_keys(vp_ref, vc_ref, ps, n)
            for blk in range(2):
                rows = slice(blk * n, (blk + 1) * n)
                q2 = q_ref[rows, ps]
                o2, l2 = [], []
                for sel in (first, jnp.logical_not(first)):
                    s = jnp.where(masks[blk], _dot_nt(jnp.where(sel, q2, jnp.zeros_like(q2)), keys[blk]) * scale, -1e30)
                    m = jnp.max(s, axis=1, keepdims=True)
                    p = jnp.exp(s - m)
                    den = jnp.sum(p, axis=1, keepdims=True)
                    o2.append(_dot((p / den).astype(BF16), vals[blk]))
                    l2.append(m + jnp.log(den))
                o_ref[rows, ps] = jnp.where(first, o2[0], o2[1]).astype(BF16)
                for half in range(2):
                    lse[blk] = jnp.where(head_of_lane == 2 * pair + half, l2[half], lse[blk])
        for blk in range(2):
            l_ref[blk * n:(blk + 1) * n, :] = lse[blk]

    two, prv, _ = _band_specs(n)
    stat = pl.BlockSpec((None, 2 * n, LANES), lambda r, i: (r, i, 0))
    return pl.pallas_call(
        body, grid=(d, M // (2 * n)),
        in_specs=[two, prv, two, prv, two], out_specs=[two, stat],
        out_shape=[_sds((d, M, GW), BF16), _sds((d, M, LANES), F32)],
        compiler_params=_cp("arbitrary", "arbitrary"), name=f"attn_fwd_g{g}")(q, k, k, v, v)


def attn_core_bwd(q, k, v, do, delta, lse, g, n):
    d, M, GW = q.shape
    scale = HEAD_DIM ** -0.5

    def body(q_ref, kp_ref, kc_ref, vp_ref, vc_ref, do_ref, d_ref, l_ref, dq_ref, dkc_ref, dkp_ref, dvc_ref, dvp_ref):
        masks = (_valid_mask(n, pl.program_id(1)), _valid_mask(n, 1))
        first = lax.broadcasted_iota(jnp.int32, (1, LANES), 1) < HEAD_DIM
        for pair in range(HEADS_PER_GROUP * HEAD_DIM // LANES):
            ps = slice(LANES * pair, LANES * (pair + 1))
            keys, vals = _pair_keys(kp_ref, kc_ref, ps, n), _pair_keys(vp_ref, vc_ref, ps, n)
            own = []
            for blk in range(2):
                rows = slice(blk * n, (blk + 1) * n)
                q2, do2 = q_ref[rows, ps], do_ref[rows, ps]
                dq2, dk, dv = [], None, None
                for half, sel in enumerate((first, jnp.logical_not(first))):
                    qm = jnp.where(sel, q2, jnp.zeros_like(q2))
                    dom = jnp.where(sel, do2, jnp.zeros_like(do2))
                    s = jnp.where(masks[blk], _dot_nt(qm, keys[blk]) * scale, -1e30)
                    lane0 = STAT_STRIDE * (2 * pair + half)
                    p = jnp.exp(s - l_ref[rows, lane0:lane0 + 1])
                    dp = _dot_nt(dom, vals[blk])
                    ds = (p * (dp - d_ref[rows, lane0:lane0 + 1]) * scale).astype(BF16)
                    dq2.append(_dot(ds, keys[blk]))
                    dkh = _dot_tn(ds, qm)
                    dvh = _dot_tn(p.astype(BF16), dom)
                    dk = dkh if dk is None else dk + dkh
                    dv = dvh if dv is None else dv + dvh
                dq_ref[rows, ps] = jnp.where(first, dq2[0], dq2[1]).astype(BF16)
                own.append((dk, dv))
            for t, (c_ref, p_ref) in enumerate(((dkc_ref, dkp_ref), (dvc_ref, dvp_ref))):
                a, b = own[0][t], own[1][t]
                p_ref[:, ps] = a[0:n].astype(BF16)
                c_ref[0:n, ps] = (a[n:2 * n] + b[0:n]).astype(BF16)
                c_ref[n:2 * n, ps] = b[n:2 * n].astype(BF16)

    two, prv, one = _band_specs(n)
    stat = pl.BlockSpec((None, 2 * n, LANES), lambda r, i: (r, i, 0))
    return pl.pallas_call(
        body, grid=(d, M // (2 * n)),
        in_specs=[two, prv, two, prv, two, two, stat, stat], out_specs=[two, two, one, two, one],
        out_shape=[_sds((d, M, GW), BF16), _sds((d, M, GW), BF16), _sds((d, M // 2, GW), BF16),
                   _sds((d, M, GW), BF16), _sds((d, M // 2, GW), BF16)],
        compiler_params=_cp("arbitrary", "arbitrary"), name=f"attn_bwd_g{g}")(q, k, k, v, v, do, delta, lse)


def dkv_combine(cur_prev, n, name):
    d, M, GW = cur_prev[0][0].shape
    rows = min(M, 1024)
    pairs = rows // (2 * n)
    steps = M // rows
    flat = [a for pair in cur_prev for a in pair]

    def body(*refs):
        o_ref = refs[-1]
        last = pl.program_id(1) == steps - 1
        acc = None
        shifted = None
        for t in range(0, len(refs) - 1, 3):
            c = refs[t][...].astype(F32)
            nxt = jnp.where(last, 0.0, refs[t + 2][...].astype(F32))
            s = nxt if pairs == 1 else jnp.concatenate([refs[t + 1][n:pairs * n, :].astype(F32), nxt], axis=0)
            acc = c if acc is None else acc + c
            shifted = s if shifted is None else shifted + s
        for m in range(pairs):
            lo = 2 * m * n
            o_ref[lo:lo + n, :] = acc[lo:lo + n].astype(BF16)
            o_ref[lo + n:lo + 2 * n, :] = (acc[lo + n:lo + 2 * n] + shifted[m * n:(m + 1) * n]).astype(BF16)

    cur = pl.BlockSpec((None, rows, GW), lambda r, i: (r, i, 0))
    same = pl.BlockSpec((None, pairs * n, GW), lambda r, i: (r, i, 0))
    nxt = pl.BlockSpec((None, n, GW), lambda r, i: (r, jnp.minimum((i + 1) * pairs, M // (2 * n) - 1), 0))
    args = []
    for c, p in cur_prev:
        args += [c, p, p]
    return pl.pallas_call(
        body, grid=(d, steps), in_specs=[cur, same, nxt] * len(cur_prev), out_specs=cur,
        out_shape=_sds((d, M, GW), BF16),
        compiler_params=_cp("arbitrary", "arbitrary"), name=name)(*args)


def _group_weights(ls):
    mx = functools.reduce(jnp.maximum, ls)
    es = [jnp.exp(l - mx) for l in ls]
    tot = functools.reduce(lambda a, b: a + b, es)
    return [e / tot for e in es]


def _expand_heads(w):
    tm = w.shape[0]
    first = lax.broadcasted_iota(jnp.int32, (1, LANES), 1) < HEAD_DIM
    cols = [jnp.broadcast_to(w[:, STAT_STRIDE * h:STAT_STRIDE * h + 1], (tm, LANES)) for h in range(HEADS_PER_GROUP)]
    return jnp.concatenate([jnp.where(first, cols[2 * p], cols[2 * p + 1]) for p in range(HEADS_PER_GROUP // 2)], axis=1)


def _head_sums(r):
    head_of_lane = _head_of_lane()
    out = jnp.zeros((r.shape[0], LANES), F32)
    for h in range(HEADS_PER_GROUP):
        s = jnp.sum(r[:, HEAD_DIM * h:HEAD_DIM * (h + 1)], axis=1, keepdims=True)
        out = jnp.where(head_of_lane == h, s, out)
    return out


def _mix_weights(l_refs, dils, tm, stage):
    ls = [_merge_residues(lambda r, ref=ref: ref[r], d, tm, LANES, stage) for ref, d in zip(l_refs, dils)]
    return [_expand_heads(w) for w in _group_weights(ls)]


def attn_mix_out(os_, ls, dils, x, vec, w_o):
    T, D = x.shape
    GW = GROUP_WIDTH
    tm = min(TOKEN_TILE, T)
    ng = len(os_)

    def body(*refs):
        o_refs, l_refs = refs[:ng], refs[ng:2 * ng]
        x_ref, vec_ref, w_ref, xn_ref, mix_ref, y_ref, stage = refs[2 * ng:]
        natural = lambda ref, d: _merge_residues(lambda r: ref[r].astype(F32), d, tm, GW, stage)
        ws = _mix_weights(l_refs, dils, tm, stage)
        mixed = functools.reduce(lambda a, b: a + b, [w * natural(r, d) for w, r, d in zip(ws, o_refs, dils)])
        mb = mixed.astype(BF16)
        mix_ref[...] = mb
        y = _dot(mb, w_ref[...])
        y_ref[...] = y.astype(BF16)
        xn_ref[...] = x_ref[...] + (1.0 + vec_ref[3:4]) * y

    res = [_residue_spec(d, tm) for d in dils]
    stat = [_residue_spec(d, tm, LANES) for d in dils]
    return pl.pallas_call(
        body, grid=(T // tm,),
        in_specs=res + stat + [_tok(tm, D), _res((8, D)), _res((GW, D))],
        out_specs=[_tok(tm, D), _tok(tm, GW), _tok(tm, D)],
        out_shape=[_sds((T, D), F32), _sds((T, GW), BF16), _sds((T, D), BF16)],
        scratch_shapes=[_stage_scratch(tm)],
        compiler_params=_cp("arbitrary"), name="attn_mix_out")(*os_, *ls, x, vec, w_o)


def attn_mix_bwd(dxo, y, vec, w_o, os_, ls, dils):
    T, D = dxo.shape
    GW = GROUP_WIDTH
    tm = min(TOKEN_TILE, T)
    ng = len(os_)

    def body(*refs):
        dxo_ref, y_ref, vec_ref, w_ref = refs[:4]
        o_refs, l_refs = refs[4:4 + ng], refs[4 + ng:4 + 2 * ng]
        dy_ref = refs[4 + 2 * ng]
        do_refs = refs[5 + 2 * ng:5 + 3 * ng]
        d_refs = refs[5 + 3 * ng:5 + 4 * ng]
        part_ref, stage = refs[5 + 4 * ng], refs[6 + 4 * ng]
        natural = lambda ref, d: _merge_residues(lambda r: ref[r].astype(F32), d, tm, GW, stage)
        dxo_t = dxo_ref[...]
        dyb = (dxo_t * (1.0 + vec_ref[3:4])).astype(BF16)
        dy_ref[...] = dyb
        dgate = jnp.sum(dxo_t * y_ref[...].astype(F32), axis=0, keepdims=True)
        _acc_rows(part_ref, _rows8([dgate], D), pl.program_id(0) == 0)
        dmix = _dot_nt(dyb, w_ref[...])
        ws = _mix_weights(l_refs, dils, tm, stage)
        mixed = functools.reduce(lambda a, b: a + b, [w * natural(r, d) for w, r, d in zip(ws, o_refs, dils)])
        for gi in range(ng):
            do = ws[gi] * dmix
            for r, rows in enumerate(_split_residues(do, dils[gi], stage)):
                do_refs[gi][r] = rows.astype(BF16)
            for r, rows in enumerate(_split_residues(_head_sums(do * mixed), dils[gi], stage)):
                d_refs[gi][r] = rows

    res = [_residue_spec(d, tm) for d in dils]
    stat = [_residue_spec(d, tm, LANES) for d in dils]
    return pl.pallas_call(
        body, grid=(T // tm,),
        in_specs=[_tok(tm, D), _tok(tm, D), _res((8, D)), _res((GW, D))] + res + stat,
        out_specs=[_tok(tm, D)] + res + stat + [pl.BlockSpec((8, D), lambda i: (0, 0))],
        out_shape=[_sds((T, D), BF16)] + [_sds((d, T // d, GW), BF16) for d in dils]
        + [_sds((d, T // d, LANES), F32) for d in dils] + [_sds((8, D), F32)],
        scratch_shapes=[_stage_scratch(tm)],
        compiler_params=_cp("arbitrary"), name="attn_mix_bwd")(dxo, y, vec, w_o, *os_, *ls)


def final_loss(x, gvec, target):
    T, D = x.shape
    tm = min(TOKEN_TILE, T)

    def norm(xv, g):
        return xv * lax.rsqrt(jnp.mean(xv * xv, axis=-1, keepdims=True) + NORM_EPS) * g

    def body(x_ref, g_ref, t_ref, dx_ref, part_ref, loss_ref):
        first = pl.program_id(0) == 0
        yv, vjp = jax.vjp(norm, x_ref[...], g_ref[0:1])
        err = yv - t_ref[...]
        dx, dg = vjp(err * (1.0 / D))
        dx_ref[...] = dx
        _acc_rows(part_ref, _rows8([dg], D), first)
        tile_loss = 0.5 * jnp.sum(jnp.sum(err * err, axis=1, keepdims=True) * (1.0 / D), axis=0, keepdims=True)
        _acc_rows(loss_ref, jnp.broadcast_to(tile_loss, (8, LANES)), first)

    return pl.pallas_call(
        body, grid=(T // tm,),
        in_specs=[_tok(tm, D), _res((8, D)), _tok(tm, D)],
        out_specs=[_tok(tm, D), pl.BlockSpec((8, D), lambda i: (0, 0)), pl.BlockSpec((8, LANES), lambda i: (0, 0))],
        out_shape=[_sds((T, D), F32), _sds((8, D), F32), _sds((8, LANES), F32)],
        compiler_params=_cp("arbitrary"), name="final_loss")(x, gvec, target)


def mods_project(c_all, w, b):
    B, D = c_all.shape
    L, _, N = w.shape

    def body(c_ref, w_ref, b_ref, o_ref):
        cv = c_ref[...]
        cond = cv * _sigmoid(cv)
        o_ref[0] = jnp.dot(cond, w_ref[0], preferred_element_type=F32, precision=lax.Precision.HIGHEST) + b_ref[0]

    return pl.pallas_call(
        body, grid=(L,),
        in_specs=[pl.BlockSpec((B, D), lambda l: (0, 0)), pl.BlockSpec((1, D, N), lambda l: (l, 0, 0)),
                  pl.BlockSpec((1, 1, N), lambda l: (l, 0, 0))],
        out_specs=pl.BlockSpec((1, B, N), lambda l: (l, 0, 0)),
        out_shape=_sds((L, B, N), F32),
        compiler_params=_cp("arbitrary"), name="mods_project")(c_all, w, b)


def mods_weight_grad(c_all, dm):
    B, D = c_all.shape
    L, _, N = dm.shape

    def body(c_ref, d_ref, o_ref):
        cv = c_ref[...]
        cond = cv * _sigmoid(cv)
        o_ref[0] = lax.dot_general(cond, d_ref[0], (((0,), (0,)), ((), ())), preferred_element_type=F32,
                                   precision=lax.Precision.HIGHEST)

    return pl.pallas_call(
        body, grid=(L,),
        in_specs=[pl.BlockSpec((B, D), lambda l: (0, 0)), pl.BlockSpec((1, B, N), lambda l: (l, 0, 0))],
        out_specs=pl.BlockSpec((1, D, N), lambda l: (l, 0, 0)),
        out_shape=_sds((L, D, N), F32),
        compiler_params=_cp("arbitrary"), name="mods_weight_grad")(c_all, dm)


def _adam_math(g, w, m, v):
    m2 = ADAM_B1 * m + (1.0 - ADAM_B1) * g
    v2 = ADAM_B2 * v + (1.0 - ADAM_B2) * (g * g)
    m_hat = m2 / (1.0 - ADAM_B1 ** ADAM_STEP)
    v_hat = v2 / (1.0 - ADAM_B2 ** ADAM_STEP)
    delta = -ADAM_LR * (m_hat / (jnp.sqrt(v_hat) + ADAM_EPS) + ADAM_WD * w)
    return delta, m2, v2


def adam_update(g, w, m, v, parts, name):
    R, C = w.shape
    tr = _pick(R, 256, 8)

    def body(g_ref, w_ref, m_ref, v_ref, go_ref, d_ref, mo_ref, vo_ref):
        if parts:
            gv = g_ref[0].astype(F32)
            for s in range(1, N_DEV):
                gv = gv + g_ref[s].astype(F32)
        else:
            gv = g_ref[...]
        go_ref[...] = gv
        d_ref[...], mo_ref[...], vo_ref[...] = _adam_math(gv, w_ref[...], m_ref[...], v_ref[...])

    gspec = pl.BlockSpec((N_DEV, tr, C), lambda i: (0, i, 0)) if parts else _tok(tr, C)
    return pl.pallas_call(
        body, grid=(R // tr,),
        in_specs=[gspec, _tok(tr, C), _tok(tr, C), _tok(tr, C)],
        out_specs=[_tok(tr, C)] * 4, out_shape=[_sds((R, C), F32)] * 4,
        compiler_params=_cp("arbitrary"), name=name)(g, w, m, v)


def adam_layer(parts, w, m, v, prev, layer, after, name):
    L, R, C = w.shape
    tr = _pick(R, 256, 8)
    prev = (list(prev) if prev is not None else []) + [after]

    def body(p_ref, w_ref, m_ref, v_ref, *rest):
        go_ref, d_ref, mo_ref, vo_ref = rest[-4:]
        gv = p_ref[0].astype(F32)
        for s in range(1, N_DEV):
            gv = gv + p_ref[s].astype(F32)
        go_ref[...] = gv
        d_ref[...], mo_ref[...], vo_ref[...] = _adam_math(gv, w_ref[...], m_ref[...], v_ref[...])

    lay = pl.BlockSpec((None, tr, C), lambda i: (layer, i, 0))
    return pl.pallas_call(
        body, grid=(R // tr,),
        in_specs=[pl.BlockSpec((N_DEV, tr, C), lambda i: (0, i, 0)), lay, lay, lay] + [pl.BlockSpec(memory_space=pl.ANY)] * len(prev),
        out_specs=[lay] * 4, out_shape=[_sds((L, R, C), F32)] * 4,
        input_output_aliases={4 + k: k for k in range(len(prev) - 1)},
        compiler_params=_cp("arbitrary"), name=name)(parts, w, m, v, *prev)


def _my_id():
    return 4 * lax.axis_index("x") + 2 * lax.axis_index("y") + lax.axis_index("c")


def _peer(s):
    x, y, c = lax.axis_index("x"), lax.axis_index("y"), lax.axis_index("c")
    px = (1 - x) if s & 4 else x
    py = (1 - y) if s & 2 else y
    pc = (1 - c) if s & 1 else c
    return (px, py, pc), 4 * px + 2 * py + pc


def all_gather(xs, space, name):
    na = len(xs)

    def body(*refs):
        x_refs, o_refs = refs[:na], refs[na:2 * na]
        send_sems, recv_sems, local_sems = refs[2 * na:]
        me = _my_id()
        locals_, sends = [], []
        for a in range(na):
            cp = pltpu.make_async_copy(x_refs[a], o_refs[a].at[me], local_sems.at[a])
            cp.start()
            locals_.append(cp)
        for s in range(1, N_DEV):
            peer, _ = _peer(s)
            for a in range(na):
                cp = pltpu.make_async_remote_copy(
                    src_ref=x_refs[a], dst_ref=o_refs[a].at[me], send_sem=send_sems.at[a, s - 1],
                    recv_sem=recv_sems.at[a, s - 1], device_id=peer, device_id_type=MESH)
                cp.start()
                sends.append(cp)
        for s in range(1, N_DEV):
            peer, pid = _peer(s)
            for a in range(na):
                pltpu.make_async_remote_copy(
                    src_ref=x_refs[a], dst_ref=o_refs[a].at[pid], send_sem=send_sems.at[a, s - 1],
                    recv_sem=recv_sems.at[a, s - 1], device_id=peer, device_id_type=MESH).wait_recv()
        for cp in sends:
            cp.wait_send()
        for cp in locals_:
            cp.wait()

    spec = pl.BlockSpec(memory_space=space)
    return pl.pallas_call(
        body, in_specs=[spec] * na, out_specs=[spec] * na,
        out_shape=[_sds((N_DEV,) + x.shape, x.dtype) for x in xs],
        scratch_shapes=[pltpu.SemaphoreType.DMA((na, N_DEV - 1)), pltpu.SemaphoreType.DMA((na, N_DEV - 1)),
                        pltpu.SemaphoreType.DMA((na,))],
        compiler_params=pltpu.CompilerParams(vmem_limit_bytes=VMEM_LIMIT), name=name)(*xs)


def exchange_slots(xs, name):
    na = len(xs)

    def body(*refs):
        x_refs, o_refs = refs[:na], refs[na:2 * na]
        send_sems, recv_sems, local_sems = refs[2 * na:]
        me = _my_id()
        locals_, sends = [], []
        for a in range(na):
            cp = pltpu.make_async_copy(x_refs[a].at[me], o_refs[a].at[me], local_sems.at[a])
            cp.start()
            locals_.append(cp)
        for s in range(1, N_DEV):
            peer, pid = _peer(s)
            for a in range(na):
                cp = pltpu.make_async_remote_copy(
                    src_ref=x_refs[a].at[pid], dst_ref=o_refs[a].at[me], send_sem=send_sems.at[a, s - 1],
                    recv_sem=recv_sems.at[a, s - 1], device_id=peer, device_id_type=MESH)
                cp.start()
                sends.append(cp)
        for s in range(1, N_DEV):
            peer, pid = _peer(s)
            for a in range(na):
                pltpu.make_async_remote_copy(
                    src_ref=x_refs[a].at[pid], dst_ref=o_refs[a].at[pid], send_sem=send_sems.at[a, s - 1],
                    recv_sem=recv_sems.at[a, s - 1], device_id=peer, device_id_type=MESH).wait_recv()
        for cp in sends:
            cp.wait_send()
        for cp in locals_:
            cp.wait()

    spec = pl.BlockSpec(memory_space=pl.ANY)
    return pl.pallas_call(
        body, in_specs=[spec] * na, out_specs=[spec] * na,
        out_shape=[_sds(x.shape, x.dtype) for x in xs],
        scratch_shapes=[pltpu.SemaphoreType.DMA((na, N_DEV - 1)), pltpu.SemaphoreType.DMA((na, N_DEV - 1)),
                        pltpu.SemaphoreType.DMA((na,))],
        compiler_params=pltpu.CompilerParams(vmem_limit_bytes=VMEM_LIMIT), name=name)(*xs)


_HBM = pl.BlockSpec(memory_space=pltpu.HBM)
_SEM = pl.BlockSpec(memory_space=pltpu.SEMAPHORE)
_EFFECT = pltpu.SideEffectType.DATAFLOW_SIDE_EFFECTING


def _split_copies(pattern, x_ref, land_ref, send_sem, recv_sem):
    me = _my_id()
    if pattern in ("gather", "scatter"):
        plan = []
        for s in range(1, N_DEV):
            peer, pid = _peer(s)
            plan.append((x_ref.at[pid] if pattern == "scatter" else x_ref, land_ref.at[me], peer))
    elif pattern == "to_chips":
        plan = [(x_ref, land_ref.at[me], _peer(s)[0]) for s in (1, 2, 4, 6)]
    else:
        sibling = _peer(1)[0]
        plan = [(land_ref.at[_peer(s)[1]], land_ref.at[_peer(s)[1]], sibling) for s in (2, 4, 6)]
    return [pltpu.make_async_remote_copy(src_ref=src, dst_ref=dst, send_sem=send_sem, recv_sem=recv_sem,
                                         device_id=dev, device_id_type=MESH) for src, dst, dev in plan]


def comm_start(xs, pattern, after, name, lands=None):
    na = len(xs)
    extra = [] if after is None else [after]
    me = _my_id()
    if lands is None:
        lands = []
        for x in xs:
            shape = x.shape if pattern == "scatter" else (N_DEV,) + x.shape
            own = lax.dynamic_slice_in_dim(x, me, 1, 0) if pattern == "scatter" else x[None]
            lands.append(lax.dynamic_update_slice(lax.empty(shape, x.dtype), own, (me,) + (0,) * (len(shape) - 1)))

    def body(*refs):
        x_refs, land_refs = refs[:na], refs[na:2 * na]
        send_sem, recv_sem = refs[2 * na + len(extra)], refs[2 * na + len(extra) + 1]
        token = refs[-1]
        for a in range(na):
            for cp in _split_copies(pattern, x_refs[a], land_refs[a], send_sem, recv_sem):
                cp.start()
        token[...] = jnp.zeros_like(token)

    outs = pl.pallas_call(
        body, name=name,
        out_shape=(pltpu.SemaphoreType.DMA(()), pltpu.SemaphoreType.DMA(()))
        + tuple(pltpu.HBM(x.shape, x.dtype) for x in xs) + tuple(pltpu.HBM(l.shape, l.dtype) for l in lands)
        + (_sds((8, LANES), F32),),
        in_specs=(_HBM,) * (2 * na) + (pl.BlockSpec(memory_space=pl.ANY),) * len(extra),
        out_specs=(_SEM, _SEM) + (_HBM,) * (2 * na) + (pl.BlockSpec(memory_space=pltpu.VMEM),),
        input_output_aliases={a: 2 + a for a in range(2 * na)},
        compiler_params=pltpu.CompilerParams(has_side_effects=_EFFECT),
    )(*[pltpu.with_memory_space_constraint(x, pltpu.HBM) for x in xs],
      *[pltpu.with_memory_space_constraint(l, pltpu.HBM) for l in lands], *extra)
    return dict(sems=outs[0:2], xs=outs[2:2 + na], lands=outs[2 + na:2 + 2 * na], token=outs[-1], pattern=pattern)


def comm_wait(started, after, name, with_xs=False):
    xs, lands = started["xs"], started["lands"]
    pattern = started["pattern"]
    na = len(xs)

    def body(*refs):
        x_refs, land_refs = refs[:na], refs[na:2 * na]
        send_sem, recv_sem = refs[2 * na], refs[2 * na + 1]
        for a in range(na):
            for cp in _split_copies(pattern, x_refs[a], land_refs[a], send_sem, recv_sem):
                cp.wait_send()
                cp.wait_recv()

    outs = pl.pallas_call(
        body, name=name,
        out_shape=tuple(pltpu.HBM(x.shape, x.dtype) for x in xs) + tuple(pltpu.HBM(l.shape, l.dtype) for l in lands),
        in_specs=(_HBM,) * (2 * na) + (_SEM, _SEM, pl.BlockSpec(memory_space=pl.ANY)),
        out_specs=(_HBM,) * (2 * na),
        input_output_aliases={a: a for a in range(2 * na)},
        compiler_params=pltpu.CompilerParams(has_side_effects=_EFFECT),
    )(*xs, *lands, *started["sems"], after)
    return (list(outs[na:]), list(outs[:na])) if with_xs else list(outs[na:])


def _cols_to_natural(g):
    return jnp.concatenate([g[k] for k in range(N_DEV)], axis=1)


def _vec8(rows, d):
    rows = [r.reshape(1, d).astype(F32) for r in rows]
    return jnp.concatenate(rows + [jnp.zeros((8 - len(rows), d), F32)], axis=0)


def _ffn_forward(x, vec, w_in_t, w_out):
    xn, h, a, b, u, y = ffn_fwd(x, vec, w_in_t, w_out)
    return xn, (x, h, a, b, u, y)


def _ffn_backward(dxo, saved, vec, w_in_t, w_out, on_rows=None):
    x, h, a, b, u, y = saved
    dy, dab, dx, part = ffn_bwd(dxo, y, vec, w_out, w_in_t, a, b, x)
    rows = part[0:4]
    token = on_rows(rows) if on_rows is not None else None
    g_out = grad_slots(u, dy, "ffn_dw_out", after=token)
    g_in_t = grad_slots(dab, h, "ffn_dw_in", after=token)
    return dx, g_in_t, g_out, rows


_TRANSPOSED = ("ffn1_w_in", "ffn2_w_in", "attn_w_q")
_COL_NATURAL = ("conv_w_in", "w_kv", "attn_w_o")
_ROW_SHARDED = ("ffn1_w_out", "ffn2_w_out", "conv_w_out")
_BIG = _TRANSPOSED + _COL_NATURAL + _ROW_SHARDED


def weight_chunks():
    chunks = []
    for layer in range(DEPTH):
        first = [("ffn1_w_in", layer), ("ffn1_w_out", layer)]
        if layer == N_A_LAYERS:
            first = [("w_kv", layer)] + first
        mixer = [("conv_w_in", layer), ("conv_w_out", layer)] if layer < N_A_LAYERS else [("attn_w_q", layer), ("attn_w_o", layer)]
        rest = mixer + [("ffn2_w_in", layer), ("ffn2_w_out", layer)]
        chunks += [first, rest] if layer == 0 else [first + rest]
    return chunks


def stacked_index(name, layer):
    if name == "w_kv":
        return None
    return layer - N_A_LAYERS if name.startswith("attn") else layer


class ChunkComm:
    def __init__(self, shards):
        self.shards = shards
        self.chunks = weight_chunks()

    def _shard(self, name, layer):
        idx = stacked_index(name, layer)
        return self.shards[name][0 if idx is None else idx]

    def start_gather(self, ci, after):
        xs = [self._shard(n, l).astype(BF16) for n, l in self.chunks[ci]]
        return comm_start(xs, "to_chips", after, f"gather_start_{ci}")

    def relay_gather(self, ci, started, after):
        lands, xs = comm_wait(started, after, f"gather_wait_{ci}", with_xs=True)
        return comm_start(xs, "relay", None, f"gather_relay_{ci}", lands=lands)

    def finish_gather(self, ci, relayed, after):
        lands = comm_wait(relayed, after, f"gather_done_{ci}")
        W = {}
        for key, g in zip(self.chunks[ci], lands):
            W[key] = _cols_to_natural(g) if key[0] in _COL_NATURAL else g.reshape(-1, g.shape[2])
        return W

    def start_exchange(self, ci, slots, after):
        return comm_start([slots[key] for key in self.chunks[ci]], "scatter", after, f"exchange_start_{ci}")

    def finish_exchange(self, ci, started, after):
        lands = comm_wait(started, after, f"exchange_wait_{ci}")
        return dict(zip(self.chunks[ci], lands))


def device_step(x, positions, target, mods, kvmods, small, comm, gather0):
    T, D = x.shape
    groups = DILATED_GROUPS
    dils = [dil for _, dil in groups]
    lane = jnp.arange(LANES) % HEAD_DIM
    inv = ROPE_THETA ** (-jnp.arange(0, ROPE_DIM, 2, dtype=F32) / ROPE_DIM)
    lane_rows = _vec8([jnp.where(lane < ROPE_DIM, inv[lane % (ROPE_DIM // 2)], 0.0), lane < ROPE_DIM,
                       (lane >= ROPE_DIM // 2) & (lane < ROPE_DIM), lane < ROPE_DIM // 2], LANES)
    tabs = rope_tables(positions.reshape(T, 1), lane_rows)

    def after_token(v, token):
        return v if token is None else v + token[0, 0]

    def vec_of(layer, sub):
        return _vec8([small["norm_g"][layer, sub], mods[layer, 3 * sub], mods[layer, 3 * sub + 1], mods[layer, 3 * sub + 2]], D)

    saved = []
    kv_saved = None
    k_sh = v_sh = None
    qw = GROUP_WIDTH * len(groups)
    chunk_of = {key: ci for ci, chunk in enumerate(comm.chunks) for key in chunk}
    W = {}
    flight = {"ci": 0, "started": gather0}

    def need(key, after):
        if key not in W:
            ci = chunk_of[key]
            assert ci == flight["ci"], (key, ci)
            relayed = comm.relay_gather(ci, flight["started"], after)
            token = relayed["token"]
            if ci + 1 < len(comm.chunks):
                flight.update(ci=ci + 1, started=comm.start_gather(ci + 1, token))
                token = flight["started"]["token"]
            W.update(comm.finish_gather(ci, relayed, token))
        return W[key]

    for layer in range(DEPTH):
        if layer == N_A_LAYERS:
            w_kv = need(("w_kv", layer), x)
            kv_vec = _vec8([small["kv_norm_g"], kvmods[0], kvmods[1]], D)
            h_kv, *kv_pieces = proj_rope_fwd(x, kv_vec, w_kv, tabs, qw, False, dils, "kv_fwd")
            k_sh, v_sh = kv_pieces[:len(groups)], kv_pieces[len(groups):]
            kv_saved = (x, h_kv, kv_vec)
        rec = {}
        w_in, w_out = need(("ffn1_w_in", layer), x), need(("ffn1_w_out", layer), x)
        v1 = vec_of(layer, 0)
        x, rec["ffn1"] = _ffn_forward(x, v1, w_in, w_out)
        if layer < N_A_LAYERS:
            w_in, w_out = need(("conv_w_in", layer), x), need(("conv_w_out", layer), x)
            v2 = vec_of(layer, 1)
            cw = _vec8(list(small["conv_w"][layer]), D)
            x_in = x
            x, h, bcu, cv, z, y = conv_fwd(x, v2, cw, w_in, w_out)
            rec["mix"] = (x_in, h, bcu, cv, z, y, cw)
        else:
            w_q, w_o = need(("attn_w_q", layer), x), need(("attn_w_o", layer), x)
            v2 = vec_of(layer, 1)
            x_in = x
            h, *q = proj_rope_fwd(x, v2, w_q, tabs, qw, True, dils, "q_fwd")
            os_, ls = [], []
            for g, (win, dil) in enumerate(groups):
                o, l = attn_core_fwd(q[g], k_sh[g], v_sh[g], g, win // dil)
                os_.append(o)
                ls.append(l)
            x, mixed, y = attn_mix_out(os_, ls, dils, x, v2, w_o)
            rec["mix"] = (x_in, h, q, os_, ls, mixed, y)
        w_in, w_out = need(("ffn2_w_in", layer), x), need(("ffn2_w_out", layer), x)
        v3 = vec_of(layer, 2)
        x, rec["ffn2"] = _ffn_forward(x, v3, w_in, w_out)
        rec["vecs"] = (v1, v2, v3)
        saved.append(rec)

    dx, part_final, loss_tile = final_loss(x, _vec8([small["final_norm_g"]], D), target)
    loss = loss_tile[0, 0]

    conv_rows = [None] * N_A_LAYERS
    kv_rows = None
    mod_rows = [[None] * 3 for _ in range(DEPTH)]
    dkv_pairs = [{"k": [], "v": []} for _ in groups]
    slots = {}
    exchanges = []
    token = None

    def send_ready_chunks():
        nonlocal token
        for ci in reversed(range(len(comm.chunks))):
            if ci not in [e[0] for e in exchanges] and all(key in slots for key in comm.chunks[ci]):
                started = comm.start_exchange(ci, slots, token)
                exchanges.append((ci, started))
                token = started["token"]

    vector_gather = {}

    def start_vector_gather(rows0):
        mod_rows[0][0] = rows0
        rows = jnp.stack([jnp.stack(r) for r in mod_rows])
        vecs = jnp.concatenate([rows[:, :, 1:4].reshape(-1), kv_rows[1:3].reshape(-1), kv_rows[0], part_final[0],
                                rows[:, :, 0].reshape(-1), jnp.stack(conv_rows).reshape(-1)])
        vector_gather["count"] = vecs.shape[0]
        vecs = _pad_rows(vecs.reshape(-1, 1), 8 * LANES).reshape(-1, LANES)
        vector_gather["started"] = comm_start([vecs], "gather", None, "vector_grads_start")
        return vector_gather["started"]["token"]

    for layer in reversed(range(DEPTH)):
        rec = saved[layer]
        v1, v2, v3 = rec["vecs"]
        dx, slots[("ffn2_w_in", layer)], slots[("ffn2_w_out", layer)], mod_rows[layer][2] = _ffn_backward(
            dx, rec["ffn2"], after_token(v3, token), W[("ffn2_w_in", layer)], W[("ffn2_w_out", layer)])
        if layer < N_A_LAYERS:
            x_in, h, bcu, cv, z, y, cw = rec["mix"]
            dx, dy, dbcu, part, dcw = conv_bwd(dx, x_in, y, bcu, cv, v2, cw, W[("conv_w_in", layer)], W[("conv_w_out", layer)])
            slots[("conv_w_out", layer)] = grad_slots(z, dy, "conv_dw_out")
            slots[("conv_w_in", layer)] = grad_slots(h, dbcu, "conv_dw_in", col_slots=True)
            conv_rows[layer] = dcw[0:3]
            mod_rows[layer][1] = part[0:4]
        else:
            x_in, h, q, os_, ls, mixed, y = rec["mix"]
            outs = attn_mix_bwd(dx, y, v2, W[("attn_w_o", layer)], os_, ls, dils)
            ng = len(groups)
            dy, dos, deltas, part_gate = outs[0], outs[1:1 + ng], outs[1 + ng:1 + 2 * ng], outs[1 + 2 * ng]
            slots[("attn_w_o", layer)] = grad_slots(mixed, dy, "attn_dw_o", col_slots=True)
            dqs = []
            for g, (win, dil) in enumerate(groups):
                dq, dkc, dkp, dvc, dvp = attn_core_bwd(q[g], k_sh[g], v_sh[g], dos[g], deltas[g], ls[g], g, win // dil)
                dqs.append(dq)
                dkv_pairs[g]["k"].append((dkc, dkp))
                dkv_pairs[g]["v"].append((dvc, dvp))
            dx, dqr, part_norm = proj_rope_bwd(dqs, dils, x_in, dx, v2, W[("attn_w_q", layer)], tabs, qw, True, "q_bwd")
            slots[("attn_w_q", layer)] = grad_slots(dqr, h, "attn_dw_q")
            mod_rows[layer][1] = jnp.concatenate([part_norm[0:3], part_gate[0:1]], axis=0)
        send_ready_chunks()
        dx, slots[("ffn1_w_in", layer)], slots[("ffn1_w_out", layer)], mod_rows[layer][0] = _ffn_backward(
            dx, rec["ffn1"], after_token(v1, token), W[("ffn1_w_in", layer)], W[("ffn1_w_out", layer)],
            on_rows=start_vector_gather if layer == 0 else None)
        if layer == N_A_LAYERS:
            x_kv, h_kv, kv_vec = kv_saved
            dparts = [dkv_combine(dkv_pairs[g]["k"], win // dil, f"dk_combine_g{g}") for g, (win, dil) in enumerate(groups)]
            dparts += [dkv_combine(dkv_pairs[g]["v"], win // dil, f"dv_combine_g{g}") for g, (win, dil) in enumerate(groups)]
            dx, dkvp, part_kv = proj_rope_bwd(dparts, dils, x_kv, dx, kv_vec, W[("w_kv", layer)], tabs, qw, False, "kv_bwd")
            slots[("w_kv", layer)] = grad_slots(h_kv, dkvp, "kv_dw", col_slots=True)
            kv_rows = part_kv[0:3]
        send_ready_chunks()

    return loss, dx, {"exchanges": exchanges, "vector_gather": vector_gather}


def _flat2(a):
    return a.reshape(-1, a.shape[-1])


def _pad_rows(a, mult):
    r = a.shape[0]
    pad = (-r) % mult
    return a if pad == 0 else jnp.concatenate([a, jnp.zeros((pad,) + a.shape[1:], a.dtype)], axis=0)


def kernel(x, c, positions, norm_g, ada_w, ada_b, ffn1_w_in, ffn1_w_out, ffn2_w_in, ffn2_w_out, conv_w_in, conv_w, conv_w_out, kv_norm_g, kv_ada_w, kv_ada_b, w_kv, attn_w_q, attn_w_o, final_norm_g, loss_target, m_norm_g, m_ada_w, m_ada_b, m_ffn1_w_in, m_ffn1_w_out, m_ffn2_w_in, m_ffn2_w_out, m_conv_w_in, m_conv_w, m_conv_w_out, m_kv_norm_g, m_kv_ada_w, m_kv_ada_b, m_w_kv, m_attn_w_q, m_attn_w_o, m_final_norm_g, v_norm_g, v_ada_w, v_ada_b, v_ffn1_w_in, v_ffn1_w_out, v_ffn2_w_in, v_ffn2_w_out, v_conv_w_in, v_conv_w, v_conv_w_out, v_kv_norm_g, v_kv_ada_w, v_kv_ada_b, v_w_kv, v_attn_w_q, v_attn_w_o, v_final_norm_g):
    names = ("norm_g", "ada_w", "ada_b", "ffn1_w_in", "ffn1_w_out", "ffn2_w_in", "ffn2_w_out", "conv_w_in", "conv_w",
             "conv_w_out", "kv_norm_g", "kv_ada_w", "kv_ada_b", "w_kv", "attn_w_q", "attn_w_o", "final_norm_g")
    wts = dict(zip(names, (norm_g, ada_w, ada_b, ffn1_w_in, ffn1_w_out, ffn2_w_in, ffn2_w_out, conv_w_in, conv_w, conv_w_out,
                           kv_norm_g, kv_ada_w, kv_ada_b, w_kv, attn_w_q, attn_w_o, final_norm_g)))
    mom = dict(zip(names, (m_norm_g, m_ada_w, m_ada_b, m_ffn1_w_in, m_ffn1_w_out, m_ffn2_w_in, m_ffn2_w_out, m_conv_w_in,
                           m_conv_w, m_conv_w_out, m_kv_norm_g, m_kv_ada_w, m_kv_ada_b, m_w_kv, m_attn_w_q, m_attn_w_o,
                           m_final_norm_g)))
    var = dict(zip(names, (v_norm_g, v_ada_w, v_ada_b, v_ffn1_w_in, v_ffn1_w_out, v_ffn2_w_in, v_ffn2_w_out, v_conv_w_in,
                           v_conv_w, v_conv_w_out, v_kv_norm_g, v_kv_ada_w, v_kv_ada_b, v_w_kv, v_attn_w_q, v_attn_w_o,
                           v_final_norm_g)))
    T, D = x.shape[1], x.shape[2]
    me = _my_id()
    nmod = ada_w.shape[2]
    nkv = kv_ada_w.shape[1]

    def stacked(w, n):
        w = w if w.ndim == 3 else w[None]
        return jnp.swapaxes(w, 1, 2) if n in _TRANSPOSED else w

    comm = ChunkComm({n: stacked(wts[n], n) for n in _BIG})
    W = {}

    ds = norm_g.shape[2]
    small = jnp.concatenate([c.reshape(-1), norm_g.reshape(-1), conv_w.reshape(-1)]).astype(F32)
    n_small = small.shape[0]
    small = _pad_rows(small.reshape(-1, 1), 8 * LANES).reshape(-1, LANES)
    (small_all,) = all_gather([small], pltpu.VMEM, "gather_small")
    small_all = small_all.reshape(N_DEV, -1)[:, :n_small]
    c_all = small_all[:, :D]
    def full_rows(off, count):
        return jnp.stack([small_all[:, off + i * ds:off + (i + 1) * ds].reshape(D) for i in range(count)])

    W["norm_g"] = full_rows(D, DEPTH * 3).reshape(DEPTH, 3, D)
    W["conv_w"] = full_rows(D + DEPTH * 3 * ds, N_A_LAYERS * 3).reshape(N_A_LAYERS, 3, D)
    W["kv_norm_g"], W["final_norm_g"] = kv_norm_g, final_norm_g

    ada_b_mine = lax.dynamic_slice_in_dim(ada_b, me * nmod, nmod, axis=1).reshape(DEPTH, 1, nmod)
    kv_b_mine = lax.dynamic_slice_in_dim(kv_ada_b, me * nkv, nkv, axis=0).reshape(1, 1, nkv)
    mods_cols = mods_project(c_all, ada_w, ada_b_mine)
    kv_cols = mods_project(c_all, kv_ada_w.reshape(1, D, nkv), kv_b_mine)
    mcat = jnp.concatenate([mods_cols[l] for l in range(DEPTH)] + [kv_cols[0]], axis=1)
    wm = mcat.shape[1]
    if wm % LANES:
        mcat = jnp.concatenate([mcat, jnp.zeros((N_DEV, LANES - wm % LANES), F32)], axis=1)
    (mods_all,) = exchange_slots([mcat.reshape(N_DEV, 1, -1)], "exchange_mods")
    gather0 = comm.start_gather(0, mods_all)
    mods_all = mods_all.reshape(N_DEV, -1)
    mods = jnp.stack([mods_all[:, l * nmod:(l + 1) * nmod].reshape(N_MOD, D) for l in range(DEPTH)])
    kvmods = mods_all[:, DEPTH * nmod:DEPTH * nmod + nkv].reshape(2, D)

    loss_local, dx, grads = device_step(x[0], positions[0], loss_target[0], mods, kvmods, W, comm, gather0)
    loss = lax.psum(loss_local, MESH_AXES)

    (vec_all,) = comm_wait(grads["vector_gather"]["started"], grads["exchanges"][-1][1]["token"], "vector_grads_wait")
    vec_all = vec_all.reshape(N_DEV, -1)[:, :grads["vector_gather"]["count"]]
    nm_, nk_ = DEPTH * N_MOD * D, 2 * D
    dmods_all = vec_all[:, :nm_].reshape(N_DEV, DEPTH, N_MOD * D)
    dkvm_all = vec_all[:, nm_:nm_ + nk_]
    rest = vec_all[:, nm_ + nk_:]
    parts_kv_norm, parts_final = rest[:, :D].reshape(N_DEV, 1, D), rest[:, D:2 * D].reshape(N_DEV, 1, D)
    parts_norm = lax.dynamic_slice_in_dim(rest[:, 2 * D:2 * D + DEPTH * 3 * D].reshape(N_DEV, DEPTH * 3, D), me * ds, ds, axis=2)
    parts_conv = lax.dynamic_slice_in_dim(rest[:, 2 * D + DEPTH * 3 * D:].reshape(N_DEV, N_A_LAYERS * 3, D), me * ds, ds, axis=2)
    dm_cols = lax.dynamic_slice_in_dim(dmods_all, me * nmod, nmod, axis=2)
    dm_mine = jnp.stack([dm_cols[:, l] for l in range(DEPTH)])
    dkv_mine = lax.dynamic_slice_in_dim(dkvm_all, me * nkv, nkv, axis=1).reshape(1, N_DEV, nkv)
    g_ada_w = mods_weight_grad(c_all, dm_mine)
    g_kv_ada_w = mods_weight_grad(c_all, dkv_mine)[0]

    out_g, out_d, out_m, out_v = {}, {}, {}, {}

    def update(n, g, w, parts=False):
        shp = w.shape
        w2 = w.reshape(1, -1) if w.ndim == 1 else _flat2(w)
        g2 = g if parts else g.reshape(w2.shape)
        res = adam_update(g2, w2, mom[n].reshape(w2.shape), var[n].reshape(w2.shape), parts, "adam_" + n)
        out_g[n], out_d[n], out_m[n], out_v[n] = (r.reshape(shp) for r in res)

    moms = {n: stacked(mom[n], n) for n in _BIG}
    vars_ = {n: stacked(var[n], n) for n in _BIG}
    results = {}
    after = dx
    for ci, started in grads["exchanges"]:
        for (n, layer), parts in comm.finish_exchange(ci, started, after).items():
            idx = stacked_index(n, layer)
            results[n] = adam_layer(parts, comm.shards[n], moms[n], vars_[n], results.get(n), 0 if idx is None else idx,
                                    after, f"adam_{n}_{layer}")
            after = results[n][1]
    for n in _BIG:
        res = [jnp.swapaxes(r, 1, 2) if n in _TRANSPOSED else r for r in results[n]]
        out_g[n], out_d[n], out_m[n], out_v[n] = (r.reshape(wts[n].shape) for r in res)
    update("ada_w", g_ada_w, ada_w)
    update("kv_ada_w", g_kv_ada_w, kv_ada_w)
    update("ada_b", dmods_all, ada_b, True)
    update("kv_ada_b", dkvm_all.reshape(N_DEV, 1, nk_), kv_ada_b, True)
    update("kv_norm_g", parts_kv_norm, kv_norm_g, True)
    update("final_norm_g", parts_final, final_norm_g, True)
    update("norm_g", parts_norm, norm_g, True)
    update("conv_w", parts_conv, conv_w, True)

    return (loss, dx.reshape(x.shape), *[out_g[n] for n in names], *[out_d[n] for n in names],
            *[out_m[n] for n in names], *[out_v[n] for n in names])
```

```python
import functools

import jax
import jax.numpy as jnp
from jax import lax
from jax.experimental import pallas as pl
from jax.experimental.pallas import tpu as pltpu

F32, BF16 = jnp.float32, jnp.bfloat16

N_DEV = 8
MESH_AXES = ("x", "y", "c")
DEPTH = 4
N_A_LAYERS = 2
HEAD_DIM = 64
HEADS_PER_GROUP = 8
GROUP_WIDTH = HEAD_DIM * HEADS_PER_GROUP
DILATED_GROUPS = ((128, 1), (512, 4), (2048, 16))
ROPE_DIM = HEAD_DIM // 4
ROPE_THETA = 500000.0
NORM_EPS = 1e-5
FFN_RES_WEIGHT = 0.5
N_MOD = 9
ADAM_LR, ADAM_B1, ADAM_B2, ADAM_EPS, ADAM_WD, ADAM_STEP = 0.001, 0.9, 0.999, 1e-08, 0.01, 10

LANES = 128
TOKEN_TILE = 512
FFN_BWD_TILE = 256
CONTRACT_TILE = 4096
GRAD_COLS = 768
MXU_WIDTH = 256
VMEM_LIMIT = 56 * 1024 * 1024
MESH = pl.DeviceIdType.MESH


def _cp(*sem):
    return pltpu.CompilerParams(dimension_semantics=sem, vmem_limit_bytes=VMEM_LIMIT)


def _pick(n, cap, mult=LANES):
    if n <= cap:
        return n
    best = None
    for t in range(mult, cap + 1, mult):
        if n % t == 0:
            best = t
    assert best is not None, (n, cap)
    return best


def _tok(tm, w):
    return pl.BlockSpec((tm, w), lambda i: (i, 0))


def _res(shape):
    nd = len(shape)
    return pl.BlockSpec(shape, lambda *_: (0,) * nd, pipeline_mode=pl.Buffered(1))


def _sds(shape, dt):
    return jax.ShapeDtypeStruct(shape, dt)


def _sigmoid(a):
    return 1.0 / (1.0 + jnp.exp(-a))


def _modnorm(x, g, sh, sc):
    r = lax.rsqrt(jnp.mean(x * x, axis=-1, keepdims=True) + NORM_EPS)
    return (x * r * g) * (1.0 + sc) + sh


def _dot(a, b):
    return jnp.dot(a, b, preferred_element_type=F32)


def _dot_nt(a, b):
    return lax.dot_general(a, b, (((1,), (1,)), ((), ())), preferred_element_type=F32)


def _dot_tn(a, b):
    return lax.dot_general(a, b, (((0,), (0,)), ((), ())), preferred_element_type=F32)


def _rows8(rows, d):
    pad = 8 - len(rows)
    return jnp.concatenate(list(rows) + [jnp.zeros((pad, d), F32)], axis=0)


def _acc_rows(ref, tile, first):
    @pl.when(first)
    def _():
        ref[...] = tile

    @pl.when(jnp.logical_not(first))
    def _():
        ref[...] += tile


def ffn_fwd(x, vec, w_in_t, w_out):
    T, D = x.shape
    F = w_in_t.shape[0] // 2
    tm, cw = min(TOKEN_TILE, T), _pick(F, MXU_WIDTH)

    def body(x_ref, vec_ref, wi_ref, wo_ref, xn_ref, h_ref, ga_ref, gb_ref, u_ref, y_ref):
        x_t = x_ref[...]
        hb = _modnorm(x_t, vec_ref[0:1], vec_ref[1:2], vec_ref[2:3]).astype(BF16)
        h_ref[...] = hb
        for c in range(F // cw):
            lo, hi = c * cw, (c + 1) * cw
            a = _dot_nt(hb, wi_ref[lo:hi, :])
            b = _dot_nt(hb, wi_ref[F + lo:F + hi, :])
            sg = _sigmoid(a)
            silu = a * sg
            ga_ref[:, lo:hi] = (b * (sg + silu * (1.0 - sg))).astype(BF16)
            gb_ref[:, lo:hi] = silu.astype(BF16)
            u_ref[:, lo:hi] = (silu * b).astype(BF16)
        y = _dot(u_ref[...], wo_ref[...])
        y_ref[...] = y.astype(BF16)
        xn_ref[...] = x_t + (FFN_RES_WEIGHT * (1.0 + vec_ref[3:4])) * y

    return pl.pallas_call(
        body, grid=(T // tm,),
        in_specs=[_tok(tm, D), _res((8, D)), _res((2 * F, D)), _res((F, D))],
        out_specs=[_tok(tm, D), _tok(tm, D), _tok(tm, F), _tok(tm, F), _tok(tm, F), _tok(tm, D)],
        out_shape=[_sds((T, D), F32), _sds((T, D), BF16), _sds((T, F), BF16), _sds((T, F), BF16), _sds((T, F), BF16),
                   _sds((T, D), BF16)],
        compiler_params=_cp("arbitrary"), name="ffn_fwd")(x, vec, w_in_t, w_out)


def ffn_bwd(dxo, y, vec, w_out, w_in_t, a, b, x):
    T, D = x.shape
    F = a.shape[1]
    tm, cw = min(FFN_BWD_TILE, T), _pick(F, MXU_WIDTH)

    def body(dxo_ref, y_ref, vec_ref, wo_ref, wi_ref, a_ref, b_ref, x_ref, dy_ref, dab_ref, dx_ref, part_ref):
        dxo_t = dxo_ref[...]
        dyb = (dxo_t * (FFN_RES_WEIGHT * (1.0 + vec_ref[3:4]))).astype(BF16)
        dy_ref[...] = dyb
        dgate = FFN_RES_WEIGHT * jnp.sum(dxo_t * y_ref[...].astype(F32), axis=0, keepdims=True)
        for c in range(F // cw):
            lo, hi = c * cw, (c + 1) * cw
            du = _dot_nt(dyb, wo_ref[lo:hi, :])
            dab_ref[:, lo:hi] = (du * a_ref[:, lo:hi].astype(F32)).astype(BF16)
            dab_ref[:, F + lo:F + hi] = (du * b_ref[:, lo:hi].astype(F32)).astype(BF16)
        dh = _dot(dab_ref[...], wi_ref[...])
        _, vjp = jax.vjp(_modnorm, x_ref[...], vec_ref[0:1], vec_ref[1:2], vec_ref[2:3])
        dx, dg, dsh, dsc = vjp(dh)
        dx_ref[...] = dxo_t + dx
        _acc_rows(part_ref, _rows8([dg, dsh, dsc, dgate], D), pl.program_id(0) == 0)

    return pl.pallas_call(
        body, grid=(T // tm,),
        in_specs=[_tok(tm, D), _tok(tm, D), _res((8, D)), _res((F, D)), _res((2 * F, D)), _tok(tm, F), _tok(tm, F), _tok(tm, D)],
        out_specs=[_tok(tm, D), _tok(tm, 2 * F), _tok(tm, D), pl.BlockSpec((8, D), lambda i: (0, 0))],
        out_shape=[_sds((T, D), BF16), _sds((T, 2 * F), BF16), _sds((T, D), F32), _sds((8, D), F32)],
        compiler_params=_cp("arbitrary"), name="ffn_bwd")(dxo, y, vec, w_out, w_in_t, a, b, x)


def grad_slots(a, b, name, col_slots=False, after=None):
    T, M = a.shape
    extra = [] if after is None else [after]
    N = b.shape[1]
    tk = min(CONTRACT_TILE, T)
    nk = T // tk
    tmm = _pick(M, 1408)
    if col_slots:
        ns = N // N_DEV
        sp = max(s for s in (1, 2, 4, 8) if ns * s <= GRAD_COLS or s == 1)
        tn = ns * sp
    else:
        tn = _pick(N, GRAD_COLS)

    def body(a_ref, b_ref, *rest):
        o_ref, acc = rest[-2:]
        k = pl.program_id(2)
        t = _dot_tn(a_ref[...], b_ref[...])

        @pl.when(k == 0)
        def _():
            acc[...] = t

        @pl.when(k > 0)
        def _():
            acc[...] += t

        @pl.when(k == nk - 1)
        def _():
            if col_slots:
                for s in range(sp):
                    o_ref[s] = acc[:, s * ns:(s + 1) * ns].astype(BF16)
            else:
                o_ref[...] = acc[...].astype(BF16)

    if col_slots:
        out_spec, out_shape = pl.BlockSpec((sp, tmm, ns), lambda i, j, k: (j, i, 0)), _sds((N_DEV, M, ns), BF16)
    else:
        out_spec, out_shape = pl.BlockSpec((tmm, tn), lambda i, j, k: (i, j)), _sds((M, N), BF16)
    out = pl.pallas_call(
        body, grid=(M // tmm, N // tn, nk),
        in_specs=[pl.BlockSpec((tk, tmm), lambda i, j, k: (k, i)), pl.BlockSpec((tk, tn), lambda i, j, k: (k, j))]
        + [pl.BlockSpec(memory_space=pl.ANY)] * len(extra),
        out_specs=out_spec, out_shape=out_shape,
        scratch_shapes=[pltpu.VMEM((tmm, tn), F32)],
        compiler_params=_cp("arbitrary", "arbitrary", "arbitrary"), name=name)(a, b, *extra)
    return out if col_slots else out.reshape(N_DEV, M // N_DEV, N)


def conv_fwd(x, vec, cw, w_in, w_out):
    T, D = x.shape
    tm = min(TOKEN_TILE, T)

    def body(x_ref, vec_ref, cw_ref, wi_ref, wo_ref, xn_ref, h_ref, bcu_ref, cv_ref, z_ref, y_ref, vbuf):
        @pl.when(pl.program_id(0) == 0)
        def _():
            vbuf[0:8, :] = jnp.zeros((8, D), F32)

        x_t = x_ref[...]
        hb = _modnorm(x_t, vec_ref[0:1], vec_ref[1:2], vec_ref[2:3]).astype(BF16)
        h_ref[...] = hb
        bcu = _dot(hb, wi_ref[...])
        bcu_ref[...] = bcu.astype(BF16)
        bg, v = bcu[:, 0:D], bcu[:, D:2 * D] * bcu[:, 2 * D:3 * D]
        vbuf[8:8 + tm, :] = v
        conv = cw_ref[0:1] * vbuf[6:6 + tm, :] + cw_ref[1:2] * vbuf[7:7 + tm, :] + cw_ref[2:3] * v
        cv_ref[...] = conv.astype(BF16)
        zb = (bg * conv).astype(BF16)
        z_ref[...] = zb
        y = _dot(zb, wo_ref[...])
        y_ref[...] = y.astype(BF16)
        xn_ref[...] = x_t + (1.0 + vec_ref[3:4]) * y
        vbuf[0:8, :] = vbuf[tm:tm + 8, :]

    return pl.pallas_call(
        body, grid=(T // tm,),
        in_specs=[_tok(tm, D), _res((8, D)), _res((8, D)), _res((D, 3 * D)), _res((D, D))],
        out_specs=[_tok(tm, D), _tok(tm, D), _tok(tm, 3 * D), _tok(tm, D), _tok(tm, D), _tok(tm, D)],
        out_shape=[_sds((T, D), F32), _sds((T, D), BF16), _sds((T, 3 * D), BF16), _sds((T, D), BF16),
                   _sds((T, D), BF16), _sds((T, D), BF16)],
        scratch_shapes=[pltpu.VMEM((tm + 8, D), F32)],
        compiler_params=_cp("arbitrary"), name="conv_fwd")(x, vec, cw, w_in, w_out)


def conv_bwd(dxo, x, y, bcu, cv, vec, cw, w_in, w_out):
    T, D = x.shape
    tm = min(TOKEN_TILE, T)
    nt = T // tm

    def body(dxo_ref, x_ref, y_ref, bcu_ref, cv_ref, vec_ref, cw_ref, wi_ref, wo_ref,
             dx_ref, dy_ref, dbcu_ref, part_ref, dcw_ref, dcbuf):
        first = pl.program_id(0) == 0

        @pl.when(first)
        def _():
            dcbuf[tm:tm + 8, :] = jnp.zeros((8, D), F32)

        dxo_t = dxo_ref[...]
        dyb = (dxo_t * (1.0 + vec_ref[3:4])).astype(BF16)
        dy_ref[...] = dyb
        dgate = jnp.sum(dxo_t * y_ref[...].astype(F32), axis=0, keepdims=True)
        dz = _dot_nt(dyb, wo_ref[...])
        bcu_t = bcu_ref[...].astype(F32)
        bg, cg, ug = bcu_t[:, 0:D], bcu_t[:, D:2 * D], bcu_t[:, 2 * D:3 * D]
        dconv = dz * bg
        dbg = dz * cv_ref[...].astype(F32)
        dcbuf[0:tm, :] = dconv
        d1, d2 = dcbuf[1:tm + 1, :], dcbuf[2:tm + 2, :]
        dv = cw_ref[2:3] * dconv + cw_ref[1:2] * d1 + cw_ref[0:1] * d2
        v = cg * ug
        dcw = _rows8([jnp.sum(d2 * v, axis=0, keepdims=True), jnp.sum(d1 * v, axis=0, keepdims=True),
                      jnp.sum(dconv * v, axis=0, keepdims=True)], D)
        dbcu = jnp.concatenate([dbg, dv * ug, dv * cg], axis=1).astype(BF16)
        dbcu_ref[...] = dbcu
        dh = _dot_nt(dbcu, wi_ref[...])
        _, vjp = jax.vjp(_modnorm, x_ref[...], vec_ref[0:1], vec_ref[1:2], vec_ref[2:3])
        dx, dg, dsh, dsc = vjp(dh)
        dx_ref[...] = dxo_t + dx
        _acc_rows(part_ref, _rows8([dg, dsh, dsc, dgate], D), first)
        _acc_rows(dcw_ref, dcw, first)
        dcbuf[tm:tm + 8, :] = dcbuf[0:8, :]

    def rev(w):
        return pl.BlockSpec((tm, w), lambda i: (nt - 1 - i, 0))

    return pl.pallas_call(
        body, grid=(nt,),
        in_specs=[rev(D), rev(D), rev(D), rev(3 * D), rev(D), _res((8, D)), _res((8, D)), _res((D, 3 * D)), _res((D, D))],
        out_specs=[rev(D), rev(D), rev(3 * D), pl.BlockSpec((8, D), lambda i: (0, 0)), pl.BlockSpec((8, D), lambda i: (0, 0))],
        out_shape=[_sds((T, D), F32), _sds((T, D), BF16), _sds((T, 3 * D), BF16), _sds((8, D), F32), _sds((8, D), F32)],
        scratch_shapes=[pltpu.VMEM((tm + 8, D), F32)],
        compiler_params=_cp("arbitrary"), name="conv_bwd")(dxo, x, y, bcu, cv, vec, cw, w_in, w_out)


def rope_tables(pos, lane_rows):
    T = pos.shape[0]
    tm = min(TOKEN_TILE, T)

    def body(p_ref, lr_ref, c_ref, sp_ref, sm_ref):
        ang = p_ref[...].astype(F32) * lr_ref[0:1]
        cs, sn = jnp.cos(ang), jnp.sin(ang)
        c_ref[...] = jnp.where(lr_ref[1:2] > 0.5, cs, 1.0)
        sp_ref[...] = jnp.where(lr_ref[2:3] > 0.5, sn, 0.0)
        sm_ref[...] = jnp.where(lr_ref[3:4] > 0.5, -sn, 0.0)

    return pl.pallas_call(
        body, grid=(T // tm,),
        in_specs=[_tok(tm, 1), _res((8, LANES))],
        out_specs=[_tok(tm, LANES)] * 3,
        out_shape=[_sds((T, LANES), F32)] * 3,
        compiler_params=_cp("arbitrary"), name="rope_tables")(pos, lane_rows)


def _rope(t, c, sp, sm):
    w = t.shape[1]
    reps = w // LANES
    cf, spf, smf = jnp.tile(c, (1, reps)), jnp.tile(sp, (1, reps)), jnp.tile(sm, (1, reps))
    half = ROPE_DIM // 2
    return t * cf + pltpu.roll(t, half, axis=1) * spf + pltpu.roll(t, w - half, axis=1) * smf


def _rope_t(d, c, sp, sm):
    w = d.shape[1]
    reps = w // LANES
    cf, spf, smf = jnp.tile(c, (1, reps)), jnp.tile(sp, (1, reps)), jnp.tile(sm, (1, reps))
    half = ROPE_DIM // 2
    return d * cf + pltpu.roll(d * spf, w - half, axis=1) + pltpu.roll(d * smf, half, axis=1)


def _split_residues(v, d, stage):
    tm, width = v.shape
    if d == 1:
        return [v]
    nj = width // LANES
    for j in range(nj):
        stage[j] = v[:, j * LANES:(j + 1) * LANES]
    return [jnp.concatenate([stage[j, pl.ds(r, tm // d, stride=d), :] for j in range(nj)], axis=1) for r in range(d)]


def _merge_residues(piece, d, tm, width, stage):
    if d == 1:
        return piece(0)
    nj = width // LANES
    for r in range(d):
        p = piece(r)
        for j in range(nj):
            stage[j, pl.ds(r, tm // d, stride=d), :] = p[:, j * LANES:(j + 1) * LANES]
    return jnp.concatenate([stage[j] for j in range(nj)], axis=1)


def _residue_spec(d, tm, width=GROUP_WIDTH):
    return pl.BlockSpec((d, tm // d, width), lambda i: (0, i, 0))


def _stage_scratch(tm):
    return pltpu.VMEM((GROUP_WIDTH // LANES, tm, LANES), F32)


def proj_rope_fwd(x, vec, w, tabs, n_rope, transposed, dils, name):
    T, D = x.shape
    N = w.shape[0] if transposed else w.shape[1]
    tm = min(TOKEN_TILE, T)
    GW = GROUP_WIDTH
    piece_dils = [dils[j % len(dils)] for j in range(N // GW)]

    def body(x_ref, vec_ref, w_ref, c_ref, sp_ref, sm_ref, h_ref, *rest):
        out_refs, stage = rest[:-1], rest[-1]
        hb = _modnorm(x_ref[...], vec_ref[0:1], vec_ref[1:2], vec_ref[2:3]).astype(BF16)
        h_ref[...] = hb
        p = _dot_nt(hb, w_ref[...]) if transposed else _dot(hb, w_ref[...])
        pr = _rope(p[:, 0:n_rope], c_ref[...], sp_ref[...], sm_ref[...])
        for j, d in enumerate(piece_dils):
            src = pr if (j + 1) * GW <= n_rope else p
            for r, rows in enumerate(_split_residues(src[:, j * GW:(j + 1) * GW], d, stage)):
                out_refs[j][r] = rows.astype(BF16)

    return pl.pallas_call(
        body, grid=(T // tm,),
        in_specs=[_tok(tm, D), _res((8, D)), _res(w.shape)] + [_tok(tm, LANES)] * 3,
        out_specs=[_tok(tm, D)] + [_residue_spec(d, tm) for d in piece_dils],
        out_shape=[_sds((T, D), BF16)] + [_sds((d, T // d, GW), BF16) for d in piece_dils],
        scratch_shapes=[_stage_scratch(tm)],
        compiler_params=_cp("arbitrary"), name=name)(x, vec, w, *tabs)


def proj_rope_bwd(dparts, dils, x, dxo, vec, w, tabs, n_rope, transposed, name):
    T, D = x.shape
    N = w.shape[0] if transposed else w.shape[1]
    tm = min(TOKEN_TILE, T)
    GW = GROUP_WIDTH
    npart = len(dparts)
    piece_dils = [dils[j % len(dils)] for j in range(npart)]

    def body(*refs):
        d_refs = refs[:npart]
        x_ref, dxo_ref, vec_ref, w_ref, c_ref, sp_ref, sm_ref, dx_ref, dp_ref, part_ref, stage = refs[npart:]
        d = jnp.concatenate([_merge_residues(lambda r, ref=ref: ref[r].astype(F32), dd, tm, GW, stage)
                             for ref, dd in zip(d_refs, piece_dils)], axis=1)
        dr = _rope_t(d[:, 0:n_rope], c_ref[...], sp_ref[...], sm_ref[...])
        if n_rope < N:
            dr = jnp.concatenate([dr, d[:, n_rope:N]], axis=1)
        dpb = dr.astype(BF16)
        dp_ref[...] = dpb
        dh = _dot(dpb, w_ref[...]) if transposed else _dot_nt(dpb, w_ref[...])
        _, vjp = jax.vjp(_modnorm, x_ref[...], vec_ref[0:1], vec_ref[1:2], vec_ref[2:3])
        dx, dg, dsh, dsc = vjp(dh)
        dx_ref[...] = dxo_ref[...] + dx
        _acc_rows(part_ref, _rows8([dg, dsh, dsc], D), pl.program_id(0) == 0)

    return pl.pallas_call(
        body, grid=(T // tm,),
        in_specs=[_residue_spec(d, tm) for d in piece_dils] + [_tok(tm, D), _tok(tm, D), _res((8, D)), _res(w.shape)]
        + [_tok(tm, LANES)] * 3,
        out_specs=[_tok(tm, D), _tok(tm, N), pl.BlockSpec((8, D), lambda i: (0, 0))],
        out_shape=[_sds((T, D), F32), _sds((T, N), BF16), _sds((8, D), F32)],
        scratch_shapes=[_stage_scratch(tm)],
        compiler_params=_cp("arbitrary"), name=name)(*dparts, x, dxo, vec, w, *tabs)


def _valid_mask(n, i):
    qi = lax.broadcasted_iota(jnp.int32, (n, 2 * n), 0)
    kj = lax.broadcasted_iota(jnp.int32, (n, 2 * n), 1)
    dist = n + qi - kj
    return (dist >= 0) & (dist <= n) & ((kj >= n) | (i > 0))


def _band_specs(n):
    two = pl.BlockSpec((None, 2 * n, GROUP_WIDTH), lambda r, i: (r, i, 0))
    prv = pl.BlockSpec((None, n, GROUP_WIDTH), lambda r, i: (r, jnp.maximum(2 * i - 1, 0), 0))
    one = pl.BlockSpec((None, n, GROUP_WIDTH), lambda r, i: (r, i, 0))
    return two, prv, one


def _pair_keys(prev_ref, two_ref, ps, n):
    cur2 = two_ref[:, ps]
    return jnp.concatenate([prev_ref[:, ps], cur2[0:n]], axis=0), cur2


STAT_STRIDE = LANES // HEADS_PER_GROUP


def _head_of_lane():
    return lax.broadcasted_iota(jnp.int32, (1, LANES), 1) // STAT_STRIDE


def attn_core_fwd(q, k, v, g, n):
    d, M, GW = q.shape
    scale = HEAD_DIM ** -0.5

    def body(q_ref, kp_ref, kc_ref, vp_ref, vc_ref, o_ref, l_ref):
        masks = (_valid_mask(n, pl.program_id(1)), _valid_mask(n, 1))
        first = lax.broadcasted_iota(jnp.int32, (1, LANES), 1) < HEAD_DIM
        head_of_lane = _head_of_lane()
        lse = [jnp.zeros((n, LANES), F32), jnp.zeros((n, LANES), F32)]
        for pair in range(HEADS_PER_GROUP * HEAD_DIM // LANES):
            ps = slice(LANES * pair, LANES * (pair + 1))
            keys, vals = _pair_keys(kp_ref, kc_ref, ps, n), _pair_keys(vp_ref, vc_ref, ps, n)
            for blk in range(2):
                rows = slice(blk * n, (blk + 1) * n)
                q2 = q_ref[rows, ps]
                o2, l2 = [], []
                for sel in (first, jnp.logical_not(first)):
                    s = jnp.where(masks[blk], _dot_nt(jnp.where(sel, q2, jnp.zeros_like(q2)), keys[blk]) * scale, -1e30)
                    m = jnp.max(s, axis=1, keepdims=True)
                    p = jnp.exp(s - m)
                    den = jnp.sum(p, axis=1, keepdims=True)
                    o2.append(_dot((p / den).astype(BF16), vals[blk]))
                    l2.append(m + jnp.log(den))
                o_ref[rows, ps] = jnp.where(first, o2[0], o2[1]).astype(BF16)
                for half in range(2):
                    lse[blk] = jnp.where(head_of_lane == 2 * pair + half, l2[half], lse[blk])
        for blk in range(2):
            l_ref[blk * n:(blk + 1) * n, :] = lse[blk]

    two, prv, _ = _band_specs(n)
    stat = pl.BlockSpec((None, 2 * n, LANES), lambda r, i: (r, i, 0))
    return pl.pallas_call(
        body, grid=(d, M // (2 * n)),
        in_specs=[two, prv, two, prv, two], out_specs=[two, stat],
        out_shape=[_sds((d, M, GW), BF16), _sds((d, M, LANES), F32)],
        compiler_params=_cp("arbitrary", "arbitrary"), name=f"attn_fwd_g{g}")(q, k, k, v, v)


def attn_core_bwd(q, k, v, do, delta, lse, g, n):
    d, M, GW = q.shape
    scale = HEAD_DIM ** -0.5

    def body(q_ref, kp_ref, kc_ref, vp_ref, vc_ref, do_ref, d_ref, l_ref, dq_ref, dkc_ref, dkp_ref, dvc_ref, dvp_ref):
        masks = (_valid_mask(n, pl.program_id(1)), _valid_mask(n, 1))
        first = lax.broadcasted_iota(jnp.int32, (1, LANES), 1) < HEAD_DIM
        for pair in range(HEADS_PER_GROUP * HEAD_DIM // LANES):
            ps = slice(LANES * pair, LANES * (pair + 1))
            keys, vals = _pair_keys(kp_ref, kc_ref, ps, n), _pair_keys(vp_ref, vc_ref, ps, n)
            own = []
            for blk in range(2):
                rows = slice(blk * n, (blk + 1) * n)
                q2, do2 = q_ref[rows, ps], do_ref[rows, ps]
                dq2, dk, dv = [], None, None
                for half, sel in enumerate((first, jnp.logical_not(first))):
                    qm = jnp.where(sel, q2, jnp.zeros_like(q2))
                    dom = jnp.where(sel, do2, jnp.zeros_like(do2))
                    s = jnp.where(masks[blk], _dot_nt(qm, keys[blk]) * scale, -1e30)
                    lane0 = STAT_STRIDE * (2 * pair + half)
                    p = jnp.exp(s - l_ref[rows, lane0:lane0 + 1])
                    dp = _dot_nt(dom, vals[blk])
                    ds = (p * (dp - d_ref[rows, lane0:lane0 + 1]) * scale).astype(BF16)
                    dq2.append(_dot(ds, keys[blk]))
                    dkh = _dot_tn(ds, qm)
                    dvh = _dot_tn(p.astype(BF16), dom)
                    dk = dkh if dk is None else dk + dkh
                    dv = dvh if dv is None else dv + dvh
                dq_ref[rows, ps] = jnp.where(first, dq2[0], dq2[1]).astype(BF16)
                own.append((dk, dv))
            for t, (c_ref, p_ref) in enumerate(((dkc_ref, dkp_ref), (dvc_ref, dvp_ref))):
                a, b = own[0][t], own[1][t]
                p_ref[:, ps] = a[0:n].astype(BF16)
                c_ref[0:n, ps] = (a[n:2 * n] + b[0:n]).astype(BF16)
                c_ref[n:2 * n, ps] = b[n:2 * n].astype(BF16)

    two, prv, one = _band_specs(n)
    stat = pl.BlockSpec((None, 2 * n, LANES), lambda r, i: (r, i, 0))
    return pl.pallas_call(
        body, grid=(d, M // (2 * n)),
        in_specs=[two, prv, two, prv, two, two, stat, stat], out_specs=[two, two, one, two, one],
        out_shape=[_sds((d, M, GW), BF16), _sds((d, M, GW), BF16), _sds((d, M // 2, GW), BF16),
                   _sds((d, M, GW), BF16), _sds((d, M // 2, GW), BF16)],
        compiler_params=_cp("arbitrary", "arbitrary"), name=f"attn_bwd_g{g}")(q, k, k, v, v, do, delta, lse)


def dkv_combine(cur_prev, n, name):
    d, M, GW = cur_prev[0][0].shape
    rows = min(M, 1024)
    pairs = rows // (2 * n)
    steps = M // rows
    flat = [a for pair in cur_prev for a in pair]

    def body(*refs):
        o_ref = refs[-1]
        last = pl.program_id(1) == steps - 1
        acc = None
        shifted = None
        for t in range(0, len(refs) - 1, 3):
            c = refs[t][...].astype(F32)
            nxt = jnp.where(last, 0.0, refs[t + 2][...].astype(F32))
            s = nxt if pairs == 1 else jnp.concatenate([refs[t + 1][n:pairs * n, :].astype(F32), nxt], axis=0)
            acc = c if acc is None else acc + c
            shifted = s if shifted is None else shifted + s
        for m in range(pairs):
            lo = 2 * m * n
            o_ref[lo:lo + n, :] = acc[lo:lo + n].astype(BF16)
            o_ref[lo + n:lo + 2 * n, :] = (acc[lo + n:lo + 2 * n] + shifted[m * n:(m + 1) * n]).astype(BF16)

    cur = pl.BlockSpec((None, rows, GW), lambda r, i: (r, i, 0))
    same = pl.BlockSpec((None, pairs * n, GW), lambda r, i: (r, i, 0))
    nxt = pl.BlockSpec((None, n, GW), lambda r, i: (r, jnp.minimum((i + 1) * pairs, M // (2 * n) - 1), 0))
    args = []
    for c, p in cur_prev:
        args += [c, p, p]
    return pl.pallas_call(
        body, grid=(d, steps), in_specs=[cur, same, nxt] * len(cur_prev), out_specs=cur,
        out_shape=_sds((d, M, GW), BF16),
        compiler_params=_cp("arbitrary", "arbitrary"), name=name)(*args)


def _group_weights(ls):
    mx = functools.reduce(jnp.maximum, ls)
    es = [jnp.exp(l - mx) for l in ls]
    tot = functools.reduce(lambda a, b: a + b, es)
    return [e / tot for e in es]


def _expand_heads(w):
    tm = w.shape[0]
    first = lax.broadcasted_iota(jnp.int32, (1, LANES), 1) < HEAD_DIM
    cols = [jnp.broadcast_to(w[:, STAT_STRIDE * h:STAT_STRIDE * h + 1], (tm, LANES)) for h in range(HEADS_PER_GROUP)]
    return jnp.concatenate([jnp.where(first, cols[2 * p], cols[2 * p + 1]) for p in range(HEADS_PER_GROUP // 2)], axis=1)


def _head_sums(r):
    head_of_lane = _head_of_lane()
    out = jnp.zeros((r.shape[0], LANES), F32)
    for h in range(HEADS_PER_GROUP):
        s = jnp.sum(r[:, HEAD_DIM * h:HEAD_DIM * (h + 1)], axis=1, keepdims=True)
        out = jnp.where(head_of_lane == h, s, out)
    return out


def _mix_weights(l_refs, dils, tm, stage):
    ls = [_merge_residues(lambda r, ref=ref: ref[r], d, tm, LANES, stage) for ref, d in zip(l_refs, dils)]
    return [_expand_heads(w) for w in _group_weights(ls)]


def attn_mix_out(os_, ls, dils, x, vec, w_o):
    T, D = x.shape
    GW = GROUP_WIDTH
    tm = min(TOKEN_TILE, T)
    ng = len(os_)

    def body(*refs):
        o_refs, l_refs = refs[:ng], refs[ng:2 * ng]
        x_ref, vec_ref, w_ref, xn_ref, mix_ref, y_ref, stage = refs[2 * ng:]
        natural = lambda ref, d: _merge_residues(lambda r: ref[r].astype(F32), d, tm, GW, stage)
        ws = _mix_weights(l_refs, dils, tm, stage)
        mixed = functools.reduce(lambda a, b: a + b, [w * natural(r, d) for w, r, d in zip(ws, o_refs, dils)])
        mb = mixed.astype(BF16)
        mix_ref[...] = mb
        y = _dot(mb, w_ref[...])
        y_ref[...] = y.astype(BF16)
        xn_ref[...] = x_ref[...] + (1.0 + vec_ref[3:4]) * y

    res = [_residue_spec(d, tm) for d in dils]
    stat = [_residue_spec(d, tm, LANES) for d in dils]
    return pl.pallas_call(
        body, grid=(T // tm,),
        in_specs=res + stat + [_tok(tm, D), _res((8, D)), _res((GW, D))],
        out_specs=[_tok(tm, D), _tok(tm, GW), _tok(tm, D)],
        out_shape=[_sds((T, D), F32), _sds((T, GW), BF16), _sds((T, D), BF16)],
        scratch_shapes=[_stage_scratch(tm)],
        compiler_params=_cp("arbitrary"), name="attn_mix_out")(*os_, *ls, x, vec, w_o)


def attn_mix_bwd(dxo, y, vec, w_o, os_, ls, dils):
    T, D = dxo.shape
    GW = GROUP_WIDTH
    tm = min(TOKEN_TILE, T)
    ng = len(os_)

    def body(*refs):
        dxo_ref, y_ref, vec_ref, w_ref = refs[:4]
        o_refs, l_refs = refs[4:4 + ng], refs[4 + ng:4 + 2 * ng]
        dy_ref = refs[4 + 2 * ng]
        do_refs = refs[5 + 2 * ng:5 + 3 * ng]
        d_refs = refs[5 + 3 * ng:5 + 4 * ng]
        part_ref, stage = refs[5 + 4 * ng], refs[6 + 4 * ng]
        natural = lambda ref, d: _merge_residues(lambda r: ref[r].astype(F32), d, tm, GW, stage)
        dxo_t = dxo_ref[...]
        dyb = (dxo_t * (1.0 + vec_ref[3:4])).astype(BF16)
        dy_ref[...] = dyb
        dgate = jnp.sum(dxo_t * y_ref[...].astype(F32), axis=0, keepdims=True)
        _acc_rows(part_ref, _rows8([dgate], D), pl.program_id(0) == 0)
        dmix = _dot_nt(dyb, w_ref[...])
        ws = _mix_weights(l_refs, dils, tm, stage)
        mixed = functools.reduce(lambda a, b: a + b, [w * natural(r, d) for w, r, d in zip(ws, o_refs, dils)])
        for gi in range(ng):
            do = ws[gi] * dmix
            for r, rows in enumerate(_split_residues(do, dils[gi], stage)):
                do_refs[gi][r] = rows.astype(BF16)
            for r, rows in enumerate(_split_residues(_head_sums(do * mixed), dils[gi], stage)):
                d_refs[gi][r] = rows

    res = [_residue_spec(d, tm) for d in dils]
    stat = [_residue_spec(d, tm, LANES) for d in dils]
    return pl.pallas_call(
        body, grid=(T // tm,),
        in_specs=[_tok(tm, D), _tok(tm, D), _res((8, D)), _res((GW, D))] + res + stat,
        out_specs=[_tok(tm, D)] + res + stat + [pl.BlockSpec((8, D), lambda i: (0, 0))],
        out_shape=[_sds((T, D), BF16)] + [_sds((d, T // d, GW), BF16) for d in dils]
        + [_sds((d, T // d, LANES), F32) for d in dils] + [_sds((8, D), F32)],
        scratch_shapes=[_stage_scratch(tm)],
        compiler_params=_cp("arbitrary"), name="attn_mix_bwd")(dxo, y, vec, w_o, *os_, *ls)


def final_loss(x, gvec, target):
    T, D = x.shape
    tm = min(TOKEN_TILE, T)

    def norm(xv, g):
        return xv * lax.rsqrt(jnp.mean(xv * xv, axis=-1, keepdims=True) + NORM_EPS) * g

    def body(x_ref, g_ref, t_ref, dx_ref, part_ref, loss_ref):
        first = pl.program_id(0) == 0
        yv, vjp = jax.vjp(norm, x_ref[...], g_ref[0:1])
        err = yv - t_ref[...]
        dx, dg = vjp(err * (1.0 / D))
        dx_ref[...] = dx
        _acc_rows(part_ref, _rows8([dg], D), first)
        tile_loss = 0.5 * jnp.sum(jnp.sum(err * err, axis=1, keepdims=True) * (1.0 / D), axis=0, keepdims=True)
        _acc_rows(loss_ref, jnp.broadcast_to(tile_loss, (8, LANES)), first)

    return pl.pallas_call(
        body, grid=(T // tm,),
        in_specs=[_tok(tm, D), _res((8, D)), _tok(tm, D)],
        out_specs=[_tok(tm, D), pl.BlockSpec((8, D), lambda i: (0, 0)), pl.BlockSpec((8, LANES), lambda i: (0, 0))],
        out_shape=[_sds((T, D), F32), _sds((8, D), F32), _sds((8, LANES), F32)],
        compiler_params=_cp("arbitrary"), name="final_loss")(x, gvec, target)


def mods_project(c_all, w, b):
    B, D = c_all.shape
    L, _, N = w.shape

    def body(c_ref, w_ref, b_ref, o_ref):
        cv = c_ref[...]
        cond = cv * _sigmoid(cv)
        o_ref[0] = jnp.dot(cond, w_ref[0], preferred_element_type=F32, precision=lax.Precision.HIGHEST) + b_ref[0]

    return pl.pallas_call(
        body, grid=(L,),
        in_specs=[pl.BlockSpec((B, D), lambda l: (0, 0)), pl.BlockSpec((1, D, N), lambda l: (l, 0, 0)),
                  pl.BlockSpec((1, 1, N), lambda l: (l, 0, 0))],
        out_specs=pl.BlockSpec((1, B, N), lambda l: (l, 0, 0)),
        out_shape=_sds((L, B, N), F32),
        compiler_params=_cp("arbitrary"), name="mods_project")(c_all, w, b)


def mods_weight_grad(c_all, dm):
    B, D = c_all.shape
    L, _, N = dm.shape

    def body(c_ref, d_ref, o_ref):
        cv = c_ref[...]
        cond = cv * _sigmoid(cv)
        o_ref[0] = lax.dot_general(cond, d_ref[0], (((0,), (0,)), ((), ())), preferred_element_type=F32,
                                   precision=lax.Precision.HIGHEST)

    return pl.pallas_call(
        body, grid=(L,),
        in_specs=[pl.BlockSpec((B, D), lambda l: (0, 0)), pl.BlockSpec((1, B, N), lambda l: (l, 0, 0))],
        out_specs=pl.BlockSpec((1, D, N), lambda l: (l, 0, 0)),
        out_shape=_sds((L, D, N), F32),
        compiler_params=_cp("arbitrary"), name="mods_weight_grad")(c_all, dm)


def _adam_math(g, w, m, v):
    m2 = ADAM_B1 * m + (1.0 - ADAM_B1) * g
    v2 = ADAM_B2 * v + (1.0 - ADAM_B2) * (g * g)
    m_hat = m2 / (1.0 - ADAM_B1 ** ADAM_STEP)
    v_hat = v2 / (1.0 - ADAM_B2 ** ADAM_STEP)
    delta = -ADAM_LR * (m_hat / (jnp.sqrt(v_hat) + ADAM_EPS) + ADAM_WD * w)
    return delta, m2, v2


def adam_update(g, w, m, v, parts, name):
    R, C = w.shape
    tr = _pick(R, 256, 8)

    def body(g_ref, w_ref, m_ref, v_ref, go_ref, d_ref, mo_ref, vo_ref):
        if parts:
            gv = g_ref[0].astype(F32)
            for s in range(1, N_DEV):
                gv = gv + g_ref[s].astype(F32)
        else:
            gv = g_ref[...]
        go_ref[...] = gv
        d_ref[...], mo_ref[...], vo_ref[...] = _adam_math(gv, w_ref[...], m_ref[...], v_ref[...])

    gspec = pl.BlockSpec((N_DEV, tr, C), lambda i: (0, i, 0)) if parts else _tok(tr, C)
    return pl.pallas_call(
        body, grid=(R // tr,),
        in_specs=[gspec, _tok(tr, C), _tok(tr, C), _tok(tr, C)],
        out_specs=[_tok(tr, C)] * 4, out_shape=[_sds((R, C), F32)] * 4,
        compiler_params=_cp("arbitrary"), name=name)(g, w, m, v)


def adam_layer(parts, w, m, v, prev, layer, after, name):
    L, R, C = w.shape
    tr = _pick(R, 256, 8)
    prev = (list(prev) if prev is not None else []) + [after]

    def body(p_ref, w_ref, m_ref, v_ref, *rest):
        go_ref, d_ref, mo_ref, vo_ref = rest[-4:]
        gv = p_ref[0].astype(F32)
        for s in range(1, N_DEV):
            gv = gv + p_ref[s].astype(F32)
        go_ref[...] = gv
        d_ref[...], mo_ref[...], vo_ref[...] = _adam_math(gv, w_ref[...], m_ref[...], v_ref[...])

    lay = pl.BlockSpec((None, tr, C), lambda i: (layer, i, 0))
    return pl.pallas_call(
        body, grid=(R // tr,),
        in_specs=[pl.BlockSpec((N_DEV, tr, C), lambda i: (0, i, 0)), lay, lay, lay] + [pl.BlockSpec(memory_space=pl.ANY)] * len(prev),
        out_specs=[lay] * 4, out_shape=[_sds((L, R, C), F32)] * 4,
        input_output_aliases={4 + k: k for k in range(len(prev) - 1)},
        compiler_params=_cp("arbitrary"), name=name)(parts, w, m, v, *prev)


def _my_id():
    return 4 * lax.axis_index("x") + 2 * lax.axis_index("y") + lax.axis_index("c")


def _peer(s):
    x, y, c = lax.axis_index("x"), lax.axis_index("y"), lax.axis_index("c")
    px = (1 - x) if s & 4 else x
    py = (1 - y) if s & 2 else y
    pc = (1 - c) if s & 1 else c
    return (px, py, pc), 4 * px + 2 * py + pc


def all_gather(xs, space, name):
    na = len(xs)

    def body(*refs):
        x_refs, o_refs = refs[:na], refs[na:2 * na]
        send_sems, recv_sems, local_sems = refs[2 * na:]
        me = _my_id()
        locals_, sends = [], []
        for a in range(na):
            cp = pltpu.make_async_copy(x_refs[a], o_refs[a].at[me], local_sems.at[a])
            cp.start()
            locals_.append(cp)
        for s in range(1, N_DEV):
            peer, _ = _peer(s)
            for a in range(na):
                cp = pltpu.make_async_remote_copy(
                    src_ref=x_refs[a], dst_ref=o_refs[a].at[me], send_sem=send_sems.at[a, s - 1],
                    recv_sem=recv_sems.at[a, s - 1], device_id=peer, device_id_type=MESH)
                cp.start()
                sends.append(cp)
        for s in range(1, N_DEV):
            peer, pid = _peer(s)
            for a in range(na):
                pltpu.make_async_remote_copy(
                    src_ref=x_refs[a], dst_ref=o_refs[a].at[pid], send_sem=send_sems.at[a, s - 1],
                    recv_sem=recv_sems.at[a, s - 1], device_id=peer, device_id_type=MESH).wait_recv()
        for cp in sends:
            cp.wait_send()
        for cp in locals_:
            cp.wait()

    spec = pl.BlockSpec(memory_space=space)
    return pl.pallas_call(
        body, in_specs=[spec] * na, out_specs=[spec] * na,
        out_shape=[_sds((N_DEV,) + x.shape, x.dtype) for x in xs],
        scratch_shapes=[pltpu.SemaphoreType.DMA((na, N_DEV - 1)), pltpu.SemaphoreType.DMA((na, N_DEV - 1)),
                        pltpu.SemaphoreType.DMA((na,))],
        compiler_params=pltpu.CompilerParams(vmem_limit_bytes=VMEM_LIMIT), name=name)(*xs)


def exchange_slots(xs, name):
    na = len(xs)

    def body(*refs):
        x_refs, o_refs = refs[:na], refs[na:2 * na]
        send_sems, recv_sems, local_sems = refs[2 * na:]
        me = _my_id()
        locals_, sends = [], []
        for a in range(na):
            cp = pltpu.make_async_copy(x_refs[a].at[me], o_refs[a].at[me], local_sems.at[a])
            cp.start()
            locals_.append(cp)
        for s in range(1, N_DEV):
            peer, pid = _peer(s)
            for a in range(na):
                cp = pltpu.make_async_remote_copy(
                    src_ref=x_refs[a].at[pid], dst_ref=o_refs[a].at[me], send_sem=send_sems.at[a, s - 1],
                    recv_sem=recv_sems.at[a, s - 1], device_id=peer, device_id_type=MESH)
                cp.start()
                sends.append(cp)
        for s in range(1, N_DEV):
            peer, pid = _peer(s)
            for a in range(na):
                pltpu.make_async_remote_copy(
                    src_ref=x_refs[a].at[pid], dst_ref=o_refs[a].at[pid], send_sem=send_sems.at[a, s - 1],
                    recv_sem=recv_sems.at[a, s - 1], device_id=peer, device_id_type=MESH).wait_recv()
        for cp in sends:
            cp.wait_send()
        for cp in locals_:
            cp.wait()

    spec = pl.BlockSpec(memory_space=pl.ANY)
    return pl.pallas_call(
        body, in_specs=[spec] * na, out_specs=[spec] * na,
        out_shape=[_sds(x.shape, x.dtype) for x in xs],
        scratch_shapes=[pltpu.SemaphoreType.DMA((na, N_DEV - 1)), pltpu.SemaphoreType.DMA((na, N_DEV - 1)),
                        pltpu.SemaphoreType.DMA((na,))],
        compiler_params=pltpu.CompilerParams(vmem_limit_bytes=VMEM_LIMIT), name=name)(*xs)


_HBM = pl.BlockSpec(memory_space=pltpu.HBM)
_SEM = pl.BlockSpec(memory_space=pltpu.SEMAPHORE)
_EFFECT = pltpu.SideEffectType.DATAFLOW_SIDE_EFFECTING


def _split_copies(pattern, x_ref, land_ref, send_sem, recv_sem):
    me = _my_id()
    if pattern in ("gather", "scatter"):
        plan = []
        for s in range(1, N_DEV):
            peer, pid = _peer(s)
            plan.append((x_ref.at[pid] if pattern == "scatter" else x_ref, land_ref.at[me], peer))
    elif pattern == "to_chips":
        plan = [(x_ref, land_ref.at[me], _peer(s)[0]) for s in (1, 2, 4, 6)]
    else:
        sibling = _peer(1)[0]
        plan = [(land_ref.at[_peer(s)[1]], land_ref.at[_peer(s)[1]], sibling) for s in (2, 4, 6)]
    return [pltpu.make_async_remote_copy(src_ref=src, dst_ref=dst, send_sem=send_sem, recv_sem=recv_sem,
                                         device_id=dev, device_id_type=MESH) for src, dst, dev in plan]


def comm_start(xs, pattern, after, name, lands=None):
    na = len(xs)
    extra = [] if after is None else [after]
    me = _my_id()
    if lands is None:
        lands = []
        for x in xs:
            shape = x.shape if pattern == "scatter" else (N_DEV,) + x.shape
            own = lax.dynamic_slice_in_dim(x, me, 1, 0) if pattern == "scatter" else x[None]
            lands.append(lax.dynamic_update_slice(lax.empty(shape, x.dtype), own, (me,) + (0,) * (len(shape) - 1)))

    def body(*refs):
        x_refs, land_refs = refs[:na], refs[na:2 * na]
        send_sem, recv_sem = refs[2 * na + len(extra)], refs[2 * na + len(extra) + 1]
        token = refs[-1]
        for a in range(na):
            for cp in _split_copies(pattern, x_refs[a], land_refs[a], send_sem, recv_sem):
                cp.start()
        token[...] = jnp.zeros_like(token)

    outs = pl.pallas_call(
        body, name=name,
        out_shape=(pltpu.SemaphoreType.DMA(()), pltpu.SemaphoreType.DMA(()))
        + tuple(pltpu.HBM(x.shape, x.dtype) for x in xs) + tuple(pltpu.HBM(l.shape, l.dtype) for l in lands)
        + (_sds((8, LANES), F32),),
        in_specs=(_HBM,) * (2 * na) + (pl.BlockSpec(memory_space=pl.ANY),) * len(extra),
        out_specs=(_SEM, _SEM) + (_HBM,) * (2 * na) + (pl.BlockSpec(memory_space=pltpu.VMEM),),
        input_output_aliases={a: 2 + a for a in range(2 * na)},
        compiler_params=pltpu.CompilerParams(has_side_effects=_EFFECT),
    )(*[pltpu.with_memory_space_constraint(x, pltpu.HBM) for x in xs],
      *[pltpu.with_memory_space_constraint(l, pltpu.HBM) for l in lands], *extra)
    return dict(sems=outs[0:2], xs=outs[2:2 + na], lands=outs[2 + na:2 + 2 * na], token=outs[-1], pattern=pattern)


def comm_wait(started, after, name, with_xs=False):
    xs, lands = started["xs"], started["lands"]
    pattern = started["pattern"]
    na = len(xs)

    def body(*refs):
        x_refs, land_refs = refs[:na], refs[na:2 * na]
        send_sem, recv_sem = refs[2 * na], refs[2 * na + 1]
        for a in range(na):
            for cp in _split_copies(pattern, x_refs[a], land_refs[a], send_sem, recv_sem):
                cp.wait_send()
                cp.wait_recv()

    outs = pl.pallas_call(
        body, name=name,
        out_shape=tuple(pltpu.HBM(x.shape, x.dtype) for x in xs) + tuple(pltpu.HBM(l.shape, l.dtype) for l in lands),
        in_specs=(_HBM,) * (2 * na) + (_SEM, _SEM, pl.BlockSpec(memory_space=pl.ANY)),
        out_specs=(_HBM,) * (2 * na),
        input_output_aliases={a: a for a in range(2 * na)},
        compiler_params=pltpu.CompilerParams(has_side_effects=_EFFECT),
    )(*xs, *lands, *started["sems"], after)
    return (list(outs[na:]), list(outs[:na])) if with_xs else list(outs[na:])


def _cols_to_natural(g):
    return jnp.concatenate([g[k] for k in range(N_DEV)], axis=1)


def _vec8(rows, d):
    rows = [r.reshape(1, d).astype(F32) for r in rows]
    return jnp.concatenate(rows + [jnp.zeros((8 - len(rows), d), F32)], axis=0)


def _ffn_forward(x, vec, w_in_t, w_out):
    xn, h, a, b, u, y = ffn_fwd(x, vec, w_in_t, w_out)
    return xn, (x, h, a, b, u, y)


def _ffn_backward(dxo, saved, vec, w_in_t, w_out, on_rows=None):
    x, h, a, b, u, y = saved
    dy, dab, dx, part = ffn_bwd(dxo, y, vec, w_out, w_in_t, a, b, x)
    rows = part[0:4]
    token = on_rows(rows) if on_rows is not None else None
    g_out = grad_slots(u, dy, "ffn_dw_out", after=token)
    g_in_t = grad_slots(dab, h, "ffn_dw_in", after=token)
    return dx, g_in_t, g_out, rows


_TRANSPOSED = ("ffn1_w_in", "ffn2_w_in", "attn_w_q")
_COL_NATURAL = ("conv_w_in", "w_kv", "attn_w_o")
_ROW_SHARDED = ("ffn1_w_out", "ffn2_w_out", "conv_w_out")
_BIG = _TRANSPOSED + _COL_NATURAL + _ROW_SHARDED


def weight_chunks():
    chunks = []
    for layer in range(DEPTH):
        first = [("ffn1_w_in", layer), ("ffn1_w_out", layer)]
        if layer == N_A_LAYERS:
            first = [("w_kv", layer)] + first
        mixer = [("conv_w_in", layer), ("conv_w_out", layer)] if layer < N_A_LAYERS else [("attn_w_q", layer), ("attn_w_o", layer)]
        rest = mixer + [("ffn2_w_in", layer), ("ffn2_w_out", layer)]
        chunks += [first, rest] if layer == 0 else [first + rest]
    return chunks


def stacked_index(name, layer):
    if name == "w_kv":
        return None
    return layer - N_A_LAYERS if name.startswith("attn") else layer


class ChunkComm:
    def __init__(self, shards):
        self.shards = shards
        self.chunks = weight_chunks()

    def _shard(self, name, layer):
        idx = stacked_index(name, layer)
        return self.shards[name][0 if idx is None else idx]

    def start_gather(self, ci, after):
        xs = [self._shard(n, l).astype(BF16) for n, l in self.chunks[ci]]
        return comm_start(xs, "to_chips", after, f"gather_start_{ci}")

    def relay_gather(self, ci, started, after):
        lands, xs = comm_wait(started, after, f"gather_wait_{ci}", with_xs=True)
        return comm_start(xs, "relay", None, f"gather_relay_{ci}", lands=lands)

    def finish_gather(self, ci, relayed, after):
        lands = comm_wait(relayed, after, f"gather_done_{ci}")
        W = {}
        for key, g in zip(self.chunks[ci], lands):
            W[key] = _cols_to_natural(g) if key[0] in _COL_NATURAL else g.reshape(-1, g.shape[2])
        return W

    def start_exchange(self, ci, slots, after):
        return comm_start([slots[key] for key in self.chunks[ci]], "scatter", after, f"exchange_start_{ci}")

    def finish_exchange(self, ci, started, after):
        lands = comm_wait(started, after, f"exchange_wait_{ci}")
        return dict(zip(self.chunks[ci], lands))


def device_step(x, positions, target, mods, kvmods, small, comm, gather0):
    T, D = x.shape
    groups = DILATED_GROUPS
    dils = [dil for _, dil in groups]
    lane = jnp.arange(LANES) % HEAD_DIM
    inv = ROPE_THETA ** (-jnp.arange(0, ROPE_DIM, 2, dtype=F32) / ROPE_DIM)
    lane_rows = _vec8([jnp.where(lane < ROPE_DIM, inv[lane % (ROPE_DIM // 2)], 0.0), lane < ROPE_DIM,
                       (lane >= ROPE_DIM // 2) & (lane < ROPE_DIM), lane < ROPE_DIM // 2], LANES)
    tabs = rope_tables(positions.reshape(T, 1), lane_rows)

    def after_token(v, token):
        return v if token is None else v + token[0, 0]

    def vec_of(layer, sub):
        return _vec8([small["norm_g"][layer, sub], mods[layer, 3 * sub], mods[layer, 3 * sub + 1], mods[layer, 3 * sub + 2]], D)

    saved = []
    kv_saved = None
    k_sh = v_sh = None
    qw = GROUP_WIDTH * len(groups)
    chunk_of = {key: ci for ci, chunk in enumerate(comm.chunks) for key in chunk}
    W = {}
    flight = {"ci": 0, "started": gather0}

    def need(key, after):
        if key not in W:
            ci = chunk_of[key]
            assert ci == flight["ci"], (key, ci)
            relayed = comm.relay_gather(ci, flight["started"], after)
            token = relayed["token"]
            if ci + 1 < len(comm.chunks):
                flight.update(ci=ci + 1, started=comm.start_gather(ci + 1, token))
                token = flight["started"]["token"]
            W.update(comm.finish_gather(ci, relayed, token))
        return W[key]

    for layer in range(DEPTH):
        if layer == N_A_LAYERS:
            w_kv = need(("w_kv", layer), x)
            kv_vec = _vec8([small["kv_norm_g"], kvmods[0], kvmods[1]], D)
            h_kv, *kv_pieces = proj_rope_fwd(x, kv_vec, w_kv, tabs, qw, False, dils, "kv_fwd")
            k_sh, v_sh = kv_pieces[:len(groups)], kv_pieces[len(groups):]
            kv_saved = (x, h_kv, kv_vec)
        rec = {}
        w_in, w_out = need(("ffn1_w_in", layer), x), need(("ffn1_w_out", layer), x)
        v1 = vec_of(layer, 0)
        x, rec["ffn1"] = _ffn_forward(x, v1, w_in, w_out)
        if layer < N_A_LAYERS:
            w_in, w_out = need(("conv_w_in", layer), x), need(("conv_w_out", layer), x)
            v2 = vec_of(layer, 1)
            cw = _vec8(list(small["conv_w"][layer]), D)
            x_in = x
            x, h, bcu, cv, z, y = conv_fwd(x, v2, cw, w_in, w_out)
            rec["mix"] = (x_in, h, bcu, cv, z, y, cw)
        else:
            w_q, w_o = need(("attn_w_q", layer), x), need(("attn_w_o", layer), x)
            v2 = vec_of(layer, 1)
            x_in = x
            h, *q = proj_rope_fwd(x, v2, w_q, tabs, qw, True, dils, "q_fwd")
            os_, ls = [], []
            for g, (win, dil) in enumerate(groups):
                o, l = attn_core_fwd(q[g], k_sh[g], v_sh[g], g, win // dil)
                os_.append(o)
                ls.append(l)
            x, mixed, y = attn_mix_out(os_, ls, dils, x, v2, w_o)
            rec["mix"] = (x_in, h, q, os_, ls, mixed, y)
        w_in, w_out = need(("ffn2_w_in", layer), x), need(("ffn2_w_out", layer), x)
        v3 = vec_of(layer, 2)
        x, rec["ffn2"] = _ffn_forward(x, v3, w_in, w_out)
        rec["vecs"] = (v1, v2, v3)
        saved.append(rec)

    dx, part_final, loss_tile = final_loss(x, _vec8([small["final_norm_g"]], D), target)
    loss = loss_tile[0, 0]

    conv_rows = [None] * N_A_LAYERS
    kv_rows = None
    mod_rows = [[None] * 3 for _ in range(DEPTH)]
    dkv_pairs = [{"k": [], "v": []} for _ in groups]
    slots = {}
    exchanges = []
    token = None

    def send_ready_chunks():
        nonlocal token
        for ci in reversed(range(len(comm.chunks))):
            if ci not in [e[0] for e in exchanges] and all(key in slots for key in comm.chunks[ci]):
                started = comm.start_exchange(ci, slots, token)
                exchanges.append((ci, started))
                token = started["token"]

    vector_gather = {}

    def start_vector_gather(rows0):
        mod_rows[0][0] = rows0
        rows = jnp.stack([jnp.stack(r) for r in mod_rows])
        vecs = jnp.concatenate([rows[:, :, 1:4].reshape(-1), kv_rows[1:3].reshape(-1), kv_rows[0], part_final[0],
                                rows[:, :, 0].reshape(-1), jnp.stack(conv_rows).reshape(-1)])
        vector_gather["count"] = vecs.shape[0]
        vecs = _pad_rows(vecs.reshape(-1, 1), 8 * LANES).reshape(-1, LANES)
        vector_gather["started"] = comm_start([vecs], "gather", None, "vector_grads_start")
        return vector_gather["started"]["token"]

    for layer in reversed(range(DEPTH)):
        rec = saved[layer]
        v1, v2, v3 = rec["vecs"]
        dx, slots[("ffn2_w_in", layer)], slots[("ffn2_w_out", layer)], mod_rows[layer][2] = _ffn_backward(
            dx, rec["ffn2"], after_token(v3, token), W[("ffn2_w_in", layer)], W[("ffn2_w_out", layer)])
        if layer < N_A_LAYERS:
            x_in, h, bcu, cv, z, y, cw = rec["mix"]
            dx, dy, dbcu, part, dcw = conv_bwd(dx, x_in, y, bcu, cv, v2, cw, W[("conv_w_in", layer)], W[("conv_w_out", layer)])
            slots[("conv_w_out", layer)] = grad_slots(z, dy, "conv_dw_out")
            slots[("conv_w_in", layer)] = grad_slots(h, dbcu, "conv_dw_in", col_slots=True)
            conv_rows[layer] = dcw[0:3]
            mod_rows[layer][1] = part[0:4]
        else:
            x_in, h, q, os_, ls, mixed, y = rec["mix"]
            outs = attn_mix_bwd(dx, y, v2, W[("attn_w_o", layer)], os_, ls, dils)
            ng = len(groups)
            dy, dos, deltas, part_gate = outs[0], outs[1:1 + ng], outs[1 + ng:1 + 2 * ng], outs[1 + 2 * ng]
            slots[("attn_w_o", layer)] = grad_slots(mixed, dy, "attn_dw_o", col_slots=True)
            dqs = []
            for g, (win, dil) in enumerate(groups):
                dq, dkc, dkp, dvc, dvp = attn_core_bwd(q[g], k_sh[g], v_sh[g], dos[g], deltas[g], ls[g], g, win // dil)
                dqs.append(dq)
                dkv_pairs[g]["k"].append((dkc, dkp))
                dkv_pairs[g]["v"].append((dvc, dvp))
            dx, dqr, part_norm = proj_rope_bwd(dqs, dils, x_in, dx, v2, W[("attn_w_q", layer)], tabs, qw, True, "q_bwd")
            slots[("attn_w_q", layer)] = grad_slots(dqr, h, "attn_dw_q")
            mod_rows[layer][1] = jnp.concatenate([part_norm[0:3], part_gate[0:1]], axis=0)
        send_ready_chunks()
        dx, slots[("ffn1_w_in", layer)], slots[("ffn1_w_out", layer)], mod_rows[layer][0] = _ffn_backward(
            dx, rec["ffn1"], after_token(v1, token), W[("ffn1_w_in", layer)], W[("ffn1_w_out", layer)],
            on_rows=start_vector_gather if layer == 0 else None)
        if layer == N_A_LAYERS:
            x_kv, h_kv, kv_vec = kv_saved
            dparts = [dkv_combine(dkv_pairs[g]["k"], win // dil, f"dk_combine_g{g}") for g, (win, dil) in enumerate(groups)]
            dparts += [dkv_combine(dkv_pairs[g]["v"], win // dil, f"dv_combine_g{g}") for g, (win, dil) in enumerate(groups)]
            dx, dkvp, part_kv = proj_rope_bwd(dparts, dils, x_kv, dx, kv_vec, W[("w_kv", layer)], tabs, qw, False, "kv_bwd")
            slots[("w_kv", layer)] = grad_slots(h_kv, dkvp, "kv_dw", col_slots=True)
            kv_rows = part_kv[0:3]
        send_ready_chunks()

    return loss, dx, {"exchanges": exchanges, "vector_gather": vector_gather}


def _flat2(a):
    return a.reshape(-1, a.shape[-1])


def _pad_rows(a, mult):
    r = a.shape[0]
    pad = (-r) % mult
    return a if pad == 0 else jnp.concatenate([a, jnp.zeros((pad,) + a.shape[1:], a.dtype)], axis=0)


def kernel(x, c, positions, norm_g, ada_w, ada_b, ffn1_w_in, ffn1_w_out, ffn2_w_in, ffn2_w_out, conv_w_in, conv_w, conv_w_out, kv_norm_g, kv_ada_w, kv_ada_b, w_kv, attn_w_q, attn_w_o, final_norm_g, loss_target, m_norm_g, m_ada_w, m_ada_b, m_ffn1_w_in, m_ffn1_w_out, m_ffn2_w_in, m_ffn2_w_out, m_conv_w_in, m_conv_w, m_conv_w_out, m_kv_norm_g, m_kv_ada_w, m_kv_ada_b, m_w_kv, m_attn_w_q, m_attn_w_o, m_final_norm_g, v_norm_g, v_ada_w, v_ada_b, v_ffn1_w_in, v_ffn1_w_out, v_ffn2_w_in, v_ffn2_w_out, v_conv_w_in, v_conv_w, v_conv_w_out, v_kv_norm_g, v_kv_ada_w, v_kv_ada_b, v_w_kv, v_attn_w_q, v_attn_w_o, v_final_norm_g):
    names = ("norm_g", "ada_w", "ada_b", "ffn1_w_in", "ffn1_w_out", "ffn2_w_in", "ffn2_w_out", "conv_w_in", "conv_w",
             "conv_w_out", "kv_norm_g", "kv_ada_w", "kv_ada_b", "w_kv", "attn_w_q", "attn_w_o", "final_norm_g")
    wts = dict(zip(names, (norm_g, ada_w, ada_b, ffn1_w_in, ffn1_w_out, ffn2_w_in, ffn2_w_out, conv_w_in, conv_w, conv_w_out,
                           kv_norm_g, kv_ada_w, kv_ada_b, w_kv, attn_w_q, attn_w_o, final_norm_g)))
    mom = dict(zip(names, (m_norm_g, m_ada_w, m_ada_b, m_ffn1_w_in, m_ffn1_w_out, m_ffn2_w_in, m_ffn2_w_out, m_conv_w_in,
                           m_conv_w, m_conv_w_out, m_kv_norm_g, m_kv_ada_w, m_kv_ada_b, m_w_kv, m_attn_w_q, m_attn_w_o,
                           m_final_norm_g)))
    var = dict(zip(names, (v_norm_g, v_ada_w, v_ada_b, v_ffn1_w_in, v_ffn1_w_out, v_ffn2_w_in, v_ffn2_w_out, v_conv_w_in,
                           v_conv_w, v_conv_w_out, v_kv_norm_g, v_kv_ada_w, v_kv_ada_b, v_w_kv, v_attn_w_q, v_attn_w_o,
                           v_final_norm_g)))
    T, D = x.shape[1], x.shape[2]
    me = _my_id()
    nmod = ada_w.shape[2]
    nkv = kv_ada_w.shape[1]

    def stacked(w, n):
        w = w if w.ndim == 3 else w[None]
        return jnp.swapaxes(w, 1, 2) if n in _TRANSPOSED else w

    comm = ChunkComm({n: stacked(wts[n], n) for n in _BIG})
    W = {}

    ds = norm_g.shape[2]
    small = jnp.concatenate([c.reshape(-1), norm_g.reshape(-1), conv_w.reshape(-1)]).astype(F32)
    n_small = small.shape[0]
    small = _pad_rows(small.reshape(-1, 1), 8 * LANES).reshape(-1, LANES)
    (small_all,) = all_gather([small], pltpu.VMEM, "gather_small")
    small_all = small_all.reshape(N_DEV, -1)[:, :n_small]
    c_all = small_all[:, :D]
    def full_rows(off, count):
        return jnp.stack([small_all[:, off + i * ds:off + (i + 1) * ds].reshape(D) for i in range(count)])

    W["norm_g"] = full_rows(D, DEPTH * 3).reshape(DEPTH, 3, D)
    W["conv_w"] = full_rows(D + DEPTH * 3 * ds, N_A_LAYERS * 3).reshape(N_A_LAYERS, 3, D)
    W["kv_norm_g"], W["final_norm_g"] = kv_norm_g, final_norm_g

    ada_b_mine = lax.dynamic_slice_in_dim(ada_b, me * nmod, nmod, axis=1).reshape(DEPTH, 1, nmod)
    kv_b_mine = lax.dynamic_slice_in_dim(kv_ada_b, me * nkv, nkv, axis=0).reshape(1, 1, nkv)
    mods_cols = mods_project(c_all, ada_w, ada_b_mine)
    kv_cols = mods_project(c_all, kv_ada_w.reshape(1, D, nkv), kv_b_mine)
    mcat = jnp.concatenate([mods_cols[l] for l in range(DEPTH)] + [kv_cols[0]], axis=1)
    wm = mcat.shape[1]
    if wm % LANES:
        mcat = jnp.concatenate([mcat, jnp.zeros((N_DEV, LANES - wm % LANES), F32)], axis=1)
    (mods_all,) = exchange_slots([mcat.reshape(N_DEV, 1, -1)], "exchange_mods")
    gather0 = comm.start_gather(0, mods_all)
    mods_all = mods_all.reshape(N_DEV, -1)
    mods = jnp.stack([mods_all[:, l * nmod:(l + 1) * nmod].reshape(N_MOD, D) for l in range(DEPTH)])
    kvmods = mods_all[:, DEPTH * nmod:DEPTH * nmod + nkv].reshape(2, D)

    loss_local, dx, grads = device_step(x[0], positions[0], loss_target[0], mods, kvmods, W, comm, gather0)
    loss = lax.psum(loss_local, MESH_AXES)

    (vec_all,) = comm_wait(grads["vector_gather"]["started"], grads["exchanges"][-1][1]["token"], "vector_grads_wait")
    vec_all = vec_all.reshape(N_DEV, -1)[:, :grads["vector_gather"]["count"]]
    nm_, nk_ = DEPTH * N_MOD * D, 2 * D
    dmods_all = vec_all[:, :nm_].reshape(N_DEV, DEPTH, N_MOD * D)
    dkvm_all = vec_all[:, nm_:nm_ + nk_]
    rest = vec_all[:, nm_ + nk_:]
    parts_kv_norm, parts_final = rest[:, :D].reshape(N_DEV, 1, D), rest[:, D:2 * D].reshape(N_DEV, 1, D)
    parts_norm = lax.dynamic_slice_in_dim(rest[:, 2 * D:2 * D + DEPTH * 3 * D].reshape(N_DEV, DEPTH * 3, D), me * ds, ds, axis=2)
    parts_conv = lax.dynamic_slice_in_dim(rest[:, 2 * D + DEPTH * 3 * D:].reshape(N_DEV, N_A_LAYERS * 3, D), me * ds, ds, axis=2)
    dm_cols = lax.dynamic_slice_in_dim(dmods_all, me * nmod, nmod, axis=2)
    dm_mine = jnp.stack([dm_cols[:, l] for l in range(DEPTH)])
    dkv_mine = lax.dynamic_slice_in_dim(dkvm_all, me * nkv, nkv, axis=1).reshape(1, N_DEV, nkv)
    g_ada_w = mods_weight_grad(c_all, dm_mine)
    g_kv_ada_w = mods_weight_grad(c_all, dkv_mine)[0]

    out_g, out_d, out_m, out_v = {}, {}, {}, {}

    def update(n, g, w, parts=False):
        shp = w.shape
        w2 = w.reshape(1, -1) if w.ndim == 1 else _flat2(w)
        g2 = g if parts else g.reshape(w2.shape)
        res = adam_update(g2, w2, mom[n].reshape(w2.shape), var[n].reshape(w2.shape), parts, "adam_" + n)
        out_g[n], out_d[n], out_m[n], out_v[n] = (r.reshape(shp) for r in res)

    moms = {n: stacked(mom[n], n) for n in _BIG}
    vars_ = {n: stacked(var[n], n) for n in _BIG}
    results = {}
    after = dx
    for ci, started in grads["exchanges"]:
        for (n, layer), parts in comm.finish_exchange(ci, started, after).items():
            idx = stacked_index(n, layer)
            results[n] = adam_layer(parts, comm.shards[n], moms[n], vars_[n], results.get(n), 0 if idx is None else idx,
                                    after, f"adam_{n}_{layer}")
            after = results[n][1]
    for n in _BIG:
        res = [jnp.swapaxes(r, 1, 2) if n in _TRANSPOSED else r for r in results[n]]
        out_g[n], out_d[n], out_m[n], out_v[n] = (r.reshape(wts[n].shape) for r in res)
    update("ada_w", g_ada_w, ada_w)
    update("kv_ada_w", g_kv_ada_w, kv_ada_w)
    update("ada_b", dmods_all, ada_b, True)
    update("kv_ada_b", dkvm_all.reshape(N_DEV, 1, nk_), kv_ada_b, True)
    update("kv_norm_g", parts_kv_norm, kv_norm_g, True)
    update("final_norm_g", parts_final, final_norm_g, True)
    update("norm_g", parts_norm, norm_g, True)
    update("conv_w", parts_conv, conv_w, True)

    return (loss, dx.reshape(x.shape), *[out_g[n] for n in names], *[out_d[n] for n in names],
            *[out_m[n] for n in names], *[out_v[n] for n in names])
```

```python
import functools

import jax
import jax.numpy as jnp
from jax import lax
from jax.experimental import pallas as pl
from jax.experimental.pallas import tpu as pltpu

F32, BF16 = jnp.float32, jnp.bfloat16

N_DEV = 8
MESH_AXES = ("x", "y", "c")
DEPTH = 4
N_A_LAYERS = 2
HEAD_DIM = 64
HEADS_PER_GROUP = 8
GROUP_WIDTH = HEAD_DIM * HEADS_PER_GROUP
DILATED_GROUPS = ((128, 1), (512, 4), (2048, 16))
ROPE_DIM = HEAD_DIM // 4
ROPE_THETA = 500000.0
NORM_EPS = 1e-5
FFN_RES_WEIGHT = 0.5
N_MOD = 9
ADAM_LR, ADAM_B1, ADAM_B2, ADAM_EPS, ADAM_WD, ADAM_STEP = 0.001, 0.9, 0.999, 1e-08, 0.01, 10

LANES = 128
TOKEN_TILE = 512
FFN_BWD_TILE = 256
CONTRACT_TILE = 4096
GRAD_COLS = 768
MXU_WIDTH = 256
VMEM_LIMIT = 56 * 1024 * 1024
MESH = pl.DeviceIdType.MESH


def _cp(*sem):
    return pltpu.CompilerParams(dimension_semantics=sem, vmem_limit_bytes=VMEM_LIMIT)


def _pick(n, cap, mult=LANES):
    if n <= cap:
        return n
    best = None
    for t in range(mult, cap + 1, mult):
        if n % t == 0:
            best = t
    assert best is not None, (n, cap)
    return best


def _tok(tm, w):
    return pl.BlockSpec((tm, w), lambda i: (i, 0))


def _res(shape):
    nd = len(shape)
    return pl.BlockSpec(shape, lambda *_: (0,) * nd, pipeline_mode=pl.Buffered(1))


def _sds(shape, dt):
    return jax.ShapeDtypeStruct(shape, dt)


def _sigmoid(a):
    return 1.0 / (1.0 + jnp.exp(-a))


def _modnorm(x, g, sh, sc):
    r = lax.rsqrt(jnp.mean(x * x, axis=-1, keepdims=True) + NORM_EPS)
    return (x * r * g) * (1.0 + sc) + sh


def _dot(a, b):
    return jnp.dot(a, b, preferred_element_type=F32)


def _dot_nt(a, b):
    return lax.dot_general(a, b, (((1,), (1,)), ((), ())), preferred_element_type=F32)


def _dot_tn(a, b):
    return lax.dot_general(a, b, (((0,), (0,)), ((), ())), preferred_element_type=F32)


def _rows8(rows, d):
    pad = 8 - len(rows)
    return jnp.concatenate(list(rows) + [jnp.zeros((pad, d), F32)], axis=0)


def _acc_rows(ref, tile, first):
    @pl.when(first)
    def _():
        ref[...] = tile

    @pl.when(jnp.logical_not(first))
    def _():
        ref[...] += tile


def ffn_fwd(x, vec, w_in_t, w_out):
    T, D = x.shape
    F = w_in_t.shape[0] // 2
    tm, cw = min(TOKEN_TILE, T), _pick(F, MXU_WIDTH)

    def body(x_ref, vec_ref, wi_ref, wo_ref, xn_ref, h_ref, ga_ref, gb_ref, u_ref, y_ref):
        x_t = x_ref[...]
        hb = _modnorm(x_t, vec_ref[0:1], vec_ref[1:2], vec_ref[2:3]).astype(BF16)
        h_ref[...] = hb
        for c in range(F // cw):
            lo, hi = c * cw, (c + 1) * cw
            a = _dot_nt(hb, wi_ref[lo:hi, :])
            b = _dot_nt(hb, wi_ref[F + lo:F + hi, :])
            sg = _sigmoid(a)
            silu = a * sg
            ga_ref[:, lo:hi] = (b * (sg + silu * (1.0 - sg))).astype(BF16)
            gb_ref[:, lo:hi] = silu.astype(BF16)
            u_ref[:, lo:hi] = (silu * b).astype(BF16)
        y = _dot(u_ref[...], wo_ref[...])
        y_ref[...] = y.astype(BF16)
        xn_ref[...] = x_t + (FFN_RES_WEIGHT * (1.0 + vec_ref[3:4])) * y

    return pl.pallas_call(
        body, grid=(T // tm,),
        in_specs=[_tok(tm, D), _res((8, D)), _res((2 * F, D)), _res((F, D))],
        out_specs=[_tok(tm, D), _tok(tm, D), _tok(tm, F), _tok(tm, F), _tok(tm, F), _tok(tm, D)],
        out_shape=[_sds((T, D), F32), _sds((T, D), BF16), _sds((T, F), BF16), _sds((T, F), BF16), _sds((T, F), BF16),
                   _sds((T, D), BF16)],
        compiler_params=_cp("arbitrary"), name="ffn_fwd")(x, vec, w_in_t, w_out)


def ffn_bwd(dxo, y, vec, w_out, w_in_t, a, b, x):
    T, D = x.shape
    F = a.shape[1]
    tm, cw = min(FFN_BWD_TILE, T), _pick(F, MXU_WIDTH)

    def body(dxo_ref, y_ref, vec_ref, wo_ref, wi_ref, a_ref, b_ref, x_ref, dy_ref, dab_ref, dx_ref, part_ref):
        dxo_t = dxo_ref[...]
        dyb = (dxo_t * (FFN_RES_WEIGHT * (1.0 + vec_ref[3:4]))).astype(BF16)
        dy_ref[...] = dyb
        dgate = FFN_RES_WEIGHT * jnp.sum(dxo_t * y_ref[...].astype(F32), axis=0, keepdims=True)
        for c in range(F // cw):
            lo, hi = c * cw, (c + 1) * cw
            du = _dot_nt(dyb, wo_ref[lo:hi, :])
            dab_ref[:, lo:hi] = (du * a_ref[:, lo:hi].astype(F32)).astype(BF16)
            dab_ref[:, F + lo:F + hi] = (du * b_ref[:, lo:hi].astype(F32)).astype(BF16)
        dh = _dot(dab_ref[...], wi_ref[...])
        _, vjp = jax.vjp(_modnorm, x_ref[...], vec_ref[0:1], vec_ref[1:2], vec_ref[2:3])
        dx, dg, dsh, dsc = vjp(dh)
        dx_ref[...] = dxo_t + dx
        _acc_rows(part_ref, _rows8([dg, dsh, dsc, dgate], D), pl.program_id(0) == 0)

    return pl.pallas_call(
        body, grid=(T // tm,),
        in_specs=[_tok(tm, D), _tok(tm, D), _res((8, D)), _res((F, D)), _res((2 * F, D)), _tok(tm, F), _tok(tm, F), _tok(tm, D)],
        out_specs=[_tok(tm, D), _tok(tm, 2 * F), _tok(tm, D), pl.BlockSpec((8, D), lambda i: (0, 0))],
        out_shape=[_sds((T, D), BF16), _sds((T, 2 * F), BF16), _sds((T, D), F32), _sds((8, D), F32)],
        compiler_params=_cp("arbitrary"), name="ffn_bwd")(dxo, y, vec, w_out, w_in_t, a, b, x)


def grad_slots(a, b, name, col_slots=False, after=None):
    T, M = a.shape
    extra = [] if after is None else [after]
    N = b.shape[1]
    tk = min(CONTRACT_TILE, T)
    nk = T // tk
    tmm = _pick(M, 1408)
    if col_slots:
        ns = N // N_DEV
        sp = max(s for s in (1, 2, 4, 8) if ns * s <= GRAD_COLS or s == 1)
        tn = ns * sp
    else:
        tn = _pick(N, GRAD_COLS)

    def body(a_ref, b_ref, *rest):
        o_ref, acc = rest[-2:]
        k = pl.program_id(2)
        t = _dot_tn(a_ref[...], b_ref[...])

        @pl.when(k == 0)
        def _():
            acc[...] = t

        @pl.when(k > 0)
        def _():
            acc[...] += t

        @pl.when(k == nk - 1)
        def _():
            if col_slots:
                for s in range(sp):
                    o_ref[s] = acc[:, s * ns:(s + 1) * ns].astype(BF16)
            else:
                o_ref[...] = acc[...].astype(BF16)

    if col_slots:
        out_spec, out_shape = pl.BlockSpec((sp, tmm, ns), lambda i, j, k: (j, i, 0)), _sds((N_DEV, M, ns), BF16)
    else:
        out_spec, out_shape = pl.BlockSpec((tmm, tn), lambda i, j, k: (i, j)), _sds((M, N), BF16)
    out = pl.pallas_call(
        body, grid=(M // tmm, N // tn, nk),
        in_specs=[pl.BlockSpec((tk, tmm), lambda i, j, k: (k, i)), pl.BlockSpec((tk, tn), lambda i, j, k: (k, j))]
        + [pl.BlockSpec(memory_space=pl.ANY)] * len(extra),
        out_specs=out_spec, out_shape=out_shape,
        scratch_shapes=[pltpu.VMEM((tmm, tn), F32)],
        compiler_params=_cp("arbitrary", "arbitrary", "arbitrary"), name=name)(a, b, *extra)
    return out if col_slots else out.reshape(N_DEV, M // N_DEV, N)


def conv_fwd(x, vec, cw, w_in, w_out):
    T, D = x.shape
    tm = min(TOKEN_TILE, T)

    def body(x_ref, vec_ref, cw_ref, wi_ref, wo_ref, xn_ref, h_ref, bcu_ref, cv_ref, z_ref, y_ref, vbuf):
        @pl.when(pl.program_id(0) == 0)
        def _():
            vbuf[0:8, :] = jnp.zeros((8, D), F32)

        x_t = x_ref[...]
        hb = _modnorm(x_t, vec_ref[0:1], vec_ref[1:2], vec_ref[2:3]).astype(BF16)
        h_ref[...] = hb
        bcu = _dot(hb, wi_ref[...])
        bcu_ref[...] = bcu.astype(BF16)
        bg, v = bcu[:, 0:D], bcu[:, D:2 * D] * bcu[:, 2 * D:3 * D]
        vbuf[8:8 + tm, :] = v
        conv = cw_ref[0:1] * vbuf[6:6 + tm, :] + cw_ref[1:2] * vbuf[7:7 + tm, :] + cw_ref[2:3] * v
        cv_ref[...] = conv.astype(BF16)
        zb = (bg * conv).astype(BF16)
        z_ref[...] = zb
        y = _dot(zb, wo_ref[...])
        y_ref[...] = y.astype(BF16)
        xn_ref[...] = x_t + (1.0 + vec_ref[3:4]) * y
        vbuf[0:8, :] = vbuf[tm:tm + 8, :]

    return pl.pallas_call(
        body, grid=(T // tm,),
        in_specs=[_tok(tm, D), _res((8, D)), _res((8, D)), _res((D, 3 * D)), _res((D, D))],
        out_specs=[_tok(tm, D), _tok(tm, D), _tok(tm, 3 * D), _tok(tm, D), _tok(tm, D), _tok(tm, D)],
        out_shape=[_sds((T, D), F32), _sds((T, D), BF16), _sds((T, 3 * D), BF16), _sds((T, D), BF16),
                   _sds((T, D), BF16), _sds((T, D), BF16)],
        scratch_shapes=[pltpu.VMEM((tm + 8, D), F32)],
        compiler_params=_cp("arbitrary"), name="conv_fwd")(x, vec, cw, w_in, w_out)


def conv_bwd(dxo, x, y, bcu, cv, vec, cw, w_in, w_out):
    T, D = x.shape
    tm = min(TOKEN_TILE, T)
    nt = T // tm

    def body(dxo_ref, x_ref, y_ref, bcu_ref, cv_ref, vec_ref, cw_ref, wi_ref, wo_ref,
             dx_ref, dy_ref, dbcu_ref, part_ref, dcw_ref, dcbuf):
        first = pl.program_id(0) == 0

        @pl.when(first)
        def _():
            dcbuf[tm:tm + 8, :] = jnp.zeros((8, D), F32)

        dxo_t = dxo_ref[...]
        dyb = (dxo_t * (1.0 + vec_ref[3:4])).astype(BF16)
        dy_ref[...] = dyb
        dgate = jnp.sum(dxo_t * y_ref[...].astype(F32), axis=0, keepdims=True)
        dz = _dot_nt(dyb, wo_ref[...])
        bcu_t = bcu_ref[...].astype(F32)
        bg, cg, ug = bcu_t[:, 0:D], bcu_t[:, D:2 * D], bcu_t[:, 2 * D:3 * D]
        dconv = dz * bg
        dbg = dz * cv_ref[...].astype(F32)
        dcbuf[0:tm, :] = dconv
        d1, d2 = dcbuf[1:tm + 1, :], dcbuf[2:tm + 2, :]
        dv = cw_ref[2:3] * dconv + cw_ref[1:2] * d1 + cw_ref[0:1] * d2
        v = cg * ug
        dcw = _rows8([jnp.sum(d2 * v, axis=0, keepdims=True), jnp.sum(d1 * v, axis=0, keepdims=True),
                      jnp.sum(dconv * v, axis=0, keepdims=True)], D)
        dbcu = jnp.concatenate([dbg, dv * ug, dv * cg], axis=1).astype(BF16)
        dbcu_ref[...] = dbcu
        dh = _dot_nt(dbcu, wi_ref[...])
        _, vjp = jax.vjp(_modnorm, x_ref[...], vec_ref[0:1], vec_ref[1:2], vec_ref[2:3])
        dx, dg, dsh, dsc = vjp(dh)
        dx_ref[...] = dxo_t + dx
        _acc_rows(part_ref, _rows8([dg, dsh, dsc, dgate], D), first)
        _acc_rows(dcw_ref, dcw, first)
        dcbuf[tm:tm + 8, :] = dcbuf[0:8, :]

    def rev(w):
        return pl.BlockSpec((tm, w), lambda i: (nt - 1 - i, 0))

    return pl.pallas_call(
        body, grid=(nt,),
        in_specs=[rev(D), rev(D), rev(D), rev(3 * D), rev(D), _res((8, D)), _res((8, D)), _res((D, 3 * D)), _res((D, D))],
        out_specs=[rev(D), rev(D), rev(3 * D), pl.BlockSpec((8, D), lambda i: (0, 0)), pl.BlockSpec((8, D), lambda i: (0, 0))],
        out_shape=[_sds((T, D), F32), _sds((T, D), BF16), _sds((T, 3 * D), BF16), _sds((8, D), F32), _sds((8, D), F32)],
        scratch_shapes=[pltpu.VMEM((tm + 8, D), F32)],
        compiler_params=_cp("arbitrary"), name="conv_bwd")(dxo, x, y, bcu, cv, vec, cw, w_in, w_out)


def rope_tables(pos, lane_rows):
    T = pos.shape[0]
    tm = min(TOKEN_TILE, T)

    def body(p_ref, lr_ref, c_ref, sp_ref, sm_ref):
        ang = p_ref[...].astype(F32) * lr_ref[0:1]
        cs, sn = jnp.cos(ang), jnp.sin(ang)
        c_ref[...] = jnp.where(lr_ref[1:2] > 0.5, cs, 1.0)
        sp_ref[...] = jnp.where(lr_ref[2:3] > 0.5, sn, 0.0)
        sm_ref[...] = jnp.where(lr_ref[3:4] > 0.5, -sn, 0.0)

    return pl.pallas_call(
        body, grid=(T // tm,),
        in_specs=[_tok(tm, 1), _res((8, LANES))],
        out_specs=[_tok(tm, LANES)] * 3,
        out_shape=[_sds((T, LANES), F32)] * 3,
        compiler_params=_cp("arbitrary"), name="rope_tables")(pos, lane_rows)


def _rope(t, c, sp, sm):
    w = t.shape[1]
    reps = w // LANES
    cf, spf, smf = jnp.tile(c, (1, reps)), jnp.tile(sp, (1, reps)), jnp.tile(sm, (1, reps))
    half = ROPE_DIM // 2
    return t * cf + pltpu.roll(t, half, axis=1) * spf + pltpu.roll(t, w - half, axis=1) * smf


def _rope_t(d, c, sp, sm):
    w = d.shape[1]
    reps = w // LANES
    cf, spf, smf = jnp.tile(c, (1, reps)), jnp.tile(sp, (1, reps)), jnp.tile(sm, (1, reps))
    half = ROPE_DIM // 2
    return d * cf + pltpu.roll(d * spf, w - half, axis=1) + pltpu.roll(d * smf, half, axis=1)


def _split_residues(v, d, stage):
    tm, width = v.shape
    if d == 1:
        return [v]
    nj = width // LANES
    for j in range(nj):
        stage[j] = v[:, j * LANES:(j + 1) * LANES]
    return [jnp.concatenate([stage[j, pl.ds(r, tm // d, stride=d), :] for j in range(nj)], axis=1) for r in range(d)]


def _merge_residues(piece, d, tm, width, stage):
    if d == 1:
        return piece(0)
    nj = width // LANES
    for r in range(d):
        p = piece(r)
        for j in range(nj):
            stage[j, pl.ds(r, tm // d, stride=d), :] = p[:, j * LANES:(j + 1) * LANES]
    return jnp.concatenate([stage[j] for j in range(nj)], axis=1)


def _residue_spec(d, tm, width=GROUP_WIDTH):
    return pl.BlockSpec((d, tm // d, width), lambda i: (0, i, 0))


def _stage_scratch(tm):
    return pltpu.VMEM((GROUP_WIDTH // LANES, tm, LANES), F32)


def proj_rope_fwd(x, vec, w, tabs, n_rope, transposed, dils, name):
    T, D = x.shape
    N = w.shape[0] if transposed else w.shape[1]
    tm = min(TOKEN_TILE, T)
    GW = GROUP_WIDTH
    piece_dils = [dils[j % len(dils)] for j in range(N // GW)]

    def body(x_ref, vec_ref, w_ref, c_ref, sp_ref, sm_ref, h_ref, *rest):
        out_refs, stage = rest[:-1], rest[-1]
        hb = _modnorm(x_ref[...], vec_ref[0:1], vec_ref[1:2], vec_ref[2:3]).astype(BF16)
        h_ref[...] = hb
        p = _dot_nt(hb, w_ref[...]) if transposed else _dot(hb, w_ref[...])
        pr = _rope(p[:, 0:n_rope], c_ref[...], sp_ref[...], sm_ref[...])
        for j, d in enumerate(piece_dils):
            src = pr if (j + 1) * GW <= n_rope else p
            for r, rows in enumerate(_split_residues(src[:, j * GW:(j + 1) * GW], d, stage)):
                out_refs[j][r] = rows.astype(BF16)

    return pl.pallas_call(
        body, grid=(T // tm,),
        in_specs=[_tok(tm, D), _res((8, D)), _res(w.shape)] + [_tok(tm, LANES)] * 3,
        out_specs=[_tok(tm, D)] + [_residue_spec(d, tm) for d in piece_dils],
        out_shape=[_sds((T, D), BF16)] + [_sds((d, T // d, GW), BF16) for d in piece_dils],
        scratch_shapes=[_stage_scratch(tm)],
        compiler_params=_cp("arbitrary"), name=name)(x, vec, w, *tabs)


def proj_rope_bwd(dparts, dils, x, dxo, vec, w, tabs, n_rope, transposed, name):
    T, D = x.shape
    N = w.shape[0] if transposed else w.shape[1]
    tm = min(TOKEN_TILE, T)
    GW = GROUP_WIDTH
    npart = len(dparts)
    piece_dils = [dils[j % len(dils)] for j in range(npart)]

    def body(*refs):
        d_refs = refs[:npart]
        x_ref, dxo_ref, vec_ref, w_ref, c_ref, sp_ref, sm_ref, dx_ref, dp_ref, part_ref, stage = refs[npart:]
        d = jnp.concatenate([_merge_residues(lambda r, ref=ref: ref[r].astype(F32), dd, tm, GW, stage)
                             for ref, dd in zip(d_refs, piece_dils)], axis=1)
        dr = _rope_t(d[:, 0:n_rope], c_ref[...], sp_ref[...], sm_ref[...])
        if n_rope < N:
            dr = jnp.concatenate([dr, d[:, n_rope:N]], axis=1)
        dpb = dr.astype(BF16)
        dp_ref[...] = dpb
        dh = _dot(dpb, w_ref[...]) if transposed else _dot_nt(dpb, w_ref[...])
        _, vjp = jax.vjp(_modnorm, x_ref[...], vec_ref[0:1], vec_ref[1:2], vec_ref[2:3])
        dx, dg, dsh, dsc = vjp(dh)
        dx_ref[...] = dxo_ref[...] + dx
        _acc_rows(part_ref, _rows8([dg, dsh, dsc], D), pl.program_id(0) == 0)

    return pl.pallas_call(
        body, grid=(T // tm,),
        in_specs=[_residue_spec(d, tm) for d in piece_dils] + [_tok(tm, D), _tok(tm, D), _res((8, D)), _res(w.shape)]
        + [_tok(tm, LANES)] * 3,
        out_specs=[_tok(tm, D), _tok(tm, N), pl.BlockSpec((8, D), lambda i: (0, 0))],
        out_shape=[_sds((T, D), F32), _sds((T, N), BF16), _sds((8, D), F32)],
        scratch_shapes=[_stage_scratch(tm)],
        compiler_params=_cp("arbitrary"), name=name)(*dparts, x, dxo, vec, w, *tabs)


def _valid_mask(n, i):
    qi = lax.broadcasted_iota(jnp.int32, (n, 2 * n), 0)
    kj = lax.broadcasted_iota(jnp.int32, (n, 2 * n), 1)
    dist = n + qi - kj
    return (dist >= 0) & (dist <= n) & ((kj >= n) | (i > 0))


def _band_specs(n):
    two = pl.BlockSpec((None, 2 * n, GROUP_WIDTH), lambda r, i: (r, i, 0))
    prv = pl.BlockSpec((None, n, GROUP_WIDTH), lambda r, i: (r, jnp.maximum(2 * i - 1, 0), 0))
    one = pl.BlockSpec((None, n, GROUP_WIDTH), lambda r, i: (r, i, 0))
    return two, prv, one


def _pair_keys(prev_ref, two_ref, ps, n):
    cur2 = two_ref[:, ps]
    return jnp.concatenate([prev_ref[:, ps], cur2[0:n]], axis=0), cur2


STAT_STRIDE = LANES // HEADS_PER_GROUP


def _head_of_lane():
    return lax.broadcasted_iota(jnp.int32, (1, LANES), 1) // STAT_STRIDE


def attn_core_fwd(q, k, v, g, n):
    d, M, GW = q.shape
    scale = HEAD_DIM ** -0.5

    def body(q_ref, kp_ref, kc_ref, vp_ref, vc_ref, o_ref, l_ref):
        masks = (_valid_mask(n, pl.program_id(1)), _valid_mask(n, 1))
        first = lax.broadcasted_iota(jnp.int32, (1, LANES), 1) < HEAD_DIM
        head_of_lane = _head_of_lane()
        lse = [jnp.zeros((n, LANES), F32), jnp.zeros((n, LANES), F32)]
        for pair in range(HEADS_PER_GROUP * HEAD_DIM // LANES):
            ps = slice(LANES * pair, LANES * (pair + 1))
            keys, vals = _pair_keys(kp_ref, kc_ref, ps, n), _pair_keys(vp_ref, vc_ref, ps, n)
            for blk in range(2):
                rows = slice(blk * n, (blk + 1) * n)
                q2 = q_ref[rows, ps]
                o2, l2 = [], []
                for sel in (first, jnp.logical_not(first)):
                    s = jnp.where(masks[blk], _dot_nt(jnp.where(sel, q2, jnp.zeros_like(q2)), keys[blk]) * scale, -1e30)
                    m = jnp.max(s, axis=1, keepdims=True)
                    p = jnp.exp(s - m)
                    den = jnp.sum(p, axis=1, keepdims=True)
                    o2.append(_dot((p / den).astype(BF16), vals[blk]))
                    l2.append(m + jnp.log(den))
                o_ref[rows, ps] = jnp.where(first, o2[0], o2[1]).astype(BF16)
                for half in range(2):
                    lse[blk] = jnp.where(head_of_lane == 2 * pair + half, l2[half], lse[blk])
        for blk in range(2):
            l_ref[blk * n:(blk + 1) * n, :] = lse[blk]

    two, prv, _ = _band_specs(n)
    stat = pl.BlockSpec((None, 2 * n, LANES), lambda r, i: (r, i, 0))
    return pl.pallas_call(
        body, grid=(d, M // (2 * n)),
        in_specs=[two, prv, two, prv, two], out_specs=[two, stat],
        out_shape=[_sds((d, M, GW), BF16), _sds((d, M, LANES), F32)],
        compiler_params=_cp("arbitrary", "arbitrary"), name=f"attn_fwd_g{g}")(q, k, k, v, v)


def attn_core_bwd(q, k, v, do, delta, lse, g, n):
    d, M, GW = q.shape
    scale = HEAD_DIM ** -0.5

    def body(q_ref, kp_ref, kc_ref, vp_ref, vc_ref, do_ref, d_ref, l_ref, dq_ref, dkc_ref, dkp_ref, dvc_ref, dvp_ref):
        masks = (_valid_mask(n, pl.program_id(1)), _valid_mask(n, 1))
        first = lax.broadcasted_iota(jnp.int32, (1, LANES), 1) < HEAD_DIM
        for pair in range(HEADS_PER_GROUP * HEAD_DIM // LANES):
            ps = slice(LANES * pair, LANES * (pair + 1))
            keys, vals = _pair_keys(kp_ref, kc_ref, ps, n), _pair_keys(vp_ref, vc_ref, ps, n)
            own = []
            for blk in range(2):
                rows = slice(blk * n, (blk + 1) * n)
                q2, do2 = q_ref[rows, ps], do_ref[rows, ps]
                dq2, dk, dv = [], None, None
                for half, sel in enumerate((first, jnp.logical_not(first))):
                    qm = jnp.where(sel, q2, jnp.zeros_like(q2))
                    dom = jnp.where(sel, do2, jnp.zeros_like(do2))
                    s = jnp.where(masks[blk], _dot_nt(qm, keys[blk]) * scale, -1e30)
                    lane0 = STAT_STRIDE * (2 * pair + half)
                    p = jnp.exp(s - l_ref[rows, lane0:lane0 + 1])
                    dp = _dot_nt(dom, vals[blk])
                    ds = (p * (dp - d_ref[rows, lane0:lane0 + 1]) * scale).astype(BF16)
                    dq2.append(_dot(ds, keys[blk]))
                    dkh = _dot_tn(ds, qm)
                    dvh = _dot_tn(p.astype(BF16), dom)
                    dk = dkh if dk is None else dk + dkh
                    dv = dvh if dv is None else dv + dvh
                dq_ref[rows, ps] = jnp.where(first, dq2[0], dq2[1]).astype(BF16)
                own.append((dk, dv))
            for t, (c_ref, p_ref) in enumerate(((dkc_ref, dkp_ref), (dvc_ref, dvp_ref))):
                a, b = own[0][t], own[1][t]
                p_ref[:, ps] = a[0:n].astype(BF16)
                c_ref[0:n, ps] = (a[n:2 * n] + b[0:n]).astype(BF16)
                c_ref[n:2 * n, ps] = b[n:2 * n].astype(BF16)

    two, prv, one = _band_specs(n)
    stat = pl.BlockSpec((None, 2 * n, LANES), lambda r, i: (r, i, 0))
    return pl.pallas_call(
        body, grid=(d, M // (2 * n)),
        in_specs=[two, prv, two, prv, two, two, stat, stat], out_specs=[two, two, one, two, one],
        out_shape=[_sds((d, M, GW), BF16), _sds((d, M, GW), BF16), _sds((d, M // 2, GW), BF16),
                   _sds((d, M, GW), BF16), _sds((d, M // 2, GW), BF16)],
        compiler_params=_cp("arbitrary", "arbitrary"), name=f"attn_bwd_g{g}")(q, k, k, v, v, do, delta, lse)


def dkv_combine(cur_prev, n, name):
    d, M, GW = cur_prev[0][0].shape
    rows = min(M, 1024)
    pairs = rows // (2 * n)
    steps = M // rows
    flat = [a for pair in cur_prev for a in pair]

    def body(*refs):
        o_ref = refs[-1]
        last = pl.program_id(1) == steps - 1
        acc = None
        shifted = None
        for t in range(0, len(refs) - 1, 3):
            c = refs[t][...].astype(F32)
            nxt = jnp.where(last, 0.0, refs[t + 2][...].astype(F32))
            s = nxt if pairs == 1 else jnp.concatenate([refs[t + 1][n:pairs * n, :].astype(F32), nxt], axis=0)
            acc = c if acc is None else acc + c
            shifted = s if shifted is None else shifted + s
        for m in range(pairs):
            lo = 2 * m * n
            o_ref[lo:lo + n, :] = acc[lo:lo + n].astype(BF16)
            o_ref[lo + n:lo + 2 * n, :] = (acc[lo + n:lo + 2 * n] + shifted[m * n:(m + 1) * n]).astype(BF16)

    cur = pl.BlockSpec((None, rows, GW), lambda r, i: (r, i, 0))
    same = pl.BlockSpec((None, pairs * n, GW), lambda r, i: (r, i, 0))
    nxt = pl.BlockSpec((None, n, GW), lambda r, i: (r, jnp.minimum((i + 1) * pairs, M // (2 * n) - 1), 0))
    args = []
    for c, p in cur_prev:
        args += [c, p, p]
    return pl.pallas_call(
        body, grid=(d, steps), in_specs=[cur, same, nxt] * len(cur_prev), out_specs=cur,
        out_shape=_sds((d, M, GW), BF16),
        compiler_params=_cp("arbitrary", "arbitrary"), name=name)(*args)


def _group_weights(ls):
    mx = functools.reduce(jnp.maximum, ls)
    es = [jnp.exp(l - mx) for l in ls]
    tot = functools.reduce(lambda a, b: a + b, es)
    return [e / tot for e in es]


def _expand_heads(w):
    tm = w.shape[0]
    first = lax.broadcasted_iota(jnp.int32, (1, LANES), 1) < HEAD_DIM
    cols = [jnp.broadcast_to(w[:, STAT_STRIDE * h:STAT_STRIDE * h + 1], (tm, LANES)) for h in range(HEADS_PER_GROUP)]
    return jnp.concatenate([jnp.where(first, cols[2 * p], cols[2 * p + 1]) for p in range(HEADS_PER_GROUP // 2)], axis=1)


def _head_sums(r):
    width = r.shape[1]
    feat_head = lax.broadcasted_iota(jnp.int32, (width, LANES), 0) // HEAD_DIM
    stat_head = lax.broadcasted_iota(jnp.int32, (width, LANES), 1) // STAT_STRIDE
    ones = jnp.where(feat_head == stat_head, 1.0, 0.0).astype(BF16)
    hi = r.astype(BF16)
    lo = (r - hi.astype(F32)).astype(BF16)
    return _dot(hi, ones) + _dot(lo, ones)


def _mix_weights(l_refs, dils, tm, stage):
    ls = [_merge_residues(lambda r, ref=ref: ref[r], d, tm, LANES, stage) for ref, d in zip(l_refs, dils)]
    return [_expand_heads(w) for w in _group_weights(ls)]


def attn_mix_out(os_, ls, dils, x, vec, w_o):
    T, D = x.shape
    GW = GROUP_WIDTH
    tm = min(TOKEN_TILE, T)
    ng = len(os_)

    def body(*refs):
        o_refs, l_refs = refs[:ng], refs[ng:2 * ng]
        x_ref, vec_ref, w_ref, xn_ref, mix_ref, y_ref, stage = refs[2 * ng:]
        natural = lambda ref, d: _merge_residues(lambda r: ref[r].astype(F32), d, tm, GW, stage)
        ws = _mix_weights(l_refs, dils, tm, stage)
        mixed = functools.reduce(lambda a, b: a + b, [w * natural(r, d) for w, r, d in zip(ws, o_refs, dils)])
        mb = mixed.astype(BF16)
        mix_ref[...] = mb
        y = _dot(mb, w_ref[...])
        y_ref[...] = y.astype(BF16)
        xn_ref[...] = x_ref[...] + (1.0 + vec_ref[3:4]) * y

    res = [_residue_spec(d, tm) for d in dils]
    stat = [_residue_spec(d, tm, LANES) for d in dils]
    return pl.pallas_call(
        body, grid=(T // tm,),
        in_specs=res + stat + [_tok(tm, D), _res((8, D)), _res((GW, D))],
        out_specs=[_tok(tm, D), _tok(tm, GW), _tok(tm, D)],
        out_shape=[_sds((T, D), F32), _sds((T, GW), BF16), _sds((T, D), BF16)],
        scratch_shapes=[_stage_scratch(tm)],
        compiler_params=_cp("arbitrary"), name="attn_mix_out")(*os_, *ls, x, vec, w_o)


def attn_mix_bwd(dxo, y, vec, w_o, os_, ls, dils):
    T, D = dxo.shape
    GW = GROUP_WIDTH
    tm = min(TOKEN_TILE, T)
    ng = len(os_)

    def body(*refs):
        dxo_ref, y_ref, vec_ref, w_ref = refs[:4]
        o_refs, l_refs = refs[4:4 + ng], refs[4 + ng:4 + 2 * ng]
        dy_ref = refs[4 + 2 * ng]
        do_refs = refs[5 + 2 * ng:5 + 3 * ng]
        d_refs = refs[5 + 3 * ng:5 + 4 * ng]
        part_ref, stage = refs[5 + 4 * ng], refs[6 + 4 * ng]
        natural = lambda ref, d: _merge_residues(lambda r: ref[r].astype(F32), d, tm, GW, stage)
        dxo_t = dxo_ref[...]
        dyb = (dxo_t * (1.0 + vec_ref[3:4])).astype(BF16)
        dy_ref[...] = dyb
        dgate = jnp.sum(dxo_t * y_ref[...].astype(F32), axis=0, keepdims=True)
        _acc_rows(part_ref, _rows8([dgate], D), pl.program_id(0) == 0)
        dmix = _dot_nt(dyb, w_ref[...])
        ws = _mix_weights(l_refs, dils, tm, stage)
        mixed = functools.reduce(lambda a, b: a + b, [w * natural(r, d) for w, r, d in zip(ws, o_refs, dils)])
        for gi in range(ng):
            do = ws[gi] * dmix
            for r, rows in enumerate(_split_residues(do, dils[gi], stage)):
                do_refs[gi][r] = rows.astype(BF16)
            for r, rows in enumerate(_split_residues(_head_sums(do * mixed), dils[gi], stage)):
                d_refs[gi][r] = rows

    res = [_residue_spec(d, tm) for d in dils]
    stat = [_residue_spec(d, tm, LANES) for d in dils]
    return pl.pallas_call(
        body, grid=(T // tm,),
        in_specs=[_tok(tm, D), _tok(tm, D), _res((8, D)), _res((GW, D))] + res + stat,
        out_specs=[_tok(tm, D)] + res + stat + [pl.BlockSpec((8, D), lambda i: (0, 0))],
        out_shape=[_sds((T, D), BF16)] + [_sds((d, T // d, GW), BF16) for d in dils]
        + [_sds((d, T // d, LANES), F32) for d in dils] + [_sds((8, D), F32)],
        scratch_shapes=[_stage_scratch(tm)],
        compiler_params=_cp("arbitrary"), name="attn_mix_bwd")(dxo, y, vec, w_o, *os_, *ls)


def final_loss(x, gvec, target):
    T, D = x.shape
    tm = min(TOKEN_TILE, T)

    def norm(xv, g):
        return xv * lax.rsqrt(jnp.mean(xv * xv, axis=-1, keepdims=True) + NORM_EPS) * g

    def body(x_ref, g_ref, t_ref, dx_ref, part_ref, loss_ref):
        first = pl.program_id(0) == 0
        yv, vjp = jax.vjp(norm, x_ref[...], g_ref[0:1])
        err = yv - t_ref[...]
        dx, dg = vjp(err * (1.0 / D))
        dx_ref[...] = dx
        _acc_rows(part_ref, _rows8([dg], D), first)
        tile_loss = 0.5 * jnp.sum(jnp.sum(err * err, axis=1, keepdims=True) * (1.0 / D), axis=0, keepdims=True)
        _acc_rows(loss_ref, jnp.broadcast_to(tile_loss, (8, LANES)), first)

    return pl.pallas_call(
        body, grid=(T // tm,),
        in_specs=[_tok(tm, D), _res((8, D)), _tok(tm, D)],
        out_specs=[_tok(tm, D), pl.BlockSpec((8, D), lambda i: (0, 0)), pl.BlockSpec((8, LANES), lambda i: (0, 0))],
        out_shape=[_sds((T, D), F32), _sds((8, D), F32), _sds((8, LANES), F32)],
        compiler_params=_cp("arbitrary"), name="final_loss")(x, gvec, target)


def mods_project(c_all, w, b):
    B, D = c_all.shape
    L, _, N = w.shape

    def body(c_ref, w_ref, b_ref, o_ref):
        cv = c_ref[...]
        cond = cv * _sigmoid(cv)
        o_ref[0] = jnp.dot(cond, w_ref[0], preferred_element_type=F32, precision=lax.Precision.HIGHEST) + b_ref[0]

    return pl.pallas_call(
        body, grid=(L,),
        in_specs=[pl.BlockSpec((B, D), lambda l: (0, 0)), pl.BlockSpec((1, D, N), lambda l: (l, 0, 0)),
                  pl.BlockSpec((1, 1, N), lambda l: (l, 0, 0))],
        out_specs=pl.BlockSpec((1, B, N), lambda l: (l, 0, 0)),
        out_shape=_sds((L, B, N), F32),
        compiler_params=_cp("arbitrary"), name="mods_project")(c_all, w, b)


def mods_weight_grad(c_all, dm):
    B, D = c_all.shape
    L, _, N = dm.shape

    def body(c_ref, d_ref, o_ref):
        cv = c_ref[...]
        cond = cv * _sigmoid(cv)
        o_ref[0] = lax.dot_general(cond, d_ref[0], (((0,), (0,)), ((), ())), preferred_element_type=F32,
                                   precision=lax.Precision.HIGHEST)

    return pl.pallas_call(
        body, grid=(L,),
        in_specs=[pl.BlockSpec((B, D), lambda l: (0, 0)), pl.BlockSpec((1, B, N), lambda l: (l, 0, 0))],
        out_specs=pl.BlockSpec((1, D, N), lambda l: (l, 0, 0)),
        out_shape=_sds((L, D, N), F32),
        compiler_params=_cp("arbitrary"), name="mods_weight_grad")(c_all, dm)


def _adam_math(g, w, m, v):
    m2 = ADAM_B1 * m + (1.0 - ADAM_B1) * g
    v2 = ADAM_B2 * v + (1.0 - ADAM_B2) * (g * g)
    m_hat = m2 / (1.0 - ADAM_B1 ** ADAM_STEP)
    v_hat = v2 / (1.0 - ADAM_B2 ** ADAM_STEP)
    delta = -ADAM_LR * (m_hat / (jnp.sqrt(v_hat) + ADAM_EPS) + ADAM_WD * w)
    return delta, m2, v2


def adam_update(g, w, m, v, parts, name):
    R, C = w.shape
    tr = _pick(R, 256, 8)

    def body(g_ref, w_ref, m_ref, v_ref, go_ref, d_ref, mo_ref, vo_ref):
        if parts:
            gv = g_ref[0].astype(F32)
            for s in range(1, N_DEV):
                gv = gv + g_ref[s].astype(F32)
        else:
            gv = g_ref[...]
        go_ref[...] = gv
        d_ref[...], mo_ref[...], vo_ref[...] = _adam_math(gv, w_ref[...], m_ref[...], v_ref[...])

    gspec = pl.BlockSpec((N_DEV, tr, C), lambda i: (0, i, 0)) if parts else _tok(tr, C)
    return pl.pallas_call(
        body, grid=(R // tr,),
        in_specs=[gspec, _tok(tr, C), _tok(tr, C), _tok(tr, C)],
        out_specs=[_tok(tr, C)] * 4, out_shape=[_sds((R, C), F32)] * 4,
        compiler_params=_cp("arbitrary"), name=name)(g, w, m, v)


def adam_layer(parts, w, m, v, prev, layer, after, name):
    L, R, C = w.shape
    tr = _pick(R, 256, 8)
    prev = (list(prev) if prev is not None else []) + [after]

    def body(p_ref, w_ref, m_ref, v_ref, *rest):
        go_ref, d_ref, mo_ref, vo_ref = rest[-4:]
        gv = p_ref[0].astype(F32)
        for s in range(1, N_DEV):
            gv = gv + p_ref[s].astype(F32)
        go_ref[...] = gv
        d_ref[...], mo_ref[...], vo_ref[...] = _adam_math(gv, w_ref[...], m_ref[...], v_ref[...])

    lay = pl.BlockSpec((None, tr, C), lambda i: (layer, i, 0))
    return pl.pallas_call(
        body, grid=(R // tr,),
        in_specs=[pl.BlockSpec((N_DEV, tr, C), lambda i: (0, i, 0)), lay, lay, lay] + [pl.BlockSpec(memory_space=pl.ANY)] * len(prev),
        out_specs=[lay] * 4, out_shape=[_sds((L, R, C), F32)] * 4,
        input_output_aliases={4 + k: k for k in range(len(prev) - 1)},
        compiler_params=_cp("arbitrary"), name=name)(parts, w, m, v, *prev)


def _my_id():
    return 4 * lax.axis_index("x") + 2 * lax.axis_index("y") + lax.axis_index("c")


def _peer(s):
    x, y, c = lax.axis_index("x"), lax.axis_index("y"), lax.axis_index("c")
    px = (1 - x) if s & 4 else x
    py = (1 - y) if s & 2 else y
    pc = (1 - c) if s & 1 else c
    return (px, py, pc), 4 * px + 2 * py + pc


def all_gather(xs, space, name):
    na = len(xs)

    def body(*refs):
        x_refs, o_refs = refs[:na], refs[na:2 * na]
        send_sems, recv_sems, local_sems = refs[2 * na:]
        me = _my_id()
        locals_, sends = [], []
        for a in range(na):
            cp = pltpu.make_async_copy(x_refs[a], o_refs[a].at[me], local_sems.at[a])
            cp.start()
            locals_.append(cp)
        for s in range(1, N_DEV):
            peer, _ = _peer(s)
            for a in range(na):
                cp = pltpu.make_async_remote_copy(
                    src_ref=x_refs[a], dst_ref=o_refs[a].at[me], send_sem=send_sems.at[a, s - 1],
                    recv_sem=recv_sems.at[a, s - 1], device_id=peer, device_id_type=MESH)
                cp.start()
                sends.append(cp)
        for s in range(1, N_DEV):
            peer, pid = _peer(s)
            for a in range(na):
                pltpu.make_async_remote_copy(
                    src_ref=x_refs[a], dst_ref=o_refs[a].at[pid], send_sem=send_sems.at[a, s - 1],
                    recv_sem=recv_sems.at[a, s - 1], device_id=peer, device_id_type=MESH).wait_recv()
        for cp in sends:
            cp.wait_send()
        for cp in locals_:
            cp.wait()

    spec = pl.BlockSpec(memory_space=space)
    return pl.pallas_call(
        body, in_specs=[spec] * na, out_specs=[spec] * na,
        out_shape=[_sds((N_DEV,) + x.shape, x.dtype) for x in xs],
        scratch_shapes=[pltpu.SemaphoreType.DMA((na, N_DEV - 1)), pltpu.SemaphoreType.DMA((na, N_DEV - 1)),
                        pltpu.SemaphoreType.DMA((na,))],
        compiler_params=pltpu.CompilerParams(vmem_limit_bytes=VMEM_LIMIT), name=name)(*xs)


def exchange_slots(xs, name):
    na = len(xs)

    def body(*refs):
        x_refs, o_refs = refs[:na], refs[na:2 * na]
        send_sems, recv_sems, local_sems = refs[2 * na:]
        me = _my_id()
        locals_, sends = [], []
        for a in range(na):
            cp = pltpu.make_async_copy(x_refs[a].at[me], o_refs[a].at[me], local_sems.at[a])
            cp.start()
            locals_.append(cp)
        for s in range(1, N_DEV):
            peer, pid = _peer(s)
            for a in range(na):
                cp = pltpu.make_async_remote_copy(
                    src_ref=x_refs[a].at[pid], dst_ref=o_refs[a].at[me], send_sem=send_sems.at[a, s - 1],
                    recv_sem=recv_sems.at[a, s - 1], device_id=peer, device_id_type=MESH)
                cp.start()
                sends.append(cp)
        for s in range(1, N_DEV):
            peer, pid = _peer(s)
            for a in range(na):
                pltpu.make_async_remote_copy(
                    src_ref=x_refs[a].at[pid], dst_ref=o_refs[a].at[pid], send_sem=send_sems.at[a, s - 1],
                    recv_sem=recv_sems.at[a, s - 1], device_id=peer, device_id_type=MESH).wait_recv()
        for cp in sends:
            cp.wait_send()
        for cp in locals_:
            cp.wait()

    spec = pl.BlockSpec(memory_space=pl.ANY)
    return pl.pallas_call(
        body, in_specs=[spec] * na, out_specs=[spec] * na,
        out_shape=[_sds(x.shape, x.dtype) for x in xs],
        scratch_shapes=[pltpu.SemaphoreType.DMA((na, N_DEV - 1)), pltpu.SemaphoreType.DMA((na, N_DEV - 1)),
                        pltpu.SemaphoreType.DMA((na,))],
        compiler_params=pltpu.CompilerParams(vmem_limit_bytes=VMEM_LIMIT), name=name)(*xs)


_HBM = pl.BlockSpec(memory_space=pltpu.HBM)
_SEM = pl.BlockSpec(memory_space=pltpu.SEMAPHORE)
_EFFECT = pltpu.SideEffectType.DATAFLOW_SIDE_EFFECTING


def _split_copies(pattern, x_ref, land_ref, send_sem, recv_sem):
    me = _my_id()
    if pattern in ("gather", "scatter"):
        plan = []
        for s in range(1, N_DEV):
            peer, pid = _peer(s)
            plan.append((x_ref.at[pid] if pattern == "scatter" else x_ref, land_ref.at[me], peer))
    elif pattern == "to_chips":
        plan = [(x_ref, land_ref.at[me], _peer(s)[0]) for s in (1, 2, 4, 6)]
    else:
        sibling = _peer(1)[0]
        plan = [(land_ref.at[_peer(s)[1]], land_ref.at[_peer(s)[1]], sibling) for s in (2, 4, 6)]
    return [pltpu.make_async_remote_copy(src_ref=src, dst_ref=dst, send_sem=send_sem, recv_sem=recv_sem,
                                         device_id=dev, device_id_type=MESH) for src, dst, dev in plan]


def comm_start(xs, pattern, after, name, lands=None):
    na = len(xs)
    extra = [] if after is None else [after]
    me = _my_id()
    if lands is None:
        lands = []
        for x in xs:
            shape = x.shape if pattern == "scatter" else (N_DEV,) + x.shape
            own = lax.dynamic_slice_in_dim(x, me, 1, 0) if pattern == "scatter" else x[None]
            lands.append(lax.dynamic_update_slice(lax.empty(shape, x.dtype), own, (me,) + (0,) * (len(shape) - 1)))

    def body(*refs):
        x_refs, land_refs = refs[:na], refs[na:2 * na]
        send_sem, recv_sem = refs[2 * na + len(extra)], refs[2 * na + len(extra) + 1]
        token = refs[-1]
        for a in range(na):
            for cp in _split_copies(pattern, x_refs[a], land_refs[a], send_sem, recv_sem):
                cp.start()
        token[...] = jnp.zeros_like(token)

    outs = pl.pallas_call(
        body, name=name,
        out_shape=(pltpu.SemaphoreType.DMA(()), pltpu.SemaphoreType.DMA(()))
        + tuple(pltpu.HBM(x.shape, x.dtype) for x in xs) + tuple(pltpu.HBM(l.shape, l.dtype) for l in lands)
        + (_sds((8, LANES), F32),),
        in_specs=(_HBM,) * (2 * na) + (pl.BlockSpec(memory_space=pl.ANY),) * len(extra),
        out_specs=(_SEM, _SEM) + (_HBM,) * (2 * na) + (pl.BlockSpec(memory_space=pltpu.VMEM),),
        input_output_aliases={a: 2 + a for a in range(2 * na)},
        compiler_params=pltpu.CompilerParams(has_side_effects=_EFFECT),
    )(*[pltpu.with_memory_space_constraint(x, pltpu.HBM) for x in xs],
      *[pltpu.with_memory_space_constraint(l, pltpu.HBM) for l in lands], *extra)
    return dict(sems=outs[0:2], xs=outs[2:2 + na], lands=outs[2 + na:2 + 2 * na], token=outs[-1], pattern=pattern)


def comm_wait(started, after, name, with_xs=False):
    xs, lands = started["xs"], started["lands"]
    pattern = started["pattern"]
    na = len(xs)

    def body(*refs):
        x_refs, land_refs = refs[:na], refs[na:2 * na]
        send_sem, recv_sem = refs[2 * na], refs[2 * na + 1]
        for a in range(na):
            for cp in _split_copies(pattern, x_refs[a], land_refs[a], send_sem, recv_sem):
                cp.wait_send()
                cp.wait_recv()

    outs = pl.pallas_call(
        body, name=name,
        out_shape=tuple(pltpu.HBM(x.shape, x.dtype) for x in xs) + tuple(pltpu.HBM(l.shape, l.dtype) for l in lands),
        in_specs=(_HBM,) * (2 * na) + (_SEM, _SEM, pl.BlockSpec(memory_space=pl.ANY)),
        out_specs=(_HBM,) * (2 * na),
        input_output_aliases={a: a for a in range(2 * na)},
        compiler_params=pltpu.CompilerParams(has_side_effects=_EFFECT),
    )(*xs, *lands, *started["sems"], after)
    return (list(outs[na:]), list(outs[:na])) if with_xs else list(outs[na:])


def _cols_to_natural(g):
    return jnp.concatenate([g[k] for k in range(N_DEV)], axis=1)


def _vec8(rows, d):
    rows = [r.reshape(1, d).astype(F32) for r in rows]
    return jnp.concatenate(rows + [jnp.zeros((8 - len(rows), d), F32)], axis=0)


def _ffn_forward(x, vec, w_in_t, w_out):
    xn, h, a, b, u, y = ffn_fwd(x, vec, w_in_t, w_out)
    return xn, (x, h, a, b, u, y)


def _ffn_backward(dxo, saved, vec, w_in_t, w_out, on_rows=None):
    x, h, a, b, u, y = saved
    dy, dab, dx, part = ffn_bwd(dxo, y, vec, w_out, w_in_t, a, b, x)
    rows = part[0:4]
    token = on_rows(rows) if on_rows is not None else None
    g_out = grad_slots(u, dy, "ffn_dw_out", after=token)
    g_in_t = grad_slots(dab, h, "ffn_dw_in", after=token)
    return dx, g_in_t, g_out, rows


_TRANSPOSED = ("ffn1_w_in", "ffn2_w_in", "attn_w_q")
_COL_NATURAL = ("conv_w_in", "w_kv", "attn_w_o")
_ROW_SHARDED = ("ffn1_w_out", "ffn2_w_out", "conv_w_out")
_BIG = _TRANSPOSED + _COL_NATURAL + _ROW_SHARDED


def weight_chunks():
    chunks = []
    for layer in range(DEPTH):
        first = [("ffn1_w_in", layer), ("ffn1_w_out", layer)]
        if layer == N_A_LAYERS:
            first = [("w_kv", layer)] + first
        mixer = [("conv_w_in", layer), ("conv_w_out", layer)] if layer < N_A_LAYERS else [("attn_w_q", layer), ("attn_w_o", layer)]
        rest = mixer + [("ffn2_w_in", layer), ("ffn2_w_out", layer)]
        chunks += [first, rest] if layer == 0 else [first + rest]
    return chunks


def stacked_index(name, layer):
    if name == "w_kv":
        return None
    return layer - N_A_LAYERS if name.startswith("attn") else layer


class ChunkComm:
    def __init__(self, shards):
        self.shards = shards
        self.chunks = weight_chunks()

    def _shard(self, name, layer):
        idx = stacked_index(name, layer)
        return self.shards[name][0 if idx is None else idx]

    def start_gather(self, ci, after):
        xs = [self._shard(n, l).astype(BF16) for n, l in self.chunks[ci]]
        return comm_start(xs, "to_chips", after, f"gather_start_{ci}")

    def relay_gather(self, ci, started, after):
        lands, xs = comm_wait(started, after, f"gather_wait_{ci}", with_xs=True)
        return comm_start(xs, "relay", None, f"gather_relay_{ci}", lands=lands)

    def finish_gather(self, ci, relayed, after):
        lands = comm_wait(relayed, after, f"gather_done_{ci}")
        W = {}
        for key, g in zip(self.chunks[ci], lands):
            W[key] = _cols_to_natural(g) if key[0] in _COL_NATURAL else g.reshape(-1, g.shape[2])
        return W

    def start_exchange(self, ci, slots, after):
        return comm_start([slots[key] for key in self.chunks[ci]], "scatter", after, f"exchange_start_{ci}")

    def finish_exchange(self, ci, started, after):
        lands = comm_wait(started, after, f"exchange_wait_{ci}")
        return dict(zip(self.chunks[ci], lands))


def device_step(x, positions, target, mods, kvmods, small, comm, gather0):
    T, D = x.shape
    groups = DILATED_GROUPS
    dils = [dil for _, dil in groups]
    lane = jnp.arange(LANES) % HEAD_DIM
    inv = ROPE_THETA ** (-jnp.arange(0, ROPE_DIM, 2, dtype=F32) / ROPE_DIM)
    lane_rows = _vec8([jnp.where(lane < ROPE_DIM, inv[lane % (ROPE_DIM // 2)], 0.0), lane < ROPE_DIM,
                       (lane >= ROPE_DIM // 2) & (lane < ROPE_DIM), lane < ROPE_DIM // 2], LANES)
    tabs = rope_tables(positions.reshape(T, 1), lane_rows)

    def after_token(v, token):
        return v if token is None else v + token[0, 0]

    def vec_of(layer, sub):
        return _vec8([small["norm_g"][layer, sub], mods[layer, 3 * sub], mods[layer, 3 * sub + 1], mods[layer, 3 * sub + 2]], D)

    saved = []
    kv_saved = None
    k_sh = v_sh = None
    qw = GROUP_WIDTH * len(groups)
    chunk_of = {key: ci for ci, chunk in enumerate(comm.chunks) for key in chunk}
    W = {}
    flight = {"ci": 0, "started": gather0}

    def need(key, after):
        if key not in W:
            ci = chunk_of[key]
            assert ci == flight["ci"], (key, ci)
            relayed = comm.relay_gather(ci, flight["started"], after)
            token = relayed["token"]
            if ci + 1 < len(comm.chunks):
                flight.update(ci=ci + 1, started=comm.start_gather(ci + 1, token))
                token = flight["started"]["token"]
            W.update(comm.finish_gather(ci, relayed, token))
        return W[key]

    for layer in range(DEPTH):
        if layer == N_A_LAYERS:
            w_kv = need(("w_kv", layer), x)
            kv_vec = _vec8([small["kv_norm_g"], kvmods[0], kvmods[1]], D)
            h_kv, *kv_pieces = proj_rope_fwd(x, kv_vec, w_kv, tabs, qw, False, dils, "kv_fwd")
            k_sh, v_sh = kv_pieces[:len(groups)], kv_pieces[len(groups):]
            kv_saved = (x, h_kv, kv_vec)
        rec = {}
        behind = tabs[0] if layer == 0 else x
        w_in, w_out = need(("ffn1_w_in", layer), behind), need(("ffn1_w_out", layer), behind)
        v1 = vec_of(layer, 0)
        x, rec["ffn1"] = _ffn_forward(x, v1, w_in, w_out)
        if layer < N_A_LAYERS:
            w_in, w_out = need(("conv_w_in", layer), x), need(("conv_w_out", layer), x)
            v2 = vec_of(layer, 1)
            cw = _vec8(list(small["conv_w"][layer]), D)
            x_in = x
            x, h, bcu, cv, z, y = conv_fwd(x, v2, cw, w_in, w_out)
            rec["mix"] = (x_in, h, bcu, cv, z, y, cw)
        else:
            w_q, w_o = need(("attn_w_q", layer), x), need(("attn_w_o", layer), x)
            v2 = vec_of(layer, 1)
            x_in = x
            h, *q = proj_rope_fwd(x, v2, w_q, tabs, qw, True, dils, "q_fwd")
            os_, ls = [], []
            for g, (win, dil) in enumerate(groups):
                o, l = attn_core_fwd(q[g], k_sh[g], v_sh[g], g, win // dil)
                os_.append(o)
                ls.append(l)
            x, mixed, y = attn_mix_out(os_, ls, dils, x, v2, w_o)
            rec["mix"] = (x_in, h, q, os_, ls, mixed, y)
        w_in, w_out = need(("ffn2_w_in", layer), x), need(("ffn2_w_out", layer), x)
        v3 = vec_of(layer, 2)
        x, rec["ffn2"] = _ffn_forward(x, v3, w_in, w_out)
        rec["vecs"] = (v1, v2, v3)
        saved.append(rec)

    dx, part_final, loss_tile = final_loss(x, _vec8([small["final_norm_g"]], D), target)
    loss = loss_tile[0, 0]

    conv_rows = [None] * N_A_LAYERS
    kv_rows = None
    mod_rows = [[None] * 3 for _ in range(DEPTH)]
    dkv_pairs = [{"k": [], "v": []} for _ in groups]
    slots = {}
    exchanges = []
    token = None

    def send_ready_chunks():
        nonlocal token
        for ci in reversed(range(len(comm.chunks))):
            if ci not in [e[0] for e in exchanges] and all(key in slots for key in comm.chunks[ci]):
                started = comm.start_exchange(ci, slots, token)
                exchanges.append((ci, started))
                token = started["token"]

    vector_gather = {}

    def start_vector_gather(rows0):
        mod_rows[0][0] = rows0
        rows = jnp.stack([jnp.stack(r) for r in mod_rows])
        vecs = jnp.concatenate([rows[:, :, 1:4].reshape(-1), kv_rows[1:3].reshape(-1), kv_rows[0], part_final[0],
                                rows[:, :, 0].reshape(-1), jnp.stack(conv_rows).reshape(-1)])
        vector_gather["count"] = vecs.shape[0]
        vecs = _pad_rows(vecs.reshape(-1, 1), 8 * LANES).reshape(-1, LANES)
        vector_gather["started"] = comm_start([vecs], "gather", None, "vector_grads_start")
        return vector_gather["started"]["token"]

    for layer in reversed(range(DEPTH)):
        rec = saved[layer]
        v1, v2, v3 = rec["vecs"]
        dx, slots[("ffn2_w_in", layer)], slots[("ffn2_w_out", layer)], mod_rows[layer][2] = _ffn_backward(
            dx, rec["ffn2"], after_token(v3, token), W[("ffn2_w_in", layer)], W[("ffn2_w_out", layer)])
        if layer < N_A_LAYERS:
            x_in, h, bcu, cv, z, y, cw = rec["mix"]
            dx, dy, dbcu, part, dcw = conv_bwd(dx, x_in, y, bcu, cv, v2, cw, W[("conv_w_in", layer)], W[("conv_w_out", layer)])
            slots[("conv_w_out", layer)] = grad_slots(z, dy, "conv_dw_out")
            slots[("conv_w_in", layer)] = grad_slots(h, dbcu, "conv_dw_in", col_slots=True)
            conv_rows[layer] = dcw[0:3]
            mod_rows[layer][1] = part[0:4]
        else:
            x_in, h, q, os_, ls, mixed, y = rec["mix"]
            outs = attn_mix_bwd(dx, y, v2, W[("attn_w_o", layer)], os_, ls, dils)
            ng = len(groups)
            dy, dos, deltas, part_gate = outs[0], outs[1:1 + ng], outs[1 + ng:1 + 2 * ng], outs[1 + 2 * ng]
            slots[("attn_w_o", layer)] = grad_slots(mixed, dy, "attn_dw_o", col_slots=True)
            dqs = []
            for g, (win, dil) in enumerate(groups):
                dq, dkc, dkp, dvc, dvp = attn_core_bwd(q[g], k_sh[g], v_sh[g], dos[g], deltas[g], ls[g], g, win // dil)
                dqs.append(dq)
                dkv_pairs[g]["k"].append((dkc, dkp))
                dkv_pairs[g]["v"].append((dvc, dvp))
            dx, dqr, part_norm = proj_rope_bwd(dqs, dils, x_in, dx, v2, W[("attn_w_q", layer)], tabs, qw, True, "q_bwd")
            slots[("attn_w_q", layer)] = grad_slots(dqr, h, "attn_dw_q")
            mod_rows[layer][1] = jnp.concatenate([part_norm[0:3], part_gate[0:1]], axis=0)
        send_ready_chunks()
        dx, slots[("ffn1_w_in", layer)], slots[("ffn1_w_out", layer)], mod_rows[layer][0] = _ffn_backward(
            dx, rec["ffn1"], after_token(v1, token), W[("ffn1_w_in", layer)], W[("ffn1_w_out", layer)],
            on_rows=start_vector_gather if layer == 0 else None)
        if layer == N_A_LAYERS:
            x_kv, h_kv, kv_vec = kv_saved
            dparts = [dkv_combine(dkv_pairs[g]["k"], win // dil, f"dk_combine_g{g}") for g, (win, dil) in enumerate(groups)]
            dparts += [dkv_combine(dkv_pairs[g]["v"], win // dil, f"dv_combine_g{g}") for g, (win, dil) in enumerate(groups)]
            dx, dkvp, part_kv = proj_rope_bwd(dparts, dils, x_kv, dx, kv_vec, W[("w_kv", layer)], tabs, qw, False, "kv_bwd")
            slots[("w_kv", layer)] = grad_slots(h_kv, dkvp, "kv_dw", col_slots=True)
            kv_rows = part_kv[0:3]
        send_ready_chunks()

    return loss, dx, {"exchanges": exchanges, "vector_gather": vector_gather}


def _flat2(a):
    return a.reshape(-1, a.shape[-1])


def _pad_rows(a, mult):
    r = a.shape[0]
    pad = (-r) % mult
    return a if pad == 0 else jnp.concatenate([a, jnp.zeros((pad,) + a.shape[1:], a.dtype)], axis=0)


def kernel(x, c, positions, norm_g, ada_w, ada_b, ffn1_w_in, ffn1_w_out, ffn2_w_in, ffn2_w_out, conv_w_in, conv_w, conv_w_out, kv_norm_g, kv_ada_w, kv_ada_b, w_kv, attn_w_q, attn_w_o, final_norm_g, loss_target, m_norm_g, m_ada_w, m_ada_b, m_ffn1_w_in, m_ffn1_w_out, m_ffn2_w_in, m_ffn2_w_out, m_conv_w_in, m_conv_w, m_conv_w_out, m_kv_norm_g, m_kv_ada_w, m_kv_ada_b, m_w_kv, m_attn_w_q, m_attn_w_o, m_final_norm_g, v_norm_g, v_ada_w, v_ada_b, v_ffn1_w_in, v_ffn1_w_out, v_ffn2_w_in, v_ffn2_w_out, v_conv_w_in, v_conv_w, v_conv_w_out, v_kv_norm_g, v_kv_ada_w, v_kv_ada_b, v_w_kv, v_attn_w_q, v_attn_w_o, v_final_norm_g):
    names = ("norm_g", "ada_w", "ada_b", "ffn1_w_in", "ffn1_w_out", "ffn2_w_in", "ffn2_w_out", "conv_w_in", "conv_w",
             "conv_w_out", "kv_norm_g", "kv_ada_w", "kv_ada_b", "w_kv", "attn_w_q", "attn_w_o", "final_norm_g")
    wts = dict(zip(names, (norm_g, ada_w, ada_b, ffn1_w_in, ffn1_w_out, ffn2_w_in, ffn2_w_out, conv_w_in, conv_w, conv_w_out,
                           kv_norm_g, kv_ada_w, kv_ada_b, w_kv, attn_w_q, attn_w_o, final_norm_g)))
    mom = dict(zip(names, (m_norm_g, m_ada_w, m_ada_b, m_ffn1_w_in, m_ffn1_w_out, m_ffn2_w_in, m_ffn2_w_out, m_conv_w_in,
                           m_conv_w, m_conv_w_out, m_kv_norm_g, m_kv_ada_w, m_kv_ada_b, m_w_kv, m_attn_w_q, m_attn_w_o,
                           m_final_norm_g)))
    var = dict(zip(names, (v_norm_g, v_ada_w, v_ada_b, v_ffn1_w_in, v_ffn1_w_out, v_ffn2_w_in, v_ffn2_w_out, v_conv_w_in,
                           v_conv_w, v_conv_w_out, v_kv_norm_g, v_kv_ada_w, v_kv_ada_b, v_w_kv, v_attn_w_q, v_attn_w_o,
                           v_final_norm_g)))
    T, D = x.shape[1], x.shape[2]
    me = _my_id()
    nmod = ada_w.shape[2]
    nkv = kv_ada_w.shape[1]

    def stacked(w, n):
        w = w if w.ndim == 3 else w[None]
        return jnp.swapaxes(w, 1, 2) if n in _TRANSPOSED else w

    comm = ChunkComm({n: stacked(wts[n], n) for n in _BIG})
    W = {}

    ds = norm_g.shape[2]
    small = jnp.concatenate([c.reshape(-1), norm_g.reshape(-1), conv_w.reshape(-1)]).astype(F32)
    n_small = small.shape[0]
    small = _pad_rows(small.reshape(-1, 1), 8 * LANES).reshape(-1, LANES)
    (small_all,) = all_gather([small], pltpu.VMEM, "gather_small")
    small_all = small_all.reshape(N_DEV, -1)[:, :n_small]
    c_all = small_all[:, :D]
    def full_rows(off, count):
        return jnp.stack([small_all[:, off + i * ds:off + (i + 1) * ds].reshape(D) for i in range(count)])

    W["norm_g"] = full_rows(D, DEPTH * 3).reshape(DEPTH, 3, D)
    W["conv_w"] = full_rows(D + DEPTH * 3 * ds, N_A_LAYERS * 3).reshape(N_A_LAYERS, 3, D)
    W["kv_norm_g"], W["final_norm_g"] = kv_norm_g, final_norm_g

    ada_b_mine = lax.dynamic_slice_in_dim(ada_b, me * nmod, nmod, axis=1).reshape(DEPTH, 1, nmod)
    kv_b_mine = lax.dynamic_slice_in_dim(kv_ada_b, me * nkv, nkv, axis=0).reshape(1, 1, nkv)
    mods_cols = mods_project(c_all, ada_w, ada_b_mine)
    kv_cols = mods_project(c_all, kv_ada_w.reshape(1, D, nkv), kv_b_mine)
    mcat = jnp.concatenate([mods_cols[l] for l in range(DEPTH)] + [kv_cols[0]], axis=1)
    wm = mcat.shape[1]
    if wm % LANES:
        mcat = jnp.concatenate([mcat, jnp.zeros((N_DEV, LANES - wm % LANES), F32)], axis=1)
    (mods_all,) = exchange_slots([mcat.reshape(N_DEV, 1, -1)], "exchange_mods")
    gather0 = comm.start_gather(0, mods_all)
    mods_all = mods_all.reshape(N_DEV, -1)
    mods = jnp.stack([mods_all[:, l * nmod:(l + 1) * nmod].reshape(N_MOD, D) for l in range(DEPTH)])
    kvmods = mods_all[:, DEPTH * nmod:DEPTH * nmod + nkv].reshape(2, D)

    loss_local, dx, grads = device_step(x[0], positions[0], loss_target[0], mods, kvmods, W, comm, gather0)
    loss = lax.psum(loss_local, MESH_AXES)

    (vec_all,) = comm_wait(grads["vector_gather"]["started"], grads["exchanges"][-1][1]["token"], "vector_grads_wait")
    vec_all = vec_all.reshape(N_DEV, -1)[:, :grads["vector_gather"]["count"]]
    nm_, nk_ = DEPTH * N_MOD * D, 2 * D
    dmods_all = vec_all[:, :nm_].reshape(N_DEV, DEPTH, N_MOD * D)
    dkvm_all = vec_all[:, nm_:nm_ + nk_]
    rest = vec_all[:, nm_ + nk_:]
    parts_kv_norm, parts_final = rest[:, :D].reshape(N_DEV, 1, D), rest[:, D:2 * D].reshape(N_DEV, 1, D)
    parts_norm = lax.dynamic_slice_in_dim(rest[:, 2 * D:2 * D + DEPTH * 3 * D].reshape(N_DEV, DEPTH * 3, D), me * ds, ds, axis=2)
    parts_conv = lax.dynamic_slice_in_dim(rest[:, 2 * D + DEPTH * 3 * D:].reshape(N_DEV, N_A_LAYERS * 3, D), me * ds, ds, axis=2)
    dm_cols = lax.dynamic_slice_in_dim(dmods_all, me * nmod, nmod, axis=2)
    dm_mine = jnp.stack([dm_cols[:, l] for l in range(DEPTH)])
    dkv_mine = lax.dynamic_slice_in_dim(dkvm_all, me * nkv, nkv, axis=1).reshape(1, N_DEV, nkv)
    g_ada_w = mods_weight_grad(c_all, dm_mine)
    g_kv_ada_w = mods_weight_grad(c_all, dkv_mine)[0]

    out_g, out_d, out_m, out_v = {}, {}, {}, {}

    def update(n, g, w, parts=False):
        shp = w.shape
        w2 = w.reshape(1, -1) if w.ndim == 1 else _flat2(w)
        g2 = g if parts else g.reshape(w2.shape)
        res = adam_update(g2, w2, mom[n].reshape(w2.shape), var[n].reshape(w2.shape), parts, "adam_" + n)
        out_g[n], out_d[n], out_m[n], out_v[n] = (r.reshape(shp) for r in res)

    moms = {n: stacked(mom[n], n) for n in _BIG}
    vars_ = {n: stacked(var[n], n) for n in _BIG}
    results = {}
    after = dx
    for ci, started in grads["exchanges"]:
        for (n, layer), parts in comm.finish_exchange(ci, started, after).items():
            idx = stacked_index(n, layer)
            results[n] = adam_layer(parts, comm.shards[n], moms[n], vars_[n], results.get(n), 0 if idx is None else idx,
                                    after, f"adam_{n}_{layer}")
            after = results[n][1]
    for n in _BIG:
        res = [jnp.swapaxes(r, 1, 2) if n in _TRANSPOSED else r for r in results[n]]
        out_g[n], out_d[n], out_m[n], out_v[n] = (r.reshape(wts[n].shape) for r in res)
    update("ada_w", g_ada_w, ada_w)
    update("kv_ada_w", g_kv_ada_w, kv_ada_w)
    update("ada_b", dmods_all, ada_b, True)
    update("kv_ada_b", dkvm_all.reshape(N_DEV, 1, nk_), kv_ada_b, True)
    update("kv_norm_g", parts_kv_norm, kv_norm_g, True)
    update("final_norm_g", parts_final, final_norm_g, True)
    update("norm_g", parts_norm, norm_g, True)
    update("conv_w", parts_conv, conv_w, True)

    return (loss, dx.reshape(x.shape), *[out_g[n] for n in names], *[out_d[n] for n in names],
            *[out_m[n] for n in names], *[out_v[n] for n in names])
```

```python
import functools

import jax
import jax.numpy as jnp
from jax import lax
from jax.experimental import pallas as pl
from jax.experimental.pallas import tpu as pltpu

F32, BF16 = jnp.float32, jnp.bfloat16

N_DEV = 8
MESH_AXES = ("x", "y", "c")
DEPTH = 4
N_A_LAYERS = 2
HEAD_DIM = 64
HEADS_PER_GROUP = 8
GROUP_WIDTH = HEAD_DIM * HEADS_PER_GROUP
DILATED_GROUPS = ((128, 1), (512, 4), (2048, 16))
ROPE_DIM = HEAD_DIM // 4
ROPE_THETA = 500000.0
NORM_EPS = 1e-5
FFN_RES_WEIGHT = 0.5
N_MOD = 9
ADAM_LR, ADAM_B1, ADAM_B2, ADAM_EPS, ADAM_WD, ADAM_STEP = 0.001, 0.9, 0.999, 1e-08, 0.01, 10

LANES = 128
TOKEN_TILE = 512
FFN_BWD_TILE = 256
CONTRACT_TILE = 4096
GRAD_COLS = 768
MXU_WIDTH = 256
VMEM_LIMIT = 56 * 1024 * 1024
MESH = pl.DeviceIdType.MESH


def _cp(*sem):
    return pltpu.CompilerParams(dimension_semantics=sem, vmem_limit_bytes=VMEM_LIMIT)


def _pick(n, cap, mult=LANES):
    if n <= cap:
        return n
    best = None
    for t in range(mult, cap + 1, mult):
        if n % t == 0:
            best = t
    assert best is not None, (n, cap)
    return best


def _tok(tm, w):
    return pl.BlockSpec((tm, w), lambda i: (i, 0))


def _res(shape):
    nd = len(shape)
    return pl.BlockSpec(shape, lambda *_: (0,) * nd, pipeline_mode=pl.Buffered(1))


def _sds(shape, dt):
    return jax.ShapeDtypeStruct(shape, dt)


def _sigmoid(a):
    return 1.0 / (1.0 + jnp.exp(-a))


def _modnorm(x, g, sh, sc):
    r = lax.rsqrt(jnp.mean(x * x, axis=-1, keepdims=True) + NORM_EPS)
    return (x * r * g) * (1.0 + sc) + sh


def _dot(a, b):
    return jnp.dot(a, b, preferred_element_type=F32)


def _dot_nt(a, b):
    return lax.dot_general(a, b, (((1,), (1,)), ((), ())), preferred_element_type=F32)


def _dot_tn(a, b):
    return lax.dot_general(a, b, (((0,), (0,)), ((), ())), preferred_element_type=F32)


def _rows8(rows, d):
    pad = 8 - len(rows)
    return jnp.concatenate(list(rows) + [jnp.zeros((pad, d), F32)], axis=0)


def _acc_rows(ref, tile, first):
    @pl.when(first)
    def _():
        ref[...] = tile

    @pl.when(jnp.logical_not(first))
    def _():
        ref[...] += tile


def ffn_fwd(x, vec, w_in_t, w_out):
    T, D = x.shape
    F = w_in_t.shape[0] // 2
    tm, cw = min(TOKEN_TILE, T), _pick(F, MXU_WIDTH)

    def body(x_ref, vec_ref, wi_ref, wo_ref, xn_ref, h_ref, ga_ref, gb_ref, u_ref, y_ref):
        x_t = x_ref[...]
        hb = _modnorm(x_t, vec_ref[0:1], vec_ref[1:2], vec_ref[2:3]).astype(BF16)
        h_ref[...] = hb
        for c in range(F // cw):
            lo, hi = c * cw, (c + 1) * cw
            a = _dot_nt(hb, wi_ref[lo:hi, :])
            b = _dot_nt(hb, wi_ref[F + lo:F + hi, :])
            sg = _sigmoid(a)
            silu = a * sg
            ga_ref[:, lo:hi] = (b * (sg + silu * (1.0 - sg))).astype(BF16)
            gb_ref[:, lo:hi] = silu.astype(BF16)
            u_ref[:, lo:hi] = (silu * b).astype(BF16)
        y = _dot(u_ref[...], wo_ref[...])
        y_ref[...] = y.astype(BF16)
        xn_ref[...] = x_t + (FFN_RES_WEIGHT * (1.0 + vec_ref[3:4])) * y

    return pl.pallas_call(
        body, grid=(T // tm,),
        in_specs=[_tok(tm, D), _res((8, D)), _res((2 * F, D)), _res((F, D))],
        out_specs=[_tok(tm, D), _tok(tm, D), _tok(tm, F), _tok(tm, F), _tok(tm, F), _tok(tm, D)],
        out_shape=[_sds((T, D), F32), _sds((T, D), BF16), _sds((T, F), BF16), _sds((T, F), BF16), _sds((T, F), BF16),
                   _sds((T, D), BF16)],
        compiler_params=_cp("arbitrary"), name="ffn_fwd")(x, vec, w_in_t, w_out)


def ffn_bwd(dxo, y, vec, w_out, w_in_t, a, b, x):
    T, D = x.shape
    F = a.shape[1]
    tm, cw = min(FFN_BWD_TILE, T), _pick(F, MXU_WIDTH)

    def body(dxo_ref, y_ref, vec_ref, wo_ref, wi_ref, a_ref, b_ref, x_ref, dy_ref, dab_ref, dx_ref, part_ref):
        dxo_t = dxo_ref[...]
        dyb = (dxo_t * (FFN_RES_WEIGHT * (1.0 + vec_ref[3:4]))).astype(BF16)
        dy_ref[...] = dyb
        dgate = FFN_RES_WEIGHT * jnp.sum(dxo_t * y_ref[...].astype(F32), axis=0, keepdims=True)
        for c in range(F // cw):
            lo, hi = c * cw, (c + 1) * cw
            du = _dot_nt(dyb, wo_ref[lo:hi, :])
            dab_ref[:, lo:hi] = (du * a_ref[:, lo:hi].astype(F32)).astype(BF16)
            dab_ref[:, F + lo:F + hi] = (du * b_ref[:, lo:hi].astype(F32)).astype(BF16)
        dh = _dot(dab_ref[...], wi_ref[...])
        _, vjp = jax.vjp(_modnorm, x_ref[...], vec_ref[0:1], vec_ref[1:2], vec_ref[2:3])
        dx, dg, dsh, dsc = vjp(dh)
        dx_ref[...] = dxo_t + dx
        _acc_rows(part_ref, _rows8([dg, dsh, dsc, dgate], D), pl.program_id(0) == 0)

    return pl.pallas_call(
        body, grid=(T // tm,),
        in_specs=[_tok(tm, D), _tok(tm, D), _res((8, D)), _res((F, D)), _res((2 * F, D)), _tok(tm, F), _tok(tm, F), _tok(tm, D)],
        out_specs=[_tok(tm, D), _tok(tm, 2 * F), _tok(tm, D), pl.BlockSpec((8, D), lambda i: (0, 0))],
        out_shape=[_sds((T, D), BF16), _sds((T, 2 * F), BF16), _sds((T, D), F32), _sds((8, D), F32)],
        compiler_params=_cp("arbitrary"), name="ffn_bwd")(dxo, y, vec, w_out, w_in_t, a, b, x)


def grad_slots(a, b, name, col_slots=False, after=None):
    T, M = a.shape
    extra = [] if after is None else [after]
    N = b.shape[1]
    tk = min(CONTRACT_TILE, T)
    nk = T // tk
    tmm = _pick(M, 1408)
    if col_slots:
        ns = N // N_DEV
        sp = max(s for s in (1, 2, 4, 8) if ns * s <= GRAD_COLS or s == 1)
        tn = ns * sp
    else:
        tn = _pick(N, GRAD_COLS)

    def body(a_ref, b_ref, *rest):
        o_ref, acc = rest[-2:]
        k = pl.program_id(2)
        t = _dot_tn(a_ref[...], b_ref[...])

        @pl.when(k == 0)
        def _():
            acc[...] = t

        @pl.when(k > 0)
        def _():
            acc[...] += t

        @pl.when(k == nk - 1)
        def _():
            if col_slots:
                for s in range(sp):
                    o_ref[s] = acc[:, s * ns:(s + 1) * ns].astype(BF16)
            else:
                o_ref[...] = acc[...].astype(BF16)

    if col_slots:
        out_spec, out_shape = pl.BlockSpec((sp, tmm, ns), lambda i, j, k: (j, i, 0)), _sds((N_DEV, M, ns), BF16)
    else:
        out_spec, out_shape = pl.BlockSpec((tmm, tn), lambda i, j, k: (i, j)), _sds((M, N), BF16)
    out = pl.pallas_call(
        body, grid=(M // tmm, N // tn, nk),
        in_specs=[pl.BlockSpec((tk, tmm), lambda i, j, k: (k, i)), pl.BlockSpec((tk, tn), lambda i, j, k: (k, j))]
        + [pl.BlockSpec(memory_space=pl.ANY)] * len(extra),
        out_specs=out_spec, out_shape=out_shape,
        scratch_shapes=[pltpu.VMEM((tmm, tn), F32)],
        compiler_params=_cp("arbitrary", "arbitrary", "arbitrary"), name=name)(a, b, *extra)
    return out if col_slots else out.reshape(N_DEV, M // N_DEV, N)


def conv_fwd(x, vec, cw, w_in, w_out):
    T, D = x.shape
    tm = min(TOKEN_TILE, T)

    def body(x_ref, vec_ref, cw_ref, wi_ref, wo_ref, xn_ref, h_ref, bcu_ref, cv_ref, z_ref, y_ref, vbuf):
        @pl.when(pl.program_id(0) == 0)
        def _():
            vbuf[0:8, :] = jnp.zeros((8, D), F32)

        x_t = x_ref[...]
        hb = _modnorm(x_t, vec_ref[0:1], vec_ref[1:2], vec_ref[2:3]).astype(BF16)
        h_ref[...] = hb
        bcu = _dot(hb, wi_ref[...])
        bcu_ref[...] = bcu.astype(BF16)
        bg, v = bcu[:, 0:D], bcu[:, D:2 * D] * bcu[:, 2 * D:3 * D]
        vbuf[8:8 + tm, :] = v
        conv = cw_ref[0:1] * vbuf[6:6 + tm, :] + cw_ref[1:2] * vbuf[7:7 + tm, :] + cw_ref[2:3] * v
        cv_ref[...] = conv.astype(BF16)
        zb = (bg * conv).astype(BF16)
        z_ref[...] = zb
        y = _dot(zb, wo_ref[...])
        y_ref[...] = y.astype(BF16)
        xn_ref[...] = x_t + (1.0 + vec_ref[3:4]) * y
        vbuf[0:8, :] = vbuf[tm:tm + 8, :]

    return pl.pallas_call(
        body, grid=(T // tm,),
        in_specs=[_tok(tm, D), _res((8, D)), _res((8, D)), _res((D, 3 * D)), _res((D, D))],
        out_specs=[_tok(tm, D), _tok(tm, D), _tok(tm, 3 * D), _tok(tm, D), _tok(tm, D), _tok(tm, D)],
        out_shape=[_sds((T, D), F32), _sds((T, D), BF16), _sds((T, 3 * D), BF16), _sds((T, D), BF16),
                   _sds((T, D), BF16), _sds((T, D), BF16)],
        scratch_shapes=[pltpu.VMEM((tm + 8, D), F32)],
        compiler_params=_cp("arbitrary"), name="conv_fwd")(x, vec, cw, w_in, w_out)


def conv_bwd(dxo, x, y, bcu, cv, vec, cw, w_in, w_out):
    T, D = x.shape
    tm = min(TOKEN_TILE, T)
    nt = T // tm

    def body(dxo_ref, x_ref, y_ref, bcu_ref, cv_ref, vec_ref, cw_ref, wi_ref, wo_ref,
             dx_ref, dy_ref, dbcu_ref, part_ref, dcw_ref, dcbuf):
        first = pl.program_id(0) == 0

        @pl.when(first)
        def _():
            dcbuf[tm:tm + 8, :] = jnp.zeros((8, D), F32)

        dxo_t = dxo_ref[...]
        dyb = (dxo_t * (1.0 + vec_ref[3:4])).astype(BF16)
        dy_ref[...] = dyb
        dgate = jnp.sum(dxo_t * y_ref[...].astype(F32), axis=0, keepdims=True)
        dz = _dot_nt(dyb, wo_ref[...])
        bcu_t = bcu_ref[...].astype(F32)
        bg, cg, ug = bcu_t[:, 0:D], bcu_t[:, D:2 * D], bcu_t[:, 2 * D:3 * D]
        dconv = dz * bg
        dbg = dz * cv_ref[...].astype(F32)
        dcbuf[0:tm, :] = dconv
        d1, d2 = dcbuf[1:tm + 1, :], dcbuf[2:tm + 2, :]
        dv = cw_ref[2:3] * dconv + cw_ref[1:2] * d1 + cw_ref[0:1] * d2
        v = cg * ug
        dcw = _rows8([jnp.sum(d2 * v, axis=0, keepdims=True), jnp.sum(d1 * v, axis=0, keepdims=True),
                      jnp.sum(dconv * v, axis=0, keepdims=True)], D)
        dbcu = jnp.concatenate([dbg, dv * ug, dv * cg], axis=1).astype(BF16)
        dbcu_ref[...] = dbcu
        dh = _dot_nt(dbcu, wi_ref[...])
        _, vjp = jax.vjp(_modnorm, x_ref[...], vec_ref[0:1], vec_ref[1:2], vec_ref[2:3])
        dx, dg, dsh, dsc = vjp(dh)
        dx_ref[...] = dxo_t + dx
        _acc_rows(part_ref, _rows8([dg, dsh, dsc, dgate], D), first)
        _acc_rows(dcw_ref, dcw, first)
        dcbuf[tm:tm + 8, :] = dcbuf[0:8, :]

    def rev(w):
        return pl.BlockSpec((tm, w), lambda i: (nt - 1 - i, 0))

    return pl.pallas_call(
        body, grid=(nt,),
        in_specs=[rev(D), rev(D), rev(D), rev(3 * D), rev(D), _res((8, D)), _res((8, D)), _res((D, 3 * D)), _res((D, D))],
        out_specs=[rev(D), rev(D), rev(3 * D), pl.BlockSpec((8, D), lambda i: (0, 0)), pl.BlockSpec((8, D), lambda i: (0, 0))],
        out_shape=[_sds((T, D), F32), _sds((T, D), BF16), _sds((T, 3 * D), BF16), _sds((8, D), F32), _sds((8, D), F32)],
        scratch_shapes=[pltpu.VMEM((tm + 8, D), F32)],
        compiler_params=_cp("arbitrary"), name="conv_bwd")(dxo, x, y, bcu, cv, vec, cw, w_in, w_out)


def rope_tables(pos, lane_rows):
    T = pos.shape[0]
    tm = min(TOKEN_TILE, T)

    def body(p_ref, lr_ref, c_ref, sp_ref, sm_ref):
        ang = p_ref[...].astype(F32) * lr_ref[0:1]
        cs, sn = jnp.cos(ang), jnp.sin(ang)
        c_ref[...] = jnp.where(lr_ref[1:2] > 0.5, cs, 1.0)
        sp_ref[...] = jnp.where(lr_ref[2:3] > 0.5, sn, 0.0)
        sm_ref[...] = jnp.where(lr_ref[3:4] > 0.5, -sn, 0.0)

    return pl.pallas_call(
        body, grid=(T // tm,),
        in_specs=[_tok(tm, 1), _res((8, LANES))],
        out_specs=[_tok(tm, LANES)] * 3,
        out_shape=[_sds((T, LANES), F32)] * 3,
        compiler_params=_cp("arbitrary"), name="rope_tables")(pos, lane_rows)


def _rope(t, c, sp, sm):
    w = t.shape[1]
    reps = w // LANES
    cf, spf, smf = jnp.tile(c, (1, reps)), jnp.tile(sp, (1, reps)), jnp.tile(sm, (1, reps))
    half = ROPE_DIM // 2
    return t * cf + pltpu.roll(t, half, axis=1) * spf + pltpu.roll(t, w - half, axis=1) * smf


def _rope_t(d, c, sp, sm):
    w = d.shape[1]
    reps = w // LANES
    cf, spf, smf = jnp.tile(c, (1, reps)), jnp.tile(sp, (1, reps)), jnp.tile(sm, (1, reps))
    half = ROPE_DIM // 2
    return d * cf + pltpu.roll(d * spf, w - half, axis=1) + pltpu.roll(d * smf, half, axis=1)


def _split_residues(v, d, stage):
    tm, width = v.shape
    if d == 1:
        return [v]
    nj = width // LANES
    for j in range(nj):
        stage[j] = v[:, j * LANES:(j + 1) * LANES]
    return [jnp.concatenate([stage[j, pl.ds(r, tm // d, stride=d), :] for j in range(nj)], axis=1) for r in range(d)]


def _merge_residues(piece, d, tm, width, stage):
    if d == 1:
        return piece(0)
    nj = width // LANES
    for r in range(d):
        p = piece(r)
        for j in range(nj):
            stage[j, pl.ds(r, tm // d, stride=d), :] = p[:, j * LANES:(j + 1) * LANES]
    return jnp.concatenate([stage[j] for j in range(nj)], axis=1)


def _residue_spec(d, tm, width=GROUP_WIDTH):
    return pl.BlockSpec((d, tm // d, width), lambda i: (0, i, 0))


def _stage_scratch(tm):
    return pltpu.VMEM((GROUP_WIDTH // LANES, tm, LANES), F32)


def proj_rope_fwd(x, vec, w, tabs, n_rope, transposed, dils, name):
    T, D = x.shape
    N = w.shape[0] if transposed else w.shape[1]
    tm = min(TOKEN_TILE, T)
    GW = GROUP_WIDTH
    piece_dils = [dils[j % len(dils)] for j in range(N // GW)]

    def body(x_ref, vec_ref, w_ref, c_ref, sp_ref, sm_ref, h_ref, *rest):
        out_refs, stage = rest[:-1], rest[-1]
        hb = _modnorm(x_ref[...], vec_ref[0:1], vec_ref[1:2], vec_ref[2:3]).astype(BF16)
        h_ref[...] = hb
        p = _dot_nt(hb, w_ref[...]) if transposed else _dot(hb, w_ref[...])
        pr = _rope(p[:, 0:n_rope], c_ref[...], sp_ref[...], sm_ref[...])
        for j, d in enumerate(piece_dils):
            src = pr if (j + 1) * GW <= n_rope else p
            for r, rows in enumerate(_split_residues(src[:, j * GW:(j + 1) * GW], d, stage)):
                out_refs[j][r] = rows.astype(BF16)

    return pl.pallas_call(
        body, grid=(T // tm,),
        in_specs=[_tok(tm, D), _res((8, D)), _res(w.shape)] + [_tok(tm, LANES)] * 3,
        out_specs=[_tok(tm, D)] + [_residue_spec(d, tm) for d in piece_dils],
        out_shape=[_sds((T, D), BF16)] + [_sds((d, T // d, GW), BF16) for d in piece_dils],
        scratch_shapes=[_stage_scratch(tm)],
        compiler_params=_cp("arbitrary"), name=name)(x, vec, w, *tabs)


def proj_rope_bwd(dparts, dils, x, dxo, vec, w, tabs, n_rope, transposed, name):
    T, D = x.shape
    N = w.shape[0] if transposed else w.shape[1]
    tm = min(TOKEN_TILE, T)
    GW = GROUP_WIDTH
    npart = len(dparts)
    piece_dils = [dils[j % len(dils)] for j in range(npart)]

    def body(*refs):
        d_refs = refs[:npart]
        x_ref, dxo_ref, vec_ref, w_ref, c_ref, sp_ref, sm_ref, dx_ref, dp_ref, part_ref, stage = refs[npart:]
        d = jnp.concatenate([_merge_residues(lambda r, ref=ref: ref[r].astype(F32), dd, tm, GW, stage)
                             for ref, dd in zip(d_refs, piece_dils)], axis=1)
        dr = _rope_t(d[:, 0:n_rope], c_ref[...], sp_ref[...], sm_ref[...])
        if n_rope < N:
            dr = jnp.concatenate([dr, d[:, n_rope:N]], axis=1)
        dpb = dr.astype(BF16)
        dp_ref[...] = dpb
        dh = _dot(dpb, w_ref[...]) if transposed else _dot_nt(dpb, w_ref[...])
        _, vjp = jax.vjp(_modnorm, x_ref[...], vec_ref[0:1], vec_ref[1:2], vec_ref[2:3])
        dx, dg, dsh, dsc = vjp(dh)
        dx_ref[...] = dxo_ref[...] + dx
        _acc_rows(part_ref, _rows8([dg, dsh, dsc], D), pl.program_id(0) == 0)

    return pl.pallas_call(
        body, grid=(T // tm,),
        in_specs=[_residue_spec(d, tm) for d in piece_dils] + [_tok(tm, D), _tok(tm, D), _res((8, D)), _res(w.shape)]
        + [_tok(tm, LANES)] * 3,
        out_specs=[_tok(tm, D), _tok(tm, N), pl.BlockSpec((8, D), lambda i: (0, 0))],
        out_shape=[_sds((T, D), F32), _sds((T, N), BF16), _sds((8, D), F32)],
        scratch_shapes=[_stage_scratch(tm)],
        compiler_params=_cp("arbitrary"), name=name)(*dparts, x, dxo, vec, w, *tabs)


def _valid_mask(n, i):
    qi = lax.broadcasted_iota(jnp.int32, (n, 2 * n), 0)
    kj = lax.broadcasted_iota(jnp.int32, (n, 2 * n), 1)
    dist = n + qi - kj
    return (dist >= 0) & (dist <= n) & ((kj >= n) | (i > 0))


def _band_specs(n):
    two = pl.BlockSpec((None, 2 * n, GROUP_WIDTH), lambda r, i: (r, i, 0))
    prv = pl.BlockSpec((None, n, GROUP_WIDTH), lambda r, i: (r, jnp.maximum(2 * i - 1, 0), 0))
    one = pl.BlockSpec((None, n, GROUP_WIDTH), lambda r, i: (r, i, 0))
    return two, prv, one


def _pair_keys(prev_ref, two_ref, ps, n):
    cur2 = two_ref[:, ps]
    return jnp.concatenate([prev_ref[:, ps], cur2[0:n]], axis=0), cur2


STAT_STRIDE = LANES // HEADS_PER_GROUP


def _head_of_lane():
    return lax.broadcasted_iota(jnp.int32, (1, LANES), 1) // STAT_STRIDE


def attn_core_fwd(q, k, v, g, n):
    d, M, GW = q.shape
    scale = HEAD_DIM ** -0.5

    def body(q_ref, kp_ref, kc_ref, vp_ref, vc_ref, o_ref, l_ref):
        masks = (_valid_mask(n, pl.program_id(1)), _valid_mask(n, 1))
        first = lax.broadcasted_iota(jnp.int32, (1, LANES), 1) < HEAD_DIM
        head_of_lane = _head_of_lane()
        lse = [jnp.zeros((n, LANES), F32), jnp.zeros((n, LANES), F32)]
        for pair in range(HEADS_PER_GROUP * HEAD_DIM // LANES):
            ps = slice(LANES * pair, LANES * (pair + 1))
            keys, vals = _pair_keys(kp_ref, kc_ref, ps, n), _pair_keys(vp_ref, vc_ref, ps, n)
            for blk in range(2):
                rows = slice(blk * n, (blk + 1) * n)
                q2 = q_ref[rows, ps]
                o2, l2 = [], []
                for sel in (first, jnp.logical_not(first)):
                    s = jnp.where(masks[blk], _dot_nt(jnp.where(sel, q2, jnp.zeros_like(q2)), keys[blk]) * scale, -1e30)
                    m = jnp.max(s, axis=1, keepdims=True)
                    p = jnp.exp(s - m)
                    den = jnp.sum(p, axis=1, keepdims=True)
                    o2.append(_dot((p / den).astype(BF16), vals[blk]))
                    l2.append(m + jnp.log(den))
                o_ref[rows, ps] = jnp.where(first, o2[0], o2[1]).astype(BF16)
                for half in range(2):
                    lse[blk] = jnp.where(head_of_lane == 2 * pair + half, l2[half], lse[blk])
        for blk in range(2):
            l_ref[blk * n:(blk + 1) * n, :] = lse[blk]

    two, prv, _ = _band_specs(n)
    stat = pl.BlockSpec((None, 2 * n, LANES), lambda r, i: (r, i, 0))
    return pl.pallas_call(
        body, grid=(d, M // (2 * n)),
        in_specs=[two, prv, two, prv, two], out_specs=[two, stat],
        out_shape=[_sds((d, M, GW), BF16), _sds((d, M, LANES), F32)],
        compiler_params=_cp("arbitrary", "arbitrary"), name=f"attn_fwd_g{g}")(q, k, k, v, v)


def attn_core_bwd(q, k, v, do, delta, lse, g, n):
    d, M, GW = q.shape
    scale = HEAD_DIM ** -0.5

    def body(q_ref, kp_ref, kc_ref, vp_ref, vc_ref, do_ref, d_ref, l_ref, dq_ref, dkc_ref, dkp_ref, dvc_ref, dvp_ref):
        masks = (_valid_mask(n, pl.program_id(1)), _valid_mask(n, 1))
        first = lax.broadcasted_iota(jnp.int32, (1, LANES), 1) < HEAD_DIM
        for pair in range(HEADS_PER_GROUP * HEAD_DIM // LANES):
            ps = slice(LANES * pair, LANES * (pair + 1))
            keys, vals = _pair_keys(kp_ref, kc_ref, ps, n), _pair_keys(vp_ref, vc_ref, ps, n)
            own = []
            for blk in range(2):
                rows = slice(blk * n, (blk + 1) * n)
                q2, do2 = q_ref[rows, ps], do_ref[rows, ps]
                dq2, dk, dv = [], None, None
                for half, sel in enumerate((first, jnp.logical_not(first))):
                    qm = jnp.where(sel, q2, jnp.zeros_like(q2))
                    dom = jnp.where(sel, do2, jnp.zeros_like(do2))
                    s = jnp.where(masks[blk], _dot_nt(qm, keys[blk]) * scale, -1e30)
                    lane0 = STAT_STRIDE * (2 * pair + half)
                    p = jnp.exp(s - l_ref[rows, lane0:lane0 + 1])
                    dp = _dot_nt(dom, vals[blk])
                    ds = (p * (dp - d_ref[rows, lane0:lane0 + 1]) * scale).astype(BF16)
                    dq2.append(_dot(ds, keys[blk]))
                    dkh = _dot_tn(ds, qm)
                    dvh = _dot_tn(p.astype(BF16), dom)
                    dk = dkh if dk is None else dk + dkh
                    dv = dvh if dv is None else dv + dvh
                dq_ref[rows, ps] = jnp.where(first, dq2[0], dq2[1]).astype(BF16)
                own.append((dk, dv))
            for t, (c_ref, p_ref) in enumerate(((dkc_ref, dkp_ref), (dvc_ref, dvp_ref))):
                a, b = own[0][t], own[1][t]
                p_ref[:, ps] = a[0:n].astype(BF16)
                c_ref[0:n, ps] = (a[n:2 * n] + b[0:n]).astype(BF16)
                c_ref[n:2 * n, ps] = b[n:2 * n].astype(BF16)

    two, prv, one = _band_specs(n)
    stat = pl.BlockSpec((None, 2 * n, LANES), lambda r, i: (r, i, 0))
    return pl.pallas_call(
        body, grid=(d, M // (2 * n)),
        in_specs=[two, prv, two, prv, two, two, stat, stat], out_specs=[two, two, one, two, one],
        out_shape=[_sds((d, M, GW), BF16), _sds((d, M, GW), BF16), _sds((d, M // 2, GW), BF16),
                   _sds((d, M, GW), BF16), _sds((d, M // 2, GW), BF16)],
        compiler_params=_cp("arbitrary", "arbitrary"), name=f"attn_bwd_g{g}")(q, k, k, v, v, do, delta, lse)


def dkv_combine(cur_prev, n, name):
    d, M, GW = cur_prev[0][0].shape
    rows = min(M, 1024)
    pairs = rows // (2 * n)
    steps = M // rows
    flat = [a for pair in cur_prev for a in pair]

    def body(*refs):
        o_ref = refs[-1]
        last = pl.program_id(1) == steps - 1
        acc = None
        shifted = None
        for t in range(0, len(refs) - 1, 3):
            c = refs[t][...].astype(F32)
            nxt = jnp.where(last, 0.0, refs[t + 2][...].astype(F32))
            s = nxt if pairs == 1 else jnp.concatenate([refs[t + 1][n:pairs * n, :].astype(F32), nxt], axis=0)
            acc = c if acc is None else acc + c
            shifted = s if shifted is None else shifted + s
        for m in range(pairs):
            lo = 2 * m * n
            o_ref[lo:lo + n, :] = acc[lo:lo + n].astype(BF16)
            o_ref[lo + n:lo + 2 * n, :] = (acc[lo + n:lo + 2 * n] + shifted[m * n:(m + 1) * n]).astype(BF16)

    cur = pl.BlockSpec((None, rows, GW), lambda r, i: (r, i, 0))
    same = pl.BlockSpec((None, pairs * n, GW), lambda r, i: (r, i, 0))
    nxt = pl.BlockSpec((None, n, GW), lambda r, i: (r, jnp.minimum((i + 1) * pairs, M // (2 * n) - 1), 0))
    args = []
    for c, p in cur_prev:
        args += [c, p, p]
    return pl.pallas_call(
        body, grid=(d, steps), in_specs=[cur, same, nxt] * len(cur_prev), out_specs=cur,
        out_shape=_sds((d, M, GW), BF16),
        compiler_params=_cp("arbitrary", "arbitrary"), name=name)(*args)


def _group_weights(ls):
    mx = functools.reduce(jnp.maximum, ls)
    es = [jnp.exp(l - mx) for l in ls]
    tot = functools.reduce(lambda a, b: a + b, es)
    return [e / tot for e in es]


def _expand_heads(w):
    tm = w.shape[0]
    first = lax.broadcasted_iota(jnp.int32, (1, LANES), 1) < HEAD_DIM
    cols = [jnp.broadcast_to(w[:, STAT_STRIDE * h:STAT_STRIDE * h + 1], (tm, LANES)) for h in range(HEADS_PER_GROUP)]
    return jnp.concatenate([jnp.where(first, cols[2 * p], cols[2 * p + 1]) for p in range(HEADS_PER_GROUP // 2)], axis=1)


def _head_sums(r):
    width = r.shape[1]
    feat_head = lax.broadcasted_iota(jnp.int32, (width, LANES), 0) // HEAD_DIM
    stat_head = lax.broadcasted_iota(jnp.int32, (width, LANES), 1) // STAT_STRIDE
    ones = jnp.where(feat_head == stat_head, 1.0, 0.0).astype(BF16)
    hi = r.astype(BF16)
    lo = (r - hi.astype(F32)).astype(BF16)
    return _dot(hi, ones) + _dot(lo, ones)


def _mix_weights(l_refs, dils, tm, stage):
    ls = [_merge_residues(lambda r, ref=ref: ref[r], d, tm, LANES, stage) for ref, d in zip(l_refs, dils)]
    return [_expand_heads(w) for w in _group_weights(ls)]


def attn_mix_out(os_, ls, dils, x, vec, w_o):
    T, D = x.shape
    GW = GROUP_WIDTH
    tm = min(TOKEN_TILE, T)
    ng = len(os_)

    def body(*refs):
        o_refs, l_refs = refs[:ng], refs[ng:2 * ng]
        x_ref, vec_ref, w_ref, xn_ref, mix_ref, y_ref, stage = refs[2 * ng:]
        natural = lambda ref, d: _merge_residues(lambda r: ref[r].astype(F32), d, tm, GW, stage)
        ws = _mix_weights(l_refs, dils, tm, stage)
        mixed = functools.reduce(lambda a, b: a + b, [w * natural(r, d) for w, r, d in zip(ws, o_refs, dils)])
        mb = mixed.astype(BF16)
        mix_ref[...] = mb
        y = _dot(mb, w_ref[...])
        y_ref[...] = y.astype(BF16)
        xn_ref[...] = x_ref[...] + (1.0 + vec_ref[3:4]) * y

    res = [_residue_spec(d, tm) for d in dils]
    stat = [_residue_spec(d, tm, LANES) for d in dils]
    return pl.pallas_call(
        body, grid=(T // tm,),
        in_specs=res + stat + [_tok(tm, D), _res((8, D)), _res((GW, D))],
        out_specs=[_tok(tm, D), _tok(tm, GW), _tok(tm, D)],
        out_shape=[_sds((T, D), F32), _sds((T, GW), BF16), _sds((T, D), BF16)],
        scratch_shapes=[_stage_scratch(tm)],
        compiler_params=_cp("arbitrary"), name="attn_mix_out")(*os_, *ls, x, vec, w_o)


def attn_mix_bwd(dxo, y, vec, w_o, os_, ls, dils):
    T, D = dxo.shape
    GW = GROUP_WIDTH
    tm = min(TOKEN_TILE, T)
    ng = len(os_)

    def body(*refs):
        dxo_ref, y_ref, vec_ref, w_ref = refs[:4]
        o_refs, l_refs = refs[4:4 + ng], refs[4 + ng:4 + 2 * ng]
        dy_ref = refs[4 + 2 * ng]
        do_refs = refs[5 + 2 * ng:5 + 3 * ng]
        d_refs = refs[5 + 3 * ng:5 + 4 * ng]
        part_ref, stage = refs[5 + 4 * ng], refs[6 + 4 * ng]
        natural = lambda ref, d: _merge_residues(lambda r: ref[r].astype(F32), d, tm, GW, stage)
        dxo_t = dxo_ref[...]
        dyb = (dxo_t * (1.0 + vec_ref[3:4])).astype(BF16)
        dy_ref[...] = dyb
        dgate = jnp.sum(dxo_t * y_ref[...].astype(F32), axis=0, keepdims=True)
        _acc_rows(part_ref, _rows8([dgate], D), pl.program_id(0) == 0)
        dmix = _dot_nt(dyb, w_ref[...])
        ws = _mix_weights(l_refs, dils, tm, stage)
        mixed = functools.reduce(lambda a, b: a + b, [w * natural(r, d) for w, r, d in zip(ws, o_refs, dils)])
        for gi in range(ng):
            do = ws[gi] * dmix
            for r, rows in enumerate(_split_residues(do, dils[gi], stage)):
                do_refs[gi][r] = rows.astype(BF16)
            for r, rows in enumerate(_split_residues(_head_sums(do * mixed), dils[gi], stage)):
                d_refs[gi][r] = rows

    res = [_residue_spec(d, tm) for d in dils]
    stat = [_residue_spec(d, tm, LANES) for d in dils]
    return pl.pallas_call(
        body, grid=(T // tm,),
        in_specs=[_tok(tm, D), _tok(tm, D), _res((8, D)), _res((GW, D))] + res + stat,
        out_specs=[_tok(tm, D)] + res + stat + [pl.BlockSpec((8, D), lambda i: (0, 0))],
        out_shape=[_sds((T, D), BF16)] + [_sds((d, T // d, GW), BF16) for d in dils]
        + [_sds((d, T // d, LANES), F32) for d in dils] + [_sds((8, D), F32)],
        scratch_shapes=[_stage_scratch(tm)],
        compiler_params=_cp("arbitrary"), name="attn_mix_bwd")(dxo, y, vec, w_o, *os_, *ls)


def final_loss(x, gvec, target):
    T, D = x.shape
    tm = min(TOKEN_TILE, T)

    def norm(xv, g):
        return xv * lax.rsqrt(jnp.mean(xv * xv, axis=-1, keepdims=True) + NORM_EPS) * g

    def body(x_ref, g_ref, t_ref, dx_ref, part_ref, loss_ref):
        first = pl.program_id(0) == 0
        yv, vjp = jax.vjp(norm, x_ref[...], g_ref[0:1])
        err = yv - t_ref[...]
        dx, dg = vjp(err * (1.0 / D))
        dx_ref[...] = dx
        _acc_rows(part_ref, _rows8([dg], D), first)
        tile_loss = 0.5 * jnp.sum(jnp.sum(err * err, axis=1, keepdims=True) * (1.0 / D), axis=0, keepdims=True)
        _acc_rows(loss_ref, jnp.broadcast_to(tile_loss, (8, LANES)), first)

    return pl.pallas_call(
        body, grid=(T // tm,),
        in_specs=[_tok(tm, D), _res((8, D)), _tok(tm, D)],
        out_specs=[_tok(tm, D), pl.BlockSpec((8, D), lambda i: (0, 0)), pl.BlockSpec((8, LANES), lambda i: (0, 0))],
        out_shape=[_sds((T, D), F32), _sds((8, D), F32), _sds((8, LANES), F32)],
        compiler_params=_cp("arbitrary"), name="final_loss")(x, gvec, target)


def mods_project(c_all, w, b):
    B, D = c_all.shape
    L, _, N = w.shape

    def body(c_ref, w_ref, b_ref, o_ref):
        cv = c_ref[...]
        cond = cv * _sigmoid(cv)
        o_ref[0] = jnp.dot(cond, w_ref[0], preferred_element_type=F32, precision=lax.Precision.HIGHEST) + b_ref[0]

    return pl.pallas_call(
        body, grid=(L,),
        in_specs=[pl.BlockSpec((B, D), lambda l: (0, 0)), pl.BlockSpec((1, D, N), lambda l: (l, 0, 0)),
                  pl.BlockSpec((1, 1, N), lambda l: (l, 0, 0))],
        out_specs=pl.BlockSpec((1, B, N), lambda l: (l, 0, 0)),
        out_shape=_sds((L, B, N), F32),
        compiler_params=_cp("arbitrary"), name="mods_project")(c_all, w, b)


def mods_weight_grad(c_all, dm):
    B, D = c_all.shape
    L, _, N = dm.shape

    def body(c_ref, d_ref, o_ref):
        cv = c_ref[...]
        cond = cv * _sigmoid(cv)
        o_ref[0] = lax.dot_general(cond, d_ref[0], (((0,), (0,)), ((), ())), preferred_element_type=F32,
                                   precision=lax.Precision.HIGHEST)

    return pl.pallas_call(
        body, grid=(L,),
        in_specs=[pl.BlockSpec((B, D), lambda l: (0, 0)), pl.BlockSpec((1, B, N), lambda l: (l, 0, 0))],
        out_specs=pl.BlockSpec((1, D, N), lambda l: (l, 0, 0)),
        out_shape=_sds((L, D, N), F32),
        compiler_params=_cp("arbitrary"), name="mods_weight_grad")(c_all, dm)


def _adam_math(g, w, m, v):
    m2 = ADAM_B1 * m + (1.0 - ADAM_B1) * g
    v2 = ADAM_B2 * v + (1.0 - ADAM_B2) * (g * g)
    m_hat = m2 / (1.0 - ADAM_B1 ** ADAM_STEP)
    v_hat = v2 / (1.0 - ADAM_B2 ** ADAM_STEP)
    delta = -ADAM_LR * (m_hat / (jnp.sqrt(v_hat) + ADAM_EPS) + ADAM_WD * w)
    return delta, m2, v2


def adam_update(g, w, m, v, parts, name):
    R, C = w.shape
    tr = _pick(R, 256, 8)

    def body(g_ref, w_ref, m_ref, v_ref, go_ref, d_ref, mo_ref, vo_ref):
        if parts:
            gv = g_ref[0].astype(F32)
            for s in range(1, N_DEV):
                gv = gv + g_ref[s].astype(F32)
        else:
            gv = g_ref[...]
        go_ref[...] = gv
        d_ref[...], mo_ref[...], vo_ref[...] = _adam_math(gv, w_ref[...], m_ref[...], v_ref[...])

    gspec = pl.BlockSpec((N_DEV, tr, C), lambda i: (0, i, 0)) if parts else _tok(tr, C)
    return pl.pallas_call(
        body, grid=(R // tr,),
        in_specs=[gspec, _tok(tr, C), _tok(tr, C), _tok(tr, C)],
        out_specs=[_tok(tr, C)] * 4, out_shape=[_sds((R, C), F32)] * 4,
        compiler_params=_cp("arbitrary"), name=name)(g, w, m, v)


def adam_layer(parts, w, m, v, prev, layer, after, name):
    L, R, C = w.shape
    tr = _pick(R, 256, 8)
    prev = (list(prev) if prev is not None else []) + [after]

    def body(p_ref, w_ref, m_ref, v_ref, *rest):
        go_ref, d_ref, mo_ref, vo_ref = rest[-4:]
        gv = p_ref[0].astype(F32)
        for s in range(1, N_DEV):
            gv = gv + p_ref[s].astype(F32)
        go_ref[...] = gv
        d_ref[...], mo_ref[...], vo_ref[...] = _adam_math(gv, w_ref[...], m_ref[...], v_ref[...])

    lay = pl.BlockSpec((None, tr, C), lambda i: (layer, i, 0))
    return pl.pallas_call(
        body, grid=(R // tr,),
        in_specs=[pl.BlockSpec((N_DEV, tr, C), lambda i: (0, i, 0)), lay, lay, lay] + [pl.BlockSpec(memory_space=pl.ANY)] * len(prev),
        out_specs=[lay] * 4, out_shape=[_sds((L, R, C), F32)] * 4,
        input_output_aliases={4 + k: k for k in range(len(prev) - 1)},
        compiler_params=_cp("arbitrary"), name=name)(parts, w, m, v, *prev)


def _my_id():
    return 4 * lax.axis_index("x") + 2 * lax.axis_index("y") + lax.axis_index("c")


def _peer(s):
    x, y, c = lax.axis_index("x"), lax.axis_index("y"), lax.axis_index("c")
    px = (1 - x) if s & 4 else x
    py = (1 - y) if s & 2 else y
    pc = (1 - c) if s & 1 else c
    return (px, py, pc), 4 * px + 2 * py + pc


def all_gather(xs, space, name):
    na = len(xs)

    def body(*refs):
        x_refs, o_refs = refs[:na], refs[na:2 * na]
        send_sems, recv_sems, local_sems = refs[2 * na:]
        me = _my_id()
        locals_, sends = [], []
        for a in range(na):
            cp = pltpu.make_async_copy(x_refs[a], o_refs[a].at[me], local_sems.at[a])
            cp.start()
            locals_.append(cp)
        for s in range(1, N_DEV):
            peer, _ = _peer(s)
            for a in range(na):
                cp = pltpu.make_async_remote_copy(
                    src_ref=x_refs[a], dst_ref=o_refs[a].at[me], send_sem=send_sems.at[a, s - 1],
                    recv_sem=recv_sems.at[a, s - 1], device_id=peer, device_id_type=MESH)
                cp.start()
                sends.append(cp)
        for s in range(1, N_DEV):
            peer, pid = _peer(s)
            for a in range(na):
                pltpu.make_async_remote_copy(
                    src_ref=x_refs[a], dst_ref=o_refs[a].at[pid], send_sem=send_sems.at[a, s - 1],
                    recv_sem=recv_sems.at[a, s - 1], device_id=peer, device_id_type=MESH).wait_recv()
        for cp in sends:
            cp.wait_send()
        for cp in locals_:
            cp.wait()

    spec = pl.BlockSpec(memory_space=space)
    return pl.pallas_call(
        body, in_specs=[spec] * na, out_specs=[spec] * na,
        out_shape=[_sds((N_DEV,) + x.shape, x.dtype) for x in xs],
        scratch_shapes=[pltpu.SemaphoreType.DMA((na, N_DEV - 1)), pltpu.SemaphoreType.DMA((na, N_DEV - 1)),
                        pltpu.SemaphoreType.DMA((na,))],
        compiler_params=pltpu.CompilerParams(vmem_limit_bytes=VMEM_LIMIT), name=name)(*xs)


def exchange_slots(xs, name):
    na = len(xs)

    def body(*refs):
        x_refs, o_refs = refs[:na], refs[na:2 * na]
        send_sems, recv_sems, local_sems = refs[2 * na:]
        me = _my_id()
        locals_, sends = [], []
        for a in range(na):
            cp = pltpu.make_async_copy(x_refs[a].at[me], o_refs[a].at[me], local_sems.at[a])
            cp.start()
            locals_.append(cp)
        for s in range(1, N_DEV):
            peer, pid = _peer(s)
            for a in range(na):
                cp = pltpu.make_async_remote_copy(
                    src_ref=x_refs[a].at[pid], dst_ref=o_refs[a].at[me], send_sem=send_sems.at[a, s - 1],
                    recv_sem=recv_sems.at[a, s - 1], device_id=peer, device_id_type=MESH)
                cp.start()
                sends.append(cp)
        for s in range(1, N_DEV):
            peer, pid = _peer(s)
            for a in range(na):
                pltpu.make_async_remote_copy(
                    src_ref=x_refs[a].at[pid], dst_ref=o_refs[a].at[pid], send_sem=send_sems.at[a, s - 1],
                    recv_sem=recv_sems.at[a, s - 1], device_id=peer, device_id_type=MESH).wait_recv()
        for cp in sends:
            cp.wait_send()
        for cp in locals_:
            cp.wait()

    spec = pl.BlockSpec(memory_space=pl.ANY)
    return pl.pallas_call(
        body, in_specs=[spec] * na, out_specs=[spec] * na,
        out_shape=[_sds(x.shape, x.dtype) for x in xs],
        scratch_shapes=[pltpu.SemaphoreType.DMA((na, N_DEV - 1)), pltpu.SemaphoreType.DMA((na, N_DEV - 1)),
                        pltpu.SemaphoreType.DMA((na,))],
        compiler_params=pltpu.CompilerParams(vmem_limit_bytes=VMEM_LIMIT), name=name)(*xs)


_HBM = pl.BlockSpec(memory_space=pltpu.HBM)
_SEM = pl.BlockSpec(memory_space=pltpu.SEMAPHORE)
_EFFECT = pltpu.SideEffectType.DATAFLOW_SIDE_EFFECTING


def _split_copies(pattern, x_ref, land_ref, send_sem, recv_sem):
    me = _my_id()
    if pattern in ("gather", "scatter"):
        plan = []
        for s in range(1, N_DEV):
            peer, pid = _peer(s)
            plan.append((x_ref.at[pid] if pattern == "scatter" else x_ref, land_ref.at[me], peer))
    elif pattern == "to_chips":
        plan = [(x_ref, land_ref.at[me], _peer(s)[0]) for s in (1, 2, 4, 6)]
    else:
        sibling = _peer(1)[0]
        plan = [(land_ref.at[_peer(s)[1]], land_ref.at[_peer(s)[1]], sibling) for s in (2, 4, 6)]
    return [pltpu.make_async_remote_copy(src_ref=src, dst_ref=dst, send_sem=send_sem, recv_sem=recv_sem,
                                         device_id=dev, device_id_type=MESH) for src, dst, dev in plan]


def comm_start(xs, pattern, after, name, lands=None):
    na = len(xs)
    extra = [] if after is None else [after]
    me = _my_id()
    if lands is None:
        lands = []
        for x in xs:
            shape = x.shape if pattern == "scatter" else (N_DEV,) + x.shape
            own = lax.dynamic_slice_in_dim(x, me, 1, 0) if pattern == "scatter" else x[None]
            lands.append(lax.dynamic_update_slice(lax.empty(shape, x.dtype), own, (me,) + (0,) * (len(shape) - 1)))

    def body(*refs):
        x_refs, land_refs = refs[:na], refs[na:2 * na]
        send_sem, recv_sem = refs[2 * na + len(extra)], refs[2 * na + len(extra) + 1]
        token = refs[-1]
        for a in range(na):
            for cp in _split_copies(pattern, x_refs[a], land_refs[a], send_sem, recv_sem):
                cp.start()
        token[...] = jnp.zeros_like(token)

    outs = pl.pallas_call(
        body, name=name,
        out_shape=(pltpu.SemaphoreType.DMA(()), pltpu.SemaphoreType.DMA(()))
        + tuple(pltpu.HBM(x.shape, x.dtype) for x in xs) + tuple(pltpu.HBM(l.shape, l.dtype) for l in lands)
        + (_sds((8, LANES), F32),),
        in_specs=(_HBM,) * (2 * na) + (pl.BlockSpec(memory_space=pl.ANY),) * len(extra),
        out_specs=(_SEM, _SEM) + (_HBM,) * (2 * na) + (pl.BlockSpec(memory_space=pltpu.VMEM),),
        input_output_aliases={a: 2 + a for a in range(2 * na)},
        compiler_params=pltpu.CompilerParams(has_side_effects=_EFFECT),
    )(*[pltpu.with_memory_space_constraint(x, pltpu.HBM) for x in xs],
      *[pltpu.with_memory_space_constraint(l, pltpu.HBM) for l in lands], *extra)
    return dict(sems=outs[0:2], xs=outs[2:2 + na], lands=outs[2 + na:2 + 2 * na], token=outs[-1], pattern=pattern)


def comm_wait(started, after, name, with_xs=False):
    xs, lands = started["xs"], started["lands"]
    pattern = started["pattern"]
    na = len(xs)

    def body(*refs):
        x_refs, land_refs = refs[:na], refs[na:2 * na]
        send_sem, recv_sem = refs[2 * na], refs[2 * na + 1]
        for a in range(na):
            for cp in _split_copies(pattern, x_refs[a], land_refs[a], send_sem, recv_sem):
                cp.wait_send()
                cp.wait_recv()

    outs = pl.pallas_call(
        body, name=name,
        out_shape=tuple(pltpu.HBM(x.shape, x.dtype) for x in xs) + tuple(pltpu.HBM(l.shape, l.dtype) for l in lands),
        in_specs=(_HBM,) * (2 * na) + (_SEM, _SEM, pl.BlockSpec(memory_space=pl.ANY)),
        out_specs=(_HBM,) * (2 * na),
        input_output_aliases={a: a for a in range(2 * na)},
        compiler_params=pltpu.CompilerParams(has_side_effects=_EFFECT),
    )(*xs, *lands, *started["sems"], after)
    return (list(outs[na:]), list(outs[:na])) if with_xs else list(outs[na:])


def _cols_to_natural(g):
    return jnp.concatenate([g[k] for k in range(N_DEV)], axis=1)


def _vec8(rows, d):
    rows = [r.reshape(1, d).astype(F32) for r in rows]
    return jnp.concatenate(rows + [jnp.zeros((8 - len(rows), d), F32)], axis=0)


def _ffn_forward(x, vec, w_in_t, w_out):
    xn, h, a, b, u, y = ffn_fwd(x, vec, w_in_t, w_out)
    return xn, (x, h, a, b, u, y)


def _ffn_backward(dxo, saved, vec, w_in_t, w_out, on_rows=None):
    x, h, a, b, u, y = saved
    dy, dab, dx, part = ffn_bwd(dxo, y, vec, w_out, w_in_t, a, b, x)
    rows = part[0:4]
    token = on_rows(rows) if on_rows is not None else None
    g_out = grad_slots(u, dy, "ffn_dw_out", after=token)
    g_in_t = grad_slots(dab, h, "ffn_dw_in", after=token)
    return dx, g_in_t, g_out, rows


_TRANSPOSED = ("ffn1_w_in", "ffn2_w_in", "attn_w_q")
_COL_NATURAL = ("conv_w_in", "w_kv", "attn_w_o")
_ROW_SHARDED = ("ffn1_w_out", "ffn2_w_out", "conv_w_out")
_BIG = _TRANSPOSED + _COL_NATURAL + _ROW_SHARDED


def weight_chunks():
    chunks = []
    for layer in range(DEPTH):
        first = [("ffn1_w_in", layer), ("ffn1_w_out", layer)]
        if layer == N_A_LAYERS:
            first = [("w_kv", layer)] + first
        mixer = [("conv_w_in", layer), ("conv_w_out", layer)] if layer < N_A_LAYERS else [("attn_w_q", layer), ("attn_w_o", layer)]
        rest = mixer + [("ffn2_w_in", layer), ("ffn2_w_out", layer)]
        chunks += [first, rest] if layer == 0 else [first + rest]
    return chunks


def stacked_index(name, layer):
    if name == "w_kv":
        return None
    return layer - N_A_LAYERS if name.startswith("attn") else layer


class ChunkComm:
    def __init__(self, shards):
        self.shards = shards
        self.chunks = weight_chunks()

    def _shard(self, name, layer):
        idx = stacked_index(name, layer)
        return self.shards[name][0 if idx is None else idx]

    def start_gather(self, ci, after):
        xs = [self._shard(n, l).astype(BF16) for n, l in self.chunks[ci]]
        return comm_start(xs, "to_chips", after, f"gather_start_{ci}")

    def relay_gather(self, ci, started, after):
        lands, xs = comm_wait(started, after, f"gather_wait_{ci}", with_xs=True)
        return comm_start(xs, "relay", None, f"gather_relay_{ci}", lands=lands)

    def finish_gather(self, ci, relayed, after):
        lands = comm_wait(relayed, after, f"gather_done_{ci}")
        W = {}
        for key, g in zip(self.chunks[ci], lands):
            W[key] = _cols_to_natural(g) if key[0] in _COL_NATURAL else g.reshape(-1, g.shape[2])
        return W

    def start_exchange(self, ci, slots, after):
        return comm_start([slots[key] for key in self.chunks[ci]], "scatter", after, f"exchange_start_{ci}")

    def finish_exchange(self, ci, started, after):
        lands = comm_wait(started, after, f"exchange_wait_{ci}")
        return dict(zip(self.chunks[ci], lands))


def device_step(x, positions, target, mods, kvmods, small, comm, gather0):
    T, D = x.shape
    groups = DILATED_GROUPS
    dils = [dil for _, dil in groups]
    lane = jnp.arange(LANES) % HEAD_DIM
    inv = ROPE_THETA ** (-jnp.arange(0, ROPE_DIM, 2, dtype=F32) / ROPE_DIM)
    lane_rows = _vec8([jnp.where(lane < ROPE_DIM, inv[lane % (ROPE_DIM // 2)], 0.0), lane < ROPE_DIM,
                       (lane >= ROPE_DIM // 2) & (lane < ROPE_DIM), lane < ROPE_DIM // 2], LANES)
    tabs = rope_tables(positions.reshape(T, 1), lane_rows)

    def after_token(v, token):
        return v if token is None else v + token[0, 0]

    def vec_of(layer, sub):
        return _vec8([small["norm_g"][layer, sub], mods[layer, 3 * sub], mods[layer, 3 * sub + 1], mods[layer, 3 * sub + 2]], D)

    saved = []
    kv_saved = None
    k_sh = v_sh = None
    qw = GROUP_WIDTH * len(groups)
    chunk_of = {key: ci for ci, chunk in enumerate(comm.chunks) for key in chunk}
    W = {}
    flight = {"ci": 0, "started": gather0}

    relayed = {}

    def advance(after):
        ci = flight["ci"]
        if flight["started"] is None or ci in relayed:
            return None
        relayed[ci] = comm.relay_gather(ci, flight["started"], after)
        relayed[ci]["behind"] = relayed[ci]["token"]
        nxt = comm.start_gather(ci + 1, relayed[ci]["token"]) if ci + 1 < len(comm.chunks) else None
        flight.update(ci=ci + 1, started=nxt)
        if nxt is not None:
            relayed[ci]["behind"] = nxt["token"]
        return relayed[ci]["behind"]

    def need(key, after):
        if key not in W:
            ci = chunk_of[key]
            if ci not in relayed:
                assert ci == flight["ci"], (key, ci)
                advance(after)
            W.update(comm.finish_gather(ci, relayed[ci], relayed[ci]["behind"]))
        return W[key]

    for layer in range(DEPTH):
        if layer == N_A_LAYERS:
            w_kv = need(("w_kv", layer), x)
            kv_vec = _vec8([small["kv_norm_g"], kvmods[0], kvmods[1]], D)
            h_kv, *kv_pieces = proj_rope_fwd(x, kv_vec, w_kv, tabs, qw, False, dils, "kv_fwd")
            k_sh, v_sh = kv_pieces[:len(groups)], kv_pieces[len(groups):]
            kv_saved = (x, h_kv, kv_vec)
        rec = {}
        behind = tabs[0] if layer == 0 else x
        w_in, w_out = need(("ffn1_w_in", layer), behind), need(("ffn1_w_out", layer), behind)
        v1 = vec_of(layer, 0)
        x, rec["ffn1"] = _ffn_forward(x, v1, w_in, w_out)
        if layer < N_A_LAYERS:
            w_in, w_out = need(("conv_w_in", layer), x), need(("conv_w_out", layer), x)
            v2 = vec_of(layer, 1)
            cw = _vec8(list(small["conv_w"][layer]), D)
            x_in = x
            x, h, bcu, cv, z, y = conv_fwd(x, v2, cw, w_in, w_out)
            rec["mix"] = (x_in, h, bcu, cv, z, y, cw)
        else:
            w_q, w_o = need(("attn_w_q", layer), x), need(("attn_w_o", layer), x)
            v2 = vec_of(layer, 1)
            x_in = x
            h, *q = proj_rope_fwd(x, v2, w_q, tabs, qw, True, dils, "q_fwd")
            os_, ls = [], []
            for g, (win, dil) in enumerate(groups):
                o, l = attn_core_fwd(q[g], k_sh[g], v_sh[g], g, win // dil)
                os_.append(o)
                ls.append(l)
            x, mixed, y = attn_mix_out(os_, ls, dils, x, v2, w_o)
            rec["mix"] = (x_in, h, q, os_, ls, mixed, y)
        token = advance(x) if layer >= 1 else None
        w_in, w_out = need(("ffn2_w_in", layer), x), need(("ffn2_w_out", layer), x)
        v3 = after_token(vec_of(layer, 2), token)
        x, rec["ffn2"] = _ffn_forward(x, v3, w_in, w_out)
        rec["vecs"] = (v1, v2, v3)
        saved.append(rec)

    dx, part_final, loss_tile = final_loss(x, _vec8([small["final_norm_g"]], D), target)
    loss = loss_tile[0, 0]

    conv_rows = [None] * N_A_LAYERS
    kv_rows = None
    mod_rows = [[None] * 3 for _ in range(DEPTH)]
    dkv_pairs = [{"k": [], "v": []} for _ in groups]
    slots = {}
    exchanges = []
    token = None

    def send_ready_chunks():
        nonlocal token
        for ci in reversed(range(len(comm.chunks))):
            if ci not in [e[0] for e in exchanges] and all(key in slots for key in comm.chunks[ci]):
                started = comm.start_exchange(ci, slots, token)
                exchanges.append((ci, started))
                token = started["token"]

    vector_gather = {}

    def start_vector_gather(rows0):
        mod_rows[0][0] = rows0
        rows = jnp.stack([jnp.stack(r) for r in mod_rows])
        vecs = jnp.concatenate([rows[:, :, 1:4].reshape(-1), kv_rows[1:3].reshape(-1), kv_rows[0], part_final[0],
                                rows[:, :, 0].reshape(-1), jnp.stack(conv_rows).reshape(-1)])
        vector_gather["count"] = vecs.shape[0]
        vecs = _pad_rows(vecs.reshape(-1, 1), 8 * LANES).reshape(-1, LANES)
        vector_gather["started"] = comm_start([vecs], "gather", None, "vector_grads_start")
        return vector_gather["started"]["token"]

    for layer in reversed(range(DEPTH)):
        rec = saved[layer]
        v1, v2, v3 = rec["vecs"]
        dx, slots[("ffn2_w_in", layer)], slots[("ffn2_w_out", layer)], mod_rows[layer][2] = _ffn_backward(
            dx, rec["ffn2"], after_token(v3, token), W[("ffn2_w_in", layer)], W[("ffn2_w_out", layer)])
        if layer < N_A_LAYERS:
            x_in, h, bcu, cv, z, y, cw = rec["mix"]
            dx, dy, dbcu, part, dcw = conv_bwd(dx, x_in, y, bcu, cv, v2, cw, W[("conv_w_in", layer)], W[("conv_w_out", layer)])
            slots[("conv_w_out", layer)] = grad_slots(z, dy, "conv_dw_out")
            slots[("conv_w_in", layer)] = grad_slots(h, dbcu, "conv_dw_in", col_slots=True)
            conv_rows[layer] = dcw[0:3]
            mod_rows[layer][1] = part[0:4]
        else:
            x_in, h, q, os_, ls, mixed, y = rec["mix"]
            outs = attn_mix_bwd(dx, y, v2, W[("attn_w_o", layer)], os_, ls, dils)
            ng = len(groups)
            dy, dos, deltas, part_gate = outs[0], outs[1:1 + ng], outs[1 + ng:1 + 2 * ng], outs[1 + 2 * ng]
            slots[("attn_w_o", layer)] = grad_slots(mixed, dy, "attn_dw_o", col_slots=True)
            dqs = []
            for g, (win, dil) in enumerate(groups):
                dq, dkc, dkp, dvc, dvp = attn_core_bwd(q[g], k_sh[g], v_sh[g], dos[g], deltas[g], ls[g], g, win // dil)
                dqs.append(dq)
                dkv_pairs[g]["k"].append((dkc, dkp))
                dkv_pairs[g]["v"].append((dvc, dvp))
            dx, dqr, part_norm = proj_rope_bwd(dqs, dils, x_in, dx, v2, W[("attn_w_q", layer)], tabs, qw, True, "q_bwd")
            slots[("attn_w_q", layer)] = grad_slots(dqr, h, "attn_dw_q")
            mod_rows[layer][1] = jnp.concatenate([part_norm[0:3], part_gate[0:1]], axis=0)
        send_ready_chunks()
        dx, slots[("ffn1_w_in", layer)], slots[("ffn1_w_out", layer)], mod_rows[layer][0] = _ffn_backward(
            dx, rec["ffn1"], after_token(v1, token), W[("ffn1_w_in", layer)], W[("ffn1_w_out", layer)],
            on_rows=start_vector_gather if layer == 0 else None)
        if layer == N_A_LAYERS:
            x_kv, h_kv, kv_vec = kv_saved
            dparts = [dkv_combine(dkv_pairs[g]["k"], win // dil, f"dk_combine_g{g}") for g, (win, dil) in enumerate(groups)]
            dparts += [dkv_combine(dkv_pairs[g]["v"], win // dil, f"dv_combine_g{g}") for g, (win, dil) in enumerate(groups)]
            dx, dkvp, part_kv = proj_rope_bwd(dparts, dils, x_kv, dx, kv_vec, W[("w_kv", layer)], tabs, qw, False, "kv_bwd")
            slots[("w_kv", layer)] = grad_slots(h_kv, dkvp, "kv_dw", col_slots=True)
            kv_rows = part_kv[0:3]
        send_ready_chunks()

    return loss, dx, {"exchanges": exchanges, "vector_gather": vector_gather}


def _flat2(a):
    return a.reshape(-1, a.shape[-1])


def _pad_rows(a, mult):
    r = a.shape[0]
    pad = (-r) % mult
    return a if pad == 0 else jnp.concatenate([a, jnp.zeros((pad,) + a.shape[1:], a.dtype)], axis=0)


def kernel(x, c, positions, norm_g, ada_w, ada_b, ffn1_w_in, ffn1_w_out, ffn2_w_in, ffn2_w_out, conv_w_in, conv_w, conv_w_out, kv_norm_g, kv_ada_w, kv_ada_b, w_kv, attn_w_q, attn_w_o, final_norm_g, loss_target, m_norm_g, m_ada_w, m_ada_b, m_ffn1_w_in, m_ffn1_w_out, m_ffn2_w_in, m_ffn2_w_out, m_conv_w_in, m_conv_w, m_conv_w_out, m_kv_norm_g, m_kv_ada_w, m_kv_ada_b, m_w_kv, m_attn_w_q, m_attn_w_o, m_final_norm_g, v_norm_g, v_ada_w, v_ada_b, v_ffn1_w_in, v_ffn1_w_out, v_ffn2_w_in, v_ffn2_w_out, v_conv_w_in, v_conv_w, v_conv_w_out, v_kv_norm_g, v_kv_ada_w, v_kv_ada_b, v_w_kv, v_attn_w_q, v_attn_w_o, v_final_norm_g):
    names = ("norm_g", "ada_w", "ada_b", "ffn1_w_in", "ffn1_w_out", "ffn2_w_in", "ffn2_w_out", "conv_w_in", "conv_w",
             "conv_w_out", "kv_norm_g", "kv_ada_w", "kv_ada_b", "w_kv", "attn_w_q", "attn_w_o", "final_norm_g")
    wts = dict(zip(names, (norm_g, ada_w, ada_b, ffn1_w_in, ffn1_w_out, ffn2_w_in, ffn2_w_out, conv_w_in, conv_w, conv_w_out,
                           kv_norm_g, kv_ada_w, kv_ada_b, w_kv, attn_w_q, attn_w_o, final_norm_g)))
    mom = dict(zip(names, (m_norm_g, m_ada_w, m_ada_b, m_ffn1_w_in, m_ffn1_w_out, m_ffn2_w_in, m_ffn2_w_out, m_conv_w_in,
                           m_conv_w, m_conv_w_out, m_kv_norm_g, m_kv_ada_w, m_kv_ada_b, m_w_kv, m_attn_w_q, m_attn_w_o,
                           m_final_norm_g)))
    var = dict(zip(names, (v_norm_g, v_ada_w, v_ada_b, v_ffn1_w_in, v_ffn1_w_out, v_ffn2_w_in, v_ffn2_w_out, v_conv_w_in,
                           v_conv_w, v_conv_w_out, v_kv_norm_g, v_kv_ada_w, v_kv_ada_b, v_w_kv, v_attn_w_q, v_attn_w_o,
                           v_final_norm_g)))
    T, D = x.shape[1], x.shape[2]
    me = _my_id()
    nmod = ada_w.shape[2]
    nkv = kv_ada_w.shape[1]

    def stacked(w, n):
        w = w if w.ndim == 3 else w[None]
        return jnp.swapaxes(w, 1, 2) if n in _TRANSPOSED else w

    comm = ChunkComm({n: stacked(wts[n], n) for n in _BIG})
    W = {}

    ds = norm_g.shape[2]
    small = jnp.concatenate([c.reshape(-1), norm_g.reshape(-1), conv_w.reshape(-1)]).astype(F32)
    n_small = small.shape[0]
    small = _pad_rows(small.reshape(-1, 1), 8 * LANES).reshape(-1, LANES)
    (small_all,) = all_gather([small], pltpu.VMEM, "gather_small")
    small_all = small_all.reshape(N_DEV, -1)[:, :n_small]
    c_all = small_all[:, :D]
    def full_rows(off, count):
        return jnp.stack([small_all[:, off + i * ds:off + (i + 1) * ds].reshape(D) for i in range(count)])

    W["norm_g"] = full_rows(D, DEPTH * 3).reshape(DEPTH, 3, D)
    W["conv_w"] = full_rows(D + DEPTH * 3 * ds, N_A_LAYERS * 3).reshape(N_A_LAYERS, 3, D)
    W["kv_norm_g"], W["final_norm_g"] = kv_norm_g, final_norm_g

    ada_b_mine = lax.dynamic_slice_in_dim(ada_b, me * nmod, nmod, axis=1).reshape(DEPTH, 1, nmod)
    kv_b_mine = lax.dynamic_slice_in_dim(kv_ada_b, me * nkv, nkv, axis=0).reshape(1, 1, nkv)
    mods_cols = mods_project(c_all, ada_w, ada_b_mine)
    kv_cols = mods_project(c_all, kv_ada_w.reshape(1, D, nkv), kv_b_mine)
    mcat = jnp.concatenate([mods_cols[l] for l in range(DEPTH)] + [kv_cols[0]], axis=1)
    wm = mcat.shape[1]
    if wm % LANES:
        mcat = jnp.concatenate([mcat, jnp.zeros((N_DEV, LANES - wm % LANES), F32)], axis=1)
    (mods_all,) = exchange_slots([mcat.reshape(N_DEV, 1, -1)], "exchange_mods")
    gather0 = comm.start_gather(0, mods_all)
    mods_all = mods_all.reshape(N_DEV, -1)
    mods = jnp.stack([mods_all[:, l * nmod:(l + 1) * nmod].reshape(N_MOD, D) for l in range(DEPTH)])
    kvmods = mods_all[:, DEPTH * nmod:DEPTH * nmod + nkv].reshape(2, D)

    loss_local, dx, grads = device_step(x[0], positions[0], loss_target[0], mods, kvmods, W, comm, gather0)
    loss = lax.psum(loss_local, MESH_AXES)

    (vec_all,) = comm_wait(grads["vector_gather"]["started"], grads["exchanges"][-1][1]["token"], "vector_grads_wait")
    vec_all = vec_all.reshape(N_DEV, -1)[:, :grads["vector_gather"]["count"]]
    nm_, nk_ = DEPTH * N_MOD * D, 2 * D
    dmods_all = vec_all[:, :nm_].reshape(N_DEV, DEPTH, N_MOD * D)
    dkvm_all = vec_all[:, nm_:nm_ + nk_]
    rest = vec_all[:, nm_ + nk_:]
    parts_kv_norm, parts_final = rest[:, :D].reshape(N_DEV, 1, D), rest[:, D:2 * D].reshape(N_DEV, 1, D)
    parts_norm = lax.dynamic_slice_in_dim(rest[:, 2 * D:2 * D + DEPTH * 3 * D].reshape(N_DEV, DEPTH * 3, D), me * ds, ds, axis=2)
    parts_conv = lax.dynamic_slice_in_dim(rest[:, 2 * D + DEPTH * 3 * D:].reshape(N_DEV, N_A_LAYERS * 3, D), me * ds, ds, axis=2)
    dm_cols = lax.dynamic_slice_in_dim(dmods_all, me * nmod, nmod, axis=2)
    dm_mine = jnp.stack([dm_cols[:, l] for l in range(DEPTH)])
    dkv_mine = lax.dynamic_slice_in_dim(dkvm_all, me * nkv, nkv, axis=1).reshape(1, N_DEV, nkv)
    g_ada_w = mods_weight_grad(c_all, dm_mine)
    g_kv_ada_w = mods_weight_grad(c_all, dkv_mine)[0]

    out_g, out_d, out_m, out_v = {}, {}, {}, {}

    def update(n, g, w, parts=False):
        shp = w.shape
        w2 = w.reshape(1, -1) if w.ndim == 1 else _flat2(w)
        g2 = g if parts else g.reshape(w2.shape)
        res = adam_update(g2, w2, mom[n].reshape(w2.shape), var[n].reshape(w2.shape), parts, "adam_" + n)
        out_g[n], out_d[n], out_m[n], out_v[n] = (r.reshape(shp) for r in res)

    moms = {n: stacked(mom[n], n) for n in _BIG}
    vars_ = {n: stacked(var[n], n) for n in _BIG}
    results = {}
    after = dx
    for ci, started in grads["exchanges"]:
        for (n, layer), parts in comm.finish_exchange(ci, started, after).items():
            idx = stacked_index(n, layer)
            results[n] = adam_layer(parts, comm.shards[n], moms[n], vars_[n], results.get(n), 0 if idx is None else idx,
                                    after, f"adam_{n}_{layer}")
            after = results[n][1]
    for n in _BIG:
        res = [jnp.swapaxes(r, 1, 2) if n in _TRANSPOSED else r for r in results[n]]
        out_g[n], out_d[n], out_m[n], out_v[n] = (r.reshape(wts[n].shape) for r in res)
    update("ada_w", g_ada_w, ada_w)
    update("kv_ada_w", g_kv_ada_w, kv_ada_w)
    update("ada_b", dmods_all, ada_b, True)
    update("kv_ada_b", dkvm_all.reshape(N_DEV, 1, nk_), kv_ada_b, True)
    update("kv_norm_g", parts_kv_norm, kv_norm_g, True)
    update("final_norm_g", parts_final, final_norm_g, True)
    update("norm_g", parts_norm, norm_g, True)
    update("conv_w", parts_conv, conv_w, True)

    return (loss, dx.reshape(x.shape), *[out_g[n] for n in names], *[out_d[n] for n in names],
            *[out_m[n] for n in names], *[out_v[n] for n in names])
```

```python
import functools

import jax
import jax.numpy as jnp
from jax import lax
from jax.experimental import pallas as pl
from jax.experimental.pallas import tpu as pltpu

F32, BF16 = jnp.float32, jnp.bfloat16

N_DEV = 8
MESH_AXES = ("x", "y", "c")
DEPTH = 4
N_A_LAYERS = 2
HEAD_DIM = 64
HEADS_PER_GROUP = 8
GROUP_WIDTH = HEAD_DIM * HEADS_PER_GROUP
DILATED_GROUPS = ((128, 1), (512, 4), (2048, 16))
ROPE_DIM = HEAD_DIM // 4
ROPE_THETA = 500000.0
NORM_EPS = 1e-5
FFN_RES_WEIGHT = 0.5
N_MOD = 9
ADAM_LR, ADAM_B1, ADAM_B2, ADAM_EPS, ADAM_WD, ADAM_STEP = 0.001, 0.9, 0.999, 1e-08, 0.01, 10

LANES = 128
TOKEN_TILE = 512
FFN_BWD_TILE = 256
FWD_QUERY_BLOCKS = 4
CONTRACT_TILE = 4096
GRAD_COLS = 768
MXU_WIDTH = 256
VMEM_LIMIT = 56 * 1024 * 1024
MESH = pl.DeviceIdType.MESH


def _cp(*sem):
    return pltpu.CompilerParams(dimension_semantics=sem, vmem_limit_bytes=VMEM_LIMIT)


def _pick(n, cap, mult=LANES):
    if n <= cap:
        return n
    best = None
    for t in range(mult, cap + 1, mult):
        if n % t == 0:
            best = t
    assert best is not None, (n, cap)
    return best


def _tok(tm, w):
    return pl.BlockSpec((tm, w), lambda i: (i, 0))


def _res(shape):
    nd = len(shape)
    return pl.BlockSpec(shape, lambda *_: (0,) * nd, pipeline_mode=pl.Buffered(1))


def _sds(shape, dt):
    return jax.ShapeDtypeStruct(shape, dt)


def _sigmoid(a):
    return 1.0 / (1.0 + jnp.exp(-a))


def _modnorm(x, g, sh, sc):
    r = lax.rsqrt(jnp.mean(x * x, axis=-1, keepdims=True) + NORM_EPS)
    return (x * r * g) * (1.0 + sc) + sh


def _dot(a, b):
    return jnp.dot(a, b, preferred_element_type=F32)


def _dot_nt(a, b):
    return lax.dot_general(a, b, (((1,), (1,)), ((), ())), preferred_element_type=F32)


def _dot_tn(a, b):
    return lax.dot_general(a, b, (((0,), (0,)), ((), ())), preferred_element_type=F32)


def _rows8(rows, d):
    pad = 8 - len(rows)
    return jnp.concatenate(list(rows) + [jnp.zeros((pad, d), F32)], axis=0)


def _acc_rows(ref, tile, first):
    @pl.when(first)
    def _():
        ref[...] = tile

    @pl.when(jnp.logical_not(first))
    def _():
        ref[...] += tile


def ffn_fwd(x, vec, w_in_t, w_out):
    T, D = x.shape
    F = w_in_t.shape[0] // 2
    tm, cw = min(TOKEN_TILE, T), _pick(F, MXU_WIDTH)

    def body(x_ref, vec_ref, wi_ref, wo_ref, xn_ref, h_ref, ga_ref, gb_ref, u_ref, y_ref):
        x_t = x_ref[...]
        hb = _modnorm(x_t, vec_ref[0:1], vec_ref[1:2], vec_ref[2:3]).astype(BF16)
        h_ref[...] = hb
        for c in range(F // cw):
            lo, hi = c * cw, (c + 1) * cw
            a = _dot_nt(hb, wi_ref[lo:hi, :])
            b = _dot_nt(hb, wi_ref[F + lo:F + hi, :])
            sg = _sigmoid(a)
            silu = a * sg
            ga_ref[:, lo:hi] = (b * (sg + silu * (1.0 - sg))).astype(BF16)
            gb_ref[:, lo:hi] = silu.astype(BF16)
            u_ref[:, lo:hi] = (silu * b).astype(BF16)
        y = _dot(u_ref[...], wo_ref[...])
        y_ref[...] = y.astype(BF16)
        xn_ref[...] = x_t + (FFN_RES_WEIGHT * (1.0 + vec_ref[3:4])) * y

    return pl.pallas_call(
        body, grid=(T // tm,),
        in_specs=[_tok(tm, D), _res((8, D)), _res((2 * F, D)), _res((F, D))],
        out_specs=[_tok(tm, D), _tok(tm, D), _tok(tm, F), _tok(tm, F), _tok(tm, F), _tok(tm, D)],
        out_shape=[_sds((T, D), F32), _sds((T, D), BF16), _sds((T, F), BF16), _sds((T, F), BF16), _sds((T, F), BF16),
                   _sds((T, D), BF16)],
        compiler_params=_cp("arbitrary"), name="ffn_fwd")(x, vec, w_in_t, w_out)


def ffn_bwd(dxo, y, vec, w_out, w_in_t, a, b, x):
    T, D = x.shape
    F = a.shape[1]
    tm, cw = min(FFN_BWD_TILE, T), _pick(F, MXU_WIDTH)

    def body(dxo_ref, y_ref, vec_ref, wo_ref, wi_ref, a_ref, b_ref, x_ref, dy_ref, dab_ref, dx_ref, part_ref):
        dxo_t = dxo_ref[...]
        dyb = (dxo_t * (FFN_RES_WEIGHT * (1.0 + vec_ref[3:4]))).astype(BF16)
        dy_ref[...] = dyb
        dgate = FFN_RES_WEIGHT * jnp.sum(dxo_t * y_ref[...].astype(F32), axis=0, keepdims=True)
        for c in range(F // cw):
            lo, hi = c * cw, (c + 1) * cw
            du = _dot_nt(dyb, wo_ref[lo:hi, :])
            dab_ref[:, lo:hi] = (du * a_ref[:, lo:hi].astype(F32)).astype(BF16)
            dab_ref[:, F + lo:F + hi] = (du * b_ref[:, lo:hi].astype(F32)).astype(BF16)
        dh = _dot(dab_ref[...], wi_ref[...])
        _, vjp = jax.vjp(_modnorm, x_ref[...], vec_ref[0:1], vec_ref[1:2], vec_ref[2:3])
        dx, dg, dsh, dsc = vjp(dh)
        dx_ref[...] = dxo_t + dx
        _acc_rows(part_ref, _rows8([dg, dsh, dsc, dgate], D), pl.program_id(0) == 0)

    return pl.pallas_call(
        body, grid=(T // tm,),
        in_specs=[_tok(tm, D), _tok(tm, D), _res((8, D)), _res((F, D)), _res((2 * F, D)), _tok(tm, F), _tok(tm, F), _tok(tm, D)],
        out_specs=[_tok(tm, D), _tok(tm, 2 * F), _tok(tm, D), pl.BlockSpec((8, D), lambda i: (0, 0))],
        out_shape=[_sds((T, D), BF16), _sds((T, 2 * F), BF16), _sds((T, D), F32), _sds((8, D), F32)],
        compiler_params=_cp("arbitrary"), name="ffn_bwd")(dxo, y, vec, w_out, w_in_t, a, b, x)


def grad_slots(a, b, name, col_slots=False, after=None):
    T, M = a.shape
    extra = [] if after is None else [after]
    N = b.shape[1]
    tk = min(CONTRACT_TILE, T)
    nk = T // tk
    tmm = _pick(M, 1408)
    if col_slots:
        ns = N // N_DEV
        sp = max(s for s in (1, 2, 4, 8) if ns * s <= GRAD_COLS or s == 1)
        tn = ns * sp
    else:
        tn = _pick(N, GRAD_COLS)

    def body(a_ref, b_ref, *rest):
        o_ref, acc = rest[-2:]
        k = pl.program_id(2)
        t = _dot_tn(a_ref[...], b_ref[...])

        @pl.when(k == 0)
        def _():
            acc[...] = t

        @pl.when(k > 0)
        def _():
            acc[...] += t

        @pl.when(k == nk - 1)
        def _():
            if col_slots:
                for s in range(sp):
                    o_ref[s] = acc[:, s * ns:(s + 1) * ns].astype(BF16)
            else:
                o_ref[...] = acc[...].astype(BF16)

    if col_slots:
        out_spec, out_shape = pl.BlockSpec((sp, tmm, ns), lambda i, j, k: (j, i, 0)), _sds((N_DEV, M, ns), BF16)
    else:
        out_spec, out_shape = pl.BlockSpec((tmm, tn), lambda i, j, k: (i, j)), _sds((M, N), BF16)
    out = pl.pallas_call(
        body, grid=(M // tmm, N // tn, nk),
        in_specs=[pl.BlockSpec((tk, tmm), lambda i, j, k: (k, i)), pl.BlockSpec((tk, tn), lambda i, j, k: (k, j))]
        + [pl.BlockSpec(memory_space=pl.ANY)] * len(extra),
        out_specs=out_spec, out_shape=out_shape,
        scratch_shapes=[pltpu.VMEM((tmm, tn), F32)],
        compiler_params=_cp("arbitrary", "arbitrary", "arbitrary"), name=name)(a, b, *extra)
    return out if col_slots else out.reshape(N_DEV, M // N_DEV, N)


def conv_fwd(x, vec, cw, w_in, w_out):
    T, D = x.shape
    tm = min(TOKEN_TILE, T)

    def body(x_ref, vec_ref, cw_ref, wi_ref, wo_ref, xn_ref, h_ref, bcu_ref, cv_ref, z_ref, y_ref, vbuf):
        @pl.when(pl.program_id(0) == 0)
        def _():
            vbuf[0:8, :] = jnp.zeros((8, D), F32)

        x_t = x_ref[...]
        hb = _modnorm(x_t, vec_ref[0:1], vec_ref[1:2], vec_ref[2:3]).astype(BF16)
        h_ref[...] = hb
        bcu = _dot(hb, wi_ref[...])
        bcu_ref[...] = bcu.astype(BF16)
        bg, v = bcu[:, 0:D], bcu[:, D:2 * D] * bcu[:, 2 * D:3 * D]
        vbuf[8:8 + tm, :] = v
        conv = cw_ref[0:1] * vbuf[6:6 + tm, :] + cw_ref[1:2] * vbuf[7:7 + tm, :] + cw_ref[2:3] * v
        cv_ref[...] = conv.astype(BF16)
        zb = (bg * conv).astype(BF16)
        z_ref[...] = zb
        y = _dot(zb, wo_ref[...])
        y_ref[...] = y.astype(BF16)
        xn_ref[...] = x_t + (1.0 + vec_ref[3:4]) * y
        vbuf[0:8, :] = vbuf[tm:tm + 8, :]

    return pl.pallas_call(
        body, grid=(T // tm,),
        in_specs=[_tok(tm, D), _res((8, D)), _res((8, D)), _res((D, 3 * D)), _res((D, D))],
        out_specs=[_tok(tm, D), _tok(tm, D), _tok(tm, 3 * D), _tok(tm, D), _tok(tm, D), _tok(tm, D)],
        out_shape=[_sds((T, D), F32), _sds((T, D), BF16), _sds((T, 3 * D), BF16), _sds((T, D), BF16),
                   _sds((T, D), BF16), _sds((T, D), BF16)],
        scratch_shapes=[pltpu.VMEM((tm + 8, D), F32)],
        compiler_params=_cp("arbitrary"), name="conv_fwd")(x, vec, cw, w_in, w_out)


def conv_bwd(dxo, x, y, bcu, cv, vec, cw, w_in, w_out):
    T, D = x.shape
    tm = min(TOKEN_TILE, T)
    nt = T // tm

    def body(dxo_ref, x_ref, y_ref, bcu_ref, cv_ref, vec_ref, cw_ref, wi_ref, wo_ref,
             dx_ref, dy_ref, dbcu_ref, part_ref, dcw_ref, dcbuf):
        first = pl.program_id(0) == 0

        @pl.when(first)
        def _():
            dcbuf[tm:tm + 8, :] = jnp.zeros((8, D), F32)

        dxo_t = dxo_ref[...]
        dyb = (dxo_t * (1.0 + vec_ref[3:4])).astype(BF16)
        dy_ref[...] = dyb
        dgate = jnp.sum(dxo_t * y_ref[...].astype(F32), axis=0, keepdims=True)
        dz = _dot_nt(dyb, wo_ref[...])
        bcu_t = bcu_ref[...].astype(F32)
        bg, cg, ug = bcu_t[:, 0:D], bcu_t[:, D:2 * D], bcu_t[:, 2 * D:3 * D]
        dconv = dz * bg
        dbg = dz * cv_ref[...].astype(F32)
        dcbuf[0:tm, :] = dconv
        d1, d2 = dcbuf[1:tm + 1, :], dcbuf[2:tm + 2, :]
        dv = cw_ref[2:3] * dconv + cw_ref[1:2] * d1 + cw_ref[0:1] * d2
        v = cg * ug
        dcw = _rows8([jnp.sum(d2 * v, axis=0, keepdims=True), jnp.sum(d1 * v, axis=0, keepdims=True),
                      jnp.sum(dconv * v, axis=0, keepdims=True)], D)
        dbcu = jnp.concatenate([dbg, dv * ug, dv * cg], axis=1).astype(BF16)
        dbcu_ref[...] = dbcu
        dh = _dot_nt(dbcu, wi_ref[...])
        _, vjp = jax.vjp(_modnorm, x_ref[...], vec_ref[0:1], vec_ref[1:2], vec_ref[2:3])
        dx, dg, dsh, dsc = vjp(dh)
        dx_ref[...] = dxo_t + dx
        _acc_rows(part_ref, _rows8([dg, dsh, dsc, dgate], D), first)
        _acc_rows(dcw_ref, dcw, first)
        dcbuf[tm:tm + 8, :] = dcbuf[0:8, :]

    def rev(w):
        return pl.BlockSpec((tm, w), lambda i: (nt - 1 - i, 0))

    return pl.pallas_call(
        body, grid=(nt,),
        in_specs=[rev(D), rev(D), rev(D), rev(3 * D), rev(D), _res((8, D)), _res((8, D)), _res((D, 3 * D)), _res((D, D))],
        out_specs=[rev(D), rev(D), rev(3 * D), pl.BlockSpec((8, D), lambda i: (0, 0)), pl.BlockSpec((8, D), lambda i: (0, 0))],
        out_shape=[_sds((T, D), F32), _sds((T, D), BF16), _sds((T, 3 * D), BF16), _sds((8, D), F32), _sds((8, D), F32)],
        scratch_shapes=[pltpu.VMEM((tm + 8, D), F32)],
        compiler_params=_cp("arbitrary"), name="conv_bwd")(dxo, x, y, bcu, cv, vec, cw, w_in, w_out)


def rope_tables(pos, lane_rows):
    T = pos.shape[0]
    tm = min(TOKEN_TILE, T)

    def body(p_ref, lr_ref, c_ref, sp_ref, sm_ref):
        ang = p_ref[...].astype(F32) * lr_ref[0:1]
        cs, sn = jnp.cos(ang), jnp.sin(ang)
        c_ref[...] = jnp.where(lr_ref[1:2] > 0.5, cs, 1.0)
        sp_ref[...] = jnp.where(lr_ref[2:3] > 0.5, sn, 0.0)
        sm_ref[...] = jnp.where(lr_ref[3:4] > 0.5, -sn, 0.0)

    return pl.pallas_call(
        body, grid=(T // tm,),
        in_specs=[_tok(tm, 1), _res((8, LANES))],
        out_specs=[_tok(tm, LANES)] * 3,
        out_shape=[_sds((T, LANES), F32)] * 3,
        compiler_params=_cp("arbitrary"), name="rope_tables")(pos, lane_rows)


def _rope(t, c, sp, sm):
    w = t.shape[1]
    reps = w // LANES
    cf, spf, smf = jnp.tile(c, (1, reps)), jnp.tile(sp, (1, reps)), jnp.tile(sm, (1, reps))
    half = ROPE_DIM // 2
    return t * cf + pltpu.roll(t, half, axis=1) * spf + pltpu.roll(t, w - half, axis=1) * smf


def _rope_t(d, c, sp, sm):
    w = d.shape[1]
    reps = w // LANES
    cf, spf, smf = jnp.tile(c, (1, reps)), jnp.tile(sp, (1, reps)), jnp.tile(sm, (1, reps))
    half = ROPE_DIM // 2
    return d * cf + pltpu.roll(d * spf, w - half, axis=1) + pltpu.roll(d * smf, half, axis=1)


def _split_residues(v, d, stage):
    tm, width = v.shape
    if d == 1:
        return [v]
    nj = width // LANES
    for j in range(nj):
        stage[j] = v[:, j * LANES:(j + 1) * LANES]
    return [jnp.concatenate([stage[j, pl.ds(r, tm // d, stride=d), :] for j in range(nj)], axis=1) for r in range(d)]


def _merge_residues(piece, d, tm, width, stage):
    if d == 1:
        return piece(0)
    nj = width // LANES
    for r in range(d):
        p = piece(r)
        for j in range(nj):
            stage[j, pl.ds(r, tm // d, stride=d), :] = p[:, j * LANES:(j + 1) * LANES]
    return jnp.concatenate([stage[j] for j in range(nj)], axis=1)


def _residue_spec(d, tm, width=GROUP_WIDTH):
    return pl.BlockSpec((d, tm // d, width), lambda i: (0, i, 0))


def _stage_scratch(tm):
    return pltpu.VMEM((GROUP_WIDTH // LANES, tm, LANES), F32)


def proj_rope_fwd(x, vec, w, tabs, n_rope, transposed, dils, name):
    T, D = x.shape
    N = w.shape[0] if transposed else w.shape[1]
    tm = min(TOKEN_TILE, T)
    GW = GROUP_WIDTH
    piece_dils = [dils[j % len(dils)] for j in range(N // GW)]

    def body(x_ref, vec_ref, w_ref, c_ref, sp_ref, sm_ref, h_ref, *rest):
        out_refs, stage = rest[:-1], rest[-1]
        hb = _modnorm(x_ref[...], vec_ref[0:1], vec_ref[1:2], vec_ref[2:3]).astype(BF16)
        h_ref[...] = hb
        p = _dot_nt(hb, w_ref[...]) if transposed else _dot(hb, w_ref[...])
        pr = _rope(p[:, 0:n_rope], c_ref[...], sp_ref[...], sm_ref[...])
        for j, d in enumerate(piece_dils):
            src = pr if (j + 1) * GW <= n_rope else p
            for r, rows in enumerate(_split_residues(src[:, j * GW:(j + 1) * GW], d, stage)):
                out_refs[j][r] = rows.astype(BF16)

    return pl.pallas_call(
        body, grid=(T // tm,),
        in_specs=[_tok(tm, D), _res((8, D)), _res(w.shape)] + [_tok(tm, LANES)] * 3,
        out_specs=[_tok(tm, D)] + [_residue_spec(d, tm) for d in piece_dils],
        out_shape=[_sds((T, D), BF16)] + [_sds((d, T // d, GW), BF16) for d in piece_dils],
        scratch_shapes=[_stage_scratch(tm)],
        compiler_params=_cp("arbitrary"), name=name)(x, vec, w, *tabs)


def proj_rope_bwd(dparts, dils, x, dxo, vec, w, tabs, n_rope, transposed, name):
    T, D = x.shape
    N = w.shape[0] if transposed else w.shape[1]
    tm = min(TOKEN_TILE, T)
    GW = GROUP_WIDTH
    npart = len(dparts)
    piece_dils = [dils[j % len(dils)] for j in range(npart)]

    def body(*refs):
        d_refs = refs[:npart]
        x_ref, dxo_ref, vec_ref, w_ref, c_ref, sp_ref, sm_ref, dx_ref, dp_ref, part_ref, stage = refs[npart:]
        d = jnp.concatenate([_merge_residues(lambda r, ref=ref: ref[r].astype(F32), dd, tm, GW, stage)
                             for ref, dd in zip(d_refs, piece_dils)], axis=1)
        dr = _rope_t(d[:, 0:n_rope], c_ref[...], sp_ref[...], sm_ref[...])
        if n_rope < N:
            dr = jnp.concatenate([dr, d[:, n_rope:N]], axis=1)
        dpb = dr.astype(BF16)
        dp_ref[...] = dpb
        dh = _dot(dpb, w_ref[...]) if transposed else _dot_nt(dpb, w_ref[...])
        _, vjp = jax.vjp(_modnorm, x_ref[...], vec_ref[0:1], vec_ref[1:2], vec_ref[2:3])
        dx, dg, dsh, dsc = vjp(dh)
        dx_ref[...] = dxo_ref[...] + dx
        _acc_rows(part_ref, _rows8([dg, dsh, dsc], D), pl.program_id(0) == 0)

    return pl.pallas_call(
        body, grid=(T // tm,),
        in_specs=[_residue_spec(d, tm) for d in piece_dils] + [_tok(tm, D), _tok(tm, D), _res((8, D)), _res(w.shape)]
        + [_tok(tm, LANES)] * 3,
        out_specs=[_tok(tm, D), _tok(tm, N), pl.BlockSpec((8, D), lambda i: (0, 0))],
        out_shape=[_sds((T, D), F32), _sds((T, N), BF16), _sds((8, D), F32)],
        scratch_shapes=[_stage_scratch(tm)],
        compiler_params=_cp("arbitrary"), name=name)(*dparts, x, dxo, vec, w, *tabs)


def _valid_mask(n, i):
    qi = lax.broadcasted_iota(jnp.int32, (n, 2 * n), 0)
    kj = lax.broadcasted_iota(jnp.int32, (n, 2 * n), 1)
    dist = n + qi - kj
    return (dist >= 0) & (dist <= n) & ((kj >= n) | (i > 0))


def _band_specs(n):
    two = pl.BlockSpec((None, 2 * n, GROUP_WIDTH), lambda r, i: (r, i, 0))
    prv = pl.BlockSpec((None, n, GROUP_WIDTH), lambda r, i: (r, jnp.maximum(2 * i - 1, 0), 0))
    one = pl.BlockSpec((None, n, GROUP_WIDTH), lambda r, i: (r, i, 0))
    return two, prv, one


def _pair_keys(prev_ref, two_ref, ps, n):
    cur2 = two_ref[:, ps]
    return jnp.concatenate([prev_ref[:, ps], cur2[0:n]], axis=0), cur2


STAT_STRIDE = LANES // HEADS_PER_GROUP


def _head_of_lane():
    return lax.broadcasted_iota(jnp.int32, (1, LANES), 1) // STAT_STRIDE


def attn_core_fwd(q, k, v, g, n):
    d, M, GW = q.shape
    scale = HEAD_DIM ** -0.5

    qb = min(FWD_QUERY_BLOCKS, M // n)

    def body(q_ref, kp_ref, kc_ref, vp_ref, vc_ref, o_ref, l_ref):
        masks = [_valid_mask(n, pl.program_id(1))] + [_valid_mask(n, 1)] * (qb - 1)
        first = lax.broadcasted_iota(jnp.int32, (1, LANES), 1) < HEAD_DIM
        head_of_lane = _head_of_lane()
        lse = [jnp.zeros((n, LANES), F32) for _ in range(qb)]
        for pair in range(HEADS_PER_GROUP * HEAD_DIM // LANES):
            ps = slice(LANES * pair, LANES * (pair + 1))
            kall = jnp.concatenate([kp_ref[:, ps], kc_ref[:, ps]], axis=0)
            vall = jnp.concatenate([vp_ref[:, ps], vc_ref[:, ps]], axis=0)
            for blk in range(qb):
                rows = slice(blk * n, (blk + 1) * n)
                keys, vals = kall[blk * n:(blk + 2) * n], vall[blk * n:(blk + 2) * n]
                q2 = q_ref[rows, ps]
                o2, l2 = [], []
                for sel in (first, jnp.logical_not(first)):
                    s = jnp.where(masks[blk], _dot_nt(jnp.where(sel, q2, jnp.zeros_like(q2)), keys) * scale, -1e30)
                    m = jnp.max(s, axis=1, keepdims=True)
                    p = jnp.exp(s - m)
                    den = jnp.sum(p, axis=1, keepdims=True)
                    o2.append(_dot((p / den).astype(BF16), vals))
                    l2.append(m + jnp.log(den))
                o_ref[rows, ps] = jnp.where(first, o2[0], o2[1]).astype(BF16)
                for half in range(2):
                    lse[blk] = jnp.where(head_of_lane == 2 * pair + half, l2[half], lse[blk])
        for blk in range(qb):
            l_ref[blk * n:(blk + 1) * n, :] = lse[blk]

    two = pl.BlockSpec((None, qb * n, GW), lambda r, i: (r, i, 0))
    prv = pl.BlockSpec((None, n, GW), lambda r, i: (r, jnp.maximum(qb * i - 1, 0), 0))
    stat = pl.BlockSpec((None, qb * n, LANES), lambda r, i: (r, i, 0))
    return pl.pallas_call(
        body, grid=(d, M // (qb * n)),
        in_specs=[two, prv, two, prv, two], out_specs=[two, stat],
        out_shape=[_sds((d, M, GW), BF16), _sds((d, M, LANES), F32)],
        compiler_params=_cp("arbitrary", "arbitrary"), name=f"attn_fwd_g{g}")(q, k, k, v, v)


def attn_core_bwd(q, k, v, do, delta, lse, g, n):
    d, M, GW = q.shape
    scale = HEAD_DIM ** -0.5

    def body(q_ref, kp_ref, kc_ref, vp_ref, vc_ref, do_ref, d_ref, l_ref, dq_ref, dkc_ref, dkp_ref, dvc_ref, dvp_ref):
        masks = (_valid_mask(n, pl.program_id(1)), _valid_mask(n, 1))
        first = lax.broadcasted_iota(jnp.int32, (1, LANES), 1) < HEAD_DIM
        for pair in range(HEADS_PER_GROUP * HEAD_DIM // LANES):
            ps = slice(LANES * pair, LANES * (pair + 1))
            keys, vals = _pair_keys(kp_ref, kc_ref, ps, n), _pair_keys(vp_ref, vc_ref, ps, n)
            own = []
            for blk in range(2):
                rows = slice(blk * n, (blk + 1) * n)
                q2, do2 = q_ref[rows, ps], do_ref[rows, ps]
                dq2, dk, dv = [], None, None
                for half, sel in enumerate((first, jnp.logical_not(first))):
                    qm = jnp.where(sel, q2, jnp.zeros_like(q2))
                    dom = jnp.where(sel, do2, jnp.zeros_like(do2))
                    s = jnp.where(masks[blk], _dot_nt(qm, keys[blk]) * scale, -1e30)
                    lane0 = STAT_STRIDE * (2 * pair + half)
                    p = jnp.exp(s - l_ref[rows, lane0:lane0 + 1])
                    dp = _dot_nt(dom, vals[blk])
                    ds = (p * (dp - d_ref[rows, lane0:lane0 + 1]) * scale).astype(BF16)
                    dq2.append(_dot(ds, keys[blk]))
                    dkh = _dot_tn(ds, qm)
                    dvh = _dot_tn(p.astype(BF16), dom)
                    dk = dkh if dk is None else dk + dkh
                    dv = dvh if dv is None else dv + dvh
                dq_ref[rows, ps] = jnp.where(first, dq2[0], dq2[1]).astype(BF16)
                own.append((dk, dv))
            for t, (c_ref, p_ref) in enumerate(((dkc_ref, dkp_ref), (dvc_ref, dvp_ref))):
                a, b = own[0][t], own[1][t]
                p_ref[:, ps] = a[0:n].astype(BF16)
                c_ref[0:n, ps] = (a[n:2 * n] + b[0:n]).astype(BF16)
                c_ref[n:2 * n, ps] = b[n:2 * n].astype(BF16)

    two, prv, one = _band_specs(n)
    stat = pl.BlockSpec((None, 2 * n, LANES), lambda r, i: (r, i, 0))
    return pl.pallas_call(
        body, grid=(d, M // (2 * n)),
        in_specs=[two, prv, two, prv, two, two, stat, stat], out_specs=[two, two, one, two, one],
        out_shape=[_sds((d, M, GW), BF16), _sds((d, M, GW), BF16), _sds((d, M // 2, GW), BF16),
                   _sds((d, M, GW), BF16), _sds((d, M // 2, GW), BF16)],
        compiler_params=_cp("arbitrary", "arbitrary"), name=f"attn_bwd_g{g}")(q, k, k, v, v, do, delta, lse)


def dkv_combine(cur_prev, n, name):
    d, M, GW = cur_prev[0][0].shape
    rows = min(M, 1024)
    pairs = rows // (2 * n)
    steps = M // rows
    flat = [a for pair in cur_prev for a in pair]

    def body(*refs):
        o_ref = refs[-1]
        last = pl.program_id(1) == steps - 1
        acc = None
        shifted = None
        for t in range(0, len(refs) - 1, 3):
            c = refs[t][...].astype(F32)
            nxt = jnp.where(last, 0.0, refs[t + 2][...].astype(F32))
            s = nxt if pairs == 1 else jnp.concatenate([refs[t + 1][n:pairs * n, :].astype(F32), nxt], axis=0)
            acc = c if acc is None else acc + c
            shifted = s if shifted is None else shifted + s
        for m in range(pairs):
            lo = 2 * m * n
            o_ref[lo:lo + n, :] = acc[lo:lo + n].astype(BF16)
            o_ref[lo + n:lo + 2 * n, :] = (acc[lo + n:lo + 2 * n] + shifted[m * n:(m + 1) * n]).astype(BF16)

    cur = pl.BlockSpec((None, rows, GW), lambda r, i: (r, i, 0))
    same = pl.BlockSpec((None, pairs * n, GW), lambda r, i: (r, i, 0))
    nxt = pl.BlockSpec((None, n, GW), lambda r, i: (r, jnp.minimum((i + 1) * pairs, M // (2 * n) - 1), 0))
    args = []
    for c, p in cur_prev:
        args += [c, p, p]
    return pl.pallas_call(
        body, grid=(d, steps), in_specs=[cur, same, nxt] * len(cur_prev), out_specs=cur,
        out_shape=_sds((d, M, GW), BF16),
        compiler_params=_cp("arbitrary", "arbitrary"), name=name)(*args)


def _group_weights(ls):
    mx = functools.reduce(jnp.maximum, ls)
    es = [jnp.exp(l - mx) for l in ls]
    tot = functools.reduce(lambda a, b: a + b, es)
    return [e / tot for e in es]


def _expand_heads(w):
    tm = w.shape[0]
    first = lax.broadcasted_iota(jnp.int32, (1, LANES), 1) < HEAD_DIM
    cols = [jnp.broadcast_to(w[:, STAT_STRIDE * h:STAT_STRIDE * h + 1], (tm, LANES)) for h in range(HEADS_PER_GROUP)]
    return jnp.concatenate([jnp.where(first, cols[2 * p], cols[2 * p + 1]) for p in range(HEADS_PER_GROUP // 2)], axis=1)


def _head_sums(r):
    width = r.shape[1]
    feat_head = lax.broadcasted_iota(jnp.int32, (width, LANES), 0) // HEAD_DIM
    stat_head = lax.broadcasted_iota(jnp.int32, (width, LANES), 1) // STAT_STRIDE
    ones = jnp.where(feat_head == stat_head, 1.0, 0.0).astype(BF16)
    hi = r.astype(BF16)
    lo = (r - hi.astype(F32)).astype(BF16)
    return _dot(hi, ones) + _dot(lo, ones)


def _mix_weights(l_refs, dils, tm, stage):
    ls = [_merge_residues(lambda r, ref=ref: ref[r], d, tm, LANES, stage) for ref, d in zip(l_refs, dils)]
    return [_expand_heads(w) for w in _group_weights(ls)]


def attn_mix_out(os_, ls, dils, x, vec, w_o):
    T, D = x.shape
    GW = GROUP_WIDTH
    tm = min(TOKEN_TILE, T)
    ng = len(os_)

    def body(*refs):
        o_refs, l_refs = refs[:ng], refs[ng:2 * ng]
        x_ref, vec_ref, w_ref, xn_ref, mix_ref, y_ref, stage = refs[2 * ng:]
        natural = lambda ref, d: _merge_residues(lambda r: ref[r].astype(F32), d, tm, GW, stage)
        ws = _mix_weights(l_refs, dils, tm, stage)
        mixed = functools.reduce(lambda a, b: a + b, [w * natural(r, d) for w, r, d in zip(ws, o_refs, dils)])
        mb = mixed.astype(BF16)
        mix_ref[...] = mb
        y = _dot(mb, w_ref[...])
        y_ref[...] = y.astype(BF16)
        xn_ref[...] = x_ref[...] + (1.0 + vec_ref[3:4]) * y

    res = [_residue_spec(d, tm) for d in dils]
    stat = [_residue_spec(d, tm, LANES) for d in dils]
    return pl.pallas_call(
        body, grid=(T // tm,),
        in_specs=res + stat + [_tok(tm, D), _res((8, D)), _res((GW, D))],
        out_specs=[_tok(tm, D), _tok(tm, GW), _tok(tm, D)],
        out_shape=[_sds((T, D), F32), _sds((T, GW), BF16), _sds((T, D), BF16)],
        scratch_shapes=[_stage_scratch(tm)],
        compiler_params=_cp("arbitrary"), name="attn_mix_out")(*os_, *ls, x, vec, w_o)


def attn_mix_bwd(dxo, y, vec, w_o, os_, ls, dils):
    T, D = dxo.shape
    GW = GROUP_WIDTH
    tm = min(TOKEN_TILE, T)
    ng = len(os_)

    def body(*refs):
        dxo_ref, y_ref, vec_ref, w_ref = refs[:4]
        o_refs, l_refs = refs[4:4 + ng], refs[4 + ng:4 + 2 * ng]
        dy_ref = refs[4 + 2 * ng]
        do_refs = refs[5 + 2 * ng:5 + 3 * ng]
        d_refs = refs[5 + 3 * ng:5 + 4 * ng]
        part_ref, stage = refs[5 + 4 * ng], refs[6 + 4 * ng]
        natural = lambda ref, d: _merge_residues(lambda r: ref[r].astype(F32), d, tm, GW, stage)
        dxo_t = dxo_ref[...]
        dyb = (dxo_t * (1.0 + vec_ref[3:4])).astype(BF16)
        dy_ref[...] = dyb
        dgate = jnp.sum(dxo_t * y_ref[...].astype(F32), axis=0, keepdims=True)
        _acc_rows(part_ref, _rows8([dgate], D), pl.program_id(0) == 0)
        dmix = _dot_nt(dyb, w_ref[...])
        ws = _mix_weights(l_refs, dils, tm, stage)
        mixed = functools.reduce(lambda a, b: a + b, [w * natural(r, d) for w, r, d in zip(ws, o_refs, dils)])
        for gi in range(ng):
            do = ws[gi] * dmix
            for r, rows in enumerate(_split_residues(do, dils[gi], stage)):
                do_refs[gi][r] = rows.astype(BF16)
            for r, rows in enumerate(_split_residues(_head_sums(do * mixed), dils[gi], stage)):
                d_refs[gi][r] = rows

    res = [_residue_spec(d, tm) for d in dils]
    stat = [_residue_spec(d, tm, LANES) for d in dils]
    return pl.pallas_call(
        body, grid=(T // tm,),
        in_specs=[_tok(tm, D), _tok(tm, D), _res((8, D)), _res((GW, D))] + res + stat,
        out_specs=[_tok(tm, D)] + res + stat + [pl.BlockSpec((8, D), lambda i: (0, 0))],
        out_shape=[_sds((T, D), BF16)] + [_sds((d, T // d, GW), BF16) for d in dils]
        + [_sds((d, T // d, LANES), F32) for d in dils] + [_sds((8, D), F32)],
        scratch_shapes=[_stage_scratch(tm)],
        compiler_params=_cp("arbitrary"), name="attn_mix_bwd")(dxo, y, vec, w_o, *os_, *ls)


def final_loss(x, gvec, target):
    T, D = x.shape
    tm = min(TOKEN_TILE, T)

    def norm(xv, g):
        return xv * lax.rsqrt(jnp.mean(xv * xv, axis=-1, keepdims=True) + NORM_EPS) * g

    def body(x_ref, g_ref, t_ref, dx_ref, part_ref, loss_ref):
        first = pl.program_id(0) == 0
        yv, vjp = jax.vjp(norm, x_ref[...], g_ref[0:1])
        err = yv - t_ref[...]
        dx, dg = vjp(err * (1.0 / D))
        dx_ref[...] = dx
        _acc_rows(part_ref, _rows8([dg], D), first)
        tile_loss = 0.5 * jnp.sum(jnp.sum(err * err, axis=1, keepdims=True) * (1.0 / D), axis=0, keepdims=True)
        _acc_rows(loss_ref, jnp.broadcast_to(tile_loss, (8, LANES)), first)

    return pl.pallas_call(
        body, grid=(T // tm,),
        in_specs=[_tok(tm, D), _res((8, D)), _tok(tm, D)],
        out_specs=[_tok(tm, D), pl.BlockSpec((8, D), lambda i: (0, 0)), pl.BlockSpec((8, LANES), lambda i: (0, 0))],
        out_shape=[_sds((T, D), F32), _sds((8, D), F32), _sds((8, LANES), F32)],
        compiler_params=_cp("arbitrary"), name="final_loss")(x, gvec, target)


def mods_project(c_all, w, b):
    B, D = c_all.shape
    L, _, N = w.shape

    def body(c_ref, w_ref, b_ref, o_ref):
        cv = c_ref[...]
        cond = cv * _sigmoid(cv)
        o_ref[0] = jnp.dot(cond, w_ref[0], preferred_element_type=F32, precision=lax.Precision.HIGHEST) + b_ref[0]

    return pl.pallas_call(
        body, grid=(L,),
        in_specs=[pl.BlockSpec((B, D), lambda l: (0, 0)), pl.BlockSpec((1, D, N), lambda l: (l, 0, 0)),
                  pl.BlockSpec((1, 1, N), lambda l: (l, 0, 0))],
        out_specs=pl.BlockSpec((1, B, N), lambda l: (l, 0, 0)),
        out_shape=_sds((L, B, N), F32),
        compiler_params=_cp("arbitrary"), name="mods_project")(c_all, w, b)


def mods_weight_grad(c_all, dm):
    B, D = c_all.shape
    L, _, N = dm.shape

    def body(c_ref, d_ref, o_ref):
        cv = c_ref[...]
        cond = cv * _sigmoid(cv)
        o_ref[0] = lax.dot_general(cond, d_ref[0], (((0,), (0,)), ((), ())), preferred_element_type=F32,
                                   precision=lax.Precision.HIGHEST)

    return pl.pallas_call(
        body, grid=(L,),
        in_specs=[pl.BlockSpec((B, D), lambda l: (0, 0)), pl.BlockSpec((1, B, N), lambda l: (l, 0, 0))],
        out_specs=pl.BlockSpec((1, D, N), lambda l: (l, 0, 0)),
        out_shape=_sds((L, D, N), F32),
        compiler_params=_cp("arbitrary"), name="mods_weight_grad")(c_all, dm)


def _adam_math(g, w, m, v):
    m2 = ADAM_B1 * m + (1.0 - ADAM_B1) * g
    v2 = ADAM_B2 * v + (1.0 - ADAM_B2) * (g * g)
    m_hat = m2 / (1.0 - ADAM_B1 ** ADAM_STEP)
    v_hat = v2 / (1.0 - ADAM_B2 ** ADAM_STEP)
    delta = -ADAM_LR * (m_hat / (jnp.sqrt(v_hat) + ADAM_EPS) + ADAM_WD * w)
    return delta, m2, v2


def adam_update(g, w, m, v, parts, name):
    R, C = w.shape
    tr = _pick(R, 256, 8)

    def body(g_ref, w_ref, m_ref, v_ref, go_ref, d_ref, mo_ref, vo_ref):
        if parts:
            gv = g_ref[0].astype(F32)
            for s in range(1, N_DEV):
                gv = gv + g_ref[s].astype(F32)
        else:
            gv = g_ref[...]
        go_ref[...] = gv
        d_ref[...], mo_ref[...], vo_ref[...] = _adam_math(gv, w_ref[...], m_ref[...], v_ref[...])

    gspec = pl.BlockSpec((N_DEV, tr, C), lambda i: (0, i, 0)) if parts else _tok(tr, C)
    return pl.pallas_call(
        body, grid=(R // tr,),
        in_specs=[gspec, _tok(tr, C), _tok(tr, C), _tok(tr, C)],
        out_specs=[_tok(tr, C)] * 4, out_shape=[_sds((R, C), F32)] * 4,
        compiler_params=_cp("arbitrary"), name=name)(g, w, m, v)


def adam_layer(parts, w, m, v, prev, layer, after, name):
    L, R, C = w.shape
    tr = _pick(R, 256, 8)
    prev = (list(prev) if prev is not None else []) + [after]

    def body(p_ref, w_ref, m_ref, v_ref, *rest):
        go_ref, d_ref, mo_ref, vo_ref = rest[-4:]
        gv = p_ref[0].astype(F32)
        for s in range(1, N_DEV):
            gv = gv + p_ref[s].astype(F32)
        go_ref[...] = gv
        d_ref[...], mo_ref[...], vo_ref[...] = _adam_math(gv, w_ref[...], m_ref[...], v_ref[...])

    lay = pl.BlockSpec((None, tr, C), lambda i: (layer, i, 0))
    return pl.pallas_call(
        body, grid=(R // tr,),
        in_specs=[pl.BlockSpec((N_DEV, tr, C), lambda i: (0, i, 0)), lay, lay, lay] + [pl.BlockSpec(memory_space=pl.ANY)] * len(prev),
        out_specs=[lay] * 4, out_shape=[_sds((L, R, C), F32)] * 4,
        input_output_aliases={4 + k: k for k in range(len(prev) - 1)},
        compiler_params=_cp("arbitrary"), name=name)(parts, w, m, v, *prev)


def _my_id():
    return 4 * lax.axis_index("x") + 2 * lax.axis_index("y") + lax.axis_index("c")


def _peer(s):
    x, y, c = lax.axis_index("x"), lax.axis_index("y"), lax.axis_index("c")
    px = (1 - x) if s & 4 else x
    py = (1 - y) if s & 2 else y
    pc = (1 - c) if s & 1 else c
    return (px, py, pc), 4 * px + 2 * py + pc


def all_gather(xs, space, name):
    na = len(xs)

    def body(*refs):
        x_refs, o_refs = refs[:na], refs[na:2 * na]
        send_sems, recv_sems, local_sems = refs[2 * na:]
        me = _my_id()
        locals_, sends = [], []
        for a in range(na):
            cp = pltpu.make_async_copy(x_refs[a], o_refs[a].at[me], local_sems.at[a])
            cp.start()
            locals_.append(cp)
        for s in range(1, N_DEV):
            peer, _ = _peer(s)
            for a in range(na):
                cp = pltpu.make_async_remote_copy(
                    src_ref=x_refs[a], dst_ref=o_refs[a].at[me], send_sem=send_sems.at[a, s - 1],
                    recv_sem=recv_sems.at[a, s - 1], device_id=peer, device_id_type=MESH)
                cp.start()
                sends.append(cp)
        for s in range(1, N_DEV):
            peer, pid = _peer(s)
            for a in range(na):
                pltpu.make_async_remote_copy(
                    src_ref=x_refs[a], dst_ref=o_refs[a].at[pid], send_sem=send_sems.at[a, s - 1],
                    recv_sem=recv_sems.at[a, s - 1], device_id=peer, device_id_type=MESH).wait_recv()
        for cp in sends:
            cp.wait_send()
        for cp in locals_:
            cp.wait()

    spec = pl.BlockSpec(memory_space=space)
    return pl.pallas_call(
        body, in_specs=[spec] * na, out_specs=[spec] * na,
        out_shape=[_sds((N_DEV,) + x.shape, x.dtype) for x in xs],
        scratch_shapes=[pltpu.SemaphoreType.DMA((na, N_DEV - 1)), pltpu.SemaphoreType.DMA((na, N_DEV - 1)),
                        pltpu.SemaphoreType.DMA((na,))],
        compiler_params=pltpu.CompilerParams(vmem_limit_bytes=VMEM_LIMIT), name=name)(*xs)


def exchange_slots(xs, name):
    na = len(xs)

    def body(*refs):
        x_refs, o_refs = refs[:na], refs[na:2 * na]
        send_sems, recv_sems, local_sems = refs[2 * na:]
        me = _my_id()
        locals_, sends = [], []
        for a in range(na):
            cp = pltpu.make_async_copy(x_refs[a].at[me], o_refs[a].at[me], local_sems.at[a])
            cp.start()
            locals_.append(cp)
        for s in range(1, N_DEV):
            peer, pid = _peer(s)
            for a in range(na):
                cp = pltpu.make_async_remote_copy(
                    src_ref=x_refs[a].at[pid], dst_ref=o_refs[a].at[me], send_sem=send_sems.at[a, s - 1],
                    recv_sem=recv_sems.at[a, s - 1], device_id=peer, device_id_type=MESH)
                cp.start()
                sends.append(cp)
        for s in range(1, N_DEV):
            peer, pid = _peer(s)
            for a in range(na):
                pltpu.make_async_remote_copy(
                    src_ref=x_refs[a].at[pid], dst_ref=o_refs[a].at[pid], send_sem=send_sems.at[a, s - 1],
                    recv_sem=recv_sems.at[a, s - 1], device_id=peer, device_id_type=MESH).wait_recv()
        for cp in sends:
            cp.wait_send()
        for cp in locals_:
            cp.wait()

    spec = pl.BlockSpec(memory_space=pl.ANY)
    return pl.pallas_call(
        body, in_specs=[spec] * na, out_specs=[spec] * na,
        out_shape=[_sds(x.shape, x.dtype) for x in xs],
        scratch_shapes=[pltpu.SemaphoreType.DMA((na, N_DEV - 1)), pltpu.SemaphoreType.DMA((na, N_DEV - 1)),
                        pltpu.SemaphoreType.DMA((na,))],
        compiler_params=pltpu.CompilerParams(vmem_limit_bytes=VMEM_LIMIT), name=name)(*xs)


_HBM = pl.BlockSpec(memory_space=pltpu.HBM)
_SEM = pl.BlockSpec(memory_space=pltpu.SEMAPHORE)
_EFFECT = pltpu.SideEffectType.DATAFLOW_SIDE_EFFECTING


def _split_copies(pattern, x_ref, land_ref, send_sem, recv_sem):
    me = _my_id()
    if pattern in ("gather", "scatter"):
        plan = []
        for s in range(1, N_DEV):
            peer, pid = _peer(s)
            plan.append((x_ref.at[pid] if pattern == "scatter" else x_ref, land_ref.at[me], peer))
    elif pattern == "to_chips":
        plan = [(x_ref, land_ref.at[me], _peer(s)[0]) for s in (1, 2, 4, 6)]
    else:
        sibling = _peer(1)[0]
        plan = [(land_ref.at[_peer(s)[1]], land_ref.at[_peer(s)[1]], sibling) for s in (2, 4, 6)]
    return [pltpu.make_async_remote_copy(src_ref=src, dst_ref=dst, send_sem=send_sem, recv_sem=recv_sem,
                                         device_id=dev, device_id_type=MESH) for src, dst, dev in plan]


def comm_start(xs, pattern, after, name, lands=None):
    na = len(xs)
    extra = [] if after is None else [after]
    me = _my_id()
    if lands is None:
        lands = []
        for x in xs:
            shape = x.shape if pattern == "scatter" else (N_DEV,) + x.shape
            own = lax.dynamic_slice_in_dim(x, me, 1, 0) if pattern == "scatter" else x[None]
            lands.append(lax.dynamic_update_slice(lax.empty(shape, x.dtype), own, (me,) + (0,) * (len(shape) - 1)))

    def body(*refs):
        x_refs, land_refs = refs[:na], refs[na:2 * na]
        send_sem, recv_sem = refs[2 * na + len(extra)], refs[2 * na + len(extra) + 1]
        token = refs[-1]
        for a in range(na):
            for cp in _split_copies(pattern, x_refs[a], land_refs[a], send_sem, recv_sem):
                cp.start()
        token[...] = jnp.zeros_like(token)

    outs = pl.pallas_call(
        body, name=name,
        out_shape=(pltpu.SemaphoreType.DMA(()), pltpu.SemaphoreType.DMA(()))
        + tuple(pltpu.HBM(x.shape, x.dtype) for x in xs) + tuple(pltpu.HBM(l.shape, l.dtype) for l in lands)
        + (_sds((8, LANES), F32),),
        in_specs=(_HBM,) * (2 * na) + (pl.BlockSpec(memory_space=pl.ANY),) * len(extra),
        out_specs=(_SEM, _SEM) + (_HBM,) * (2 * na) + (pl.BlockSpec(memory_space=pltpu.VMEM),),
        input_output_aliases={a: 2 + a for a in range(2 * na)},
        compiler_params=pltpu.CompilerParams(has_side_effects=_EFFECT),
    )(*[pltpu.with_memory_space_constraint(x, pltpu.HBM) for x in xs],
      *[pltpu.with_memory_space_constraint(l, pltpu.HBM) for l in lands], *extra)
    return dict(sems=outs[0:2], xs=outs[2:2 + na], lands=outs[2 + na:2 + 2 * na], token=outs[-1], pattern=pattern)


def comm_wait(started, after, name, with_xs=False):
    xs, lands = started["xs"], started["lands"]
    pattern = started["pattern"]
    na = len(xs)

    def body(*refs):
        x_refs, land_refs = refs[:na], refs[na:2 * na]
        send_sem, recv_sem = refs[2 * na], refs[2 * na + 1]
        for a in range(na):
            for cp in _split_copies(pattern, x_refs[a], land_refs[a], send_sem, recv_sem):
                cp.wait_send()
                cp.wait_recv()

    outs = pl.pallas_call(
        body, name=name,
        out_shape=tuple(pltpu.HBM(x.shape, x.dtype) for x in xs) + tuple(pltpu.HBM(l.shape, l.dtype) for l in lands),
        in_specs=(_HBM,) * (2 * na) + (_SEM, _SEM, pl.BlockSpec(memory_space=pl.ANY)),
        out_specs=(_HBM,) * (2 * na),
        input_output_aliases={a: a for a in range(2 * na)},
        compiler_params=pltpu.CompilerParams(has_side_effects=_EFFECT),
    )(*xs, *lands, *started["sems"], after)
    return (list(outs[na:]), list(outs[:na])) if with_xs else list(outs[na:])


def _cols_to_natural(g):
    return jnp.concatenate([g[k] for k in range(N_DEV)], axis=1)


def _vec8(rows, d):
    rows = [r.reshape(1, d).astype(F32) for r in rows]
    return jnp.concatenate(rows + [jnp.zeros((8 - len(rows), d), F32)], axis=0)


def _ffn_forward(x, vec, w_in_t, w_out):
    xn, h, a, b, u, y = ffn_fwd(x, vec, w_in_t, w_out)
    return xn, (x, h, a, b, u, y)


def _ffn_backward(dxo, saved, vec, w_in_t, w_out, on_rows=None):
    x, h, a, b, u, y = saved
    dy, dab, dx, part = ffn_bwd(dxo, y, vec, w_out, w_in_t, a, b, x)
    rows = part[0:4]
    token = on_rows(rows) if on_rows is not None else None
    g_out = grad_slots(u, dy, "ffn_dw_out", after=token)
    g_in_t = grad_slots(dab, h, "ffn_dw_in", after=token)
    return dx, g_in_t, g_out, rows


_TRANSPOSED = ("ffn1_w_in", "ffn2_w_in", "attn_w_q")
_COL_NATURAL = ("conv_w_in", "w_kv", "attn_w_o")
_ROW_SHARDED = ("ffn1_w_out", "ffn2_w_out", "conv_w_out")
_BIG = _TRANSPOSED + _COL_NATURAL + _ROW_SHARDED


def weight_chunks():
    chunks = []
    for layer in range(DEPTH):
        first = [("ffn1_w_in", layer), ("ffn1_w_out", layer)]
        if layer == N_A_LAYERS:
            first = [("w_kv", layer)] + first
        mixer = [("conv_w_in", layer), ("conv_w_out", layer)] if layer < N_A_LAYERS else [("attn_w_q", layer), ("attn_w_o", layer)]
        rest = mixer + [("ffn2_w_in", layer), ("ffn2_w_out", layer)]
        chunks += [first, rest] if layer == 0 else [first + rest]
    return chunks


def stacked_index(name, layer):
    if name == "w_kv":
        return None
    return layer - N_A_LAYERS if name.startswith("attn") else layer


class ChunkComm:
    def __init__(self, shards):
        self.shards = shards
        self.chunks = weight_chunks()

    def _shard(self, name, layer):
        idx = stacked_index(name, layer)
        return self.shards[name][0 if idx is None else idx]

    def start_gather(self, ci, after):
        xs = [self._shard(n, l).astype(BF16) for n, l in self.chunks[ci]]
        return comm_start(xs, "to_chips", after, f"gather_start_{ci}")

    def relay_gather(self, ci, started, after):
        lands, xs = comm_wait(started, after, f"gather_wait_{ci}", with_xs=True)
        return comm_start(xs, "relay", None, f"gather_relay_{ci}", lands=lands)

    def finish_gather(self, ci, relayed, after):
        lands = comm_wait(relayed, after, f"gather_done_{ci}")
        W = {}
        for key, g in zip(self.chunks[ci], lands):
            W[key] = _cols_to_natural(g) if key[0] in _COL_NATURAL else g.reshape(-1, g.shape[2])
        return W

    def start_exchange(self, ci, slots, after):
        return comm_start([slots[key] for key in self.chunks[ci]], "scatter", after, f"exchange_start_{ci}")

    def finish_exchange(self, ci, started, after):
        lands = comm_wait(started, after, f"exchange_wait_{ci}")
        return dict(zip(self.chunks[ci], lands))


def device_step(x, positions, target, mods, kvmods, small, comm, gather0):
    T, D = x.shape
    groups = DILATED_GROUPS
    dils = [dil for _, dil in groups]
    lane = jnp.arange(LANES) % HEAD_DIM
    inv = ROPE_THETA ** (-jnp.arange(0, ROPE_DIM, 2, dtype=F32) / ROPE_DIM)
    lane_rows = _vec8([jnp.where(lane < ROPE_DIM, inv[lane % (ROPE_DIM // 2)], 0.0), lane < ROPE_DIM,
                       (lane >= ROPE_DIM // 2) & (lane < ROPE_DIM), lane < ROPE_DIM // 2], LANES)
    tabs = rope_tables(positions.reshape(T, 1), lane_rows)

    def after_token(v, token):
        return v if token is None else v + token[0, 0]

    def vec_of(layer, sub):
        return _vec8([small["norm_g"][layer, sub], mods[layer, 3 * sub], mods[layer, 3 * sub + 1], mods[layer, 3 * sub + 2]], D)

    saved = []
    kv_saved = None
    k_sh = v_sh = None
    qw = GROUP_WIDTH * len(groups)
    chunk_of = {key: ci for ci, chunk in enumerate(comm.chunks) for key in chunk}
    W = {}
    flight = {"ci": 0, "started": gather0}

    relayed = {}

    def advance(after):
        ci = flight["ci"]
        if flight["started"] is None or ci in relayed:
            return None
        relayed[ci] = comm.relay_gather(ci, flight["started"], after)
        relayed[ci]["behind"] = relayed[ci]["token"]
        nxt = comm.start_gather(ci + 1, relayed[ci]["token"]) if ci + 1 < len(comm.chunks) else None
        flight.update(ci=ci + 1, started=nxt)
        if nxt is not None:
            relayed[ci]["behind"] = nxt["token"]
        return relayed[ci]["behind"]

    def need(key, after):
        if key not in W:
            ci = chunk_of[key]
            if ci not in relayed:
                assert ci == flight["ci"], (key, ci)
                advance(after)
            W.update(comm.finish_gather(ci, relayed[ci], relayed[ci]["behind"]))
        return W[key]

    for layer in range(DEPTH):
        if layer == N_A_LAYERS:
            w_kv = need(("w_kv", layer), x)
            kv_vec = _vec8([small["kv_norm_g"], kvmods[0], kvmods[1]], D)
            h_kv, *kv_pieces = proj_rope_fwd(x, kv_vec, w_kv, tabs, qw, False, dils, "kv_fwd")
            k_sh, v_sh = kv_pieces[:len(groups)], kv_pieces[len(groups):]
            kv_saved = (x, h_kv, kv_vec)
        rec = {}
        behind = tabs[0] if layer == 0 else x
        w_in, w_out = need(("ffn1_w_in", layer), behind), need(("ffn1_w_out", layer), behind)
        v1 = vec_of(layer, 0)
        x, rec["ffn1"] = _ffn_forward(x, v1, w_in, w_out)
        if layer < N_A_LAYERS:
            w_in, w_out = need(("conv_w_in", layer), x), need(("conv_w_out", layer), x)
            v2 = vec_of(layer, 1)
            cw = _vec8(list(small["conv_w"][layer]), D)
            x_in = x
            x, h, bcu, cv, z, y = conv_fwd(x, v2, cw, w_in, w_out)
            rec["mix"] = (x_in, h, bcu, cv, z, y, cw)
        else:
            w_q, w_o = need(("attn_w_q", layer), x), need(("attn_w_o", layer), x)
            v2 = vec_of(layer, 1)
            x_in = x
            h, *q = proj_rope_fwd(x, v2, w_q, tabs, qw, True, dils, "q_fwd")
            os_, ls = [], []
            for g, (win, dil) in enumerate(groups):
                o, l = attn_core_fwd(q[g], k_sh[g], v_sh[g], g, win // dil)
                os_.append(o)
                ls.append(l)
            x, mixed, y = attn_mix_out(os_, ls, dils, x, v2, w_o)
            rec["mix"] = (x_in, h, q, os_, ls, mixed, y)
        token = advance(x) if layer >= 1 else None
        w_in, w_out = need(("ffn2_w_in", layer), x), need(("ffn2_w_out", layer), x)
        v3 = after_token(vec_of(layer, 2), token)
        x, rec["ffn2"] = _ffn_forward(x, v3, w_in, w_out)
        rec["vecs"] = (v1, v2, v3)
        saved.append(rec)

    dx, part_final, loss_tile = final_loss(x, _vec8([small["final_norm_g"]], D), target)
    loss = loss_tile[0, 0]

    conv_rows = [None] * N_A_LAYERS
    kv_rows = None
    mod_rows = [[None] * 3 for _ in range(DEPTH)]
    dkv_pairs = [{"k": [], "v": []} for _ in groups]
    slots = {}
    exchanges = []
    token = None

    def send_ready_chunks():
        nonlocal token
        for ci in reversed(range(len(comm.chunks))):
            if ci not in [e[0] for e in exchanges] and all(key in slots for key in comm.chunks[ci]):
                started = comm.start_exchange(ci, slots, token)
                exchanges.append((ci, started))
                token = started["token"]

    vector_gather = {}

    def start_vector_gather(rows0):
        mod_rows[0][0] = rows0
        rows = jnp.stack([jnp.stack(r) for r in mod_rows])
        vecs = jnp.concatenate([rows[:, :, 1:4].reshape(-1), kv_rows[1:3].reshape(-1), kv_rows[0], part_final[0],
                                rows[:, :, 0].reshape(-1), jnp.stack(conv_rows).reshape(-1)])
        vector_gather["count"] = vecs.shape[0]
        vecs = _pad_rows(vecs.reshape(-1, 1), 8 * LANES).reshape(-1, LANES)
        vector_gather["started"] = comm_start([vecs], "gather", None, "vector_grads_start")
        return vector_gather["started"]["token"]

    for layer in reversed(range(DEPTH)):
        rec = saved[layer]
        v1, v2, v3 = rec["vecs"]
        dx, slots[("ffn2_w_in", layer)], slots[("ffn2_w_out", layer)], mod_rows[layer][2] = _ffn_backward(
            dx, rec["ffn2"], after_token(v3, token), W[("ffn2_w_in", layer)], W[("ffn2_w_out", layer)])
        if layer < N_A_LAYERS:
            x_in, h, bcu, cv, z, y, cw = rec["mix"]
            dx, dy, dbcu, part, dcw = conv_bwd(dx, x_in, y, bcu, cv, v2, cw, W[("conv_w_in", layer)], W[("conv_w_out", layer)])
            slots[("conv_w_out", layer)] = grad_slots(z, dy, "conv_dw_out")
            slots[("conv_w_in", layer)] = grad_slots(h, dbcu, "conv_dw_in", col_slots=True)
            conv_rows[layer] = dcw[0:3]
            mod_rows[layer][1] = part[0:4]
        else:
            x_in, h, q, os_, ls, mixed, y = rec["mix"]
            outs = attn_mix_bwd(dx, y, v2, W[("attn_w_o", layer)], os_, ls, dils)
            ng = len(groups)
            dy, dos, deltas, part_gate = outs[0], outs[1:1 + ng], outs[1 + ng:1 + 2 * ng], outs[1 + 2 * ng]
            slots[("attn_w_o", layer)] = grad_slots(mixed, dy, "attn_dw_o", col_slots=True)
            dqs = []
            for g, (win, dil) in enumerate(groups):
                dq, dkc, dkp, dvc, dvp = attn_core_bwd(q[g], k_sh[g], v_sh[g], dos[g], deltas[g], ls[g], g, win // dil)
                dqs.append(dq)
                dkv_pairs[g]["k"].append((dkc, dkp))
                dkv_pairs[g]["v"].append((dvc, dvp))
            dx, dqr, part_norm = proj_rope_bwd(dqs, dils, x_in, dx, v2, W[("attn_w_q", layer)], tabs, qw, True, "q_bwd")
            slots[("attn_w_q", layer)] = grad_slots(dqr, h, "attn_dw_q")
            mod_rows[layer][1] = jnp.concatenate([part_norm[0:3], part_gate[0:1]], axis=0)
        send_ready_chunks()
        dx, slots[("ffn1_w_in", layer)], slots[("ffn1_w_out", layer)], mod_rows[layer][0] = _ffn_backward(
            dx, rec["ffn1"], after_token(v1, token), W[("ffn1_w_in", layer)], W[("ffn1_w_out", layer)],
            on_rows=start_vector_gather if layer == 0 else None)
        if layer == N_A_LAYERS:
            x_kv, h_kv, kv_vec = kv_saved
            dparts = [dkv_combine(dkv_pairs[g]["k"], win // dil, f"dk_combine_g{g}") for g, (win, dil) in enumerate(groups)]
            dparts += [dkv_combine(dkv_pairs[g]["v"], win // dil, f"dv_combine_g{g}") for g, (win, dil) in enumerate(groups)]
            dx, dkvp, part_kv = proj_rope_bwd(dparts, dils, x_kv, dx, kv_vec, W[("w_kv", layer)], tabs, qw, False, "kv_bwd")
            slots[("w_kv", layer)] = grad_slots(h_kv, dkvp, "kv_dw", col_slots=True)
            kv_rows = part_kv[0:3]
        send_ready_chunks()

    return loss, dx, {"exchanges": exchanges, "vector_gather": vector_gather}


def _flat2(a):
    return a.reshape(-1, a.shape[-1])


def _pad_rows(a, mult):
    r = a.shape[0]
    pad = (-r) % mult
    return a if pad == 0 else jnp.concatenate([a, jnp.zeros((pad,) + a.shape[1:], a.dtype)], axis=0)


def kernel(x, c, positions, norm_g, ada_w, ada_b, ffn1_w_in, ffn1_w_out, ffn2_w_in, ffn2_w_out, conv_w_in, conv_w, conv_w_out, kv_norm_g, kv_ada_w, kv_ada_b, w_kv, attn_w_q, attn_w_o, final_norm_g, loss_target, m_norm_g, m_ada_w, m_ada_b, m_ffn1_w_in, m_ffn1_w_out, m_ffn2_w_in, m_ffn2_w_out, m_conv_w_in, m_conv_w, m_conv_w_out, m_kv_norm_g, m_kv_ada_w, m_kv_ada_b, m_w_kv, m_attn_w_q, m_attn_w_o, m_final_norm_g, v_norm_g, v_ada_w, v_ada_b, v_ffn1_w_in, v_ffn1_w_out, v_ffn2_w_in, v_ffn2_w_out, v_conv_w_in, v_conv_w, v_conv_w_out, v_kv_norm_g, v_kv_ada_w, v_kv_ada_b, v_w_kv, v_attn_w_q, v_attn_w_o, v_final_norm_g):
    names = ("norm_g", "ada_w", "ada_b", "ffn1_w_in", "ffn1_w_out", "ffn2_w_in", "ffn2_w_out", "conv_w_in", "conv_w",
             "conv_w_out", "kv_norm_g", "kv_ada_w", "kv_ada_b", "w_kv", "attn_w_q", "attn_w_o", "final_norm_g")
    wts = dict(zip(names, (norm_g, ada_w, ada_b, ffn1_w_in, ffn1_w_out, ffn2_w_in, ffn2_w_out, conv_w_in, conv_w, conv_w_out,
                           kv_norm_g, kv_ada_w, kv_ada_b, w_kv, attn_w_q, attn_w_o, final_norm_g)))
    mom = dict(zip(names, (m_norm_g, m_ada_w, m_ada_b, m_ffn1_w_in, m_ffn1_w_out, m_ffn2_w_in, m_ffn2_w_out, m_conv_w_in,
                           m_conv_w, m_conv_w_out, m_kv_norm_g, m_kv_ada_w, m_kv_ada_b, m_w_kv, m_attn_w_q, m_attn_w_o,
                           m_final_norm_g)))
    var = dict(zip(names, (v_norm_g, v_ada_w, v_ada_b, v_ffn1_w_in, v_ffn1_w_out, v_ffn2_w_in, v_ffn2_w_out, v_conv_w_in,
                           v_conv_w, v_conv_w_out, v_kv_norm_g, v_kv_ada_w, v_kv_ada_b, v_w_kv, v_attn_w_q, v_attn_w_o,
                           v_final_norm_g)))
    T, D = x.shape[1], x.shape[2]
    me = _my_id()
    nmod = ada_w.shape[2]
    nkv = kv_ada_w.shape[1]

    def stacked(w, n):
        w = w if w.ndim == 3 else w[None]
        return jnp.swapaxes(w, 1, 2) if n in _TRANSPOSED else w

    comm = ChunkComm({n: stacked(wts[n], n) for n in _BIG})
    W = {}

    ds = norm_g.shape[2]
    small = jnp.concatenate([c.reshape(-1), norm_g.reshape(-1), conv_w.reshape(-1)]).astype(F32)
    n_small = small.shape[0]
    small = _pad_rows(small.reshape(-1, 1), 8 * LANES).reshape(-1, LANES)
    (small_all,) = all_gather([small], pltpu.VMEM, "gather_small")
    small_all = small_all.reshape(N_DEV, -1)[:, :n_small]
    c_all = small_all[:, :D]
    def full_rows(off, count):
        return jnp.stack([small_all[:, off + i * ds:off + (i + 1) * ds].reshape(D) for i in range(count)])

    W["norm_g"] = full_rows(D, DEPTH * 3).reshape(DEPTH, 3, D)
    W["conv_w"] = full_rows(D + DEPTH * 3 * ds, N_A_LAYERS * 3).reshape(N_A_LAYERS, 3, D)
    W["kv_norm_g"], W["final_norm_g"] = kv_norm_g, final_norm_g

    ada_b_mine = lax.dynamic_slice_in_dim(ada_b, me * nmod, nmod, axis=1).reshape(DEPTH, 1, nmod)
    kv_b_mine = lax.dynamic_slice_in_dim(kv_ada_b, me * nkv, nkv, axis=0).reshape(1, 1, nkv)
    mods_cols = mods_project(c_all, ada_w, ada_b_mine)
    kv_cols = mods_project(c_all, kv_ada_w.reshape(1, D, nkv), kv_b_mine)
    mcat = jnp.concatenate([mods_cols[l] for l in range(DEPTH)] + [kv_cols[0]], axis=1)
    wm = mcat.shape[1]
    if wm % LANES:
        mcat = jnp.concatenate([mcat, jnp.zeros((N_DEV, LANES - wm % LANES), F32)], axis=1)
    (mods_all,) = exchange_slots([mcat.reshape(N_DEV, 1, -1)], "exchange_mods")
    gather0 = comm.start_gather(0, mods_all)
    mods_all = mods_all.reshape(N_DEV, -1)
    mods = jnp.stack([mods_all[:, l * nmod:(l + 1) * nmod].reshape(N_MOD, D) for l in range(DEPTH)])
    kvmods = mods_all[:, DEPTH * nmod:DEPTH * nmod + nkv].reshape(2, D)

    loss_local, dx, grads = device_step(x[0], positions[0], loss_target[0], mods, kvmods, W, comm, gather0)
    loss = lax.psum(loss_local, MESH_AXES)

    (vec_all,) = comm_wait(grads["vector_gather"]["started"], grads["exchanges"][-1][1]["token"], "vector_grads_wait")
    vec_all = vec_all.reshape(N_DEV, -1)[:, :grads["vector_gather"]["count"]]
    nm_, nk_ = DEPTH * N_MOD * D, 2 * D
    dmods_all = vec_all[:, :nm_].reshape(N_DEV, DEPTH, N_MOD * D)
    dkvm_all = vec_all[:, nm_:nm_ + nk_]
    rest = vec_all[:, nm_ + nk_:]
    parts_kv_norm, parts_final = rest[:, :D].reshape(N_DEV, 1, D), rest[:, D:2 * D].reshape(N_DEV, 1, D)
    parts_norm = lax.dynamic_slice_in_dim(rest[:, 2 * D:2 * D + DEPTH * 3 * D].reshape(N_DEV, DEPTH * 3, D), me * ds, ds, axis=2)
    parts_conv = lax.dynamic_slice_in_dim(rest[:, 2 * D + DEPTH * 3 * D:].reshape(N_DEV, N_A_LAYERS * 3, D), me * ds, ds, axis=2)
    dm_cols = lax.dynamic_slice_in_dim(dmods_all, me * nmod, nmod, axis=2)
    dm_mine = jnp.stack([dm_cols[:, l] for l in range(DEPTH)])
    dkv_mine = lax.dynamic_slice_in_dim(dkvm_all, me * nkv, nkv, axis=1).reshape(1, N_DEV, nkv)
    g_ada_w = mods_weight_grad(c_all, dm_mine)
    g_kv_ada_w = mods_weight_grad(c_all, dkv_mine)[0]

    out_g, out_d, out_m, out_v = {}, {}, {}, {}

    def update(n, g, w, parts=False):
        shp = w.shape
        w2 = w.reshape(1, -1) if w.ndim == 1 else _flat2(w)
        g2 = g if parts else g.reshape(w2.shape)
        res = adam_update(g2, w2, mom[n].reshape(w2.shape), var[n].reshape(w2.shape), parts, "adam_" + n)
        out_g[n], out_d[n], out_m[n], out_v[n] = (r.reshape(shp) for r in res)

    moms = {n: stacked(mom[n], n) for n in _BIG}
    vars_ = {n: stacked(var[n], n) for n in _BIG}
    results = {}
    after = dx
    for ci, started in grads["exchanges"]:
        for (n, layer), parts in comm.finish_exchange(ci, started, after).items():
            idx = stacked_index(n, layer)
            results[n] = adam_layer(parts, comm.shards[n], moms[n], vars_[n], results.get(n), 0 if idx is None else idx,
                                    after, f"adam_{n}_{layer}")
            after = results[n][1]
    for n in _BIG:
        res = [jnp.swapaxes(r, 1, 2) if n in _TRANSPOSED else r for r in results[n]]
        out_g[n], out_d[n], out_m[n], out_v[n] = (r.reshape(wts[n].shape) for r in res)
    update("ada_w", g_ada_w, ada_w)
    update("kv_ada_w", g_kv_ada_w, kv_ada_w)
    update("ada_b", dmods_all, ada_b, True)
    update("kv_ada_b", dkvm_all.reshape(N_DEV, 1, nk_), kv_ada_b, True)
    update("kv_norm_g", parts_kv_norm, kv_norm_g, True)
    update("final_norm_g", parts_final, final_norm_g, True)
    update("norm_g", parts_norm, norm_g, True)
    update("conv_w", parts_conv, conv_w, True)

    return (loss, dx.reshape(x.shape), *[out_g[n] for n in names], *[out_d[n] for n in names],
            *[out_m[n] for n in names], *[out_v[n] for n in names])
```

```python
import functools

import jax
import jax.numpy as jnp
from jax import lax
from jax.experimental import pallas as pl
from jax.experimental.pallas import tpu as pltpu

F32, BF16 = jnp.float32, jnp.bfloat16

N_DEV = 8
MESH_AXES = ("x", "y", "c")
DEPTH = 4
N_A_LAYERS = 2
HEAD_DIM = 64
HEADS_PER_GROUP = 8
GROUP_WIDTH = HEAD_DIM * HEADS_PER_GROUP
DILATED_GROUPS = ((128, 1), (512, 4), (2048, 16))
ROPE_DIM = HEAD_DIM // 4
ROPE_THETA = 500000.0
NORM_EPS = 1e-5
FFN_RES_WEIGHT = 0.5
N_MOD = 9
ADAM_LR, ADAM_B1, ADAM_B2, ADAM_EPS, ADAM_WD, ADAM_STEP = 0.001, 0.9, 0.999, 1e-08, 0.01, 10

LANES = 128
TOKEN_TILE = 512
FFN_BWD_TILE = 256
FWD_QUERY_BLOCKS = 8
BWD_QUERY_BLOCKS = 4
CONTRACT_TILE = 4096
GRAD_COLS = 768
MXU_WIDTH = 256
VMEM_LIMIT = 56 * 1024 * 1024
MESH = pl.DeviceIdType.MESH


def _cp(*sem):
    return pltpu.CompilerParams(dimension_semantics=sem, vmem_limit_bytes=VMEM_LIMIT)


def _pick(n, cap, mult=LANES):
    if n <= cap:
        return n
    best = None
    for t in range(mult, cap + 1, mult):
        if n % t == 0:
            best = t
    assert best is not None, (n, cap)
    return best


def _tok(tm, w):
    return pl.BlockSpec((tm, w), lambda i: (i, 0))


def _res(shape):
    nd = len(shape)
    return pl.BlockSpec(shape, lambda *_: (0,) * nd, pipeline_mode=pl.Buffered(1))


def _sds(shape, dt):
    return jax.ShapeDtypeStruct(shape, dt)


def _sigmoid(a):
    return 1.0 / (1.0 + jnp.exp(-a))


def _modnorm(x, g, sh, sc):
    r = lax.rsqrt(jnp.mean(x * x, axis=-1, keepdims=True) + NORM_EPS)
    return (x * r * g) * (1.0 + sc) + sh


def _dot(a, b):
    return jnp.dot(a, b, preferred_element_type=F32)


def _dot_nt(a, b):
    return lax.dot_general(a, b, (((1,), (1,)), ((), ())), preferred_element_type=F32)


def _dot_tn(a, b):
    return lax.dot_general(a, b, (((0,), (0,)), ((), ())), preferred_element_type=F32)


def _rows8(rows, d):
    pad = 8 - len(rows)
    return jnp.concatenate(list(rows) + [jnp.zeros((pad, d), F32)], axis=0)


def _acc_rows(ref, tile, first):
    @pl.when(first)
    def _():
        ref[...] = tile

    @pl.when(jnp.logical_not(first))
    def _():
        ref[...] += tile


def ffn_fwd(x, vec, w_in_t, w_out):
    T, D = x.shape
    F = w_in_t.shape[0] // 2
    tm, cw = min(TOKEN_TILE, T), _pick(F, MXU_WIDTH)

    def body(x_ref, vec_ref, wi_ref, wo_ref, xn_ref, h_ref, ga_ref, gb_ref, u_ref, y_ref):
        x_t = x_ref[...]
        hb = _modnorm(x_t, vec_ref[0:1], vec_ref[1:2], vec_ref[2:3]).astype(BF16)
        h_ref[...] = hb
        for c in range(F // cw):
            lo, hi = c * cw, (c + 1) * cw
            a = _dot_nt(hb, wi_ref[lo:hi, :])
            b = _dot_nt(hb, wi_ref[F + lo:F + hi, :])
            sg = _sigmoid(a)
            silu = a * sg
            ga_ref[:, lo:hi] = (b * (sg + silu * (1.0 - sg))).astype(BF16)
            gb_ref[:, lo:hi] = silu.astype(BF16)
            u_ref[:, lo:hi] = (silu * b).astype(BF16)
        y = _dot(u_ref[...], wo_ref[...])
        y_ref[...] = y.astype(BF16)
        xn_ref[...] = x_t + (FFN_RES_WEIGHT * (1.0 + vec_ref[3:4])) * y

    return pl.pallas_call(
        body, grid=(T // tm,),
        in_specs=[_tok(tm, D), _res((8, D)), _res((2 * F, D)), _res((F, D))],
        out_specs=[_tok(tm, D), _tok(tm, D), _tok(tm, F), _tok(tm, F), _tok(tm, F), _tok(tm, D)],
        out_shape=[_sds((T, D), F32), _sds((T, D), BF16), _sds((T, F), BF16), _sds((T, F), BF16), _sds((T, F), BF16),
                   _sds((T, D), BF16)],
        compiler_params=_cp("arbitrary"), name="ffn_fwd")(x, vec, w_in_t, w_out)


def ffn_bwd(dxo, y, vec, w_out, w_in_t, a, b, x):
    T, D = x.shape
    F = a.shape[1]
    tm, cw = min(FFN_BWD_TILE, T), _pick(F, MXU_WIDTH)

    def body(dxo_ref, y_ref, vec_ref, wo_ref, wi_ref, a_ref, b_ref, x_ref, dy_ref, dab_ref, dx_ref, part_ref):
        dxo_t = dxo_ref[...]
        dyb = (dxo_t * (FFN_RES_WEIGHT * (1.0 + vec_ref[3:4]))).astype(BF16)
        dy_ref[...] = dyb
        dgate = FFN_RES_WEIGHT * jnp.sum(dxo_t * y_ref[...].astype(F32), axis=0, keepdims=True)
        for c in range(F // cw):
            lo, hi = c * cw, (c + 1) * cw
            du = _dot_nt(dyb, wo_ref[lo:hi, :])
            dab_ref[:, lo:hi] = (du * a_ref[:, lo:hi].astype(F32)).astype(BF16)
            dab_ref[:, F + lo:F + hi] = (du * b_ref[:, lo:hi].astype(F32)).astype(BF16)
        dh = _dot(dab_ref[...], wi_ref[...])
        _, vjp = jax.vjp(_modnorm, x_ref[...], vec_ref[0:1], vec_ref[1:2], vec_ref[2:3])
        dx, dg, dsh, dsc = vjp(dh)
        dx_ref[...] = dxo_t + dx
        _acc_rows(part_ref, _rows8([dg, dsh, dsc, dgate], D), pl.program_id(0) == 0)

    return pl.pallas_call(
        body, grid=(T // tm,),
        in_specs=[_tok(tm, D), _tok(tm, D), _res((8, D)), _res((F, D)), _res((2 * F, D)), _tok(tm, F), _tok(tm, F), _tok(tm, D)],
        out_specs=[_tok(tm, D), _tok(tm, 2 * F), _tok(tm, D), pl.BlockSpec((8, D), lambda i: (0, 0))],
        out_shape=[_sds((T, D), BF16), _sds((T, 2 * F), BF16), _sds((T, D), F32), _sds((8, D), F32)],
        compiler_params=_cp("arbitrary"), name="ffn_bwd")(dxo, y, vec, w_out, w_in_t, a, b, x)


def grad_slots(a, b, name, col_slots=False, after=None):
    T, M = a.shape
    extra = [] if after is None else [after]
    N = b.shape[1]
    tk = min(CONTRACT_TILE, T)
    nk = T // tk
    tmm = _pick(M, 1408)
    if col_slots:
        ns = N // N_DEV
        sp = max(s for s in (1, 2, 4, 8) if ns * s <= GRAD_COLS or s == 1)
        tn = ns * sp
    else:
        tn = _pick(N, GRAD_COLS)

    def body(a_ref, b_ref, *rest):
        o_ref, acc = rest[-2:]
        k = pl.program_id(2)
        t = _dot_tn(a_ref[...], b_ref[...])

        @pl.when(k == 0)
        def _():
            acc[...] = t

        @pl.when(k > 0)
        def _():
            acc[...] += t

        @pl.when(k == nk - 1)
        def _():
            if col_slots:
                for s in range(sp):
                    o_ref[s] = acc[:, s * ns:(s + 1) * ns].astype(BF16)
            else:
                o_ref[...] = acc[...].astype(BF16)

    if col_slots:
        out_spec, out_shape = pl.BlockSpec((sp, tmm, ns), lambda i, j, k: (j, i, 0)), _sds((N_DEV, M, ns), BF16)
    else:
        out_spec, out_shape = pl.BlockSpec((tmm, tn), lambda i, j, k: (i, j)), _sds((M, N), BF16)
    out = pl.pallas_call(
        body, grid=(M // tmm, N // tn, nk),
        in_specs=[pl.BlockSpec((tk, tmm), lambda i, j, k: (k, i)), pl.BlockSpec((tk, tn), lambda i, j, k: (k, j))]
        + [pl.BlockSpec(memory_space=pl.ANY)] * len(extra),
        out_specs=out_spec, out_shape=out_shape,
        scratch_shapes=[pltpu.VMEM((tmm, tn), F32)],
        compiler_params=_cp("arbitrary", "arbitrary", "arbitrary"), name=name)(a, b, *extra)
    return out if col_slots else out.reshape(N_DEV, M // N_DEV, N)


def conv_fwd(x, vec, cw, w_in, w_out):
    T, D = x.shape
    tm = min(TOKEN_TILE, T)

    def body(x_ref, vec_ref, cw_ref, wi_ref, wo_ref, xn_ref, h_ref, bcu_ref, cv_ref, z_ref, y_ref, vbuf):
        @pl.when(pl.program_id(0) == 0)
        def _():
            vbuf[0:8, :] = jnp.zeros((8, D), F32)

        x_t = x_ref[...]
        hb = _modnorm(x_t, vec_ref[0:1], vec_ref[1:2], vec_ref[2:3]).astype(BF16)
        h_ref[...] = hb
        bcu = _dot(hb, wi_ref[...])
        bcu_ref[...] = bcu.astype(BF16)
        bg, v = bcu[:, 0:D], bcu[:, D:2 * D] * bcu[:, 2 * D:3 * D]
        vbuf[8:8 + tm, :] = v
        conv = cw_ref[0:1] * vbuf[6:6 + tm, :] + cw_ref[1:2] * vbuf[7:7 + tm, :] + cw_ref[2:3] * v
        cv_ref[...] = conv.astype(BF16)
        zb = (bg * conv).astype(BF16)
        z_ref[...] = zb
        y = _dot(zb, wo_ref[...])
        y_ref[...] = y.astype(BF16)
        xn_ref[...] = x_t + (1.0 + vec_ref[3:4]) * y
        vbuf[0:8, :] = vbuf[tm:tm + 8, :]

    return pl.pallas_call(
        body, grid=(T // tm,),
        in_specs=[_tok(tm, D), _res((8, D)), _res((8, D)), _res((D, 3 * D)), _res((D, D))],
        out_specs=[_tok(tm, D), _tok(tm, D), _tok(tm, 3 * D), _tok(tm, D), _tok(tm, D), _tok(tm, D)],
        out_shape=[_sds((T, D), F32), _sds((T, D), BF16), _sds((T, 3 * D), BF16), _sds((T, D), BF16),
                   _sds((T, D), BF16), _sds((T, D), BF16)],
        scratch_shapes=[pltpu.VMEM((tm + 8, D), F32)],
        compiler_params=_cp("arbitrary"), name="conv_fwd")(x, vec, cw, w_in, w_out)


def conv_bwd(dxo, x, y, bcu, cv, vec, cw, w_in, w_out):
    T, D = x.shape
    tm = min(TOKEN_TILE, T)
    nt = T // tm

    def body(dxo_ref, x_ref, y_ref, bcu_ref, cv_ref, vec_ref, cw_ref, wi_ref, wo_ref,
             dx_ref, dy_ref, dbcu_ref, part_ref, dcw_ref, dcbuf):
        first = pl.program_id(0) == 0

        @pl.when(first)
        def _():
            dcbuf[tm:tm + 8, :] = jnp.zeros((8, D), F32)

        dxo_t = dxo_ref[...]
        dyb = (dxo_t * (1.0 + vec_ref[3:4])).astype(BF16)
        dy_ref[...] = dyb
        dgate = jnp.sum(dxo_t * y_ref[...].astype(F32), axis=0, keepdims=True)
        dz = _dot_nt(dyb, wo_ref[...])
        bcu_t = bcu_ref[...].astype(F32)
        bg, cg, ug = bcu_t[:, 0:D], bcu_t[:, D:2 * D], bcu_t[:, 2 * D:3 * D]
        dconv = dz * bg
        dbg = dz * cv_ref[...].astype(F32)
        dcbuf[0:tm, :] = dconv
        d1, d2 = dcbuf[1:tm + 1, :], dcbuf[2:tm + 2, :]
        dv = cw_ref[2:3] * dconv + cw_ref[1:2] * d1 + cw_ref[0:1] * d2
        v = cg * ug
        dcw = _rows8([jnp.sum(d2 * v, axis=0, keepdims=True), jnp.sum(d1 * v, axis=0, keepdims=True),
                      jnp.sum(dconv * v, axis=0, keepdims=True)], D)
        dbcu = jnp.concatenate([dbg, dv * ug, dv * cg], axis=1).astype(BF16)
        dbcu_ref[...] = dbcu
        dh = _dot_nt(dbcu, wi_ref[...])
        _, vjp = jax.vjp(_modnorm, x_ref[...], vec_ref[0:1], vec_ref[1:2], vec_ref[2:3])
        dx, dg, dsh, dsc = vjp(dh)
        dx_ref[...] = dxo_t + dx
        _acc_rows(part_ref, _rows8([dg, dsh, dsc, dgate], D), first)
        _acc_rows(dcw_ref, dcw, first)
        dcbuf[tm:tm + 8, :] = dcbuf[0:8, :]

    def rev(w):
        return pl.BlockSpec((tm, w), lambda i: (nt - 1 - i, 0))

    return pl.pallas_call(
        body, grid=(nt,),
        in_specs=[rev(D), rev(D), rev(D), rev(3 * D), rev(D), _res((8, D)), _res((8, D)), _res((D, 3 * D)), _res((D, D))],
        out_specs=[rev(D), rev(D), rev(3 * D), pl.BlockSpec((8, D), lambda i: (0, 0)), pl.BlockSpec((8, D), lambda i: (0, 0))],
        out_shape=[_sds((T, D), F32), _sds((T, D), BF16), _sds((T, 3 * D), BF16), _sds((8, D), F32), _sds((8, D), F32)],
        scratch_shapes=[pltpu.VMEM((tm + 8, D), F32)],
        compiler_params=_cp("arbitrary"), name="conv_bwd")(dxo, x, y, bcu, cv, vec, cw, w_in, w_out)


def rope_tables(pos, lane_rows):
    T = pos.shape[0]
    tm = min(TOKEN_TILE, T)

    def body(p_ref, lr_ref, c_ref, sp_ref, sm_ref):
        ang = p_ref[...].astype(F32) * lr_ref[0:1]
        cs, sn = jnp.cos(ang), jnp.sin(ang)
        c_ref[...] = jnp.where(lr_ref[1:2] > 0.5, cs, 1.0)
        sp_ref[...] = jnp.where(lr_ref[2:3] > 0.5, sn, 0.0)
        sm_ref[...] = jnp.where(lr_ref[3:4] > 0.5, -sn, 0.0)

    return pl.pallas_call(
        body, grid=(T // tm,),
        in_specs=[_tok(tm, 1), _res((8, LANES))],
        out_specs=[_tok(tm, LANES)] * 3,
        out_shape=[_sds((T, LANES), F32)] * 3,
        compiler_params=_cp("arbitrary"), name="rope_tables")(pos, lane_rows)


def _rope(t, c, sp, sm):
    w = t.shape[1]
    reps = w // LANES
    cf, spf, smf = jnp.tile(c, (1, reps)), jnp.tile(sp, (1, reps)), jnp.tile(sm, (1, reps))
    half = ROPE_DIM // 2
    return t * cf + pltpu.roll(t, half, axis=1) * spf + pltpu.roll(t, w - half, axis=1) * smf


def _rope_t(d, c, sp, sm):
    w = d.shape[1]
    reps = w // LANES
    cf, spf, smf = jnp.tile(c, (1, reps)), jnp.tile(sp, (1, reps)), jnp.tile(sm, (1, reps))
    half = ROPE_DIM // 2
    return d * cf + pltpu.roll(d * spf, w - half, axis=1) + pltpu.roll(d * smf, half, axis=1)


def _split_residues(v, d, stage):
    tm, width = v.shape
    if d == 1:
        return [v]
    nj = width // LANES
    for j in range(nj):
        stage[j] = v[:, j * LANES:(j + 1) * LANES]
    return [jnp.concatenate([stage[j, pl.ds(r, tm // d, stride=d), :] for j in range(nj)], axis=1) for r in range(d)]


def _merge_residues(piece, d, tm, width, stage):
    if d == 1:
        return piece(0)
    nj = width // LANES
    for r in range(d):
        p = piece(r)
        for j in range(nj):
            stage[j, pl.ds(r, tm // d, stride=d), :] = p[:, j * LANES:(j + 1) * LANES]
    return jnp.concatenate([stage[j] for j in range(nj)], axis=1)


def _residue_spec(d, tm, width=GROUP_WIDTH):
    return pl.BlockSpec((d, tm // d, width), lambda i: (0, i, 0))


def _stage_scratch(tm):
    return pltpu.VMEM((GROUP_WIDTH // LANES, tm, LANES), F32)


def proj_rope_fwd(x, vec, w, tabs, n_rope, transposed, dils, name):
    T, D = x.shape
    N = w.shape[0] if transposed else w.shape[1]
    tm = min(TOKEN_TILE, T)
    GW = GROUP_WIDTH
    piece_dils = [dils[j % len(dils)] for j in range(N // GW)]

    def body(x_ref, vec_ref, w_ref, c_ref, sp_ref, sm_ref, h_ref, *rest):
        out_refs, stage = rest[:-1], rest[-1]
        hb = _modnorm(x_ref[...], vec_ref[0:1], vec_ref[1:2], vec_ref[2:3]).astype(BF16)
        h_ref[...] = hb
        p = _dot_nt(hb, w_ref[...]) if transposed else _dot(hb, w_ref[...])
        pr = _rope(p[:, 0:n_rope], c_ref[...], sp_ref[...], sm_ref[...])
        for j, d in enumerate(piece_dils):
            src = pr if (j + 1) * GW <= n_rope else p
            for r, rows in enumerate(_split_residues(src[:, j * GW:(j + 1) * GW], d, stage)):
                out_refs[j][r] = rows.astype(BF16)

    return pl.pallas_call(
        body, grid=(T // tm,),
        in_specs=[_tok(tm, D), _res((8, D)), _res(w.shape)] + [_tok(tm, LANES)] * 3,
        out_specs=[_tok(tm, D)] + [_residue_spec(d, tm) for d in piece_dils],
        out_shape=[_sds((T, D), BF16)] + [_sds((d, T // d, GW), BF16) for d in piece_dils],
        scratch_shapes=[_stage_scratch(tm)],
        compiler_params=_cp("arbitrary"), name=name)(x, vec, w, *tabs)


def proj_rope_bwd(dparts, dils, x, dxo, vec, w, tabs, n_rope, transposed, name):
    T, D = x.shape
    N = w.shape[0] if transposed else w.shape[1]
    tm = min(TOKEN_TILE, T)
    GW = GROUP_WIDTH
    npart = len(dparts)
    piece_dils = [dils[j % len(dils)] for j in range(npart)]

    def body(*refs):
        d_refs = refs[:npart]
        x_ref, dxo_ref, vec_ref, w_ref, c_ref, sp_ref, sm_ref, dx_ref, dp_ref, part_ref, stage = refs[npart:]
        d = jnp.concatenate([_merge_residues(lambda r, ref=ref: ref[r].astype(F32), dd, tm, GW, stage)
                             for ref, dd in zip(d_refs, piece_dils)], axis=1)
        dr = _rope_t(d[:, 0:n_rope], c_ref[...], sp_ref[...], sm_ref[...])
        if n_rope < N:
            dr = jnp.concatenate([dr, d[:, n_rope:N]], axis=1)
        dpb = dr.astype(BF16)
        dp_ref[...] = dpb
        dh = _dot(dpb, w_ref[...]) if transposed else _dot_nt(dpb, w_ref[...])
        _, vjp = jax.vjp(_modnorm, x_ref[...], vec_ref[0:1], vec_ref[1:2], vec_ref[2:3])
        dx, dg, dsh, dsc = vjp(dh)
        dx_ref[...] = dxo_ref[...] + dx
        _acc_rows(part_ref, _rows8([dg, dsh, dsc], D), pl.program_id(0) == 0)

    return pl.pallas_call(
        body, grid=(T // tm,),
        in_specs=[_residue_spec(d, tm) for d in piece_dils] + [_tok(tm, D), _tok(tm, D), _res((8, D)), _res(w.shape)]
        + [_tok(tm, LANES)] * 3,
        out_specs=[_tok(tm, D), _tok(tm, N), pl.BlockSpec((8, D), lambda i: (0, 0))],
        out_shape=[_sds((T, D), F32), _sds((T, N), BF16), _sds((8, D), F32)],
        scratch_shapes=[_stage_scratch(tm)],
        compiler_params=_cp("arbitrary"), name=name)(*dparts, x, dxo, vec, w, *tabs)


def _valid_mask(n, i):
    qi = lax.broadcasted_iota(jnp.int32, (n, 2 * n), 0)
    kj = lax.broadcasted_iota(jnp.int32, (n, 2 * n), 1)
    dist = n + qi - kj
    return (dist >= 0) & (dist <= n) & ((kj >= n) | (i > 0))


STAT_STRIDE = LANES // HEADS_PER_GROUP


def _head_of_lane():
    return lax.broadcasted_iota(jnp.int32, (1, LANES), 1) // STAT_STRIDE


def attn_core_fwd(q, k, v, g, n):
    d, M, GW = q.shape
    scale = HEAD_DIM ** -0.5

    qb = min(FWD_QUERY_BLOCKS, M // n)

    def body(q_ref, kp_ref, kc_ref, vp_ref, vc_ref, o_ref, l_ref):
        masks = [_valid_mask(n, pl.program_id(1))] + [_valid_mask(n, 1)] * (qb - 1)
        first = lax.broadcasted_iota(jnp.int32, (1, LANES), 1) < HEAD_DIM
        head_of_lane = _head_of_lane()
        lse = [jnp.zeros((n, LANES), F32) for _ in range(qb)]
        for pair in range(HEADS_PER_GROUP * HEAD_DIM // LANES):
            ps = slice(LANES * pair, LANES * (pair + 1))
            kall = jnp.concatenate([kp_ref[:, ps], kc_ref[:, ps]], axis=0)
            vall = jnp.concatenate([vp_ref[:, ps], vc_ref[:, ps]], axis=0)
            for blk in range(qb):
                rows = slice(blk * n, (blk + 1) * n)
                keys, vals = kall[blk * n:(blk + 2) * n], vall[blk * n:(blk + 2) * n]
                q2 = q_ref[rows, ps]
                o2, l2 = [], []
                for sel in (first, jnp.logical_not(first)):
                    s = jnp.where(masks[blk], _dot_nt(jnp.where(sel, q2, jnp.zeros_like(q2)), keys) * scale, -1e30)
                    m = jnp.max(s, axis=1, keepdims=True)
                    p = jnp.exp(s - m)
                    den = jnp.sum(p, axis=1, keepdims=True)
                    o2.append(_dot((p / den).astype(BF16), vals))
                    l2.append(m + jnp.log(den))
                o_ref[rows, ps] = jnp.where(first, o2[0], o2[1]).astype(BF16)
                for half in range(2):
                    lse[blk] = jnp.where(head_of_lane == 2 * pair + half, l2[half], lse[blk])
        for blk in range(qb):
            l_ref[blk * n:(blk + 1) * n, :] = lse[blk]

    two = pl.BlockSpec((None, qb * n, GW), lambda r, i: (r, i, 0))
    prv = pl.BlockSpec((None, n, GW), lambda r, i: (r, jnp.maximum(qb * i - 1, 0), 0))
    stat = pl.BlockSpec((None, qb * n, LANES), lambda r, i: (r, i, 0))
    return pl.pallas_call(
        body, grid=(d, M // (qb * n)),
        in_specs=[two, prv, two, prv, two], out_specs=[two, stat],
        out_shape=[_sds((d, M, GW), BF16), _sds((d, M, LANES), F32)],
        compiler_params=_cp("arbitrary", "arbitrary"), name=f"attn_fwd_g{g}")(q, k, k, v, v)


def attn_core_bwd(q, k, v, do, delta, lse, g, n):
    d, M, GW = q.shape
    scale = HEAD_DIM ** -0.5
    qb = min(BWD_QUERY_BLOCKS, M // n)

    def body(q_ref, kp_ref, kc_ref, vp_ref, vc_ref, do_ref, d_ref, l_ref, dq_ref, dkc_ref, dkp_ref, dvc_ref, dvp_ref):
        masks = [_valid_mask(n, pl.program_id(1))] + [_valid_mask(n, 1)] * (qb - 1)
        first = lax.broadcasted_iota(jnp.int32, (1, LANES), 1) < HEAD_DIM
        for pair in range(HEADS_PER_GROUP * HEAD_DIM // LANES):
            ps = slice(LANES * pair, LANES * (pair + 1))
            kall = jnp.concatenate([kp_ref[:, ps], kc_ref[:, ps]], axis=0)
            vall = jnp.concatenate([vp_ref[:, ps], vc_ref[:, ps]], axis=0)
            own = []
            for blk in range(qb):
                rows = slice(blk * n, (blk + 1) * n)
                keys, vals = kall[blk * n:(blk + 2) * n], vall[blk * n:(blk + 2) * n]
                q2, do2 = q_ref[rows, ps], do_ref[rows, ps]
                dq2, dk, dv = [], None, None
                for half, sel in enumerate((first, jnp.logical_not(first))):
                    qm = jnp.where(sel, q2, jnp.zeros_like(q2))
                    dom = jnp.where(sel, do2, jnp.zeros_like(do2))
                    s = jnp.where(masks[blk], _dot_nt(qm, keys) * scale, -1e30)
                    lane0 = STAT_STRIDE * (2 * pair + half)
                    p = jnp.exp(s - l_ref[rows, lane0:lane0 + 1])
                    dp = _dot_nt(dom, vals)
                    ds = (p * (dp - d_ref[rows, lane0:lane0 + 1]) * scale).astype(BF16)
                    dq2.append(_dot(ds, keys))
                    dkh = _dot_tn(ds, qm)
                    dvh = _dot_tn(p.astype(BF16), dom)
                    dk = dkh if dk is None else dk + dkh
                    dv = dvh if dv is None else dv + dvh
                dq_ref[rows, ps] = jnp.where(first, dq2[0], dq2[1]).astype(BF16)
                own.append((dk, dv))
            for t, (c_ref, p_ref) in enumerate(((dkc_ref, dkp_ref), (dvc_ref, dvp_ref))):
                p_ref[:, ps] = own[0][t][0:n].astype(BF16)
                for blk in range(qb - 1):
                    c_ref[blk * n:(blk + 1) * n, ps] = (own[blk][t][n:2 * n] + own[blk + 1][t][0:n]).astype(BF16)
                c_ref[(qb - 1) * n:qb * n, ps] = own[qb - 1][t][n:2 * n].astype(BF16)

    run = pl.BlockSpec((None, qb * n, GW), lambda r, i: (r, i, 0))
    prv = pl.BlockSpec((None, n, GW), lambda r, i: (r, jnp.maximum(qb * i - 1, 0), 0))
    one = pl.BlockSpec((None, n, GW), lambda r, i: (r, i, 0))
    stat = pl.BlockSpec((None, qb * n, LANES), lambda r, i: (r, i, 0))
    return pl.pallas_call(
        body, grid=(d, M // (qb * n)),
        in_specs=[run, prv, run, prv, run, run, stat, stat], out_specs=[run, run, one, run, one],
        out_shape=[_sds((d, M, GW), BF16), _sds((d, M, GW), BF16), _sds((d, M // qb, GW), BF16),
                   _sds((d, M, GW), BF16), _sds((d, M // qb, GW), BF16)],
        compiler_params=_cp("arbitrary", "arbitrary"), name=f"attn_bwd_g{g}")(q, k, k, v, v, do, delta, lse)


def dkv_combine(cur_prev, n, name):
    d, M, GW = cur_prev[0][0].shape
    qb = M // cur_prev[0][1].shape[1]
    rows = min(M, 1024)
    pairs = rows // (qb * n)
    steps = M // rows

    def body(*refs):
        o_ref = refs[-1]
        last = pl.program_id(1) == steps - 1
        acc = None
        shifted = None
        for t in range(0, len(refs) - 1, 3):
            c = refs[t][...].astype(F32)
            nxt = jnp.where(last, 0.0, refs[t + 2][...].astype(F32))
            s = nxt if pairs == 1 else jnp.concatenate([refs[t + 1][n:pairs * n, :].astype(F32), nxt], axis=0)
            acc = c if acc is None else acc + c
            shifted = s if shifted is None else shifted + s
        for m in range(pairs):
            lo, hi = qb * m * n, (qb * m + qb - 1) * n
            o_ref[lo:hi, :] = acc[lo:hi].astype(BF16)
            o_ref[hi:hi + n, :] = (acc[hi:hi + n] + shifted[m * n:(m + 1) * n]).astype(BF16)

    cur = pl.BlockSpec((None, rows, GW), lambda r, i: (r, i, 0))
    same = pl.BlockSpec((None, pairs * n, GW), lambda r, i: (r, i, 0))
    nxt = pl.BlockSpec((None, n, GW), lambda r, i: (r, jnp.minimum((i + 1) * pairs, M // (qb * n) - 1), 0))
    args = []
    for c, p in cur_prev:
        args += [c, p, p]
    return pl.pallas_call(
        body, grid=(d, steps), in_specs=[cur, same, nxt] * len(cur_prev), out_specs=cur,
        out_shape=_sds((d, M, GW), BF16),
        compiler_params=_cp("arbitrary", "arbitrary"), name=name)(*args)


def _group_weights(ls):
    mx = functools.reduce(jnp.maximum, ls)
    es = [jnp.exp(l - mx) for l in ls]
    tot = functools.reduce(lambda a, b: a + b, es)
    return [e / tot for e in es]


def _expand_heads(w):
    tm = w.shape[0]
    first = lax.broadcasted_iota(jnp.int32, (1, LANES), 1) < HEAD_DIM
    cols = [jnp.broadcast_to(w[:, STAT_STRIDE * h:STAT_STRIDE * h + 1], (tm, LANES)) for h in range(HEADS_PER_GROUP)]
    return jnp.concatenate([jnp.where(first, cols[2 * p], cols[2 * p + 1]) for p in range(HEADS_PER_GROUP // 2)], axis=1)


def _head_sums(r):
    width = r.shape[1]
    feat_head = lax.broadcasted_iota(jnp.int32, (width, LANES), 0) // HEAD_DIM
    stat_head = lax.broadcasted_iota(jnp.int32, (width, LANES), 1) // STAT_STRIDE
    ones = jnp.where(feat_head == stat_head, 1.0, 0.0).astype(BF16)
    hi = r.astype(BF16)
    lo = (r - hi.astype(F32)).astype(BF16)
    return _dot(hi, ones) + _dot(lo, ones)


def _mix_weights(l_refs, dils, tm, stage):
    ls = [_merge_residues(lambda r, ref=ref: ref[r], d, tm, LANES, stage) for ref, d in zip(l_refs, dils)]
    return [_expand_heads(w) for w in _group_weights(ls)]


def attn_mix_out(os_, ls, dils, x, vec, w_o):
    T, D = x.shape
    GW = GROUP_WIDTH
    tm = min(TOKEN_TILE, T)
    ng = len(os_)

    def body(*refs):
        o_refs, l_refs = refs[:ng], refs[ng:2 * ng]
        x_ref, vec_ref, w_ref, xn_ref, mix_ref, y_ref, stage = refs[2 * ng:]
        natural = lambda ref, d: _merge_residues(lambda r: ref[r].astype(F32), d, tm, GW, stage)
        ws = _mix_weights(l_refs, dils, tm, stage)
        mixed = functools.reduce(lambda a, b: a + b, [w * natural(r, d) for w, r, d in zip(ws, o_refs, dils)])
        mb = mixed.astype(BF16)
        mix_ref[...] = mb
        y = _dot(mb, w_ref[...])
        y_ref[...] = y.astype(BF16)
        xn_ref[...] = x_ref[...] + (1.0 + vec_ref[3:4]) * y

    res = [_residue_spec(d, tm) for d in dils]
    stat = [_residue_spec(d, tm, LANES) for d in dils]
    return pl.pallas_call(
        body, grid=(T // tm,),
        in_specs=res + stat + [_tok(tm, D), _res((8, D)), _res((GW, D))],
        out_specs=[_tok(tm, D), _tok(tm, GW), _tok(tm, D)],
        out_shape=[_sds((T, D), F32), _sds((T, GW), BF16), _sds((T, D), BF16)],
        scratch_shapes=[_stage_scratch(tm)],
        compiler_params=_cp("arbitrary"), name="attn_mix_out")(*os_, *ls, x, vec, w_o)


def attn_mix_bwd(dxo, y, vec, w_o, os_, ls, dils):
    T, D = dxo.shape
    GW = GROUP_WIDTH
    tm = min(TOKEN_TILE, T)
    ng = len(os_)

    def body(*refs):
        dxo_ref, y_ref, vec_ref, w_ref = refs[:4]
        o_refs, l_refs = refs[4:4 + ng], refs[4 + ng:4 + 2 * ng]
        dy_ref = refs[4 + 2 * ng]
        do_refs = refs[5 + 2 * ng:5 + 3 * ng]
        d_refs = refs[5 + 3 * ng:5 + 4 * ng]
        part_ref, stage = refs[5 + 4 * ng], refs[6 + 4 * ng]
        natural = lambda ref, d: _merge_residues(lambda r: ref[r].astype(F32), d, tm, GW, stage)
        dxo_t = dxo_ref[...]
        dyb = (dxo_t * (1.0 + vec_ref[3:4])).astype(BF16)
        dy_ref[...] = dyb
        dgate = jnp.sum(dxo_t * y_ref[...].astype(F32), axis=0, keepdims=True)
        _acc_rows(part_ref, _rows8([dgate], D), pl.program_id(0) == 0)
        dmix = _dot_nt(dyb, w_ref[...])
        ws = _mix_weights(l_refs, dils, tm, stage)
        mixed = functools.reduce(lambda a, b: a + b, [w * natural(r, d) for w, r, d in zip(ws, o_refs, dils)])
        for gi in range(ng):
            do = ws[gi] * dmix
            for r, rows in enumerate(_split_residues(do, dils[gi], stage)):
                do_refs[gi][r] = rows.astype(BF16)
            for r, rows in enumerate(_split_residues(_head_sums(do * mixed), dils[gi], stage)):
                d_refs[gi][r] = rows

    res = [_residue_spec(d, tm) for d in dils]
    stat = [_residue_spec(d, tm, LANES) for d in dils]
    return pl.pallas_call(
        body, grid=(T // tm,),
        in_specs=[_tok(tm, D), _tok(tm, D), _res((8, D)), _res((GW, D))] + res + stat,
        out_specs=[_tok(tm, D)] + res + stat + [pl.BlockSpec((8, D), lambda i: (0, 0))],
        out_shape=[_sds((T, D), BF16)] + [_sds((d, T // d, GW), BF16) for d in dils]
        + [_sds((d, T // d, LANES), F32) for d in dils] + [_sds((8, D), F32)],
        scratch_shapes=[_stage_scratch(tm)],
        compiler_params=_cp("arbitrary"), name="attn_mix_bwd")(dxo, y, vec, w_o, *os_, *ls)


def final_loss(x, gvec, target):
    T, D = x.shape
    tm = min(TOKEN_TILE, T)

    def norm(xv, g):
        return xv * lax.rsqrt(jnp.mean(xv * xv, axis=-1, keepdims=True) + NORM_EPS) * g

    def body(x_ref, g_ref, t_ref, dx_ref, part_ref, loss_ref):
        first = pl.program_id(0) == 0
        yv, vjp = jax.vjp(norm, x_ref[...], g_ref[0:1])
        err = yv - t_ref[...]
        dx, dg = vjp(err * (1.0 / D))
        dx_ref[...] = dx
        _acc_rows(part_ref, _rows8([dg], D), first)
        tile_loss = 0.5 * jnp.sum(jnp.sum(err * err, axis=1, keepdims=True) * (1.0 / D), axis=0, keepdims=True)
        _acc_rows(loss_ref, jnp.broadcast_to(tile_loss, (8, LANES)), first)

    return pl.pallas_call(
        body, grid=(T // tm,),
        in_specs=[_tok(tm, D), _res((8, D)), _tok(tm, D)],
        out_specs=[_tok(tm, D), pl.BlockSpec((8, D), lambda i: (0, 0)), pl.BlockSpec((8, LANES), lambda i: (0, 0))],
        out_shape=[_sds((T, D), F32), _sds((8, D), F32), _sds((8, LANES), F32)],
        compiler_params=_cp("arbitrary"), name="final_loss")(x, gvec, target)


def mods_project(c_all, w, b):
    B, D = c_all.shape
    L, _, N = w.shape

    def body(c_ref, w_ref, b_ref, o_ref):
        cv = c_ref[...]
        cond = cv * _sigmoid(cv)
        o_ref[0] = jnp.dot(cond, w_ref[0], preferred_element_type=F32, precision=lax.Precision.HIGHEST) + b_ref[0]

    return pl.pallas_call(
        body, grid=(L,),
        in_specs=[pl.BlockSpec((B, D), lambda l: (0, 0)), pl.BlockSpec((1, D, N), lambda l: (l, 0, 0)),
                  pl.BlockSpec((1, 1, N), lambda l: (l, 0, 0))],
        out_specs=pl.BlockSpec((1, B, N), lambda l: (l, 0, 0)),
        out_shape=_sds((L, B, N), F32),
        compiler_params=_cp("arbitrary"), name="mods_project")(c_all, w, b)


def mods_weight_grad(c_all, dm):
    B, D = c_all.shape
    L, _, N = dm.shape

    def body(c_ref, d_ref, o_ref):
        cv = c_ref[...]
        cond = cv * _sigmoid(cv)
        o_ref[0] = lax.dot_general(cond, d_ref[0], (((0,), (0,)), ((), ())), preferred_element_type=F32,
                                   precision=lax.Precision.HIGHEST)

    return pl.pallas_call(
        body, grid=(L,),
        in_specs=[pl.BlockSpec((B, D), lambda l: (0, 0)), pl.BlockSpec((1, B, N), lambda l: (l, 0, 0))],
        out_specs=pl.BlockSpec((1, D, N), lambda l: (l, 0, 0)),
        out_shape=_sds((L, D, N), F32),
        compiler_params=_cp("arbitrary"), name="mods_weight_grad")(c_all, dm)


def _adam_math(g, w, m, v):
    m2 = ADAM_B1 * m + (1.0 - ADAM_B1) * g
    v2 = ADAM_B2 * v + (1.0 - ADAM_B2) * (g * g)
    m_hat = m2 / (1.0 - ADAM_B1 ** ADAM_STEP)
    v_hat = v2 / (1.0 - ADAM_B2 ** ADAM_STEP)
    delta = -ADAM_LR * (m_hat / (jnp.sqrt(v_hat) + ADAM_EPS) + ADAM_WD * w)
    return delta, m2, v2


def adam_update(g, w, m, v, parts, name):
    R, C = w.shape
    tr = _pick(R, 256, 8)

    def body(g_ref, w_ref, m_ref, v_ref, go_ref, d_ref, mo_ref, vo_ref):
        if parts:
            gv = g_ref[0].astype(F32)
            for s in range(1, N_DEV):
                gv = gv + g_ref[s].astype(F32)
        else:
            gv = g_ref[...]
        go_ref[...] = gv
        d_ref[...], mo_ref[...], vo_ref[...] = _adam_math(gv, w_ref[...], m_ref[...], v_ref[...])

    gspec = pl.BlockSpec((N_DEV, tr, C), lambda i: (0, i, 0)) if parts else _tok(tr, C)
    return pl.pallas_call(
        body, grid=(R // tr,),
        in_specs=[gspec, _tok(tr, C), _tok(tr, C), _tok(tr, C)],
        out_specs=[_tok(tr, C)] * 4, out_shape=[_sds((R, C), F32)] * 4,
        compiler_params=_cp("arbitrary"), name=name)(g, w, m, v)


def adam_layer(parts, w, m, v, prev, layer, after, name):
    L, R, C = w.shape
    tr = _pick(R, 256, 8)
    prev = (list(prev) if prev is not None else []) + [after]

    def body(p_ref, w_ref, m_ref, v_ref, *rest):
        go_ref, d_ref, mo_ref, vo_ref = rest[-4:]
        gv = p_ref[0].astype(F32)
        for s in range(1, N_DEV):
            gv = gv + p_ref[s].astype(F32)
        go_ref[...] = gv
        d_ref[...], mo_ref[...], vo_ref[...] = _adam_math(gv, w_ref[...], m_ref[...], v_ref[...])

    lay = pl.BlockSpec((None, tr, C), lambda i: (layer, i, 0))
    return pl.pallas_call(
        body, grid=(R // tr,),
        in_specs=[pl.BlockSpec((N_DEV, tr, C), lambda i: (0, i, 0)), lay, lay, lay] + [pl.BlockSpec(memory_space=pl.ANY)] * len(prev),
        out_specs=[lay] * 4, out_shape=[_sds((L, R, C), F32)] * 4,
        input_output_aliases={4 + k: k for k in range(len(prev) - 1)},
        compiler_params=_cp("arbitrary"), name=name)(parts, w, m, v, *prev)


def _my_id():
    return 4 * lax.axis_index("x") + 2 * lax.axis_index("y") + lax.axis_index("c")


def _peer(s):
    x, y, c = lax.axis_index("x"), lax.axis_index("y"), lax.axis_index("c")
    px = (1 - x) if s & 4 else x
    py = (1 - y) if s & 2 else y
    pc = (1 - c) if s & 1 else c
    return (px, py, pc), 4 * px + 2 * py + pc


def all_gather(xs, space, name):
    na = len(xs)

    def body(*refs):
        x_refs, o_refs = refs[:na], refs[na:2 * na]
        send_sems, recv_sems, local_sems = refs[2 * na:]
        me = _my_id()
        locals_, sends = [], []
        for a in range(na):
            cp = pltpu.make_async_copy(x_refs[a], o_refs[a].at[me], local_sems.at[a])
            cp.start()
            locals_.append(cp)
        for s in range(1, N_DEV):
            peer, _ = _peer(s)
            for a in range(na):
                cp = pltpu.make_async_remote_copy(
                    src_ref=x_refs[a], dst_ref=o_refs[a].at[me], send_sem=send_sems.at[a, s - 1],
                    recv_sem=recv_sems.at[a, s - 1], device_id=peer, device_id_type=MESH)
                cp.start()
                sends.append(cp)
        for s in range(1, N_DEV):
            peer, pid = _peer(s)
            for a in range(na):
                pltpu.make_async_remote_copy(
                    src_ref=x_refs[a], dst_ref=o_refs[a].at[pid], send_sem=send_sems.at[a, s - 1],
                    recv_sem=recv_sems.at[a, s - 1], device_id=peer, device_id_type=MESH).wait_recv()
        for cp in sends:
            cp.wait_send()
        for cp in locals_:
            cp.wait()

    spec = pl.BlockSpec(memory_space=space)
    return pl.pallas_call(
        body, in_specs=[spec] * na, out_specs=[spec] * na,
        out_shape=[_sds((N_DEV,) + x.shape, x.dtype) for x in xs],
        scratch_shapes=[pltpu.SemaphoreType.DMA((na, N_DEV - 1)), pltpu.SemaphoreType.DMA((na, N_DEV - 1)),
                        pltpu.SemaphoreType.DMA((na,))],
        compiler_params=pltpu.CompilerParams(vmem_limit_bytes=VMEM_LIMIT), name=name)(*xs)


def exchange_slots(xs, name):
    na = len(xs)

    def body(*refs):
        x_refs, o_refs = refs[:na], refs[na:2 * na]
        send_sems, recv_sems, local_sems = refs[2 * na:]
        me = _my_id()
        locals_, sends = [], []
        for a in range(na):
            cp = pltpu.make_async_copy(x_refs[a].at[me], o_refs[a].at[me], local_sems.at[a])
            cp.start()
            locals_.append(cp)
        for s in range(1, N_DEV):
            peer, pid = _peer(s)
            for a in range(na):
                cp = pltpu.make_async_remote_copy(
                    src_ref=x_refs[a].at[pid], dst_ref=o_refs[a].at[me], send_sem=send_sems.at[a, s - 1],
                    recv_sem=recv_sems.at[a, s - 1], device_id=peer, device_id_type=MESH)
                cp.start()
                sends.append(cp)
        for s in range(1, N_DEV):
            peer, pid = _peer(s)
            for a in range(na):
                pltpu.make_async_remote_copy(
                    src_ref=x_refs[a].at[pid], dst_ref=o_refs[a].at[pid], send_sem=send_sems.at[a, s - 1],
                    recv_sem=recv_sems.at[a, s - 1], device_id=peer, device_id_type=MESH).wait_recv()
        for cp in sends:
            cp.wait_send()
        for cp in locals_:
            cp.wait()

    spec = pl.BlockSpec(memory_space=pl.ANY)
    return pl.pallas_call(
        body, in_specs=[spec] * na, out_specs=[spec] * na,
        out_shape=[_sds(x.shape, x.dtype) for x in xs],
        scratch_shapes=[pltpu.SemaphoreType.DMA((na, N_DEV - 1)), pltpu.SemaphoreType.DMA((na, N_DEV - 1)),
                        pltpu.SemaphoreType.DMA((na,))],
        compiler_params=pltpu.CompilerParams(vmem_limit_bytes=VMEM_LIMIT), name=name)(*xs)


_HBM = pl.BlockSpec(memory_space=pltpu.HBM)
_SEM = pl.BlockSpec(memory_space=pltpu.SEMAPHORE)
_EFFECT = pltpu.SideEffectType.DATAFLOW_SIDE_EFFECTING


def _split_copies(pattern, x_ref, land_ref, send_sem, recv_sem):
    me = _my_id()
    if pattern in ("gather", "scatter"):
        plan = []
        for s in range(1, N_DEV):
            peer, pid = _peer(s)
            plan.append((x_ref.at[pid] if pattern == "scatter" else x_ref, land_ref.at[me], peer))
    elif pattern == "to_chips":
        plan = [(x_ref, land_ref.at[me], _peer(s)[0]) for s in (1, 2, 4, 6)]
    else:
        sibling = _peer(1)[0]
        plan = [(land_ref.at[_peer(s)[1]], land_ref.at[_peer(s)[1]], sibling) for s in (2, 4, 6)]
    return [pltpu.make_async_remote_copy(src_ref=src, dst_ref=dst, send_sem=send_sem, recv_sem=recv_sem,
                                         device_id=dev, device_id_type=MESH) for src, dst, dev in plan]


def comm_start(xs, pattern, after, name, lands=None):
    na = len(xs)
    extra = [] if after is None else [after]
    me = _my_id()
    if lands is None:
        lands = []
        for x in xs:
            shape = x.shape if pattern == "scatter" else (N_DEV,) + x.shape
            own = lax.dynamic_slice_in_dim(x, me, 1, 0) if pattern == "scatter" else x[None]
            lands.append(lax.dynamic_update_slice(lax.empty(shape, x.dtype), own, (me,) + (0,) * (len(shape) - 1)))

    def body(*refs):
        x_refs, land_refs = refs[:na], refs[na:2 * na]
        send_sem, recv_sem = refs[2 * na + len(extra)], refs[2 * na + len(extra) + 1]
        token = refs[-1]
        for a in range(na):
            for cp in _split_copies(pattern, x_refs[a], land_refs[a], send_sem, recv_sem):
                cp.start()
        token[...] = jnp.zeros_like(token)

    outs = pl.pallas_call(
        body, name=name,
        out_shape=(pltpu.SemaphoreType.DMA(()), pltpu.SemaphoreType.DMA(()))
        + tuple(pltpu.HBM(x.shape, x.dtype) for x in xs) + tuple(pltpu.HBM(l.shape, l.dtype) for l in lands)
        + (_sds((8, LANES), F32),),
        in_specs=(_HBM,) * (2 * na) + (pl.BlockSpec(memory_space=pl.ANY),) * len(extra),
        out_specs=(_SEM, _SEM) + (_HBM,) * (2 * na) + (pl.BlockSpec(memory_space=pltpu.VMEM),),
        input_output_aliases={a: 2 + a for a in range(2 * na)},
        compiler_params=pltpu.CompilerParams(has_side_effects=_EFFECT),
    )(*[pltpu.with_memory_space_constraint(x, pltpu.HBM) for x in xs],
      *[pltpu.with_memory_space_constraint(l, pltpu.HBM) for l in lands], *extra)
    return dict(sems=outs[0:2], xs=outs[2:2 + na], lands=outs[2 + na:2 + 2 * na], token=outs[-1], pattern=pattern)


def comm_wait(started, after, name, with_xs=False):
    xs, lands = started["xs"], started["lands"]
    pattern = started["pattern"]
    na = len(xs)

    def body(*refs):
        x_refs, land_refs = refs[:na], refs[na:2 * na]
        send_sem, recv_sem = refs[2 * na], refs[2 * na + 1]
        for a in range(na):
            for cp in _split_copies(pattern, x_refs[a], land_refs[a], send_sem, recv_sem):
                cp.wait_send()
                cp.wait_recv()

    outs = pl.pallas_call(
        body, name=name,
        out_shape=tuple(pltpu.HBM(x.shape, x.dtype) for x in xs) + tuple(pltpu.HBM(l.shape, l.dtype) for l in lands),
        in_specs=(_HBM,) * (2 * na) + (_SEM, _SEM, pl.BlockSpec(memory_space=pl.ANY)),
        out_specs=(_HBM,) * (2 * na),
        input_output_aliases={a: a for a in range(2 * na)},
        compiler_params=pltpu.CompilerParams(has_side_effects=_EFFECT),
    )(*xs, *lands, *started["sems"], after)
    return (list(outs[na:]), list(outs[:na])) if with_xs else list(outs[na:])


def _cols_to_natural(g):
    return jnp.concatenate([g[k] for k in range(N_DEV)], axis=1)


def _vec8(rows, d):
    rows = [r.reshape(1, d).astype(F32) for r in rows]
    return jnp.concatenate(rows + [jnp.zeros((8 - len(rows), d), F32)], axis=0)


def _ffn_forward(x, vec, w_in_t, w_out):
    xn, h, a, b, u, y = ffn_fwd(x, vec, w_in_t, w_out)
    return xn, (x, h, a, b, u, y)


def _ffn_backward(dxo, saved, vec, w_in_t, w_out, on_rows=None):
    x, h, a, b, u, y = saved
    dy, dab, dx, part = ffn_bwd(dxo, y, vec, w_out, w_in_t, a, b, x)
    rows = part[0:4]
    token = on_rows(rows) if on_rows is not None else None
    g_out = grad_slots(u, dy, "ffn_dw_out", after=token)
    g_in_t = grad_slots(dab, h, "ffn_dw_in", after=token)
    return dx, g_in_t, g_out, rows


_TRANSPOSED = ("ffn1_w_in", "ffn2_w_in", "attn_w_q")
_COL_NATURAL = ("conv_w_in", "w_kv", "attn_w_o")
_ROW_SHARDED = ("ffn1_w_out", "ffn2_w_out", "conv_w_out")
_BIG = _TRANSPOSED + _COL_NATURAL + _ROW_SHARDED


def weight_chunks():
    chunks = []
    for layer in range(DEPTH):
        first = [("ffn1_w_in", layer), ("ffn1_w_out", layer)]
        if layer == N_A_LAYERS:
            first = [("w_kv", layer)] + first
        mixer = [("conv_w_in", layer), ("conv_w_out", layer)] if layer < N_A_LAYERS else [("attn_w_q", layer), ("attn_w_o", layer)]
        rest = mixer + [("ffn2_w_in", layer), ("ffn2_w_out", layer)]
        chunks += [first, rest] if layer == 0 else [first + rest]
    return chunks


def stacked_index(name, layer):
    if name == "w_kv":
        return None
    return layer - N_A_LAYERS if name.startswith("attn") else layer


class ChunkComm:
    def __init__(self, shards):
        self.shards = shards
        self.chunks = weight_chunks()

    def _shard(self, name, layer):
        idx = stacked_index(name, layer)
        return self.shards[name][0 if idx is None else idx]

    def start_gather(self, ci, after):
        xs = [self._shard(n, l).astype(BF16) for n, l in self.chunks[ci]]
        return comm_start(xs, "to_chips", after, f"gather_start_{ci}")

    def relay_gather(self, ci, started, after):
        lands, xs = comm_wait(started, after, f"gather_wait_{ci}", with_xs=True)
        return comm_start(xs, "relay", None, f"gather_relay_{ci}", lands=lands)

    def finish_gather(self, ci, relayed, after):
        lands = comm_wait(relayed, after, f"gather_done_{ci}")
        W = {}
        for key, g in zip(self.chunks[ci], lands):
            W[key] = _cols_to_natural(g) if key[0] in _COL_NATURAL else g.reshape(-1, g.shape[2])
        return W

    def start_exchange(self, ci, slots, after):
        return comm_start([slots[key] for key in self.chunks[ci]], "scatter", after, f"exchange_start_{ci}")

    def finish_exchange(self, ci, started, after):
        lands = comm_wait(started, after, f"exchange_wait_{ci}")
        return dict(zip(self.chunks[ci], lands))


def device_step(x, positions, target, mods, kvmods, small, comm, gather0):
    T, D = x.shape
    groups = DILATED_GROUPS
    dils = [dil for _, dil in groups]
    lane = jnp.arange(LANES) % HEAD_DIM
    inv = ROPE_THETA ** (-jnp.arange(0, ROPE_DIM, 2, dtype=F32) / ROPE_DIM)
    lane_rows = _vec8([jnp.where(lane < ROPE_DIM, inv[lane % (ROPE_DIM // 2)], 0.0), lane < ROPE_DIM,
                       (lane >= ROPE_DIM // 2) & (lane < ROPE_DIM), lane < ROPE_DIM // 2], LANES)
    tabs = rope_tables(positions.reshape(T, 1), lane_rows)

    def after_token(v, token):
        return v if token is None else v + token[0, 0]

    def vec_of(layer, sub):
        return _vec8([small["norm_g"][layer, sub], mods[layer, 3 * sub], mods[layer, 3 * sub + 1], mods[layer, 3 * sub + 2]], D)

    saved = []
    kv_saved = None
    k_sh = v_sh = None
    qw = GROUP_WIDTH * len(groups)
    chunk_of = {key: ci for ci, chunk in enumerate(comm.chunks) for key in chunk}
    W = {}
    flight = {"ci": 0, "started": gather0}

    relayed = {}

    def advance(after):
        ci = flight["ci"]
        if flight["started"] is None or ci in relayed:
            return None
        relayed[ci] = comm.relay_gather(ci, flight["started"], after)
        relayed[ci]["behind"] = relayed[ci]["token"]
        nxt = comm.start_gather(ci + 1, relayed[ci]["token"]) if ci + 1 < len(comm.chunks) else None
        flight.update(ci=ci + 1, started=nxt)
        if nxt is not None:
            relayed[ci]["behind"] = nxt["token"]
        return relayed[ci]["behind"]

    def need(key, after):
        if key not in W:
            ci = chunk_of[key]
            if ci not in relayed:
                assert ci == flight["ci"], (key, ci)
                advance(after)
            W.update(comm.finish_gather(ci, relayed[ci], relayed[ci]["behind"]))
        return W[key]

    for layer in range(DEPTH):
        if layer == N_A_LAYERS:
            w_kv = need(("w_kv", layer), x)
            kv_vec = _vec8([small["kv_norm_g"], kvmods[0], kvmods[1]], D)
            h_kv, *kv_pieces = proj_rope_fwd(x, kv_vec, w_kv, tabs, qw, False, dils, "kv_fwd")
            k_sh, v_sh = kv_pieces[:len(groups)], kv_pieces[len(groups):]
            kv_saved = (x, h_kv, kv_vec)
        rec = {}
        behind = tabs[0] if layer == 0 else x
        w_in, w_out = need(("ffn1_w_in", layer), behind), need(("ffn1_w_out", layer), behind)
        v1 = vec_of(layer, 0)
        x, rec["ffn1"] = _ffn_forward(x, v1, w_in, w_out)
        if layer < N_A_LAYERS:
            w_in, w_out = need(("conv_w_in", layer), x), need(("conv_w_out", layer), x)
            v2 = vec_of(layer, 1)
            cw = _vec8(list(small["conv_w"][layer]), D)
            x_in = x
            x, h, bcu, cv, z, y = conv_fwd(x, v2, cw, w_in, w_out)
            rec["mix"] = (x_in, h, bcu, cv, z, y, cw)
        else:
            w_q, w_o = need(("attn_w_q", layer), x), need(("attn_w_o", layer), x)
            v2 = vec_of(layer, 1)
            x_in = x
            h, *q = proj_rope_fwd(x, v2, w_q, tabs, qw, True, dils, "q_fwd")
            os_, ls = [], []
            for g, (win, dil) in enumerate(groups):
                o, l = attn_core_fwd(q[g], k_sh[g], v_sh[g], g, win // dil)
                os_.append(o)
                ls.append(l)
            x, mixed, y = attn_mix_out(os_, ls, dils, x, v2, w_o)
            rec["mix"] = (x_in, h, q, os_, ls, mixed, y)
        token = advance(x) if layer >= 1 else None
        w_in, w_out = need(("ffn2_w_in", layer), x), need(("ffn2_w_out", layer), x)
        v3 = after_token(vec_of(layer, 2), token)
        x, rec["ffn2"] = _ffn_forward(x, v3, w_in, w_out)
        rec["vecs"] = (v1, v2, v3)
        saved.append(rec)

    dx, part_final, loss_tile = final_loss(x, _vec8([small["final_norm_g"]], D), target)
    loss = loss_tile[0, 0]

    conv_rows = [None] * N_A_LAYERS
    kv_rows = None
    mod_rows = [[None] * 3 for _ in range(DEPTH)]
    dkv_pairs = [{"k": [], "v": []} for _ in groups]
    slots = {}
    exchanges = []
    token = None

    def send_ready_chunks():
        nonlocal token
        for ci in reversed(range(len(comm.chunks))):
            if ci not in [e[0] for e in exchanges] and all(key in slots for key in comm.chunks[ci]):
                started = comm.start_exchange(ci, slots, token)
                exchanges.append((ci, started))
                token = started["token"]

    vector_gather = {}

    def start_vector_gather(rows0):
        mod_rows[0][0] = rows0
        rows = jnp.stack([jnp.stack(r) for r in mod_rows])
        vecs = jnp.concatenate([rows[:, :, 1:4].reshape(-1), kv_rows[1:3].reshape(-1), kv_rows[0], part_final[0],
                                rows[:, :, 0].reshape(-1), jnp.stack(conv_rows).reshape(-1)])
        vector_gather["count"] = vecs.shape[0]
        vecs = _pad_rows(vecs.reshape(-1, 1), 8 * LANES).reshape(-1, LANES)
        vector_gather["started"] = comm_start([vecs], "gather", None, "vector_grads_start")
        return vector_gather["started"]["token"]

    for layer in reversed(range(DEPTH)):
        rec = saved[layer]
        v1, v2, v3 = rec["vecs"]
        dx, slots[("ffn2_w_in", layer)], slots[("ffn2_w_out", layer)], mod_rows[layer][2] = _ffn_backward(
            dx, rec["ffn2"], after_token(v3, token), W[("ffn2_w_in", layer)], W[("ffn2_w_out", layer)])
        if layer < N_A_LAYERS:
            x_in, h, bcu, cv, z, y, cw = rec["mix"]
            dx, dy, dbcu, part, dcw = conv_bwd(dx, x_in, y, bcu, cv, v2, cw, W[("conv_w_in", layer)], W[("conv_w_out", layer)])
            slots[("conv_w_out", layer)] = grad_slots(z, dy, "conv_dw_out")
            slots[("conv_w_in", layer)] = grad_slots(h, dbcu, "conv_dw_in", col_slots=True)
            conv_rows[layer] = dcw[0:3]
            mod_rows[layer][1] = part[0:4]
        else:
            x_in, h, q, os_, ls, mixed, y = rec["mix"]
            outs = attn_mix_bwd(dx, y, v2, W[("attn_w_o", layer)], os_, ls, dils)
            ng = len(groups)
            dy, dos, deltas, part_gate = outs[0], outs[1:1 + ng], outs[1 + ng:1 + 2 * ng], outs[1 + 2 * ng]
            slots[("attn_w_o", layer)] = grad_slots(mixed, dy, "attn_dw_o", col_slots=True)
            dqs = []
            for g, (win, dil) in enumerate(groups):
                dq, dkc, dkp, dvc, dvp = attn_core_bwd(q[g], k_sh[g], v_sh[g], dos[g], deltas[g], ls[g], g, win // dil)
                dqs.append(dq)
                dkv_pairs[g]["k"].append((dkc, dkp))
                dkv_pairs[g]["v"].append((dvc, dvp))
            dx, dqr, part_norm = proj_rope_bwd(dqs, dils, x_in, dx, v2, W[("attn_w_q", layer)], tabs, qw, True, "q_bwd")
            slots[("attn_w_q", layer)] = grad_slots(dqr, h, "attn_dw_q")
            mod_rows[layer][1] = jnp.concatenate([part_norm[0:3], part_gate[0:1]], axis=0)
        send_ready_chunks()
        dx, slots[("ffn1_w_in", layer)], slots[("ffn1_w_out", layer)], mod_rows[layer][0] = _ffn_backward(
            dx, rec["ffn1"], after_token(v1, token), W[("ffn1_w_in", layer)], W[("ffn1_w_out", layer)],
            on_rows=start_vector_gather if layer == 0 else None)
        if layer == N_A_LAYERS:
            x_kv, h_kv, kv_vec = kv_saved
            dparts = [dkv_combine(dkv_pairs[g]["k"], win // dil, f"dk_combine_g{g}") for g, (win, dil) in enumerate(groups)]
            dparts += [dkv_combine(dkv_pairs[g]["v"], win // dil, f"dv_combine_g{g}") for g, (win, dil) in enumerate(groups)]
            dx, dkvp, part_kv = proj_rope_bwd(dparts, dils, x_kv, dx, kv_vec, W[("w_kv", layer)], tabs, qw, False, "kv_bwd")
            slots[("w_kv", layer)] = grad_slots(h_kv, dkvp, "kv_dw", col_slots=True)
            kv_rows = part_kv[0:3]
        send_ready_chunks()

    return loss, dx, {"exchanges": exchanges, "vector_gather": vector_gather}


def _flat2(a):
    return a.reshape(-1, a.shape[-1])


def _pad_rows(a, mult):
    r = a.shape[0]
    pad = (-r) % mult
    return a if pad == 0 else jnp.concatenate([a, jnp.zeros((pad,) + a.shape[1:], a.dtype)], axis=0)


def kernel(x, c, positions, norm_g, ada_w, ada_b, ffn1_w_in, ffn1_w_out, ffn2_w_in, ffn2_w_out, conv_w_in, conv_w, conv_w_out, kv_norm_g, kv_ada_w, kv_ada_b, w_kv, attn_w_q, attn_w_o, final_norm_g, loss_target, m_norm_g, m_ada_w, m_ada_b, m_ffn1_w_in, m_ffn1_w_out, m_ffn2_w_in, m_ffn2_w_out, m_conv_w_in, m_conv_w, m_conv_w_out, m_kv_norm_g, m_kv_ada_w, m_kv_ada_b, m_w_kv, m_attn_w_q, m_attn_w_o, m_final_norm_g, v_norm_g, v_ada_w, v_ada_b, v_ffn1_w_in, v_ffn1_w_out, v_ffn2_w_in, v_ffn2_w_out, v_conv_w_in, v_conv_w, v_conv_w_out, v_kv_norm_g, v_kv_ada_w, v_kv_ada_b, v_w_kv, v_attn_w_q, v_attn_w_o, v_final_norm_g):
    names = ("norm_g", "ada_w", "ada_b", "ffn1_w_in", "ffn1_w_out", "ffn2_w_in", "ffn2_w_out", "conv_w_in", "conv_w",
             "conv_w_out", "kv_norm_g", "kv_ada_w", "kv_ada_b", "w_kv", "attn_w_q", "attn_w_o", "final_norm_g")
    wts = dict(zip(names, (norm_g, ada_w, ada_b, ffn1_w_in, ffn1_w_out, ffn2_w_in, ffn2_w_out, conv_w_in, conv_w, conv_w_out,
                           kv_norm_g, kv_ada_w, kv_ada_b, w_kv, attn_w_q, attn_w_o, final_norm_g)))
    mom = dict(zip(names, (m_norm_g, m_ada_w, m_ada_b, m_ffn1_w_in, m_ffn1_w_out, m_ffn2_w_in, m_ffn2_w_out, m_conv_w_in,
                           m_conv_w, m_conv_w_out, m_kv_norm_g, m_kv_ada_w, m_kv_ada_b, m_w_kv, m_attn_w_q, m_attn_w_o,
                           m_final_norm_g)))
    var = dict(zip(names, (v_norm_g, v_ada_w, v_ada_b, v_ffn1_w_in, v_ffn1_w_out, v_ffn2_w_in, v_ffn2_w_out, v_conv_w_in,
                           v_conv_w, v_conv_w_out, v_kv_norm_g, v_kv_ada_w, v_kv_ada_b, v_w_kv, v_attn_w_q, v_attn_w_o,
                           v_final_norm_g)))
    T, D = x.shape[1], x.shape[2]
    me = _my_id()
    nmod = ada_w.shape[2]
    nkv = kv_ada_w.shape[1]

    def stacked(w, n):
        w = w if w.ndim == 3 else w[None]
        return jnp.swapaxes(w, 1, 2) if n in _TRANSPOSED else w

    comm = ChunkComm({n: stacked(wts[n], n) for n in _BIG})
    W = {}

    ds = norm_g.shape[2]
    small = jnp.concatenate([c.reshape(-1), norm_g.reshape(-1), conv_w.reshape(-1)]).astype(F32)
    n_small = small.shape[0]
    small = _pad_rows(small.reshape(-1, 1), 8 * LANES).reshape(-1, LANES)
    (small_all,) = all_gather([small], pltpu.VMEM, "gather_small")
    small_all = small_all.reshape(N_DEV, -1)[:, :n_small]
    c_all = small_all[:, :D]
    def full_rows(off, count):
        return jnp.stack([small_all[:, off + i * ds:off + (i + 1) * ds].reshape(D) for i in range(count)])

    W["norm_g"] = full_rows(D, DEPTH * 3).reshape(DEPTH, 3, D)
    W["conv_w"] = full_rows(D + DEPTH * 3 * ds, N_A_LAYERS * 3).reshape(N_A_LAYERS, 3, D)
    W["kv_norm_g"], W["final_norm_g"] = kv_norm_g, final_norm_g

    ada_b_mine = lax.dynamic_slice_in_dim(ada_b, me * nmod, nmod, axis=1).reshape(DEPTH, 1, nmod)
    kv_b_mine = lax.dynamic_slice_in_dim(kv_ada_b, me * nkv, nkv, axis=0).reshape(1, 1, nkv)
    mods_cols = mods_project(c_all, ada_w, ada_b_mine)
    kv_cols = mods_project(c_all, kv_ada_w.reshape(1, D, nkv), kv_b_mine)
    mcat = jnp.concatenate([mods_cols[l] for l in range(DEPTH)] + [kv_cols[0]], axis=1)
    wm = mcat.shape[1]
    if wm % LANES:
        mcat = jnp.concatenate([mcat, jnp.zeros((N_DEV, LANES - wm % LANES), F32)], axis=1)
    (mods_all,) = exchange_slots([mcat.reshape(N_DEV, 1, -1)], "exchange_mods")
    gather0 = comm.start_gather(0, mods_all)
    mods_all = mods_all.reshape(N_DEV, -1)
    mods = jnp.stack([mods_all[:, l * nmod:(l + 1) * nmod].reshape(N_MOD, D) for l in range(DEPTH)])
    kvmods = mods_all[:, DEPTH * nmod:DEPTH * nmod + nkv].reshape(2, D)

    loss_local, dx, grads = device_step(x[0], positions[0], loss_target[0], mods, kvmods, W, comm, gather0)
    loss = lax.psum(loss_local, MESH_AXES)

    (vec_all,) = comm_wait(grads["vector_gather"]["started"], grads["exchanges"][-1][1]["token"], "vector_grads_wait")
    vec_all = vec_all.reshape(N_DEV, -1)[:, :grads["vector_gather"]["count"]]
    nm_, nk_ = DEPTH * N_MOD * D, 2 * D
    dmods_all = vec_all[:, :nm_].reshape(N_DEV, DEPTH, N_MOD * D)
    dkvm_all = vec_all[:, nm_:nm_ + nk_]
    rest = vec_all[:, nm_ + nk_:]
    parts_kv_norm, parts_final = rest[:, :D].reshape(N_DEV, 1, D), rest[:, D:2 * D].reshape(N_DEV, 1, D)
    parts_norm = lax.dynamic_slice_in_dim(rest[:, 2 * D:2 * D + DEPTH * 3 * D].reshape(N_DEV, DEPTH * 3, D), me * ds, ds, axis=2)
    parts_conv = lax.dynamic_slice_in_dim(rest[:, 2 * D + DEPTH * 3 * D:].reshape(N_DEV, N_A_LAYERS * 3, D), me * ds, ds, axis=2)
    dm_cols = lax.dynamic_slice_in_dim(dmods_all, me * nmod, nmod, axis=2)
    dm_mine = jnp.stack([dm_cols[:, l] for l in range(DEPTH)])
    dkv_mine = lax.dynamic_slice_in_dim(dkvm_all, me * nkv, nkv, axis=1).reshape(1, N_DEV, nkv)
    g_ada_w = mods_weight_grad(c_all, dm_mine)
    g_kv_ada_w = mods_weight_grad(c_all, dkv_mine)[0]

    out_g, out_d, out_m, out_v = {}, {}, {}, {}

    def update(n, g, w, parts=False):
        shp = w.shape
        w2 = w.reshape(1, -1) if w.ndim == 1 else _flat2(w)
        g2 = g if parts else g.reshape(w2.shape)
        res = adam_update(g2, w2, mom[n].reshape(w2.shape), var[n].reshape(w2.shape), parts, "adam_" + n)
        out_g[n], out_d[n], out_m[n], out_v[n] = (r.reshape(shp) for r in res)

    moms = {n: stacked(mom[n], n) for n in _BIG}
    vars_ = {n: stacked(var[n], n) for n in _BIG}
    results = {}
    after = dx
    for ci, started in grads["exchanges"]:
        for (n, layer), parts in comm.finish_exchange(ci, started, after).items():
            idx = stacked_index(n, layer)
            results[n] = adam_layer(parts, comm.shards[n], moms[n], vars_[n], results.get(n), 0 if idx is None else idx,
                                    after, f"adam_{n}_{layer}")
            after = results[n][1]
    for n in _BIG:
        res = [jnp.swapaxes(r, 1, 2) if n in _TRANSPOSED else r for r in results[n]]
        out_g[n], out_d[n], out_m[n], out_v[n] = (r.reshape(wts[n].shape) for r in res)
    update("ada_w", g_ada_w, ada_w)
    update("kv_ada_w", g_kv_ada_w, kv_ada_w)
    update("ada_b", dmods_all, ada_b, True)
    update("kv_ada_b", dkvm_all.reshape(N_DEV, 1, nk_), kv_ada_b, True)
    update("kv_norm_g", parts_kv_norm, kv_norm_g, True)
    update("final_norm_g", parts_final, final_norm_g, True)
    update("norm_g", parts_norm, norm_g, True)
    update("conv_w", parts_conv, conv_w, True)

    return (loss, dx.reshape(x.shape), *[out_g[n] for n in names], *[out_d[n] for n in names],
            *[out_m[n] for n in names], *[out_v[n] for n in names])
```

```python
import functools

import jax
import jax.numpy as jnp
from jax import lax
from jax.experimental import pallas as pl
from jax.experimental.pallas import tpu as pltpu

F32, BF16 = jnp.float32, jnp.bfloat16

N_DEV = 8
MESH_AXES = ("x", "y", "c")
DEPTH = 4
N_A_LAYERS = 2
HEAD_DIM = 64
HEADS_PER_GROUP = 8
GROUP_WIDTH = HEAD_DIM * HEADS_PER_GROUP
DILATED_GROUPS = ((128, 1), (512, 4), (2048, 16))
ROPE_DIM = HEAD_DIM // 4
ROPE_THETA = 500000.0
NORM_EPS = 1e-5
FFN_RES_WEIGHT = 0.5
N_MOD = 9
ADAM_LR, ADAM_B1, ADAM_B2, ADAM_EPS, ADAM_WD, ADAM_STEP = 0.001, 0.9, 0.999, 1e-08, 0.01, 10

LANES = 128
TOKEN_TILE = 512
FFN_BWD_TILE = 256
FWD_QUERY_BLOCKS = 8
BWD_QUERY_BLOCKS = 4
CONTRACT_TILE = 4096
GRAD_COLS = 768
MXU_WIDTH = 256
VMEM_LIMIT = 56 * 1024 * 1024
MESH = pl.DeviceIdType.MESH


def _cp(*sem):
    return pltpu.CompilerParams(dimension_semantics=sem, vmem_limit_bytes=VMEM_LIMIT)


def _pick(n, cap, mult=LANES):
    if n <= cap:
        return n
    best = None
    for t in range(mult, cap + 1, mult):
        if n % t == 0:
            best = t
    assert best is not None, (n, cap)
    return best


def _tok(tm, w):
    return pl.BlockSpec((tm, w), lambda i: (i, 0))


def _res(shape):
    nd = len(shape)
    return pl.BlockSpec(shape, lambda *_: (0,) * nd, pipeline_mode=pl.Buffered(1))


def _sds(shape, dt):
    return jax.ShapeDtypeStruct(shape, dt)


def _sigmoid(a):
    return 1.0 / (1.0 + jnp.exp(-a))


def _modnorm(x, g, sh, sc):
    r = lax.rsqrt(jnp.mean(x * x, axis=-1, keepdims=True) + NORM_EPS)
    return (x * r * g) * (1.0 + sc) + sh


def _dot(a, b):
    return jnp.dot(a, b, preferred_element_type=F32)


def _dot_nt(a, b):
    return lax.dot_general(a, b, (((1,), (1,)), ((), ())), preferred_element_type=F32)


def _dot_tn(a, b):
    return lax.dot_general(a, b, (((0,), (0,)), ((), ())), preferred_element_type=F32)


def _rows8(rows, d):
    pad = 8 - len(rows)
    return jnp.concatenate(list(rows) + [jnp.zeros((pad, d), F32)], axis=0)


def _acc_rows(ref, tile, first):
    @pl.when(first)
    def _():
        ref[...] = tile

    @pl.when(jnp.logical_not(first))
    def _():
        ref[...] += tile


def ffn_fwd(x, vec, w_in_t, w_out):
    T, D = x.shape
    F = w_in_t.shape[0] // 2
    tm, cw = min(TOKEN_TILE, T), _pick(F, MXU_WIDTH)

    def body(x_ref, vec_ref, wi_ref, wo_ref, xn_ref, h_ref, ga_ref, gb_ref, u_ref, y_ref):
        x_t = x_ref[...]
        hb = _modnorm(x_t, vec_ref[0:1], vec_ref[1:2], vec_ref[2:3]).astype(BF16)
        h_ref[...] = hb
        for c in range(F // cw):
            lo, hi = c * cw, (c + 1) * cw
            a = _dot_nt(hb, wi_ref[lo:hi, :])
            b = _dot_nt(hb, wi_ref[F + lo:F + hi, :])
            sg = _sigmoid(a)
            silu = a * sg
            ga_ref[:, lo:hi] = (b * (sg + silu * (1.0 - sg))).astype(BF16)
            gb_ref[:, lo:hi] = silu.astype(BF16)
            u_ref[:, lo:hi] = (silu * b).astype(BF16)
        y = _dot(u_ref[...], wo_ref[...])
        y_ref[...] = y.astype(BF16)
        xn_ref[...] = x_t + (FFN_RES_WEIGHT * (1.0 + vec_ref[3:4])) * y

    return pl.pallas_call(
        body, grid=(T // tm,),
        in_specs=[_tok(tm, D), _res((8, D)), _res((2 * F, D)), _res((F, D))],
        out_specs=[_tok(tm, D), _tok(tm, D), _tok(tm, F), _tok(tm, F), _tok(tm, F), _tok(tm, D)],
        out_shape=[_sds((T, D), F32), _sds((T, D), BF16), _sds((T, F), BF16), _sds((T, F), BF16), _sds((T, F), BF16),
                   _sds((T, D), BF16)],
        compiler_params=_cp("arbitrary"), name="ffn_fwd")(x, vec, w_in_t, w_out)


def ffn_bwd(dxo, y, vec, w_out, w_in_t, a, b, x):
    T, D = x.shape
    F = a.shape[1]
    tm, cw = min(FFN_BWD_TILE, T), _pick(F, MXU_WIDTH)

    def body(dxo_ref, y_ref, vec_ref, wo_ref, wi_ref, a_ref, b_ref, x_ref, dy_ref, dab_ref, dx_ref, part_ref):
        dxo_t = dxo_ref[...]
        dyb = (dxo_t * (FFN_RES_WEIGHT * (1.0 + vec_ref[3:4]))).astype(BF16)
        dy_ref[...] = dyb
        dgate = FFN_RES_WEIGHT * jnp.sum(dxo_t * y_ref[...].astype(F32), axis=0, keepdims=True)
        for c in range(F // cw):
            lo, hi = c * cw, (c + 1) * cw
            du = _dot_nt(dyb, wo_ref[lo:hi, :])
            dab_ref[:, lo:hi] = (du * a_ref[:, lo:hi].astype(F32)).astype(BF16)
            dab_ref[:, F + lo:F + hi] = (du * b_ref[:, lo:hi].astype(F32)).astype(BF16)
        dh = _dot(dab_ref[...], wi_ref[...])
        _, vjp = jax.vjp(_modnorm, x_ref[...], vec_ref[0:1], vec_ref[1:2], vec_ref[2:3])
        dx, dg, dsh, dsc = vjp(dh)
        dx_ref[...] = dxo_t + dx
        _acc_rows(part_ref, _rows8([dg, dsh, dsc, dgate], D), pl.program_id(0) == 0)

    return pl.pallas_call(
        body, grid=(T // tm,),
        in_specs=[_tok(tm, D), _tok(tm, D), _res((8, D)), _res((F, D)), _res((2 * F, D)), _tok(tm, F), _tok(tm, F), _tok(tm, D)],
        out_specs=[_tok(tm, D), _tok(tm, 2 * F), _tok(tm, D), pl.BlockSpec((8, D), lambda i: (0, 0))],
        out_shape=[_sds((T, D), BF16), _sds((T, 2 * F), BF16), _sds((T, D), F32), _sds((8, D), F32)],
        compiler_params=_cp("arbitrary"), name="ffn_bwd")(dxo, y, vec, w_out, w_in_t, a, b, x)


def grad_slots(a, b, name, col_slots=False, after=None):
    T, M = a.shape
    extra = [] if after is None else [after]
    N = b.shape[1]
    tk = min(CONTRACT_TILE, T)
    nk = T // tk
    tmm = _pick(M, 1408)
    if col_slots:
        ns = N // N_DEV
        sp = max(s for s in (1, 2, 4, 8) if ns * s <= GRAD_COLS or s == 1)
        tn = ns * sp
    else:
        tn = _pick(N, GRAD_COLS)

    def body(a_ref, b_ref, *rest):
        o_ref, acc = rest[-2:]
        k = pl.program_id(2)
        t = _dot_tn(a_ref[...], b_ref[...])

        @pl.when(k == 0)
        def _():
            acc[...] = t

        @pl.when(k > 0)
        def _():
            acc[...] += t

        @pl.when(k == nk - 1)
        def _():
            if col_slots:
                for s in range(sp):
                    o_ref[s] = acc[:, s * ns:(s + 1) * ns].astype(BF16)
            else:
                o_ref[...] = acc[...].astype(BF16)

    if col_slots:
        out_spec, out_shape = pl.BlockSpec((sp, tmm, ns), lambda i, j, k: (j, i, 0)), _sds((N_DEV, M, ns), BF16)
    else:
        out_spec, out_shape = pl.BlockSpec((tmm, tn), lambda i, j, k: (i, j)), _sds((M, N), BF16)
    out = pl.pallas_call(
        body, grid=(M // tmm, N // tn, nk),
        in_specs=[pl.BlockSpec((tk, tmm), lambda i, j, k: (k, i)), pl.BlockSpec((tk, tn), lambda i, j, k: (k, j))]
        + [pl.BlockSpec(memory_space=pl.ANY)] * len(extra),
        out_specs=out_spec, out_shape=out_shape,
        scratch_shapes=[pltpu.VMEM((tmm, tn), F32)],
        compiler_params=_cp("arbitrary", "arbitrary", "arbitrary"), name=name)(a, b, *extra)
    return out if col_slots else out.reshape(N_DEV, M // N_DEV, N)


def conv_fwd(x, vec, cw, w_in, w_out):
    T, D = x.shape
    tm = min(TOKEN_TILE, T)

    def body(x_ref, vec_ref, cw_ref, wi_ref, wo_ref, xn_ref, h_ref, bcu_ref, cv_ref, z_ref, y_ref, vbuf):
        @pl.when(pl.program_id(0) == 0)
        def _():
            vbuf[0:8, :] = jnp.zeros((8, D), F32)

        x_t = x_ref[...]
        hb = _modnorm(x_t, vec_ref[0:1], vec_ref[1:2], vec_ref[2:3]).astype(BF16)
        h_ref[...] = hb
        bcu = _dot(hb, wi_ref[...])
        bcu_ref[...] = bcu.astype(BF16)
        bg, v = bcu[:, 0:D], bcu[:, D:2 * D] * bcu[:, 2 * D:3 * D]
        vbuf[8:8 + tm, :] = v
        conv = cw_ref[0:1] * vbuf[6:6 + tm, :] + cw_ref[1:2] * vbuf[7:7 + tm, :] + cw_ref[2:3] * v
        cv_ref[...] = conv.astype(BF16)
        zb = (bg * conv).astype(BF16)
        z_ref[...] = zb
        y = _dot(zb, wo_ref[...])
        y_ref[...] = y.astype(BF16)
        xn_ref[...] = x_t + (1.0 + vec_ref[3:4]) * y
        vbuf[0:8, :] = vbuf[tm:tm + 8, :]

    return pl.pallas_call(
        body, grid=(T // tm,),
        in_specs=[_tok(tm, D), _res((8, D)), _res((8, D)), _res((D, 3 * D)), _res((D, D))],
        out_specs=[_tok(tm, D), _tok(tm, D), _tok(tm, 3 * D), _tok(tm, D), _tok(tm, D), _tok(tm, D)],
        out_shape=[_sds((T, D), F32), _sds((T, D), BF16), _sds((T, 3 * D), BF16), _sds((T, D), BF16),
                   _sds((T, D), BF16), _sds((T, D), BF16)],
        scratch_shapes=[pltpu.VMEM((tm + 8, D), F32)],
        compiler_params=_cp("arbitrary"), name="conv_fwd")(x, vec, cw, w_in, w_out)


def conv_bwd(dxo, x, y, bcu, cv, vec, cw, w_in, w_out):
    T, D = x.shape
    tm = min(TOKEN_TILE, T)
    nt = T // tm

    def body(dxo_ref, x_ref, y_ref, bcu_ref, cv_ref, vec_ref, cw_ref, wi_ref, wo_ref,
             dx_ref, dy_ref, dbcu_ref, part_ref, dcw_ref, dcbuf):
        first = pl.program_id(0) == 0

        @pl.when(first)
        def _():
            dcbuf[tm:tm + 8, :] = jnp.zeros((8, D), F32)

        dxo_t = dxo_ref[...]
        dyb = (dxo_t * (1.0 + vec_ref[3:4])).astype(BF16)
        dy_ref[...] = dyb
        dgate = jnp.sum(dxo_t * y_ref[...].astype(F32), axis=0, keepdims=True)
        dz = _dot_nt(dyb, wo_ref[...])
        bcu_t = bcu_ref[...].astype(F32)
        bg, cg, ug = bcu_t[:, 0:D], bcu_t[:, D:2 * D], bcu_t[:, 2 * D:3 * D]
        dconv = dz * bg
        dbg = dz * cv_ref[...].astype(F32)
        dcbuf[0:tm, :] = dconv
        d1, d2 = dcbuf[1:tm + 1, :], dcbuf[2:tm + 2, :]
        dv = cw_ref[2:3] * dconv + cw_ref[1:2] * d1 + cw_ref[0:1] * d2
        v = cg * ug
        dcw = _rows8([jnp.sum(d2 * v, axis=0, keepdims=True), jnp.sum(d1 * v, axis=0, keepdims=True),
                      jnp.sum(dconv * v, axis=0, keepdims=True)], D)
        dbcu = jnp.concatenate([dbg, dv * ug, dv * cg], axis=1).astype(BF16)
        dbcu_ref[...] = dbcu
        dh = _dot_nt(dbcu, wi_ref[...])
        _, vjp = jax.vjp(_modnorm, x_ref[...], vec_ref[0:1], vec_ref[1:2], vec_ref[2:3])
        dx, dg, dsh, dsc = vjp(dh)
        dx_ref[...] = dxo_t + dx
        _acc_rows(part_ref, _rows8([dg, dsh, dsc, dgate], D), first)
        _acc_rows(dcw_ref, dcw, first)
        dcbuf[tm:tm + 8, :] = dcbuf[0:8, :]

    def rev(w):
        return pl.BlockSpec((tm, w), lambda i: (nt - 1 - i, 0))

    return pl.pallas_call(
        body, grid=(nt,),
        in_specs=[rev(D), rev(D), rev(D), rev(3 * D), rev(D), _res((8, D)), _res((8, D)), _res((D, 3 * D)), _res((D, D))],
        out_specs=[rev(D), rev(D), rev(3 * D), pl.BlockSpec((8, D), lambda i: (0, 0)), pl.BlockSpec((8, D), lambda i: (0, 0))],
        out_shape=[_sds((T, D), F32), _sds((T, D), BF16), _sds((T, 3 * D), BF16), _sds((8, D), F32), _sds((8, D), F32)],
        scratch_shapes=[pltpu.VMEM((tm + 8, D), F32)],
        compiler_params=_cp("arbitrary"), name="conv_bwd")(dxo, x, y, bcu, cv, vec, cw, w_in, w_out)


def rope_tables(pos, lane_rows):
    T = pos.shape[0]
    tm = min(TOKEN_TILE, T)

    def body(p_ref, lr_ref, c_ref, sp_ref, sm_ref):
        ang = p_ref[...].astype(F32) * lr_ref[0:1]
        cs, sn = jnp.cos(ang), jnp.sin(ang)
        c_ref[...] = jnp.where(lr_ref[1:2] > 0.5, cs, 1.0)
        sp_ref[...] = jnp.where(lr_ref[2:3] > 0.5, sn, 0.0)
        sm_ref[...] = jnp.where(lr_ref[3:4] > 0.5, -sn, 0.0)

    return pl.pallas_call(
        body, grid=(T // tm,),
        in_specs=[_tok(tm, 1), _res((8, LANES))],
        out_specs=[_tok(tm, LANES)] * 3,
        out_shape=[_sds((T, LANES), F32)] * 3,
        compiler_params=_cp("arbitrary"), name="rope_tables")(pos, lane_rows)


def _rope(t, c, sp, sm):
    w = t.shape[1]
    reps = w // LANES
    cf, spf, smf = jnp.tile(c, (1, reps)), jnp.tile(sp, (1, reps)), jnp.tile(sm, (1, reps))
    half = ROPE_DIM // 2
    return t * cf + pltpu.roll(t, half, axis=1) * spf + pltpu.roll(t, w - half, axis=1) * smf


def _rope_t(d, c, sp, sm):
    w = d.shape[1]
    reps = w // LANES
    cf, spf, smf = jnp.tile(c, (1, reps)), jnp.tile(sp, (1, reps)), jnp.tile(sm, (1, reps))
    half = ROPE_DIM // 2
    return d * cf + pltpu.roll(d * spf, w - half, axis=1) + pltpu.roll(d * smf, half, axis=1)


def _split_residues(v, d, stage):
    tm, width = v.shape
    if d == 1:
        return [v]
    nj = width // LANES
    for j in range(nj):
        stage[j] = v[:, j * LANES:(j + 1) * LANES]
    return [jnp.concatenate([stage[j, pl.ds(r, tm // d, stride=d), :] for j in range(nj)], axis=1) for r in range(d)]


def _merge_residues(piece, d, tm, width, stage):
    if d == 1:
        return piece(0)
    nj = width // LANES
    for r in range(d):
        p = piece(r)
        for j in range(nj):
            stage[j, pl.ds(r, tm // d, stride=d), :] = p[:, j * LANES:(j + 1) * LANES]
    return jnp.concatenate([stage[j] for j in range(nj)], axis=1)


def _residue_spec(d, tm, width=GROUP_WIDTH):
    return pl.BlockSpec((d, tm // d, width), lambda i: (0, i, 0))


def _stage_scratch(tm):
    return pltpu.VMEM((GROUP_WIDTH // LANES, tm, LANES), F32)


def proj_rope_fwd(x, vec, w, tabs, n_rope, transposed, dils, name):
    T, D = x.shape
    N = w.shape[0] if transposed else w.shape[1]
    tm = min(TOKEN_TILE, T)
    GW = GROUP_WIDTH
    piece_dils = [dils[j % len(dils)] for j in range(N // GW)]

    def body(x_ref, vec_ref, w_ref, c_ref, sp_ref, sm_ref, h_ref, *rest):
        out_refs, stage = rest[:-1], rest[-1]
        hb = _modnorm(x_ref[...], vec_ref[0:1], vec_ref[1:2], vec_ref[2:3]).astype(BF16)
        h_ref[...] = hb
        p = _dot_nt(hb, w_ref[...]) if transposed else _dot(hb, w_ref[...])
        pr = _rope(p[:, 0:n_rope], c_ref[...], sp_ref[...], sm_ref[...])
        for j, d in enumerate(piece_dils):
            src = pr if (j + 1) * GW <= n_rope else p
            for r, rows in enumerate(_split_residues(src[:, j * GW:(j + 1) * GW], d, stage)):
                out_refs[j][r] = rows.astype(BF16)

    return pl.pallas_call(
        body, grid=(T // tm,),
        in_specs=[_tok(tm, D), _res((8, D)), _res(w.shape)] + [_tok(tm, LANES)] * 3,
        out_specs=[_tok(tm, D)] + [_residue_spec(d, tm) for d in piece_dils],
        out_shape=[_sds((T, D), BF16)] + [_sds((d, T // d, GW), BF16) for d in piece_dils],
        scratch_shapes=[_stage_scratch(tm)],
        compiler_params=_cp("arbitrary"), name=name)(x, vec, w, *tabs)


def proj_rope_bwd(dparts, dils, x, dxo, vec, w, tabs, n_rope, transposed, name):
    T, D = x.shape
    N = w.shape[0] if transposed else w.shape[1]
    tm = min(TOKEN_TILE, T)
    GW = GROUP_WIDTH
    npart = len(dparts)
    piece_dils = [dils[j % len(dils)] for j in range(npart)]

    def body(*refs):
        d_refs = refs[:npart]
        x_ref, dxo_ref, vec_ref, w_ref, c_ref, sp_ref, sm_ref, dx_ref, dp_ref, part_ref, stage = refs[npart:]
        d = jnp.concatenate([_merge_residues(lambda r, ref=ref: ref[r].astype(F32), dd, tm, GW, stage)
                             for ref, dd in zip(d_refs, piece_dils)], axis=1)
        dr = _rope_t(d[:, 0:n_rope], c_ref[...], sp_ref[...], sm_ref[...])
        if n_rope < N:
            dr = jnp.concatenate([dr, d[:, n_rope:N]], axis=1)
        dpb = dr.astype(BF16)
        dp_ref[...] = dpb
        dh = _dot(dpb, w_ref[...]) if transposed else _dot_nt(dpb, w_ref[...])
        _, vjp = jax.vjp(_modnorm, x_ref[...], vec_ref[0:1], vec_ref[1:2], vec_ref[2:3])
        dx, dg, dsh, dsc = vjp(dh)
        dx_ref[...] = dxo_ref[...] + dx
        _acc_rows(part_ref, _rows8([dg, dsh, dsc], D), pl.program_id(0) == 0)

    return pl.pallas_call(
        body, grid=(T // tm,),
        in_specs=[_residue_spec(d, tm) for d in piece_dils] + [_tok(tm, D), _tok(tm, D), _res((8, D)), _res(w.shape)]
        + [_tok(tm, LANES)] * 3,
        out_specs=[_tok(tm, D), _tok(tm, N), pl.BlockSpec((8, D), lambda i: (0, 0))],
        out_shape=[_sds((T, D), F32), _sds((T, N), BF16), _sds((8, D), F32)],
        scratch_shapes=[_stage_scratch(tm)],
        compiler_params=_cp("arbitrary"), name=name)(*dparts, x, dxo, vec, w, *tabs)


def _valid_mask(n, i):
    qi = lax.broadcasted_iota(jnp.int32, (n, 2 * n), 0)
    kj = lax.broadcasted_iota(jnp.int32, (n, 2 * n), 1)
    dist = n + qi - kj
    return (dist >= 0) & (dist <= n) & ((kj >= n) | (i > 0))


STAT_STRIDE = LANES // HEADS_PER_GROUP


def _head_of_lane():
    return lax.broadcasted_iota(jnp.int32, (1, LANES), 1) // STAT_STRIDE


def attn_core_fwd(q, k, v, g, n):
    d, M, GW = q.shape
    scale = HEAD_DIM ** -0.5

    qb = min(FWD_QUERY_BLOCKS, M // n)

    def body(q_ref, kp_ref, kc_ref, vp_ref, vc_ref, o_ref, l_ref):
        masks = [_valid_mask(n, pl.program_id(1))] + [_valid_mask(n, 1)] * (qb - 1)
        first = lax.broadcasted_iota(jnp.int32, (1, LANES), 1) < HEAD_DIM
        head_of_lane = _head_of_lane()
        lse = [jnp.zeros((n, LANES), F32) for _ in range(qb)]
        for pair in range(HEADS_PER_GROUP * HEAD_DIM // LANES):
            ps = slice(LANES * pair, LANES * (pair + 1))
            kall = jnp.concatenate([kp_ref[:, ps], kc_ref[:, ps]], axis=0)
            vall = jnp.concatenate([vp_ref[:, ps], vc_ref[:, ps]], axis=0)
            for blk in range(qb):
                rows = slice(blk * n, (blk + 1) * n)
                keys, vals = kall[blk * n:(blk + 2) * n], vall[blk * n:(blk + 2) * n]
                q2 = q_ref[rows, ps]
                o2, l2 = [], []
                for sel in (first, jnp.logical_not(first)):
                    s = jnp.where(masks[blk], _dot_nt(jnp.where(sel, q2, jnp.zeros_like(q2)), keys) * scale, -1e30)
                    m = jnp.max(s, axis=1, keepdims=True)
                    p = jnp.exp(s - m)
                    den = jnp.sum(p, axis=1, keepdims=True)
                    o2.append(_dot((p / den).astype(BF16), vals))
                    l2.append(m + jnp.log(den))
                o_ref[rows, ps] = jnp.where(first, o2[0], o2[1]).astype(BF16)
                for half in range(2):
                    lse[blk] = jnp.where(head_of_lane == 2 * pair + half, l2[half], lse[blk])
        for blk in range(qb):
            l_ref[blk * n:(blk + 1) * n, :] = lse[blk]

    two = pl.BlockSpec((None, qb * n, GW), lambda r, i: (r, i, 0))
    prv = pl.BlockSpec((None, n, GW), lambda r, i: (r, jnp.maximum(qb * i - 1, 0), 0))
    stat = pl.BlockSpec((None, qb * n, LANES), lambda r, i: (r, i, 0))
    return pl.pallas_call(
        body, grid=(d, M // (qb * n)),
        in_specs=[two, prv, two, prv, two], out_specs=[two, stat],
        out_shape=[_sds((d, M, GW), BF16), _sds((d, M, LANES), F32)],
        compiler_params=_cp("arbitrary", "arbitrary"), name=f"attn_fwd_g{g}")(q, k, k, v, v)


def attn_core_bwd(q, k, v, do, delta, lse, g, n, prior=None):
    d, M, GW = q.shape
    scale = HEAD_DIM ** -0.5
    qb = min(BWD_QUERY_BLOCKS, M // n)
    steps = M // (qb * n)
    priors = () if prior is None else tuple(prior)

    def body(q_ref, kp_ref, kc_ref, vp_ref, vc_ref, do_ref, d_ref, l_ref, *rest):
        prior_refs, (dq_ref, dk_ref, dv_ref, carry_k, carry_v) = rest[:-5], rest[-5:]

        @pl.when(pl.program_id(1) == 0)
        def _():
            carry_k[...] = jnp.zeros_like(carry_k)
            carry_v[...] = jnp.zeros_like(carry_v)

        masks = [_valid_mask(n, steps - 1 - pl.program_id(1))] + [_valid_mask(n, 1)] * (qb - 1)
        first = lax.broadcasted_iota(jnp.int32, (1, LANES), 1) < HEAD_DIM
        for pair in range(HEADS_PER_GROUP * HEAD_DIM // LANES):
            ps = slice(LANES * pair, LANES * (pair + 1))
            kall = jnp.concatenate([kp_ref[:, ps], kc_ref[:, ps]], axis=0)
            vall = jnp.concatenate([vp_ref[:, ps], vc_ref[:, ps]], axis=0)
            own = []
            for blk in range(qb):
                rows = slice(blk * n, (blk + 1) * n)
                keys, vals = kall[blk * n:(blk + 2) * n], vall[blk * n:(blk + 2) * n]
                q2, do2 = q_ref[rows, ps], do_ref[rows, ps]
                dq2, dk, dv = [], None, None
                for half, sel in enumerate((first, jnp.logical_not(first))):
                    qm = jnp.where(sel, q2, jnp.zeros_like(q2))
                    dom = jnp.where(sel, do2, jnp.zeros_like(do2))
                    s = jnp.where(masks[blk], _dot_nt(qm, keys) * scale, -1e30)
                    lane0 = STAT_STRIDE * (2 * pair + half)
                    p = jnp.exp(s - l_ref[rows, lane0:lane0 + 1])
                    dp = _dot_nt(dom, vals)
                    ds = (p * (dp - d_ref[rows, lane0:lane0 + 1]) * scale).astype(BF16)
                    dq2.append(_dot(ds, keys))
                    dkh = _dot_tn(ds, qm)
                    dvh = _dot_tn(p.astype(BF16), dom)
                    dk = dkh if dk is None else dk + dkh
                    dv = dvh if dv is None else dv + dvh
                dq_ref[rows, ps] = jnp.where(first, dq2[0], dq2[1]).astype(BF16)
                own.append((dk, dv))
            for t, (o_ref, carry) in enumerate(((dk_ref, carry_k), (dv_ref, carry_v))):
                pieces = [own[blk][t][n:2 * n] + own[blk + 1][t][0:n] for blk in range(qb - 1)]
                pieces.append(own[qb - 1][t][n:2 * n] + carry[:, ps])
                carry[:, ps] = own[0][t][0:n]
                for blk, piece in enumerate(pieces):
                    rows = slice(blk * n, (blk + 1) * n)
                    if prior_refs:
                        piece = piece + prior_refs[t][rows, ps].astype(F32)
                    o_ref[rows, ps] = piece.astype(BF16)

    run = pl.BlockSpec((None, qb * n, GW), lambda r, i: (r, steps - 1 - i, 0))
    prv = pl.BlockSpec((None, n, GW), lambda r, i: (r, jnp.maximum(qb * (steps - 1 - i) - 1, 0), 0))
    stat = pl.BlockSpec((None, qb * n, LANES), lambda r, i: (r, steps - 1 - i, 0))
    return pl.pallas_call(
        body, grid=(d, steps),
        in_specs=[run, prv, run, prv, run, run, stat, stat] + [run] * len(priors), out_specs=[run, run, run],
        out_shape=[_sds((d, M, GW), BF16)] * 3,
        scratch_shapes=[pltpu.VMEM((n, GW), F32)] * 2,
        compiler_params=_cp("arbitrary", "arbitrary"), name=f"attn_bwd_g{g}")(q, k, k, v, v, do, delta, lse, *priors)


def _group_weights(ls):
    mx = functools.reduce(jnp.maximum, ls)
    es = [jnp.exp(l - mx) for l in ls]
    tot = functools.reduce(lambda a, b: a + b, es)
    return [e / tot for e in es]


def _expand_heads(w):
    tm = w.shape[0]
    first = lax.broadcasted_iota(jnp.int32, (1, LANES), 1) < HEAD_DIM
    cols = [jnp.broadcast_to(w[:, STAT_STRIDE * h:STAT_STRIDE * h + 1], (tm, LANES)) for h in range(HEADS_PER_GROUP)]
    return jnp.concatenate([jnp.where(first, cols[2 * p], cols[2 * p + 1]) for p in range(HEADS_PER_GROUP // 2)], axis=1)


def _head_sums(r):
    width = r.shape[1]
    feat_head = lax.broadcasted_iota(jnp.int32, (width, LANES), 0) // HEAD_DIM
    stat_head = lax.broadcasted_iota(jnp.int32, (width, LANES), 1) // STAT_STRIDE
    ones = jnp.where(feat_head == stat_head, 1.0, 0.0).astype(BF16)
    hi = r.astype(BF16)
    lo = (r - hi.astype(F32)).astype(BF16)
    return _dot(hi, ones) + _dot(lo, ones)


def _mix_weights(l_refs, dils, tm, stage):
    ls = [_merge_residues(lambda r, ref=ref: ref[r], d, tm, LANES, stage) for ref, d in zip(l_refs, dils)]
    return [_expand_heads(w) for w in _group_weights(ls)]


def attn_mix_out(os_, ls, dils, x, vec, w_o):
    T, D = x.shape
    GW = GROUP_WIDTH
    tm = min(TOKEN_TILE, T)
    ng = len(os_)

    def body(*refs):
        o_refs, l_refs = refs[:ng], refs[ng:2 * ng]
        x_ref, vec_ref, w_ref, xn_ref, mix_ref, y_ref, stage = refs[2 * ng:]
        natural = lambda ref, d: _merge_residues(lambda r: ref[r].astype(F32), d, tm, GW, stage)
        ws = _mix_weights(l_refs, dils, tm, stage)
        mixed = functools.reduce(lambda a, b: a + b, [w * natural(r, d) for w, r, d in zip(ws, o_refs, dils)])
        mb = mixed.astype(BF16)
        mix_ref[...] = mb
        y = _dot(mb, w_ref[...])
        y_ref[...] = y.astype(BF16)
        xn_ref[...] = x_ref[...] + (1.0 + vec_ref[3:4]) * y

    res = [_residue_spec(d, tm) for d in dils]
    stat = [_residue_spec(d, tm, LANES) for d in dils]
    return pl.pallas_call(
        body, grid=(T // tm,),
        in_specs=res + stat + [_tok(tm, D), _res((8, D)), _res((GW, D))],
        out_specs=[_tok(tm, D), _tok(tm, GW), _tok(tm, D)],
        out_shape=[_sds((T, D), F32), _sds((T, GW), BF16), _sds((T, D), BF16)],
        scratch_shapes=[_stage_scratch(tm)],
        compiler_params=_cp("arbitrary"), name="attn_mix_out")(*os_, *ls, x, vec, w_o)


def attn_mix_bwd(dxo, y, vec, w_o, os_, ls, dils):
    T, D = dxo.shape
    GW = GROUP_WIDTH
    tm = min(TOKEN_TILE, T)
    ng = len(os_)

    def body(*refs):
        dxo_ref, y_ref, vec_ref, w_ref = refs[:4]
        o_refs, l_refs = refs[4:4 + ng], refs[4 + ng:4 + 2 * ng]
        dy_ref = refs[4 + 2 * ng]
        do_refs = refs[5 + 2 * ng:5 + 3 * ng]
        d_refs = refs[5 + 3 * ng:5 + 4 * ng]
        part_ref, stage = refs[5 + 4 * ng], refs[6 + 4 * ng]
        natural = lambda ref, d: _merge_residues(lambda r: ref[r].astype(F32), d, tm, GW, stage)
        dxo_t = dxo_ref[...]
        dyb = (dxo_t * (1.0 + vec_ref[3:4])).astype(BF16)
        dy_ref[...] = dyb
        dgate = jnp.sum(dxo_t * y_ref[...].astype(F32), axis=0, keepdims=True)
        _acc_rows(part_ref, _rows8([dgate], D), pl.program_id(0) == 0)
        dmix = _dot_nt(dyb, w_ref[...])
        ws = _mix_weights(l_refs, dils, tm, stage)
        mixed = functools.reduce(lambda a, b: a + b, [w * natural(r, d) for w, r, d in zip(ws, o_refs, dils)])
        for gi in range(ng):
            do = ws[gi] * dmix
            for r, rows in enumerate(_split_residues(do, dils[gi], stage)):
                do_refs[gi][r] = rows.astype(BF16)
            for r, rows in enumerate(_split_residues(_head_sums(do * mixed), dils[gi], stage)):
                d_refs[gi][r] = rows

    res = [_residue_spec(d, tm) for d in dils]
    stat = [_residue_spec(d, tm, LANES) for d in dils]
    return pl.pallas_call(
        body, grid=(T // tm,),
        in_specs=[_tok(tm, D), _tok(tm, D), _res((8, D)), _res((GW, D))] + res + stat,
        out_specs=[_tok(tm, D)] + res + stat + [pl.BlockSpec((8, D), lambda i: (0, 0))],
        out_shape=[_sds((T, D), BF16)] + [_sds((d, T // d, GW), BF16) for d in dils]
        + [_sds((d, T // d, LANES), F32) for d in dils] + [_sds((8, D), F32)],
        scratch_shapes=[_stage_scratch(tm)],
        compiler_params=_cp("arbitrary"), name="attn_mix_bwd")(dxo, y, vec, w_o, *os_, *ls)


def final_loss(x, gvec, target):
    T, D = x.shape
    tm = min(TOKEN_TILE, T)

    def norm(xv, g):
        return xv * lax.rsqrt(jnp.mean(xv * xv, axis=-1, keepdims=True) + NORM_EPS) * g

    def body(x_ref, g_ref, t_ref, dx_ref, part_ref, loss_ref):
        first = pl.program_id(0) == 0
        yv, vjp = jax.vjp(norm, x_ref[...], g_ref[0:1])
        err = yv - t_ref[...]
        dx, dg = vjp(err * (1.0 / D))
        dx_ref[...] = dx
        _acc_rows(part_ref, _rows8([dg], D), first)
        tile_loss = 0.5 * jnp.sum(jnp.sum(err * err, axis=1, keepdims=True) * (1.0 / D), axis=0, keepdims=True)
        _acc_rows(loss_ref, jnp.broadcast_to(tile_loss, (8, LANES)), first)

    return pl.pallas_call(
        body, grid=(T // tm,),
        in_specs=[_tok(tm, D), _res((8, D)), _tok(tm, D)],
        out_specs=[_tok(tm, D), pl.BlockSpec((8, D), lambda i: (0, 0)), pl.BlockSpec((8, LANES), lambda i: (0, 0))],
        out_shape=[_sds((T, D), F32), _sds((8, D), F32), _sds((8, LANES), F32)],
        compiler_params=_cp("arbitrary"), name="final_loss")(x, gvec, target)


def mods_project(c_all, w, b):
    B, D = c_all.shape
    L, _, N = w.shape

    def body(c_ref, w_ref, b_ref, o_ref):
        cv = c_ref[...]
        cond = cv * _sigmoid(cv)
        o_ref[0] = jnp.dot(cond, w_ref[0], preferred_element_type=F32, precision=lax.Precision.HIGHEST) + b_ref[0]

    return pl.pallas_call(
        body, grid=(L,),
        in_specs=[pl.BlockSpec((B, D), lambda l: (0, 0)), pl.BlockSpec((1, D, N), lambda l: (l, 0, 0)),
                  pl.BlockSpec((1, 1, N), lambda l: (l, 0, 0))],
        out_specs=pl.BlockSpec((1, B, N), lambda l: (l, 0, 0)),
        out_shape=_sds((L, B, N), F32),
        compiler_params=_cp("arbitrary"), name="mods_project")(c_all, w, b)


def mods_weight_grad(c_all, dm):
    B, D = c_all.shape
    L, _, N = dm.shape

    def body(c_ref, d_ref, o_ref):
        cv = c_ref[...]
        cond = cv * _sigmoid(cv)
        o_ref[0] = lax.dot_general(cond, d_ref[0], (((0,), (0,)), ((), ())), preferred_element_type=F32,
                                   precision=lax.Precision.HIGHEST)

    return pl.pallas_call(
        body, grid=(L,),
        in_specs=[pl.BlockSpec((B, D), lambda l: (0, 0)), pl.BlockSpec((1, B, N), lambda l: (l, 0, 0))],
        out_specs=pl.BlockSpec((1, D, N), lambda l: (l, 0, 0)),
        out_shape=_sds((L, D, N), F32),
        compiler_params=_cp("arbitrary"), name="mods_weight_grad")(c_all, dm)


def _adam_math(g, w, m, v):
    m2 = ADAM_B1 * m + (1.0 - ADAM_B1) * g
    v2 = ADAM_B2 * v + (1.0 - ADAM_B2) * (g * g)
    m_hat = m2 / (1.0 - ADAM_B1 ** ADAM_STEP)
    v_hat = v2 / (1.0 - ADAM_B2 ** ADAM_STEP)
    delta = -ADAM_LR * (m_hat / (jnp.sqrt(v_hat) + ADAM_EPS) + ADAM_WD * w)
    return delta, m2, v2


def adam_update(g, w, m, v, parts, name):
    R, C = w.shape
    tr = _pick(R, 256, 8)

    def body(g_ref, w_ref, m_ref, v_ref, go_ref, d_ref, mo_ref, vo_ref):
        if parts:
            gv = g_ref[0].astype(F32)
            for s in range(1, N_DEV):
                gv = gv + g_ref[s].astype(F32)
        else:
            gv = g_ref[...]
        go_ref[...] = gv
        d_ref[...], mo_ref[...], vo_ref[...] = _adam_math(gv, w_ref[...], m_ref[...], v_ref[...])

    gspec = pl.BlockSpec((N_DEV, tr, C), lambda i: (0, i, 0)) if parts else _tok(tr, C)
    return pl.pallas_call(
        body, grid=(R // tr,),
        in_specs=[gspec, _tok(tr, C), _tok(tr, C), _tok(tr, C)],
        out_specs=[_tok(tr, C)] * 4, out_shape=[_sds((R, C), F32)] * 4,
        compiler_params=_cp("arbitrary"), name=name)(g, w, m, v)


def adam_layer(parts, w, m, v, prev, layer, after, name):
    L, R, C = w.shape
    tr = _pick(R, 256, 8)
    prev = (list(prev) if prev is not None else []) + [after]

    def body(p_ref, w_ref, m_ref, v_ref, *rest):
        go_ref, d_ref, mo_ref, vo_ref = rest[-4:]
        gv = p_ref[0].astype(F32)
        for s in range(1, N_DEV):
            gv = gv + p_ref[s].astype(F32)
        go_ref[...] = gv
        d_ref[...], mo_ref[...], vo_ref[...] = _adam_math(gv, w_ref[...], m_ref[...], v_ref[...])

    lay = pl.BlockSpec((None, tr, C), lambda i: (layer, i, 0))
    return pl.pallas_call(
        body, grid=(R // tr,),
        in_specs=[pl.BlockSpec((N_DEV, tr, C), lambda i: (0, i, 0)), lay, lay, lay] + [pl.BlockSpec(memory_space=pl.ANY)] * len(prev),
        out_specs=[lay] * 4, out_shape=[_sds((L, R, C), F32)] * 4,
        input_output_aliases={4 + k: k for k in range(len(prev) - 1)},
        compiler_params=_cp("arbitrary"), name=name)(parts, w, m, v, *prev)


def _my_id():
    return 4 * lax.axis_index("x") + 2 * lax.axis_index("y") + lax.axis_index("c")


def _peer(s):
    x, y, c = lax.axis_index("x"), lax.axis_index("y"), lax.axis_index("c")
    px = (1 - x) if s & 4 else x
    py = (1 - y) if s & 2 else y
    pc = (1 - c) if s & 1 else c
    return (px, py, pc), 4 * px + 2 * py + pc


def all_gather(xs, space, name):
    na = len(xs)

    def body(*refs):
        x_refs, o_refs = refs[:na], refs[na:2 * na]
        send_sems, recv_sems, local_sems = refs[2 * na:]
        me = _my_id()
        locals_, sends = [], []
        for a in range(na):
            cp = pltpu.make_async_copy(x_refs[a], o_refs[a].at[me], local_sems.at[a])
            cp.start()
            locals_.append(cp)
        for s in range(1, N_DEV):
            peer, _ = _peer(s)
            for a in range(na):
                cp = pltpu.make_async_remote_copy(
                    src_ref=x_refs[a], dst_ref=o_refs[a].at[me], send_sem=send_sems.at[a, s - 1],
                    recv_sem=recv_sems.at[a, s - 1], device_id=peer, device_id_type=MESH)
                cp.start()
                sends.append(cp)
        for s in range(1, N_DEV):
            peer, pid = _peer(s)
            for a in range(na):
                pltpu.make_async_remote_copy(
                    src_ref=x_refs[a], dst_ref=o_refs[a].at[pid], send_sem=send_sems.at[a, s - 1],
                    recv_sem=recv_sems.at[a, s - 1], device_id=peer, device_id_type=MESH).wait_recv()
        for cp in sends:
            cp.wait_send()
        for cp in locals_:
            cp.wait()

    spec = pl.BlockSpec(memory_space=space)
    return pl.pallas_call(
        body, in_specs=[spec] * na, out_specs=[spec] * na,
        out_shape=[_sds((N_DEV,) + x.shape, x.dtype) for x in xs],
        scratch_shapes=[pltpu.SemaphoreType.DMA((na, N_DEV - 1)), pltpu.SemaphoreType.DMA((na, N_DEV - 1)),
                        pltpu.SemaphoreType.DMA((na,))],
        compiler_params=pltpu.CompilerParams(vmem_limit_bytes=VMEM_LIMIT), name=name)(*xs)


def exchange_slots(xs, name):
    na = len(xs)

    def body(*refs):
        x_refs, o_refs = refs[:na], refs[na:2 * na]
        send_sems, recv_sems, local_sems = refs[2 * na:]
        me = _my_id()
        locals_, sends = [], []
        for a in range(na):
            cp = pltpu.make_async_copy(x_refs[a].at[me], o_refs[a].at[me], local_sems.at[a])
            cp.start()
            locals_.append(cp)
        for s in range(1, N_DEV):
            peer, pid = _peer(s)
            for a in range(na):
                cp = pltpu.make_async_remote_copy(
                    src_ref=x_refs[a].at[pid], dst_ref=o_refs[a].at[me], send_sem=send_sems.at[a, s - 1],
                    recv_sem=recv_sems.at[a, s - 1], device_id=peer, device_id_type=MESH)
                cp.start()
                sends.append(cp)
        for s in range(1, N_DEV):
            peer, pid = _peer(s)
            for a in range(na):
                pltpu.make_async_remote_copy(
                    src_ref=x_refs[a].at[pid], dst_ref=o_refs[a].at[pid], send_sem=send_sems.at[a, s - 1],
                    recv_sem=recv_sems.at[a, s - 1], device_id=peer, device_id_type=MESH).wait_recv()
        for cp in sends:
            cp.wait_send()
        for cp in locals_:
            cp.wait()

    spec = pl.BlockSpec(memory_space=pl.ANY)
    return pl.pallas_call(
        body, in_specs=[spec] * na, out_specs=[spec] * na,
        out_shape=[_sds(x.shape, x.dtype) for x in xs],
        scratch_shapes=[pltpu.SemaphoreType.DMA((na, N_DEV - 1)), pltpu.SemaphoreType.DMA((na, N_DEV - 1)),
                        pltpu.SemaphoreType.DMA((na,))],
        compiler_params=pltpu.CompilerParams(vmem_limit_bytes=VMEM_LIMIT), name=name)(*xs)


_HBM = pl.BlockSpec(memory_space=pltpu.HBM)
_SEM = pl.BlockSpec(memory_space=pltpu.SEMAPHORE)
_EFFECT = pltpu.SideEffectType.DATAFLOW_SIDE_EFFECTING


def _split_copies(pattern, x_ref, land_ref, send_sem, recv_sem):
    me = _my_id()
    if pattern in ("gather", "scatter"):
        plan = []
        for s in range(1, N_DEV):
            peer, pid = _peer(s)
            plan.append((x_ref.at[pid] if pattern == "scatter" else x_ref, land_ref.at[me], peer))
    elif pattern == "to_chips":
        plan = [(x_ref, land_ref.at[me], _peer(s)[0]) for s in (1, 2, 4, 6)]
    else:
        sibling = _peer(1)[0]
        plan = [(land_ref.at[_peer(s)[1]], land_ref.at[_peer(s)[1]], sibling) for s in (2, 4, 6)]
    return [pltpu.make_async_remote_copy(src_ref=src, dst_ref=dst, send_sem=send_sem, recv_sem=recv_sem,
                                         device_id=dev, device_id_type=MESH) for src, dst, dev in plan]


def comm_start(xs, pattern, after, name, lands=None):
    na = len(xs)
    extra = [] if after is None else [after]
    me = _my_id()
    if lands is None:
        lands = []
        for x in xs:
            shape = x.shape if pattern == "scatter" else (N_DEV,) + x.shape
            own = lax.dynamic_slice_in_dim(x, me, 1, 0) if pattern == "scatter" else x[None]
            lands.append(lax.dynamic_update_slice(lax.empty(shape, x.dtype), own, (me,) + (0,) * (len(shape) - 1)))

    def body(*refs):
        x_refs, land_refs = refs[:na], refs[na:2 * na]
        send_sem, recv_sem = refs[2 * na + len(extra)], refs[2 * na + len(extra) + 1]
        token = refs[-1]
        for a in range(na):
            for cp in _split_copies(pattern, x_refs[a], land_refs[a], send_sem, recv_sem):
                cp.start()
        token[...] = jnp.zeros_like(token)

    outs = pl.pallas_call(
        body, name=name,
        out_shape=(pltpu.SemaphoreType.DMA(()), pltpu.SemaphoreType.DMA(()))
        + tuple(pltpu.HBM(x.shape, x.dtype) for x in xs) + tuple(pltpu.HBM(l.shape, l.dtype) for l in lands)
        + (_sds((8, LANES), F32),),
        in_specs=(_HBM,) * (2 * na) + (pl.BlockSpec(memory_space=pl.ANY),) * len(extra),
        out_specs=(_SEM, _SEM) + (_HBM,) * (2 * na) + (pl.BlockSpec(memory_space=pltpu.VMEM),),
        input_output_aliases={a: 2 + a for a in range(2 * na)},
        compiler_params=pltpu.CompilerParams(has_side_effects=_EFFECT),
    )(*[pltpu.with_memory_space_constraint(x, pltpu.HBM) for x in xs],
      *[pltpu.with_memory_space_constraint(l, pltpu.HBM) for l in lands], *extra)
    return dict(sems=outs[0:2], xs=outs[2:2 + na], lands=outs[2 + na:2 + 2 * na], token=outs[-1], pattern=pattern)


def comm_wait(started, after, name, with_xs=False):
    xs, lands = started["xs"], started["lands"]
    pattern = started["pattern"]
    na = len(xs)

    def body(*refs):
        x_refs, land_refs = refs[:na], refs[na:2 * na]
        send_sem, recv_sem = refs[2 * na], refs[2 * na + 1]
        for a in range(na):
            for cp in _split_copies(pattern, x_refs[a], land_refs[a], send_sem, recv_sem):
                cp.wait_send()
                cp.wait_recv()

    outs = pl.pallas_call(
        body, name=name,
        out_shape=tuple(pltpu.HBM(x.shape, x.dtype) for x in xs) + tuple(pltpu.HBM(l.shape, l.dtype) for l in lands),
        in_specs=(_HBM,) * (2 * na) + (_SEM, _SEM, pl.BlockSpec(memory_space=pl.ANY)),
        out_specs=(_HBM,) * (2 * na),
        input_output_aliases={a: a for a in range(2 * na)},
        compiler_params=pltpu.CompilerParams(has_side_effects=_EFFECT),
    )(*xs, *lands, *started["sems"], after)
    return (list(outs[na:]), list(outs[:na])) if with_xs else list(outs[na:])


def _cols_to_natural(g):
    return jnp.concatenate([g[k] for k in range(N_DEV)], axis=1)


def _vec8(rows, d):
    rows = [r.reshape(1, d).astype(F32) for r in rows]
    return jnp.concatenate(rows + [jnp.zeros((8 - len(rows), d), F32)], axis=0)


def _ffn_forward(x, vec, w_in_t, w_out):
    xn, h, a, b, u, y = ffn_fwd(x, vec, w_in_t, w_out)
    return xn, (x, h, a, b, u, y)


def _ffn_backward(dxo, saved, vec, w_in_t, w_out, on_rows=None):
    x, h, a, b, u, y = saved
    dy, dab, dx, part = ffn_bwd(dxo, y, vec, w_out, w_in_t, a, b, x)
    rows = part[0:4]
    token = on_rows(rows) if on_rows is not None else None
    g_out = grad_slots(u, dy, "ffn_dw_out", after=token)
    g_in_t = grad_slots(dab, h, "ffn_dw_in", after=token)
    return dx, g_in_t, g_out, rows


_TRANSPOSED = ("ffn1_w_in", "ffn2_w_in", "attn_w_q")
_COL_NATURAL = ("conv_w_in", "w_kv", "attn_w_o")
_ROW_SHARDED = ("ffn1_w_out", "ffn2_w_out", "conv_w_out")
_BIG = _TRANSPOSED + _COL_NATURAL + _ROW_SHARDED


def weight_chunks():
    chunks = []
    for layer in range(DEPTH):
        first = [("ffn1_w_in", layer), ("ffn1_w_out", layer)]
        if layer == N_A_LAYERS:
            first = [("w_kv", layer)] + first
        mixer = [("conv_w_in", layer), ("conv_w_out", layer)] if layer < N_A_LAYERS else [("attn_w_q", layer), ("attn_w_o", layer)]
        rest = mixer + [("ffn2_w_in", layer), ("ffn2_w_out", layer)]
        chunks += [first, rest] if layer == 0 else [first + rest]
    return chunks


def stacked_index(name, layer):
    if name == "w_kv":
        return None
    return layer - N_A_LAYERS if name.startswith("attn") else layer


class ChunkComm:
    def __init__(self, shards):
        self.shards = shards
        self.chunks = weight_chunks()

    def _shard(self, name, layer):
        idx = stacked_index(name, layer)
        return self.shards[name][0 if idx is None else idx]

    def start_gather(self, ci, after):
        xs = [self._shard(n, l).astype(BF16) for n, l in self.chunks[ci]]
        return comm_start(xs, "to_chips", after, f"gather_start_{ci}")

    def relay_gather(self, ci, started, after):
        lands, xs = comm_wait(started, after, f"gather_wait_{ci}", with_xs=True)
        return comm_start(xs, "relay", None, f"gather_relay_{ci}", lands=lands)

    def finish_gather(self, ci, relayed, after):
        lands = comm_wait(relayed, after, f"gather_done_{ci}")
        W = {}
        for key, g in zip(self.chunks[ci], lands):
            W[key] = _cols_to_natural(g) if key[0] in _COL_NATURAL else g.reshape(-1, g.shape[2])
        return W

    def start_exchange(self, ci, slots, after):
        return comm_start([slots[key] for key in self.chunks[ci]], "scatter", after, f"exchange_start_{ci}")

    def finish_exchange(self, ci, started, after):
        lands = comm_wait(started, after, f"exchange_wait_{ci}")
        return dict(zip(self.chunks[ci], lands))


def device_step(x, positions, target, mods, kvmods, small, comm, gather0):
    T, D = x.shape
    groups = DILATED_GROUPS
    dils = [dil for _, dil in groups]
    lane = jnp.arange(LANES) % HEAD_DIM
    inv = ROPE_THETA ** (-jnp.arange(0, ROPE_DIM, 2, dtype=F32) / ROPE_DIM)
    lane_rows = _vec8([jnp.where(lane < ROPE_DIM, inv[lane % (ROPE_DIM // 2)], 0.0), lane < ROPE_DIM,
                       (lane >= ROPE_DIM // 2) & (lane < ROPE_DIM), lane < ROPE_DIM // 2], LANES)
    tabs = rope_tables(positions.reshape(T, 1), lane_rows)

    def after_token(v, token):
        return v if token is None else v + token[0, 0]

    def vec_of(layer, sub):
        return _vec8([small["norm_g"][layer, sub], mods[layer, 3 * sub], mods[layer, 3 * sub + 1], mods[layer, 3 * sub + 2]], D)

    saved = []
    kv_saved = None
    k_sh = v_sh = None
    qw = GROUP_WIDTH * len(groups)
    chunk_of = {key: ci for ci, chunk in enumerate(comm.chunks) for key in chunk}
    W = {}
    flight = {"ci": 0, "started": gather0}

    relayed = {}

    def advance(after):
        ci = flight["ci"]
        if flight["started"] is None or ci in relayed:
            return None
        relayed[ci] = comm.relay_gather(ci, flight["started"], after)
        relayed[ci]["behind"] = relayed[ci]["token"]
        nxt = comm.start_gather(ci + 1, relayed[ci]["token"]) if ci + 1 < len(comm.chunks) else None
        flight.update(ci=ci + 1, started=nxt)
        if nxt is not None:
            relayed[ci]["behind"] = nxt["token"]
        return relayed[ci]["behind"]

    def need(key, after):
        if key not in W:
            ci = chunk_of[key]
            if ci not in relayed:
                assert ci == flight["ci"], (key, ci)
                advance(after)
            W.update(comm.finish_gather(ci, relayed[ci], relayed[ci]["behind"]))
        return W[key]

    for layer in range(DEPTH):
        if layer == N_A_LAYERS:
            w_kv = need(("w_kv", layer), x)
            kv_vec = _vec8([small["kv_norm_g"], kvmods[0], kvmods[1]], D)
            h_kv, *kv_pieces = proj_rope_fwd(x, kv_vec, w_kv, tabs, qw, False, dils, "kv_fwd")
            k_sh, v_sh = kv_pieces[:len(groups)], kv_pieces[len(groups):]
            kv_saved = (x, h_kv, kv_vec)
        rec = {}
        behind = tabs[0] if layer == 0 else x
        w_in, w_out = need(("ffn1_w_in", layer), behind), need(("ffn1_w_out", layer), behind)
        v1 = vec_of(layer, 0)
        x, rec["ffn1"] = _ffn_forward(x, v1, w_in, w_out)
        if layer < N_A_LAYERS:
            w_in, w_out = need(("conv_w_in", layer), x), need(("conv_w_out", layer), x)
            v2 = vec_of(layer, 1)
            cw = _vec8(list(small["conv_w"][layer]), D)
            x_in = x
            x, h, bcu, cv, z, y = conv_fwd(x, v2, cw, w_in, w_out)
            rec["mix"] = (x_in, h, bcu, cv, z, y, cw)
        else:
            w_q, w_o = need(("attn_w_q", layer), x), need(("attn_w_o", layer), x)
            v2 = vec_of(layer, 1)
            x_in = x
            h, *q = proj_rope_fwd(x, v2, w_q, tabs, qw, True, dils, "q_fwd")
            os_, ls = [], []
            for g, (win, dil) in enumerate(groups):
                o, l = attn_core_fwd(q[g], k_sh[g], v_sh[g], g, win // dil)
                os_.append(o)
                ls.append(l)
            x, mixed, y = attn_mix_out(os_, ls, dils, x, v2, w_o)
            rec["mix"] = (x_in, h, q, os_, ls, mixed, y)
        token = advance(x) if layer >= 1 else None
        w_in, w_out = need(("ffn2_w_in", layer), x), need(("ffn2_w_out", layer), x)
        v3 = after_token(vec_of(layer, 2), token)
        x, rec["ffn2"] = _ffn_forward(x, v3, w_in, w_out)
        rec["vecs"] = (v1, v2, v3)
        saved.append(rec)

    dx, part_final, loss_tile = final_loss(x, _vec8([small["final_norm_g"]], D), target)
    loss = loss_tile[0, 0]

    conv_rows = [None] * N_A_LAYERS
    kv_rows = None
    mod_rows = [[None] * 3 for _ in range(DEPTH)]
    dkv_sums = [None for _ in groups]
    slots = {}
    exchanges = []
    token = None

    def send_ready_chunks():
        nonlocal token
        for ci in reversed(range(len(comm.chunks))):
            if ci not in [e[0] for e in exchanges] and all(key in slots for key in comm.chunks[ci]):
                started = comm.start_exchange(ci, slots, token)
                exchanges.append((ci, started))
                token = started["token"]

    vector_gather = {}

    def start_vector_gather(rows0):
        mod_rows[0][0] = rows0
        rows = jnp.stack([jnp.stack(r) for r in mod_rows])
        vecs = jnp.concatenate([rows[:, :, 1:4].reshape(-1), kv_rows[1:3].reshape(-1), kv_rows[0], part_final[0],
                                rows[:, :, 0].reshape(-1), jnp.stack(conv_rows).reshape(-1)])
        vector_gather["count"] = vecs.shape[0]
        vecs = _pad_rows(vecs.reshape(-1, 1), 8 * LANES).reshape(-1, LANES)
        vector_gather["started"] = comm_start([vecs], "gather", None, "vector_grads_start")
        return vector_gather["started"]["token"]

    for layer in reversed(range(DEPTH)):
        rec = saved[layer]
        v1, v2, v3 = rec["vecs"]
        dx, slots[("ffn2_w_in", layer)], slots[("ffn2_w_out", layer)], mod_rows[layer][2] = _ffn_backward(
            dx, rec["ffn2"], after_token(v3, token), W[("ffn2_w_in", layer)], W[("ffn2_w_out", layer)])
        if layer < N_A_LAYERS:
            x_in, h, bcu, cv, z, y, cw = rec["mix"]
            dx, dy, dbcu, part, dcw = conv_bwd(dx, x_in, y, bcu, cv, v2, cw, W[("conv_w_in", layer)], W[("conv_w_out", layer)])
            slots[("conv_w_out", layer)] = grad_slots(z, dy, "conv_dw_out")
            slots[("conv_w_in", layer)] = grad_slots(h, dbcu, "conv_dw_in", col_slots=True)
            conv_rows[layer] = dcw[0:3]
            mod_rows[layer][1] = part[0:4]
        else:
            x_in, h, q, os_, ls, mixed, y = rec["mix"]
            outs = attn_mix_bwd(dx, y, v2, W[("attn_w_o", layer)], os_, ls, dils)
            ng = len(groups)
            dy, dos, deltas, part_gate = outs[0], outs[1:1 + ng], outs[1 + ng:1 + 2 * ng], outs[1 + 2 * ng]
            slots[("attn_w_o", layer)] = grad_slots(mixed, dy, "attn_dw_o", col_slots=True)
            dqs = []
            for g, (win, dil) in enumerate(groups):
                dq, *dkv_sums[g] = attn_core_bwd(q[g], k_sh[g], v_sh[g], dos[g], deltas[g], ls[g], g, win // dil, dkv_sums[g])
                dqs.append(dq)
            dx, dqr, part_norm = proj_rope_bwd(dqs, dils, x_in, dx, v2, W[("attn_w_q", layer)], tabs, qw, True, "q_bwd")
            slots[("attn_w_q", layer)] = grad_slots(dqr, h, "attn_dw_q")
            mod_rows[layer][1] = jnp.concatenate([part_norm[0:3], part_gate[0:1]], axis=0)
        send_ready_chunks()
        dx, slots[("ffn1_w_in", layer)], slots[("ffn1_w_out", layer)], mod_rows[layer][0] = _ffn_backward(
            dx, rec["ffn1"], after_token(v1, token), W[("ffn1_w_in", layer)], W[("ffn1_w_out", layer)],
            on_rows=start_vector_gather if layer == 0 else None)
        if layer == N_A_LAYERS:
            x_kv, h_kv, kv_vec = kv_saved
            dparts = [dk for dk, _ in dkv_sums] + [dv for _, dv in dkv_sums]
            dx, dkvp, part_kv = proj_rope_bwd(dparts, dils, x_kv, dx, kv_vec, W[("w_kv", layer)], tabs, qw, False, "kv_bwd")
            slots[("w_kv", layer)] = grad_slots(h_kv, dkvp, "kv_dw", col_slots=True)
            kv_rows = part_kv[0:3]
        send_ready_chunks()

    return loss, dx, {"exchanges": exchanges, "vector_gather": vector_gather}


def _flat2(a):
    return a.reshape(-1, a.shape[-1])


def _pad_rows(a, mult):
    r = a.shape[0]
    pad = (-r) % mult
    return a if pad == 0 else jnp.concatenate([a, jnp.zeros((pad,) + a.shape[1:], a.dtype)], axis=0)


def kernel(x, c, positions, norm_g, ada_w, ada_b, ffn1_w_in, ffn1_w_out, ffn2_w_in, ffn2_w_out, conv_w_in, conv_w, conv_w_out, kv_norm_g, kv_ada_w, kv_ada_b, w_kv, attn_w_q, attn_w_o, final_norm_g, loss_target, m_norm_g, m_ada_w, m_ada_b, m_ffn1_w_in, m_ffn1_w_out, m_ffn2_w_in, m_ffn2_w_out, m_conv_w_in, m_conv_w, m_conv_w_out, m_kv_norm_g, m_kv_ada_w, m_kv_ada_b, m_w_kv, m_attn_w_q, m_attn_w_o, m_final_norm_g, v_norm_g, v_ada_w, v_ada_b, v_ffn1_w_in, v_ffn1_w_out, v_ffn2_w_in, v_ffn2_w_out, v_conv_w_in, v_conv_w, v_conv_w_out, v_kv_norm_g, v_kv_ada_w, v_kv_ada_b, v_w_kv, v_attn_w_q, v_attn_w_o, v_final_norm_g):
    names = ("norm_g", "ada_w", "ada_b", "ffn1_w_in", "ffn1_w_out", "ffn2_w_in", "ffn2_w_out", "conv_w_in", "conv_w",
             "conv_w_out", "kv_norm_g", "kv_ada_w", "kv_ada_b", "w_kv", "attn_w_q", "attn_w_o", "final_norm_g")
    wts = dict(zip(names, (norm_g, ada_w, ada_b, ffn1_w_in, ffn1_w_out, ffn2_w_in, ffn2_w_out, conv_w_in, conv_w, conv_w_out,
                           kv_norm_g, kv_ada_w, kv_ada_b, w_kv, attn_w_q, attn_w_o, final_norm_g)))
    mom = dict(zip(names, (m_norm_g, m_ada_w, m_ada_b, m_ffn1_w_in, m_ffn1_w_out, m_ffn2_w_in, m_ffn2_w_out, m_conv_w_in,
                           m_conv_w, m_conv_w_out, m_kv_norm_g, m_kv_ada_w, m_kv_ada_b, m_w_kv, m_attn_w_q, m_attn_w_o,
                           m_final_norm_g)))
    var = dict(zip(names, (v_norm_g, v_ada_w, v_ada_b, v_ffn1_w_in, v_ffn1_w_out, v_ffn2_w_in, v_ffn2_w_out, v_conv_w_in,
                           v_conv_w, v_conv_w_out, v_kv_norm_g, v_kv_ada_w, v_kv_ada_b, v_w_kv, v_attn_w_q, v_attn_w_o,
                           v_final_norm_g)))
    T, D = x.shape[1], x.shape[2]
    me = _my_id()
    nmod = ada_w.shape[2]
    nkv = kv_ada_w.shape[1]

    def stacked(w, n):
        w = w if w.ndim == 3 else w[None]
        return jnp.swapaxes(w, 1, 2) if n in _TRANSPOSED else w

    comm = ChunkComm({n: stacked(wts[n], n) for n in _BIG})
    W = {}

    ds = norm_g.shape[2]
    small = jnp.concatenate([c.reshape(-1), norm_g.reshape(-1), conv_w.reshape(-1)]).astype(F32)
    n_small = small.shape[0]
    small = _pad_rows(small.reshape(-1, 1), 8 * LANES).reshape(-1, LANES)
    (small_all,) = all_gather([small], pltpu.VMEM, "gather_small")
    small_all = small_all.reshape(N_DEV, -1)[:, :n_small]
    c_all = small_all[:, :D]
    def full_rows(off, count):
        return jnp.stack([small_all[:, off + i * ds:off + (i + 1) * ds].reshape(D) for i in range(count)])

    W["norm_g"] = full_rows(D, DEPTH * 3).reshape(DEPTH, 3, D)
    W["conv_w"] = full_rows(D + DEPTH * 3 * ds, N_A_LAYERS * 3).reshape(N_A_LAYERS, 3, D)
    W["kv_norm_g"], W["final_norm_g"] = kv_norm_g, final_norm_g

    ada_b_mine = lax.dynamic_slice_in_dim(ada_b, me * nmod, nmod, axis=1).reshape(DEPTH, 1, nmod)
    kv_b_mine = lax.dynamic_slice_in_dim(kv_ada_b, me * nkv, nkv, axis=0).reshape(1, 1, nkv)
    mods_cols = mods_project(c_all, ada_w, ada_b_mine)
    kv_cols = mods_project(c_all, kv_ada_w.reshape(1, D, nkv), kv_b_mine)
    mcat = jnp.concatenate([mods_cols[l] for l in range(DEPTH)] + [kv_cols[0]], axis=1)
    wm = mcat.shape[1]
    if wm % LANES:
        mcat = jnp.concatenate([mcat, jnp.zeros((N_DEV, LANES - wm % LANES), F32)], axis=1)
    (mods_all,) = exchange_slots([mcat.reshape(N_DEV, 1, -1)], "exchange_mods")
    gather0 = comm.start_gather(0, mods_all)
    mods_all = mods_all.reshape(N_DEV, -1)
    mods = jnp.stack([mods_all[:, l * nmod:(l + 1) * nmod].reshape(N_MOD, D) for l in range(DEPTH)])
    kvmods = mods_all[:, DEPTH * nmod:DEPTH * nmod + nkv].reshape(2, D)

    loss_local, dx, grads = device_step(x[0], positions[0], loss_target[0], mods, kvmods, W, comm, gather0)
    loss = lax.psum(loss_local, MESH_AXES)

    (vec_all,) = comm_wait(grads["vector_gather"]["started"], grads["exchanges"][-1][1]["token"], "vector_grads_wait")
    vec_all = vec_all.reshape(N_DEV, -1)[:, :grads["vector_gather"]["count"]]
    nm_, nk_ = DEPTH * N_MOD * D, 2 * D
    dmods_all = vec_all[:, :nm_].reshape(N_DEV, DEPTH, N_MOD * D)
    dkvm_all = vec_all[:, nm_:nm_ + nk_]
    rest = vec_all[:, nm_ + nk_:]
    parts_kv_norm, parts_final = rest[:, :D].reshape(N_DEV, 1, D), rest[:, D:2 * D].reshape(N_DEV, 1, D)
    parts_norm = lax.dynamic_slice_in_dim(rest[:, 2 * D:2 * D + DEPTH * 3 * D].reshape(N_DEV, DEPTH * 3, D), me * ds, ds, axis=2)
    parts_conv = lax.dynamic_slice_in_dim(rest[:, 2 * D + DEPTH * 3 * D:].reshape(N_DEV, N_A_LAYERS * 3, D), me * ds, ds, axis=2)
    dm_cols = lax.dynamic_slice_in_dim(dmods_all, me * nmod, nmod, axis=2)
    dm_mine = jnp.stack([dm_cols[:, l] for l in range(DEPTH)])
    dkv_mine = lax.dynamic_slice_in_dim(dkvm_all, me * nkv, nkv, axis=1).reshape(1, N_DEV, nkv)
    g_ada_w = mods_weight_grad(c_all, dm_mine)
    g_kv_ada_w = mods_weight_grad(c_all, dkv_mine)[0]

    out_g, out_d, out_m, out_v = {}, {}, {}, {}

    def update(n, g, w, parts=False):
        shp = w.shape
        w2 = w.reshape(1, -1) if w.ndim == 1 else _flat2(w)
        g2 = g if parts else g.reshape(w2.shape)
        res = adam_update(g2, w2, mom[n].reshape(w2.shape), var[n].reshape(w2.shape), parts, "adam_" + n)
        out_g[n], out_d[n], out_m[n], out_v[n] = (r.reshape(shp) for r in res)

    moms = {n: stacked(mom[n], n) for n in _BIG}
    vars_ = {n: stacked(var[n], n) for n in _BIG}
    results = {}
    after = dx
    for ci, started in grads["exchanges"]:
        for (n, layer), parts in comm.finish_exchange(ci, started, after).items():
            idx = stacked_index(n, layer)
            results[n] = adam_layer(parts, comm.shards[n], moms[n], vars_[n], results.get(n), 0 if idx is None else idx,
                                    after, f"adam_{n}_{layer}")
            after = results[n][1]
    for n in _BIG:
        res = [jnp.swapaxes(r, 1, 2) if n in _TRANSPOSED else r for r in results[n]]
        out_g[n], out_d[n], out_m[n], out_v[n] = (r.reshape(wts[n].shape) for r in res)
    update("ada_w", g_ada_w, ada_w)
    update("kv_ada_w", g_kv_ada_w, kv_ada_w)
    update("ada_b", dmods_all, ada_b, True)
    update("kv_ada_b", dkvm_all.reshape(N_DEV, 1, nk_), kv_ada_b, True)
    update("kv_norm_g", parts_kv_norm, kv_norm_g, True)
    update("final_norm_g", parts_final, final_norm_g, True)
    update("norm_g", parts_norm, norm_g, True)
    update("conv_w", parts_conv, conv_w, True)

    return (loss, dx.reshape(x.shape), *[out_g[n] for n in names], *[out_d[n] for n in names],
            *[out_m[n] for n in names], *[out_v[n] for n in names])
```

```python
import functools

import jax
import jax.numpy as jnp
from jax import lax
from jax.experimental import pallas as pl
from jax.experimental.pallas import tpu as pltpu

F32, BF16 = jnp.float32, jnp.bfloat16

N_DEV = 8
MESH_AXES = ("x", "y", "c")
DEPTH = 4
N_A_LAYERS = 2
HEAD_DIM = 64
HEADS_PER_GROUP = 8
GROUP_WIDTH = HEAD_DIM * HEADS_PER_GROUP
DILATED_GROUPS = ((128, 1), (512, 4), (2048, 16))
ROPE_DIM = HEAD_DIM // 4
ROPE_THETA = 500000.0
NORM_EPS = 1e-5
FFN_RES_WEIGHT = 0.5
N_MOD = 9
ADAM_LR, ADAM_B1, ADAM_B2, ADAM_EPS, ADAM_WD, ADAM_STEP = 0.001, 0.9, 0.999, 1e-08, 0.01, 10

LANES = 128
TOKEN_TILE = 512
FFN_BWD_TILE = 256
FWD_QUERY_BLOCKS = 8
BWD_QUERY_BLOCKS = 4
CONTRACT_TILE = 4096
GRAD_COLS = 768
MXU_WIDTH = 256
VMEM_LIMIT = 56 * 1024 * 1024
MESH = pl.DeviceIdType.MESH


def _cp(*sem):
    return pltpu.CompilerParams(dimension_semantics=sem, vmem_limit_bytes=VMEM_LIMIT)


def _pick(n, cap, mult=LANES):
    if n <= cap:
        return n
    best = None
    for t in range(mult, cap + 1, mult):
        if n % t == 0:
            best = t
    assert best is not None, (n, cap)
    return best


def _tok(tm, w):
    return pl.BlockSpec((tm, w), lambda i: (i, 0))


def _res(shape):
    nd = len(shape)
    return pl.BlockSpec(shape, lambda *_: (0,) * nd, pipeline_mode=pl.Buffered(1))


def _sds(shape, dt):
    return jax.ShapeDtypeStruct(shape, dt)


def _sigmoid(a):
    return 1.0 / (1.0 + jnp.exp(-a))


def _modnorm(x, g, sh, sc):
    r = lax.rsqrt(jnp.mean(x * x, axis=-1, keepdims=True) + NORM_EPS)
    return (x * r * g) * (1.0 + sc) + sh


def _dot(a, b):
    return jnp.dot(a, b, preferred_element_type=F32)


def _dot_nt(a, b):
    return lax.dot_general(a, b, (((1,), (1,)), ((), ())), preferred_element_type=F32)


def _dot_tn(a, b):
    return lax.dot_general(a, b, (((0,), (0,)), ((), ())), preferred_element_type=F32)


def _rows8(rows, d):
    pad = 8 - len(rows)
    return jnp.concatenate(list(rows) + [jnp.zeros((pad, d), F32)], axis=0)


def _acc_rows(ref, tile, first):
    @pl.when(first)
    def _():
        ref[...] = tile

    @pl.when(jnp.logical_not(first))
    def _():
        ref[...] += tile


def ffn_fwd(x, vec, w_in_t, w_out):
    T, D = x.shape
    F = w_in_t.shape[0] // 2
    tm, cw = min(TOKEN_TILE, T), _pick(F, MXU_WIDTH)

    def body(x_ref, vec_ref, wi_ref, wo_ref, xn_ref, h_ref, ga_ref, gb_ref, u_ref, y_ref):
        x_t = x_ref[...]
        hb = _modnorm(x_t, vec_ref[0:1], vec_ref[1:2], vec_ref[2:3]).astype(BF16)
        h_ref[...] = hb
        for c in range(F // cw):
            lo, hi = c * cw, (c + 1) * cw
            a = _dot_nt(hb, wi_ref[lo:hi, :])
            b = _dot_nt(hb, wi_ref[F + lo:F + hi, :])
            sg = _sigmoid(a)
            silu = a * sg
            ga_ref[:, lo:hi] = (b * (sg + silu * (1.0 - sg))).astype(BF16)
            gb_ref[:, lo:hi] = silu.astype(BF16)
            u_ref[:, lo:hi] = (silu * b).astype(BF16)
        y = _dot(u_ref[...], wo_ref[...])
        y_ref[...] = y.astype(BF16)
        xn_ref[...] = x_t + (FFN_RES_WEIGHT * (1.0 + vec_ref[3:4])) * y

    return pl.pallas_call(
        body, grid=(T // tm,),
        in_specs=[_tok(tm, D), _res((8, D)), _res((2 * F, D)), _res((F, D))],
        out_specs=[_tok(tm, D), _tok(tm, D), _tok(tm, F), _tok(tm, F), _tok(tm, F), _tok(tm, D)],
        out_shape=[_sds((T, D), F32), _sds((T, D), BF16), _sds((T, F), BF16), _sds((T, F), BF16), _sds((T, F), BF16),
                   _sds((T, D), BF16)],
        compiler_params=_cp("arbitrary"), name="ffn_fwd")(x, vec, w_in_t, w_out)


def ffn_bwd(dxo, y, vec, w_out, w_in_t, a, b, x):
    T, D = x.shape
    F = a.shape[1]
    tm, cw = min(FFN_BWD_TILE, T), _pick(F, MXU_WIDTH)

    def body(dxo_ref, y_ref, vec_ref, wo_ref, wi_ref, a_ref, b_ref, x_ref, dy_ref, dab_ref, dx_ref, part_ref):
        dxo_t = dxo_ref[...]
        dyb = (dxo_t * (FFN_RES_WEIGHT * (1.0 + vec_ref[3:4]))).astype(BF16)
        dy_ref[...] = dyb
        dgate = FFN_RES_WEIGHT * jnp.sum(dxo_t * y_ref[...].astype(F32), axis=0, keepdims=True)
        for c in range(F // cw):
            lo, hi = c * cw, (c + 1) * cw
            du = _dot_nt(dyb, wo_ref[lo:hi, :])
            dab_ref[:, lo:hi] = (du * a_ref[:, lo:hi].astype(F32)).astype(BF16)
            dab_ref[:, F + lo:F + hi] = (du * b_ref[:, lo:hi].astype(F32)).astype(BF16)
        dh = _dot(dab_ref[...], wi_ref[...])
        _, vjp = jax.vjp(_modnorm, x_ref[...], vec_ref[0:1], vec_ref[1:2], vec_ref[2:3])
        dx, dg, dsh, dsc = vjp(dh)
        dx_ref[...] = dxo_t + dx
        _acc_rows(part_ref, _rows8([dg, dsh, dsc, dgate], D), pl.program_id(0) == 0)

    return pl.pallas_call(
        body, grid=(T // tm,),
        in_specs=[_tok(tm, D), _tok(tm, D), _res((8, D)), _res((F, D)), _res((2 * F, D)), _tok(tm, F), _tok(tm, F), _tok(tm, D)],
        out_specs=[_tok(tm, D), _tok(tm, 2 * F), _tok(tm, D), pl.BlockSpec((8, D), lambda i: (0, 0))],
        out_shape=[_sds((T, D), BF16), _sds((T, 2 * F), BF16), _sds((T, D), F32), _sds((8, D), F32)],
        compiler_params=_cp("arbitrary"), name="ffn_bwd")(dxo, y, vec, w_out, w_in_t, a, b, x)


def grad_slots(a, b, name, col_slots=False, after=None):
    T, M = a.shape
    extra = [] if after is None else [after]
    N = b.shape[1]
    tk = min(CONTRACT_TILE, T)
    nk = T // tk
    tmm = _pick(M, 1408)
    if col_slots:
        ns = N // N_DEV
        sp = max(s for s in (1, 2, 4, 8) if ns * s <= GRAD_COLS or s == 1)
        tn = ns * sp
    else:
        tn = _pick(N, GRAD_COLS)

    def body(a_ref, b_ref, *rest):
        o_ref, acc = rest[-2:]
        k = pl.program_id(2)
        t = _dot_tn(a_ref[...], b_ref[...])

        @pl.when(k == 0)
        def _():
            acc[...] = t

        @pl.when(k > 0)
        def _():
            acc[...] += t

        @pl.when(k == nk - 1)
        def _():
            if col_slots:
                for s in range(sp):
                    o_ref[s] = acc[:, s * ns:(s + 1) * ns].astype(BF16)
            else:
                o_ref[...] = acc[...].astype(BF16)

    if col_slots:
        out_spec, out_shape = pl.BlockSpec((sp, tmm, ns), lambda i, j, k: (j, i, 0)), _sds((N_DEV, M, ns), BF16)
    else:
        out_spec, out_shape = pl.BlockSpec((tmm, tn), lambda i, j, k: (i, j)), _sds((M, N), BF16)
    out = pl.pallas_call(
        body, grid=(M // tmm, N // tn, nk),
        in_specs=[pl.BlockSpec((tk, tmm), lambda i, j, k: (k, i)), pl.BlockSpec((tk, tn), lambda i, j, k: (k, j))]
        + [pl.BlockSpec(memory_space=pl.ANY)] * len(extra),
        out_specs=out_spec, out_shape=out_shape,
        scratch_shapes=[pltpu.VMEM((tmm, tn), F32)],
        compiler_params=_cp("arbitrary", "arbitrary", "arbitrary"), name=name)(a, b, *extra)
    return out if col_slots else out.reshape(N_DEV, M // N_DEV, N)


def conv_fwd(x, vec, cw, w_in, w_out):
    T, D = x.shape
    tm = min(TOKEN_TILE, T)

    def body(x_ref, vec_ref, cw_ref, wi_ref, wo_ref, xn_ref, h_ref, bcu_ref, cv_ref, z_ref, y_ref, vbuf):
        @pl.when(pl.program_id(0) == 0)
        def _():
            vbuf[0:8, :] = jnp.zeros((8, D), F32)

        x_t = x_ref[...]
        hb = _modnorm(x_t, vec_ref[0:1], vec_ref[1:2], vec_ref[2:3]).astype(BF16)
        h_ref[...] = hb
        bcu = _dot(hb, wi_ref[...])
        bcu_ref[...] = bcu.astype(BF16)
        bg, v = bcu[:, 0:D], bcu[:, D:2 * D] * bcu[:, 2 * D:3 * D]
        vbuf[8:8 + tm, :] = v
        conv = cw_ref[0:1] * vbuf[6:6 + tm, :] + cw_ref[1:2] * vbuf[7:7 + tm, :] + cw_ref[2:3] * v
        cv_ref[...] = conv.astype(BF16)
        zb = (bg * conv).astype(BF16)
        z_ref[...] = zb
        y = _dot(zb, wo_ref[...])
        y_ref[...] = y.astype(BF16)
        xn_ref[...] = x_t + (1.0 + vec_ref[3:4]) * y
        vbuf[0:8, :] = vbuf[tm:tm + 8, :]

    return pl.pallas_call(
        body, grid=(T // tm,),
        in_specs=[_tok(tm, D), _res((8, D)), _res((8, D)), _res((D, 3 * D)), _res((D, D))],
        out_specs=[_tok(tm, D), _tok(tm, D), _tok(tm, 3 * D), _tok(tm, D), _tok(tm, D), _tok(tm, D)],
        out_shape=[_sds((T, D), F32), _sds((T, D), BF16), _sds((T, 3 * D), BF16), _sds((T, D), BF16),
                   _sds((T, D), BF16), _sds((T, D), BF16)],
        scratch_shapes=[pltpu.VMEM((tm + 8, D), F32)],
        compiler_params=_cp("arbitrary"), name="conv_fwd")(x, vec, cw, w_in, w_out)


def conv_bwd(dxo, x, y, bcu, cv, vec, cw, w_in, w_out):
    T, D = x.shape
    tm = min(TOKEN_TILE, T)
    nt = T // tm

    def body(dxo_ref, x_ref, y_ref, bcu_ref, cv_ref, vec_ref, cw_ref, wi_ref, wo_ref,
             dx_ref, dy_ref, dbcu_ref, part_ref, dcw_ref, dcbuf):
        first = pl.program_id(0) == 0

        @pl.when(first)
        def _():
            dcbuf[tm:tm + 8, :] = jnp.zeros((8, D), F32)

        dxo_t = dxo_ref[...]
        dyb = (dxo_t * (1.0 + vec_ref[3:4])).astype(BF16)
        dy_ref[...] = dyb
        dgate = jnp.sum(dxo_t * y_ref[...].astype(F32), axis=0, keepdims=True)
        dz = _dot_nt(dyb, wo_ref[...])
        bcu_t = bcu_ref[...].astype(F32)
        bg, cg, ug = bcu_t[:, 0:D], bcu_t[:, D:2 * D], bcu_t[:, 2 * D:3 * D]
        dconv = dz * bg
        dbg = dz * cv_ref[...].astype(F32)
        dcbuf[0:tm, :] = dconv
        d1, d2 = dcbuf[1:tm + 1, :], dcbuf[2:tm + 2, :]
        dv = cw_ref[2:3] * dconv + cw_ref[1:2] * d1 + cw_ref[0:1] * d2
        v = cg * ug
        dcw = _rows8([jnp.sum(d2 * v, axis=0, keepdims=True), jnp.sum(d1 * v, axis=0, keepdims=True),
                      jnp.sum(dconv * v, axis=0, keepdims=True)], D)
        dbcu = jnp.concatenate([dbg, dv * ug, dv * cg], axis=1).astype(BF16)
        dbcu_ref[...] = dbcu
        dh = _dot_nt(dbcu, wi_ref[...])
        _, vjp = jax.vjp(_modnorm, x_ref[...], vec_ref[0:1], vec_ref[1:2], vec_ref[2:3])
        dx, dg, dsh, dsc = vjp(dh)
        dx_ref[...] = dxo_t + dx
        _acc_rows(part_ref, _rows8([dg, dsh, dsc, dgate], D), first)
        _acc_rows(dcw_ref, dcw, first)
        dcbuf[tm:tm + 8, :] = dcbuf[0:8, :]

    def rev(w):
        return pl.BlockSpec((tm, w), lambda i: (nt - 1 - i, 0))

    return pl.pallas_call(
        body, grid=(nt,),
        in_specs=[rev(D), rev(D), rev(D), rev(3 * D), rev(D), _res((8, D)), _res((8, D)), _res((D, 3 * D)), _res((D, D))],
        out_specs=[rev(D), rev(D), rev(3 * D), pl.BlockSpec((8, D), lambda i: (0, 0)), pl.BlockSpec((8, D), lambda i: (0, 0))],
        out_shape=[_sds((T, D), F32), _sds((T, D), BF16), _sds((T, 3 * D), BF16), _sds((8, D), F32), _sds((8, D), F32)],
        scratch_shapes=[pltpu.VMEM((tm + 8, D), F32)],
        compiler_params=_cp("arbitrary"), name="conv_bwd")(dxo, x, y, bcu, cv, vec, cw, w_in, w_out)


def rope_tables(pos, lane_rows):
    T = pos.shape[0]
    tm = min(TOKEN_TILE, T)

    def body(p_ref, lr_ref, c_ref, sp_ref, sm_ref):
        ang = p_ref[...].astype(F32) * lr_ref[0:1]
        cs, sn = jnp.cos(ang), jnp.sin(ang)
        c_ref[...] = jnp.where(lr_ref[1:2] > 0.5, cs, 1.0)
        sp_ref[...] = jnp.where(lr_ref[2:3] > 0.5, sn, 0.0)
        sm_ref[...] = jnp.where(lr_ref[3:4] > 0.5, -sn, 0.0)

    return pl.pallas_call(
        body, grid=(T // tm,),
        in_specs=[_tok(tm, 1), _res((8, LANES))],
        out_specs=[_tok(tm, LANES)] * 3,
        out_shape=[_sds((T, LANES), F32)] * 3,
        compiler_params=_cp("arbitrary"), name="rope_tables")(pos, lane_rows)


def _rope(t, c, sp, sm):
    w = t.shape[1]
    reps = w // LANES
    cf, spf, smf = jnp.tile(c, (1, reps)), jnp.tile(sp, (1, reps)), jnp.tile(sm, (1, reps))
    half = ROPE_DIM // 2
    return t * cf + pltpu.roll(t, half, axis=1) * spf + pltpu.roll(t, w - half, axis=1) * smf


def _rope_t(d, c, sp, sm):
    w = d.shape[1]
    reps = w // LANES
    cf, spf, smf = jnp.tile(c, (1, reps)), jnp.tile(sp, (1, reps)), jnp.tile(sm, (1, reps))
    half = ROPE_DIM // 2
    return d * cf + pltpu.roll(d * spf, w - half, axis=1) + pltpu.roll(d * smf, half, axis=1)


def _split_residues(v, d, stage):
    tm, width = v.shape
    if d == 1:
        return [v]
    nj = width // LANES
    for j in range(nj):
        stage[j] = v[:, j * LANES:(j + 1) * LANES]
    return [jnp.concatenate([stage[j, pl.ds(r, tm // d, stride=d), :] for j in range(nj)], axis=1) for r in range(d)]


def _merge_residues(piece, d, tm, width, stage):
    if d == 1:
        return piece(0)
    nj = width // LANES
    for r in range(d):
        p = piece(r)
        for j in range(nj):
            stage[j, pl.ds(r, tm // d, stride=d), :] = p[:, j * LANES:(j + 1) * LANES]
    return jnp.concatenate([stage[j] for j in range(nj)], axis=1)


def _residue_spec(d, tm, width=GROUP_WIDTH):
    return pl.BlockSpec((d, tm // d, width), lambda i: (0, i, 0))


def _stage_scratch(tm):
    return pltpu.VMEM((GROUP_WIDTH // LANES, tm, LANES), F32)


def proj_rope_fwd(x, vec, w, tabs, n_rope, transposed, dils, name):
    T, D = x.shape
    N = w.shape[0] if transposed else w.shape[1]
    tm = min(TOKEN_TILE, T)
    GW = GROUP_WIDTH
    piece_dils = [dils[j % len(dils)] for j in range(N // GW)]

    def body(x_ref, vec_ref, w_ref, c_ref, sp_ref, sm_ref, h_ref, *rest):
        out_refs, stage = rest[:-1], rest[-1]
        hb = _modnorm(x_ref[...], vec_ref[0:1], vec_ref[1:2], vec_ref[2:3]).astype(BF16)
        h_ref[...] = hb
        p = _dot_nt(hb, w_ref[...]) if transposed else _dot(hb, w_ref[...])
        pr = _rope(p[:, 0:n_rope], c_ref[...], sp_ref[...], sm_ref[...])
        for j, d in enumerate(piece_dils):
            src = pr if (j + 1) * GW <= n_rope else p
            for r, rows in enumerate(_split_residues(src[:, j * GW:(j + 1) * GW], d, stage)):
                out_refs[j][r] = rows.astype(BF16)

    return pl.pallas_call(
        body, grid=(T // tm,),
        in_specs=[_tok(tm, D), _res((8, D)), _res(w.shape)] + [_tok(tm, LANES)] * 3,
        out_specs=[_tok(tm, D)] + [_residue_spec(d, tm) for d in piece_dils],
        out_shape=[_sds((T, D), BF16)] + [_sds((d, T // d, GW), BF16) for d in piece_dils],
        scratch_shapes=[_stage_scratch(tm)],
        compiler_params=_cp("arbitrary"), name=name)(x, vec, w, *tabs)


def proj_rope_bwd(dparts, dils, x, dxo, vec, w, tabs, n_rope, transposed, name):
    T, D = x.shape
    N = w.shape[0] if transposed else w.shape[1]
    tm = min(TOKEN_TILE, T)
    GW = GROUP_WIDTH
    npart = len(dparts)
    piece_dils = [dils[j % len(dils)] for j in range(npart)]

    def body(*refs):
        d_refs = refs[:npart]
        x_ref, dxo_ref, vec_ref, w_ref, c_ref, sp_ref, sm_ref, dx_ref, dp_ref, part_ref, stage = refs[npart:]
        d = jnp.concatenate([_merge_residues(lambda r, ref=ref: ref[r].astype(F32), dd, tm, GW, stage)
                             for ref, dd in zip(d_refs, piece_dils)], axis=1)
        dr = _rope_t(d[:, 0:n_rope], c_ref[...], sp_ref[...], sm_ref[...])
        if n_rope < N:
            dr = jnp.concatenate([dr, d[:, n_rope:N]], axis=1)
        dpb = dr.astype(BF16)
        dp_ref[...] = dpb
        dh = _dot(dpb, w_ref[...]) if transposed else _dot_nt(dpb, w_ref[...])
        _, vjp = jax.vjp(_modnorm, x_ref[...], vec_ref[0:1], vec_ref[1:2], vec_ref[2:3])
        dx, dg, dsh, dsc = vjp(dh)
        dx_ref[...] = dxo_ref[...] + dx
        _acc_rows(part_ref, _rows8([dg, dsh, dsc], D), pl.program_id(0) == 0)

    return pl.pallas_call(
        body, grid=(T // tm,),
        in_specs=[_residue_spec(d, tm) for d in piece_dils] + [_tok(tm, D), _tok(tm, D), _res((8, D)), _res(w.shape)]
        + [_tok(tm, LANES)] * 3,
        out_specs=[_tok(tm, D), _tok(tm, N), pl.BlockSpec((8, D), lambda i: (0, 0))],
        out_shape=[_sds((T, D), F32), _sds((T, N), BF16), _sds((8, D), F32)],
        scratch_shapes=[_stage_scratch(tm)],
        compiler_params=_cp("arbitrary"), name=name)(*dparts, x, dxo, vec, w, *tabs)


def _valid_mask(n, i):
    qi = lax.broadcasted_iota(jnp.int32, (n, 2 * n), 0)
    kj = lax.broadcasted_iota(jnp.int32, (n, 2 * n), 1)
    dist = n + qi - kj
    return (dist >= 0) & (dist <= n) & ((kj >= n) | (i > 0))


STAT_STRIDE = LANES // HEADS_PER_GROUP


def _head_of_lane():
    return lax.broadcasted_iota(jnp.int32, (1, LANES), 1) // STAT_STRIDE


def attn_core_fwd(q, k, v, g, n):
    d, M, GW = q.shape
    scale = HEAD_DIM ** -0.5

    qb = min(FWD_QUERY_BLOCKS, M // n)

    def body(q_ref, kp_ref, kc_ref, vp_ref, vc_ref, o_ref, l_ref):
        masks = [_valid_mask(n, pl.program_id(1))] + [_valid_mask(n, 1)] * (qb - 1)
        first = lax.broadcasted_iota(jnp.int32, (1, LANES), 1) < HEAD_DIM
        head_of_lane = _head_of_lane()
        lse = [jnp.zeros((n, LANES), F32) for _ in range(qb)]
        for pair in range(HEADS_PER_GROUP * HEAD_DIM // LANES):
            ps = slice(LANES * pair, LANES * (pair + 1))
            kall = jnp.concatenate([kp_ref[:, ps], kc_ref[:, ps]], axis=0)
            vall = jnp.concatenate([vp_ref[:, ps], vc_ref[:, ps]], axis=0)
            for blk in range(qb):
                rows = slice(blk * n, (blk + 1) * n)
                keys, vals = kall[blk * n:(blk + 2) * n], vall[blk * n:(blk + 2) * n]
                q2 = q_ref[rows, ps]
                o2, l2 = [], []
                for sel in (first, jnp.logical_not(first)):
                    s = jnp.where(masks[blk], _dot_nt(jnp.where(sel, q2, jnp.zeros_like(q2)), keys) * scale, -1e30)
                    m = jnp.max(s, axis=1, keepdims=True)
                    p = jnp.exp(s - m)
                    den = jnp.sum(p, axis=1, keepdims=True)
                    o2.append(_dot((p / den).astype(BF16), vals))
                    l2.append(m + jnp.log(den))
                o_ref[rows, ps] = jnp.where(first, o2[0], o2[1]).astype(BF16)
                for half in range(2):
                    lse[blk] = jnp.where(head_of_lane == 2 * pair + half, l2[half], lse[blk])
        for blk in range(qb):
            l_ref[blk * n:(blk + 1) * n, :] = lse[blk]

    two = pl.BlockSpec((None, qb * n, GW), lambda r, i: (r, i, 0))
    prv = pl.BlockSpec((None, n, GW), lambda r, i: (r, jnp.maximum(qb * i - 1, 0), 0))
    stat = pl.BlockSpec((None, qb * n, LANES), lambda r, i: (r, i, 0))
    return pl.pallas_call(
        body, grid=(d, M // (qb * n)),
        in_specs=[two, prv, two, prv, two], out_specs=[two, stat],
        out_shape=[_sds((d, M, GW), BF16), _sds((d, M, LANES), F32)],
        compiler_params=_cp("arbitrary", "arbitrary"), name=f"attn_fwd_g{g}")(q, k, k, v, v)


def attn_core_bwd(q, k, v, do, delta, lse, g, n, prior=None):
    d, M, GW = q.shape
    scale = HEAD_DIM ** -0.5
    qb = min(BWD_QUERY_BLOCKS, M // n)
    steps = M // (qb * n)
    priors = () if prior is None else tuple(prior)

    def body(q_ref, kp_ref, kc_ref, vp_ref, vc_ref, do_ref, d_ref, l_ref, *rest):
        prior_refs, (dq_ref, dk_ref, dv_ref, carry_k, carry_v) = rest[:-5], rest[-5:]

        @pl.when(pl.program_id(1) == 0)
        def _():
            carry_k[...] = jnp.zeros_like(carry_k)
            carry_v[...] = jnp.zeros_like(carry_v)

        masks = [_valid_mask(n, steps - 1 - pl.program_id(1))] + [_valid_mask(n, 1)] * (qb - 1)
        first = lax.broadcasted_iota(jnp.int32, (1, LANES), 1) < HEAD_DIM
        for pair in range(HEADS_PER_GROUP * HEAD_DIM // LANES):
            ps = slice(LANES * pair, LANES * (pair + 1))
            kall = jnp.concatenate([kp_ref[:, ps], kc_ref[:, ps]], axis=0)
            vall = jnp.concatenate([vp_ref[:, ps], vc_ref[:, ps]], axis=0)
            own = []
            for blk in range(qb):
                rows = slice(blk * n, (blk + 1) * n)
                keys, vals = kall[blk * n:(blk + 2) * n], vall[blk * n:(blk + 2) * n]
                q2, do2 = q_ref[rows, ps], do_ref[rows, ps]
                dq2, dk, dv = [], None, None
                for half, sel in enumerate((first, jnp.logical_not(first))):
                    qm = jnp.where(sel, q2, jnp.zeros_like(q2))
                    dom = jnp.where(sel, do2, jnp.zeros_like(do2))
                    s = jnp.where(masks[blk], _dot_nt(qm, keys) * scale, -1e30)
                    lane0 = STAT_STRIDE * (2 * pair + half)
                    p = jnp.exp(s - l_ref[rows, lane0:lane0 + 1])
                    dp = _dot_nt(dom, vals)
                    ds = (p * (dp - d_ref[rows, lane0:lane0 + 1]) * scale).astype(BF16)
                    dq2.append(_dot(ds, keys))
                    dkh = _dot_tn(ds, qm)
                    dvh = _dot_tn(p.astype(BF16), dom)
                    dk = dkh if dk is None else dk + dkh
                    dv = dvh if dv is None else dv + dvh
                dq_ref[rows, ps] = jnp.where(first, dq2[0], dq2[1]).astype(BF16)
                own.append((dk, dv))
            for t, (o_ref, carry) in enumerate(((dk_ref, carry_k), (dv_ref, carry_v))):
                pieces = [own[blk][t][n:2 * n] + own[blk + 1][t][0:n] for blk in range(qb - 1)]
                pieces.append(own[qb - 1][t][n:2 * n] + carry[:, ps])
                carry[:, ps] = own[0][t][0:n]
                for blk, piece in enumerate(pieces):
                    rows = slice(blk * n, (blk + 1) * n)
                    if prior_refs:
                        piece = piece + prior_refs[t][rows, ps].astype(F32)
                    o_ref[rows, ps] = piece.astype(BF16)

    run = pl.BlockSpec((None, qb * n, GW), lambda r, i: (r, steps - 1 - i, 0))
    prv = pl.BlockSpec((None, n, GW), lambda r, i: (r, jnp.maximum(qb * (steps - 1 - i) - 1, 0), 0))
    stat = pl.BlockSpec((None, qb * n, LANES), lambda r, i: (r, steps - 1 - i, 0))
    return pl.pallas_call(
        body, grid=(d, steps),
        in_specs=[run, prv, run, prv, run, run, stat, stat] + [run] * len(priors), out_specs=[run, run, run],
        out_shape=[_sds((d, M, GW), BF16)] * 3,
        scratch_shapes=[pltpu.VMEM((n, GW), F32)] * 2,
        compiler_params=_cp("arbitrary", "arbitrary"), name=f"attn_bwd_g{g}")(q, k, k, v, v, do, delta, lse, *priors)


def _group_weights(ls):
    mx = functools.reduce(jnp.maximum, ls)
    es = [jnp.exp(l - mx) for l in ls]
    tot = functools.reduce(lambda a, b: a + b, es)
    return [e / tot for e in es]


def _expand_heads(w):
    tm = w.shape[0]
    first = lax.broadcasted_iota(jnp.int32, (1, LANES), 1) < HEAD_DIM
    cols = [jnp.broadcast_to(w[:, STAT_STRIDE * h:STAT_STRIDE * h + 1], (tm, LANES)) for h in range(HEADS_PER_GROUP)]
    return jnp.concatenate([jnp.where(first, cols[2 * p], cols[2 * p + 1]) for p in range(HEADS_PER_GROUP // 2)], axis=1)


def _head_sums(r):
    width = r.shape[1]
    feat_head = lax.broadcasted_iota(jnp.int32, (width, LANES), 0) // HEAD_DIM
    stat_head = lax.broadcasted_iota(jnp.int32, (width, LANES), 1) // STAT_STRIDE
    ones = jnp.where(feat_head == stat_head, 1.0, 0.0).astype(BF16)
    hi = r.astype(BF16)
    lo = (r - hi.astype(F32)).astype(BF16)
    return _dot(hi, ones) + _dot(lo, ones)


def _mix_weights(l_refs, dils, tm, stage):
    ls = [_merge_residues(lambda r, ref=ref: ref[r], d, tm, LANES, stage) for ref, d in zip(l_refs, dils)]
    return [_expand_heads(w) for w in _group_weights(ls)]


def attn_mix_out(os_, ls, dils, x, vec, w_o):
    T, D = x.shape
    GW = GROUP_WIDTH
    tm = min(TOKEN_TILE, T)
    ng = len(os_)

    def body(*refs):
        o_refs, l_refs = refs[:ng], refs[ng:2 * ng]
        x_ref, vec_ref, w_ref, xn_ref, mix_ref, y_ref, stage = refs[2 * ng:]
        natural = lambda ref, d: _merge_residues(lambda r: ref[r].astype(F32), d, tm, GW, stage)
        ws = _mix_weights(l_refs, dils, tm, stage)
        mixed = functools.reduce(lambda a, b: a + b, [w * natural(r, d) for w, r, d in zip(ws, o_refs, dils)])
        mb = mixed.astype(BF16)
        mix_ref[...] = mb
        y = _dot(mb, w_ref[...])
        y_ref[...] = y.astype(BF16)
        xn_ref[...] = x_ref[...] + (1.0 + vec_ref[3:4]) * y

    res = [_residue_spec(d, tm) for d in dils]
    stat = [_residue_spec(d, tm, LANES) for d in dils]
    return pl.pallas_call(
        body, grid=(T // tm,),
        in_specs=res + stat + [_tok(tm, D), _res((8, D)), _res((GW, D))],
        out_specs=[_tok(tm, D), _tok(tm, GW), _tok(tm, D)],
        out_shape=[_sds((T, D), F32), _sds((T, GW), BF16), _sds((T, D), BF16)],
        scratch_shapes=[_stage_scratch(tm)],
        compiler_params=_cp("arbitrary"), name="attn_mix_out")(*os_, *ls, x, vec, w_o)


def attn_mix_bwd(dxo, y, vec, w_o, os_, ls, dils):
    T, D = dxo.shape
    GW = GROUP_WIDTH
    tm = min(TOKEN_TILE, T)
    ng = len(os_)

    def body(*refs):
        dxo_ref, y_ref, vec_ref, w_ref = refs[:4]
        o_refs, l_refs = refs[4:4 + ng], refs[4 + ng:4 + 2 * ng]
        dy_ref = refs[4 + 2 * ng]
        do_refs = refs[5 + 2 * ng:5 + 3 * ng]
        d_refs = refs[5 + 3 * ng:5 + 4 * ng]
        part_ref, stage = refs[5 + 4 * ng], refs[6 + 4 * ng]
        natural = lambda ref, d: _merge_residues(lambda r: ref[r].astype(F32), d, tm, GW, stage)
        dxo_t = dxo_ref[...]
        dyb = (dxo_t * (1.0 + vec_ref[3:4])).astype(BF16)
        dy_ref[...] = dyb
        dgate = jnp.sum(dxo_t * y_ref[...].astype(F32), axis=0, keepdims=True)
        _acc_rows(part_ref, _rows8([dgate], D), pl.program_id(0) == 0)
        dmix = _dot_nt(dyb, w_ref[...])
        ws = _mix_weights(l_refs, dils, tm, stage)
        mixed = functools.reduce(lambda a, b: a + b, [w * natural(r, d) for w, r, d in zip(ws, o_refs, dils)])
        for gi in range(ng):
            do = ws[gi] * dmix
            for r, rows in enumerate(_split_residues(do, dils[gi], stage)):
                do_refs[gi][r] = rows.astype(BF16)
            for r, rows in enumerate(_split_residues(_head_sums(do * mixed), dils[gi], stage)):
                d_refs[gi][r] = rows

    res = [_residue_spec(d, tm) for d in dils]
    stat = [_residue_spec(d, tm, LANES) for d in dils]
    return pl.pallas_call(
        body, grid=(T // tm,),
        in_specs=[_tok(tm, D), _tok(tm, D), _res((8, D)), _res((GW, D))] + res + stat,
        out_specs=[_tok(tm, D)] + res + stat + [pl.BlockSpec((8, D), lambda i: (0, 0))],
        out_shape=[_sds((T, D), BF16)] + [_sds((d, T // d, GW), BF16) for d in dils]
        + [_sds((d, T // d, LANES), F32) for d in dils] + [_sds((8, D), F32)],
        scratch_shapes=[_stage_scratch(tm)],
        compiler_params=_cp("arbitrary"), name="attn_mix_bwd")(dxo, y, vec, w_o, *os_, *ls)


def final_loss(x, gvec, target):
    T, D = x.shape
    tm = min(TOKEN_TILE, T)

    def norm(xv, g):
        return xv * lax.rsqrt(jnp.mean(xv * xv, axis=-1, keepdims=True) + NORM_EPS) * g

    def body(x_ref, g_ref, t_ref, dx_ref, part_ref, loss_ref):
        first = pl.program_id(0) == 0
        yv, vjp = jax.vjp(norm, x_ref[...], g_ref[0:1])
        err = yv - t_ref[...]
        dx, dg = vjp(err * (1.0 / D))
        dx_ref[...] = dx
        _acc_rows(part_ref, _rows8([dg], D), first)
        tile_loss = 0.5 * jnp.sum(jnp.sum(err * err, axis=1, keepdims=True) * (1.0 / D), axis=0, keepdims=True)
        _acc_rows(loss_ref, jnp.broadcast_to(tile_loss, (8, LANES)), first)

    return pl.pallas_call(
        body, grid=(T // tm,),
        in_specs=[_tok(tm, D), _res((8, D)), _tok(tm, D)],
        out_specs=[_tok(tm, D), pl.BlockSpec((8, D), lambda i: (0, 0)), pl.BlockSpec((8, LANES), lambda i: (0, 0))],
        out_shape=[_sds((T, D), F32), _sds((8, D), F32), _sds((8, LANES), F32)],
        compiler_params=_cp("arbitrary"), name="final_loss")(x, gvec, target)


def mods_project(c_all, w, b):
    B, D = c_all.shape
    L, _, N = w.shape

    def body(c_ref, w_ref, b_ref, o_ref):
        cv = c_ref[...]
        cond = cv * _sigmoid(cv)
        o_ref[0] = jnp.dot(cond, w_ref[0], preferred_element_type=F32, precision=lax.Precision.HIGHEST) + b_ref[0]

    return pl.pallas_call(
        body, grid=(L,),
        in_specs=[pl.BlockSpec((B, D), lambda l: (0, 0)), pl.BlockSpec((1, D, N), lambda l: (l, 0, 0)),
                  pl.BlockSpec((1, 1, N), lambda l: (l, 0, 0))],
        out_specs=pl.BlockSpec((1, B, N), lambda l: (l, 0, 0)),
        out_shape=_sds((L, B, N), F32),
        compiler_params=_cp("arbitrary"), name="mods_project")(c_all, w, b)


def mods_weight_grad(c_all, dm):
    B, D = c_all.shape
    L, _, N = dm.shape

    def body(c_ref, d_ref, o_ref):
        cv = c_ref[...]
        cond = cv * _sigmoid(cv)
        o_ref[0] = lax.dot_general(cond, d_ref[0], (((0,), (0,)), ((), ())), preferred_element_type=F32,
                                   precision=lax.Precision.HIGHEST)

    return pl.pallas_call(
        body, grid=(L,),
        in_specs=[pl.BlockSpec((B, D), lambda l: (0, 0)), pl.BlockSpec((1, B, N), lambda l: (l, 0, 0))],
        out_specs=pl.BlockSpec((1, D, N), lambda l: (l, 0, 0)),
        out_shape=_sds((L, D, N), F32),
        compiler_params=_cp("arbitrary"), name="mods_weight_grad")(c_all, dm)


def _adam_math(g, w, m, v):
    m2 = ADAM_B1 * m + (1.0 - ADAM_B1) * g
    v2 = ADAM_B2 * v + (1.0 - ADAM_B2) * (g * g)
    m_hat = m2 / (1.0 - ADAM_B1 ** ADAM_STEP)
    v_hat = v2 / (1.0 - ADAM_B2 ** ADAM_STEP)
    delta = -ADAM_LR * (m_hat / (jnp.sqrt(v_hat) + ADAM_EPS) + ADAM_WD * w)
    return delta, m2, v2


def adam_update(g, w, m, v, parts, name):
    R, C = w.shape
    tr = _pick(R, 256, 8)

    def body(g_ref, w_ref, m_ref, v_ref, go_ref, d_ref, mo_ref, vo_ref):
        if parts:
            gv = g_ref[0].astype(F32)
            for s in range(1, N_DEV):
                gv = gv + g_ref[s].astype(F32)
        else:
            gv = g_ref[...]
        go_ref[...] = gv
        d_ref[...], mo_ref[...], vo_ref[...] = _adam_math(gv, w_ref[...], m_ref[...], v_ref[...])

    gspec = pl.BlockSpec((N_DEV, tr, C), lambda i: (0, i, 0)) if parts else _tok(tr, C)
    return pl.pallas_call(
        body, grid=(R // tr,),
        in_specs=[gspec, _tok(tr, C), _tok(tr, C), _tok(tr, C)],
        out_specs=[_tok(tr, C)] * 4, out_shape=[_sds((R, C), F32)] * 4,
        compiler_params=_cp("arbitrary"), name=name)(g, w, m, v)


def adam_layer(parts, w, m, v, prev, layer, after, name):
    L, R, C = w.shape
    tr = _pick(R, 256, 8)
    prev = (list(prev) if prev is not None else []) + [after]

    def body(p_ref, w_ref, m_ref, v_ref, *rest):
        go_ref, d_ref, mo_ref, vo_ref = rest[-4:]
        gv = p_ref[0].astype(F32)
        for s in range(1, N_DEV):
            gv = gv + p_ref[s].astype(F32)
        go_ref[...] = gv
        d_ref[...], mo_ref[...], vo_ref[...] = _adam_math(gv, w_ref[...], m_ref[...], v_ref[...])

    lay = pl.BlockSpec((None, tr, C), lambda i: (layer, i, 0))
    return pl.pallas_call(
        body, grid=(R // tr,),
        in_specs=[pl.BlockSpec((N_DEV, tr, C), lambda i: (0, i, 0)), lay, lay, lay] + [pl.BlockSpec(memory_space=pl.ANY)] * len(prev),
        out_specs=[lay] * 4, out_shape=[_sds((L, R, C), F32)] * 4,
        input_output_aliases={4 + k: k for k in range(len(prev) - 1)},
        compiler_params=_cp("arbitrary"), name=name)(parts, w, m, v, *prev)


def _my_id():
    return 4 * lax.axis_index("x") + 2 * lax.axis_index("y") + lax.axis_index("c")


def _peer(s):
    x, y, c = lax.axis_index("x"), lax.axis_index("y"), lax.axis_index("c")
    px = (1 - x) if s & 4 else x
    py = (1 - y) if s & 2 else y
    pc = (1 - c) if s & 1 else c
    return (px, py, pc), 4 * px + 2 * py + pc


def all_gather(xs, space, name):
    na = len(xs)

    def body(*refs):
        x_refs, o_refs = refs[:na], refs[na:2 * na]
        send_sems, recv_sems, local_sems = refs[2 * na:]
        me = _my_id()
        locals_, sends = [], []
        for a in range(na):
            cp = pltpu.make_async_copy(x_refs[a], o_refs[a].at[me], local_sems.at[a])
            cp.start()
            locals_.append(cp)
        for s in range(1, N_DEV):
            peer, _ = _peer(s)
            for a in range(na):
                cp = pltpu.make_async_remote_copy(
                    src_ref=x_refs[a], dst_ref=o_refs[a].at[me], send_sem=send_sems.at[a, s - 1],
                    recv_sem=recv_sems.at[a, s - 1], device_id=peer, device_id_type=MESH)
                cp.start()
                sends.append(cp)
        for s in range(1, N_DEV):
            peer, pid = _peer(s)
            for a in range(na):
                pltpu.make_async_remote_copy(
                    src_ref=x_refs[a], dst_ref=o_refs[a].at[pid], send_sem=send_sems.at[a, s - 1],
                    recv_sem=recv_sems.at[a, s - 1], device_id=peer, device_id_type=MESH).wait_recv()
        for cp in sends:
            cp.wait_send()
        for cp in locals_:
            cp.wait()

    spec = pl.BlockSpec(memory_space=space)
    return pl.pallas_call(
        body, in_specs=[spec] * na, out_specs=[spec] * na,
        out_shape=[_sds((N_DEV,) + x.shape, x.dtype) for x in xs],
        scratch_shapes=[pltpu.SemaphoreType.DMA((na, N_DEV - 1)), pltpu.SemaphoreType.DMA((na, N_DEV - 1)),
                        pltpu.SemaphoreType.DMA((na,))],
        compiler_params=pltpu.CompilerParams(vmem_limit_bytes=VMEM_LIMIT), name=name)(*xs)


def exchange_slots(xs, name):
    na = len(xs)

    def body(*refs):
        x_refs, o_refs = refs[:na], refs[na:2 * na]
        send_sems, recv_sems, local_sems = refs[2 * na:]
        me = _my_id()
        locals_, sends = [], []
        for a in range(na):
            cp = pltpu.make_async_copy(x_refs[a].at[me], o_refs[a].at[me], local_sems.at[a])
            cp.start()
            locals_.append(cp)
        for s in range(1, N_DEV):
            peer, pid = _peer(s)
            for a in range(na):
                cp = pltpu.make_async_remote_copy(
                    src_ref=x_refs[a].at[pid], dst_ref=o_refs[a].at[me], send_sem=send_sems.at[a, s - 1],
                    recv_sem=recv_sems.at[a, s - 1], device_id=peer, device_id_type=MESH)
                cp.start()
                sends.append(cp)
        for s in range(1, N_DEV):
            peer, pid = _peer(s)
            for a in range(na):
                pltpu.make_async_remote_copy(
                    src_ref=x_refs[a].at[pid], dst_ref=o_refs[a].at[pid], send_sem=send_sems.at[a, s - 1],
                    recv_sem=recv_sems.at[a, s - 1], device_id=peer, device_id_type=MESH).wait_recv()
        for cp in sends:
            cp.wait_send()
        for cp in locals_:
            cp.wait()

    spec = pl.BlockSpec(memory_space=pl.ANY)
    return pl.pallas_call(
        body, in_specs=[spec] * na, out_specs=[spec] * na,
        out_shape=[_sds(x.shape, x.dtype) for x in xs],
        scratch_shapes=[pltpu.SemaphoreType.DMA((na, N_DEV - 1)), pltpu.SemaphoreType.DMA((na, N_DEV - 1)),
                        pltpu.SemaphoreType.DMA((na,))],
        compiler_params=pltpu.CompilerParams(vmem_limit_bytes=VMEM_LIMIT), name=name)(*xs)


_HBM = pl.BlockSpec(memory_space=pltpu.HBM)
_SEM = pl.BlockSpec(memory_space=pltpu.SEMAPHORE)
_EFFECT = pltpu.SideEffectType.DATAFLOW_SIDE_EFFECTING


def _split_copies(pattern, x_ref, land_ref, send_sem, recv_sem):
    me = _my_id()
    if pattern in ("gather", "scatter"):
        plan = []
        for s in range(1, N_DEV):
            peer, pid = _peer(s)
            plan.append((x_ref.at[pid] if pattern == "scatter" else x_ref, land_ref.at[me], peer))
    elif pattern == "to_chips":
        plan = [(x_ref, land_ref.at[me], _peer(s)[0]) for s in (1, 2, 4, 6)]
    else:
        sibling = _peer(1)[0]
        plan = [(land_ref.at[_peer(s)[1]], land_ref.at[_peer(s)[1]], sibling) for s in (2, 4, 6)]
    return [pltpu.make_async_remote_copy(src_ref=src, dst_ref=dst, send_sem=send_sem, recv_sem=recv_sem,
                                         device_id=dev, device_id_type=MESH) for src, dst, dev in plan]


def comm_start(xs, pattern, after, name, lands=None):
    na = len(xs)
    extra = [] if after is None else [after]
    fill_own = lands is None
    if fill_own:
        lands = [lax.empty(x.shape if pattern == "scatter" else (N_DEV,) + x.shape, x.dtype) for x in xs]

    def body(*refs):
        x_refs, land_refs = refs[:na], refs[na:2 * na]
        send_sem, recv_sem = refs[2 * na + len(extra)], refs[2 * na + len(extra) + 1]
        token, own_sems = refs[-2], refs[-1]
        local = []
        if fill_own:
            me = _my_id()
            for a in range(na):
                src = x_refs[a].at[me] if pattern == "scatter" else x_refs[a]
                local.append(pltpu.make_async_copy(src, land_refs[a].at[me], own_sems.at[a]))
                local[-1].start()
        for a in range(na):
            for cp in _split_copies(pattern, x_refs[a], land_refs[a], send_sem, recv_sem):
                cp.start()
        for cp in local:
            cp.wait()
        token[...] = jnp.zeros_like(token)

    outs = pl.pallas_call(
        body, name=name,
        out_shape=(pltpu.SemaphoreType.DMA(()), pltpu.SemaphoreType.DMA(()))
        + tuple(pltpu.HBM(x.shape, x.dtype) for x in xs) + tuple(pltpu.HBM(l.shape, l.dtype) for l in lands)
        + (_sds((8, LANES), F32),),
        in_specs=(_HBM,) * (2 * na) + (pl.BlockSpec(memory_space=pl.ANY),) * len(extra),
        out_specs=(_SEM, _SEM) + (_HBM,) * (2 * na) + (pl.BlockSpec(memory_space=pltpu.VMEM),),
        input_output_aliases={a: 2 + a for a in range(2 * na)},
        scratch_shapes=[pltpu.SemaphoreType.DMA((na,))],
        compiler_params=pltpu.CompilerParams(has_side_effects=_EFFECT),
    )(*[pltpu.with_memory_space_constraint(x, pltpu.HBM) for x in xs],
      *[pltpu.with_memory_space_constraint(l, pltpu.HBM) for l in lands], *extra)
    return dict(sems=outs[0:2], xs=outs[2:2 + na], lands=outs[2 + na:2 + 2 * na], token=outs[-1], pattern=pattern)


def comm_wait(started, after, name, with_xs=False):
    xs, lands = started["xs"], started["lands"]
    pattern = started["pattern"]
    na = len(xs)

    def body(*refs):
        x_refs, land_refs = refs[:na], refs[na:2 * na]
        send_sem, recv_sem = refs[2 * na], refs[2 * na + 1]
        for a in range(na):
            for cp in _split_copies(pattern, x_refs[a], land_refs[a], send_sem, recv_sem):
                cp.wait_send()
                cp.wait_recv()

    outs = pl.pallas_call(
        body, name=name,
        out_shape=tuple(pltpu.HBM(x.shape, x.dtype) for x in xs) + tuple(pltpu.HBM(l.shape, l.dtype) for l in lands),
        in_specs=(_HBM,) * (2 * na) + (_SEM, _SEM, pl.BlockSpec(memory_space=pl.ANY)),
        out_specs=(_HBM,) * (2 * na),
        input_output_aliases={a: a for a in range(2 * na)},
        compiler_params=pltpu.CompilerParams(has_side_effects=_EFFECT),
    )(*xs, *lands, *started["sems"], after)
    return (list(outs[na:]), list(outs[:na])) if with_xs else list(outs[na:])


def _cols_to_natural(g):
    return jnp.concatenate([g[k] for k in range(N_DEV)], axis=1)


def _vec8(rows, d):
    rows = [r.reshape(1, d).astype(F32) for r in rows]
    return jnp.concatenate(rows + [jnp.zeros((8 - len(rows), d), F32)], axis=0)


def _ffn_forward(x, vec, w_in_t, w_out):
    xn, h, a, b, u, y = ffn_fwd(x, vec, w_in_t, w_out)
    return xn, (x, h, a, b, u, y)


def _ffn_backward(dxo, saved, vec, w_in_t, w_out, on_rows=None):
    x, h, a, b, u, y = saved
    dy, dab, dx, part = ffn_bwd(dxo, y, vec, w_out, w_in_t, a, b, x)
    rows = part[0:4]
    token = on_rows(rows) if on_rows is not None else None
    g_out = grad_slots(u, dy, "ffn_dw_out", after=token)
    g_in_t = grad_slots(dab, h, "ffn_dw_in", after=token)
    return dx, g_in_t, g_out, rows


_TRANSPOSED = ("ffn1_w_in", "ffn2_w_in", "attn_w_q")
_COL_NATURAL = ("conv_w_in", "w_kv", "attn_w_o")
_ROW_SHARDED = ("ffn1_w_out", "ffn2_w_out", "conv_w_out")
_BIG = _TRANSPOSED + _COL_NATURAL + _ROW_SHARDED


def weight_chunks():
    chunks = []
    for layer in range(DEPTH):
        first = [("ffn1_w_in", layer), ("ffn1_w_out", layer)]
        if layer == N_A_LAYERS:
            first = [("w_kv", layer)] + first
        mixer = [("conv_w_in", layer), ("conv_w_out", layer)] if layer < N_A_LAYERS else [("attn_w_q", layer), ("attn_w_o", layer)]
        rest = mixer + [("ffn2_w_in", layer), ("ffn2_w_out", layer)]
        chunks += [first, rest] if layer == 0 else [first + rest]
    return chunks


def stacked_index(name, layer):
    if name == "w_kv":
        return None
    return layer - N_A_LAYERS if name.startswith("attn") else layer


class ChunkComm:
    def __init__(self, shards):
        self.shards = shards
        self.chunks = weight_chunks()

    def _shard(self, name, layer):
        idx = stacked_index(name, layer)
        return self.shards[name][0 if idx is None else idx]

    def start_gather(self, ci, after):
        xs = [self._shard(n, l).astype(BF16) for n, l in self.chunks[ci]]
        return comm_start(xs, "to_chips", after, f"gather_start_{ci}")

    def relay_gather(self, ci, started, after):
        lands, xs = comm_wait(started, after, f"gather_wait_{ci}", with_xs=True)
        return comm_start(xs, "relay", None, f"gather_relay_{ci}", lands=lands)

    def finish_gather(self, ci, relayed, after):
        lands = comm_wait(relayed, after, f"gather_done_{ci}")
        W = {}
        for key, g in zip(self.chunks[ci], lands):
            W[key] = _cols_to_natural(g) if key[0] in _COL_NATURAL else g.reshape(-1, g.shape[2])
        return W

    def start_exchange(self, ci, slots, after):
        return comm_start([slots[key] for key in self.chunks[ci]], "scatter", after, f"exchange_start_{ci}")

    def finish_exchange(self, ci, started, after):
        lands = comm_wait(started, after, f"exchange_wait_{ci}")
        return dict(zip(self.chunks[ci], lands))


def device_step(x, positions, target, mods, kvmods, small, comm, gather0):
    T, D = x.shape
    groups = DILATED_GROUPS
    dils = [dil for _, dil in groups]
    lane = jnp.arange(LANES) % HEAD_DIM
    inv = ROPE_THETA ** (-jnp.arange(0, ROPE_DIM, 2, dtype=F32) / ROPE_DIM)
    lane_rows = _vec8([jnp.where(lane < ROPE_DIM, inv[lane % (ROPE_DIM // 2)], 0.0), lane < ROPE_DIM,
                       (lane >= ROPE_DIM // 2) & (lane < ROPE_DIM), lane < ROPE_DIM // 2], LANES)
    tabs = rope_tables(positions.reshape(T, 1), lane_rows)

    def after_token(v, token):
        return v if token is None else v + token[0, 0]

    def vec_of(layer, sub):
        return _vec8([small["norm_g"][layer, sub], mods[layer, 3 * sub], mods[layer, 3 * sub + 1], mods[layer, 3 * sub + 2]], D)

    saved = []
    kv_saved = None
    k_sh = v_sh = None
    qw = GROUP_WIDTH * len(groups)
    chunk_of = {key: ci for ci, chunk in enumerate(comm.chunks) for key in chunk}
    W = {}
    flight = {"ci": 0, "started": gather0}

    relayed = {}

    def advance(after):
        ci = flight["ci"]
        if flight["started"] is None or ci in relayed:
            return None
        relayed[ci] = comm.relay_gather(ci, flight["started"], after)
        relayed[ci]["behind"] = relayed[ci]["token"]
        nxt = comm.start_gather(ci + 1, relayed[ci]["token"]) if ci + 1 < len(comm.chunks) else None
        flight.update(ci=ci + 1, started=nxt)
        if nxt is not None:
            relayed[ci]["behind"] = nxt["token"]
        return relayed[ci]["behind"]

    def need(key, after):
        if key not in W:
            ci = chunk_of[key]
            if ci not in relayed:
                assert ci == flight["ci"], (key, ci)
                advance(after)
            W.update(comm.finish_gather(ci, relayed[ci], relayed[ci]["behind"]))
        return W[key]

    for layer in range(DEPTH):
        if layer == N_A_LAYERS:
            w_kv = need(("w_kv", layer), x)
            kv_vec = _vec8([small["kv_norm_g"], kvmods[0], kvmods[1]], D)
            h_kv, *kv_pieces = proj_rope_fwd(x, kv_vec, w_kv, tabs, qw, False, dils, "kv_fwd")
            k_sh, v_sh = kv_pieces[:len(groups)], kv_pieces[len(groups):]
            kv_saved = (x, h_kv, kv_vec)
        rec = {}
        behind = tabs[0] if layer == 0 else x
        w_in, w_out = need(("ffn1_w_in", layer), behind), need(("ffn1_w_out", layer), behind)
        v1 = vec_of(layer, 0)
        x, rec["ffn1"] = _ffn_forward(x, v1, w_in, w_out)
        if layer < N_A_LAYERS:
            w_in, w_out = need(("conv_w_in", layer), x), need(("conv_w_out", layer), x)
            v2 = vec_of(layer, 1)
            cw = _vec8(list(small["conv_w"][layer]), D)
            x_in = x
            x, h, bcu, cv, z, y = conv_fwd(x, v2, cw, w_in, w_out)
            rec["mix"] = (x_in, h, bcu, cv, z, y, cw)
        else:
            w_q, w_o = need(("attn_w_q", layer), x), need(("attn_w_o", layer), x)
            v2 = vec_of(layer, 1)
            x_in = x
            h, *q = proj_rope_fwd(x, v2, w_q, tabs, qw, True, dils, "q_fwd")
            os_, ls = [], []
            for g, (win, dil) in enumerate(groups):
                o, l = attn_core_fwd(q[g], k_sh[g], v_sh[g], g, win // dil)
                os_.append(o)
                ls.append(l)
            x, mixed, y = attn_mix_out(os_, ls, dils, x, v2, w_o)
            rec["mix"] = (x_in, h, q, os_, ls, mixed, y)
        token = advance(x) if layer >= 1 else None
        w_in, w_out = need(("ffn2_w_in", layer), x), need(("ffn2_w_out", layer), x)
        v3 = after_token(vec_of(layer, 2), token)
        x, rec["ffn2"] = _ffn_forward(x, v3, w_in, w_out)
        rec["vecs"] = (v1, v2, v3)
        saved.append(rec)

    dx, part_final, loss_tile = final_loss(x, _vec8([small["final_norm_g"]], D), target)
    loss = loss_tile[0, 0]

    conv_rows = [None] * N_A_LAYERS
    kv_rows = None
    mod_rows = [[None] * 3 for _ in range(DEPTH)]
    dkv_sums = [None for _ in groups]
    slots = {}
    exchanges = []
    token = None

    def send_ready_chunks():
        nonlocal token
        for ci in reversed(range(len(comm.chunks))):
            if ci not in [e[0] for e in exchanges] and all(key in slots for key in comm.chunks[ci]):
                started = comm.start_exchange(ci, slots, token)
                exchanges.append((ci, started))
                token = started["token"]

    vector_gather = {}

    def start_vector_gather(rows0):
        mod_rows[0][0] = rows0
        rows = jnp.stack([jnp.stack(r) for r in mod_rows])
        vecs = jnp.concatenate([rows[:, :, 1:4].reshape(-1), kv_rows[1:3].reshape(-1), kv_rows[0], part_final[0],
                                rows[:, :, 0].reshape(-1), jnp.stack(conv_rows).reshape(-1)])
        vector_gather["count"] = vecs.shape[0]
        vecs = _pad_rows(vecs.reshape(-1, 1), 8 * LANES).reshape(-1, LANES)
        vector_gather["started"] = comm_start([vecs], "gather", None, "vector_grads_start")
        return vector_gather["started"]["token"]

    for layer in reversed(range(DEPTH)):
        rec = saved[layer]
        v1, v2, v3 = rec["vecs"]
        dx, slots[("ffn2_w_in", layer)], slots[("ffn2_w_out", layer)], mod_rows[layer][2] = _ffn_backward(
            dx, rec["ffn2"], after_token(v3, token), W[("ffn2_w_in", layer)], W[("ffn2_w_out", layer)])
        if layer < N_A_LAYERS:
            x_in, h, bcu, cv, z, y, cw = rec["mix"]
            dx, dy, dbcu, part, dcw = conv_bwd(dx, x_in, y, bcu, cv, v2, cw, W[("conv_w_in", layer)], W[("conv_w_out", layer)])
            slots[("conv_w_out", layer)] = grad_slots(z, dy, "conv_dw_out")
            slots[("conv_w_in", layer)] = grad_slots(h, dbcu, "conv_dw_in", col_slots=True)
            conv_rows[layer] = dcw[0:3]
            mod_rows[layer][1] = part[0:4]
        else:
            x_in, h, q, os_, ls, mixed, y = rec["mix"]
            outs = attn_mix_bwd(dx, y, v2, W[("attn_w_o", layer)], os_, ls, dils)
            ng = len(groups)
            dy, dos, deltas, part_gate = outs[0], outs[1:1 + ng], outs[1 + ng:1 + 2 * ng], outs[1 + 2 * ng]
            slots[("attn_w_o", layer)] = grad_slots(mixed, dy, "attn_dw_o", col_slots=True)
            dqs = []
            for g, (win, dil) in enumerate(groups):
                dq, *dkv_sums[g] = attn_core_bwd(q[g], k_sh[g], v_sh[g], dos[g], deltas[g], ls[g], g, win // dil, dkv_sums[g])
                dqs.append(dq)
            dx, dqr, part_norm = proj_rope_bwd(dqs, dils, x_in, dx, v2, W[("attn_w_q", layer)], tabs, qw, True, "q_bwd")
            slots[("attn_w_q", layer)] = grad_slots(dqr, h, "attn_dw_q")
            mod_rows[layer][1] = jnp.concatenate([part_norm[0:3], part_gate[0:1]], axis=0)
        send_ready_chunks()
        dx, slots[("ffn1_w_in", layer)], slots[("ffn1_w_out", layer)], mod_rows[layer][0] = _ffn_backward(
            dx, rec["ffn1"], after_token(v1, token), W[("ffn1_w_in", layer)], W[("ffn1_w_out", layer)],
            on_rows=start_vector_gather if layer == 0 else None)
        if layer == N_A_LAYERS:
            x_kv, h_kv, kv_vec = kv_saved
            dparts = [dk for dk, _ in dkv_sums] + [dv for _, dv in dkv_sums]
            dx, dkvp, part_kv = proj_rope_bwd(dparts, dils, x_kv, dx, kv_vec, W[("w_kv", layer)], tabs, qw, False, "kv_bwd")
            slots[("w_kv", layer)] = grad_slots(h_kv, dkvp, "kv_dw", col_slots=True)
            kv_rows = part_kv[0:3]
        send_ready_chunks()

    return loss, dx, {"exchanges": exchanges, "vector_gather": vector_gather}


def _flat2(a):
    return a.reshape(-1, a.shape[-1])


def _pad_rows(a, mult):
    r = a.shape[0]
    pad = (-r) % mult
    return a if pad == 0 else jnp.concatenate([a, jnp.zeros((pad,) + a.shape[1:], a.dtype)], axis=0)


def kernel(x, c, positions, norm_g, ada_w, ada_b, ffn1_w_in, ffn1_w_out, ffn2_w_in, ffn2_w_out, conv_w_in, conv_w, conv_w_out, kv_norm_g, kv_ada_w, kv_ada_b, w_kv, attn_w_q, attn_w_o, final_norm_g, loss_target, m_norm_g, m_ada_w, m_ada_b, m_ffn1_w_in, m_ffn1_w_out, m_ffn2_w_in, m_ffn2_w_out, m_conv_w_in, m_conv_w, m_conv_w_out, m_kv_norm_g, m_kv_ada_w, m_kv_ada_b, m_w_kv, m_attn_w_q, m_attn_w_o, m_final_norm_g, v_norm_g, v_ada_w, v_ada_b, v_ffn1_w_in, v_ffn1_w_out, v_ffn2_w_in, v_ffn2_w_out, v_conv_w_in, v_conv_w, v_conv_w_out, v_kv_norm_g, v_kv_ada_w, v_kv_ada_b, v_w_kv, v_attn_w_q, v_attn_w_o, v_final_norm_g):
    names = ("norm_g", "ada_w", "ada_b", "ffn1_w_in", "ffn1_w_out", "ffn2_w_in", "ffn2_w_out", "conv_w_in", "conv_w",
             "conv_w_out", "kv_norm_g", "kv_ada_w", "kv_ada_b", "w_kv", "attn_w_q", "attn_w_o", "final_norm_g")
    wts = dict(zip(names, (norm_g, ada_w, ada_b, ffn1_w_in, ffn1_w_out, ffn2_w_in, ffn2_w_out, conv_w_in, conv_w, conv_w_out,
                           kv_norm_g, kv_ada_w, kv_ada_b, w_kv, attn_w_q, attn_w_o, final_norm_g)))
    mom = dict(zip(names, (m_norm_g, m_ada_w, m_ada_b, m_ffn1_w_in, m_ffn1_w_out, m_ffn2_w_in, m_ffn2_w_out, m_conv_w_in,
                           m_conv_w, m_conv_w_out, m_kv_norm_g, m_kv_ada_w, m_kv_ada_b, m_w_kv, m_attn_w_q, m_attn_w_o,
                           m_final_norm_g)))
    var = dict(zip(names, (v_norm_g, v_ada_w, v_ada_b, v_ffn1_w_in, v_ffn1_w_out, v_ffn2_w_in, v_ffn2_w_out, v_conv_w_in,
                           v_conv_w, v_conv_w_out, v_kv_norm_g, v_kv_ada_w, v_kv_ada_b, v_w_kv, v_attn_w_q, v_attn_w_o,
                           v_final_norm_g)))
    T, D = x.shape[1], x.shape[2]
    me = _my_id()
    nmod = ada_w.shape[2]
    nkv = kv_ada_w.shape[1]

    def stacked(w, n):
        w = w if w.ndim == 3 else w[None]
        return jnp.swapaxes(w, 1, 2) if n in _TRANSPOSED else w

    comm = ChunkComm({n: stacked(wts[n], n) for n in _BIG})
    W = {}

    ds = norm_g.shape[2]
    small = jnp.concatenate([c.reshape(-1), norm_g.reshape(-1), conv_w.reshape(-1)]).astype(F32)
    n_small = small.shape[0]
    small = _pad_rows(small.reshape(-1, 1), 8 * LANES).reshape(-1, LANES)
    (small_all,) = all_gather([small], pltpu.VMEM, "gather_small")
    small_all = small_all.reshape(N_DEV, -1)[:, :n_small]
    c_all = small_all[:, :D]
    def full_rows(off, count):
        return jnp.stack([small_all[:, off + i * ds:off + (i + 1) * ds].reshape(D) for i in range(count)])

    W["norm_g"] = full_rows(D, DEPTH * 3).reshape(DEPTH, 3, D)
    W["conv_w"] = full_rows(D + DEPTH * 3 * ds, N_A_LAYERS * 3).reshape(N_A_LAYERS, 3, D)
    W["kv_norm_g"], W["final_norm_g"] = kv_norm_g, final_norm_g

    ada_b_mine = lax.dynamic_slice_in_dim(ada_b, me * nmod, nmod, axis=1).reshape(DEPTH, 1, nmod)
    kv_b_mine = lax.dynamic_slice_in_dim(kv_ada_b, me * nkv, nkv, axis=0).reshape(1, 1, nkv)
    mods_cols = mods_project(c_all, ada_w, ada_b_mine)
    kv_cols = mods_project(c_all, kv_ada_w.reshape(1, D, nkv), kv_b_mine)
    mcat = jnp.concatenate([mods_cols[l] for l in range(DEPTH)] + [kv_cols[0]], axis=1)
    wm = mcat.shape[1]
    if wm % LANES:
        mcat = jnp.concatenate([mcat, jnp.zeros((N_DEV, LANES - wm % LANES), F32)], axis=1)
    (mods_all,) = exchange_slots([mcat.reshape(N_DEV, 1, -1)], "exchange_mods")
    gather0 = comm.start_gather(0, mods_all)
    mods_all = mods_all.reshape(N_DEV, -1)
    mods = jnp.stack([mods_all[:, l * nmod:(l + 1) * nmod].reshape(N_MOD, D) for l in range(DEPTH)])
    kvmods = mods_all[:, DEPTH * nmod:DEPTH * nmod + nkv].reshape(2, D)

    loss_local, dx, grads = device_step(x[0], positions[0], loss_target[0], mods, kvmods, W, comm, gather0)
    loss = lax.psum(loss_local, MESH_AXES)

    (vec_all,) = comm_wait(grads["vector_gather"]["started"], grads["exchanges"][-1][1]["token"], "vector_grads_wait")
    vec_all = vec_all.reshape(N_DEV, -1)[:, :grads["vector_gather"]["count"]]
    nm_, nk_ = DEPTH * N_MOD * D, 2 * D
    dmods_all = vec_all[:, :nm_].reshape(N_DEV, DEPTH, N_MOD * D)
    dkvm_all = vec_all[:, nm_:nm_ + nk_]
    rest = vec_all[:, nm_ + nk_:]
    parts_kv_norm, parts_final = rest[:, :D].reshape(N_DEV, 1, D), rest[:, D:2 * D].reshape(N_DEV, 1, D)
    parts_norm = lax.dynamic_slice_in_dim(rest[:, 2 * D:2 * D + DEPTH * 3 * D].reshape(N_DEV, DEPTH * 3, D), me * ds, ds, axis=2)
    parts_conv = lax.dynamic_slice_in_dim(rest[:, 2 * D + DEPTH * 3 * D:].reshape(N_DEV, N_A_LAYERS * 3, D), me * ds, ds, axis=2)
    dm_cols = lax.dynamic_slice_in_dim(dmods_all, me * nmod, nmod, axis=2)
    dm_mine = jnp.stack([dm_cols[:, l] for l in range(DEPTH)])
    dkv_mine = lax.dynamic_slice_in_dim(dkvm_all, me * nkv, nkv, axis=1).reshape(1, N_DEV, nkv)
    g_ada_w = mods_weight_grad(c_all, dm_mine)
    g_kv_ada_w = mods_weight_grad(c_all, dkv_mine)[0]

    out_g, out_d, out_m, out_v = {}, {}, {}, {}

    def update(n, g, w, parts=False):
        shp = w.shape
        w2 = w.reshape(1, -1) if w.ndim == 1 else _flat2(w)
        g2 = g if parts else g.reshape(w2.shape)
        res = adam_update(g2, w2, mom[n].reshape(w2.shape), var[n].reshape(w2.shape), parts, "adam_" + n)
        out_g[n], out_d[n], out_m[n], out_v[n] = (r.reshape(shp) for r in res)

    moms = {n: stacked(mom[n], n) for n in _BIG}
    vars_ = {n: stacked(var[n], n) for n in _BIG}
    results = {}
    after = dx
    for ci, started in grads["exchanges"]:
        for (n, layer), parts in comm.finish_exchange(ci, started, after).items():
            idx = stacked_index(n, layer)
            results[n] = adam_layer(parts, comm.shards[n], moms[n], vars_[n], results.get(n), 0 if idx is None else idx,
                                    after, f"adam_{n}_{layer}")
            after = results[n][1]
    for n in _BIG:
        res = [jnp.swapaxes(r, 1, 2) if n in _TRANSPOSED else r for r in results[n]]
        out_g[n], out_d[n], out_m[n], out_v[n] = (r.reshape(wts[n].shape) for r in res)
    update("ada_w", g_ada_w, ada_w)
    update("kv_ada_w", g_kv_ada_w, kv_ada_w)
    update("ada_b", dmods_all, ada_b, True)
    update("kv_ada_b", dkvm_all.reshape(N_DEV, 1, nk_), kv_ada_b, True)
    update("kv_norm_g", parts_kv_norm, kv_norm_g, True)
    update("final_norm_g", parts_final, final_norm_g, True)
    update("norm_g", parts_norm, norm_g, True)
    update("conv_w", parts_conv, conv_w, True)

    return (loss, dx.reshape(x.shape), *[out_g[n] for n in names], *[out_d[n] for n in names],
            *[out_m[n] for n in names], *[out_v[n] for n in names])
```
